```python
import math
import jax, jax.numpy as jnp
from jax import lax
import numpy as np

D_MODEL = 2048
BATCH = 16
SEQ = 2048
DEPTH = 1

CHUNK = 64
PLE_DIM = 256
EPS = 1e-6
GLA_HEADS = 4
GLA_DK = D_MODEL // (2 * GLA_HEADS)
GLA_DV = D_MODEL // GLA_HEADS
GLA_LOWRANK = 16
GLA_TAU = 16.0
DN_HEADS = 16
DN_DK = D_MODEL // DN_HEADS
DN_DV = D_MODEL // DN_HEADS
DN_CONV = 4
D_FF = 4 * D_MODEL

GLA_QK = GLA_HEADS * GLA_DK
GLA_V = GLA_HEADS * GLA_DV
DN_QK = DN_HEADS * DN_DK
DN_V = DN_HEADS * DN_DV
DN_QKV = 2 * DN_QK + DN_V
IN_SPLITS = (GLA_QK, GLA_QK, GLA_V, GLA_V, GLA_LOWRANK, DN_QKV, DN_V, DN_HEADS, DN_HEADS, D_MODEL, D_MODEL)
D_IN = 2 * GLA_QK + 2 * GLA_V + GLA_LOWRANK + DN_QKV + DN_V + 2 * DN_HEADS + 2 * D_MODEL

kernel_name = "hybrid_gla_gated_deltanet_block"


def rms_norm(x, g):
    xf = x.astype(jnp.float32)
    y = xf * lax.rsqrt(jnp.mean(xf * xf, axis=-1, keepdims=True) + EPS)
    return (y * g.astype(jnp.float32)).astype(x.dtype)


def head_rms_norm(o, g):
    return o * lax.rsqrt(jnp.mean(o * o, axis=-1, keepdims=True) + EPS) * g.astype(jnp.float32)


def l2_normalize(t):
    return t * lax.rsqrt(jnp.sum(t * t, axis=-1, keepdims=True) + EPS)


def split_cols(z, sizes):
    out, off = [], 0
    for s in sizes:
        out.append(z[..., off:off + s])
        off += s
    return out


def to_chunks(t, n_heads):
    b, s, _ = t.shape
    return t.astype(jnp.float32).reshape(b, s // CHUNK, CHUNK, n_heads, -1).transpose(0, 3, 1, 2, 4)


def heads_to_chunks(t):
    b, s, h = t.shape
    return t.astype(jnp.float32).reshape(b, s // CHUNK, CHUNK, h).transpose(0, 3, 1, 2)


def from_chunks(t):
    b, h, nc, c, d = t.shape
    return t.transpose(0, 2, 3, 1, 4).reshape(b, nc * c, h * d)


def causal_depthwise_conv(x, w):
    k, c = w.shape
    return lax.conv_general_dilated(x, w[:, None, :].astype(x.dtype), window_strides=(1,),
                                    padding=[(k - 1, 0)], dimension_numbers=('NWC', 'WIO', 'NWC'),
                                    feature_group_count=c)


def chunk_major(t):
    return jnp.moveaxis(t, 2, 0)


def gla_mixer(q, k, v, log_f):
    b, h, nc, c, dk = q.shape
    dv = v.shape[-1]
    bcum = jnp.cumsum(log_f, axis=3)
    q_in = q * jnp.exp(bcum)
    k_in = k * jnp.exp(-bcum)
    causal = jnp.tril(jnp.ones((c, c), dtype=bool))
    a = jnp.where(causal, jnp.einsum('bhncd,bhnsd->bhncs', q_in, k_in), 0.0)
    o_intra = jnp.einsum('bhncs,bhnsv->bhncv', a, v)
    b_last = bcum[:, :, :, -1, :]
    k_dec = k * jnp.exp(b_last[:, :, :, None, :] - bcum)

    def step(state, xs):
        q_c, k_c, v_c, f_last = xs
        o = jnp.einsum('bhcd,bhdv->bhcv', q_c, state)
        state = state * f_last[..., None] + jnp.einsum('bhcd,bhcv->bhdv', k_c, v_c)
        return state, o

    s0 = jnp.zeros((b, h, dk, dv), jnp.float32)
    _, o_inter = lax.scan(step, s0, (chunk_major(q_in), chunk_major(k_dec), chunk_major(v),
                                     chunk_major(jnp.exp(b_last))))
    return o_intra + jnp.moveaxis(o_inter, 0, 2)


def gated_delta_mixer(q, k, v, g, beta):
    b, h, nc, c, dk = q.shape
    dv = v.shape[-1]
    gcum = jnp.cumsum(g, axis=-1)
    incl = jnp.tril(jnp.ones((c, c), dtype=bool))
    strict = jnp.tril(jnp.ones((c, c), dtype=bool), -1)
    decay = jnp.exp(jnp.where(incl, gcum[..., :, None] - gcum[..., None, :], -jnp.inf))
    k_beta = k * beta[..., None]
    a = jnp.where(strict, jnp.einsum('bhncd,bhnsd->bhncs', k_beta, k) * decay, 0.0)
    eye = jnp.eye(c, dtype=jnp.float32)
    t_mat = lax.linalg.triangular_solve(eye + a, jnp.broadcast_to(eye, a.shape), left_side=True,
                                        lower=True, unit_diagonal=True)
    u = jnp.einsum('bhncs,bhnsv->bhncv', t_mat, v * beta[..., None])
    w = jnp.einsum('bhncs,bhnsd->bhncd', t_mat, k_beta * jnp.exp(gcum)[..., None])
    attn = jnp.where(incl, jnp.einsum('bhncd,bhnsd->bhncs', q, k) * decay, 0.0)
    q_dec = q * jnp.exp(gcum)[..., None]
    g_last = gcum[..., -1]
    k_dec = k * jnp.exp(g_last[..., None] - gcum)[..., None]

    def step(state, xs):
        q_c, k_c, u_c, w_c, attn_c, f_last = xs
        v_new = u_c - jnp.einsum('bhcd,bhdv->bhcv', w_c, state)
        o = jnp.einsum('bhcd,bhdv->bhcv', q_c, state) + jnp.einsum('bhcs,bhsv->bhcv', attn_c, v_new)
        state = state * f_last[..., None, None] + jnp.einsum('bhcd,bhcv->bhdv', k_c, v_new)
        return state, o

    s0 = jnp.zeros((b, h, dk, dv), jnp.float32)
    _, o = lax.scan(step, s0, (chunk_major(q_dec), chunk_major(k_dec), chunk_major(u), chunk_major(w),
                               chunk_major(attn), chunk_major(jnp.exp(g_last))))
    return jnp.moveaxis(o, 0, 2)


def hybrid_layer(x, p_i, g_mix, w_in, gla_w2, gla_b, gla_norm, dn_conv, dn_a_log, dn_dt_bias, dn_norm,
                 w_out, g_mlp, w_up, w_down, g_ple, w_ple_gate, w_ple_proj):
    f32 = jnp.float32
    h = rms_norm(x, g_mix)
    z = h @ w_in
    (gla_q, gla_k, gla_v, gla_g, gla_lr, dn_qkv, dn_z, dn_a, dn_b, gate_a, gate_b) = split_cols(z, IN_SPLITS)

    log_f = jax.nn.log_sigmoid((gla_lr @ gla_w2 + gla_b).astype(f32)) / GLA_TAU
    o_gla = gla_mixer(to_chunks(gla_q, GLA_HEADS) * (GLA_DK ** -0.5), to_chunks(gla_k, GLA_HEADS),
                      to_chunks(gla_v, GLA_HEADS), to_chunks(log_f, GLA_HEADS))
    o_gla = from_chunks(head_rms_norm(o_gla, gla_norm)) * jax.nn.silu(gla_g.astype(f32))

    qkv = jax.nn.silu(causal_depthwise_conv(dn_qkv, dn_conv))
    dq, dk, dv = split_cols(qkv, (DN_QK, DN_QK, DN_V))
    dq = l2_normalize(to_chunks(dq, DN_HEADS)) * (DN_DK ** -0.5)
    dk = l2_normalize(to_chunks(dk, DN_HEADS))
    a_neg = -jnp.exp(dn_a_log.astype(f32))[:, None, None]
    g = a_neg * jax.nn.softplus(heads_to_chunks(dn_a) + dn_dt_bias.astype(f32)[:, None, None])
    beta = jax.nn.sigmoid(heads_to_chunks(dn_b))
    o_dn = gated_delta_mixer(dq, dk, to_chunks(dv, DN_HEADS), g, beta)
    o_dn = from_chunks(head_rms_norm(o_dn, dn_norm)) * jax.nn.silu(dn_z.astype(f32))

    mixed = (jax.nn.sigmoid(gate_a.astype(f32)) * o_gla + jax.nn.sigmoid(gate_b.astype(f32)) * o_dn).astype(x.dtype)
    x = x + mixed @ w_out

    h2 = rms_norm(x, g_mlp)
    x = x + jnp.square(jax.nn.relu(h2 @ w_up)) @ w_down

    h3 = rms_norm(x, g_ple)
    x = x + jax.nn.sigmoid(h3 @ w_ple_gate) * (p_i @ w_ple_proj)
    return x


def _fwd_setup_inputs(seed: int = 0) -> dict:
    key = jax.random.key(seed)
    ks = jax.random.split(key, 24)
    f32 = jnp.float32

    def nrm(k, shape, scale):
        return jax.random.normal(k, shape, f32) * scale

    def gain(k, shape):
        return 1.0 + 0.02 * jax.random.normal(k, shape, f32)

    dt = jnp.exp(jax.random.uniform(ks[10], (DEPTH, DN_HEADS), f32, math.log(1e-3), math.log(1e-1)))
    return {
        "x": nrm(ks[0], (BATCH, SEQ, D_MODEL), 1.0),
        "p": nrm(ks[1], (DEPTH, BATCH, SEQ, PLE_DIM), 1.0),
        "g_mix": gain(ks[2], (DEPTH, D_MODEL)),
        "w_in": nrm(ks[3], (DEPTH, D_MODEL, D_IN), D_MODEL ** -0.5),
        "gla_w2": nrm(ks[4], (DEPTH, GLA_LOWRANK, GLA_QK), GLA_LOWRANK ** -0.5),
        "gla_b": nrm(ks[5], (DEPTH, GLA_QK), 0.1),
        "gla_norm": gain(ks[6], (DEPTH, GLA_DV)),
        "dn_conv": nrm(ks[7], (DEPTH, DN_CONV, DN_QKV), DN_CONV ** -0.5),
        "dn_a_log": jnp.log(jax.random.uniform(ks[8], (DEPTH, DN_HEADS), f32, 1.0, 16.0)),
        "dn_dt_bias": dt + jnp.log(-jnp.expm1(-dt)),
        "dn_norm": gain(ks[9], (DEPTH, DN_DV)),
        "w_out": nrm(ks[11], (DEPTH, D_MODEL, D_MODEL), D_MODEL ** -0.5),
        "g_mlp": gain(ks[12], (DEPTH, D_MODEL)),
        "w_up": nrm(ks[13], (DEPTH, D_MODEL, D_FF), D_MODEL ** -0.5),
        "w_down": nrm(ks[14], (DEPTH, D_FF, D_MODEL), D_FF ** -0.5),
        "g_ple": gain(ks[15], (DEPTH, D_MODEL)),
        "w_ple_gate": nrm(ks[16], (DEPTH, D_MODEL, D_MODEL), D_MODEL ** -0.5),
        "w_ple_proj": nrm(ks[17], (DEPTH, PLE_DIM, D_MODEL), PLE_DIM ** -0.5),
        "g_final": gain(ks[18], (D_MODEL,)),
    }


def _fwd_reference(x, p, g_mix, w_in, gla_w2, gla_b, gla_norm, dn_conv, dn_a_log, dn_dt_bias, dn_norm,
              w_out, g_mlp, w_up, w_down, g_ple, w_ple_gate, w_ple_proj, g_final):
    for i in range(DEPTH):
        x = hybrid_layer(x, p[i], g_mix[i], w_in[i], gla_w2[i], gla_b[i], gla_norm[i], dn_conv[i],
                         dn_a_log[i], dn_dt_bias[i], dn_norm[i], w_out[i], g_mlp[i], w_up[i], w_down[i],
                         g_ple[i], w_ple_gate[i], w_ple_proj[i])
    return rms_norm(x, g_final)


import jax as _jax
import jax.numpy as _jnp

TWIN_FORMAT = 'train_step'
FWD_PARAMS = ['x', 'p', 'g_mix', 'w_in', 'gla_w2', 'gla_b', 'gla_norm', 'dn_conv', 'dn_a_log', 'dn_dt_bias', 'dn_norm', 'w_out', 'g_mlp', 'w_up', 'w_down', 'g_ple', 'w_ple_gate', 'w_ple_proj', 'g_final']
TWIN_WEIGHTS = ['g_mix', 'w_in', 'gla_w2', 'gla_b', 'gla_norm', 'dn_conv', 'dn_a_log', 'dn_dt_bias', 'dn_norm', 'w_out', 'g_mlp', 'w_up', 'w_down', 'g_ple', 'w_ple_gate', 'w_ple_proj', 'g_final']
TWIN_DIFF_INPUT = 'x'
TWIN_INPUTS = ['x', 'p', 'g_mix', 'w_in', 'gla_w2', 'gla_b', 'gla_norm', 'dn_conv', 'dn_a_log', 'dn_dt_bias', 'dn_norm', 'w_out', 'g_mlp', 'w_up', 'w_down', 'g_ple', 'w_ple_gate', 'w_ple_proj', 'g_final', 'loss_target', 'm_g_mix', 'm_w_in', 'm_gla_w2', 'm_gla_b', 'm_gla_norm', 'm_dn_conv', 'm_dn_a_log', 'm_dn_dt_bias', 'm_dn_norm', 'm_w_out', 'm_g_mlp', 'm_w_up', 'm_w_down', 'm_g_ple', 'm_w_ple_gate', 'm_w_ple_proj', 'm_g_final', 'v_g_mix', 'v_w_in', 'v_gla_w2', 'v_gla_b', 'v_gla_norm', 'v_dn_conv', 'v_dn_a_log', 'v_dn_dt_bias', 'v_dn_norm', 'v_w_out', 'v_g_mlp', 'v_w_up', 'v_w_down', 'v_g_ple', 'v_w_ple_gate', 'v_w_ple_proj', 'v_g_final']
TWIN_OUTPUTS = ['loss', 'grad_x', 'grad_g_mix', 'grad_w_in', 'grad_gla_w2', 'grad_gla_b', 'grad_gla_norm', 'grad_dn_conv', 'grad_dn_a_log', 'grad_dn_dt_bias', 'grad_dn_norm', 'grad_w_out', 'grad_g_mlp', 'grad_w_up', 'grad_w_down', 'grad_g_ple', 'grad_w_ple_gate', 'grad_w_ple_proj', 'grad_g_final', 'delta_g_mix', 'delta_w_in', 'delta_gla_w2', 'delta_gla_b', 'delta_gla_norm', 'delta_dn_conv', 'delta_dn_a_log', 'delta_dn_dt_bias', 'delta_dn_norm', 'delta_w_out', 'delta_g_mlp', 'delta_w_up', 'delta_w_down', 'delta_g_ple', 'delta_w_ple_gate', 'delta_w_ple_proj', 'delta_g_final', 'new_m_g_mix', 'new_m_w_in', 'new_m_gla_w2', 'new_m_gla_b', 'new_m_gla_norm', 'new_m_dn_conv', 'new_m_dn_a_log', 'new_m_dn_dt_bias', 'new_m_dn_norm', 'new_m_w_out', 'new_m_g_mlp', 'new_m_w_up', 'new_m_w_down', 'new_m_g_ple', 'new_m_w_ple_gate', 'new_m_w_ple_proj', 'new_m_g_final', 'new_v_g_mix', 'new_v_w_in', 'new_v_gla_w2', 'new_v_gla_b', 'new_v_gla_norm', 'new_v_dn_conv', 'new_v_dn_a_log', 'new_v_dn_dt_bias', 'new_v_dn_norm', 'new_v_w_out', 'new_v_g_mlp', 'new_v_w_up', 'new_v_w_down', 'new_v_g_ple', 'new_v_w_ple_gate', 'new_v_w_ple_proj', 'new_v_g_final']
TWIN_LEAF_KINDS = {'loss': 'loss', 'grad_x': 'grad_x', 'grad_g_mix': 'grad_w', 'grad_w_in': 'grad_w', 'grad_gla_w2': 'grad_w', 'grad_gla_b': 'grad_w', 'grad_gla_norm': 'grad_w', 'grad_dn_conv': 'grad_w', 'grad_dn_a_log': 'grad_w', 'grad_dn_dt_bias': 'grad_w', 'grad_dn_norm': 'grad_w', 'grad_w_out': 'grad_w', 'grad_g_mlp': 'grad_w', 'grad_w_up': 'grad_w', 'grad_w_down': 'grad_w', 'grad_g_ple': 'grad_w', 'grad_w_ple_gate': 'grad_w', 'grad_w_ple_proj': 'grad_w', 'grad_g_final': 'grad_w', 'delta_g_mix': 'delta_w', 'delta_w_in': 'delta_w', 'delta_gla_w2': 'delta_w', 'delta_gla_b': 'delta_w', 'delta_gla_norm': 'delta_w', 'delta_dn_conv': 'delta_w', 'delta_dn_a_log': 'delta_w', 'delta_dn_dt_bias': 'delta_w', 'delta_dn_norm': 'delta_w', 'delta_w_out': 'delta_w', 'delta_g_mlp': 'delta_w', 'delta_w_up': 'delta_w', 'delta_w_down': 'delta_w', 'delta_g_ple': 'delta_w', 'delta_w_ple_gate': 'delta_w', 'delta_w_ple_proj': 'delta_w', 'delta_g_final': 'delta_w', 'new_m_g_mix': 'new_m', 'new_m_w_in': 'new_m', 'new_m_gla_w2': 'new_m', 'new_m_gla_b': 'new_m', 'new_m_gla_norm': 'new_m', 'new_m_dn_conv': 'new_m', 'new_m_dn_a_log': 'new_m', 'new_m_dn_dt_bias': 'new_m', 'new_m_dn_norm': 'new_m', 'new_m_w_out': 'new_m', 'new_m_g_mlp': 'new_m', 'new_m_w_up': 'new_m', 'new_m_w_down': 'new_m', 'new_m_g_ple': 'new_m', 'new_m_w_ple_gate': 'new_m', 'new_m_w_ple_proj': 'new_m', 'new_m_g_final': 'new_m', 'new_v_g_mix': 'new_v', 'new_v_w_in': 'new_v', 'new_v_gla_w2': 'new_v', 'new_v_gla_b': 'new_v', 'new_v_gla_norm': 'new_v', 'new_v_dn_conv': 'new_v', 'new_v_dn_a_log': 'new_v', 'new_v_dn_dt_bias': 'new_v', 'new_v_dn_norm': 'new_v', 'new_v_w_out': 'new_v', 'new_v_g_mlp': 'new_v', 'new_v_w_up': 'new_v', 'new_v_w_down': 'new_v', 'new_v_g_ple': 'new_v', 'new_v_w_ple_gate': 'new_v', 'new_v_w_ple_proj': 'new_v', 'new_v_g_final': 'new_v'}


def _forward(args):
    return _fwd_reference(*[args[k] for k in FWD_PARAMS])


def _output_shape():
    out = _jax.eval_shape(lambda: _forward(_fwd_setup_inputs(0)))
    return out.shape, out.dtype

N_MICROBATCH = 1
ADAM_LR = 0.001
ADAM_B1 = 0.9
ADAM_B2 = 0.999
ADAM_EPS = 1e-08
ADAM_WD = 0.01
ADAM_STEP = 10
PER_EXAMPLE_BATCH_AXIS = {'x': 0, 'p': 1, 'loss_target': 0}
SHARED_INPUTS = []
_WEIGHT_DTYPES = {'g_mix': _jnp.float32, 'w_in': _jnp.float32, 'gla_w2': _jnp.float32, 'gla_b': _jnp.float32, 'gla_norm': _jnp.float32, 'dn_conv': _jnp.float32, 'dn_a_log': _jnp.float32, 'dn_dt_bias': _jnp.float32, 'dn_norm': _jnp.float32, 'w_out': _jnp.float32, 'g_mlp': _jnp.float32, 'w_up': _jnp.float32, 'w_down': _jnp.float32, 'g_ple': _jnp.float32, 'w_ple_gate': _jnp.float32, 'w_ple_proj': _jnp.float32, 'g_final': _jnp.float32}
MOMENT_SCALE = {'g_mix': 7.663167e-02, 'w_in': 2.437895e-02, 'gla_w2': 4.540180e-03, 'gla_b': 2.047048e-02, 'gla_norm': 5.419908e-02, 'dn_conv': 2.108155e-02, 'dn_a_log': 1.751817e-01, 'dn_dt_bias': 1.619012e-01, 'dn_norm': 1.284447e-01, 'w_out': 3.827484e-02, 'g_mlp': 7.625290e-02, 'w_up': 3.659236e-02, 'w_down': 7.054048e-02, 'g_ple': 1.152271e-02, 'w_ple_gate': 1.134254e-02, 'w_ple_proj': 2.780941e-02, 'g_final': 1.611837e+01}


def _to_microbatches(a, axis):
    t = _jnp.moveaxis(a, axis, 0)
    t = t.reshape((N_MICROBATCH, t.shape[0] // N_MICROBATCH) + t.shape[1:])
    return _jnp.moveaxis(t, 1, axis + 1)


def setup_inputs(seed: int = 0) -> dict:
    inp = _fwd_setup_inputs(seed)
    key = _jax.random.fold_in(_jax.random.key(seed), 7919)
    shape, _ = _output_shape()
    out = dict(inp)
    out["loss_target"] = _jax.random.normal(_jax.random.fold_in(key, 0), shape, _jnp.float32)
    for i, name in enumerate(TWIN_WEIGHTS):
        w = inp[name].astype(_jnp.float32)
        if MOMENT_SCALE is None:
            s = _jnp.sqrt(_jnp.mean(_jnp.square(w)) + 1e-30)
        else:
            s = MOMENT_SCALE[name]
        km, kv = _jax.random.split(_jax.random.fold_in(key, i + 1))
        out[name] = w
        out["m_" + name] = s * _jax.random.normal(km, w.shape, _jnp.float32)
        out["v_" + name] = (s * s) * _jax.random.uniform(kv, w.shape, _jnp.float32, 0.5, 1.5)
    if N_MICROBATCH > 1:
        for name, axis in PER_EXAMPLE_BATCH_AXIS.items():
            out[name] = _to_microbatches(out[name], axis)
    return {'x': out['x'], 'p': out['p'], 'g_mix': out['g_mix'], 'w_in': out['w_in'], 'gla_w2': out['gla_w2'], 'gla_b': out['gla_b'], 'gla_norm': out['gla_norm'], 'dn_conv': out['dn_conv'], 'dn_a_log': out['dn_a_log'], 'dn_dt_bias': out['dn_dt_bias'], 'dn_norm': out['dn_norm'], 'w_out': out['w_out'], 'g_mlp': out['g_mlp'], 'w_up': out['w_up'], 'w_down': out['w_down'], 'g_ple': out['g_ple'], 'w_ple_gate': out['w_ple_gate'], 'w_ple_proj': out['w_ple_proj'], 'g_final': out['g_final'], 'loss_target': out['loss_target'], 'm_g_mix': out['m_g_mix'], 'm_w_in': out['m_w_in'], 'm_gla_w2': out['m_gla_w2'], 'm_gla_b': out['m_gla_b'], 'm_gla_norm': out['m_gla_norm'], 'm_dn_conv': out['m_dn_conv'], 'm_dn_a_log': out['m_dn_a_log'], 'm_dn_dt_bias': out['m_dn_dt_bias'], 'm_dn_norm': out['m_dn_norm'], 'm_w_out': out['m_w_out'], 'm_g_mlp': out['m_g_mlp'], 'm_w_up': out['m_w_up'], 'm_w_down': out['m_w_down'], 'm_g_ple': out['m_g_ple'], 'm_w_ple_gate': out['m_w_ple_gate'], 'm_w_ple_proj': out['m_w_ple_proj'], 'm_g_final': out['m_g_final'], 'v_g_mix': out['v_g_mix'], 'v_w_in': out['v_w_in'], 'v_gla_w2': out['v_gla_w2'], 'v_gla_b': out['v_gla_b'], 'v_gla_norm': out['v_gla_norm'], 'v_dn_conv': out['v_dn_conv'], 'v_dn_a_log': out['v_dn_a_log'], 'v_dn_dt_bias': out['v_dn_dt_bias'], 'v_dn_norm': out['v_dn_norm'], 'v_w_out': out['v_w_out'], 'v_g_mlp': out['v_g_mlp'], 'v_w_up': out['v_w_up'], 'v_w_down': out['v_w_down'], 'v_g_ple': out['v_g_ple'], 'v_w_ple_gate': out['v_w_ple_gate'], 'v_w_ple_proj': out['v_w_ple_proj'], 'v_g_final': out['v_g_final']}


def _loss(weights, diff, rest, loss_target):
    with _jax.named_scope("forward"):
        args = {**rest, TWIN_DIFF_INPUT: diff, **{k: w.astype(_WEIGHT_DTYPES[k]) for k, w in weights.items()}}
        y = _forward(args)
    with _jax.named_scope("loss_head"):
        err = _jnp.square(y.astype(_jnp.float32) - loss_target)
        return 0.5 * _jnp.sum(_jnp.mean(err, axis=-1)) if err.ndim else 0.5 * err


def _adamw(w, g, m, v):
    m = ADAM_B1 * m + (1.0 - ADAM_B1) * g
    v = ADAM_B2 * v + (1.0 - ADAM_B2) * _jnp.square(g)
    m_hat = m / (1.0 - ADAM_B1 ** ADAM_STEP)
    v_hat = v / (1.0 - ADAM_B2 ** ADAM_STEP)
    delta = -ADAM_LR * (m_hat / (_jnp.sqrt(v_hat) + ADAM_EPS) + ADAM_WD * w)
    return delta, m, v


def reference(x, p, g_mix, w_in, gla_w2, gla_b, gla_norm, dn_conv, dn_a_log, dn_dt_bias, dn_norm, w_out, g_mlp, w_up, w_down, g_ple, w_ple_gate, w_ple_proj, g_final, loss_target, m_g_mix, m_w_in, m_gla_w2, m_gla_b, m_gla_norm, m_dn_conv, m_dn_a_log, m_dn_dt_bias, m_dn_norm, m_w_out, m_g_mlp, m_w_up, m_w_down, m_g_ple, m_w_ple_gate, m_w_ple_proj, m_g_final, v_g_mix, v_w_in, v_gla_w2, v_gla_b, v_gla_norm, v_dn_conv, v_dn_a_log, v_dn_dt_bias, v_dn_norm, v_w_out, v_g_mlp, v_w_up, v_w_down, v_g_ple, v_w_ple_gate, v_w_ple_proj, v_g_final):
    given = dict(x=x, p=p, g_mix=g_mix, w_in=w_in, gla_w2=gla_w2, gla_b=gla_b, gla_norm=gla_norm, dn_conv=dn_conv, dn_a_log=dn_a_log, dn_dt_bias=dn_dt_bias, dn_norm=dn_norm, w_out=w_out, g_mlp=g_mlp, w_up=w_up, w_down=w_down, g_ple=g_ple, w_ple_gate=w_ple_gate, w_ple_proj=w_ple_proj, g_final=g_final, loss_target=loss_target, m_g_mix=m_g_mix, m_w_in=m_w_in, m_gla_w2=m_gla_w2, m_gla_b=m_gla_b, m_gla_norm=m_gla_norm, m_dn_conv=m_dn_conv, m_dn_a_log=m_dn_a_log, m_dn_dt_bias=m_dn_dt_bias, m_dn_norm=m_dn_norm, m_w_out=m_w_out, m_g_mlp=m_g_mlp, m_w_up=m_w_up, m_w_down=m_w_down, m_g_ple=m_g_ple, m_w_ple_gate=m_w_ple_gate, m_w_ple_proj=m_w_ple_proj, m_g_final=m_g_final, v_g_mix=v_g_mix, v_w_in=v_w_in, v_gla_w2=v_gla_w2, v_gla_b=v_gla_b, v_gla_norm=v_gla_norm, v_dn_conv=v_dn_conv, v_dn_a_log=v_dn_a_log, v_dn_dt_bias=v_dn_dt_bias, v_dn_norm=v_dn_norm, v_w_out=v_w_out, v_g_mlp=v_g_mlp, v_w_up=v_w_up, v_w_down=v_w_down, v_g_ple=v_g_ple, v_w_ple_gate=v_w_ple_gate, v_w_ple_proj=v_w_ple_proj, v_g_final=v_g_final)
    weights = {n: given[n] for n in TWIN_WEIGHTS}
    shared = {n: given[n] for n in SHARED_INPUTS}
    per_example = {n: given[n] for n in ['x', 'p']}
    grad_fn = _jax.value_and_grad(_loss, argnums=(0, 1))

    def one_microbatch(ex, loss_target):
        ex = dict(ex)
        diff = ex.pop(TWIN_DIFF_INPUT)
        return grad_fn(weights, diff, {**shared, **ex}, loss_target)

    if N_MICROBATCH == 1:
        loss, (grad_w, grad_x) = one_microbatch(per_example, given["loss_target"])
    else:
        def body(carry, xs):
            loss_sum, grad_sum = carry
            l_k, (gw_k, gx_k) = one_microbatch(xs[0], xs[1])
            with _jax.named_scope("update"):
                return (loss_sum + l_k, _jax.tree.map(_jnp.add, grad_sum, gw_k)), gx_k

        init = (_jnp.zeros((), _jnp.float32), _jax.tree.map(_jnp.zeros_like, weights))
        (loss, grad_w), grad_x = _jax.lax.scan(body, init, (per_example, given["loss_target"]))
    with _jax.named_scope("update"):
        delta_w, new_m, new_v = {}, {}, {}
        for n in TWIN_WEIGHTS:
            delta_w[n], new_m[n], new_v[n] = _adamw(weights[n], grad_w[n], given["m_" + n], given["v_" + n])
    return (loss, grad_x, *[grad_w[n] for n in TWIN_WEIGHTS], *[delta_w[n] for n in TWIN_WEIGHTS],
            *[new_m[n] for n in TWIN_WEIGHTS], *[new_v[n] for n in TWIN_WEIGHTS])
```

```python
import functools
import math

import jax
import jax.numpy as jnp
from jax import lax
from jax.experimental import pallas as pl
from jax.experimental.pallas import tpu as pltpu

F32 = jnp.float32
BF16 = jnp.bfloat16
HIGHEST = lax.Precision.HIGHEST

N_DEV = 8
D_MODEL = 2048
CHUNK = 64
PLE_DIM = 256
EPS = 1e-6
GLA_HEADS = 4
GLA_DK = 256
GLA_DV = 512
GLA_LOWRANK = 16
GLA_TAU = 16.0
DN_HEADS = 16
DN_D = 128
DN_CONV = 4
D_FF = 4 * D_MODEL
GLA_QK = GLA_HEADS * GLA_DK
GLA_V = GLA_HEADS * GLA_DV
DN_QKV = 3 * DN_HEADS * DN_D
D_IN = 2 * GLA_QK + 2 * GLA_V + GLA_LOWRANK + DN_QKV + D_MODEL + 2 * DN_HEADS + 2 * D_MODEL
D_IN_SHARD = D_IN // N_DEV

ADAM_LR = 0.001
ADAM_B1 = 0.9
ADAM_B2 = 0.999
ADAM_EPS = 1e-08
ADAM_WD = 0.01
ADAM_STEP = 10

LANE = 128
ZB_GQ, ZB_GK, ZB_GV, ZB_GG = 0, 1024, 2048, 4096
ZB_DQKV, ZB_DZ, ZB_GA, ZB_GB = 6144, 12288, 14336, 16384
ZB_W = 18432
ZS_LR, ZS_DA, ZS_DB = 0, 128, 256
ZS_W = 384
WI_LR = 2 * GLA_QK + 2 * GLA_V
WI_DQKV = WI_LR + GLA_LOWRANK
WI_DA = WI_DQKV + DN_QKV + D_MODEL
WI_DB = WI_DA + DN_HEADS
WI_GA = WI_DB + DN_HEADS

VMEM_LIMIT = 56 * 1024 * 1024

NN = (((1,), (0,)), ((), ()))
NT = (((1,), (1,)), ((), ()))
TN = (((0,), (0,)), ((), ()))


def _bdot(a, b, dims):
    return lax.dot_general(a.astype(BF16), b.astype(BF16), dims, preferred_element_type=F32)


def _fdot(a, b, dims):
    return lax.dot_general(a, b, dims, precision=HIGHEST, preferred_element_type=F32)


def _sigmoid(x):
    return 1.0 / (1.0 + jnp.exp(-x))


def _silu(x):
    return x * _sigmoid(x)


def _softplus(x):
    return jnp.maximum(x, 0.0) + jnp.log(1.0 + jnp.exp(-jnp.abs(x)))


def _iota2(shape, dim):
    return lax.broadcasted_iota(jnp.int32, shape, dim)


def _cparams(sem=None):
    return pltpu.CompilerParams(dimension_semantics=sem, vmem_limit_bytes=VMEM_LIMIT)


def _gla_chunk(st, q, k, v, lr, w2, b, gg, ga, gn):
    c = q.shape[0]
    row, col = _iota2((c, c), 0), _iota2((c, c), 1)
    incl = row >= col
    tri = incl.astype(F32)
    q = q.astype(F32) * (GLA_DK ** -0.5)
    k = k.astype(F32)
    v = v.astype(F32)
    lf = -_softplus(-(_bdot(lr, w2, NN) + b)) / GLA_TAU
    bcum = _fdot(tri, lf, NN)
    blast = jnp.sum(lf, axis=0, keepdims=True)
    q_in = q * jnp.exp(bcum)
    k_in = k * jnp.exp(-bcum)
    a = jnp.where(incl, _bdot(q_in, k_in, NT), 0.0)
    o = _bdot(a, v, NN) + _bdot(q_in, st, NT)
    k_dec = k * jnp.exp(blast - bcum)
    st_new = st * jnp.exp(blast) + _bdot(v, k_dec, TN)
    on = o * lax.rsqrt(jnp.mean(o * o, axis=-1, keepdims=True) + EPS) * gn
    res = _sigmoid(ga.astype(F32)) * on * _silu(gg.astype(F32))
    return res, st_new


def _tri_inv_raw(a):
    c = a.shape[0]
    eye = (_iota2((c, c), 0) == _iota2((c, c), 1)).astype(F32)
    x = a
    p = eye - a
    for _ in range(5):
        x = _fdot(x, x, NN)
        p = _fdot(p, eye + x, NN)
    return p


@jax.custom_vjp
def _tri_inv(a):
    return _tri_inv_raw(a)


def _tri_inv_fwd(a):
    t = _tri_inv_raw(a)
    return t, t


def _tri_inv_bwd(t, dt):
    return (-_fdot(_fdot(t, dt, TN), t, NT),)


_tri_inv.defvjp(_tri_inv_fwd, _tri_inv_bwd)


def _dn_chunk(s, qr, kr, vr, za, zb, alog, dtb, gz, gb, dn, h, differentiable):
    c = qr.shape[0]
    row, col = _iota2((c, c), 0), _iota2((c, c), 1)
    incl = row >= col
    strict = row > col
    tri = incl.astype(F32)
    eye = (row == col).astype(F32)
    ones_cc = jnp.ones((c, c), F32)
    sel = (_iota2((LANE, LANE), 0) == h).astype(F32)
    lane0 = (_iota2((LANE, c), 0) == 0).astype(F32)

    def l2n(t):
        return t * lax.rsqrt(jnp.sum(t * t, axis=-1, keepdims=True) + EPS)

    q = l2n(qr.astype(F32)) * (DN_D ** -0.5)
    k = l2n(kr.astype(F32))
    v = vr.astype(F32)
    g = -jnp.exp(alog) * _softplus(_fdot(za, sel, NN) + dtb)
    beta = _sigmoid(_fdot(zb, sel, NN))
    gcum = _fdot(tri, g, NN)
    glast = jnp.sum(g, axis=0, keepdims=True)
    cm = _fdot(gcum, lane0, NN)
    rm = _fdot(ones_cc, cm * eye, NN)
    dec = jnp.exp(jnp.where(incl, cm - rm, -1e30))
    kb = k * beta
    a = jnp.where(strict, _bdot(kb, k, NT) * dec, 0.0)
    t = _tri_inv(a) if differentiable else _tri_inv_raw(a)
    egc = jnp.exp(gcum)
    u = _bdot(t, v * beta, NN)
    w = _bdot(t, kb * egc, NN)
    attn = jnp.where(incl, _bdot(q, k, NT) * dec, 0.0)
    q_dec = q * egc
    k_dec = k * jnp.exp(glast - gcum)
    v_new = u - _bdot(w, s, NN)
    o = _bdot(q_dec, s, NN) + _bdot(attn, v_new, NN)
    s_new = s * jnp.exp(glast) + _bdot(k_dec, v_new, TN)
    on = o * lax.rsqrt(jnp.mean(o * o, axis=-1, keepdims=True) + EPS) * dn
    res = _sigmoid(gb.astype(F32)) * on * _silu(gz.astype(F32))
    return res, s_new


def _gla_fwd(zb, zs, w2p, gla_b, gla_norm, bsz, nc):
    t = zb.shape[0]

    def body(q_ref, k_ref, v_ref, gg_ref, ga_ref, lr_ref, w2_ref, b_ref, gn_ref, o_ref, st_ref, state):
        n, h = pl.program_id(1), pl.program_id(2)

        @pl.when(n == 0)
        def _():
            state[h] = jnp.zeros((GLA_DV, GLA_DK), F32)

        st = state[h]
        st_ref[0, 0, 0] = st
        res, st_new = _gla_chunk(st, q_ref[...], k_ref[...], v_ref[...], lr_ref[...], w2_ref[...], b_ref[...],
                                 gg_ref[...], ga_ref[...], gn_ref[...])
        o_ref[...] = res
        state[h] = st_new

    def rows(b, n, h):
        return b * nc + n

    qk = lambda base: pl.BlockSpec((CHUNK, GLA_DK), lambda b, n, h: (rows(b, n, h), base // GLA_DK + h))
    vv = lambda base: pl.BlockSpec((CHUNK, GLA_DV), lambda b, n, h: (rows(b, n, h), base // GLA_DV + h))
    return pl.pallas_call(
        body, name="gla_fwd", grid=(bsz, nc, GLA_HEADS),
        in_specs=[qk(ZB_GQ), qk(ZB_GK), vv(ZB_GV), vv(ZB_GG), vv(ZB_GA),
                  pl.BlockSpec((CHUNK, LANE), lambda b, n, h: (rows(b, n, h), ZS_LR // LANE)),
                  pl.BlockSpec((LANE, GLA_DK), lambda b, n, h: (0, h)),
                  pl.BlockSpec((1, GLA_DK), lambda b, n, h: (0, h)),
                  pl.BlockSpec((1, GLA_DV), lambda b, n, h: (0, 0))],
        out_specs=[pl.BlockSpec((CHUNK, GLA_DV), lambda b, n, h: (rows(b, n, h), h)),
                   pl.BlockSpec((1, 1, 1, GLA_DV, GLA_DK), lambda b, n, h: (b, n, h, 0, 0))],
        out_shape=[jax.ShapeDtypeStruct((t, GLA_V), F32),
                   jax.ShapeDtypeStruct((bsz, nc, GLA_HEADS, GLA_DV, GLA_DK), F32)],
        scratch_shapes=[pltpu.VMEM((GLA_HEADS, GLA_DV, GLA_DK), F32)],
        compiler_params=_cparams(("arbitrary", "arbitrary", "arbitrary")),
    )(zb, zb, zb, zb, zb, zs, w2p, gla_b, gla_norm)


def _gla_bwd(zb, zs, w2p, gla_b, gla_norm, states, dmixed, bsz, nc):
    t = zb.shape[0]

    def body(q_ref, k_ref, v_ref, gg_ref, ga_ref, lr_ref, w2_ref, b_ref, gn_ref, st_ref, dm_ref,
             dq_ref, dk_ref, dv_ref, dgg_ref, dga_ref, dlr_ref, dw2_ref, db_ref, dgn_ref, dstate):
        b, n, h = pl.program_id(0), pl.program_id(1), pl.program_id(2)

        @pl.when(n == 0)
        def _():
            dstate[h] = jnp.zeros((GLA_DV, GLA_DK), F32)

        @pl.when((b == 0) & (n == 0) & (h == 0))
        def _():
            dw2_ref[...] = jnp.zeros_like(dw2_ref)
            db_ref[...] = jnp.zeros_like(db_ref)
            dgn_ref[...] = jnp.zeros_like(dgn_ref)

        args = (st_ref[0, 0, 0], q_ref[...], k_ref[...], v_ref[...], lr_ref[...], w2_ref[...], b_ref[...],
                gg_ref[...], ga_ref[...], gn_ref[...])
        _, vjp = jax.vjp(_gla_chunk, *args)
        dst, dq, dk, dv, dlr, dw2, db, dgg, dga, dgn = vjp((dm_ref[...].astype(F32), dstate[h]))
        dstate[h] = dst
        dq_ref[...] = dq.astype(dq_ref.dtype)
        dk_ref[...] = dk.astype(dk_ref.dtype)
        dv_ref[...] = dv.astype(dv_ref.dtype)
        dgg_ref[...] = dgg.astype(dgg_ref.dtype)
        dga_ref[...] = dga.astype(dga_ref.dtype)

        @pl.when(h == 0)
        def _():
            dlr_ref[...] = dlr

        @pl.when(h > 0)
        def _():
            dlr_ref[...] += dlr

        dw2_ref[h] += dw2
        db_ref[h] += db
        dgn_ref[...] += dgn

    def rows(b, n, h):
        return b * nc + (nc - 1 - n)

    qk = lambda base: pl.BlockSpec((CHUNK, GLA_DK), lambda b, n, h: (rows(b, n, h), base // GLA_DK + h))
    vv = lambda base: pl.BlockSpec((CHUNK, GLA_DV), lambda b, n, h: (rows(b, n, h), base // GLA_DV + h))
    full = lambda shape: pl.BlockSpec(shape, lambda b, n, h: (0,) * len(shape))
    return pl.pallas_call(
        body, name="gla_bwd", grid=(bsz, nc, GLA_HEADS),
        in_specs=[qk(ZB_GQ), qk(ZB_GK), vv(ZB_GV), vv(ZB_GG), vv(ZB_GA),
                  pl.BlockSpec((CHUNK, LANE), lambda b, n, h: (rows(b, n, h), ZS_LR // LANE)),
                  pl.BlockSpec((LANE, GLA_DK), lambda b, n, h: (0, h)),
                  pl.BlockSpec((1, GLA_DK), lambda b, n, h: (0, h)),
                  pl.BlockSpec((1, GLA_DV), lambda b, n, h: (0, 0)),
                  pl.BlockSpec((1, 1, 1, GLA_DV, GLA_DK), lambda b, n, h: (b, nc - 1 - n, h, 0, 0)),
                  vv(0)],
        out_specs=[qk(0), qk(0), vv(0), vv(0), vv(0),
                   pl.BlockSpec((CHUNK, LANE), lambda b, n, h: (rows(b, n, h), 0)),
                   full((GLA_HEADS, LANE, GLA_DK)), full((GLA_HEADS, 1, GLA_DK)), full((1, GLA_DV))],
        out_shape=[jax.ShapeDtypeStruct((t, GLA_QK), BF16), jax.ShapeDtypeStruct((t, GLA_QK), BF16),
                   jax.ShapeDtypeStruct((t, GLA_V), BF16), jax.ShapeDtypeStruct((t, GLA_V), BF16),
                   jax.ShapeDtypeStruct((t, GLA_V), BF16), jax.ShapeDtypeStruct((t, LANE), F32),
                   jax.ShapeDtypeStruct((GLA_HEADS, LANE, GLA_DK), F32),
                   jax.ShapeDtypeStruct((GLA_HEADS, 1, GLA_DK), F32),
                   jax.ShapeDtypeStruct((1, GLA_DV), F32)],
        scratch_shapes=[pltpu.VMEM((GLA_HEADS, GLA_DV, GLA_DK), F32)],
        compiler_params=_cparams(("arbitrary", "arbitrary", "arbitrary")),
    )(zb, zb, zb, zb, zb, zs, w2p, gla_b, gla_norm, states, dmixed)


def _dn_specs(nc, reverse):
    def rows(b, n, h):
        return b * nc + ((nc - 1 - n) if reverse else n)

    def col(base):
        return pl.BlockSpec((CHUNK, DN_D), lambda b, n, h: (rows(b, n, h), base // DN_D + h))

    def fixed(c):
        return pl.BlockSpec((CHUNK, LANE), lambda b, n, h: (rows(b, n, h), c))

    head = pl.BlockSpec((None, 1, LANE), lambda b, n, h: (h, 0, 0))
    return rows, col, fixed, head


def _dn_fwd(act, zb, zs, alog_b, dtb_b, dn_norm, mix_gla, bsz, nc):
    t = zb.shape[0]
    rows, col, fixed, head = _dn_specs(nc, False)

    def body(q_ref, k_ref, v_ref, za_ref, zb_ref, al_ref, dt_ref, gz_ref, gb_ref, dn_ref, mg_ref,
             o_ref, st_ref, state):
        n, h = pl.program_id(1), pl.program_id(2)

        @pl.when(n == 0)
        def _():
            state[h] = jnp.zeros((DN_D, DN_D), F32)

        s = state[h]
        st_ref[0, 0, 0] = s
        res, s_new = _dn_chunk(s, q_ref[...], k_ref[...], v_ref[...], za_ref[...], zb_ref[...], al_ref[...],
                               dt_ref[...], gz_ref[...], gb_ref[...], dn_ref[...], h, False)
        o_ref[...] = (res + mg_ref[...]).astype(o_ref.dtype)
        state[h] = s_new

    return pl.pallas_call(
        body, name="dn_fwd", grid=(bsz, nc, DN_HEADS),
        in_specs=[col(0), col(DN_HEADS * DN_D), col(2 * DN_HEADS * DN_D),
                  fixed(ZS_DA // LANE), fixed(ZS_DB // LANE), head, head,
                  col(ZB_DZ), col(ZB_GB), pl.BlockSpec((1, DN_D), lambda b, n, h: (0, 0)), col(0)],
        out_specs=[col(0), pl.BlockSpec((1, 1, 1, DN_D, DN_D), lambda b, n, h: (b, n, h, 0, 0))],
        out_shape=[jax.ShapeDtypeStruct((t, D_MODEL), BF16),
                   jax.ShapeDtypeStruct((bsz, nc, DN_HEADS, DN_D, DN_D), F32)],
        scratch_shapes=[pltpu.VMEM((DN_HEADS, DN_D, DN_D), F32)],
        compiler_params=_cparams(("arbitrary", "arbitrary", "arbitrary")),
    )(act, act, act, zs, zs, alog_b, dtb_b, zb, zb, dn_norm, mix_gla)


def _dn_bwd(act, zb, zs, alog_b, dtb_b, dn_norm, states, dmixed, bsz, nc):
    t = zb.shape[0]
    rows, col, fixed, head = _dn_specs(nc, True)

    def body(q_ref, k_ref, v_ref, za_ref, zb_ref, al_ref, dt_ref, gz_ref, gb_ref, dn_ref, st_ref, dm_ref,
             dq_ref, dk_ref, dv_ref, dza_ref, dzb_ref, dgz_ref, dgb_ref, dal_ref, ddt_ref, ddn_ref, dstate):
        b, n, h = pl.program_id(0), pl.program_id(1), pl.program_id(2)

        @pl.when(n == 0)
        def _():
            dstate[h] = jnp.zeros((DN_D, DN_D), F32)

        @pl.when((b == 0) & (n == 0) & (h == 0))
        def _():
            dal_ref[...] = jnp.zeros_like(dal_ref)
            ddt_ref[...] = jnp.zeros_like(ddt_ref)
            ddn_ref[...] = jnp.zeros_like(ddn_ref)

        fn = functools.partial(_dn_chunk, h=h, differentiable=True)
        _, vjp = jax.vjp(fn, st_ref[0, 0, 0], q_ref[...], k_ref[...], v_ref[...], za_ref[...], zb_ref[...],
                         al_ref[...], dt_ref[...], gz_ref[...], gb_ref[...], dn_ref[...])
        ds, dq, dk, dv, dza, dzb, dal, ddt, dgz, dgb, ddn = vjp((dm_ref[...].astype(F32), dstate[h]))
        dstate[h] = ds
        dq_ref[...] = dq.astype(dq_ref.dtype)
        dk_ref[...] = dk.astype(dk_ref.dtype)
        dv_ref[...] = dv.astype(dv_ref.dtype)
        dgz_ref[...] = dgz.astype(dgz_ref.dtype)
        dgb_ref[...] = dgb.astype(dgb_ref.dtype)

        @pl.when(h == 0)
        def _():
            dza_ref[...] = dza
            dzb_ref[...] = dzb

        @pl.when(h > 0)
        def _():
            dza_ref[...] += dza
            dzb_ref[...] += dzb

        dal_ref[h] += dal
        ddt_ref[h] += ddt
        ddn_ref[...] += ddn

    full = lambda shape: pl.BlockSpec(shape, lambda b, n, h: (0,) * len(shape))
    return pl.pallas_call(
        body, name="dn_bwd", grid=(bsz, nc, DN_HEADS),
        in_specs=[col(0), col(DN_HEADS * DN_D), col(2 * DN_HEADS * DN_D),
                  fixed(ZS_DA // LANE), fixed(ZS_DB // LANE), head, head,
                  col(ZB_DZ), col(ZB_GB), pl.BlockSpec((1, DN_D), lambda b, n, h: (0, 0)),
                  pl.BlockSpec((1, 1, 1, DN_D, DN_D), lambda b, n, h: (b, nc - 1 - n, h, 0, 0)), col(0)],
        out_specs=[col(0), col(0), col(0), fixed(0), fixed(0), col(0), col(0),
                   full((DN_HEADS, 1, LANE)), full((DN_HEADS, 1, LANE)), full((1, DN_D))],
        out_shape=[jax.ShapeDtypeStruct((t, D_MODEL), F32), jax.ShapeDtypeStruct((t, D_MODEL), F32),
                   jax.ShapeDtypeStruct((t, D_MODEL), F32),
                   jax.ShapeDtypeStruct((t, LANE), F32), jax.ShapeDtypeStruct((t, LANE), F32),
                   jax.ShapeDtypeStruct((t, D_MODEL), BF16), jax.ShapeDtypeStruct((t, D_MODEL), BF16),
                   jax.ShapeDtypeStruct((DN_HEADS, 1, LANE), F32), jax.ShapeDtypeStruct((DN_HEADS, 1, LANE), F32),
                   jax.ShapeDtypeStruct((1, DN_D), F32)],
        scratch_shapes=[pltpu.VMEM((DN_HEADS, DN_D, DN_D), F32)],
        compiler_params=_cparams(("arbitrary", "arbitrary", "arbitrary")),
    )(act, act, act, zs, zs, alog_b, dtb_b, zb, zb, dn_norm, states, dmixed)


def _pick(n, pref):
    for c in (pref, 1024, 512, 384, 256, 128):
        if c <= pref and n % c == 0:
            return c
    return n


def _mm(a, b, *, ta=False, tb=False, out_dtypes=(F32,), epilogue=None, extras=(), name, tm=512, tn=1024, tk=512):
    m, kd = (a.shape[1], a.shape[0]) if ta else a.shape
    n = b.shape[0] if tb else b.shape[1]
    tm, tn, tk = _pick(m, tm), _pick(n, tn), _pick(kd, tk)
    nk = kd // tk
    n_ex = len(extras)
    dims = (((0,) if ta else (1,), (1,) if tb else (0,)), ((), ()))

    def body(*refs):
        a_ref, b_ref = refs[0], refs[1]
        ex_refs = refs[2:2 + n_ex]
        out_refs = refs[2 + n_ex:-1]
        acc = refs[-1]
        k = pl.program_id(2)

        @pl.when(k == 0)
        def _():
            acc[...] = jnp.zeros_like(acc)

        acc[...] += lax.dot_general(a_ref[...].astype(BF16), b_ref[...].astype(BF16), dims,
                                    preferred_element_type=F32)

        @pl.when(k == nk - 1)
        def _():
            if epilogue is None:
                outs = (acc[...],)
            else:
                outs = epilogue(acc[...], *[r[...] for r in ex_refs])
            for r, o in zip(out_refs, outs):
                r[...] = o.astype(r.dtype)

    a_spec = pl.BlockSpec((tk, tm), lambda i, j, k: (k, i)) if ta else pl.BlockSpec((tm, tk), lambda i, j, k: (i, k))
    b_spec = pl.BlockSpec((tn, tk), lambda i, j, k: (j, k)) if tb else pl.BlockSpec((tk, tn), lambda i, j, k: (k, j))
    mn_spec = pl.BlockSpec((tm, tn), lambda i, j, k: (i, j))
    outs = pl.pallas_call(
        body, name=name, grid=(m // tm, n // tn, nk),
        in_specs=[a_spec, b_spec] + [mn_spec] * n_ex,
        out_specs=[mn_spec] * len(out_dtypes),
        out_shape=[jax.ShapeDtypeStruct((m, n), dt) for dt in out_dtypes],
        scratch_shapes=[pltpu.VMEM((tm, tn), F32)],
        compiler_params=_cparams(("parallel", "parallel", "arbitrary")),
    )(a, b, *extras)
    return outs[0] if len(out_dtypes) == 1 else outs


ROW_BLOCK = 256


def _row_spec(width=D_MODEL):
    return pl.BlockSpec((ROW_BLOCK, width), lambda i: (i, 0))


def _vec_spec(width=D_MODEL):
    return pl.BlockSpec((1, width), lambda i: (0, 0))


def _rms_fwd(x, g, name):
    def body(x_ref, g_ref, h_ref):
        xf = x_ref[...]
        h_ref[...] = (xf * lax.rsqrt(jnp.mean(xf * xf, axis=-1, keepdims=True) + EPS) * g_ref[...]).astype(BF16)

    t = x.shape[0]
    return pl.pallas_call(
        body, name=name, grid=(t // ROW_BLOCK,), in_specs=[_row_spec(), _vec_spec()], out_specs=_row_spec(),
        out_shape=jax.ShapeDtypeStruct((t, D_MODEL), BF16), compiler_params=_cparams(("parallel",)),
    )(x, g)


def _rms_bwd_math(xf, g, dh):
    rstd = lax.rsqrt(jnp.mean(xf * xf, axis=-1, keepdims=True) + EPS)
    xhat = xf * rstd
    dxhat = dh * g
    dx = rstd * (dxhat - xhat * jnp.mean(dxhat * xhat, axis=-1, keepdims=True))
    dg = jnp.sum(dh * xhat, axis=0, keepdims=True)
    return dx, dg


def _rms_bwd(x, g, dh, dres, name):
    def body(x_ref, g_ref, dh_ref, dres_ref, dx_ref, dg_ref):
        dx, dg = _rms_bwd_math(x_ref[...], g_ref[...], dh_ref[...].astype(F32))
        dx_ref[...] = dres_ref[...] + dx

        @pl.when(pl.program_id(0) == 0)
        def _():
            dg_ref[...] = jnp.zeros_like(dg_ref)

        dg_ref[...] += dg

    t = x.shape[0]
    return pl.pallas_call(
        body, name=name, grid=(t // ROW_BLOCK,),
        in_specs=[_row_spec(), _vec_spec(), _row_spec(), _row_spec()], out_specs=[_row_spec(), _vec_spec()],
        out_shape=[jax.ShapeDtypeStruct((t, D_MODEL), F32), jax.ShapeDtypeStruct((1, D_MODEL), F32)],
        compiler_params=_cparams(("arbitrary",)),
    )(x, g, dh, dres)


def _loss_head(x3, g, target):
    def body(x_ref, g_ref, t_ref, dx_ref, dg_ref, loss_ref):
        xf, gg = x_ref[...], g_ref[...]
        rstd = lax.rsqrt(jnp.mean(xf * xf, axis=-1, keepdims=True) + EPS)
        err = xf * rstd * gg - t_ref[...]
        dx, dg = _rms_bwd_math(xf, gg, err * (1.0 / D_MODEL))
        dx_ref[...] = dx

        @pl.when(pl.program_id(0) == 0)
        def _():
            dg_ref[...] = jnp.zeros_like(dg_ref)
            loss_ref[...] = jnp.zeros_like(loss_ref)

        dg_ref[...] += dg
        part = jnp.sum(jnp.sum(err * err, axis=-1, keepdims=True), axis=0, keepdims=True) * (0.5 / D_MODEL)
        loss_ref[...] += jnp.broadcast_to(part, loss_ref.shape)

    t = x3.shape[0]
    return pl.pallas_call(
        body, name="loss_head", grid=(t // ROW_BLOCK,),
        in_specs=[_row_spec(), _vec_spec(), _row_spec()], out_specs=[_row_spec(), _vec_spec(), _vec_spec(LANE)],
        out_shape=[jax.ShapeDtypeStruct((t, D_MODEL), F32), jax.ShapeDtypeStruct((1, D_MODEL), F32),
                   jax.ShapeDtypeStruct((1, LANE), F32)],
        compiler_params=_cparams(("arbitrary",)),
    )(x3, g, target)


def _ple_bwd(dx3, gpre, pp):
    def body(dx_ref, gp_ref, pp_ref, dgp_ref, dpp_ref):
        dx, sg = dx_ref[...], _sigmoid(gp_ref[...])
        dpp_ref[...] = (dx * sg).astype(BF16)
        dgp_ref[...] = (dx * pp_ref[...] * sg * (1.0 - sg)).astype(BF16)

    t = dx3.shape[0]
    return pl.pallas_call(
        body, name="ple_bwd", grid=(t // ROW_BLOCK,), in_specs=[_row_spec()] * 3, out_specs=[_row_spec()] * 2,
        out_shape=[jax.ShapeDtypeStruct((t, D_MODEL), BF16)] * 2, compiler_params=_cparams(("parallel",)),
    )(dx3, gpre, pp)


CONV_COLS = 256


def _shift_down(x, s):
    if s == 0:
        return x
    return jnp.where(_iota2(x.shape, 0) >= s, pltpu.roll(x, s, 0), 0.0)


def _shift_up(x, s):
    if s == 0:
        return x
    rows = x.shape[0]
    return jnp.where(_iota2(x.shape, 0) < rows - s, pltpu.roll(x, rows - s, 0), 0.0)


def _conv_pre(xf, w):
    return sum(_shift_down(xf, DN_CONV - 1 - j) * w[j:j + 1, :] for j in range(DN_CONV))


def _conv_fwd(zb, conv_w, bsz, seq):
    def body(x_ref, w_ref, y_ref):
        y_ref[...] = _silu(_conv_pre(x_ref[...], w_ref[...]))

    nblk = DN_QKV // CONV_COLS
    return pl.pallas_call(
        body, name="conv_fwd", grid=(bsz, nblk),
        in_specs=[pl.BlockSpec((seq, CONV_COLS), lambda b, j: (b, ZB_DQKV // CONV_COLS + j)),
                  pl.BlockSpec((DN_CONV, CONV_COLS), lambda b, j: (0, j))],
        out_specs=pl.BlockSpec((seq, CONV_COLS), lambda b, j: (b, j)),
        out_shape=jax.ShapeDtypeStruct((bsz * seq, DN_QKV), F32),
        compiler_params=_cparams(("parallel", "parallel")),
    )(zb, conv_w)


def _conv_bwd(zb, conv_w, dact, bsz, seq):
    def body(x_ref, w_ref, dy_ref, dx_ref, dw_ref):
        xf, w = x_ref[...], w_ref[...]
        c = _conv_pre(xf, w)
        sg = _sigmoid(c)
        dc = dy_ref[...].astype(F32) * sg * (1.0 + c * (1.0 - sg))
        dx = sum(_shift_up(dc, DN_CONV - 1 - j) * w[j:j + 1, :] for j in range(DN_CONV))
        dx_ref[...] = dx.astype(BF16)
        dw = jnp.concatenate(
            [jnp.sum(dc * _shift_down(xf, DN_CONV - 1 - j), axis=0, keepdims=True) for j in range(DN_CONV)], axis=0)

        @pl.when(pl.program_id(1) == 0)
        def _():
            dw_ref[...] = jnp.zeros_like(dw_ref)

        dw_ref[...] += dw

    nblk = DN_QKV // CONV_COLS
    return pl.pallas_call(
        body, name="conv_bwd", grid=(nblk, bsz),
        in_specs=[pl.BlockSpec((seq, CONV_COLS), lambda j, b: (b, ZB_DQKV // CONV_COLS + j)),
                  pl.BlockSpec((DN_CONV, CONV_COLS), lambda j, b: (0, j)),
                  pl.BlockSpec((seq, CONV_COLS), lambda j, b: (b, j))],
        out_specs=[pl.BlockSpec((seq, CONV_COLS), lambda j, b: (b, j)),
                   pl.BlockSpec((DN_CONV, CONV_COLS), lambda j, b: (0, j))],
        out_shape=[jax.ShapeDtypeStruct((bsz * seq, DN_QKV), BF16), jax.ShapeDtypeStruct((DN_CONV, DN_QKV), F32)],
        compiler_params=_cparams(("parallel", "arbitrary")),
    )(zb, conv_w, dact)


MESH_IDS = pl.DeviceIdType.MESH
ANY_SPEC = pl.BlockSpec(memory_space=pl.ANY)


def _all_gather(x, name):
    def body(x_ref, out_ref, send_sems, recv_sems, local_sem):
        mx, my, mc = lax.axis_index("x"), lax.axis_index("y"), lax.axis_index("c")
        me, sibling = (mx, my, mc), (mx, my, 1 - mc)
        chips = [(1 - mx, my), (mx, 1 - my), (1 - mx, 1 - my)]

        def slot(px, py, pc):
            return out_ref.at[4 * px + 2 * py + pc]

        def copy(k, block, to, src=None):
            return pltpu.make_async_remote_copy(
                src_ref=slot(*block) if src is None else src, dst_ref=slot(*block),
                send_sem=send_sems.at[k], recv_sem=recv_sems.at[k], device_id=to, device_id_type=MESH_IDS)

        mine = pltpu.make_async_copy(x_ref, slot(*me), local_sem)
        mine.start()
        first = [copy(0, me, sibling, src=x_ref)]
        first += [copy(1 + j, me, (*chip, mc), src=x_ref) for j, chip in enumerate(chips)]
        for cp in first:
            cp.start()
        passed = [copy(4 + j, (*chip, mc), sibling) for j, chip in enumerate(chips)]
        for j, chip in enumerate(chips):
            copy(1 + j, (*chip, mc), me).wait_recv()
            passed[j].start()
        copy(0, sibling, me).wait_recv()
        for j, chip in enumerate(chips):
            copy(4 + j, (*chip, 1 - mc), me).wait_recv()
        for cp in first + passed:
            cp.wait_send()
        mine.wait()

    return pl.pallas_call(
        body, name=name, out_shape=jax.ShapeDtypeStruct((N_DEV,) + x.shape, x.dtype),
        in_specs=[ANY_SPEC], out_specs=ANY_SPEC,
        scratch_shapes=[pltpu.SemaphoreType.DMA((7,)), pltpu.SemaphoreType.DMA((7,)), pltpu.SemaphoreType.DMA],
    )(x)


def _all_to_all(x, name):
    def body(x_ref, out_ref, send_sems, recv_sems, local_sem):
        mx, my, mc = lax.axis_index("x"), lax.axis_index("y"), lax.axis_index("c")
        me = 4 * mx + 2 * my + mc

        def peer(k):
            return (mx ^ ((k >> 2) & 1), my ^ ((k >> 1) & 1), mc ^ (k & 1))

        def copy(k):
            px, py, pc = peer(k)
            return pltpu.make_async_remote_copy(
                src_ref=x_ref.at[4 * px + 2 * py + pc], dst_ref=out_ref.at[me],
                send_sem=send_sems.at[k - 1], recv_sem=recv_sems.at[k - 1],
                device_id=(px, py, pc), device_id_type=MESH_IDS)

        def landed(k):
            px, py, pc = peer(k)
            return pltpu.make_async_remote_copy(
                src_ref=x_ref.at[me], dst_ref=out_ref.at[4 * px + 2 * py + pc],
                send_sem=send_sems.at[k - 1], recv_sem=recv_sems.at[k - 1],
                device_id=(px, py, pc), device_id_type=MESH_IDS)

        mine = pltpu.make_async_copy(x_ref.at[me], out_ref.at[me], local_sem)
        mine.start()
        sends = [copy(k) for k in range(1, N_DEV)]
        for cp in sends:
            cp.start()
        for k in range(1, N_DEV):
            landed(k).wait_recv()
        for cp in sends:
            cp.wait_send()
        mine.wait()

    return pl.pallas_call(
        body, name=name, out_shape=jax.ShapeDtypeStruct(x.shape, x.dtype),
        in_specs=[ANY_SPEC], out_specs=ANY_SPEC,
        scratch_shapes=[pltpu.SemaphoreType.DMA((7,)), pltpu.SemaphoreType.DMA((7,)), pltpu.SemaphoreType.DMA],
    )(x)


def _adamw_math(w, g, m, v):
    m = ADAM_B1 * m + (1.0 - ADAM_B1) * g
    v = ADAM_B2 * v + (1.0 - ADAM_B2) * jnp.square(g)
    m_hat = m / (1.0 - ADAM_B1 ** ADAM_STEP)
    v_hat = v / (1.0 - ADAM_B2 ** ADAM_STEP)
    delta = -ADAM_LR * (m_hat / (jnp.sqrt(v_hat) + ADAM_EPS) + ADAM_WD * w)
    return delta, m, v


ADAM_ROWS = 128


def _adamw_reduce(w, m, v, parts, row0, name):
    rows, cols = w.shape
    tr = ADAM_ROWS if rows % ADAM_ROWS == 0 else rows
    r0 = row0 // tr

    def body(w_ref, m_ref, v_ref, *refs):
        part_refs, (g_ref, d_ref, nm_ref, nv_ref) = refs[:N_DEV], refs[N_DEV:]
        g = part_refs[0][...].astype(F32)
        for r in part_refs[1:]:
            g = g + r[...].astype(F32)
        delta, nm, nv = _adamw_math(w_ref[...], g, m_ref[...], v_ref[...])
        g_ref[...] = g
        d_ref[...] = delta
        nm_ref[...] = nm
        nv_ref[...] = nv

    blk = pl.BlockSpec((tr, cols), lambda i: (i, 0))
    part_specs = [pl.BlockSpec((None, tr, cols), functools.partial(lambda i, k: (k, r0 + i, 0), k=k))
                  for k in range(N_DEV)]
    return pl.pallas_call(
        body, name=name, grid=(rows // tr,), in_specs=[blk] * 3 + part_specs, out_specs=[blk] * 4,
        out_shape=[jax.ShapeDtypeStruct(w.shape, F32)] * 4, compiler_params=_cparams(("parallel",)),
    )(w, m, v, *([parts] * N_DEV))


def _small_reduce(gathered, lane_sum_from):
    r = gathered.shape[1]

    def body(g_ref, o_ref):
        g = g_ref[0]
        for k in range(1, N_DEV):
            g = g + g_ref[k]
        tot = jnp.broadcast_to(jnp.sum(g, axis=-1, keepdims=True), g.shape)
        o_ref[...] = jnp.where(_iota2(g.shape, 0) >= lane_sum_from, tot, g)

    return pl.pallas_call(body, name="small_grad_reduce", out_shape=jax.ShapeDtypeStruct((r, LANE), F32))(gathered)


def _adamw_small(w, m, v, g):
    def body(w_ref, m_ref, v_ref, g_ref, d_ref, nm_ref, nv_ref):
        d_ref[...], nm_ref[...], nv_ref[...] = _adamw_math(w_ref[...], g_ref[...], m_ref[...], v_ref[...])

    return pl.pallas_call(body, name="adamw_small", out_shape=[jax.ShapeDtypeStruct(w.shape, F32)] * 3)(w, m, v, g)


def _pack_rows(arrays):
    rows = [jnp.pad(a.reshape(-1), (0, -a.size % LANE)).reshape(-1, LANE) for a in arrays]
    out = jnp.concatenate(rows, axis=0)
    return jnp.pad(out, ((0, -out.shape[0] % 8), (0, 0)))


def _unpack_rows(packed, shapes):
    out, r = [], 0
    for shp in shapes:
        size = math.prod(shp)
        nrows = -(-size // LANE)
        out.append(packed[r:r + nrows].reshape(-1)[:size].reshape(shp))
        r += nrows
    return out


def _add_residual(acc, res):
    return (res + acc,)


def _local_step(x, p, target, w):
    bsz, seq, _ = x.shape
    t, nc = bsz * seq, seq // CHUNK
    x0, p2, tgt = x.reshape(t, D_MODEL), p.reshape(t, PLE_DIM), target.reshape(t, D_MODEL)

    h = _rms_fwd(x0, w["g_mix"], "rms_mix")
    zb = _mm(h, w["wb"], name="in_proj")
    zs = _mm(h, w["ws"], name="in_proj_gates")
    act = _conv_fwd(zb, w["conv"], bsz, seq)
    mix_gla, gla_states = _gla_fwd(zb, zs, w["w2p"], w["gla_b"], w["gla_norm"], bsz, nc)
    mixed, dn_states = _dn_fwd(act, zb, zs, w["alog_b"], w["dtb_b"], w["dn_norm"], mix_gla, bsz, nc)
    x1 = _mm(mixed, w["w_out"], epilogue=_add_residual, extras=(x0,), name="out_proj")
    h2 = _rms_fwd(x1, w["g_mlp"], "rms_mlp")
    u, a = _mm(h2, w["w_up"], out_dtypes=(BF16, BF16), name="mlp_up",
               epilogue=lambda acc: (acc, jnp.square(jnp.maximum(acc, 0.0))))
    x2 = _mm(a, w["w_down"], epilogue=_add_residual, extras=(x1,), name="mlp_down")
    h3 = _rms_fwd(x2, w["g_ple"], "rms_ple")
    pp = _mm(p2, w["w_pp"], name="ple_proj")
    gpre, x3 = _mm(h3, w["w_pg"], out_dtypes=(F32, F32), extras=(x2, pp), name="ple_gate",
                   epilogue=lambda acc, res, proj: (acc, res + _sigmoid(acc) * proj))
    dx3, dg_final, loss = _loss_head(x3, w["g_final"], tgt)

    dgpre, dpp = _ple_bwd(dx3, gpre, pp)
    dw_pp = _mm(p2, dpp, ta=True, out_dtypes=(BF16,), name="d_w_ple_proj")
    dw_pg = _mm(h3, dgpre, ta=True, out_dtypes=(BF16,), name="d_w_ple_gate")
    dh3 = _mm(dgpre, w["w_pg"], tb=True, name="d_h_ple")
    dx2, dg_ple = _rms_bwd(x2, w["g_ple"], dh3, dx3, "rms_ple_bwd")
    du = _mm(dx2, w["w_down"], tb=True, out_dtypes=(BF16,), extras=(u,), name="d_mlp_hidden",
             epilogue=lambda acc, uu: (acc * (2.0 * jnp.maximum(uu.astype(F32), 0.0)),))
    dw_down = _mm(a, dx2, ta=True, out_dtypes=(BF16,), name="d_w_down")
    dw_up = _mm(h2, du, ta=True, out_dtypes=(BF16,), name="d_w_up")
    dh2 = _mm(du, w["w_up"], tb=True, name="d_h_mlp")
    dx1, dg_mlp = _rms_bwd(x1, w["g_mlp"], dh2, dx2, "rms_mlp_bwd")
    dmixed = _mm(dx1, w["w_out"], tb=True, out_dtypes=(BF16,), name="d_mixed")
    dw_out = _mm(mixed, dx1, ta=True, out_dtypes=(BF16,), name="d_w_out")

    (ddq, ddk, ddv, dza, dzb_, dgz, dgb, dal, ddt, ddn) = _dn_bwd(
        act, zb, zs, w["alog_b"], w["dtb_b"], w["dn_norm"], dn_states, dmixed, bsz, nc)
    (gdq, gdk, gdv, dgg, dga, dlr, dw2, dgla_b, dgla_norm) = _gla_bwd(
        zb, zs, w["w2p"], w["gla_b"], w["gla_norm"], gla_states, dmixed, bsz, nc)
    dqkv, dconv = _conv_bwd(zb, w["conv"], jnp.concatenate([ddq, ddk, ddv], axis=1), bsz, seq)
    dzb = jnp.concatenate([gdq, gdk, gdv, dgg, dqkv, dgz, dga, dgb], axis=1)
    dzs = jnp.concatenate([dlr, dza, dzb_], axis=1)
    dh_gates = _mm(dzs, w["ws"], tb=True, name="d_h_mix_gates")
    dh = _mm(dzb, w["wb"], tb=True, epilogue=_add_residual, extras=(dh_gates,), name="d_h_mix")
    dwb = _mm(h, dzb, ta=True, out_dtypes=(BF16,), name="d_w_in")
    dws = _mm(h, dzs, ta=True, out_dtypes=(BF16,), name="d_w_in_gates")
    gx, dg_mix = _rms_bwd(x0, w["g_mix"], dh, dx1, "rms_mix_bwd")

    dw_in = jnp.concatenate([dwb[:, :WI_LR], dws[:, ZS_LR:ZS_LR + GLA_LOWRANK],
                             dwb[:, ZB_DQKV:ZB_GA], dws[:, ZS_DA:ZS_DA + DN_HEADS],
                             dws[:, ZS_DB:ZS_DB + DN_HEADS], dwb[:, ZB_GA:]], axis=1)
    dgla_w2 = dw2[:, :GLA_LOWRANK, :].transpose(1, 0, 2).reshape(GLA_LOWRANK, GLA_QK)
    return dict(
        loss=loss[0, 0], grad_x=gx.reshape(x.shape), w_in=dw_in, w_up=dw_up, w_out=dw_out, w_ple_gate=dw_pg,
        w_down=dw_down, w_ple_proj=dw_pp,
        g_mix=dg_mix, gla_b=dgla_b.reshape(1, GLA_QK), gla_norm=dgla_norm, dn_norm=ddn, g_mlp=dg_mlp, g_ple=dg_ple,
        g_final=dg_final, gla_w2=dgla_w2, dn_conv=dconv,
        a_log_lanes=dal.reshape(DN_HEADS, LANE), dt_bias_lanes=ddt.reshape(DN_HEADS, LANE))


def _full_weights(g_mix, w_in, gla_w2, gla_b, gla_norm, dn_conv, dn_a_log, dn_dt_bias, dn_norm, w_out, g_mlp,
                  w_up, w_down, g_ple, w_ple_gate, w_ple_proj, g_final):
    cols = _all_gather(jnp.concatenate([w_in[0].astype(BF16), w_up[0].astype(BF16)], axis=1), "gather_w_cols")
    w_in_full = cols[:, :, :D_IN_SHARD].transpose(1, 0, 2).reshape(D_MODEL, D_IN)
    w_up_full = cols[:, :, D_IN_SHARD:].transpose(1, 0, 2).reshape(D_MODEL, D_FF)
    rows = _all_gather(jnp.concatenate([w_out[0], w_ple_gate[0], w_down[0]], axis=0).astype(BF16), "gather_w_rows")
    r_out = D_MODEL // N_DEV
    w_pp = _all_gather(w_ple_proj[0].astype(BF16), "gather_w_ple_proj")
    small = _all_gather(_pack_rows([gla_w2[0], dn_conv[0]]), "gather_w_small")
    n_w2 = GLA_LOWRANK * GLA_QK // N_DEV // LANE
    n_cv = DN_CONV * DN_QKV // N_DEV // LANE
    w2 = small[:, :n_w2].reshape(N_DEV, GLA_LOWRANK, GLA_QK // N_DEV).transpose(1, 0, 2).reshape(GLA_LOWRANK, GLA_QK)
    conv = small[:, n_w2:n_w2 + n_cv].reshape(N_DEV, DN_CONV, DN_QKV // N_DEV).transpose(1, 0, 2).reshape(DN_CONV, DN_QKV)

    def lane_pad(wcols):
        return jnp.pad(wcols, ((0, 0), (0, LANE - wcols.shape[1])))

    return dict(
        wb=jnp.concatenate([w_in_full[:, :WI_LR], w_in_full[:, WI_DQKV:WI_DA], w_in_full[:, WI_GA:]], axis=1),
        ws=jnp.concatenate([lane_pad(w_in_full[:, WI_LR:WI_DQKV]), lane_pad(w_in_full[:, WI_DA:WI_DB]),
                            lane_pad(w_in_full[:, WI_DB:WI_GA])], axis=1),
        w_up=w_up_full,
        w_out=rows[:, :r_out].reshape(D_MODEL, D_MODEL),
        w_pg=rows[:, r_out:2 * r_out].reshape(D_MODEL, D_MODEL),
        w_down=rows[:, 2 * r_out:].reshape(D_FF, D_MODEL),
        w_pp=w_pp.transpose(1, 0, 2).reshape(PLE_DIM, D_MODEL),
        w2p=jnp.pad(w2, ((0, LANE - GLA_LOWRANK), (0, 0))), conv=conv,
        alog_b=jnp.broadcast_to(dn_a_log[0][:, None, None], (DN_HEADS, 1, LANE)),
        dtb_b=jnp.broadcast_to(dn_dt_bias[0][:, None, None], (DN_HEADS, 1, LANE)),
        g_mix=g_mix, gla_b=gla_b, gla_norm=gla_norm, dn_norm=dn_norm, g_mlp=g_mlp, g_ple=g_ple,
        g_final=g_final.reshape(1, D_MODEL))


def kernel(x, p, g_mix, w_in, gla_w2, gla_b, gla_norm, dn_conv, dn_a_log, dn_dt_bias, dn_norm, w_out, g_mlp, w_up, w_down, g_ple, w_ple_gate, w_ple_proj, g_final, loss_target, m_g_mix, m_w_in, m_gla_w2, m_gla_b, m_gla_norm, m_dn_conv, m_dn_a_log, m_dn_dt_bias, m_dn_norm, m_w_out, m_g_mlp, m_w_up, m_w_down, m_g_ple, m_w_ple_gate, m_w_ple_proj, m_g_final, v_g_mix, v_w_in, v_gla_w2, v_gla_b, v_gla_norm, v_dn_conv, v_dn_a_log, v_dn_dt_bias, v_dn_norm, v_w_out, v_g_mlp, v_w_up, v_w_down, v_g_ple, v_w_ple_gate, v_w_ple_proj, v_g_final):
    names = ["g_mix", "w_in", "gla_w2", "gla_b", "gla_norm", "dn_conv", "dn_a_log", "dn_dt_bias", "dn_norm", "w_out",
             "g_mlp", "w_up", "w_down", "g_ple", "w_ple_gate", "w_ple_proj", "g_final"]
    ws = dict(zip(names, (g_mix, w_in, gla_w2, gla_b, gla_norm, dn_conv, dn_a_log, dn_dt_bias, dn_norm, w_out, g_mlp,
                          w_up, w_down, g_ple, w_ple_gate, w_ple_proj, g_final)))
    ms = dict(zip(names, (m_g_mix, m_w_in, m_gla_w2, m_gla_b, m_gla_norm, m_dn_conv, m_dn_a_log, m_dn_dt_bias,
                          m_dn_norm, m_w_out, m_g_mlp, m_w_up, m_w_down, m_g_ple, m_w_ple_gate, m_w_ple_proj,
                          m_g_final)))
    vs = dict(zip(names, (v_g_mix, v_w_in, v_gla_w2, v_gla_b, v_gla_norm, v_dn_conv, v_dn_a_log, v_dn_dt_bias,
                          v_dn_norm, v_w_out, v_g_mlp, v_w_up, v_w_down, v_g_ple, v_w_ple_gate, v_w_ple_proj,
                          v_g_final)))
    me = 4 * lax.axis_index("x") + 2 * lax.axis_index("y") + lax.axis_index("c")

    full = _full_weights(*[ws[n] for n in names])
    r = _local_step(x, p[0], loss_target, full)
    loss = lax.psum(r["loss"], ("x", "y", "c"))

    grads, deltas, new_m, new_v = {}, {}, {}, {}

    def big(name, parts, row0=0):
        g, d, nm, nv = _adamw_reduce(ws[name][0], ms[name][0], vs[name][0], parts, row0, "adamw_" + name)
        grads[name], deltas[name], new_m[name], new_v[name] = g[None], d[None], nm[None], nv[None]

    def by_cols(g):
        return g.reshape(g.shape[0], N_DEV, -1).transpose(1, 0, 2)

    big("w_in", _all_to_all(by_cols(r["w_in"]), "scatter_d_w_in"))
    big("w_up", _all_to_all(by_cols(r["w_up"]), "scatter_d_w_up"))
    big("w_ple_proj", _all_to_all(by_cols(r["w_ple_proj"]), "scatter_d_w_ple_proj"))
    r_out = D_MODEL // N_DEV
    rows = _all_to_all(jnp.concatenate([r["w_out"].reshape(N_DEV, r_out, D_MODEL),
                                        r["w_ple_gate"].reshape(N_DEV, r_out, D_MODEL),
                                        r["w_down"].reshape(N_DEV, D_FF // N_DEV, D_MODEL)], axis=1), "scatter_d_w_rows")
    big("w_out", rows, 0)
    big("w_ple_gate", rows, r_out)
    big("w_down", rows, 2 * r_out)

    vec_names = ["g_mix", "gla_b", "gla_norm", "dn_norm", "g_mlp", "g_ple", "g_final"]
    packed = _pack_rows([r[n] for n in vec_names] + [r["gla_w2"], r["dn_conv"]])
    lane_rows = packed.shape[0]
    packed = jnp.concatenate([packed, r["a_log_lanes"], r["dt_bias_lanes"]], axis=0)
    total = _small_reduce(_all_gather(packed, "gather_small_grads"), lane_rows)
    parts = _unpack_rows(total, [r[n].shape for n in vec_names] + [r["gla_w2"].shape, r["dn_conv"].shape])
    sg = dict(zip(vec_names, parts[:len(vec_names)]))
    sg["g_final"] = sg["g_final"].reshape(D_MODEL)
    sg["gla_w2"] = lax.dynamic_slice_in_dim(parts[-2], me * (GLA_QK // N_DEV), GLA_QK // N_DEV, axis=1)
    sg["dn_conv"] = lax.dynamic_slice_in_dim(parts[-1], me * (DN_QKV // N_DEV), DN_QKV // N_DEV, axis=1)
    sg["dn_a_log"] = total[lane_rows:lane_rows + DN_HEADS, 0]
    sg["dn_dt_bias"] = total[lane_rows + DN_HEADS:lane_rows + 2 * DN_HEADS, 0]
    small_names = vec_names + ["gla_w2", "dn_conv", "dn_a_log", "dn_dt_bias"]
    shapes = [ws[n].shape for n in small_names]
    d_s, m_s, v_s = _adamw_small(_pack_rows([ws[n] for n in small_names]), _pack_rows([ms[n] for n in small_names]),
                                 _pack_rows([vs[n] for n in small_names]), _pack_rows([sg[n] for n in small_names]))
    for n, d, nm, nv in zip(small_names, _unpack_rows(d_s, shapes), _unpack_rows(m_s, shapes), _unpack_rows(v_s, shapes)):
        grads[n], deltas[n], new_m[n], new_v[n] = sg[n].reshape(ws[n].shape), d, nm, nv

    return (loss, r["grad_x"], *[grads[n] for n in names], *[deltas[n] for n in names],
            *[new_m[n] for n in names], *[new_v[n] for n in names])
```

```python
import functools
import math

import jax
import jax.numpy as jnp
from jax import lax
from jax.experimental import pallas as pl
from jax.experimental.pallas import tpu as pltpu

F32 = jnp.float32
BF16 = jnp.bfloat16
HIGHEST = lax.Precision.HIGHEST

N_DEV = 8
D_MODEL = 2048
CHUNK = 64
PLE_DIM = 256
EPS = 1e-6
GLA_HEADS = 4
GLA_DK = 256
GLA_DV = 512
GLA_LOWRANK = 16
GLA_TAU = 16.0
DN_HEADS = 16
DN_D = 128
DN_CONV = 4
D_FF = 4 * D_MODEL
GLA_QK = GLA_HEADS * GLA_DK
GLA_V = GLA_HEADS * GLA_DV
DN_QKV = 3 * DN_HEADS * DN_D
D_IN = 2 * GLA_QK + 2 * GLA_V + GLA_LOWRANK + DN_QKV + D_MODEL + 2 * DN_HEADS + 2 * D_MODEL
D_IN_SHARD = D_IN // N_DEV

ADAM_LR = 0.001
ADAM_B1 = 0.9
ADAM_B2 = 0.999
ADAM_EPS = 1e-08
ADAM_WD = 0.01
ADAM_STEP = 10

LANE = 128
ZB_GQ, ZB_GK, ZB_GV, ZB_GG = 0, 1024, 2048, 4096
ZB_DQKV, ZB_DZ, ZB_GA, ZB_GB = 6144, 12288, 14336, 16384
ZB_W = 18432
ZS_LR, ZS_DA, ZS_DB = 0, 128, 256
ZS_W = 384
WI_LR = 2 * GLA_QK + 2 * GLA_V
WI_DQKV = WI_LR + GLA_LOWRANK
WI_DA = WI_DQKV + DN_QKV + D_MODEL
WI_DB = WI_DA + DN_HEADS
WI_GA = WI_DB + DN_HEADS

VMEM_LIMIT = 56 * 1024 * 1024

NN = (((1,), (0,)), ((), ()))
NT = (((1,), (1,)), ((), ()))
TN = (((0,), (0,)), ((), ()))


def _bdot(a, b, dims):
    return lax.dot_general(a.astype(BF16), b.astype(BF16), dims, preferred_element_type=F32)


def _split3(x):
    hi = x.astype(BF16)
    rest = x - hi.astype(F32)
    mid = rest.astype(BF16)
    return hi, mid, (rest - mid.astype(F32)).astype(BF16)


def _dot01(x, m, dims, x_first):
    m = m.astype(BF16)
    out = None
    for piece in _split3(x):
        d = lax.dot_general(piece, m, dims, preferred_element_type=F32) if x_first else \
            lax.dot_general(m, piece, dims, preferred_element_type=F32)
        out = d if out is None else out + d
    return out


@functools.partial(jax.custom_vjp, nondiff_argnums=(2, 3))
def _pick_dot(x, m, dims, dims_t):
    return _dot01(x, m, dims, True)


def _pick_dot_fwd(x, m, dims, dims_t):
    return _dot01(x, m, dims, True), m


def _pick_dot_bwd(dims, dims_t, m, ct):
    return _dot01(ct, m, dims_t, True), jnp.zeros_like(m)


_pick_dot.defvjp(_pick_dot_fwd, _pick_dot_bwd)


@functools.partial(jax.custom_vjp, nondiff_argnums=(2, 3))
def _left_dot(m, x, dims, dims_t):
    return _dot01(x, m, dims, False)


def _left_dot_fwd(m, x, dims, dims_t):
    return _dot01(x, m, dims, False), m


def _left_dot_bwd(dims, dims_t, m, ct):
    return jnp.zeros_like(m), _dot01(ct, m, dims_t, False)


_left_dot.defvjp(_left_dot_fwd, _left_dot_bwd)


def _dot3(a, b, dims):
    ah, bh = a.astype(BF16), b.astype(BF16)
    al, bl = (a - ah.astype(F32)).astype(BF16), (b - bh.astype(F32)).astype(BF16)
    dot = functools.partial(lax.dot_general, dimension_numbers=dims, preferred_element_type=F32)
    return dot(ah, bh) + (dot(ah, bl) + dot(al, bh))


def _sigmoid(x):
    return 1.0 / (1.0 + jnp.exp(-x))


def _silu(x):
    return x * _sigmoid(x)


def _softplus(x):
    return jnp.maximum(x, 0.0) + jnp.log(1.0 + jnp.exp(-jnp.abs(x)))


def _iota2(shape, dim):
    return lax.broadcasted_iota(jnp.int32, shape, dim)


def _cparams(sem=None):
    return pltpu.CompilerParams(dimension_semantics=sem, vmem_limit_bytes=VMEM_LIMIT)


def _gla_chunk(st, q, k, v, lr, w2, b, gg, ga, gn):
    c = q.shape[0]
    row, col = _iota2((c, c), 0), _iota2((c, c), 1)
    incl = row >= col
    tri = incl.astype(F32)
    q = q.astype(F32) * (GLA_DK ** -0.5)
    k = k.astype(F32)
    v = v.astype(F32)
    lf = -_softplus(-(_bdot(lr, w2, NN) + b)) / GLA_TAU
    bcum = _left_dot(tri, lf, NN, TN)
    blast = jnp.sum(lf, axis=0, keepdims=True)
    q_in = q * jnp.exp(bcum)
    k_in = k * jnp.exp(-bcum)
    a = jnp.where(incl, _bdot(q_in, k_in, NT), 0.0)
    o = _bdot(a, v, NN) + _bdot(q_in, st, NT)
    k_dec = k * jnp.exp(blast - bcum)
    st_new = st * jnp.exp(blast) + _bdot(v, k_dec, TN)
    on = o * lax.rsqrt(jnp.mean(o * o, axis=-1, keepdims=True) + EPS) * gn
    res = _sigmoid(ga.astype(F32)) * on * _silu(gg.astype(F32))
    return res, st_new


BNN = (((2,), (1,)), ((0,), (0,)))
BNT = (((2,), (2,)), ((0,), (0,)))
BTN = (((1,), (1,)), ((0,), (0,)))


def _tri_inv_raw(a):
    _, c, _ = a.shape
    eye = (_iota2((c, c), 0) == _iota2((c, c), 1)).astype(F32)[None]
    x = a
    p = eye - a
    for _ in range(5):
        x = _dot3(x, x, BNN)
        p = _dot3(p, eye + x, BNN)
    return p


@jax.custom_vjp
def _tri_inv(a):
    return _tri_inv_raw(a)


def _tri_inv_fwd(a):
    t = _tri_inv_raw(a)
    return t, t


def _tri_inv_bwd(t, dt):
    return (-_dot3(_dot3(t, dt, BTN), t, BNT),)


_tri_inv.defvjp(_tri_inv_fwd, _tri_inv_bwd)


def _dn_chunk(s, qr, kr, vr, za, zb, alog, dtb, gz, gb, dn, h0, differentiable):
    hb, c, _ = qr.shape
    row, col = _iota2((c, c), 0), _iota2((c, c), 1)
    incl = (row >= col)[None]
    strict = (row > col)[None]
    tri = jnp.broadcast_to(incl.astype(F32), (hb, c, c))
    eye = (row == col).astype(F32)[None]
    ones_cc = jnp.ones((hb, c, c), F32)
    sel_shape = (hb, LANE, LANE)
    sel = (lax.broadcasted_iota(jnp.int32, sel_shape, 1) == h0 + lax.broadcasted_iota(jnp.int32, sel_shape, 0))
    sel = sel.astype(F32)
    lane0 = (lax.broadcasted_iota(jnp.int32, (hb, LANE, c), 1) == 0).astype(F32)

    def l2n(t):
        return t * lax.rsqrt(jnp.sum(t * t, axis=-1, keepdims=True) + EPS)

    q = l2n(qr.astype(F32)) * (DN_D ** -0.5)
    k = l2n(kr.astype(F32))
    v = vr.astype(F32)
    za_b = jnp.broadcast_to(za[None], (hb,) + za.shape)
    zb_b = jnp.broadcast_to(zb[None], (hb,) + zb.shape)
    g = -jnp.exp(alog) * _softplus(_pick_dot(za_b, sel, BNN, BNT) + dtb)
    beta = _sigmoid(_pick_dot(zb_b, sel, BNN, BNT))
    gcum = _left_dot(tri, g, BNN, BTN)
    glast = jnp.sum(g, axis=1, keepdims=True)
    cm = _pick_dot(gcum, lane0, BNN, BNT)
    rm = _left_dot(ones_cc, cm * eye, BNN, BTN)
    dec = jnp.exp(jnp.where(incl, cm - rm, -1e30))
    kb = k * beta
    a = jnp.where(strict, _bdot(kb, k, BNT) * dec, 0.0)
    t = _tri_inv(a) if differentiable else _tri_inv_raw(a)
    egc = jnp.exp(gcum)
    u = _bdot(t, v * beta, BNN)
    w = _bdot(t, kb * egc, BNN)
    attn = jnp.where(incl, _bdot(q, k, BNT) * dec, 0.0)
    q_dec = q * egc
    k_dec = k * jnp.exp(glast - gcum)
    v_new = u - _bdot(w, s, BNN)
    o = _bdot(q_dec, s, BNN) + _bdot(attn, v_new, BNN)
    s_new = s * jnp.exp(glast) + _bdot(k_dec, v_new, BTN)
    on = o * lax.rsqrt(jnp.mean(o * o, axis=-1, keepdims=True) + EPS) * dn
    res = _sigmoid(gb.astype(F32)) * on * _silu(gz.astype(F32))
    return res, s_new


def _gla_fwd(zb, zs, w2p, gla_b, gla_norm, bsz, nc):
    t = zb.shape[0]

    def body(q_ref, k_ref, v_ref, gg_ref, ga_ref, lr_ref, w2_ref, b_ref, gn_ref, o_ref, st_ref, state):
        n, h = pl.program_id(1), pl.program_id(2)

        @pl.when(n == 0)
        def _():
            state[h] = jnp.zeros((GLA_DV, GLA_DK), F32)

        st = state[h]
        st_ref[0, 0, 0] = st
        res, st_new = _gla_chunk(st, q_ref[...], k_ref[...], v_ref[...], lr_ref[...], w2_ref[...], b_ref[...],
                                 gg_ref[...], ga_ref[...], gn_ref[...])
        o_ref[...] = res
        state[h] = st_new

    def rows(b, n, h):
        return b * nc + n

    qk = lambda base: pl.BlockSpec((CHUNK, GLA_DK), lambda b, n, h: (rows(b, n, h), base // GLA_DK + h))
    vv = lambda base: pl.BlockSpec((CHUNK, GLA_DV), lambda b, n, h: (rows(b, n, h), base // GLA_DV + h))
    return pl.pallas_call(
        body, name="gla_fwd", grid=(bsz, nc, GLA_HEADS),
        in_specs=[qk(ZB_GQ), qk(ZB_GK), vv(ZB_GV), vv(ZB_GG), vv(ZB_GA),
                  pl.BlockSpec((CHUNK, LANE), lambda b, n, h: (rows(b, n, h), ZS_LR // LANE)),
                  pl.BlockSpec((LANE, GLA_DK), lambda b, n, h: (0, h)),
                  pl.BlockSpec((1, GLA_DK), lambda b, n, h: (0, h)),
                  pl.BlockSpec((1, GLA_DV), lambda b, n, h: (0, 0))],
        out_specs=[pl.BlockSpec((CHUNK, GLA_DV), lambda b, n, h: (rows(b, n, h), h)),
                   pl.BlockSpec((1, 1, 1, GLA_DV, GLA_DK), lambda b, n, h: (b, n, h, 0, 0))],
        out_shape=[jax.ShapeDtypeStruct((t, GLA_V), F32),
                   jax.ShapeDtypeStruct((bsz, nc, GLA_HEADS, GLA_DV, GLA_DK), F32)],
        scratch_shapes=[pltpu.VMEM((GLA_HEADS, GLA_DV, GLA_DK), F32)],
        compiler_params=_cparams(("arbitrary", "arbitrary", "arbitrary")),
    )(zb, zb, zb, zb, zb, zs, w2p, gla_b, gla_norm)


def _gla_bwd(zb, zs, w2p, gla_b, gla_norm, states, dmixed, bsz, nc):
    t = zb.shape[0]

    def body(q_ref, k_ref, v_ref, gg_ref, ga_ref, lr_ref, w2_ref, b_ref, gn_ref, st_ref, dm_ref,
             dq_ref, dk_ref, dv_ref, dgg_ref, dga_ref, dlr_ref, dw2_ref, db_ref, dgn_ref, dstate):
        b, n, h = pl.program_id(0), pl.program_id(1), pl.program_id(2)

        @pl.when(n == 0)
        def _():
            dstate[h] = jnp.zeros((GLA_DV, GLA_DK), F32)

        @pl.when((b == 0) & (n == 0) & (h == 0))
        def _():
            dw2_ref[...] = jnp.zeros_like(dw2_ref)
            db_ref[...] = jnp.zeros_like(db_ref)
            dgn_ref[...] = jnp.zeros_like(dgn_ref)

        args = (st_ref[0, 0, 0], q_ref[...], k_ref[...], v_ref[...], lr_ref[...], w2_ref[...], b_ref[...],
                gg_ref[...], ga_ref[...], gn_ref[...])
        _, vjp = jax.vjp(_gla_chunk, *args)
        dst, dq, dk, dv, dlr, dw2, db, dgg, dga, dgn = vjp((dm_ref[...].astype(F32), dstate[h]))
        dstate[h] = dst
        dq_ref[...] = dq.astype(dq_ref.dtype)
        dk_ref[...] = dk.astype(dk_ref.dtype)
        dv_ref[...] = dv.astype(dv_ref.dtype)
        dgg_ref[...] = dgg.astype(dgg_ref.dtype)
        dga_ref[...] = dga.astype(dga_ref.dtype)

        @pl.when(h == 0)
        def _():
            dlr_ref[...] = dlr

        @pl.when(h > 0)
        def _():
            dlr_ref[...] += dlr

        dw2_ref[h] += dw2
        db_ref[h] += db
        dgn_ref[...] += dgn

    def rows(b, n, h):
        return b * nc + (nc - 1 - n)

    qk = lambda base: pl.BlockSpec((CHUNK, GLA_DK), lambda b, n, h: (rows(b, n, h), base // GLA_DK + h))
    vv = lambda base: pl.BlockSpec((CHUNK, GLA_DV), lambda b, n, h: (rows(b, n, h), base // GLA_DV + h))
    full = lambda shape: pl.BlockSpec(shape, lambda b, n, h: (0,) * len(shape))
    return pl.pallas_call(
        body, name="gla_bwd", grid=(bsz, nc, GLA_HEADS),
        in_specs=[qk(ZB_GQ), qk(ZB_GK), vv(ZB_GV), vv(ZB_GG), vv(ZB_GA),
                  pl.BlockSpec((CHUNK, LANE), lambda b, n, h: (rows(b, n, h), ZS_LR // LANE)),
                  pl.BlockSpec((LANE, GLA_DK), lambda b, n, h: (0, h)),
                  pl.BlockSpec((1, GLA_DK), lambda b, n, h: (0, h)),
                  pl.BlockSpec((1, GLA_DV), lambda b, n, h: (0, 0)),
                  pl.BlockSpec((1, 1, 1, GLA_DV, GLA_DK), lambda b, n, h: (b, nc - 1 - n, h, 0, 0)),
                  vv(0)],
        out_specs=[qk(0), qk(0), vv(0), vv(0), vv(0),
                   pl.BlockSpec((CHUNK, LANE), lambda b, n, h: (rows(b, n, h), 0)),
                   full((GLA_HEADS, LANE, GLA_DK)), full((GLA_HEADS, 1, GLA_DK)), full((1, GLA_DV))],
        out_shape=[jax.ShapeDtypeStruct((t, GLA_QK), BF16), jax.ShapeDtypeStruct((t, GLA_QK), BF16),
                   jax.ShapeDtypeStruct((t, GLA_V), BF16), jax.ShapeDtypeStruct((t, GLA_V), BF16),
                   jax.ShapeDtypeStruct((t, GLA_V), BF16), jax.ShapeDtypeStruct((t, LANE), F32),
                   jax.ShapeDtypeStruct((GLA_HEADS, LANE, GLA_DK), F32),
                   jax.ShapeDtypeStruct((GLA_HEADS, 1, GLA_DK), F32),
                   jax.ShapeDtypeStruct((1, GLA_DV), F32)],
        scratch_shapes=[pltpu.VMEM((GLA_HEADS, GLA_DV, GLA_DK), F32)],
        compiler_params=_cparams(("arbitrary", "arbitrary", "arbitrary")),
    )(zb, zb, zb, zb, zb, zs, w2p, gla_b, gla_norm, states, dmixed)


DN_HB = 16


def _dn_specs(nc, reverse):
    wide = DN_HB * DN_D

    def rows(b, n, h):
        return b * nc + ((nc - 1 - n) if reverse else n)

    def col(base):
        return pl.BlockSpec((CHUNK, wide), lambda b, n, h: (rows(b, n, h), base // wide + h))

    def fixed(c):
        return pl.BlockSpec((CHUNK, LANE), lambda b, n, h: (rows(b, n, h), c))

    head = pl.BlockSpec((DN_HB, 1, LANE), lambda b, n, h: (h, 0, 0))
    return rows, col, fixed, head


def _lanes(j):
    return slice(j * DN_D, (j + 1) * DN_D)


def _by_head(ref):
    return jnp.stack([ref[:, _lanes(j)] for j in range(DN_HB)], axis=0)


def _dn_fwd(act, zb, zs, alog_b, dtb_b, dn_norm, mix_gla, bsz, nc):
    t = zb.shape[0]
    rows, col, fixed, head = _dn_specs(nc, False)

    def body(q_ref, k_ref, v_ref, za_ref, zb_ref, al_ref, dt_ref, gz_ref, gb_ref, dn_ref, mg_ref,
             o_ref, st_ref, state):
        n, hb = pl.program_id(1), pl.program_id(2)
        heads = pl.ds(hb * DN_HB, DN_HB)
        s = jnp.where(n > 0, state[heads], 0.0)
        st_ref[0, 0] = s
        res, s_new = _dn_chunk(s, _by_head(q_ref), _by_head(k_ref), _by_head(v_ref), za_ref[...], zb_ref[...],
                               al_ref[...], dt_ref[...], _by_head(gz_ref), _by_head(gb_ref), dn_ref[...],
                               hb * DN_HB, False)
        for j in range(DN_HB):
            o_ref[:, _lanes(j)] = (res[j] + mg_ref[:, _lanes(j)]).astype(o_ref.dtype)
        state[heads] = s_new

    return pl.pallas_call(
        body, name="dn_fwd", grid=(bsz, nc, DN_HEADS // DN_HB),
        in_specs=[col(0), col(DN_HEADS * DN_D), col(2 * DN_HEADS * DN_D),
                  fixed(ZS_DA // LANE), fixed(ZS_DB // LANE), head, head,
                  col(ZB_DZ), col(ZB_GB), pl.BlockSpec((1, DN_D), lambda b, n, h: (0, 0)), col(0)],
        out_specs=[col(0), pl.BlockSpec((1, 1, DN_HB, DN_D, DN_D), lambda b, n, h: (b, n, h, 0, 0))],
        out_shape=[jax.ShapeDtypeStruct((t, D_MODEL), BF16),
                   jax.ShapeDtypeStruct((bsz, nc, DN_HEADS, DN_D, DN_D), F32)],
        scratch_shapes=[pltpu.VMEM((DN_HEADS, DN_D, DN_D), F32)],
        compiler_params=_cparams(("arbitrary", "arbitrary", "arbitrary")),
    )(act, act, act, zs, zs, alog_b, dtb_b, zb, zb, dn_norm, mix_gla)


def _dn_bwd(act, zb, zs, alog_b, dtb_b, dn_norm, states, dmixed, bsz, nc):
    t = zb.shape[0]
    rows, col, fixed, head = _dn_specs(nc, True)

    def body(q_ref, k_ref, v_ref, za_ref, zb_ref, al_ref, dt_ref, gz_ref, gb_ref, dn_ref, st_ref, dm_ref,
             dq_ref, dk_ref, dv_ref, dza_ref, dzb_ref, dgz_ref, dgb_ref, dal_ref, ddt_ref, ddn_ref, dstate):
        b, n, hb = pl.program_id(0), pl.program_id(1), pl.program_id(2)

        @pl.when((b == 0) & (n == 0) & (hb == 0))
        def _():
            dal_ref[...] = jnp.zeros_like(dal_ref)
            ddt_ref[...] = jnp.zeros_like(ddt_ref)
            ddn_ref[...] = jnp.zeros_like(ddn_ref)

        heads = pl.ds(hb * DN_HB, DN_HB)
        fn = functools.partial(_dn_chunk, h0=hb * DN_HB, differentiable=True)
        _, vjp = jax.vjp(fn, st_ref[0, 0], _by_head(q_ref), _by_head(k_ref), _by_head(v_ref), za_ref[...],
                         zb_ref[...], al_ref[...], dt_ref[...], _by_head(gz_ref), _by_head(gb_ref), dn_ref[...])
        ds_in = jnp.where(n > 0, dstate[heads], 0.0)
        ds, dq, dk, dv, dza, dzb, dal, ddt, dgz, dgb, ddn = vjp((_by_head(dm_ref).astype(F32), ds_in))
        dstate[heads] = ds
        for j in range(DN_HB):
            dq_ref[:, _lanes(j)] = dq[j].astype(dq_ref.dtype)
            dk_ref[:, _lanes(j)] = dk[j].astype(dk_ref.dtype)
            dv_ref[:, _lanes(j)] = dv[j].astype(dv_ref.dtype)
            dgz_ref[:, _lanes(j)] = dgz[j].astype(dgz_ref.dtype)
            dgb_ref[:, _lanes(j)] = dgb[j].astype(dgb_ref.dtype)
        dal_ref[heads] += dal
        ddt_ref[heads] += ddt

        @pl.when(hb == 0)
        def _():
            dza_ref[...] = dza
            dzb_ref[...] = dzb

        @pl.when(hb > 0)
        def _():
            dza_ref[...] += dza
            dzb_ref[...] += dzb

        ddn_ref[...] += ddn

    full = lambda shape: pl.BlockSpec(shape, lambda b, n, h: (0,) * len(shape))
    return pl.pallas_call(
        body, name="dn_bwd", grid=(bsz, nc, DN_HEADS // DN_HB),
        in_specs=[col(0), col(DN_HEADS * DN_D), col(2 * DN_HEADS * DN_D),
                  fixed(ZS_DA // LANE), fixed(ZS_DB // LANE), head, head,
                  col(ZB_DZ), col(ZB_GB), pl.BlockSpec((1, DN_D), lambda b, n, h: (0, 0)),
                  pl.BlockSpec((1, 1, DN_HB, DN_D, DN_D), lambda b, n, h: (b, nc - 1 - n, h, 0, 0)), col(0)],
        out_specs=[col(0), col(0), col(0), fixed(0), fixed(0), col(0), col(0),
                   full((DN_HEADS, 1, LANE)), full((DN_HEADS, 1, LANE)), full((1, DN_D))],
        out_shape=[jax.ShapeDtypeStruct((t, D_MODEL), F32), jax.ShapeDtypeStruct((t, D_MODEL), F32),
                   jax.ShapeDtypeStruct((t, D_MODEL), F32),
                   jax.ShapeDtypeStruct((t, LANE), F32), jax.ShapeDtypeStruct((t, LANE), F32),
                   jax.ShapeDtypeStruct((t, D_MODEL), BF16), jax.ShapeDtypeStruct((t, D_MODEL), BF16),
                   jax.ShapeDtypeStruct((DN_HEADS, 1, LANE), F32), jax.ShapeDtypeStruct((DN_HEADS, 1, LANE), F32),
                   jax.ShapeDtypeStruct((1, DN_D), F32)],
        scratch_shapes=[pltpu.VMEM((DN_HEADS, DN_D, DN_D), F32)],
        compiler_params=_cparams(("arbitrary", "arbitrary", "arbitrary")),
    )(act, act, act, zs, zs, alog_b, dtb_b, zb, zb, dn_norm, states, dmixed)


MM_VMEM_BUDGET = 40 * 1024 * 1024
MM_TILE_PREF = (1024, 1024, 2048)


def _divisor_tile(n, cap):
    if n <= cap:
        return n
    for c in range(cap - cap % LANE, 0, -LANE):
        if n % c == 0:
            return c
    return n


def _mm_tiles(m, n, kd, a_bytes, b_bytes, mn_bytes):
    tm, tn, tk = (_divisor_tile(d, c) for d, c in zip((m, n, kd), MM_TILE_PREF))

    def need(tm, tn, tk):
        acc = 0 if tk == kd else 4 * tm * tn
        return 2 * (tm * tk * a_bytes + tk * tn * b_bytes + tm * tn * mn_bytes) + acc + 4 * tm * tn

    while need(tm, tn, tk) > MM_VMEM_BUDGET:
        if tk > 512 and tk * max(tm * a_bytes, tn * b_bytes) >= tm * tn * mn_bytes:
            tk = _divisor_tile(kd, tk // 2)
        elif tn >= tm and tn > LANE:
            tn = _divisor_tile(n, tn // 2)
        else:
            tm = _divisor_tile(m, tm // 2)
    return tm, tn, tk


def _mm(a, b, *, ta=False, tb=False, out_dtypes=(F32,), epilogue=None, extras=(), name):
    m, kd = (a.shape[1], a.shape[0]) if ta else a.shape
    n = b.shape[0] if tb else b.shape[1]
    mn_bytes = sum(e.dtype.itemsize for e in extras) + sum(jnp.dtype(dt).itemsize for dt in out_dtypes)
    tm, tn, tk = _mm_tiles(m, n, kd, a.dtype.itemsize, b.dtype.itemsize, mn_bytes)
    nk = kd // tk
    n_ex = len(extras)
    dims = (((0,) if ta else (1,), (1,) if tb else (0,)), ((), ()))

    def finish(acc, ex_refs, out_refs):
        outs = (acc,) if epilogue is None else epilogue(acc, *[r[...] for r in ex_refs])
        for r, o in zip(out_refs, outs):
            r[...] = o.astype(r.dtype)

    def partial_product(a_ref, b_ref):
        return lax.dot_general(a_ref[...].astype(BF16), b_ref[...].astype(BF16), dims, preferred_element_type=F32)

    def body_single(*refs):
        finish(partial_product(refs[0], refs[1]), refs[2:2 + n_ex], refs[2 + n_ex:])

    def body_acc(*refs):
        acc = refs[-1]
        k = pl.program_id(2)

        @pl.when(k == 0)
        def _():
            acc[...] = partial_product(refs[0], refs[1])

        @pl.when(k > 0)
        def _():
            acc[...] += partial_product(refs[0], refs[1])

        @pl.when(k == nk - 1)
        def _():
            finish(acc[...], refs[2:2 + n_ex], refs[2 + n_ex:-1])

    a_spec = pl.BlockSpec((tk, tm), lambda i, j, k: (k, i)) if ta else pl.BlockSpec((tm, tk), lambda i, j, k: (i, k))
    b_spec = pl.BlockSpec((tn, tk), lambda i, j, k: (j, k)) if tb else pl.BlockSpec((tk, tn), lambda i, j, k: (k, j))
    mn_spec = pl.BlockSpec((tm, tn), lambda i, j, k: (i, j))
    outs = pl.pallas_call(
        body_single if nk == 1 else body_acc, name=name, grid=(m // tm, n // tn, nk),
        in_specs=[a_spec, b_spec] + [mn_spec] * n_ex,
        out_specs=[mn_spec] * len(out_dtypes),
        out_shape=[jax.ShapeDtypeStruct((m, n), dt) for dt in out_dtypes],
        scratch_shapes=[] if nk == 1 else [pltpu.VMEM((tm, tn), F32)],
        compiler_params=_cparams(("parallel", "parallel", "arbitrary")),
    )(a, b, *extras)
    return outs[0] if len(out_dtypes) == 1 else outs


ROW_BLOCK = 256


def _row_spec(width=D_MODEL):
    return pl.BlockSpec((ROW_BLOCK, width), lambda i: (i, 0))


def _vec_spec(width=D_MODEL):
    return pl.BlockSpec((1, width), lambda i: (0, 0))


def _rms_fwd(x, g, name):
    def body(x_ref, g_ref, h_ref):
        xf = x_ref[...]
        h_ref[...] = (xf * lax.rsqrt(jnp.mean(xf * xf, axis=-1, keepdims=True) + EPS) * g_ref[...]).astype(BF16)

    t = x.shape[0]
    return pl.pallas_call(
        body, name=name, grid=(t // ROW_BLOCK,), in_specs=[_row_spec(), _vec_spec()], out_specs=_row_spec(),
        out_shape=jax.ShapeDtypeStruct((t, D_MODEL), BF16), compiler_params=_cparams(("parallel",)),
    )(x, g)


def _rms_bwd_math(xf, g, dh):
    rstd = lax.rsqrt(jnp.mean(xf * xf, axis=-1, keepdims=True) + EPS)
    xhat = xf * rstd
    dxhat = dh * g
    dx = rstd * (dxhat - xhat * jnp.mean(dxhat * xhat, axis=-1, keepdims=True))
    dg = jnp.sum(dh * xhat, axis=0, keepdims=True)
    return dx, dg


def _rms_bwd(x, g, dh, dres, name):
    def body(x_ref, g_ref, dh_ref, dres_ref, dx_ref, dg_ref):
        dx, dg = _rms_bwd_math(x_ref[...], g_ref[...], dh_ref[...].astype(F32))
        dx_ref[...] = dres_ref[...] + dx

        @pl.when(pl.program_id(0) == 0)
        def _():
            dg_ref[...] = jnp.zeros_like(dg_ref)

        dg_ref[...] += dg

    t = x.shape[0]
    return pl.pallas_call(
        body, name=name, grid=(t // ROW_BLOCK,),
        in_specs=[_row_spec(), _vec_spec(), _row_spec(), _row_spec()], out_specs=[_row_spec(), _vec_spec()],
        out_shape=[jax.ShapeDtypeStruct((t, D_MODEL), F32), jax.ShapeDtypeStruct((1, D_MODEL), F32)],
        compiler_params=_cparams(("arbitrary",)),
    )(x, g, dh, dres)


def _loss_head(x3, g, target):
    def body(x_ref, g_ref, t_ref, dx_ref, dg_ref, loss_ref):
        xf, gg = x_ref[...], g_ref[...]
        rstd = lax.rsqrt(jnp.mean(xf * xf, axis=-1, keepdims=True) + EPS)
        err = xf * rstd * gg - t_ref[...]
        dx, dg = _rms_bwd_math(xf, gg, err * (1.0 / D_MODEL))
        dx_ref[...] = dx

        @pl.when(pl.program_id(0) == 0)
        def _():
            dg_ref[...] = jnp.zeros_like(dg_ref)
            loss_ref[...] = jnp.zeros_like(loss_ref)

        dg_ref[...] += dg
        part = jnp.sum(jnp.sum(err * err, axis=-1, keepdims=True), axis=0, keepdims=True) * (0.5 / D_MODEL)
        loss_ref[...] += jnp.broadcast_to(part, loss_ref.shape)

    t = x3.shape[0]
    return pl.pallas_call(
        body, name="loss_head", grid=(t // ROW_BLOCK,),
        in_specs=[_row_spec(), _vec_spec(), _row_spec()], out_specs=[_row_spec(), _vec_spec(), _vec_spec(LANE)],
        out_shape=[jax.ShapeDtypeStruct((t, D_MODEL), F32), jax.ShapeDtypeStruct((1, D_MODEL), F32),
                   jax.ShapeDtypeStruct((1, LANE), F32)],
        compiler_params=_cparams(("arbitrary",)),
    )(x3, g, target)


def _ple_bwd(dx3, gpre, pp):
    def body(dx_ref, gp_ref, pp_ref, dgp_ref, dpp_ref):
        dx, sg = dx_ref[...], _sigmoid(gp_ref[...])
        dpp_ref[...] = (dx * sg).astype(BF16)
        dgp_ref[...] = (dx * pp_ref[...] * sg * (1.0 - sg)).astype(BF16)

    t = dx3.shape[0]
    return pl.pallas_call(
        body, name="ple_bwd", grid=(t // ROW_BLOCK,), in_specs=[_row_spec()] * 3, out_specs=[_row_spec()] * 2,
        out_shape=[jax.ShapeDtypeStruct((t, D_MODEL), BF16)] * 2, compiler_params=_cparams(("parallel",)),
    )(dx3, gpre, pp)


CONV_COLS = 256


def _shift_down(x, s):
    if s == 0:
        return x
    return jnp.where(_iota2(x.shape, 0) >= s, pltpu.roll(x, s, 0), 0.0)


def _shift_up(x, s):
    if s == 0:
        return x
    rows = x.shape[0]
    return jnp.where(_iota2(x.shape, 0) < rows - s, pltpu.roll(x, rows - s, 0), 0.0)


def _conv_pre(xf, w):
    return sum(_shift_down(xf, DN_CONV - 1 - j) * w[j:j + 1, :] for j in range(DN_CONV))


def _conv_fwd(zb, conv_w, bsz, seq):
    def body(x_ref, w_ref, y_ref):
        y_ref[...] = _silu(_conv_pre(x_ref[...], w_ref[...]))

    nblk = DN_QKV // CONV_COLS
    return pl.pallas_call(
        body, name="conv_fwd", grid=(bsz, nblk),
        in_specs=[pl.BlockSpec((seq, CONV_COLS), lambda b, j: (b, ZB_DQKV // CONV_COLS + j)),
                  pl.BlockSpec((DN_CONV, CONV_COLS), lambda b, j: (0, j))],
        out_specs=pl.BlockSpec((seq, CONV_COLS), lambda b, j: (b, j)),
        out_shape=jax.ShapeDtypeStruct((bsz * seq, DN_QKV), F32),
        compiler_params=_cparams(("parallel", "parallel")),
    )(zb, conv_w)


def _conv_bwd(zb, conv_w, dact, bsz, seq):
    def body(x_ref, w_ref, dy_ref, dx_ref, dw_ref):
        xf, w = x_ref[...], w_ref[...]
        c = _conv_pre(xf, w)
        sg = _sigmoid(c)
        dc = dy_ref[...].astype(F32) * sg * (1.0 + c * (1.0 - sg))
        dx = sum(_shift_up(dc, DN_CONV - 1 - j) * w[j:j + 1, :] for j in range(DN_CONV))
        dx_ref[...] = dx.astype(BF16)
        dw = jnp.concatenate(
            [jnp.sum(dc * _shift_down(xf, DN_CONV - 1 - j), axis=0, keepdims=True) for j in range(DN_CONV)], axis=0)

        @pl.when(pl.program_id(1) == 0)
        def _():
            dw_ref[...] = jnp.zeros_like(dw_ref)

        dw_ref[...] += dw

    nblk = DN_QKV // CONV_COLS
    return pl.pallas_call(
        body, name="conv_bwd", grid=(nblk, bsz),
        in_specs=[pl.BlockSpec((seq, CONV_COLS), lambda j, b: (b, ZB_DQKV // CONV_COLS + j)),
                  pl.BlockSpec((DN_CONV, CONV_COLS), lambda j, b: (0, j)),
                  pl.BlockSpec((seq, CONV_COLS), lambda j, b: (b, j))],
        out_specs=[pl.BlockSpec((seq, CONV_COLS), lambda j, b: (b, j)),
                   pl.BlockSpec((DN_CONV, CONV_COLS), lambda j, b: (0, j))],
        out_shape=[jax.ShapeDtypeStruct((bsz * seq, DN_QKV), BF16), jax.ShapeDtypeStruct((DN_CONV, DN_QKV), F32)],
        compiler_params=_cparams(("parallel", "arbitrary")),
    )(zb, conv_w, dact)


MESH_IDS = pl.DeviceIdType.MESH
ANY_SPEC = pl.BlockSpec(memory_space=pl.ANY)


def _all_gather(x, name):
    def body(x_ref, out_ref, send_sems, recv_sems, local_sem):
        mx, my, mc = lax.axis_index("x"), lax.axis_index("y"), lax.axis_index("c")
        me, sibling = (mx, my, mc), (mx, my, 1 - mc)
        chips = [(1 - mx, my), (mx, 1 - my), (1 - mx, 1 - my)]

        def slot(px, py, pc):
            return out_ref.at[4 * px + 2 * py + pc]

        def copy(k, block, to, src=None):
            return pltpu.make_async_remote_copy(
                src_ref=slot(*block) if src is None else src, dst_ref=slot(*block),
                send_sem=send_sems.at[k], recv_sem=recv_sems.at[k], device_id=to, device_id_type=MESH_IDS)

        mine = pltpu.make_async_copy(x_ref, slot(*me), local_sem)
        mine.start()
        first = [copy(0, me, sibling, src=x_ref)]
        first += [copy(1 + j, me, (*chip, mc), src=x_ref) for j, chip in enumerate(chips)]
        for cp in first:
            cp.start()
        passed = [copy(4 + j, (*chip, mc), sibling) for j, chip in enumerate(chips)]
        for j, chip in enumerate(chips):
            copy(1 + j, (*chip, mc), me).wait_recv()
            passed[j].start()
        copy(0, sibling, me).wait_recv()
        for j, chip in enumerate(chips):
            copy(4 + j, (*chip, 1 - mc), me).wait_recv()
        for cp in first + passed:
            cp.wait_send()
        mine.wait()

    return pl.pallas_call(
        body, name=name, out_shape=jax.ShapeDtypeStruct((N_DEV,) + x.shape, x.dtype),
        in_specs=[ANY_SPEC], out_specs=ANY_SPEC,
        scratch_shapes=[pltpu.SemaphoreType.DMA((7,)), pltpu.SemaphoreType.DMA((7,)), pltpu.SemaphoreType.DMA],
    )(x)


def _all_to_all(x, name):
    def body(x_ref, out_ref, send_sems, recv_sems, local_sem):
        mx, my, mc = lax.axis_index("x"), lax.axis_index("y"), lax.axis_index("c")
        me = 4 * mx + 2 * my + mc

        def peer(k):
            return (mx ^ ((k >> 2) & 1), my ^ ((k >> 1) & 1), mc ^ (k & 1))

        def copy(k):
            px, py, pc = peer(k)
            return pltpu.make_async_remote_copy(
                src_ref=x_ref.at[4 * px + 2 * py + pc], dst_ref=out_ref.at[me],
                send_sem=send_sems.at[k - 1], recv_sem=recv_sems.at[k - 1],
                device_id=(px, py, pc), device_id_type=MESH_IDS)

        def landed(k):
            px, py, pc = peer(k)
            return pltpu.make_async_remote_copy(
                src_ref=x_ref.at[me], dst_ref=out_ref.at[4 * px + 2 * py + pc],
                send_sem=send_sems.at[k - 1], recv_sem=recv_sems.at[k - 1],
                device_id=(px, py, pc), device_id_type=MESH_IDS)

        mine = pltpu.make_async_copy(x_ref.at[me], out_ref.at[me], local_sem)
        mine.start()
        sends = [copy(k) for k in range(1, N_DEV)]
        for cp in sends:
            cp.start()
        for k in range(1, N_DEV):
            landed(k).wait_recv()
        for cp in sends:
            cp.wait_send()
        mine.wait()

    return pl.pallas_call(
        body, name=name, out_shape=jax.ShapeDtypeStruct(x.shape, x.dtype),
        in_specs=[ANY_SPEC], out_specs=ANY_SPEC,
        scratch_shapes=[pltpu.SemaphoreType.DMA((7,)), pltpu.SemaphoreType.DMA((7,)), pltpu.SemaphoreType.DMA],
    )(x)


def _adamw_math(w, g, m, v):
    m = ADAM_B1 * m + (1.0 - ADAM_B1) * g
    v = ADAM_B2 * v + (1.0 - ADAM_B2) * jnp.square(g)
    m_hat = m / (1.0 - ADAM_B1 ** ADAM_STEP)
    v_hat = v / (1.0 - ADAM_B2 ** ADAM_STEP)
    delta = -ADAM_LR * (m_hat / (jnp.sqrt(v_hat) + ADAM_EPS) + ADAM_WD * w)
    return delta, m, v


ADAM_ROWS = 128


def _adamw_reduce(w, m, v, parts, row0, name):
    rows, cols = w.shape
    tr = ADAM_ROWS if rows % ADAM_ROWS == 0 else rows
    r0 = row0 // tr

    def body(w_ref, m_ref, v_ref, *refs):
        part_refs, (g_ref, d_ref, nm_ref, nv_ref) = refs[:N_DEV], refs[N_DEV:]
        g = part_refs[0][...].astype(F32)
        for r in part_refs[1:]:
            g = g + r[...].astype(F32)
        delta, nm, nv = _adamw_math(w_ref[...], g, m_ref[...], v_ref[...])
        g_ref[...] = g
        d_ref[...] = delta
        nm_ref[...] = nm
        nv_ref[...] = nv

    blk = pl.BlockSpec((tr, cols), lambda i: (i, 0))
    part_specs = [pl.BlockSpec((None, tr, cols), functools.partial(lambda i, k: (k, r0 + i, 0), k=k))
                  for k in range(N_DEV)]
    return pl.pallas_call(
        body, name=name, grid=(rows // tr,), in_specs=[blk] * 3 + part_specs, out_specs=[blk] * 4,
        out_shape=[jax.ShapeDtypeStruct(w.shape, F32)] * 4, compiler_params=_cparams(("parallel",)),
    )(w, m, v, *([parts] * N_DEV))


def _small_reduce(gathered, lane_sum_from):
    r = gathered.shape[1]

    def body(g_ref, o_ref):
        g = g_ref[0]
        for k in range(1, N_DEV):
            g = g + g_ref[k]
        tot = jnp.broadcast_to(jnp.sum(g, axis=-1, keepdims=True), g.shape)
        o_ref[...] = jnp.where(_iota2(g.shape, 0) >= lane_sum_from, tot, g)

    return pl.pallas_call(body, name="small_grad_reduce", out_shape=jax.ShapeDtypeStruct((r, LANE), F32))(gathered)


def _adamw_small(w, m, v, g):
    def body(w_ref, m_ref, v_ref, g_ref, d_ref, nm_ref, nv_ref):
        d_ref[...], nm_ref[...], nv_ref[...] = _adamw_math(w_ref[...], g_ref[...], m_ref[...], v_ref[...])

    return pl.pallas_call(body, name="adamw_small", out_shape=[jax.ShapeDtypeStruct(w.shape, F32)] * 3)(w, m, v, g)


def _pack_rows(arrays):
    rows = [jnp.pad(a.reshape(-1), (0, -a.size % LANE)).reshape(-1, LANE) for a in arrays]
    out = jnp.concatenate(rows, axis=0)
    return jnp.pad(out, ((0, -out.shape[0] % 8), (0, 0)))


def _unpack_rows(packed, shapes):
    out, r = [], 0
    for shp in shapes:
        size = math.prod(shp)
        nrows = -(-size // LANE)
        out.append(packed[r:r + nrows].reshape(-1)[:size].reshape(shp))
        r += nrows
    return out


def _add_residual(acc, res):
    return (res + acc,)


def _local_step(x, p, target, w):
    bsz, seq, _ = x.shape
    t, nc = bsz * seq, seq // CHUNK
    x0, p2, tgt = x.reshape(t, D_MODEL), p.reshape(t, PLE_DIM), target.reshape(t, D_MODEL)

    h = _rms_fwd(x0, w["g_mix"], "rms_mix")
    zb = _mm(h, w["wb"], name="in_proj")
    zs = _mm(h, w["ws"], name="in_proj_gates")
    act = _conv_fwd(zb, w["conv"], bsz, seq)
    mix_gla, gla_states = _gla_fwd(zb, zs, w["w2p"], w["gla_b"], w["gla_norm"], bsz, nc)
    mixed, dn_states = _dn_fwd(act, zb, zs, w["alog_b"], w["dtb_b"], w["dn_norm"], mix_gla, bsz, nc)
    x1 = _mm(mixed, w["w_out"], epilogue=_add_residual, extras=(x0,), name="out_proj")
    h2 = _rms_fwd(x1, w["g_mlp"], "rms_mlp")
    u, a = _mm(h2, w["w_up"], out_dtypes=(BF16, BF16), name="mlp_up",
               epilogue=lambda acc: (acc, jnp.square(jnp.maximum(acc, 0.0))))
    x2 = _mm(a, w["w_down"], epilogue=_add_residual, extras=(x1,), name="mlp_down")
    h3 = _rms_fwd(x2, w["g_ple"], "rms_ple")
    pp = _mm(p2, w["w_pp"], name="ple_proj")
    gpre, x3 = _mm(h3, w["w_pg"], out_dtypes=(F32, F32), extras=(x2, pp), name="ple_gate",
                   epilogue=lambda acc, res, proj: (acc, res + _sigmoid(acc) * proj))
    dx3, dg_final, loss = _loss_head(x3, w["g_final"], tgt)

    dgpre, dpp = _ple_bwd(dx3, gpre, pp)
    dw_pp = _mm(p2, dpp, ta=True, out_dtypes=(BF16,), name="d_w_ple_proj")
    dw_pg = _mm(h3, dgpre, ta=True, out_dtypes=(BF16,), name="d_w_ple_gate")
    dh3 = _mm(dgpre, w["w_pg"], tb=True, name="d_h_ple")
    dx2, dg_ple = _rms_bwd(x2, w["g_ple"], dh3, dx3, "rms_ple_bwd")
    du = _mm(dx2, w["w_down"], tb=True, out_dtypes=(BF16,), extras=(u,), name="d_mlp_hidden",
             epilogue=lambda acc, uu: (acc * (2.0 * jnp.maximum(uu.astype(F32), 0.0)),))
    dw_down = _mm(a, dx2, ta=True, out_dtypes=(BF16,), name="d_w_down")
    dw_up = _mm(h2, du, ta=True, out_dtypes=(BF16,), name="d_w_up")
    dh2 = _mm(du, w["w_up"], tb=True, name="d_h_mlp")
    dx1, dg_mlp = _rms_bwd(x1, w["g_mlp"], dh2, dx2, "rms_mlp_bwd")
    dmixed = _mm(dx1, w["w_out"], tb=True, out_dtypes=(BF16,), name="d_mixed")
    dw_out = _mm(mixed, dx1, ta=True, out_dtypes=(BF16,), name="d_w_out")

    (ddq, ddk, ddv, dza, dzb_, dgz, dgb, dal, ddt, ddn) = _dn_bwd(
        act, zb, zs, w["alog_b"], w["dtb_b"], w["dn_norm"], dn_states, dmixed, bsz, nc)
    (gdq, gdk, gdv, dgg, dga, dlr, dw2, dgla_b, dgla_norm) = _gla_bwd(
        zb, zs, w["w2p"], w["gla_b"], w["gla_norm"], gla_states, dmixed, bsz, nc)
    dqkv, dconv = _conv_bwd(zb, w["conv"], jnp.concatenate([ddq, ddk, ddv], axis=1), bsz, seq)
    dzb = jnp.concatenate([gdq, gdk, gdv, dgg, dqkv, dgz, dga, dgb], axis=1)
    dzs = jnp.concatenate([dlr, dza, dzb_], axis=1)
    dh_gates = _mm(dzs, w["ws"], tb=True, name="d_h_mix_gates")
    dh = _mm(dzb, w["wb"], tb=True, epilogue=_add_residual, extras=(dh_gates,), name="d_h_mix")
    dwb = _mm(h, dzb, ta=True, out_dtypes=(BF16,), name="d_w_in")
    dws = _mm(h, dzs, ta=True, out_dtypes=(BF16,), name="d_w_in_gates")
    gx, dg_mix = _rms_bwd(x0, w["g_mix"], dh, dx1, "rms_mix_bwd")

    dw_in = jnp.concatenate([dwb[:, :WI_LR], dws[:, ZS_LR:ZS_LR + GLA_LOWRANK],
                             dwb[:, ZB_DQKV:ZB_GA], dws[:, ZS_DA:ZS_DA + DN_HEADS],
                             dws[:, ZS_DB:ZS_DB + DN_HEADS], dwb[:, ZB_GA:]], axis=1)
    dgla_w2 = dw2[:, :GLA_LOWRANK, :].transpose(1, 0, 2).reshape(GLA_LOWRANK, GLA_QK)
    return dict(
        loss=loss[0, 0], grad_x=gx.reshape(x.shape), w_in=dw_in, w_up=dw_up, w_out=dw_out, w_ple_gate=dw_pg,
        w_down=dw_down, w_ple_proj=dw_pp,
        g_mix=dg_mix, gla_b=dgla_b.reshape(1, GLA_QK), gla_norm=dgla_norm, dn_norm=ddn, g_mlp=dg_mlp, g_ple=dg_ple,
        g_final=dg_final, gla_w2=dgla_w2, dn_conv=dconv,
        a_log_lanes=dal.reshape(DN_HEADS, LANE), dt_bias_lanes=ddt.reshape(DN_HEADS, LANE))


def _full_weights(g_mix, w_in, gla_w2, gla_b, gla_norm, dn_conv, dn_a_log, dn_dt_bias, dn_norm, w_out, g_mlp,
                  w_up, w_down, g_ple, w_ple_gate, w_ple_proj, g_final):
    cols = _all_gather(jnp.concatenate([w_in[0].astype(BF16), w_up[0].astype(BF16)], axis=1), "gather_w_cols")
    w_in_full = cols[:, :, :D_IN_SHARD].transpose(1, 0, 2).reshape(D_MODEL, D_IN)
    w_up_full = cols[:, :, D_IN_SHARD:].transpose(1, 0, 2).reshape(D_MODEL, D_FF)
    rows = _all_gather(jnp.concatenate([w_out[0], w_ple_gate[0], w_down[0]], axis=0).astype(BF16), "gather_w_rows")
    r_out = D_MODEL // N_DEV
    w_pp = _all_gather(w_ple_proj[0].astype(BF16), "gather_w_ple_proj")
    small = _all_gather(_pack_rows([gla_w2[0], dn_conv[0]]), "gather_w_small")
    n_w2 = GLA_LOWRANK * GLA_QK // N_DEV // LANE
    n_cv = DN_CONV * DN_QKV // N_DEV // LANE
    w2 = small[:, :n_w2].reshape(N_DEV, GLA_LOWRANK, GLA_QK // N_DEV).transpose(1, 0, 2).reshape(GLA_LOWRANK, GLA_QK)
    conv = small[:, n_w2:n_w2 + n_cv].reshape(N_DEV, DN_CONV, DN_QKV // N_DEV).transpose(1, 0, 2).reshape(DN_CONV, DN_QKV)

    def lane_pad(wcols):
        return jnp.pad(wcols, ((0, 0), (0, LANE - wcols.shape[1])))

    return dict(
        wb=jnp.concatenate([w_in_full[:, :WI_LR], w_in_full[:, WI_DQKV:WI_DA], w_in_full[:, WI_GA:]], axis=1),
        ws=jnp.concatenate([lane_pad(w_in_full[:, WI_LR:WI_DQKV]), lane_pad(w_in_full[:, WI_DA:WI_DB]),
                            lane_pad(w_in_full[:, WI_DB:WI_GA])], axis=1),
        w_up=w_up_full,
        w_out=rows[:, :r_out].reshape(D_MODEL, D_MODEL),
        w_pg=rows[:, r_out:2 * r_out].reshape(D_MODEL, D_MODEL),
        w_down=rows[:, 2 * r_out:].reshape(D_FF, D_MODEL),
        w_pp=w_pp.transpose(1, 0, 2).reshape(PLE_DIM, D_MODEL),
        w2p=jnp.pad(w2, ((0, LANE - GLA_LOWRANK), (0, 0))), conv=conv,
        alog_b=jnp.broadcast_to(dn_a_log[0][:, None, None], (DN_HEADS, 1, LANE)),
        dtb_b=jnp.broadcast_to(dn_dt_bias[0][:, None, None], (DN_HEADS, 1, LANE)),
        g_mix=g_mix, gla_b=gla_b, gla_norm=gla_norm, dn_norm=dn_norm, g_mlp=g_mlp, g_ple=g_ple,
        g_final=g_final.reshape(1, D_MODEL))


def kernel(x, p, g_mix, w_in, gla_w2, gla_b, gla_norm, dn_conv, dn_a_log, dn_dt_bias, dn_norm, w_out, g_mlp, w_up, w_down, g_ple, w_ple_gate, w_ple_proj, g_final, loss_target, m_g_mix, m_w_in, m_gla_w2, m_gla_b, m_gla_norm, m_dn_conv, m_dn_a_log, m_dn_dt_bias, m_dn_norm, m_w_out, m_g_mlp, m_w_up, m_w_down, m_g_ple, m_w_ple_gate, m_w_ple_proj, m_g_final, v_g_mix, v_w_in, v_gla_w2, v_gla_b, v_gla_norm, v_dn_conv, v_dn_a_log, v_dn_dt_bias, v_dn_norm, v_w_out, v_g_mlp, v_w_up, v_w_down, v_g_ple, v_w_ple_gate, v_w_ple_proj, v_g_final):
    names = ["g_mix", "w_in", "gla_w2", "gla_b", "gla_norm", "dn_conv", "dn_a_log", "dn_dt_bias", "dn_norm", "w_out",
             "g_mlp", "w_up", "w_down", "g_ple", "w_ple_gate", "w_ple_proj", "g_final"]
    ws = dict(zip(names, (g_mix, w_in, gla_w2, gla_b, gla_norm, dn_conv, dn_a_log, dn_dt_bias, dn_norm, w_out, g_mlp,
                          w_up, w_down, g_ple, w_ple_gate, w_ple_proj, g_final)))
    ms = dict(zip(names, (m_g_mix, m_w_in, m_gla_w2, m_gla_b, m_gla_norm, m_dn_conv, m_dn_a_log, m_dn_dt_bias,
                          m_dn_norm, m_w_out, m_g_mlp, m_w_up, m_w_down, m_g_ple, m_w_ple_gate, m_w_ple_proj,
                          m_g_final)))
    vs = dict(zip(names, (v_g_mix, v_w_in, v_gla_w2, v_gla_b, v_gla_norm, v_dn_conv, v_dn_a_log, v_dn_dt_bias,
                          v_dn_norm, v_w_out, v_g_mlp, v_w_up, v_w_down, v_g_ple, v_w_ple_gate, v_w_ple_proj,
                          v_g_final)))
    me = 4 * lax.axis_index("x") + 2 * lax.axis_index("y") + lax.axis_index("c")

    full = _full_weights(*[ws[n] for n in names])
    r = _local_step(x, p[0], loss_target, full)
    loss = lax.psum(r["loss"], ("x", "y", "c"))

    grads, deltas, new_m, new_v = {}, {}, {}, {}

    def big(name, parts, row0=0):
        g, d, nm, nv = _adamw_reduce(ws[name][0], ms[name][0], vs[name][0], parts, row0, "adamw_" + name)
        grads[name], deltas[name], new_m[name], new_v[name] = g[None], d[None], nm[None], nv[None]

    def by_cols(g):
        return g.reshape(g.shape[0], N_DEV, -1).transpose(1, 0, 2)

    big("w_in", _all_to_all(by_cols(r["w_in"]), "scatter_d_w_in"))
    big("w_up", _all_to_all(by_cols(r["w_up"]), "scatter_d_w_up"))
    big("w_ple_proj", _all_to_all(by_cols(r["w_ple_proj"]), "scatter_d_w_ple_proj"))
    r_out = D_MODEL // N_DEV
    rows = _all_to_all(jnp.concatenate([r["w_out"].reshape(N_DEV, r_out, D_MODEL),
                                        r["w_ple_gate"].reshape(N_DEV, r_out, D_MODEL),
                                        r["w_down"].reshape(N_DEV, D_FF // N_DEV, D_MODEL)], axis=1), "scatter_d_w_rows")
    big("w_out", rows, 0)
    big("w_ple_gate", rows, r_out)
    big("w_down", rows, 2 * r_out)

    vec_names = ["g_mix", "gla_b", "gla_norm", "dn_norm", "g_mlp", "g_ple", "g_final"]
    packed = _pack_rows([r[n] for n in vec_names] + [r["gla_w2"], r["dn_conv"]])
    lane_rows = packed.shape[0]
    packed = jnp.concatenate([packed, r["a_log_lanes"], r["dt_bias_lanes"]], axis=0)
    total = _small_reduce(_all_gather(packed, "gather_small_grads"), lane_rows)
    parts = _unpack_rows(total, [r[n].shape for n in vec_names] + [r["gla_w2"].shape, r["dn_conv"].shape])
    sg = dict(zip(vec_names, parts[:len(vec_names)]))
    sg["g_final"] = sg["g_final"].reshape(D_MODEL)
    sg["gla_w2"] = lax.dynamic_slice_in_dim(parts[-2], me * (GLA_QK // N_DEV), GLA_QK // N_DEV, axis=1)
    sg["dn_conv"] = lax.dynamic_slice_in_dim(parts[-1], me * (DN_QKV // N_DEV), DN_QKV // N_DEV, axis=1)
    sg["dn_a_log"] = total[lane_rows:lane_rows + DN_HEADS, 0]
    sg["dn_dt_bias"] = total[lane_rows + DN_HEADS:lane_rows + 2 * DN_HEADS, 0]
    small_names = vec_names + ["gla_w2", "dn_conv", "dn_a_log", "dn_dt_bias"]
    shapes = [ws[n].shape for n in small_names]
    d_s, m_s, v_s = _adamw_small(_pack_rows([ws[n] for n in small_names]), _pack_rows([ms[n] for n in small_names]),
                                 _pack_rows([vs[n] for n in small_names]), _pack_rows([sg[n] for n in small_names]))
    for n, d, nm, nv in zip(small_names, _unpack_rows(d_s, shapes), _unpack_rows(m_s, shapes), _unpack_rows(v_s, shapes)):
        grads[n], deltas[n], new_m[n], new_v[n] = sg[n].reshape(ws[n].shape), d, nm, nv

    return (loss, r["grad_x"], *[grads[n] for n in names], *[deltas[n] for n in names],
            *[new_m[n] for n in names], *[new_v[n] for n in names])
```

```python
import functools
import math

import jax
import jax.numpy as jnp
from jax import lax
from jax.experimental import pallas as pl
from jax.experimental.pallas import tpu as pltpu

F32 = jnp.float32
BF16 = jnp.bfloat16
HIGHEST = lax.Precision.HIGHEST

N_DEV = 8
D_MODEL = 2048
CHUNK = 64
PLE_DIM = 256
EPS = 1e-6
GLA_HEADS = 4
GLA_DK = 256
GLA_DV = 512
GLA_LOWRANK = 16
GLA_TAU = 16.0
DN_HEADS = 16
DN_D = 128
DN_CONV = 4
D_FF = 4 * D_MODEL
GLA_QK = GLA_HEADS * GLA_DK
GLA_V = GLA_HEADS * GLA_DV
DN_QKV = 3 * DN_HEADS * DN_D
D_IN = 2 * GLA_QK + 2 * GLA_V + GLA_LOWRANK + DN_QKV + D_MODEL + 2 * DN_HEADS + 2 * D_MODEL
D_IN_SHARD = D_IN // N_DEV

ADAM_LR = 0.001
ADAM_B1 = 0.9
ADAM_B2 = 0.999
ADAM_EPS = 1e-08
ADAM_WD = 0.01
ADAM_STEP = 10

LANE = 128
ZB_GQ, ZB_GK, ZB_GV, ZB_GG = 0, 1024, 2048, 4096
ZB_DQKV, ZB_DZ, ZB_GA, ZB_GB = 6144, 12288, 14336, 16384
ZB_W = 18432
ZS_LR, ZS_DA, ZS_DB = 0, 128, 256
ZS_W = 384
WI_LR = 2 * GLA_QK + 2 * GLA_V
WI_DQKV = WI_LR + GLA_LOWRANK
WI_DA = WI_DQKV + DN_QKV + D_MODEL
WI_DB = WI_DA + DN_HEADS
WI_GA = WI_DB + DN_HEADS

VMEM_LIMIT = 56 * 1024 * 1024

NN = (((1,), (0,)), ((), ()))
NT = (((1,), (1,)), ((), ()))
TN = (((0,), (0,)), ((), ()))


def _bdot(a, b, dims):
    return lax.dot_general(a.astype(BF16), b.astype(BF16), dims, preferred_element_type=F32)


def _split3(x):
    hi = x.astype(BF16)
    rest = x - hi.astype(F32)
    mid = rest.astype(BF16)
    return hi, mid, (rest - mid.astype(F32)).astype(BF16)


def _dot01(x, m, dims, x_first):
    m = m.astype(BF16)
    out = None
    for piece in _split3(x):
        d = lax.dot_general(piece, m, dims, preferred_element_type=F32) if x_first else \
            lax.dot_general(m, piece, dims, preferred_element_type=F32)
        out = d if out is None else out + d
    return out


@functools.partial(jax.custom_vjp, nondiff_argnums=(2, 3))
def _pick_dot(x, m, dims, dims_t):
    return _dot01(x, m, dims, True)


def _pick_dot_fwd(x, m, dims, dims_t):
    return _dot01(x, m, dims, True), m


def _pick_dot_bwd(dims, dims_t, m, ct):
    return _dot01(ct, m, dims_t, True), jnp.zeros_like(m)


_pick_dot.defvjp(_pick_dot_fwd, _pick_dot_bwd)


@functools.partial(jax.custom_vjp, nondiff_argnums=(2, 3))
def _left_dot(m, x, dims, dims_t):
    return _dot01(x, m, dims, False)


def _left_dot_fwd(m, x, dims, dims_t):
    return _dot01(x, m, dims, False), m


def _left_dot_bwd(dims, dims_t, m, ct):
    return jnp.zeros_like(m), _dot01(ct, m, dims_t, False)


_left_dot.defvjp(_left_dot_fwd, _left_dot_bwd)


def _dot3(a, b, dims):
    ah, bh = a.astype(BF16), b.astype(BF16)
    al, bl = (a - ah.astype(F32)).astype(BF16), (b - bh.astype(F32)).astype(BF16)
    dot = functools.partial(lax.dot_general, dimension_numbers=dims, preferred_element_type=F32)
    return dot(ah, bh) + (dot(ah, bl) + dot(al, bh))


def _sigmoid(x):
    return 1.0 / (1.0 + jnp.exp(-x))


def _silu(x):
    return x * _sigmoid(x)


def _softplus(x):
    return jnp.maximum(x, 0.0) + jnp.log(1.0 + jnp.exp(-jnp.abs(x)))


def _iota2(shape, dim):
    return lax.broadcasted_iota(jnp.int32, shape, dim)


def _cparams(sem=None):
    return pltpu.CompilerParams(dimension_semantics=sem, vmem_limit_bytes=VMEM_LIMIT)


def _gla_chunk(st, q, k, v, lr, w2, b, gg, ga, gn):
    c = q.shape[0]
    row, col = _iota2((c, c), 0), _iota2((c, c), 1)
    incl = row >= col
    tri = incl.astype(F32)
    q = q.astype(F32) * (GLA_DK ** -0.5)
    k = k.astype(F32)
    v = v.astype(F32)
    lf = -_softplus(-(_bdot(lr, w2, NN) + b)) / GLA_TAU
    bcum = _left_dot(tri, lf, NN, TN)
    blast = jnp.sum(lf, axis=0, keepdims=True)
    q_in = q * jnp.exp(bcum)
    k_in = k * jnp.exp(-bcum)
    a = jnp.where(incl, _bdot(q_in, k_in, NT), 0.0)
    o = _bdot(a, v, NN) + _bdot(q_in, st, NT)
    k_dec = k * jnp.exp(blast - bcum)
    st_new = st * jnp.exp(blast) + _bdot(v, k_dec, TN)
    on = o * lax.rsqrt(jnp.mean(o * o, axis=-1, keepdims=True) + EPS) * gn
    res = _sigmoid(ga.astype(F32)) * on * _silu(gg.astype(F32))
    return res, st_new


BNN = (((2,), (1,)), ((0,), (0,)))
BNT = (((2,), (2,)), ((0,), (0,)))
BTN = (((1,), (1,)), ((0,), (0,)))


def _tri_inv_raw(a):
    _, c, _ = a.shape
    eye = (_iota2((c, c), 0) == _iota2((c, c), 1)).astype(F32)[None]
    x = a
    p = eye - a
    for _ in range(5):
        x = _dot3(x, x, BNN)
        p = _dot3(p, eye + x, BNN)
    return p


@jax.custom_vjp
def _tri_inv(a):
    return _tri_inv_raw(a)


def _tri_inv_fwd(a):
    t = _tri_inv_raw(a)
    return t, t


def _tri_inv_bwd(t, dt):
    return (-_dot3(_dot3(t, dt, BTN), t, BNT),)


_tri_inv.defvjp(_tri_inv_fwd, _tri_inv_bwd)


def _dn_chunk(s, qr, kr, vr, za, zb, alog, dtb, gz, gb, dn, h0, differentiable):
    hb, c, _ = qr.shape
    row, col = _iota2((c, c), 0), _iota2((c, c), 1)
    incl = (row >= col)[None]
    strict = (row > col)[None]
    tri = jnp.broadcast_to(incl.astype(F32), (hb, c, c))
    eye = (row == col).astype(F32)[None]
    ones_cc = jnp.ones((hb, c, c), F32)
    sel_shape = (hb, LANE, LANE)
    sel = (lax.broadcasted_iota(jnp.int32, sel_shape, 1) == h0 + lax.broadcasted_iota(jnp.int32, sel_shape, 0))
    sel = sel.astype(F32)
    lane0 = (lax.broadcasted_iota(jnp.int32, (hb, LANE, c), 1) == 0).astype(F32)

    def l2n(t):
        return t * lax.rsqrt(jnp.sum(t * t, axis=-1, keepdims=True) + EPS)

    q = l2n(qr.astype(F32)) * (DN_D ** -0.5)
    k = l2n(kr.astype(F32))
    v = vr.astype(F32)
    za_b = jnp.broadcast_to(za[None], (hb,) + za.shape)
    zb_b = jnp.broadcast_to(zb[None], (hb,) + zb.shape)
    g = -jnp.exp(alog) * _softplus(_pick_dot(za_b, sel, BNN, BNT) + dtb)
    beta = _sigmoid(_pick_dot(zb_b, sel, BNN, BNT))
    gcum = _left_dot(tri, g, BNN, BTN)
    glast = jnp.sum(g, axis=1, keepdims=True)
    cm = _pick_dot(gcum, lane0, BNN, BNT)
    rm = _left_dot(ones_cc, cm * eye, BNN, BTN)
    dec = jnp.exp(jnp.where(incl, cm - rm, -1e30))
    kb = k * beta
    a = jnp.where(strict, _bdot(kb, k, BNT) * dec, 0.0)
    t = _tri_inv(a) if differentiable else _tri_inv_raw(a)
    egc = jnp.exp(gcum)
    u = _bdot(t, v * beta, BNN)
    w = _bdot(t, kb * egc, BNN)
    attn = jnp.where(incl, _bdot(q, k, BNT) * dec, 0.0)
    q_dec = q * egc
    k_dec = k * jnp.exp(glast - gcum)
    v_new = u - _bdot(w, s, BNN)
    o = _bdot(q_dec, s, BNN) + _bdot(attn, v_new, BNN)
    s_new = s * jnp.exp(glast) + _bdot(k_dec, v_new, BTN)
    on = o * lax.rsqrt(jnp.mean(o * o, axis=-1, keepdims=True) + EPS) * dn
    res = _sigmoid(gb.astype(F32)) * on * _silu(gz.astype(F32))
    return res, s_new


def _gla_fwd(zb, zs, w2p, gla_b, gla_norm, bsz, nc):
    t = zb.shape[0]

    def body(q_ref, k_ref, v_ref, gg_ref, ga_ref, lr_ref, w2_ref, b_ref, gn_ref, o_ref, st_ref, state):
        n, h = pl.program_id(1), pl.program_id(2)

        @pl.when(n == 0)
        def _():
            state[h] = jnp.zeros((GLA_DV, GLA_DK), F32)

        st = state[h]
        st_ref[0, 0, 0] = st
        res, st_new = _gla_chunk(st, q_ref[...], k_ref[...], v_ref[...], lr_ref[...], w2_ref[...], b_ref[...],
                                 gg_ref[...], ga_ref[...], gn_ref[...])
        o_ref[...] = res
        state[h] = st_new

    def rows(b, n, h):
        return b * nc + n

    qk = lambda base: pl.BlockSpec((CHUNK, GLA_DK), lambda b, n, h: (rows(b, n, h), base // GLA_DK + h))
    vv = lambda base: pl.BlockSpec((CHUNK, GLA_DV), lambda b, n, h: (rows(b, n, h), base // GLA_DV + h))
    return pl.pallas_call(
        body, name="gla_fwd", grid=(bsz, nc, GLA_HEADS),
        in_specs=[qk(ZB_GQ), qk(ZB_GK), vv(ZB_GV), vv(ZB_GG), vv(ZB_GA),
                  pl.BlockSpec((CHUNK, LANE), lambda b, n, h: (rows(b, n, h), ZS_LR // LANE)),
                  pl.BlockSpec((LANE, GLA_DK), lambda b, n, h: (0, h)),
                  pl.BlockSpec((1, GLA_DK), lambda b, n, h: (0, h)),
                  pl.BlockSpec((1, GLA_DV), lambda b, n, h: (0, 0))],
        out_specs=[pl.BlockSpec((CHUNK, GLA_DV), lambda b, n, h: (rows(b, n, h), h)),
                   pl.BlockSpec((1, 1, 1, GLA_DV, GLA_DK), lambda b, n, h: (b, n, h, 0, 0))],
        out_shape=[jax.ShapeDtypeStruct((t, GLA_V), F32),
                   jax.ShapeDtypeStruct((bsz, nc, GLA_HEADS, GLA_DV, GLA_DK), F32)],
        scratch_shapes=[pltpu.VMEM((GLA_HEADS, GLA_DV, GLA_DK), F32)],
        compiler_params=_cparams(("arbitrary", "arbitrary", "arbitrary")),
    )(zb, zb, zb, zb, zb, zs, w2p, gla_b, gla_norm)


def _gla_bwd(zb, zs, w2p, gla_b, gla_norm, states, dmixed, bsz, nc, rider=None):
    t = zb.shape[0]

    def body(q_ref, k_ref, v_ref, gg_ref, ga_ref, lr_ref, w2_ref, b_ref, gn_ref, st_ref, dm_ref,
             dq_ref, dk_ref, dv_ref, dgg_ref, dga_ref, dlr_ref, dw2_ref, db_ref, dgn_ref, dstate):
        b, n, h = pl.program_id(0), pl.program_id(1), pl.program_id(2)

        @pl.when(n == 0)
        def _():
            dstate[h] = jnp.zeros((GLA_DV, GLA_DK), F32)

        @pl.when((b == 0) & (n == 0) & (h == 0))
        def _():
            dw2_ref[...] = jnp.zeros_like(dw2_ref)
            db_ref[...] = jnp.zeros_like(db_ref)
            dgn_ref[...] = jnp.zeros_like(dgn_ref)

        args = (st_ref[0, 0, 0], q_ref[...], k_ref[...], v_ref[...], lr_ref[...], w2_ref[...], b_ref[...],
                gg_ref[...], ga_ref[...], gn_ref[...])
        _, vjp = jax.vjp(_gla_chunk, *args)
        dst, dq, dk, dv, dlr, dw2, db, dgg, dga, dgn = vjp((dm_ref[...].astype(F32), dstate[h]))
        dstate[h] = dst
        dq_ref[...] = dq.astype(dq_ref.dtype)
        dk_ref[...] = dk.astype(dk_ref.dtype)
        dv_ref[...] = dv.astype(dv_ref.dtype)
        dgg_ref[...] = dgg.astype(dgg_ref.dtype)
        dga_ref[...] = dga.astype(dga_ref.dtype)

        @pl.when(h == 0)
        def _():
            dlr_ref[...] = dlr

        @pl.when(h > 0)
        def _():
            dlr_ref[...] += dlr

        dw2_ref[h] += dw2
        db_ref[h] += db
        dgn_ref[...] += dgn

    def rows(b, n, h):
        return b * nc + (nc - 1 - n)

    qk = lambda base: pl.BlockSpec((CHUNK, GLA_DK), lambda b, n, h: (rows(b, n, h), base // GLA_DK + h))
    vv = lambda base: pl.BlockSpec((CHUNK, GLA_DV), lambda b, n, h: (rows(b, n, h), base // GLA_DV + h))
    full = lambda shape: pl.BlockSpec(shape, lambda b, n, h: (0,) * len(shape))
    return _hosted_call(
        body, rider, name="gla_bwd", grid=(bsz, nc, GLA_HEADS),
        in_specs=[qk(ZB_GQ), qk(ZB_GK), vv(ZB_GV), vv(ZB_GG), vv(ZB_GA),
                  pl.BlockSpec((CHUNK, LANE), lambda b, n, h: (rows(b, n, h), ZS_LR // LANE)),
                  pl.BlockSpec((LANE, GLA_DK), lambda b, n, h: (0, h)),
                  pl.BlockSpec((1, GLA_DK), lambda b, n, h: (0, h)),
                  pl.BlockSpec((1, GLA_DV), lambda b, n, h: (0, 0)),
                  pl.BlockSpec((1, 1, 1, GLA_DV, GLA_DK), lambda b, n, h: (b, nc - 1 - n, h, 0, 0)),
                  vv(0)],
        out_specs=[qk(0), qk(0), vv(0), vv(0), vv(0),
                   pl.BlockSpec((CHUNK, LANE), lambda b, n, h: (rows(b, n, h), 0)),
                   full((GLA_HEADS, LANE, GLA_DK)), full((GLA_HEADS, 1, GLA_DK)), full((1, GLA_DV))],
        out_shape=[jax.ShapeDtypeStruct((t, GLA_QK), BF16), jax.ShapeDtypeStruct((t, GLA_QK), BF16),
                   jax.ShapeDtypeStruct((t, GLA_V), BF16), jax.ShapeDtypeStruct((t, GLA_V), BF16),
                   jax.ShapeDtypeStruct((t, GLA_V), BF16), jax.ShapeDtypeStruct((t, LANE), F32),
                   jax.ShapeDtypeStruct((GLA_HEADS, LANE, GLA_DK), F32),
                   jax.ShapeDtypeStruct((GLA_HEADS, 1, GLA_DK), F32),
                   jax.ShapeDtypeStruct((1, GLA_DV), F32)],
        scratch_shapes=[pltpu.VMEM((GLA_HEADS, GLA_DV, GLA_DK), F32)],
        args=(zb, zb, zb, zb, zb, zs, w2p, gla_b, gla_norm, states, dmixed))


DN_HB = 16


def _dn_specs(nc, reverse):
    wide = DN_HB * DN_D

    def rows(b, n, h):
        return b * nc + ((nc - 1 - n) if reverse else n)

    def col(base):
        return pl.BlockSpec((CHUNK, wide), lambda b, n, h: (rows(b, n, h), base // wide + h))

    def fixed(c):
        return pl.BlockSpec((CHUNK, LANE), lambda b, n, h: (rows(b, n, h), c))

    head = pl.BlockSpec((DN_HB, 1, LANE), lambda b, n, h: (h, 0, 0))
    return rows, col, fixed, head


def _lanes(j):
    return slice(j * DN_D, (j + 1) * DN_D)


def _by_head(ref):
    return jnp.stack([ref[:, _lanes(j)] for j in range(DN_HB)], axis=0)


def _dn_fwd(act, zb, zs, alog_b, dtb_b, dn_norm, mix_gla, bsz, nc, rider=None):
    t = zb.shape[0]
    rows, col, fixed, head = _dn_specs(nc, False)

    def body(q_ref, k_ref, v_ref, za_ref, zb_ref, al_ref, dt_ref, gz_ref, gb_ref, dn_ref, mg_ref,
             o_ref, st_ref, state):
        n, hb = pl.program_id(1), pl.program_id(2)
        heads = pl.ds(hb * DN_HB, DN_HB)
        s = jnp.where(n > 0, state[heads], 0.0)
        st_ref[0, 0] = s
        res, s_new = _dn_chunk(s, _by_head(q_ref), _by_head(k_ref), _by_head(v_ref), za_ref[...], zb_ref[...],
                               al_ref[...], dt_ref[...], _by_head(gz_ref), _by_head(gb_ref), dn_ref[...],
                               hb * DN_HB, False)
        for j in range(DN_HB):
            o_ref[:, _lanes(j)] = (res[j] + mg_ref[:, _lanes(j)]).astype(o_ref.dtype)
        state[heads] = s_new

    return _hosted_call(
        body, rider, name="dn_fwd", grid=(bsz, nc, DN_HEADS // DN_HB),
        in_specs=[col(0), col(DN_HEADS * DN_D), col(2 * DN_HEADS * DN_D),
                  fixed(ZS_DA // LANE), fixed(ZS_DB // LANE), head, head,
                  col(ZB_DZ), col(ZB_GB), pl.BlockSpec((1, DN_D), lambda b, n, h: (0, 0)), col(0)],
        out_specs=[col(0), pl.BlockSpec((1, 1, DN_HB, DN_D, DN_D), lambda b, n, h: (b, n, h, 0, 0))],
        out_shape=[jax.ShapeDtypeStruct((t, D_MODEL), BF16),
                   jax.ShapeDtypeStruct((bsz, nc, DN_HEADS, DN_D, DN_D), F32)],
        scratch_shapes=[pltpu.VMEM((DN_HEADS, DN_D, DN_D), F32)],
        args=(act, act, act, zs, zs, alog_b, dtb_b, zb, zb, dn_norm, mix_gla))


def _dn_bwd(act, zb, zs, alog_b, dtb_b, dn_norm, states, dmixed, bsz, nc, rider=None):
    t = zb.shape[0]
    rows, col, fixed, head = _dn_specs(nc, True)

    def body(q_ref, k_ref, v_ref, za_ref, zb_ref, al_ref, dt_ref, gz_ref, gb_ref, dn_ref, st_ref, dm_ref,
             dq_ref, dk_ref, dv_ref, dza_ref, dzb_ref, dgz_ref, dgb_ref, dal_ref, ddt_ref, ddn_ref, dstate):
        b, n, hb = pl.program_id(0), pl.program_id(1), pl.program_id(2)

        @pl.when((b == 0) & (n == 0) & (hb == 0))
        def _():
            dal_ref[...] = jnp.zeros_like(dal_ref)
            ddt_ref[...] = jnp.zeros_like(ddt_ref)
            ddn_ref[...] = jnp.zeros_like(ddn_ref)

        heads = pl.ds(hb * DN_HB, DN_HB)
        fn = functools.partial(_dn_chunk, h0=hb * DN_HB, differentiable=True)
        _, vjp = jax.vjp(fn, st_ref[0, 0], _by_head(q_ref), _by_head(k_ref), _by_head(v_ref), za_ref[...],
                         zb_ref[...], al_ref[...], dt_ref[...], _by_head(gz_ref), _by_head(gb_ref), dn_ref[...])
        ds_in = jnp.where(n > 0, dstate[heads], 0.0)
        ds, dq, dk, dv, dza, dzb, dal, ddt, dgz, dgb, ddn = vjp((_by_head(dm_ref).astype(F32), ds_in))
        dstate[heads] = ds
        for j in range(DN_HB):
            dq_ref[:, _lanes(j)] = dq[j].astype(dq_ref.dtype)
            dk_ref[:, _lanes(j)] = dk[j].astype(dk_ref.dtype)
            dv_ref[:, _lanes(j)] = dv[j].astype(dv_ref.dtype)
            dgz_ref[:, _lanes(j)] = dgz[j].astype(dgz_ref.dtype)
            dgb_ref[:, _lanes(j)] = dgb[j].astype(dgb_ref.dtype)
        dal_ref[heads] += dal
        ddt_ref[heads] += ddt

        @pl.when(hb == 0)
        def _():
            dza_ref[...] = dza
            dzb_ref[...] = dzb

        @pl.when(hb > 0)
        def _():
            dza_ref[...] += dza
            dzb_ref[...] += dzb

        ddn_ref[...] += ddn

    full = lambda shape: pl.BlockSpec(shape, lambda b, n, h: (0,) * len(shape))
    return _hosted_call(
        body, rider, name="dn_bwd", grid=(bsz, nc, DN_HEADS // DN_HB),
        in_specs=[col(0), col(DN_HEADS * DN_D), col(2 * DN_HEADS * DN_D),
                  fixed(ZS_DA // LANE), fixed(ZS_DB // LANE), head, head,
                  col(ZB_DZ), col(ZB_GB), pl.BlockSpec((1, DN_D), lambda b, n, h: (0, 0)),
                  pl.BlockSpec((1, 1, DN_HB, DN_D, DN_D), lambda b, n, h: (b, nc - 1 - n, h, 0, 0)), col(0)],
        out_specs=[col(0), col(0), col(0), fixed(0), fixed(0), col(0), col(0),
                   full((DN_HEADS, 1, LANE)), full((DN_HEADS, 1, LANE)), full((1, DN_D))],
        out_shape=[jax.ShapeDtypeStruct((t, D_MODEL), F32), jax.ShapeDtypeStruct((t, D_MODEL), F32),
                   jax.ShapeDtypeStruct((t, D_MODEL), F32),
                   jax.ShapeDtypeStruct((t, LANE), F32), jax.ShapeDtypeStruct((t, LANE), F32),
                   jax.ShapeDtypeStruct((t, D_MODEL), BF16), jax.ShapeDtypeStruct((t, D_MODEL), BF16),
                   jax.ShapeDtypeStruct((DN_HEADS, 1, LANE), F32), jax.ShapeDtypeStruct((DN_HEADS, 1, LANE), F32),
                   jax.ShapeDtypeStruct((1, DN_D), F32)],
        scratch_shapes=[pltpu.VMEM((DN_HEADS, DN_D, DN_D), F32)],
        args=(act, act, act, zs, zs, alog_b, dtb_b, zb, zb, dn_norm, states, dmixed))


MM_VMEM_BUDGET = 40 * 1024 * 1024
MM_TILE_PREF = (1024, 1024, 2048)


def _divisor_tile(n, cap):
    if n <= cap:
        return n
    for c in range(cap - cap % LANE, 0, -LANE):
        if n % c == 0:
            return c
    return n


def _mm_tiles(m, n, kd, a_bytes, b_bytes, mn_bytes):
    tm, tn, tk = (_divisor_tile(d, c) for d, c in zip((m, n, kd), MM_TILE_PREF))

    def need(tm, tn, tk):
        acc = 0 if tk == kd else 4 * tm * tn
        return 2 * (tm * tk * a_bytes + tk * tn * b_bytes + tm * tn * mn_bytes) + acc + 4 * tm * tn

    while need(tm, tn, tk) > MM_VMEM_BUDGET:
        if tk > 512 and tk * max(tm * a_bytes, tn * b_bytes) >= tm * tn * mn_bytes:
            tk = _divisor_tile(kd, tk // 2)
        elif tn >= tm and tn > LANE:
            tn = _divisor_tile(n, tn // 2)
        else:
            tm = _divisor_tile(m, tm // 2)
    return tm, tn, tk


def _mm(a, b, *, ta=False, tb=False, out_dtypes=(F32,), epilogue=None, extras=(), name, rider=None):
    m, kd = (a.shape[1], a.shape[0]) if ta else a.shape
    n = b.shape[0] if tb else b.shape[1]
    mn_bytes = sum(e.dtype.itemsize for e in extras) + sum(jnp.dtype(dt).itemsize for dt in out_dtypes)
    tm, tn, tk = _mm_tiles(m, n, kd, a.dtype.itemsize, b.dtype.itemsize, mn_bytes)
    nk = kd // tk
    n_ex = len(extras)
    dims = (((0,) if ta else (1,), (1,) if tb else (0,)), ((), ()))

    def finish(acc, ex_refs, out_refs):
        outs = (acc,) if epilogue is None else epilogue(acc, *[r[...] for r in ex_refs])
        for r, o in zip(out_refs, outs):
            r[...] = o.astype(r.dtype)

    def partial_product(a_ref, b_ref):
        return lax.dot_general(a_ref[...].astype(BF16), b_ref[...].astype(BF16), dims, preferred_element_type=F32)

    def body_single(*refs):
        finish(partial_product(refs[0], refs[1]), refs[2:2 + n_ex], refs[2 + n_ex:])

    def body_acc(*refs):
        acc = refs[-1]
        k = pl.program_id(2)

        @pl.when(k == 0)
        def _():
            acc[...] = partial_product(refs[0], refs[1])

        @pl.when(k > 0)
        def _():
            acc[...] += partial_product(refs[0], refs[1])

        @pl.when(k == nk - 1)
        def _():
            finish(acc[...], refs[2:2 + n_ex], refs[2 + n_ex:-1])

    a_spec = pl.BlockSpec((tk, tm), lambda i, j, k: (k, i)) if ta else pl.BlockSpec((tm, tk), lambda i, j, k: (i, k))
    b_spec = pl.BlockSpec((tn, tk), lambda i, j, k: (j, k)) if tb else pl.BlockSpec((tk, tn), lambda i, j, k: (k, j))
    mn_spec = pl.BlockSpec((tm, tn), lambda i, j, k: (i, j))
    outs = _hosted_call(
        body_single if nk == 1 else body_acc, rider, name=name, grid=(m // tm, n // tn, nk),
        in_specs=[a_spec, b_spec] + [mn_spec] * n_ex,
        out_specs=[mn_spec] * len(out_dtypes),
        out_shape=[jax.ShapeDtypeStruct((m, n), dt) for dt in out_dtypes],
        scratch_shapes=[] if nk == 1 else [pltpu.VMEM((tm, tn), F32)],
        args=(a, b, *extras))
    return outs[0] if len(outs) == 1 else outs


ROW_BLOCK = 256


def _row_spec(width=D_MODEL):
    return pl.BlockSpec((ROW_BLOCK, width), lambda i: (i, 0))


def _vec_spec(width=D_MODEL):
    return pl.BlockSpec((1, width), lambda i: (0, 0))


def _rms_fwd(x, g, name):
    def body(x_ref, g_ref, h_ref):
        xf = x_ref[...]
        h_ref[...] = (xf * lax.rsqrt(jnp.mean(xf * xf, axis=-1, keepdims=True) + EPS) * g_ref[...]).astype(BF16)

    t = x.shape[0]
    return pl.pallas_call(
        body, name=name, grid=(t // ROW_BLOCK,), in_specs=[_row_spec(), _vec_spec()], out_specs=_row_spec(),
        out_shape=jax.ShapeDtypeStruct((t, D_MODEL), BF16), compiler_params=_cparams(("parallel",)),
    )(x, g)


def _rms_bwd_math(xf, g, dh):
    rstd = lax.rsqrt(jnp.mean(xf * xf, axis=-1, keepdims=True) + EPS)
    xhat = xf * rstd
    dxhat = dh * g
    dx = rstd * (dxhat - xhat * jnp.mean(dxhat * xhat, axis=-1, keepdims=True))
    dg = jnp.sum(dh * xhat, axis=0, keepdims=True)
    return dx, dg


def _rms_bwd(x, g, dh, dres, name):
    def body(x_ref, g_ref, dh_ref, dres_ref, dx_ref, dg_ref):
        dx, dg = _rms_bwd_math(x_ref[...], g_ref[...], dh_ref[...].astype(F32))
        dx_ref[...] = dres_ref[...] + dx

        @pl.when(pl.program_id(0) == 0)
        def _():
            dg_ref[...] = jnp.zeros_like(dg_ref)

        dg_ref[...] += dg

    t = x.shape[0]
    return pl.pallas_call(
        body, name=name, grid=(t // ROW_BLOCK,),
        in_specs=[_row_spec(), _vec_spec(), _row_spec(), _row_spec()], out_specs=[_row_spec(), _vec_spec()],
        out_shape=[jax.ShapeDtypeStruct((t, D_MODEL), F32), jax.ShapeDtypeStruct((1, D_MODEL), F32)],
        compiler_params=_cparams(("arbitrary",)),
    )(x, g, dh, dres)


def _loss_head(x3, g, target):
    def body(x_ref, g_ref, t_ref, dx_ref, dg_ref, loss_ref):
        xf, gg = x_ref[...], g_ref[...]
        rstd = lax.rsqrt(jnp.mean(xf * xf, axis=-1, keepdims=True) + EPS)
        err = xf * rstd * gg - t_ref[...]
        dx, dg = _rms_bwd_math(xf, gg, err * (1.0 / D_MODEL))
        dx_ref[...] = dx

        @pl.when(pl.program_id(0) == 0)
        def _():
            dg_ref[...] = jnp.zeros_like(dg_ref)
            loss_ref[...] = jnp.zeros_like(loss_ref)

        dg_ref[...] += dg
        part = jnp.sum(jnp.sum(err * err, axis=-1, keepdims=True), axis=0, keepdims=True) * (0.5 / D_MODEL)
        loss_ref[...] += jnp.broadcast_to(part, loss_ref.shape)

    t = x3.shape[0]
    return pl.pallas_call(
        body, name="loss_head", grid=(t // ROW_BLOCK,),
        in_specs=[_row_spec(), _vec_spec(), _row_spec()], out_specs=[_row_spec(), _vec_spec(), _vec_spec(LANE)],
        out_shape=[jax.ShapeDtypeStruct((t, D_MODEL), F32), jax.ShapeDtypeStruct((1, D_MODEL), F32),
                   jax.ShapeDtypeStruct((1, LANE), F32)],
        compiler_params=_cparams(("arbitrary",)),
    )(x3, g, target)


def _ple_bwd(dx3, gpre, pp):
    def body(dx_ref, gp_ref, pp_ref, dgp_ref, dpp_ref):
        dx, sg = dx_ref[...], _sigmoid(gp_ref[...])
        dpp_ref[...] = (dx * sg).astype(BF16)
        dgp_ref[...] = (dx * pp_ref[...] * sg * (1.0 - sg)).astype(BF16)

    t = dx3.shape[0]
    return pl.pallas_call(
        body, name="ple_bwd", grid=(t // ROW_BLOCK,), in_specs=[_row_spec()] * 3, out_specs=[_row_spec()] * 2,
        out_shape=[jax.ShapeDtypeStruct((t, D_MODEL), BF16)] * 2, compiler_params=_cparams(("parallel",)),
    )(dx3, gpre, pp)


CONV_COLS = 256


def _shift_down(x, s):
    if s == 0:
        return x
    return jnp.where(_iota2(x.shape, 0) >= s, pltpu.roll(x, s, 0), 0.0)


def _shift_up(x, s):
    if s == 0:
        return x
    rows = x.shape[0]
    return jnp.where(_iota2(x.shape, 0) < rows - s, pltpu.roll(x, rows - s, 0), 0.0)


def _conv_pre(xf, w):
    return sum(_shift_down(xf, DN_CONV - 1 - j) * w[j:j + 1, :] for j in range(DN_CONV))


def _conv_fwd(zb, conv_w, bsz, seq):
    def body(x_ref, w_ref, y_ref):
        y_ref[...] = _silu(_conv_pre(x_ref[...], w_ref[...]))

    nblk = DN_QKV // CONV_COLS
    return pl.pallas_call(
        body, name="conv_fwd", grid=(bsz, nblk),
        in_specs=[pl.BlockSpec((seq, CONV_COLS), lambda b, j: (b, ZB_DQKV // CONV_COLS + j)),
                  pl.BlockSpec((DN_CONV, CONV_COLS), lambda b, j: (0, j))],
        out_specs=pl.BlockSpec((seq, CONV_COLS), lambda b, j: (b, j)),
        out_shape=jax.ShapeDtypeStruct((bsz * seq, DN_QKV), F32),
        compiler_params=_cparams(("parallel", "parallel")),
    )(zb, conv_w)


def _conv_bwd(zb, conv_w, dact, bsz, seq):
    def body(x_ref, w_ref, dy_ref, dx_ref, dw_ref):
        xf, w = x_ref[...], w_ref[...]
        c = _conv_pre(xf, w)
        sg = _sigmoid(c)
        dc = dy_ref[...].astype(F32) * sg * (1.0 + c * (1.0 - sg))
        dx = sum(_shift_up(dc, DN_CONV - 1 - j) * w[j:j + 1, :] for j in range(DN_CONV))
        dx_ref[...] = dx.astype(BF16)
        dw = jnp.concatenate(
            [jnp.sum(dc * _shift_down(xf, DN_CONV - 1 - j), axis=0, keepdims=True) for j in range(DN_CONV)], axis=0)

        @pl.when(pl.program_id(1) == 0)
        def _():
            dw_ref[...] = jnp.zeros_like(dw_ref)

        dw_ref[...] += dw

    nblk = DN_QKV // CONV_COLS
    return pl.pallas_call(
        body, name="conv_bwd", grid=(nblk, bsz),
        in_specs=[pl.BlockSpec((seq, CONV_COLS), lambda j, b: (b, ZB_DQKV // CONV_COLS + j)),
                  pl.BlockSpec((DN_CONV, CONV_COLS), lambda j, b: (0, j)),
                  pl.BlockSpec((seq, CONV_COLS), lambda j, b: (b, j))],
        out_specs=[pl.BlockSpec((seq, CONV_COLS), lambda j, b: (b, j)),
                   pl.BlockSpec((DN_CONV, CONV_COLS), lambda j, b: (0, j))],
        out_shape=[jax.ShapeDtypeStruct((bsz * seq, DN_QKV), BF16), jax.ShapeDtypeStruct((DN_CONV, DN_QKV), F32)],
        compiler_params=_cparams(("parallel", "arbitrary")),
    )(zb, conv_w, dact)


MESH_IDS = pl.DeviceIdType.MESH
ANY_SPEC = pl.BlockSpec(memory_space=pl.ANY)


COMM_SCRATCH = (pltpu.SemaphoreType.DMA((7,)), pltpu.SemaphoreType.DMA((7,)), pltpu.SemaphoreType.DMA)


def _gather_phases(x_ref, out_ref, send_sems, recv_sems, local_sem):
    mx, my, mc = lax.axis_index("x"), lax.axis_index("y"), lax.axis_index("c")
    me, sibling = (mx, my, mc), (mx, my, 1 - mc)
    chips = [(1 - mx, my), (mx, 1 - my), (1 - mx, 1 - my)]

    def slot(px, py, pc):
        return out_ref.at[4 * px + 2 * py + pc]

    def copy(k, block, to, src=None):
        return pltpu.make_async_remote_copy(
            src_ref=slot(*block) if src is None else src, dst_ref=slot(*block),
            send_sem=send_sems.at[k], recv_sem=recv_sems.at[k], device_id=to, device_id_type=MESH_IDS)

    def mine():
        return pltpu.make_async_copy(x_ref, slot(*me), local_sem)

    def first():
        return [copy(0, me, sibling, src=x_ref)] + [copy(1 + j, me, (*chip, mc), src=x_ref)
                                                    for j, chip in enumerate(chips)]

    def passed():
        return [copy(4 + j, (*chip, mc), sibling) for j, chip in enumerate(chips)]

    def start():
        mine().start()
        for cp in first():
            cp.start()

    def forward():
        for j, (chip, cp) in enumerate(zip(chips, passed())):
            copy(1 + j, (*chip, mc), me).wait_recv()
            cp.start()

    def finish():
        copy(0, sibling, me).wait_recv()
        for j, chip in enumerate(chips):
            copy(4 + j, (*chip, 1 - mc), me).wait_recv()
        for cp in first() + passed():
            cp.wait_send()
        mine().wait()

    return start, forward, finish


def _scatter_phases(x_ref, out_ref, send_sems, recv_sems, local_sem):
    mx, my, mc = lax.axis_index("x"), lax.axis_index("y"), lax.axis_index("c")
    me = 4 * mx + 2 * my + mc

    def peer(k):
        return (mx ^ ((k >> 2) & 1), my ^ ((k >> 1) & 1), mc ^ (k & 1))

    def slot_of(k):
        px, py, pc = peer(k)
        return 4 * px + 2 * py + pc

    def copy(k, src_slot, dst_slot):
        return pltpu.make_async_remote_copy(
            src_ref=x_ref.at[src_slot], dst_ref=out_ref.at[dst_slot],
            send_sem=send_sems.at[k - 1], recv_sem=recv_sems.at[k - 1],
            device_id=peer(k), device_id_type=MESH_IDS)

    def sends():
        return [copy(k, slot_of(k), me) for k in range(1, N_DEV)]

    def mine():
        return pltpu.make_async_copy(x_ref.at[me], out_ref.at[me], local_sem)

    def start():
        mine().start()
        for cp in sends():
            cp.start()

    def forward():
        pass

    def finish():
        for k in range(1, N_DEV):
            copy(k, me, slot_of(k)).wait_recv()
        for cp in sends():
            cp.wait_send()
        mine().wait()

    return start, forward, finish


class _Rider:
    def __init__(self, phases, x, out_shape):
        self.phases, self.x, self.out_shape = phases, x, out_shape


def _gather_rider(x):
    return _Rider(_gather_phases, x, jax.ShapeDtypeStruct((N_DEV,) + x.shape, x.dtype))


def _scatter_rider(x):
    return _Rider(_scatter_phases, x, jax.ShapeDtypeStruct(x.shape, x.dtype))


def _exchange(rider, name):
    def body(x_ref, out_ref, send_sems, recv_sems, local_sem):
        for phase in rider.phases(x_ref, out_ref, send_sems, recv_sems, local_sem):
            phase()

    return pl.pallas_call(body, name=name, out_shape=rider.out_shape, in_specs=[ANY_SPEC], out_specs=ANY_SPEC,
                          scratch_shapes=list(COMM_SCRATCH))(rider.x)


def _all_gather(x, name):
    return _exchange(_gather_rider(x), name)


def _all_to_all(x, name):
    return _exchange(_scatter_rider(x), name)


def _hosted_call(body, rider, *, name, grid, in_specs, out_specs, out_shape, scratch_shapes, args):
    if rider is None:
        return pl.pallas_call(body, name=name, grid=grid, in_specs=in_specs, out_specs=out_specs, out_shape=out_shape,
                              scratch_shapes=scratch_shapes, compiler_params=_cparams(("arbitrary",) * len(grid)))(*args)
    n_in, n_out, n_scr = len(in_specs), len(out_specs), len(scratch_shapes)
    total = math.prod(grid)

    def riding(*refs):
        host_in, x_ref = refs[:n_in], refs[n_in]
        host_out, out_ref = refs[n_in + 1:n_in + 1 + n_out], refs[n_in + 1 + n_out]
        host_scr = refs[n_in + 2 + n_out:n_in + 2 + n_out + n_scr]
        start, forward, finish = rider.phases(x_ref, out_ref, *refs[n_in + 2 + n_out + n_scr:])
        step = 0
        for axis, size in enumerate(grid):
            step = step * size + pl.program_id(axis)
        pl.when(step == 0)(start)
        pl.when(step == total // 2)(forward)
        body(*host_in, *host_out, *host_scr)
        pl.when(step == total - 1)(finish)

    return pl.pallas_call(
        riding, name=name, grid=grid, in_specs=list(in_specs) + [ANY_SPEC], out_specs=list(out_specs) + [ANY_SPEC],
        out_shape=list(out_shape) + [rider.out_shape], scratch_shapes=list(scratch_shapes) + list(COMM_SCRATCH),
        compiler_params=_cparams(("arbitrary",) * len(grid)))(*args, rider.x)


def _adamw_math(w, g, m, v):
    m = ADAM_B1 * m + (1.0 - ADAM_B1) * g
    v = ADAM_B2 * v + (1.0 - ADAM_B2) * jnp.square(g)
    m_hat = m / (1.0 - ADAM_B1 ** ADAM_STEP)
    v_hat = v / (1.0 - ADAM_B2 ** ADAM_STEP)
    delta = -ADAM_LR * (m_hat / (jnp.sqrt(v_hat) + ADAM_EPS) + ADAM_WD * w)
    return delta, m, v


ADAM_ROWS = 128


def _adamw_reduce(w, m, v, parts, row0, name):
    rows, cols = w.shape
    tr = ADAM_ROWS if rows % ADAM_ROWS == 0 else rows
    r0 = row0 // tr

    def body(w_ref, m_ref, v_ref, *refs):
        part_refs, (g_ref, d_ref, nm_ref, nv_ref) = refs[:N_DEV], refs[N_DEV:]
        g = part_refs[0][...].astype(F32)
        for r in part_refs[1:]:
            g = g + r[...].astype(F32)
        delta, nm, nv = _adamw_math(w_ref[...], g, m_ref[...], v_ref[...])
        g_ref[...] = g
        d_ref[...] = delta
        nm_ref[...] = nm
        nv_ref[...] = nv

    blk = pl.BlockSpec((tr, cols), lambda i: (i, 0))
    part_specs = [pl.BlockSpec((None, tr, cols), functools.partial(lambda i, k: (k, r0 + i, 0), k=k))
                  for k in range(N_DEV)]
    return pl.pallas_call(
        body, name=name, grid=(rows // tr,), in_specs=[blk] * 3 + part_specs, out_specs=[blk] * 4,
        out_shape=[jax.ShapeDtypeStruct(w.shape, F32)] * 4, compiler_params=_cparams(("parallel",)),
    )(w, m, v, *([parts] * N_DEV))


def _small_reduce(gathered, lane_sum_from):
    r = gathered.shape[1]

    def body(g_ref, o_ref):
        g = g_ref[0]
        for k in range(1, N_DEV):
            g = g + g_ref[k]
        tot = jnp.broadcast_to(jnp.sum(g, axis=-1, keepdims=True), g.shape)
        o_ref[...] = jnp.where(_iota2(g.shape, 0) >= lane_sum_from, tot, g)

    return pl.pallas_call(body, name="small_grad_reduce", out_shape=jax.ShapeDtypeStruct((r, LANE), F32))(gathered)


def _adamw_small(w, m, v, g):
    def body(w_ref, m_ref, v_ref, g_ref, d_ref, nm_ref, nv_ref):
        d_ref[...], nm_ref[...], nv_ref[...] = _adamw_math(w_ref[...], g_ref[...], m_ref[...], v_ref[...])

    return pl.pallas_call(body, name="adamw_small", out_shape=[jax.ShapeDtypeStruct(w.shape, F32)] * 3)(w, m, v, g)


def _pack_rows(arrays):
    rows = [jnp.pad(a.reshape(-1), (0, -a.size % LANE)).reshape(-1, LANE) for a in arrays]
    out = jnp.concatenate(rows, axis=0)
    return jnp.pad(out, ((0, -out.shape[0] % 8), (0, 0)))


def _unpack_rows(packed, shapes):
    out, r = [], 0
    for shp in shapes:
        size = math.prod(shp)
        nrows = -(-size // LANE)
        out.append(packed[r:r + nrows].reshape(-1)[:size].reshape(shp))
        r += nrows
    return out


def _add_residual(acc, res):
    return (res + acc,)


def _by_cols(g):
    return g.reshape(g.shape[0], N_DEV, -1).transpose(1, 0, 2)


def _from_cols(blocks):
    return blocks.transpose(1, 0, 2).reshape(blocks.shape[1], -1)


ROWS_OUT = D_MODEL // N_DEV


def _local_step(x, p, target, w, up_shard, rows_shard):
    bsz, seq, _ = x.shape
    t, nc = bsz * seq, seq // CHUNK
    x0, p2, tgt = x.reshape(t, D_MODEL), p.reshape(t, PLE_DIM), target.reshape(t, D_MODEL)

    h = _rms_fwd(x0, w["g_mix"], "rms_mix")
    zb, up_blocks = _mm(h, w["wb"], name="in_proj", rider=_gather_rider(up_shard))
    w_up = _from_cols(up_blocks)
    zs = _mm(h, w["ws"], name="in_proj_gates")
    act = _conv_fwd(zb, w["conv"], bsz, seq)
    mix_gla, gla_states = _gla_fwd(zb, zs, w["w2p"], w["gla_b"], w["gla_norm"], bsz, nc)
    mixed, dn_states, row_blocks = _dn_fwd(act, zb, zs, w["alog_b"], w["dtb_b"], w["dn_norm"], mix_gla, bsz, nc,
                                           rider=_gather_rider(rows_shard))
    w_out = row_blocks[:, :ROWS_OUT].reshape(D_MODEL, D_MODEL)
    w_pg = row_blocks[:, ROWS_OUT:2 * ROWS_OUT].reshape(D_MODEL, D_MODEL)
    w_down = row_blocks[:, 2 * ROWS_OUT:].reshape(D_FF, D_MODEL)
    x1 = _mm(mixed, w_out, epilogue=_add_residual, extras=(x0,), name="out_proj")
    h2 = _rms_fwd(x1, w["g_mlp"], "rms_mlp")
    u, a = _mm(h2, w_up, out_dtypes=(BF16, BF16), name="mlp_up",
               epilogue=lambda acc: (acc, jnp.square(jnp.maximum(acc, 0.0))))
    x2 = _mm(a, w_down, epilogue=_add_residual, extras=(x1,), name="mlp_down")
    h3 = _rms_fwd(x2, w["g_ple"], "rms_ple")
    pp = _mm(p2, w["w_pp"], name="ple_proj")
    gpre, x3 = _mm(h3, w_pg, out_dtypes=(F32, F32), extras=(x2, pp), name="ple_gate",
                   epilogue=lambda acc, res, proj: (acc, res + _sigmoid(acc) * proj))
    dx3, dg_final, loss = _loss_head(x3, w["g_final"], tgt)

    dgpre, dpp = _ple_bwd(dx3, gpre, pp)
    dw_pp = _mm(p2, dpp, ta=True, out_dtypes=(BF16,), name="d_w_ple_proj")
    dw_pg = _mm(h3, dgpre, ta=True, out_dtypes=(BF16,), name="d_w_ple_gate")
    dh3 = _mm(dgpre, w_pg, tb=True, name="d_h_ple")
    dx2, dg_ple = _rms_bwd(x2, w["g_ple"], dh3, dx3, "rms_ple_bwd")
    du = _mm(dx2, w_down, tb=True, out_dtypes=(BF16,), extras=(u,), name="d_mlp_hidden",
             epilogue=lambda acc, uu: (acc * (2.0 * jnp.maximum(uu.astype(F32), 0.0)),))
    dw_down = _mm(a, dx2, ta=True, out_dtypes=(BF16,), name="d_w_down")
    dw_up = _mm(h2, du, ta=True, out_dtypes=(BF16,), name="d_w_up")
    dh2 = _mm(du, w_up, tb=True, name="d_h_mlp")
    dx1, dg_mlp = _rms_bwd(x1, w["g_mlp"], dh2, dx2, "rms_mlp_bwd")
    dmixed = _mm(dx1, w_out, tb=True, out_dtypes=(BF16,), name="d_mixed")
    dw_out = _mm(mixed, dx1, ta=True, out_dtypes=(BF16,), name="d_w_out")

    d_rows = jnp.concatenate([dw_out.reshape(N_DEV, ROWS_OUT, D_MODEL), dw_pg.reshape(N_DEV, ROWS_OUT, D_MODEL),
                              dw_down.reshape(N_DEV, D_FF // N_DEV, D_MODEL)], axis=1)
    (ddq, ddk, ddv, dza, dzb_, dgz, dgb, dal, ddt, ddn, recv_rows) = _dn_bwd(
        act, zb, zs, w["alog_b"], w["dtb_b"], w["dn_norm"], dn_states, dmixed, bsz, nc,
        rider=_scatter_rider(d_rows))
    (gdq, gdk, gdv, dgg, dga, dlr, dw2, dgla_b, dgla_norm, recv_up) = _gla_bwd(
        zb, zs, w["w2p"], w["gla_b"], w["gla_norm"], gla_states, dmixed, bsz, nc,
        rider=_scatter_rider(_by_cols(dw_up)))
    dqkv, dconv = _conv_bwd(zb, w["conv"], jnp.concatenate([ddq, ddk, ddv], axis=1), bsz, seq)
    dzb = jnp.concatenate([gdq, gdk, gdv, dgg, dqkv, dgz, dga, dgb], axis=1)
    dzs = jnp.concatenate([dlr, dza, dzb_], axis=1)
    dwb = _mm(h, dzb, ta=True, out_dtypes=(BF16,), name="d_w_in")
    dws = _mm(h, dzs, ta=True, out_dtypes=(BF16,), name="d_w_in_gates")
    dw_in = jnp.concatenate([dwb[:, :WI_LR], dws[:, ZS_LR:ZS_LR + GLA_LOWRANK],
                             dwb[:, ZB_DQKV:ZB_GA], dws[:, ZS_DA:ZS_DA + DN_HEADS],
                             dws[:, ZS_DB:ZS_DB + DN_HEADS], dwb[:, ZB_GA:]], axis=1)
    dh_gates = _mm(dzs, w["ws"], tb=True, name="d_h_mix_gates")
    dh, recv_in = _mm(dzb, w["wb"], tb=True, epilogue=_add_residual, extras=(dh_gates,), name="d_h_mix",
                      rider=_scatter_rider(_by_cols(dw_in)))
    gx, dg_mix = _rms_bwd(x0, w["g_mix"], dh, dx1, "rms_mix_bwd")

    dgla_w2 = dw2[:, :GLA_LOWRANK, :].transpose(1, 0, 2).reshape(GLA_LOWRANK, GLA_QK)
    return dict(
        loss=loss[0, 0], grad_x=gx.reshape(x.shape), recv_in=recv_in, recv_up=recv_up, recv_rows=recv_rows,
        w_ple_proj=dw_pp,
        g_mix=dg_mix, gla_b=dgla_b.reshape(1, GLA_QK), gla_norm=dgla_norm, dn_norm=ddn, g_mlp=dg_mlp, g_ple=dg_ple,
        g_final=dg_final, gla_w2=dgla_w2, dn_conv=dconv,
        a_log_lanes=dal.reshape(DN_HEADS, LANE), dt_bias_lanes=ddt.reshape(DN_HEADS, LANE))


def _first_weights(g_mix, w_in, gla_w2, gla_b, gla_norm, dn_conv, dn_a_log, dn_dt_bias, dn_norm, g_mlp, g_ple,
                   w_ple_proj, g_final):
    w_in_full = _from_cols(_all_gather(w_in[0].astype(BF16), "gather_w_in"))
    w_pp = _all_gather(w_ple_proj[0].astype(BF16), "gather_w_ple_proj")
    small = _all_gather(_pack_rows([gla_w2[0], dn_conv[0]]), "gather_w_small")
    n_w2 = GLA_LOWRANK * GLA_QK // N_DEV // LANE
    n_cv = DN_CONV * DN_QKV // N_DEV // LANE
    w2 = small[:, :n_w2].reshape(N_DEV, GLA_LOWRANK, GLA_QK // N_DEV).transpose(1, 0, 2).reshape(GLA_LOWRANK, GLA_QK)
    conv = small[:, n_w2:n_w2 + n_cv].reshape(N_DEV, DN_CONV, DN_QKV // N_DEV).transpose(1, 0, 2).reshape(DN_CONV, DN_QKV)

    def lane_pad(wcols):
        return jnp.pad(wcols, ((0, 0), (0, LANE - wcols.shape[1])))

    return dict(
        wb=jnp.concatenate([w_in_full[:, :WI_LR], w_in_full[:, WI_DQKV:WI_DA], w_in_full[:, WI_GA:]], axis=1),
        ws=jnp.concatenate([lane_pad(w_in_full[:, WI_LR:WI_DQKV]), lane_pad(w_in_full[:, WI_DA:WI_DB]),
                            lane_pad(w_in_full[:, WI_DB:WI_GA])], axis=1),
        w_pp=_from_cols(w_pp),
        w2p=jnp.pad(w2, ((0, LANE - GLA_LOWRANK), (0, 0))), conv=conv,
        alog_b=jnp.broadcast_to(dn_a_log[0][:, None, None], (DN_HEADS, 1, LANE)),
        dtb_b=jnp.broadcast_to(dn_dt_bias[0][:, None, None], (DN_HEADS, 1, LANE)),
        g_mix=g_mix, gla_b=gla_b, gla_norm=gla_norm, dn_norm=dn_norm, g_mlp=g_mlp, g_ple=g_ple,
        g_final=g_final.reshape(1, D_MODEL))


def kernel(x, p, g_mix, w_in, gla_w2, gla_b, gla_norm, dn_conv, dn_a_log, dn_dt_bias, dn_norm, w_out, g_mlp, w_up, w_down, g_ple, w_ple_gate, w_ple_proj, g_final, loss_target, m_g_mix, m_w_in, m_gla_w2, m_gla_b, m_gla_norm, m_dn_conv, m_dn_a_log, m_dn_dt_bias, m_dn_norm, m_w_out, m_g_mlp, m_w_up, m_w_down, m_g_ple, m_w_ple_gate, m_w_ple_proj, m_g_final, v_g_mix, v_w_in, v_gla_w2, v_gla_b, v_gla_norm, v_dn_conv, v_dn_a_log, v_dn_dt_bias, v_dn_norm, v_w_out, v_g_mlp, v_w_up, v_w_down, v_g_ple, v_w_ple_gate, v_w_ple_proj, v_g_final):
    names = ["g_mix", "w_in", "gla_w2", "gla_b", "gla_norm", "dn_conv", "dn_a_log", "dn_dt_bias", "dn_norm", "w_out",
             "g_mlp", "w_up", "w_down", "g_ple", "w_ple_gate", "w_ple_proj", "g_final"]
    ws = dict(zip(names, (g_mix, w_in, gla_w2, gla_b, gla_norm, dn_conv, dn_a_log, dn_dt_bias, dn_norm, w_out, g_mlp,
                          w_up, w_down, g_ple, w_ple_gate, w_ple_proj, g_final)))
    ms = dict(zip(names, (m_g_mix, m_w_in, m_gla_w2, m_gla_b, m_gla_norm, m_dn_conv, m_dn_a_log, m_dn_dt_bias,
                          m_dn_norm, m_w_out, m_g_mlp, m_w_up, m_w_down, m_g_ple, m_w_ple_gate, m_w_ple_proj,
                          m_g_final)))
    vs = dict(zip(names, (v_g_mix, v_w_in, v_gla_w2, v_gla_b, v_gla_norm, v_dn_conv, v_dn_a_log, v_dn_dt_bias,
                          v_dn_norm, v_w_out, v_g_mlp, v_w_up, v_w_down, v_g_ple, v_w_ple_gate, v_w_ple_proj,
                          v_g_final)))
    me = 4 * lax.axis_index("x") + 2 * lax.axis_index("y") + lax.axis_index("c")

    first = _first_weights(g_mix, w_in, gla_w2, gla_b, gla_norm, dn_conv, dn_a_log, dn_dt_bias, dn_norm, g_mlp,
                           g_ple, w_ple_proj, g_final)
    rows_shard = jnp.concatenate([w_out[0], w_ple_gate[0], w_down[0]], axis=0).astype(BF16)
    r = _local_step(x, p[0], loss_target, first, w_up[0].astype(BF16), rows_shard)
    loss = lax.psum(r["loss"], ("x", "y", "c"))

    grads, deltas, new_m, new_v = {}, {}, {}, {}

    def big(name, parts, row0=0):
        g, d, nm, nv = _adamw_reduce(ws[name][0], ms[name][0], vs[name][0], parts, row0, "adamw_" + name)
        grads[name], deltas[name], new_m[name], new_v[name] = g[None], d[None], nm[None], nv[None]

    big("w_in", r["recv_in"])
    big("w_up", r["recv_up"])
    big("w_ple_proj", _all_to_all(_by_cols(r["w_ple_proj"]), "scatter_d_w_ple_proj"))
    big("w_out", r["recv_rows"], 0)
    big("w_ple_gate", r["recv_rows"], ROWS_OUT)
    big("w_down", r["recv_rows"], 2 * ROWS_OUT)

    vec_names = ["g_mix", "gla_b", "gla_norm", "dn_norm", "g_mlp", "g_ple", "g_final"]
    packed = _pack_rows([r[n] for n in vec_names] + [r["gla_w2"], r["dn_conv"]])
    lane_rows = packed.shape[0]
    packed = jnp.concatenate([packed, r["a_log_lanes"], r["dt_bias_lanes"]], axis=0)
    total = _small_reduce(_all_gather(packed, "gather_small_grads"), lane_rows)
    parts = _unpack_rows(total, [r[n].shape for n in vec_names] + [r["gla_w2"].shape, r["dn_conv"].shape])
    sg = dict(zip(vec_names, parts[:len(vec_names)]))
    sg["g_final"] = sg["g_final"].reshape(D_MODEL)
    sg["gla_w2"] = lax.dynamic_slice_in_dim(parts[-2], me * (GLA_QK // N_DEV), GLA_QK // N_DEV, axis=1)
    sg["dn_conv"] = lax.dynamic_slice_in_dim(parts[-1], me * (DN_QKV // N_DEV), DN_QKV // N_DEV, axis=1)
    sg["dn_a_log"] = total[lane_rows:lane_rows + DN_HEADS, 0]
    sg["dn_dt_bias"] = total[lane_rows + DN_HEADS:lane_rows + 2 * DN_HEADS, 0]
    small_names = vec_names + ["gla_w2", "dn_conv", "dn_a_log", "dn_dt_bias"]
    shapes = [ws[n].shape for n in small_names]
    d_s, m_s, v_s = _adamw_small(_pack_rows([ws[n] for n in small_names]), _pack_rows([ms[n] for n in small_names]),
                                 _pack_rows([vs[n] for n in small_names]), _pack_rows([sg[n] for n in small_names]))
    for n, d, nm, nv in zip(small_names, _unpack_rows(d_s, shapes), _unpack_rows(m_s, shapes), _unpack_rows(v_s, shapes)):
        grads[n], deltas[n], new_m[n], new_v[n] = sg[n].reshape(ws[n].shape), d, nm, nv

    return (loss, r["grad_x"], *[grads[n] for n in names], *[deltas[n] for n in names],
            *[new_m[n] for n in names], *[new_v[n] for n in names])
```

```python
import functools
import math

import jax
import jax.numpy as jnp
from jax import lax
from jax.experimental import pallas as pl
from jax.experimental.pallas import tpu as pltpu

F32 = jnp.float32
BF16 = jnp.bfloat16
HIGHEST = lax.Precision.HIGHEST

N_DEV = 8
D_MODEL = 2048
CHUNK = 64
PLE_DIM = 256
EPS = 1e-6
GLA_HEADS = 4
GLA_DK = 256
GLA_DV = 512
GLA_LOWRANK = 16
GLA_TAU = 16.0
DN_HEADS = 16
DN_D = 128
DN_CONV = 4
D_FF = 4 * D_MODEL
GLA_QK = GLA_HEADS * GLA_DK
GLA_V = GLA_HEADS * GLA_DV
DN_QKV = 3 * DN_HEADS * DN_D
D_IN = 2 * GLA_QK + 2 * GLA_V + GLA_LOWRANK + DN_QKV + D_MODEL + 2 * DN_HEADS + 2 * D_MODEL
D_IN_SHARD = D_IN // N_DEV

ADAM_LR = 0.001
ADAM_B1 = 0.9
ADAM_B2 = 0.999
ADAM_EPS = 1e-08
ADAM_WD = 0.01
ADAM_STEP = 10

LANE = 128
ZB_GQ, ZB_GK, ZB_GV, ZB_GG = 0, 1024, 2048, 4096
ZB_DQKV, ZB_DZ, ZB_GA, ZB_GB = 6144, 12288, 14336, 16384
ZB_W = 18432
ZS_LR, ZS_DA, ZS_DB = 0, 128, 256
ZS_W = 384
WI_LR = 2 * GLA_QK + 2 * GLA_V
WI_DQKV = WI_LR + GLA_LOWRANK
WI_DA = WI_DQKV + DN_QKV + D_MODEL
WI_DB = WI_DA + DN_HEADS
WI_GA = WI_DB + DN_HEADS

VMEM_LIMIT = 56 * 1024 * 1024

NN = (((1,), (0,)), ((), ()))
NT = (((1,), (1,)), ((), ()))
TN = (((0,), (0,)), ((), ()))


def _bdot(a, b, dims):
    return lax.dot_general(a.astype(BF16), b.astype(BF16), dims, preferred_element_type=F32)


def _split3(x):
    hi = x.astype(BF16)
    rest = x - hi.astype(F32)
    mid = rest.astype(BF16)
    return hi, mid, (rest - mid.astype(F32)).astype(BF16)


def _dot01(x, m, dims, x_first):
    m = m.astype(BF16)
    out = None
    for piece in _split3(x):
        d = lax.dot_general(piece, m, dims, preferred_element_type=F32) if x_first else \
            lax.dot_general(m, piece, dims, preferred_element_type=F32)
        out = d if out is None else out + d
    return out


@functools.partial(jax.custom_vjp, nondiff_argnums=(2, 3))
def _pick_dot(x, m, dims, dims_t):
    return _dot01(x, m, dims, True)


def _pick_dot_fwd(x, m, dims, dims_t):
    return _dot01(x, m, dims, True), m


def _pick_dot_bwd(dims, dims_t, m, ct):
    return _dot01(ct, m, dims_t, True), jnp.zeros_like(m)


_pick_dot.defvjp(_pick_dot_fwd, _pick_dot_bwd)


@functools.partial(jax.custom_vjp, nondiff_argnums=(2, 3))
def _left_dot(m, x, dims, dims_t):
    return _dot01(x, m, dims, False)


def _left_dot_fwd(m, x, dims, dims_t):
    return _dot01(x, m, dims, False), m


def _left_dot_bwd(dims, dims_t, m, ct):
    return jnp.zeros_like(m), _dot01(ct, m, dims_t, False)


_left_dot.defvjp(_left_dot_fwd, _left_dot_bwd)


def _dot3(a, b, dims):
    ah, bh = a.astype(BF16), b.astype(BF16)
    al, bl = (a - ah.astype(F32)).astype(BF16), (b - bh.astype(F32)).astype(BF16)
    dot = functools.partial(lax.dot_general, dimension_numbers=dims, preferred_element_type=F32)
    return dot(ah, bh) + (dot(ah, bl) + dot(al, bh))


def _sigmoid(x):
    return 1.0 / (1.0 + jnp.exp(-x))


def _silu(x):
    return x * _sigmoid(x)


def _softplus(x):
    return jnp.maximum(x, 0.0) + jnp.log(1.0 + jnp.exp(-jnp.abs(x)))


def _iota2(shape, dim):
    return lax.broadcasted_iota(jnp.int32, shape, dim)


def _cparams(sem=None):
    return pltpu.CompilerParams(dimension_semantics=sem, vmem_limit_bytes=VMEM_LIMIT)


BNN = (((2,), (1,)), ((0,), (0,)))
BNT = (((2,), (2,)), ((0,), (0,)))
BTN = (((1,), (1,)), ((0,), (0,)))


def _gla_chunk(st, q, k, v, lr, w2, b, gg, ga, gn):
    hb, c, _ = q.shape
    incl = (_iota2((c, c), 0) >= _iota2((c, c), 1))[None]
    tri = jnp.broadcast_to(incl.astype(F32), (hb, c, c))
    q = q.astype(F32) * (GLA_DK ** -0.5)
    k = k.astype(F32)
    v = v.astype(F32)
    lr_b = jnp.broadcast_to(lr[None], (hb,) + lr.shape)
    lf = -_softplus(-(_bdot(lr_b, w2, BNN) + b)) / GLA_TAU
    bcum = _left_dot(tri, lf, BNN, BTN)
    blast = jnp.sum(lf, axis=1, keepdims=True)
    q_in = q * jnp.exp(bcum)
    k_in = k * jnp.exp(-bcum)
    a = jnp.where(incl, _bdot(q_in, k_in, BNT), 0.0)
    o = _bdot(a, v, BNN) + _bdot(q_in, st, BNT)
    k_dec = k * jnp.exp(blast - bcum)
    st_new = st * jnp.exp(blast) + _bdot(v, k_dec, BTN)
    on = o * lax.rsqrt(jnp.mean(o * o, axis=-1, keepdims=True) + EPS) * gn
    res = _sigmoid(ga.astype(F32)) * on * _silu(gg.astype(F32))
    return res, st_new


def _tri_inv_raw(a):
    _, c, _ = a.shape
    eye = (_iota2((c, c), 0) == _iota2((c, c), 1)).astype(F32)[None]
    x = a
    p = eye - a
    for _ in range(5):
        x = _dot3(x, x, BNN)
        p = _dot3(p, eye + x, BNN)
    return p


@jax.custom_vjp
def _tri_inv(a):
    return _tri_inv_raw(a)


def _tri_inv_fwd(a):
    t = _tri_inv_raw(a)
    return t, t


def _tri_inv_bwd(t, dt):
    return (-_dot3(_dot3(t, dt, BTN), t, BNT),)


_tri_inv.defvjp(_tri_inv_fwd, _tri_inv_bwd)


def _dn_chunk(s, qr, kr, vr, za, zb, alog, dtb, gz, gb, dn, h0, differentiable):
    hb, c, _ = qr.shape
    row, col = _iota2((c, c), 0), _iota2((c, c), 1)
    incl = (row >= col)[None]
    strict = (row > col)[None]
    tri = jnp.broadcast_to(incl.astype(F32), (hb, c, c))
    eye = (row == col).astype(F32)[None]
    ones_cc = jnp.ones((hb, c, c), F32)
    sel_shape = (hb, LANE, LANE)
    sel = (lax.broadcasted_iota(jnp.int32, sel_shape, 1) == h0 + lax.broadcasted_iota(jnp.int32, sel_shape, 0))
    sel = sel.astype(F32)
    lane0 = (lax.broadcasted_iota(jnp.int32, (hb, LANE, c), 1) == 0).astype(F32)

    def l2n(t):
        return t * lax.rsqrt(jnp.sum(t * t, axis=-1, keepdims=True) + EPS)

    q = l2n(qr.astype(F32)) * (DN_D ** -0.5)
    k = l2n(kr.astype(F32))
    v = vr.astype(F32)
    za_b = jnp.broadcast_to(za[None], (hb,) + za.shape)
    zb_b = jnp.broadcast_to(zb[None], (hb,) + zb.shape)
    g = -jnp.exp(alog) * _softplus(_pick_dot(za_b, sel, BNN, BNT) + dtb)
    beta = _sigmoid(_pick_dot(zb_b, sel, BNN, BNT))
    gcum = _left_dot(tri, g, BNN, BTN)
    glast = jnp.sum(g, axis=1, keepdims=True)
    cm = _pick_dot(gcum, lane0, BNN, BNT)
    rm = _left_dot(ones_cc, cm * eye, BNN, BTN)
    dec = jnp.exp(jnp.where(incl, cm - rm, -1e30))
    kb = k * beta
    a = jnp.where(strict, _bdot(kb, k, BNT) * dec, 0.0)
    t = _tri_inv(a) if differentiable else _tri_inv_raw(a)
    egc = jnp.exp(gcum)
    u = _bdot(t, v * beta, BNN)
    w = _bdot(t, kb * egc, BNN)
    attn = jnp.where(incl, _bdot(q, k, BNT) * dec, 0.0)
    q_dec = q * egc
    k_dec = k * jnp.exp(glast - gcum)
    v_new = u - _bdot(w, s, BNN)
    o = _bdot(q_dec, s, BNN) + _bdot(attn, v_new, BNN)
    s_new = s * jnp.exp(glast) + _bdot(k_dec, v_new, BTN)
    on = o * lax.rsqrt(jnp.mean(o * o, axis=-1, keepdims=True) + EPS) * dn
    res = _sigmoid(gb.astype(F32)) * on * _silu(gz.astype(F32))
    return res, s_new


def _heads(ref, n_heads, width):
    return jnp.stack([ref[:, j * width:(j + 1) * width] for j in range(n_heads)], axis=0)


def _gla_specs(nc, reverse):
    def rows(b, n):
        return b * nc + ((nc - 1 - n) if reverse else n)

    qk = lambda base: pl.BlockSpec((CHUNK, GLA_QK), lambda b, n: (rows(b, n), base // GLA_QK))
    vv = lambda base: pl.BlockSpec((CHUNK, GLA_V), lambda b, n: (rows(b, n), base // GLA_V))
    lr = lambda c: pl.BlockSpec((CHUNK, LANE), lambda b, n: (rows(b, n), c))
    full = lambda shape: pl.BlockSpec(shape, lambda b, n: (0,) * len(shape))
    return rows, qk, vv, lr, full


def _gla_inputs(q_ref, k_ref, v_ref, gg_ref, ga_ref, lr_ref, w2_ref, b_ref, gn_ref):
    return (_heads(q_ref, GLA_HEADS, GLA_DK), _heads(k_ref, GLA_HEADS, GLA_DK), _heads(v_ref, GLA_HEADS, GLA_DV),
            lr_ref[...], _heads(w2_ref, GLA_HEADS, GLA_DK), _heads(b_ref, GLA_HEADS, GLA_DK),
            _heads(gg_ref, GLA_HEADS, GLA_DV), _heads(ga_ref, GLA_HEADS, GLA_DV), gn_ref[...])


def _gla_fwd(zb, zs, w2p, gla_b, gla_norm, bsz, nc):
    t = zb.shape[0]
    rows, qk, vv, lr, full = _gla_specs(nc, False)

    def body(q_ref, k_ref, v_ref, gg_ref, ga_ref, lr_ref, w2_ref, b_ref, gn_ref, o_ref, st_ref, state):
        st = jnp.where(pl.program_id(1) > 0, state[...], 0.0)
        st_ref[0, 0] = st
        res, st_new = _gla_chunk(st, *_gla_inputs(q_ref, k_ref, v_ref, gg_ref, ga_ref, lr_ref, w2_ref, b_ref, gn_ref))
        for j in range(GLA_HEADS):
            o_ref[:, j * GLA_DV:(j + 1) * GLA_DV] = res[j]
        state[...] = st_new

    return pl.pallas_call(
        body, name="gla_fwd", grid=(bsz, nc),
        in_specs=[qk(ZB_GQ), qk(ZB_GK), vv(ZB_GV), vv(ZB_GG), vv(ZB_GA), lr(ZS_LR // LANE),
                  full((LANE, GLA_QK)), full((1, GLA_QK)), full((1, GLA_DV))],
        out_specs=[vv(0), pl.BlockSpec((1, 1, GLA_HEADS, GLA_DV, GLA_DK), lambda b, n: (b, n, 0, 0, 0))],
        out_shape=[jax.ShapeDtypeStruct((t, GLA_V), F32),
                   jax.ShapeDtypeStruct((bsz, nc, GLA_HEADS, GLA_DV, GLA_DK), F32)],
        scratch_shapes=[pltpu.VMEM((GLA_HEADS, GLA_DV, GLA_DK), F32)],
        compiler_params=_cparams(("arbitrary", "arbitrary")),
    )(zb, zb, zb, zb, zb, zs, w2p, gla_b, gla_norm)


def _gla_bwd(zb, zs, w2p, gla_b, gla_norm, states, dmixed, bsz, nc, rider=None):
    t = zb.shape[0]
    rows, qk, vv, lr, full = _gla_specs(nc, True)

    def body(q_ref, k_ref, v_ref, gg_ref, ga_ref, lr_ref, w2_ref, b_ref, gn_ref, st_ref, dm_ref,
             dq_ref, dk_ref, dv_ref, dgg_ref, dga_ref, dlr_ref, dw2_ref, db_ref, dgn_ref, dstate):
        b, n = pl.program_id(0), pl.program_id(1)

        @pl.when((b == 0) & (n == 0))
        def _():
            dw2_ref[...] = jnp.zeros_like(dw2_ref)
            db_ref[...] = jnp.zeros_like(db_ref)
            dgn_ref[...] = jnp.zeros_like(dgn_ref)

        _, vjp = jax.vjp(_gla_chunk, st_ref[0, 0],
                         *_gla_inputs(q_ref, k_ref, v_ref, gg_ref, ga_ref, lr_ref, w2_ref, b_ref, gn_ref))
        dst_in = jnp.where(n > 0, dstate[...], 0.0)
        dst, dq, dk, dv, dlr, dw2, db, dgg, dga, dgn = vjp((_heads(dm_ref, GLA_HEADS, GLA_DV).astype(F32), dst_in))
        dstate[...] = dst
        for j in range(GLA_HEADS):
            dq_ref[:, j * GLA_DK:(j + 1) * GLA_DK] = dq[j].astype(dq_ref.dtype)
            dk_ref[:, j * GLA_DK:(j + 1) * GLA_DK] = dk[j].astype(dk_ref.dtype)
            dv_ref[:, j * GLA_DV:(j + 1) * GLA_DV] = dv[j].astype(dv_ref.dtype)
            dgg_ref[:, j * GLA_DV:(j + 1) * GLA_DV] = dgg[j].astype(dgg_ref.dtype)
            dga_ref[:, j * GLA_DV:(j + 1) * GLA_DV] = dga[j].astype(dga_ref.dtype)
        dlr_ref[...] = dlr
        dw2_ref[...] += dw2
        db_ref[...] += db
        dgn_ref[...] += dgn

    return _hosted_call(
        body, rider, name="gla_bwd", grid=(bsz, nc),
        in_specs=[qk(ZB_GQ), qk(ZB_GK), vv(ZB_GV), vv(ZB_GG), vv(ZB_GA), lr(ZS_LR // LANE),
                  full((LANE, GLA_QK)), full((1, GLA_QK)), full((1, GLA_DV)),
                  pl.BlockSpec((1, 1, GLA_HEADS, GLA_DV, GLA_DK), lambda b, n: (b, nc - 1 - n, 0, 0, 0)),
                  vv(0)],
        out_specs=[qk(0), qk(0), vv(0), vv(0), vv(0), lr(0),
                   full((GLA_HEADS, LANE, GLA_DK)), full((GLA_HEADS, 1, GLA_DK)), full((1, GLA_DV))],
        out_shape=[jax.ShapeDtypeStruct((t, GLA_QK), BF16), jax.ShapeDtypeStruct((t, GLA_QK), BF16),
                   jax.ShapeDtypeStruct((t, GLA_V), BF16), jax.ShapeDtypeStruct((t, GLA_V), BF16),
                   jax.ShapeDtypeStruct((t, GLA_V), BF16), jax.ShapeDtypeStruct((t, LANE), F32),
                   jax.ShapeDtypeStruct((GLA_HEADS, LANE, GLA_DK), F32),
                   jax.ShapeDtypeStruct((GLA_HEADS, 1, GLA_DK), F32),
                   jax.ShapeDtypeStruct((1, GLA_DV), F32)],
        scratch_shapes=[pltpu.VMEM((GLA_HEADS, GLA_DV, GLA_DK), F32)],
        args=(zb, zb, zb, zb, zb, zs, w2p, gla_b, gla_norm, states, dmixed))


DN_HB = DN_HEADS


def _dn_specs(nc, reverse):
    wide = DN_HB * DN_D

    def rows(b, n, h):
        return b * nc + ((nc - 1 - n) if reverse else n)

    def col(base):
        return pl.BlockSpec((CHUNK, wide), lambda b, n, h: (rows(b, n, h), base // wide + h))

    def fixed(c):
        return pl.BlockSpec((CHUNK, LANE), lambda b, n, h: (rows(b, n, h), c))

    head = pl.BlockSpec((DN_HB, 1, LANE), lambda b, n, h: (h, 0, 0))
    return rows, col, fixed, head


def _lanes(j):
    return slice(j * DN_D, (j + 1) * DN_D)


def _by_head(ref):
    return jnp.stack([ref[:, _lanes(j)] for j in range(DN_HB)], axis=0)


def _dn_fwd(act, zb, zs, alog_b, dtb_b, dn_norm, mix_gla, bsz, nc, rider=None):
    t = zb.shape[0]
    rows, col, fixed, head = _dn_specs(nc, False)

    def body(q_ref, k_ref, v_ref, za_ref, zb_ref, al_ref, dt_ref, gz_ref, gb_ref, dn_ref, mg_ref,
             o_ref, st_ref, state):
        n, hb = pl.program_id(1), pl.program_id(2)
        heads = pl.ds(hb * DN_HB, DN_HB)
        s = jnp.where(n > 0, state[heads], 0.0)
        st_ref[0, 0] = s
        res, s_new = _dn_chunk(s, _by_head(q_ref), _by_head(k_ref), _by_head(v_ref), za_ref[...], zb_ref[...],
                               al_ref[...], dt_ref[...], _by_head(gz_ref), _by_head(gb_ref), dn_ref[...],
                               hb * DN_HB, False)
        for j in range(DN_HB):
            o_ref[:, _lanes(j)] = (res[j] + mg_ref[:, _lanes(j)]).astype(o_ref.dtype)
        state[heads] = s_new

    return _hosted_call(
        body, rider, name="dn_fwd", grid=(bsz, nc, DN_HEADS // DN_HB),
        in_specs=[col(0), col(DN_HEADS * DN_D), col(2 * DN_HEADS * DN_D),
                  fixed(ZS_DA // LANE), fixed(ZS_DB // LANE), head, head,
                  col(ZB_DZ), col(ZB_GB), pl.BlockSpec((1, DN_D), lambda b, n, h: (0, 0)), col(0)],
        out_specs=[col(0), pl.BlockSpec((1, 1, DN_HB, DN_D, DN_D), lambda b, n, h: (b, n, h, 0, 0))],
        out_shape=[jax.ShapeDtypeStruct((t, D_MODEL), BF16),
                   jax.ShapeDtypeStruct((bsz, nc, DN_HEADS, DN_D, DN_D), F32)],
        scratch_shapes=[pltpu.VMEM((DN_HEADS, DN_D, DN_D), F32)],
        args=(act, act, act, zs, zs, alog_b, dtb_b, zb, zb, dn_norm, mix_gla))


def _dn_bwd(act, zb, zs, alog_b, dtb_b, dn_norm, states, dmixed, bsz, nc, rider=None):
    t = zb.shape[0]
    rows, col, fixed, head = _dn_specs(nc, True)

    def body(q_ref, k_ref, v_ref, za_ref, zb_ref, al_ref, dt_ref, gz_ref, gb_ref, dn_ref, st_ref, dm_ref,
             dact_ref, dza_ref, dzb_ref, dgz_ref, dgb_ref, dal_ref, ddt_ref, ddn_ref, dstate):
        b, n, hb = pl.program_id(0), pl.program_id(1), pl.program_id(2)

        @pl.when((b == 0) & (n == 0) & (hb == 0))
        def _():
            dal_ref[...] = jnp.zeros_like(dal_ref)
            ddt_ref[...] = jnp.zeros_like(ddt_ref)
            ddn_ref[...] = jnp.zeros_like(ddn_ref)

        heads = pl.ds(hb * DN_HB, DN_HB)
        fn = functools.partial(_dn_chunk, h0=hb * DN_HB, differentiable=True)
        _, vjp = jax.vjp(fn, st_ref[0, 0], _by_head(q_ref), _by_head(k_ref), _by_head(v_ref), za_ref[...],
                         zb_ref[...], al_ref[...], dt_ref[...], _by_head(gz_ref), _by_head(gb_ref), dn_ref[...])
        ds_in = jnp.where(n > 0, dstate[heads], 0.0)
        ds, dq, dk, dv, dza, dzb, dal, ddt, dgz, dgb, ddn = vjp((_by_head(dm_ref).astype(F32), ds_in))
        dstate[heads] = ds
        for j in range(DN_HB):
            for part, d in enumerate((dq, dk, dv)):
                dact_ref[:, pl.ds(part * DN_HEADS * DN_D + j * DN_D, DN_D)] = d[j]
            dgz_ref[:, _lanes(j)] = dgz[j].astype(dgz_ref.dtype)
            dgb_ref[:, _lanes(j)] = dgb[j].astype(dgb_ref.dtype)
        dal_ref[heads] += dal
        ddt_ref[heads] += ddt

        @pl.when(hb == 0)
        def _():
            dza_ref[...] = dza
            dzb_ref[...] = dzb

        @pl.when(hb > 0)
        def _():
            dza_ref[...] += dza
            dzb_ref[...] += dzb

        ddn_ref[...] += ddn

    full = lambda shape: pl.BlockSpec(shape, lambda b, n, h: (0,) * len(shape))
    return _hosted_call(
        body, rider, name="dn_bwd", grid=(bsz, nc, DN_HEADS // DN_HB),
        in_specs=[col(0), col(DN_HEADS * DN_D), col(2 * DN_HEADS * DN_D),
                  fixed(ZS_DA // LANE), fixed(ZS_DB // LANE), head, head,
                  col(ZB_DZ), col(ZB_GB), pl.BlockSpec((1, DN_D), lambda b, n, h: (0, 0)),
                  pl.BlockSpec((1, 1, DN_HB, DN_D, DN_D), lambda b, n, h: (b, nc - 1 - n, h, 0, 0)), col(0)],
        out_specs=[pl.BlockSpec((CHUNK, DN_QKV), lambda b, n, h: (rows(b, n, h), 0)), fixed(0), fixed(0), col(0), col(0),
                   full((DN_HEADS, 1, LANE)), full((DN_HEADS, 1, LANE)), full((1, DN_D))],
        out_shape=[jax.ShapeDtypeStruct((t, DN_QKV), F32),
                   jax.ShapeDtypeStruct((t, LANE), F32), jax.ShapeDtypeStruct((t, LANE), F32),
                   jax.ShapeDtypeStruct((t, D_MODEL), BF16), jax.ShapeDtypeStruct((t, D_MODEL), BF16),
                   jax.ShapeDtypeStruct((DN_HEADS, 1, LANE), F32), jax.ShapeDtypeStruct((DN_HEADS, 1, LANE), F32),
                   jax.ShapeDtypeStruct((1, DN_D), F32)],
        scratch_shapes=[pltpu.VMEM((DN_HEADS, DN_D, DN_D), F32)],
        args=(act, act, act, zs, zs, alog_b, dtb_b, zb, zb, dn_norm, states, dmixed))


MM_VMEM_BUDGET = 40 * 1024 * 1024
MM_TILE_PREF = (1024, 1024, 2048)


def _divisor_tile(n, cap):
    if n <= cap:
        return n
    for c in range(cap - cap % LANE, 0, -LANE):
        if n % c == 0:
            return c
    return n


def _mm_tiles(m, n, kd, a_bytes, b_bytes, mn_bytes):
    tm, tn, tk = (_divisor_tile(d, c) for d, c in zip((m, n, kd), MM_TILE_PREF))

    def need(tm, tn, tk):
        acc = 0 if tk == kd else 4 * tm * tn
        return 2 * (tm * tk * a_bytes + tk * tn * b_bytes + tm * tn * mn_bytes) + acc + 4 * tm * tn

    while need(tm, tn, tk) > MM_VMEM_BUDGET:
        if tk > 512 and tk * max(tm * a_bytes, tn * b_bytes) >= tm * tn * mn_bytes:
            tk = _divisor_tile(kd, tk // 2)
        elif tn >= tm and tn > LANE:
            tn = _divisor_tile(n, tn // 2)
        else:
            tm = _divisor_tile(m, tm // 2)
    return tm, tn, tk


def _mm(a, b, *, ta=False, tb=False, out_dtypes=(F32,), epilogue=None, extras=(), name, rider=None):
    m, kd = (a.shape[1], a.shape[0]) if ta else a.shape
    n = b.shape[0] if tb else b.shape[1]
    mn_bytes = sum(e.dtype.itemsize for e in extras) + sum(jnp.dtype(dt).itemsize for dt in out_dtypes)
    tm, tn, tk = _mm_tiles(m, n, kd, a.dtype.itemsize, b.dtype.itemsize, mn_bytes)
    nk = kd // tk
    n_ex = len(extras)
    dims = (((0,) if ta else (1,), (1,) if tb else (0,)), ((), ()))

    def finish(acc, ex_refs, out_refs):
        outs = (acc,) if epilogue is None else epilogue(acc, *[r[...] for r in ex_refs])
        for r, o in zip(out_refs, outs):
            r[...] = o.astype(r.dtype)

    def partial_product(a_ref, b_ref):
        return lax.dot_general(a_ref[...].astype(BF16), b_ref[...].astype(BF16), dims, preferred_element_type=F32)

    def body_single(*refs):
        finish(partial_product(refs[0], refs[1]), refs[2:2 + n_ex], refs[2 + n_ex:])

    def body_acc(*refs):
        acc = refs[-1]
        k = pl.program_id(2)

        @pl.when(k == 0)
        def _():
            acc[...] = partial_product(refs[0], refs[1])

        @pl.when(k > 0)
        def _():
            acc[...] += partial_product(refs[0], refs[1])

        @pl.when(k == nk - 1)
        def _():
            finish(acc[...], refs[2:2 + n_ex], refs[2 + n_ex:-1])

    a_spec = pl.BlockSpec((tk, tm), lambda i, j, k: (k, i)) if ta else pl.BlockSpec((tm, tk), lambda i, j, k: (i, k))
    b_spec = pl.BlockSpec((tn, tk), lambda i, j, k: (j, k)) if tb else pl.BlockSpec((tk, tn), lambda i, j, k: (k, j))
    mn_spec = pl.BlockSpec((tm, tn), lambda i, j, k: (i, j))
    outs = _hosted_call(
        body_single if nk == 1 else body_acc, rider, name=name, grid=(m // tm, n // tn, nk),
        in_specs=[a_spec, b_spec] + [mn_spec] * n_ex,
        out_specs=[mn_spec] * len(out_dtypes),
        out_shape=[jax.ShapeDtypeStruct((m, n), dt) for dt in out_dtypes],
        scratch_shapes=[] if nk == 1 else [pltpu.VMEM((tm, tn), F32)],
        args=(a, b, *extras))
    return outs[0] if len(outs) == 1 else outs


ROW_BLOCK = 256


def _row_spec(width=D_MODEL):
    return pl.BlockSpec((ROW_BLOCK, width), lambda i: (i, 0))


def _vec_spec(width=D_MODEL):
    return pl.BlockSpec((1, width), lambda i: (0, 0))


def _rms_fwd(x, g, name):
    def body(x_ref, g_ref, h_ref):
        xf = x_ref[...]
        h_ref[...] = (xf * lax.rsqrt(jnp.mean(xf * xf, axis=-1, keepdims=True) + EPS) * g_ref[...]).astype(BF16)

    t = x.shape[0]
    return pl.pallas_call(
        body, name=name, grid=(t // ROW_BLOCK,), in_specs=[_row_spec(), _vec_spec()], out_specs=_row_spec(),
        out_shape=jax.ShapeDtypeStruct((t, D_MODEL), BF16), compiler_params=_cparams(("parallel",)),
    )(x, g)


def _rms_bwd_math(xf, g, dh):
    rstd = lax.rsqrt(jnp.mean(xf * xf, axis=-1, keepdims=True) + EPS)
    xhat = xf * rstd
    dxhat = dh * g
    dx = rstd * (dxhat - xhat * jnp.mean(dxhat * xhat, axis=-1, keepdims=True))
    dg = jnp.sum(dh * xhat, axis=0, keepdims=True)
    return dx, dg


def _rms_bwd(x, g, dh, dres, name):
    def body(x_ref, g_ref, dh_ref, dres_ref, dx_ref, dg_ref):
        dx, dg = _rms_bwd_math(x_ref[...], g_ref[...], dh_ref[...].astype(F32))
        dx_ref[...] = dres_ref[...] + dx

        @pl.when(pl.program_id(0) == 0)
        def _():
            dg_ref[...] = jnp.zeros_like(dg_ref)

        dg_ref[...] += dg

    t = x.shape[0]
    return pl.pallas_call(
        body, name=name, grid=(t // ROW_BLOCK,),
        in_specs=[_row_spec(), _vec_spec(), _row_spec(), _row_spec()], out_specs=[_row_spec(), _vec_spec()],
        out_shape=[jax.ShapeDtypeStruct((t, D_MODEL), F32), jax.ShapeDtypeStruct((1, D_MODEL), F32)],
        compiler_params=_cparams(("arbitrary",)),
    )(x, g, dh, dres)


def _loss_head(x3, g, target):
    def body(x_ref, g_ref, t_ref, dx_ref, dg_ref, loss_ref):
        xf, gg = x_ref[...], g_ref[...]
        rstd = lax.rsqrt(jnp.mean(xf * xf, axis=-1, keepdims=True) + EPS)
        err = xf * rstd * gg - t_ref[...]
        dx, dg = _rms_bwd_math(xf, gg, err * (1.0 / D_MODEL))
        dx_ref[...] = dx

        @pl.when(pl.program_id(0) == 0)
        def _():
            dg_ref[...] = jnp.zeros_like(dg_ref)
            loss_ref[...] = jnp.zeros_like(loss_ref)

        dg_ref[...] += dg
        part = jnp.sum(jnp.sum(err * err, axis=-1, keepdims=True), axis=0, keepdims=True) * (0.5 / D_MODEL)
        loss_ref[...] += jnp.broadcast_to(part, loss_ref.shape)

    t = x3.shape[0]
    return pl.pallas_call(
        body, name="loss_head", grid=(t // ROW_BLOCK,),
        in_specs=[_row_spec(), _vec_spec(), _row_spec()], out_specs=[_row_spec(), _vec_spec(), _vec_spec(LANE)],
        out_shape=[jax.ShapeDtypeStruct((t, D_MODEL), F32), jax.ShapeDtypeStruct((1, D_MODEL), F32),
                   jax.ShapeDtypeStruct((1, LANE), F32)],
        compiler_params=_cparams(("arbitrary",)),
    )(x3, g, target)


def _ple_bwd(dx3, gpre, pp):
    def body(dx_ref, gp_ref, pp_ref, dgp_ref, dpp_ref):
        dx, sg = dx_ref[...], _sigmoid(gp_ref[...])
        dpp_ref[...] = (dx * sg).astype(BF16)
        dgp_ref[...] = (dx * pp_ref[...] * sg * (1.0 - sg)).astype(BF16)

    t = dx3.shape[0]
    return pl.pallas_call(
        body, name="ple_bwd", grid=(t // ROW_BLOCK,), in_specs=[_row_spec()] * 3, out_specs=[_row_spec()] * 2,
        out_shape=[jax.ShapeDtypeStruct((t, D_MODEL), BF16)] * 2, compiler_params=_cparams(("parallel",)),
    )(dx3, gpre, pp)


CONV_COLS = 256


def _shift_down(x, s):
    if s == 0:
        return x
    return jnp.where(_iota2(x.shape, 0) >= s, pltpu.roll(x, s, 0), 0.0)


def _shift_up(x, s):
    if s == 0:
        return x
    rows = x.shape[0]
    return jnp.where(_iota2(x.shape, 0) < rows - s, pltpu.roll(x, rows - s, 0), 0.0)


def _conv_pre(xf, w):
    return sum(_shift_down(xf, DN_CONV - 1 - j) * w[j:j + 1, :] for j in range(DN_CONV))


def _conv_fwd(zb, conv_w, bsz, seq):
    def body(x_ref, w_ref, y_ref):
        y_ref[...] = _silu(_conv_pre(x_ref[...], w_ref[...]))

    nblk = DN_QKV // CONV_COLS
    return pl.pallas_call(
        body, name="conv_fwd", grid=(bsz, nblk),
        in_specs=[pl.BlockSpec((seq, CONV_COLS), lambda b, j: (b, ZB_DQKV // CONV_COLS + j)),
                  pl.BlockSpec((DN_CONV, CONV_COLS), lambda b, j: (0, j))],
        out_specs=pl.BlockSpec((seq, CONV_COLS), lambda b, j: (b, j)),
        out_shape=jax.ShapeDtypeStruct((bsz * seq, DN_QKV), F32),
        compiler_params=_cparams(("parallel", "parallel")),
    )(zb, conv_w)


def _conv_bwd(zb, conv_w, dact, bsz, seq):
    def body(x_ref, w_ref, dy_ref, dx_ref, dw_ref):
        xf, w = x_ref[...], w_ref[...]
        c = _conv_pre(xf, w)
        sg = _sigmoid(c)
        dc = dy_ref[...].astype(F32) * sg * (1.0 + c * (1.0 - sg))
        dx = sum(_shift_up(dc, DN_CONV - 1 - j) * w[j:j + 1, :] for j in range(DN_CONV))
        dx_ref[...] = dx.astype(BF16)
        dw = jnp.concatenate(
            [jnp.sum(dc * _shift_down(xf, DN_CONV - 1 - j), axis=0, keepdims=True) for j in range(DN_CONV)], axis=0)

        @pl.when(pl.program_id(1) == 0)
        def _():
            dw_ref[...] = jnp.zeros_like(dw_ref)

        dw_ref[...] += dw

    nblk = DN_QKV // CONV_COLS
    return pl.pallas_call(
        body, name="conv_bwd", grid=(nblk, bsz),
        in_specs=[pl.BlockSpec((seq, CONV_COLS), lambda j, b: (b, ZB_DQKV // CONV_COLS + j)),
                  pl.BlockSpec((DN_CONV, CONV_COLS), lambda j, b: (0, j)),
                  pl.BlockSpec((seq, CONV_COLS), lambda j, b: (b, j))],
        out_specs=[pl.BlockSpec((seq, CONV_COLS), lambda j, b: (b, j)),
                   pl.BlockSpec((DN_CONV, CONV_COLS), lambda j, b: (0, j))],
        out_shape=[jax.ShapeDtypeStruct((bsz * seq, DN_QKV), BF16), jax.ShapeDtypeStruct((DN_CONV, DN_QKV), F32)],
        compiler_params=_cparams(("parallel", "arbitrary")),
    )(zb, conv_w, dact)


MESH_IDS = pl.DeviceIdType.MESH
ANY_SPEC = pl.BlockSpec(memory_space=pl.ANY)


COMM_SCRATCH = (pltpu.SemaphoreType.DMA((7,)), pltpu.SemaphoreType.DMA((7,)), pltpu.SemaphoreType.DMA)


def _gather_phases(x_ref, out_ref, send_sems, recv_sems, local_sem):
    mx, my, mc = lax.axis_index("x"), lax.axis_index("y"), lax.axis_index("c")
    me, sibling = (mx, my, mc), (mx, my, 1 - mc)
    chips = [(1 - mx, my), (mx, 1 - my), (1 - mx, 1 - my)]

    def slot(px, py, pc):
        return out_ref.at[4 * px + 2 * py + pc]

    def copy(k, block, to, src=None):
        return pltpu.make_async_remote_copy(
            src_ref=slot(*block) if src is None else src, dst_ref=slot(*block),
            send_sem=send_sems.at[k], recv_sem=recv_sems.at[k], device_id=to, device_id_type=MESH_IDS)

    def mine():
        return pltpu.make_async_copy(x_ref, slot(*me), local_sem)

    def first():
        return [copy(0, me, sibling, src=x_ref)] + [copy(1 + j, me, (*chip, mc), src=x_ref)
                                                    for j, chip in enumerate(chips)]

    def passed():
        return [copy(4 + j, (*chip, mc), sibling) for j, chip in enumerate(chips)]

    def start():
        mine().start()
        for cp in first():
            cp.start()

    def forward():
        for j, (chip, cp) in enumerate(zip(chips, passed())):
            copy(1 + j, (*chip, mc), me).wait_recv()
            cp.start()

    def finish():
        copy(0, sibling, me).wait_recv()
        for j, chip in enumerate(chips):
            copy(4 + j, (*chip, 1 - mc), me).wait_recv()
        for cp in first() + passed():
            cp.wait_send()
        mine().wait()

    return start, forward, finish


def _scatter_phases(x_ref, out_ref, send_sems, recv_sems, local_sem, among_chips=False):
    mx, my, mc = lax.axis_index("x"), lax.axis_index("y"), lax.axis_index("c")
    n_peers = 4 if among_chips else N_DEV
    me = 2 * mx + my if among_chips else 4 * mx + 2 * my + mc

    def peer(k):
        if among_chips:
            return (mx ^ ((k >> 1) & 1), my ^ (k & 1), mc)
        return (mx ^ ((k >> 2) & 1), my ^ ((k >> 1) & 1), mc ^ (k & 1))

    def slot_of(k):
        px, py, pc = peer(k)
        return 2 * px + py if among_chips else 4 * px + 2 * py + pc

    def copy(k, src_slot, dst_slot):
        return pltpu.make_async_remote_copy(
            src_ref=x_ref.at[src_slot], dst_ref=out_ref.at[dst_slot],
            send_sem=send_sems.at[k - 1], recv_sem=recv_sems.at[k - 1],
            device_id=peer(k), device_id_type=MESH_IDS)

    def sends():
        return [copy(k, slot_of(k), me) for k in range(1, n_peers)]

    def mine():
        return pltpu.make_async_copy(x_ref.at[me], out_ref.at[me], local_sem)

    def start():
        mine().start()
        for cp in sends():
            cp.start()

    def forward():
        pass

    def finish():
        for k in range(1, n_peers):
            copy(k, me, slot_of(k)).wait_recv()
        for cp in sends():
            cp.wait_send()
        mine().wait()

    return start, forward, finish


def _pair_phases(x_ref, out_ref, send_sems, recv_sems, local_sem):
    mx, my, mc = lax.axis_index("x"), lax.axis_index("y"), lax.axis_index("c")

    def copy(side):
        return pltpu.make_async_remote_copy(
            src_ref=x_ref.at[:, side], dst_ref=out_ref, send_sem=send_sems.at[0], recv_sem=recv_sems.at[0],
            device_id=(mx, my, 1 - mc), device_id_type=MESH_IDS)

    def start():
        copy(1 - mc).start()

    def forward():
        pass

    def finish():
        copy(mc).wait_recv()
        copy(1 - mc).wait_send()

    return start, forward, finish


class _Rider:
    def __init__(self, phases, x, out_shape):
        self.phases, self.x, self.out_shape = phases, x, out_shape


def _gather_rider(x):
    return _Rider(_gather_phases, x, jax.ShapeDtypeStruct((N_DEV,) + x.shape, x.dtype))


def _scatter_rider(x):
    return _Rider(_scatter_phases, x, jax.ShapeDtypeStruct(x.shape, x.dtype))


def _chip_scatter_rider(x):
    return _Rider(functools.partial(_scatter_phases, among_chips=True), x, jax.ShapeDtypeStruct(x.shape, x.dtype))


def _pair_rider(x):
    return _Rider(_pair_phases, x, jax.ShapeDtypeStruct((x.shape[0],) + x.shape[2:], x.dtype))


def _exchange(rider, name):
    def body(x_ref, out_ref, send_sems, recv_sems, local_sem):
        for phase in rider.phases(x_ref, out_ref, send_sems, recv_sems, local_sem):
            phase()

    return pl.pallas_call(body, name=name, out_shape=rider.out_shape, in_specs=[ANY_SPEC], out_specs=ANY_SPEC,
                          scratch_shapes=list(COMM_SCRATCH))(rider.x)


def _all_gather(x, name):
    return _exchange(_gather_rider(x), name)


def _all_to_all(x, name):
    return _exchange(_scatter_rider(x), name)


def _hosted_call(body, rider, *, name, grid, in_specs, out_specs, out_shape, scratch_shapes, args):
    if rider is None:
        return pl.pallas_call(body, name=name, grid=grid, in_specs=in_specs, out_specs=out_specs, out_shape=out_shape,
                              scratch_shapes=scratch_shapes, compiler_params=_cparams(("arbitrary",) * len(grid)))(*args)
    n_in, n_out, n_scr = len(in_specs), len(out_specs), len(scratch_shapes)
    total = math.prod(grid)

    def riding(*refs):
        host_in, x_ref = refs[:n_in], refs[n_in]
        host_out, out_ref = refs[n_in + 1:n_in + 1 + n_out], refs[n_in + 1 + n_out]
        host_scr = refs[n_in + 2 + n_out:n_in + 2 + n_out + n_scr]
        start, forward, finish = rider.phases(x_ref, out_ref, *refs[n_in + 2 + n_out + n_scr:])
        step = 0
        for axis, size in enumerate(grid):
            step = step * size + pl.program_id(axis)
        pl.when(step == 0)(start)
        pl.when(step == total // 2)(forward)
        body(*host_in, *host_out, *host_scr)
        pl.when(step == total - 1)(finish)

    return pl.pallas_call(
        riding, name=name, grid=grid, in_specs=list(in_specs) + [ANY_SPEC], out_specs=list(out_specs) + [ANY_SPEC],
        out_shape=list(out_shape) + [rider.out_shape], scratch_shapes=list(scratch_shapes) + list(COMM_SCRATCH),
        compiler_params=_cparams(("arbitrary",) * len(grid)))(*args, rider.x)


def _adamw_math(w, g, m, v):
    m = ADAM_B1 * m + (1.0 - ADAM_B1) * g
    v = ADAM_B2 * v + (1.0 - ADAM_B2) * jnp.square(g)
    m_hat = m / (1.0 - ADAM_B1 ** ADAM_STEP)
    v_hat = v / (1.0 - ADAM_B2 ** ADAM_STEP)
    delta = -ADAM_LR * (m_hat / (jnp.sqrt(v_hat) + ADAM_EPS) + ADAM_WD * w)
    return delta, m, v


ADAM_ROWS = 128


def _add_blocks(a, b):
    g, rows, cols = a.shape
    tr = ADAM_ROWS if rows % ADAM_ROWS == 0 else rows

    def body(a_ref, b_ref, o_ref):
        o_ref[...] = (a_ref[...].astype(F32) + b_ref[...].astype(F32)).astype(o_ref.dtype)

    blk = pl.BlockSpec((None, tr, cols), lambda k, i: (k, i, 0))
    return pl.pallas_call(body, name="add_pair_blocks", grid=(g, rows // tr), in_specs=[blk, blk], out_specs=blk,
                          out_shape=jax.ShapeDtypeStruct(a.shape, a.dtype),
                          compiler_params=_cparams(("parallel", "parallel")))(a, b)


def _adamw_reduce(w, m, v, parts, row0, name):
    rows, cols = w.shape
    n_parts = parts.shape[0]
    tr = ADAM_ROWS if rows % ADAM_ROWS == 0 else rows
    r0 = row0 // tr

    def body(w_ref, m_ref, v_ref, *refs):
        part_refs, (g_ref, d_ref, nm_ref, nv_ref) = refs[:n_parts], refs[n_parts:]
        g = part_refs[0][...].astype(F32)
        for r in part_refs[1:]:
            g = g + r[...].astype(F32)
        delta, nm, nv = _adamw_math(w_ref[...], g, m_ref[...], v_ref[...])
        g_ref[...] = g
        d_ref[...] = delta
        nm_ref[...] = nm
        nv_ref[...] = nv

    blk = pl.BlockSpec((tr, cols), lambda i: (i, 0))
    part_specs = [pl.BlockSpec((None, tr, cols), functools.partial(lambda i, k: (k, r0 + i, 0), k=k))
                  for k in range(n_parts)]
    return pl.pallas_call(
        body, name=name, grid=(rows // tr,), in_specs=[blk] * 3 + part_specs, out_specs=[blk] * 4,
        out_shape=[jax.ShapeDtypeStruct(w.shape, F32)] * 4, compiler_params=_cparams(("parallel",)),
    )(w, m, v, *([parts] * n_parts))


def _small_reduce(gathered, lane_sum_from):
    r = gathered.shape[1]

    def body(g_ref, o_ref):
        g = g_ref[0]
        for k in range(1, N_DEV):
            g = g + g_ref[k]
        tot = jnp.broadcast_to(jnp.sum(g, axis=-1, keepdims=True), g.shape)
        o_ref[...] = jnp.where(_iota2(g.shape, 0) >= lane_sum_from, tot, g)

    return pl.pallas_call(body, name="small_grad_reduce", out_shape=jax.ShapeDtypeStruct((r, LANE), F32))(gathered)


def _adamw_small(w, m, v, g):
    def body(w_ref, m_ref, v_ref, g_ref, d_ref, nm_ref, nv_ref):
        d_ref[...], nm_ref[...], nv_ref[...] = _adamw_math(w_ref[...], g_ref[...], m_ref[...], v_ref[...])

    return pl.pallas_call(body, name="adamw_small", out_shape=[jax.ShapeDtypeStruct(w.shape, F32)] * 3)(w, m, v, g)


def _pack_rows(arrays):
    rows = [jnp.pad(a.reshape(-1), (0, -a.size % LANE)).reshape(-1, LANE) for a in arrays]
    out = jnp.concatenate(rows, axis=0)
    return jnp.pad(out, ((0, -out.shape[0] % 8), (0, 0)))


def _unpack_rows(packed, shapes):
    out, r = [], 0
    for shp in shapes:
        size = math.prod(shp)
        nrows = -(-size // LANE)
        out.append(packed[r:r + nrows].reshape(-1)[:size].reshape(shp))
        r += nrows
    return out


def _add_residual(acc, res):
    return (res + acc,)


def _by_cols(g):
    return g.reshape(g.shape[0], N_DEV, -1).transpose(1, 0, 2)


def _from_cols(blocks):
    return blocks.transpose(1, 0, 2).reshape(blocks.shape[1], -1)


ROWS_OUT = D_MODEL // N_DEV


def _local_step(x, p, target, w, up_shard, rows_shard):
    bsz, seq, _ = x.shape
    t, nc = bsz * seq, seq // CHUNK
    x0, p2, tgt = x.reshape(t, D_MODEL), p.reshape(t, PLE_DIM), target.reshape(t, D_MODEL)

    h = _rms_fwd(x0, w["g_mix"], "rms_mix")
    zb, up_blocks = _mm(h, w["wb"], name="in_proj", rider=_gather_rider(up_shard))
    w_up = _from_cols(up_blocks)
    zs = _mm(h, w["ws"], name="in_proj_gates")
    act = _conv_fwd(zb, w["conv"], bsz, seq)
    mix_gla, gla_states = _gla_fwd(zb, zs, w["w2p"], w["gla_b"], w["gla_norm"], bsz, nc)
    mixed, dn_states, row_blocks = _dn_fwd(act, zb, zs, w["alog_b"], w["dtb_b"], w["dn_norm"], mix_gla, bsz, nc,
                                           rider=_gather_rider(rows_shard))
    w_out = row_blocks[:, :ROWS_OUT].reshape(D_MODEL, D_MODEL)
    w_pg = row_blocks[:, ROWS_OUT:2 * ROWS_OUT].reshape(D_MODEL, D_MODEL)
    w_down = row_blocks[:, 2 * ROWS_OUT:].reshape(D_FF, D_MODEL)
    x1 = _mm(mixed, w_out, epilogue=_add_residual, extras=(x0,), name="out_proj")
    h2 = _rms_fwd(x1, w["g_mlp"], "rms_mlp")
    u, a = _mm(h2, w_up, out_dtypes=(BF16, BF16), name="mlp_up",
               epilogue=lambda acc: (acc, jnp.square(jnp.maximum(acc, 0.0))))
    x2 = _mm(a, w_down, epilogue=_add_residual, extras=(x1,), name="mlp_down")
    h3 = _rms_fwd(x2, w["g_ple"], "rms_ple")
    pp = _mm(p2, w["w_pp"], name="ple_proj")
    gpre, x3 = _mm(h3, w_pg, out_dtypes=(F32, F32), extras=(x2, pp), name="ple_gate",
                   epilogue=lambda acc, res, proj: (acc, res + _sigmoid(acc) * proj))
    dx3, dg_final, loss = _loss_head(x3, w["g_final"], tgt)

    dgpre, dpp = _ple_bwd(dx3, gpre, pp)
    dw_pp = _mm(p2, dpp, ta=True, out_dtypes=(BF16,), name="d_w_ple_proj")
    dw_pg = _mm(h3, dgpre, ta=True, out_dtypes=(BF16,), name="d_w_ple_gate")
    dh3 = _mm(dgpre, w_pg, tb=True, name="d_h_ple")
    dx2, dg_ple = _rms_bwd(x2, w["g_ple"], dh3, dx3, "rms_ple_bwd")
    du = _mm(dx2, w_down, tb=True, out_dtypes=(BF16,), extras=(u,), name="d_mlp_hidden",
             epilogue=lambda acc, uu: (acc * (2.0 * jnp.maximum(uu.astype(F32), 0.0)),))
    dw_down = _mm(a, dx2, ta=True, out_dtypes=(BF16,), name="d_w_down")
    dw_up = _mm(h2, du, ta=True, out_dtypes=(BF16,), name="d_w_up")
    dh2 = _mm(du, w_up, tb=True, name="d_h_mlp")
    dx1, dg_mlp = _rms_bwd(x1, w["g_mlp"], dh2, dx2, "rms_mlp_bwd")
    dmixed = _mm(dx1, w_out, tb=True, out_dtypes=(BF16,), name="d_mixed")
    dw_out = _mm(mixed, dx1, ta=True, out_dtypes=(BF16,), name="d_w_out")

    d_rows = jnp.concatenate([dw_out.reshape(N_DEV, ROWS_OUT, D_MODEL), dw_pg.reshape(N_DEV, ROWS_OUT, D_MODEL),
                              dw_down.reshape(N_DEV, D_FF // N_DEV, D_MODEL)], axis=1)
    (dact, dza, dzb_, dgz, dgb, dal, ddt, ddn, recv_rows) = _dn_bwd(
        act, zb, zs, w["alog_b"], w["dtb_b"], w["dn_norm"], dn_states, dmixed, bsz, nc,
        rider=_scatter_rider(d_rows))
    (gdq, gdk, gdv, dgg, dga, dlr, dw2, dgla_b, dgla_norm, recv_up) = _gla_bwd(
        zb, zs, w["w2p"], w["gla_b"], w["gla_norm"], gla_states, dmixed, bsz, nc,
        rider=_scatter_rider(_by_cols(dw_up)))
    dqkv, dconv = _conv_bwd(zb, w["conv"], dact, bsz, seq)
    dzb = jnp.concatenate([gdq, gdk, gdv, dgg, dqkv, dgz, dga, dgb], axis=1)
    dzs = jnp.concatenate([dlr, dza, dzb_], axis=1)
    dwb = _mm(h, dzb, ta=True, out_dtypes=(BF16,), name="d_w_in")
    dws = _mm(h, dzs, ta=True, out_dtypes=(BF16,), name="d_w_in_gates")
    dw_in = jnp.concatenate([dwb[:, :WI_LR], dws[:, ZS_LR:ZS_LR + GLA_LOWRANK],
                             dwb[:, ZB_DQKV:ZB_GA], dws[:, ZS_DA:ZS_DA + DN_HEADS],
                             dws[:, ZS_DB:ZS_DB + DN_HEADS], dwb[:, ZB_GA:]], axis=1)
    by_chip = _by_cols(dw_in).reshape(N_DEV // 2, 2, D_MODEL, D_IN_SHARD)
    from_sibling = _exchange(_pair_rider(by_chip), "pair_d_w_in")
    mine = lax.dynamic_index_in_dim(by_chip, lax.axis_index("c"), axis=1, keepdims=False)
    chip_sums = _add_blocks(mine, from_sibling)
    dh_gates = _mm(dzs, w["ws"], tb=True, name="d_h_mix_gates")
    dh, recv_in = _mm(dzb, w["wb"], tb=True, epilogue=_add_residual, extras=(dh_gates,), name="d_h_mix",
                      rider=_chip_scatter_rider(chip_sums))
    gx, dg_mix = _rms_bwd(x0, w["g_mix"], dh, dx1, "rms_mix_bwd")

    dgla_w2 = dw2[:, :GLA_LOWRANK, :].transpose(1, 0, 2).reshape(GLA_LOWRANK, GLA_QK)
    return dict(
        loss=loss[0, 0], grad_x=gx.reshape(x.shape), recv_in=recv_in, recv_up=recv_up, recv_rows=recv_rows,
        w_ple_proj=dw_pp,
        g_mix=dg_mix, gla_b=dgla_b.reshape(1, GLA_QK), gla_norm=dgla_norm, dn_norm=ddn, g_mlp=dg_mlp, g_ple=dg_ple,
        g_final=dg_final, gla_w2=dgla_w2, dn_conv=dconv,
        a_log_lanes=dal.reshape(DN_HEADS, LANE), dt_bias_lanes=ddt.reshape(DN_HEADS, LANE))


def _first_weights(g_mix, w_in, gla_w2, gla_b, gla_norm, dn_conv, dn_a_log, dn_dt_bias, dn_norm, g_mlp, g_ple,
                   w_ple_proj, g_final):
    w_in_full = _from_cols(_all_gather(w_in[0].astype(BF16), "gather_w_in"))
    w_pp = _all_gather(w_ple_proj[0].astype(BF16), "gather_w_ple_proj")
    small = _all_gather(_pack_rows([gla_w2[0], dn_conv[0]]), "gather_w_small")
    n_w2 = GLA_LOWRANK * GLA_QK // N_DEV // LANE
    n_cv = DN_CONV * DN_QKV // N_DEV // LANE
    w2 = small[:, :n_w2].reshape(N_DEV, GLA_LOWRANK, GLA_QK // N_DEV).transpose(1, 0, 2).reshape(GLA_LOWRANK, GLA_QK)
    conv = small[:, n_w2:n_w2 + n_cv].reshape(N_DEV, DN_CONV, DN_QKV // N_DEV).transpose(1, 0, 2).reshape(DN_CONV, DN_QKV)

    def lane_pad(wcols):
        return jnp.pad(wcols, ((0, 0), (0, LANE - wcols.shape[1])))

    return dict(
        wb=jnp.concatenate([w_in_full[:, :WI_LR], w_in_full[:, WI_DQKV:WI_DA], w_in_full[:, WI_GA:]], axis=1),
        ws=jnp.concatenate([lane_pad(w_in_full[:, WI_LR:WI_DQKV]), lane_pad(w_in_full[:, WI_DA:WI_DB]),
                            lane_pad(w_in_full[:, WI_DB:WI_GA])], axis=1),
        w_pp=_from_cols(w_pp),
        w2p=jnp.pad(w2, ((0, LANE - GLA_LOWRANK), (0, 0))), conv=conv,
        alog_b=jnp.broadcast_to(dn_a_log[0][:, None, None], (DN_HEADS, 1, LANE)),
        dtb_b=jnp.broadcast_to(dn_dt_bias[0][:, None, None], (DN_HEADS, 1, LANE)),
        g_mix=g_mix, gla_b=gla_b, gla_norm=gla_norm, dn_norm=dn_norm, g_mlp=g_mlp, g_ple=g_ple,
        g_final=g_final.reshape(1, D_MODEL))


def kernel(x, p, g_mix, w_in, gla_w2, gla_b, gla_norm, dn_conv, dn_a_log, dn_dt_bias, dn_norm, w_out, g_mlp, w_up, w_down, g_ple, w_ple_gate, w_ple_proj, g_final, loss_target, m_g_mix, m_w_in, m_gla_w2, m_gla_b, m_gla_norm, m_dn_conv, m_dn_a_log, m_dn_dt_bias, m_dn_norm, m_w_out, m_g_mlp, m_w_up, m_w_down, m_g_ple, m_w_ple_gate, m_w_ple_proj, m_g_final, v_g_mix, v_w_in, v_gla_w2, v_gla_b, v_gla_norm, v_dn_conv, v_dn_a_log, v_dn_dt_bias, v_dn_norm, v_w_out, v_g_mlp, v_w_up, v_w_down, v_g_ple, v_w_ple_gate, v_w_ple_proj, v_g_final):
    names = ["g_mix", "w_in", "gla_w2", "gla_b", "gla_norm", "dn_conv", "dn_a_log", "dn_dt_bias", "dn_norm", "w_out",
             "g_mlp", "w_up", "w_down", "g_ple", "w_ple_gate", "w_ple_proj", "g_final"]
    ws = dict(zip(names, (g_mix, w_in, gla_w2, gla_b, gla_norm, dn_conv, dn_a_log, dn_dt_bias, dn_norm, w_out, g_mlp,
                          w_up, w_down, g_ple, w_ple_gate, w_ple_proj, g_final)))
    ms = dict(zip(names, (m_g_mix, m_w_in, m_gla_w2, m_gla_b, m_gla_norm, m_dn_conv, m_dn_a_log, m_dn_dt_bias,
                          m_dn_norm, m_w_out, m_g_mlp, m_w_up, m_w_down, m_g_ple, m_w_ple_gate, m_w_ple_proj,
                          m_g_final)))
    vs = dict(zip(names, (v_g_mix, v_w_in, v_gla_w2, v_gla_b, v_gla_norm, v_dn_conv, v_dn_a_log, v_dn_dt_bias,
                          v_dn_norm, v_w_out, v_g_mlp, v_w_up, v_w_down, v_g_ple, v_w_ple_gate, v_w_ple_proj,
                          v_g_final)))
    me = 4 * lax.axis_index("x") + 2 * lax.axis_index("y") + lax.axis_index("c")

    first = _first_weights(g_mix, w_in, gla_w2, gla_b, gla_norm, dn_conv, dn_a_log, dn_dt_bias, dn_norm, g_mlp,
                           g_ple, w_ple_proj, g_final)
    rows_shard = jnp.concatenate([w_out[0], w_ple_gate[0], w_down[0]], axis=0).astype(BF16)
    r = _local_step(x, p[0], loss_target, first, w_up[0].astype(BF16), rows_shard)
    loss = lax.psum(r["loss"], ("x", "y", "c"))

    grads, deltas, new_m, new_v = {}, {}, {}, {}

    def big(name, parts, row0=0):
        g, d, nm, nv = _adamw_reduce(ws[name][0], ms[name][0], vs[name][0], parts, row0, "adamw_" + name)
        grads[name], deltas[name], new_m[name], new_v[name] = g[None], d[None], nm[None], nv[None]

    big("w_in", r["recv_in"])
    big("w_up", r["recv_up"])
    big("w_ple_proj", _all_to_all(_by_cols(r["w_ple_proj"]), "scatter_d_w_ple_proj"))
    big("w_out", r["recv_rows"], 0)
    big("w_ple_gate", r["recv_rows"], ROWS_OUT)
    big("w_down", r["recv_rows"], 2 * ROWS_OUT)

    vec_names = ["g_mix", "gla_b", "gla_norm", "dn_norm", "g_mlp", "g_ple", "g_final"]
    packed = _pack_rows([r[n] for n in vec_names] + [r["gla_w2"], r["dn_conv"]])
    lane_rows = packed.shape[0]
    packed = jnp.concatenate([packed, r["a_log_lanes"], r["dt_bias_lanes"]], axis=0)
    total = _small_reduce(_all_gather(packed, "gather_small_grads"), lane_rows)
    parts = _unpack_rows(total, [r[n].shape for n in vec_names] + [r["gla_w2"].shape, r["dn_conv"].shape])
    sg = dict(zip(vec_names, parts[:len(vec_names)]))
    sg["g_final"] = sg["g_final"].reshape(D_MODEL)
    sg["gla_w2"] = lax.dynamic_slice_in_dim(parts[-2], me * (GLA_QK // N_DEV), GLA_QK // N_DEV, axis=1)
    sg["dn_conv"] = lax.dynamic_slice_in_dim(parts[-1], me * (DN_QKV // N_DEV), DN_QKV // N_DEV, axis=1)
    sg["dn_a_log"] = total[lane_rows:lane_rows + DN_HEADS, 0]
    sg["dn_dt_bias"] = total[lane_rows + DN_HEADS:lane_rows + 2 * DN_HEADS, 0]
    small_names = vec_names + ["gla_w2", "dn_conv", "dn_a_log", "dn_dt_bias"]
    shapes = [ws[n].shape for n in small_names]
    d_s, m_s, v_s = _adamw_small(_pack_rows([ws[n] for n in small_names]), _pack_rows([ms[n] for n in small_names]),
                                 _pack_rows([vs[n] for n in small_names]), _pack_rows([sg[n] for n in small_names]))
    for n, d, nm, nv in zip(small_names, _unpack_rows(d_s, shapes), _unpack_rows(m_s, shapes), _unpack_rows(v_s, shapes)):
        grads[n], deltas[n], new_m[n], new_v[n] = sg[n].reshape(ws[n].shape), d, nm, nv

    return (loss, r["grad_x"], *[grads[n] for n in names], *[deltas[n] for n in names],
            *[new_m[n] for n in names], *[new_v[n] for n in names])
```

```python
import functools
import math

import jax
import jax.numpy as jnp
from jax import lax
from jax.experimental import pallas as pl
from jax.experimental.pallas import tpu as pltpu

F32 = jnp.float32
BF16 = jnp.bfloat16
HIGHEST = lax.Precision.HIGHEST

N_DEV = 8
D_MODEL = 2048
CHUNK = 64
PLE_DIM = 256
EPS = 1e-6
GLA_HEADS = 4
GLA_DK = 256
GLA_DV = 512
GLA_LOWRANK = 16
GLA_TAU = 16.0
DN_HEADS = 16
DN_D = 128
DN_CONV = 4
D_FF = 4 * D_MODEL
GLA_QK = GLA_HEADS * GLA_DK
GLA_V = GLA_HEADS * GLA_DV
DN_QKV = 3 * DN_HEADS * DN_D
D_IN = 2 * GLA_QK + 2 * GLA_V + GLA_LOWRANK + DN_QKV + D_MODEL + 2 * DN_HEADS + 2 * D_MODEL
D_IN_SHARD = D_IN // N_DEV

ADAM_LR = 0.001
ADAM_B1 = 0.9
ADAM_B2 = 0.999
ADAM_EPS = 1e-08
ADAM_WD = 0.01
ADAM_STEP = 10

LANE = 128
ZB_GQ, ZB_GK, ZB_GV, ZB_GG = 0, 1024, 2048, 4096
ZB_DQKV, ZB_DZ, ZB_GA, ZB_GB = 6144, 12288, 14336, 16384
ZB_W = 18432
ZS_LR, ZS_DA, ZS_DB = 0, 128, 256
ZS_W = 384
WI_LR = 2 * GLA_QK + 2 * GLA_V
WI_DQKV = WI_LR + GLA_LOWRANK
WI_DA = WI_DQKV + DN_QKV + D_MODEL
WI_DB = WI_DA + DN_HEADS
WI_GA = WI_DB + DN_HEADS

VMEM_LIMIT = 56 * 1024 * 1024

NN = (((1,), (0,)), ((), ()))
NT = (((1,), (1,)), ((), ()))
TN = (((0,), (0,)), ((), ()))


def _bdot(a, b, dims):
    return lax.dot_general(a.astype(BF16), b.astype(BF16), dims, preferred_element_type=F32)


def _split3(x):
    hi = x.astype(BF16)
    rest = x - hi.astype(F32)
    mid = rest.astype(BF16)
    return hi, mid, (rest - mid.astype(F32)).astype(BF16)


def _dot01(x, m, dims, x_first):
    m = m.astype(BF16)
    out = None
    for piece in _split3(x):
        d = lax.dot_general(piece, m, dims, preferred_element_type=F32) if x_first else \
            lax.dot_general(m, piece, dims, preferred_element_type=F32)
        out = d if out is None else out + d
    return out


@functools.partial(jax.custom_vjp, nondiff_argnums=(2, 3))
def _pick_dot(x, m, dims, dims_t):
    return _dot01(x, m, dims, True)


def _pick_dot_fwd(x, m, dims, dims_t):
    return _dot01(x, m, dims, True), m


def _pick_dot_bwd(dims, dims_t, m, ct):
    return _dot01(ct, m, dims_t, True), jnp.zeros_like(m)


_pick_dot.defvjp(_pick_dot_fwd, _pick_dot_bwd)


@functools.partial(jax.custom_vjp, nondiff_argnums=(2, 3))
def _left_dot(m, x, dims, dims_t):
    return _dot01(x, m, dims, False)


def _left_dot_fwd(m, x, dims, dims_t):
    return _dot01(x, m, dims, False), m


def _left_dot_bwd(dims, dims_t, m, ct):
    return jnp.zeros_like(m), _dot01(ct, m, dims_t, False)


_left_dot.defvjp(_left_dot_fwd, _left_dot_bwd)


def _dot3(a, b, dims):
    ah, bh = a.astype(BF16), b.astype(BF16)
    al, bl = (a - ah.astype(F32)).astype(BF16), (b - bh.astype(F32)).astype(BF16)
    dot = functools.partial(lax.dot_general, dimension_numbers=dims, preferred_element_type=F32)
    return dot(ah, bh) + (dot(ah, bl) + dot(al, bh))


def _sigmoid(x):
    return 1.0 / (1.0 + jnp.exp(-x))


def _silu(x):
    return x * _sigmoid(x)


def _softplus(x):
    return jnp.maximum(x, 0.0) + jnp.log(1.0 + jnp.exp(-jnp.abs(x)))


def _iota2(shape, dim):
    return lax.broadcasted_iota(jnp.int32, shape, dim)


def _cparams(sem=None):
    return pltpu.CompilerParams(dimension_semantics=sem, vmem_limit_bytes=VMEM_LIMIT)


BNN = (((2,), (1,)), ((0,), (0,)))
BNT = (((2,), (2,)), ((0,), (0,)))
BTN = (((1,), (1,)), ((0,), (0,)))


def _gla_chunk(st, q, k, v, lr, w2, b, gg, ga, gn):
    hb, c, _ = q.shape
    incl = (_iota2((c, c), 0) >= _iota2((c, c), 1))[None]
    tri = jnp.broadcast_to(incl.astype(F32), (hb, c, c))
    q = q.astype(F32) * (GLA_DK ** -0.5)
    k = k.astype(F32)
    v = v.astype(F32)
    lr_b = jnp.broadcast_to(lr[None], (hb,) + lr.shape)
    lf = -_softplus(-(_bdot(lr_b, w2, BNN) + b)) / GLA_TAU
    bcum = _left_dot(tri, lf, BNN, BTN)
    blast = jnp.sum(lf, axis=1, keepdims=True)
    q_in = q * jnp.exp(bcum)
    k_in = k * jnp.exp(-bcum)
    a = jnp.where(incl, _bdot(q_in, k_in, BNT), 0.0)
    o = _bdot(a, v, BNN) + _bdot(q_in, st, BNT)
    k_dec = k * jnp.exp(blast - bcum)
    st_new = st * jnp.exp(blast) + _bdot(v, k_dec, BTN)
    on = o * lax.rsqrt(jnp.mean(o * o, axis=-1, keepdims=True) + EPS) * gn
    res = _sigmoid(ga.astype(F32)) * on * _silu(gg.astype(F32))
    return res, st_new


def _tri_inv_raw(a):
    _, c, _ = a.shape
    eye = (_iota2((c, c), 0) == _iota2((c, c), 1)).astype(F32)[None]
    x = a
    p = eye - a
    for _ in range(5):
        x = _dot3(x, x, BNN)
        p = _dot3(p, eye + x, BNN)
    return p


def _tri_inv_bwd(t, dt):
    return (-_dot3(_dot3(t, dt, BTN), t, BNT),)


@jax.custom_vjp
def _tri_inv_given(a, t):
    return t


def _tri_inv_given_fwd(a, t):
    return t, t


def _tri_inv_given_bwd(t, dt):
    return _tri_inv_bwd(t, dt) + (jnp.zeros_like(t),)


_tri_inv_given.defvjp(_tri_inv_given_fwd, _tri_inv_given_bwd)


@functools.partial(jax.custom_vjp, nondiff_argnums=(1,))
def _column_on_lanes(z, j):
    picked = jnp.where(_iota2(z.shape, 1) == j, z, 0.0)
    return jnp.broadcast_to(jnp.sum(picked, axis=1, keepdims=True), z.shape)


def _column_on_lanes_fwd(z, j):
    return _column_on_lanes(z, j), None


def _column_on_lanes_bwd(j, _, ct):
    total = jnp.broadcast_to(jnp.sum(ct, axis=1, keepdims=True), ct.shape)
    return (jnp.where(_iota2(ct.shape, 1) == j, total, 0.0),)


_column_on_lanes.defvjp(_column_on_lanes_fwd, _column_on_lanes_bwd)


def _dn_chunk(s, qr, kr, vr, za, zb, alog, dtb, gz, gb, dn, t_saved=None):
    hb, c, _ = qr.shape
    row, col = _iota2((c, c), 0), _iota2((c, c), 1)
    incl = (row >= col)[None]
    strict = (row > col)[None]
    tri = jnp.broadcast_to(incl.astype(F32), (hb, c, c))
    eye = (row == col).astype(F32)[None]
    ones_cc = jnp.ones((hb, c, c), F32)
    lane0 = (lax.broadcasted_iota(jnp.int32, (hb, LANE, c), 1) == 0).astype(F32)

    def l2n(t):
        return t * lax.rsqrt(jnp.sum(t * t, axis=-1, keepdims=True) + EPS)

    q = l2n(qr.astype(F32)) * (DN_D ** -0.5)
    k = l2n(kr.astype(F32))
    v = vr.astype(F32)
    za_b = jnp.concatenate([_column_on_lanes(za, j)[None] for j in range(hb)], axis=0)
    zb_b = jnp.concatenate([_column_on_lanes(zb, j)[None] for j in range(hb)], axis=0)
    g = -jnp.exp(alog) * _softplus(za_b + dtb)
    beta = _sigmoid(zb_b)
    gcum = _left_dot(tri, g, BNN, BTN)
    glast = jnp.sum(g, axis=1, keepdims=True)
    cm = _pick_dot(gcum, lane0, BNN, BNT)
    rm = _left_dot(ones_cc, cm * eye, BNN, BTN)
    dec = jnp.exp(jnp.where(incl, cm - rm, -1e30))
    kb = k * beta
    a = jnp.where(strict, _bdot(kb, k, BNT) * dec, 0.0)
    t = _tri_inv_raw(a) if t_saved is None else _tri_inv_given(a, t_saved)
    egc = jnp.exp(gcum)
    u = _bdot(t, v * beta, BNN)
    w = _bdot(t, kb * egc, BNN)
    attn = jnp.where(incl, _bdot(q, k, BNT) * dec, 0.0)
    q_dec = q * egc
    k_dec = k * jnp.exp(glast - gcum)
    v_new = u - _bdot(w, s, BNN)
    o = _bdot(q_dec, s, BNN) + _bdot(attn, v_new, BNN)
    s_new = s * jnp.exp(glast) + _bdot(k_dec, v_new, BTN)
    on = o * lax.rsqrt(jnp.mean(o * o, axis=-1, keepdims=True) + EPS) * dn
    res = _sigmoid(gb.astype(F32)) * on * _silu(gz.astype(F32))
    return (res, s_new, t) if t_saved is None else (res, s_new)


def _heads(ref, n_heads, width):
    return jnp.stack([ref[:, j * width:(j + 1) * width] for j in range(n_heads)], axis=0)


def _gla_specs(nc, reverse):
    def rows(b, n):
        return b * nc + ((nc - 1 - n) if reverse else n)

    qk = lambda base: pl.BlockSpec((CHUNK, GLA_QK), lambda b, n: (rows(b, n), base // GLA_QK))
    vv = lambda base: pl.BlockSpec((CHUNK, GLA_V), lambda b, n: (rows(b, n), base // GLA_V))
    lr = lambda c: pl.BlockSpec((CHUNK, LANE), lambda b, n: (rows(b, n), c))
    full = lambda shape: pl.BlockSpec(shape, lambda b, n: (0,) * len(shape))
    return rows, qk, vv, lr, full


def _gla_inputs(q_ref, k_ref, v_ref, gg_ref, ga_ref, lr_ref, w2_ref, b_ref, gn_ref):
    return (_heads(q_ref, GLA_HEADS, GLA_DK), _heads(k_ref, GLA_HEADS, GLA_DK), _heads(v_ref, GLA_HEADS, GLA_DV),
            lr_ref[...], _heads(w2_ref, GLA_HEADS, GLA_DK), _heads(b_ref, GLA_HEADS, GLA_DK),
            _heads(gg_ref, GLA_HEADS, GLA_DV), _heads(ga_ref, GLA_HEADS, GLA_DV), gn_ref[...])


def _gla_fwd(zb, zs, w2p, gla_b, gla_norm, bsz, nc):
    t = zb.shape[0]
    rows, qk, vv, lr, full = _gla_specs(nc, False)

    def body(q_ref, k_ref, v_ref, gg_ref, ga_ref, lr_ref, w2_ref, b_ref, gn_ref, o_ref, st_ref, state):
        st = jnp.where(pl.program_id(1) > 0, state[...], 0.0)
        st_ref[0, 0] = st
        res, st_new = _gla_chunk(st, *_gla_inputs(q_ref, k_ref, v_ref, gg_ref, ga_ref, lr_ref, w2_ref, b_ref, gn_ref))
        for j in range(GLA_HEADS):
            o_ref[:, j * GLA_DV:(j + 1) * GLA_DV] = res[j]
        state[...] = st_new

    return pl.pallas_call(
        body, name="gla_fwd", grid=(bsz, nc),
        in_specs=[qk(ZB_GQ), qk(ZB_GK), vv(ZB_GV), vv(ZB_GG), vv(ZB_GA), lr(ZS_LR // LANE),
                  full((LANE, GLA_QK)), full((1, GLA_QK)), full((1, GLA_DV))],
        out_specs=[vv(0), pl.BlockSpec((1, 1, GLA_HEADS, GLA_DV, GLA_DK), lambda b, n: (b, n, 0, 0, 0))],
        out_shape=[jax.ShapeDtypeStruct((t, GLA_V), F32),
                   jax.ShapeDtypeStruct((bsz, nc, GLA_HEADS, GLA_DV, GLA_DK), F32)],
        scratch_shapes=[pltpu.VMEM((GLA_HEADS, GLA_DV, GLA_DK), F32)],
        compiler_params=_cparams(("arbitrary", "arbitrary")),
    )(zb, zb, zb, zb, zb, zs, w2p, gla_b, gla_norm)


def _gla_bwd(zb, zs, w2p, gla_b, gla_norm, states, dmixed, bsz, nc, rider=None):
    t = zb.shape[0]
    rows, qk, vv, lr, full = _gla_specs(nc, True)

    def body(q_ref, k_ref, v_ref, gg_ref, ga_ref, lr_ref, w2_ref, b_ref, gn_ref, st_ref, dm_ref,
             dq_ref, dk_ref, dv_ref, dgg_ref, dga_ref, dlr_ref, dw2_ref, db_ref, dgn_ref, dstate):
        b, n = pl.program_id(0), pl.program_id(1)

        @pl.when((b == 0) & (n == 0))
        def _():
            dw2_ref[...] = jnp.zeros_like(dw2_ref)
            db_ref[...] = jnp.zeros_like(db_ref)
            dgn_ref[...] = jnp.zeros_like(dgn_ref)

        _, vjp = jax.vjp(_gla_chunk, st_ref[0, 0],
                         *_gla_inputs(q_ref, k_ref, v_ref, gg_ref, ga_ref, lr_ref, w2_ref, b_ref, gn_ref))
        dst_in = jnp.where(n > 0, dstate[...], 0.0)
        dst, dq, dk, dv, dlr, dw2, db, dgg, dga, dgn = vjp((_heads(dm_ref, GLA_HEADS, GLA_DV).astype(F32), dst_in))
        dstate[...] = dst
        for j in range(GLA_HEADS):
            dq_ref[:, j * GLA_DK:(j + 1) * GLA_DK] = dq[j].astype(dq_ref.dtype)
            dk_ref[:, j * GLA_DK:(j + 1) * GLA_DK] = dk[j].astype(dk_ref.dtype)
            dv_ref[:, j * GLA_DV:(j + 1) * GLA_DV] = dv[j].astype(dv_ref.dtype)
            dgg_ref[:, j * GLA_DV:(j + 1) * GLA_DV] = dgg[j].astype(dgg_ref.dtype)
            dga_ref[:, j * GLA_DV:(j + 1) * GLA_DV] = dga[j].astype(dga_ref.dtype)
        dlr_ref[...] = dlr
        dw2_ref[...] += dw2
        db_ref[...] += db
        dgn_ref[...] += dgn

    return _hosted_call(
        body, rider, name="gla_bwd", grid=(bsz, nc),
        in_specs=[qk(ZB_GQ), qk(ZB_GK), vv(ZB_GV), vv(ZB_GG), vv(ZB_GA), lr(ZS_LR // LANE),
                  full((LANE, GLA_QK)), full((1, GLA_QK)), full((1, GLA_DV)),
                  pl.BlockSpec((1, 1, GLA_HEADS, GLA_DV, GLA_DK), lambda b, n: (b, nc - 1 - n, 0, 0, 0)),
                  vv(0)],
        out_specs=[qk(0), qk(0), vv(0), vv(0), vv(0), lr(0),
                   full((GLA_HEADS, LANE, GLA_DK)), full((GLA_HEADS, 1, GLA_DK)), full((1, GLA_DV))],
        out_shape=[jax.ShapeDtypeStruct((t, GLA_QK), BF16), jax.ShapeDtypeStruct((t, GLA_QK), BF16),
                   jax.ShapeDtypeStruct((t, GLA_V), BF16), jax.ShapeDtypeStruct((t, GLA_V), BF16),
                   jax.ShapeDtypeStruct((t, GLA_V), BF16), jax.ShapeDtypeStruct((t, LANE), F32),
                   jax.ShapeDtypeStruct((GLA_HEADS, LANE, GLA_DK), F32),
                   jax.ShapeDtypeStruct((GLA_HEADS, 1, GLA_DK), F32),
                   jax.ShapeDtypeStruct((1, GLA_DV), F32)],
        scratch_shapes=[pltpu.VMEM((GLA_HEADS, GLA_DV, GLA_DK), F32)],
        args=(zb, zb, zb, zb, zb, zs, w2p, gla_b, gla_norm, states, dmixed))


DN_HB = DN_HEADS


def _dn_specs(nc, reverse):
    wide = DN_HB * DN_D

    def rows(b, n, h):
        return b * nc + ((nc - 1 - n) if reverse else n)

    def col(base):
        return pl.BlockSpec((CHUNK, wide), lambda b, n, h: (rows(b, n, h), base // wide + h))

    def fixed(c):
        return pl.BlockSpec((CHUNK, LANE), lambda b, n, h: (rows(b, n, h), c))

    head = pl.BlockSpec((DN_HB, 1, LANE), lambda b, n, h: (h, 0, 0))
    return rows, col, fixed, head


def _lanes(j):
    return slice(j * DN_D, (j + 1) * DN_D)


def _by_head(ref):
    return jnp.stack([ref[:, _lanes(j)] for j in range(DN_HB)], axis=0)


def _dn_fwd(act, zb, zs, alog_b, dtb_b, dn_norm, mix_gla, bsz, nc, rider=None):
    t = zb.shape[0]
    rows, col, fixed, head = _dn_specs(nc, False)

    def body(q_ref, k_ref, v_ref, za_ref, zb_ref, al_ref, dt_ref, gz_ref, gb_ref, dn_ref, mg_ref,
             o_ref, st_ref, ti_ref, state):
        s = jnp.where(pl.program_id(1) > 0, state[...], 0.0)
        st_ref[0, 0] = s
        res, s_new, t_inv = _dn_chunk(s, _by_head(q_ref), _by_head(k_ref), _by_head(v_ref), za_ref[...], zb_ref[...],
                                      al_ref[...], dt_ref[...], _by_head(gz_ref), _by_head(gb_ref), dn_ref[...])
        ti_ref[0, 0] = t_inv
        for j in range(DN_HB):
            o_ref[:, _lanes(j)] = (res[j] + mg_ref[:, _lanes(j)]).astype(o_ref.dtype)
        state[...] = s_new

    return _hosted_call(
        body, rider, name="dn_fwd", grid=(bsz, nc, DN_HEADS // DN_HB),
        in_specs=[col(0), col(DN_HEADS * DN_D), col(2 * DN_HEADS * DN_D),
                  fixed(ZS_DA // LANE), fixed(ZS_DB // LANE), head, head,
                  col(ZB_DZ), col(ZB_GB), pl.BlockSpec((1, DN_D), lambda b, n, h: (0, 0)), col(0)],
        out_specs=[col(0), pl.BlockSpec((1, 1, DN_HB, DN_D, DN_D), lambda b, n, h: (b, n, h, 0, 0)),
                   pl.BlockSpec((1, 1, DN_HB, CHUNK, CHUNK), lambda b, n, h: (b, n, h, 0, 0))],
        out_shape=[jax.ShapeDtypeStruct((t, D_MODEL), BF16),
                   jax.ShapeDtypeStruct((bsz, nc, DN_HEADS, DN_D, DN_D), F32),
                   jax.ShapeDtypeStruct((bsz, nc, DN_HEADS, CHUNK, CHUNK), F32)],
        scratch_shapes=[pltpu.VMEM((DN_HEADS, DN_D, DN_D), F32)],
        args=(act, act, act, zs, zs, alog_b, dtb_b, zb, zb, dn_norm, mix_gla))


def _dn_bwd(act, zb, zs, alog_b, dtb_b, dn_norm, states, t_invs, dmixed, bsz, nc, rider=None):
    t = zb.shape[0]
    rows, col, fixed, head = _dn_specs(nc, True)

    def body(q_ref, k_ref, v_ref, za_ref, zb_ref, al_ref, dt_ref, gz_ref, gb_ref, dn_ref, st_ref, ti_ref, dm_ref,
             dact_ref, dza_ref, dzb_ref, dgz_ref, dgb_ref, dal_ref, ddt_ref, ddn_ref, dstate):
        b, n = pl.program_id(0), pl.program_id(1)

        @pl.when((b == 0) & (n == 0))
        def _():
            dal_ref[...] = jnp.zeros_like(dal_ref)
            ddt_ref[...] = jnp.zeros_like(ddt_ref)
            ddn_ref[...] = jnp.zeros_like(ddn_ref)

        fn = functools.partial(_dn_chunk, t_saved=ti_ref[0, 0])
        _, vjp = jax.vjp(fn, st_ref[0, 0], _by_head(q_ref), _by_head(k_ref), _by_head(v_ref), za_ref[...],
                         zb_ref[...], al_ref[...], dt_ref[...], _by_head(gz_ref), _by_head(gb_ref), dn_ref[...])
        ds_in = jnp.where(n > 0, dstate[...], 0.0)
        ds, dq, dk, dv, dza, dzb, dal, ddt, dgz, dgb, ddn = vjp((_by_head(dm_ref).astype(F32), ds_in))
        dstate[...] = ds
        for j in range(DN_HB):
            for part, d in enumerate((dq, dk, dv)):
                dact_ref[:, pl.ds(part * DN_HEADS * DN_D + j * DN_D, DN_D)] = d[j]
            dgz_ref[:, _lanes(j)] = dgz[j].astype(dgz_ref.dtype)
            dgb_ref[:, _lanes(j)] = dgb[j].astype(dgb_ref.dtype)
        dal_ref[...] += dal
        ddt_ref[...] += ddt
        dza_ref[...] = dza
        dzb_ref[...] = dzb
        ddn_ref[...] += ddn

    full = lambda shape: pl.BlockSpec(shape, lambda b, n, h: (0,) * len(shape))
    return _hosted_call(
        body, rider, name="dn_bwd", grid=(bsz, nc, DN_HEADS // DN_HB),
        in_specs=[col(0), col(DN_HEADS * DN_D), col(2 * DN_HEADS * DN_D),
                  fixed(ZS_DA // LANE), fixed(ZS_DB // LANE), head, head,
                  col(ZB_DZ), col(ZB_GB), pl.BlockSpec((1, DN_D), lambda b, n, h: (0, 0)),
                  pl.BlockSpec((1, 1, DN_HB, DN_D, DN_D), lambda b, n, h: (b, nc - 1 - n, h, 0, 0)),
                  pl.BlockSpec((1, 1, DN_HB, CHUNK, CHUNK), lambda b, n, h: (b, nc - 1 - n, h, 0, 0)), col(0)],
        out_specs=[pl.BlockSpec((CHUNK, DN_QKV), lambda b, n, h: (rows(b, n, h), 0)), fixed(0), fixed(0), col(0), col(0),
                   full((DN_HEADS, 1, LANE)), full((DN_HEADS, 1, LANE)), full((1, DN_D))],
        out_shape=[jax.ShapeDtypeStruct((t, DN_QKV), F32),
                   jax.ShapeDtypeStruct((t, LANE), F32), jax.ShapeDtypeStruct((t, LANE), F32),
                   jax.ShapeDtypeStruct((t, D_MODEL), BF16), jax.ShapeDtypeStruct((t, D_MODEL), BF16),
                   jax.ShapeDtypeStruct((DN_HEADS, 1, LANE), F32), jax.ShapeDtypeStruct((DN_HEADS, 1, LANE), F32),
                   jax.ShapeDtypeStruct((1, DN_D), F32)],
        scratch_shapes=[pltpu.VMEM((DN_HEADS, DN_D, DN_D), F32)],
        args=(act, act, act, zs, zs, alog_b, dtb_b, zb, zb, dn_norm, states, t_invs, dmixed))


MM_VMEM_BUDGET = 40 * 1024 * 1024
MM_TILE_PREF = (1024, 1024, 2048)


def _divisor_tile(n, cap):
    if n <= cap:
        return n
    for c in range(cap - cap % LANE, 0, -LANE):
        if n % c == 0:
            return c
    return n


def _mm_tiles(m, n, kd, a_bytes, b_bytes, mn_bytes):
    tm, tn, tk = (_divisor_tile(d, c) for d, c in zip((m, n, kd), MM_TILE_PREF))

    def need(tm, tn, tk):
        acc = 0 if tk == kd else 4 * tm * tn
        return 2 * (tm * tk * a_bytes + tk * tn * b_bytes + tm * tn * mn_bytes) + acc + 4 * tm * tn

    while need(tm, tn, tk) > MM_VMEM_BUDGET:
        if tk > 512 and tk * max(tm * a_bytes, tn * b_bytes) >= tm * tn * mn_bytes:
            tk = _divisor_tile(kd, tk // 2)
        elif tn >= tm and tn > LANE:
            tn = _divisor_tile(n, tn // 2)
        else:
            tm = _divisor_tile(m, tm // 2)
    return tm, tn, tk


def _mm(a, b, *, ta=False, tb=False, out_dtypes=(F32,), epilogue=None, extras=(), name, rider=None):
    m, kd = (a.shape[1], a.shape[0]) if ta else a.shape
    n = b.shape[0] if tb else b.shape[1]
    mn_bytes = sum(e.dtype.itemsize for e in extras) + sum(jnp.dtype(dt).itemsize for dt in out_dtypes)
    tm, tn, tk = _mm_tiles(m, n, kd, a.dtype.itemsize, b.dtype.itemsize, mn_bytes)
    nk = kd // tk
    n_ex = len(extras)
    dims = (((0,) if ta else (1,), (1,) if tb else (0,)), ((), ()))

    def finish(acc, ex_refs, out_refs):
        outs = (acc,) if epilogue is None else epilogue(acc, *[r[...] for r in ex_refs])
        for r, o in zip(out_refs, outs):
            r[...] = o.astype(r.dtype)

    def partial_product(a_ref, b_ref):
        return lax.dot_general(a_ref[...].astype(BF16), b_ref[...].astype(BF16), dims, preferred_element_type=F32)

    def body_single(*refs):
        finish(partial_product(refs[0], refs[1]), refs[2:2 + n_ex], refs[2 + n_ex:])

    def body_acc(*refs):
        acc = refs[-1]
        k = pl.program_id(2)

        @pl.when(k == 0)
        def _():
            acc[...] = partial_product(refs[0], refs[1])

        @pl.when(k > 0)
        def _():
            acc[...] += partial_product(refs[0], refs[1])

        @pl.when(k == nk - 1)
        def _():
            finish(acc[...], refs[2:2 + n_ex], refs[2 + n_ex:-1])

    a_spec = pl.BlockSpec((tk, tm), lambda i, j, k: (k, i)) if ta else pl.BlockSpec((tm, tk), lambda i, j, k: (i, k))
    b_spec = pl.BlockSpec((tn, tk), lambda i, j, k: (j, k)) if tb else pl.BlockSpec((tk, tn), lambda i, j, k: (k, j))
    mn_spec = pl.BlockSpec((tm, tn), lambda i, j, k: (i, j))
    outs = _hosted_call(
        body_single if nk == 1 else body_acc, rider, name=name, grid=(m // tm, n // tn, nk),
        in_specs=[a_spec, b_spec] + [mn_spec] * n_ex,
        out_specs=[mn_spec] * len(out_dtypes),
        out_shape=[jax.ShapeDtypeStruct((m, n), dt) for dt in out_dtypes],
        scratch_shapes=[] if nk == 1 else [pltpu.VMEM((tm, tn), F32)],
        args=(a, b, *extras))
    return outs[0] if len(outs) == 1 else outs


ROW_BLOCK = 256


def _row_spec(width=D_MODEL):
    return pl.BlockSpec((ROW_BLOCK, width), lambda i: (i, 0))


def _vec_spec(width=D_MODEL):
    return pl.BlockSpec((1, width), lambda i: (0, 0))


def _rms_fwd(x, g, name):
    def body(x_ref, g_ref, h_ref):
        xf = x_ref[...]
        h_ref[...] = (xf * lax.rsqrt(jnp.mean(xf * xf, axis=-1, keepdims=True) + EPS) * g_ref[...]).astype(BF16)

    t = x.shape[0]
    return pl.pallas_call(
        body, name=name, grid=(t // ROW_BLOCK,), in_specs=[_row_spec(), _vec_spec()], out_specs=_row_spec(),
        out_shape=jax.ShapeDtypeStruct((t, D_MODEL), BF16), compiler_params=_cparams(("parallel",)),
    )(x, g)


def _rms_bwd_math(xf, g, dh):
    rstd = lax.rsqrt(jnp.mean(xf * xf, axis=-1, keepdims=True) + EPS)
    xhat = xf * rstd
    dxhat = dh * g
    dx = rstd * (dxhat - xhat * jnp.mean(dxhat * xhat, axis=-1, keepdims=True))
    dg = jnp.sum(dh * xhat, axis=0, keepdims=True)
    return dx, dg


def _rms_bwd(x, g, dh, dres, name):
    def body(x_ref, g_ref, dh_ref, dres_ref, dx_ref, dg_ref):
        dx, dg = _rms_bwd_math(x_ref[...], g_ref[...], dh_ref[...].astype(F32))
        dx_ref[...] = dres_ref[...] + dx

        @pl.when(pl.program_id(0) == 0)
        def _():
            dg_ref[...] = jnp.zeros_like(dg_ref)

        dg_ref[...] += dg

    t = x.shape[0]
    return pl.pallas_call(
        body, name=name, grid=(t // ROW_BLOCK,),
        in_specs=[_row_spec(), _vec_spec(), _row_spec(), _row_spec()], out_specs=[_row_spec(), _vec_spec()],
        out_shape=[jax.ShapeDtypeStruct((t, D_MODEL), F32), jax.ShapeDtypeStruct((1, D_MODEL), F32)],
        compiler_params=_cparams(("arbitrary",)),
    )(x, g, dh, dres)


def _loss_head(x3, g, target):
    def body(x_ref, g_ref, t_ref, dx_ref, dg_ref, loss_ref):
        xf, gg = x_ref[...], g_ref[...]
        rstd = lax.rsqrt(jnp.mean(xf * xf, axis=-1, keepdims=True) + EPS)
        err = xf * rstd * gg - t_ref[...]
        dx, dg = _rms_bwd_math(xf, gg, err * (1.0 / D_MODEL))
        dx_ref[...] = dx

        @pl.when(pl.program_id(0) == 0)
        def _():
            dg_ref[...] = jnp.zeros_like(dg_ref)
            loss_ref[...] = jnp.zeros_like(loss_ref)

        dg_ref[...] += dg
        part = jnp.sum(jnp.sum(err * err, axis=-1, keepdims=True), axis=0, keepdims=True) * (0.5 / D_MODEL)
        loss_ref[...] += jnp.broadcast_to(part, loss_ref.shape)

    t = x3.shape[0]
    return pl.pallas_call(
        body, name="loss_head", grid=(t // ROW_BLOCK,),
        in_specs=[_row_spec(), _vec_spec(), _row_spec()], out_specs=[_row_spec(), _vec_spec(), _vec_spec(LANE)],
        out_shape=[jax.ShapeDtypeStruct((t, D_MODEL), F32), jax.ShapeDtypeStruct((1, D_MODEL), F32),
                   jax.ShapeDtypeStruct((1, LANE), F32)],
        compiler_params=_cparams(("arbitrary",)),
    )(x3, g, target)


def _ple_bwd(dx3, gpre, pp):
    def body(dx_ref, gp_ref, pp_ref, dgp_ref, dpp_ref):
        dx, sg = dx_ref[...], _sigmoid(gp_ref[...])
        dpp_ref[...] = (dx * sg).astype(BF16)
        dgp_ref[...] = (dx * pp_ref[...] * sg * (1.0 - sg)).astype(BF16)

    t = dx3.shape[0]
    return pl.pallas_call(
        body, name="ple_bwd", grid=(t // ROW_BLOCK,), in_specs=[_row_spec()] * 3, out_specs=[_row_spec()] * 2,
        out_shape=[jax.ShapeDtypeStruct((t, D_MODEL), BF16)] * 2, compiler_params=_cparams(("parallel",)),
    )(dx3, gpre, pp)


CONV_COLS = 256


def _shift_down(x, s):
    if s == 0:
        return x
    return jnp.where(_iota2(x.shape, 0) >= s, pltpu.roll(x, s, 0), 0.0)


def _shift_up(x, s):
    if s == 0:
        return x
    rows = x.shape[0]
    return jnp.where(_iota2(x.shape, 0) < rows - s, pltpu.roll(x, rows - s, 0), 0.0)


def _conv_pre(xf, w):
    return sum(_shift_down(xf, DN_CONV - 1 - j) * w[j:j + 1, :] for j in range(DN_CONV))


def _conv_fwd(zb, conv_w, bsz, seq):
    def body(x_ref, w_ref, y_ref):
        y_ref[...] = _silu(_conv_pre(x_ref[...], w_ref[...]))

    nblk = DN_QKV // CONV_COLS
    return pl.pallas_call(
        body, name="conv_fwd", grid=(bsz, nblk),
        in_specs=[pl.BlockSpec((seq, CONV_COLS), lambda b, j: (b, ZB_DQKV // CONV_COLS + j)),
                  pl.BlockSpec((DN_CONV, CONV_COLS), lambda b, j: (0, j))],
        out_specs=pl.BlockSpec((seq, CONV_COLS), lambda b, j: (b, j)),
        out_shape=jax.ShapeDtypeStruct((bsz * seq, DN_QKV), F32),
        compiler_params=_cparams(("parallel", "parallel")),
    )(zb, conv_w)


def _conv_bwd(zb, conv_w, dact, bsz, seq):
    def body(x_ref, w_ref, dy_ref, dx_ref, dw_ref):
        xf, w = x_ref[...], w_ref[...]
        c = _conv_pre(xf, w)
        sg = _sigmoid(c)
        dc = dy_ref[...].astype(F32) * sg * (1.0 + c * (1.0 - sg))
        dx = sum(_shift_up(dc, DN_CONV - 1 - j) * w[j:j + 1, :] for j in range(DN_CONV))
        dx_ref[...] = dx.astype(BF16)
        dw = jnp.concatenate(
            [jnp.sum(dc * _shift_down(xf, DN_CONV - 1 - j), axis=0, keepdims=True) for j in range(DN_CONV)], axis=0)

        @pl.when(pl.program_id(1) == 0)
        def _():
            dw_ref[...] = jnp.zeros_like(dw_ref)

        dw_ref[...] += dw

    nblk = DN_QKV // CONV_COLS
    return pl.pallas_call(
        body, name="conv_bwd", grid=(nblk, bsz),
        in_specs=[pl.BlockSpec((seq, CONV_COLS), lambda j, b: (b, ZB_DQKV // CONV_COLS + j)),
                  pl.BlockSpec((DN_CONV, CONV_COLS), lambda j, b: (0, j)),
                  pl.BlockSpec((seq, CONV_COLS), lambda j, b: (b, j))],
        out_specs=[pl.BlockSpec((seq, CONV_COLS), lambda j, b: (b, j)),
                   pl.BlockSpec((DN_CONV, CONV_COLS), lambda j, b: (0, j))],
        out_shape=[jax.ShapeDtypeStruct((bsz * seq, DN_QKV), BF16), jax.ShapeDtypeStruct((DN_CONV, DN_QKV), F32)],
        compiler_params=_cparams(("parallel", "arbitrary")),
    )(zb, conv_w, dact)


MESH_IDS = pl.DeviceIdType.MESH
ANY_SPEC = pl.BlockSpec(memory_space=pl.ANY)


COMM_SCRATCH = (pltpu.SemaphoreType.DMA((7,)), pltpu.SemaphoreType.DMA((7,)), pltpu.SemaphoreType.DMA)


def _gather_phases(x_ref, out_ref, send_sems, recv_sems, local_sem):
    mx, my, mc = lax.axis_index("x"), lax.axis_index("y"), lax.axis_index("c")
    me, sibling = (mx, my, mc), (mx, my, 1 - mc)
    chips = [(1 - mx, my), (mx, 1 - my), (1 - mx, 1 - my)]

    def slot(px, py, pc):
        return out_ref.at[4 * px + 2 * py + pc]

    def copy(k, block, to, src=None):
        return pltpu.make_async_remote_copy(
            src_ref=slot(*block) if src is None else src, dst_ref=slot(*block),
            send_sem=send_sems.at[k], recv_sem=recv_sems.at[k], device_id=to, device_id_type=MESH_IDS)

    def mine():
        return pltpu.make_async_copy(x_ref, slot(*me), local_sem)

    def first():
        return [copy(0, me, sibling, src=x_ref)] + [copy(1 + j, me, (*chip, mc), src=x_ref)
                                                    for j, chip in enumerate(chips)]

    def passed():
        return [copy(4 + j, (*chip, mc), sibling) for j, chip in enumerate(chips)]

    def start():
        mine().start()
        for cp in first():
            cp.start()

    def forward():
        for j, (chip, cp) in enumerate(zip(chips, passed())):
            copy(1 + j, (*chip, mc), me).wait_recv()
            cp.start()

    def finish():
        copy(0, sibling, me).wait_recv()
        for j, chip in enumerate(chips):
            copy(4 + j, (*chip, 1 - mc), me).wait_recv()
        for cp in first() + passed():
            cp.wait_send()
        mine().wait()

    return start, forward, finish


def _scatter_phases(x_ref, out_ref, send_sems, recv_sems, local_sem, among_chips=False):
    mx, my, mc = lax.axis_index("x"), lax.axis_index("y"), lax.axis_index("c")
    n_peers = 4 if among_chips else N_DEV
    me = 2 * mx + my if among_chips else 4 * mx + 2 * my + mc

    def peer(k):
        if among_chips:
            return (mx ^ ((k >> 1) & 1), my ^ (k & 1), mc)
        return (mx ^ ((k >> 2) & 1), my ^ ((k >> 1) & 1), mc ^ (k & 1))

    def slot_of(k):
        px, py, pc = peer(k)
        return 2 * px + py if among_chips else 4 * px + 2 * py + pc

    def copy(k, src_slot, dst_slot):
        return pltpu.make_async_remote_copy(
            src_ref=x_ref.at[src_slot], dst_ref=out_ref.at[dst_slot],
            send_sem=send_sems.at[k - 1], recv_sem=recv_sems.at[k - 1],
            device_id=peer(k), device_id_type=MESH_IDS)

    def sends():
        return [copy(k, slot_of(k), me) for k in range(1, n_peers)]

    def mine():
        return pltpu.make_async_copy(x_ref.at[me], out_ref.at[me], local_sem)

    def start():
        mine().start()
        for cp in sends():
            cp.start()

    def forward():
        pass

    def finish():
        for k in range(1, n_peers):
            copy(k, me, slot_of(k)).wait_recv()
        for cp in sends():
            cp.wait_send()
        mine().wait()

    return start, forward, finish


def _pair_phases(x_ref, out_ref, send_sems, recv_sems, local_sem):
    mx, my, mc = lax.axis_index("x"), lax.axis_index("y"), lax.axis_index("c")

    def copy(side):
        return pltpu.make_async_remote_copy(
            src_ref=x_ref.at[:, side], dst_ref=out_ref, send_sem=send_sems.at[0], recv_sem=recv_sems.at[0],
            device_id=(mx, my, 1 - mc), device_id_type=MESH_IDS)

    def start():
        copy(1 - mc).start()

    def forward():
        pass

    def finish():
        copy(mc).wait_recv()
        copy(1 - mc).wait_send()

    return start, forward, finish


class _Rider:
    def __init__(self, phases, x, out_shape):
        self.phases, self.x, self.out_shape = phases, x, out_shape


def _gather_rider(x):
    return _Rider(_gather_phases, x, jax.ShapeDtypeStruct((N_DEV,) + x.shape, x.dtype))


def _scatter_rider(x):
    return _Rider(_scatter_phases, x, jax.ShapeDtypeStruct(x.shape, x.dtype))


def _chip_scatter_rider(x):
    return _Rider(functools.partial(_scatter_phases, among_chips=True), x, jax.ShapeDtypeStruct(x.shape, x.dtype))


def _pair_rider(x):
    return _Rider(_pair_phases, x, jax.ShapeDtypeStruct((x.shape[0],) + x.shape[2:], x.dtype))


def _exchange(rider, name):
    def body(x_ref, out_ref, send_sems, recv_sems, local_sem):
        for phase in rider.phases(x_ref, out_ref, send_sems, recv_sems, local_sem):
            phase()

    return pl.pallas_call(body, name=name, out_shape=rider.out_shape, in_specs=[ANY_SPEC], out_specs=ANY_SPEC,
                          scratch_shapes=list(COMM_SCRATCH))(rider.x)


def _all_gather(x, name):
    return _exchange(_gather_rider(x), name)


def _all_to_all(x, name):
    return _exchange(_scatter_rider(x), name)


def _hosted_call(body, rider, *, name, grid, in_specs, out_specs, out_shape, scratch_shapes, args):
    if rider is None:
        return pl.pallas_call(body, name=name, grid=grid, in_specs=in_specs, out_specs=out_specs, out_shape=out_shape,
                              scratch_shapes=scratch_shapes, compiler_params=_cparams(("arbitrary",) * len(grid)))(*args)
    n_in, n_out, n_scr = len(in_specs), len(out_specs), len(scratch_shapes)
    total = math.prod(grid)

    def riding(*refs):
        host_in, x_ref = refs[:n_in], refs[n_in]
        host_out, out_ref = refs[n_in + 1:n_in + 1 + n_out], refs[n_in + 1 + n_out]
        host_scr = refs[n_in + 2 + n_out:n_in + 2 + n_out + n_scr]
        start, forward, finish = rider.phases(x_ref, out_ref, *refs[n_in + 2 + n_out + n_scr:])
        step = 0
        for axis, size in enumerate(grid):
            step = step * size + pl.program_id(axis)
        pl.when(step == 0)(start)
        pl.when(step == total // 2)(forward)
        body(*host_in, *host_out, *host_scr)
        pl.when(step == total - 1)(finish)

    return pl.pallas_call(
        riding, name=name, grid=grid, in_specs=list(in_specs) + [ANY_SPEC], out_specs=list(out_specs) + [ANY_SPEC],
        out_shape=list(out_shape) + [rider.out_shape], scratch_shapes=list(scratch_shapes) + list(COMM_SCRATCH),
        compiler_params=_cparams(("arbitrary",) * len(grid)))(*args, rider.x)


def _adamw_math(w, g, m, v):
    m = ADAM_B1 * m + (1.0 - ADAM_B1) * g
    v = ADAM_B2 * v + (1.0 - ADAM_B2) * jnp.square(g)
    m_hat = m / (1.0 - ADAM_B1 ** ADAM_STEP)
    v_hat = v / (1.0 - ADAM_B2 ** ADAM_STEP)
    delta = -ADAM_LR * (m_hat / (jnp.sqrt(v_hat) + ADAM_EPS) + ADAM_WD * w)
    return delta, m, v


ADAM_ROWS = 128


def _add_blocks(a, b):
    g, rows, cols = a.shape
    tr = ADAM_ROWS if rows % ADAM_ROWS == 0 else rows

    def body(a_ref, b_ref, o_ref):
        o_ref[...] = (a_ref[...].astype(F32) + b_ref[...].astype(F32)).astype(o_ref.dtype)

    blk = pl.BlockSpec((None, tr, cols), lambda k, i: (k, i, 0))
    return pl.pallas_call(body, name="add_pair_blocks", grid=(g, rows // tr), in_specs=[blk, blk], out_specs=blk,
                          out_shape=jax.ShapeDtypeStruct(a.shape, a.dtype),
                          compiler_params=_cparams(("parallel", "parallel")))(a, b)


def _adamw_reduce(w, m, v, parts, row0, name):
    rows, cols = w.shape
    n_parts = parts.shape[0]
    tr = ADAM_ROWS if rows % ADAM_ROWS == 0 else rows
    r0 = row0 // tr

    def body(w_ref, m_ref, v_ref, *refs):
        part_refs, (g_ref, d_ref, nm_ref, nv_ref) = refs[:n_parts], refs[n_parts:]
        g = part_refs[0][...].astype(F32)
        for r in part_refs[1:]:
            g = g + r[...].astype(F32)
        delta, nm, nv = _adamw_math(w_ref[...], g, m_ref[...], v_ref[...])
        g_ref[...] = g
        d_ref[...] = delta
        nm_ref[...] = nm
        nv_ref[...] = nv

    blk = pl.BlockSpec((tr, cols), lambda i: (i, 0))
    part_specs = [pl.BlockSpec((None, tr, cols), functools.partial(lambda i, k: (k, r0 + i, 0), k=k))
                  for k in range(n_parts)]
    return pl.pallas_call(
        body, name=name, grid=(rows // tr,), in_specs=[blk] * 3 + part_specs, out_specs=[blk] * 4,
        out_shape=[jax.ShapeDtypeStruct(w.shape, F32)] * 4, compiler_params=_cparams(("parallel",)),
    )(w, m, v, *([parts] * n_parts))


def _small_reduce(gathered, lane_sum_from):
    r = gathered.shape[1]

    def body(g_ref, o_ref):
        g = g_ref[0]
        for k in range(1, N_DEV):
            g = g + g_ref[k]
        tot = jnp.broadcast_to(jnp.sum(g, axis=-1, keepdims=True), g.shape)
        o_ref[...] = jnp.where(_iota2(g.shape, 0) >= lane_sum_from, tot, g)

    return pl.pallas_call(body, name="small_grad_reduce", out_shape=jax.ShapeDtypeStruct((r, LANE), F32))(gathered)


def _adamw_small(w, m, v, g):
    def body(w_ref, m_ref, v_ref, g_ref, d_ref, nm_ref, nv_ref):
        d_ref[...], nm_ref[...], nv_ref[...] = _adamw_math(w_ref[...], g_ref[...], m_ref[...], v_ref[...])

    return pl.pallas_call(body, name="adamw_small", out_shape=[jax.ShapeDtypeStruct(w.shape, F32)] * 3)(w, m, v, g)


def _pack_rows(arrays):
    rows = [jnp.pad(a.reshape(-1), (0, -a.size % LANE)).reshape(-1, LANE) for a in arrays]
    out = jnp.concatenate(rows, axis=0)
    return jnp.pad(out, ((0, -out.shape[0] % 8), (0, 0)))


def _unpack_rows(packed, shapes):
    out, r = [], 0
    for shp in shapes:
        size = math.prod(shp)
        nrows = -(-size // LANE)
        out.append(packed[r:r + nrows].reshape(-1)[:size].reshape(shp))
        r += nrows
    return out


def _add_residual(acc, res):
    return (res + acc,)


def _by_cols(g):
    return g.reshape(g.shape[0], N_DEV, -1).transpose(1, 0, 2)


def _from_cols(blocks):
    return blocks.transpose(1, 0, 2).reshape(blocks.shape[1], -1)


ROWS_OUT = D_MODEL // N_DEV

W_IN_SEGMENTS = ((0, WI_LR, "big", 0), (WI_LR, WI_DQKV, "gates", ZS_LR), (WI_DQKV, WI_DA, "big", ZB_DQKV),
                 (WI_DA, WI_DB, "gates", ZS_DA), (WI_DB, WI_GA, "gates", ZS_DB), (WI_GA, D_IN, "big", ZB_GA))


def _w_in_from_shards(blocks):
    parts = {"big": [], "gates": []}
    for lo, hi, which, _ in W_IN_SEGMENTS:
        width = 0
        for j in range(N_DEV):
            a, b = max(lo, j * D_IN_SHARD), min(hi, (j + 1) * D_IN_SHARD)
            if a < b:
                parts[which].append(blocks[j][:, a - j * D_IN_SHARD:b - j * D_IN_SHARD])
                width += b - a
        if which == "gates":
            parts[which].append(jnp.zeros((blocks.shape[1], LANE - width), blocks.dtype))
    return jnp.concatenate(parts["big"], axis=1), jnp.concatenate(parts["gates"], axis=1)


def _w_in_grad_shards(dwb, dws):
    src = {"big": dwb, "gates": dws}
    blocks = []
    for j in range(N_DEV):
        pieces = []
        for lo, hi, which, first in W_IN_SEGMENTS:
            a, b = max(lo, j * D_IN_SHARD), min(hi, (j + 1) * D_IN_SHARD)
            if a < b:
                pieces.append(src[which][:, first + a - lo:first + b - lo])
        blocks.append(jnp.concatenate(pieces, axis=1))
    return jnp.stack(blocks, axis=0)


def _local_step(x, p, target, w, up_shard, rows_shard):
    bsz, seq, _ = x.shape
    t, nc = bsz * seq, seq // CHUNK
    x0, p2, tgt = x.reshape(t, D_MODEL), p.reshape(t, PLE_DIM), target.reshape(t, D_MODEL)

    h = _rms_fwd(x0, w["g_mix"], "rms_mix")
    zb, up_blocks = _mm(h, w["wb"], name="in_proj", rider=_gather_rider(up_shard))
    w_up = _from_cols(up_blocks)
    zs = _mm(h, w["ws"], name="in_proj_gates")
    act = _conv_fwd(zb, w["conv"], bsz, seq)
    mix_gla, gla_states = _gla_fwd(zb, zs, w["w2p"], w["gla_b"], w["gla_norm"], bsz, nc)
    mixed, dn_states, dn_t_invs, row_blocks = _dn_fwd(act, zb, zs, w["alog_b"], w["dtb_b"], w["dn_norm"], mix_gla,
                                                      bsz, nc, rider=_gather_rider(rows_shard))
    w_out = row_blocks[:, :ROWS_OUT].reshape(D_MODEL, D_MODEL)
    w_pg = row_blocks[:, ROWS_OUT:2 * ROWS_OUT].reshape(D_MODEL, D_MODEL)
    w_down = row_blocks[:, 2 * ROWS_OUT:].reshape(D_FF, D_MODEL)
    x1 = _mm(mixed, w_out, epilogue=_add_residual, extras=(x0,), name="out_proj")
    h2 = _rms_fwd(x1, w["g_mlp"], "rms_mlp")
    u, a = _mm(h2, w_up, out_dtypes=(BF16, BF16), name="mlp_up",
               epilogue=lambda acc: (acc, jnp.square(jnp.maximum(acc, 0.0))))
    x2 = _mm(a, w_down, epilogue=_add_residual, extras=(x1,), name="mlp_down")
    h3 = _rms_fwd(x2, w["g_ple"], "rms_ple")
    pp = _mm(p2, w["w_pp"], name="ple_proj")
    gpre, x3 = _mm(h3, w_pg, out_dtypes=(F32, F32), extras=(x2, pp), name="ple_gate",
                   epilogue=lambda acc, res, proj: (acc, res + _sigmoid(acc) * proj))
    dx3, dg_final, loss = _loss_head(x3, w["g_final"], tgt)

    dgpre, dpp = _ple_bwd(dx3, gpre, pp)
    dw_pp = _mm(p2, dpp, ta=True, out_dtypes=(BF16,), name="d_w_ple_proj")
    dw_pg = _mm(h3, dgpre, ta=True, out_dtypes=(BF16,), name="d_w_ple_gate")
    dh3 = _mm(dgpre, w_pg, tb=True, name="d_h_ple")
    dx2, dg_ple = _rms_bwd(x2, w["g_ple"], dh3, dx3, "rms_ple_bwd")
    du = _mm(dx2, w_down, tb=True, out_dtypes=(BF16,), extras=(u,), name="d_mlp_hidden",
             epilogue=lambda acc, uu: (acc * (2.0 * jnp.maximum(uu.astype(F32), 0.0)),))
    dw_down = _mm(a, dx2, ta=True, out_dtypes=(BF16,), name="d_w_down")
    dw_up = _mm(h2, du, ta=True, out_dtypes=(BF16,), name="d_w_up")
    dh2 = _mm(du, w_up, tb=True, name="d_h_mlp")
    dx1, dg_mlp = _rms_bwd(x1, w["g_mlp"], dh2, dx2, "rms_mlp_bwd")
    dmixed = _mm(dx1, w_out, tb=True, out_dtypes=(BF16,), name="d_mixed")
    dw_out = _mm(mixed, dx1, ta=True, out_dtypes=(BF16,), name="d_w_out")

    d_rows = jnp.concatenate([dw_out.reshape(N_DEV, ROWS_OUT, D_MODEL), dw_pg.reshape(N_DEV, ROWS_OUT, D_MODEL),
                              dw_down.reshape(N_DEV, D_FF // N_DEV, D_MODEL)], axis=1)
    (dact, dza, dzb_, dgz, dgb, dal, ddt, ddn, recv_rows) = _dn_bwd(
        act, zb, zs, w["alog_b"], w["dtb_b"], w["dn_norm"], dn_states, dn_t_invs, dmixed, bsz, nc,
        rider=_scatter_rider(d_rows))
    (gdq, gdk, gdv, dgg, dga, dlr, dw2, dgla_b, dgla_norm, recv_up) = _gla_bwd(
        zb, zs, w["w2p"], w["gla_b"], w["gla_norm"], gla_states, dmixed, bsz, nc,
        rider=_scatter_rider(_by_cols(dw_up)))
    dqkv, dconv = _conv_bwd(zb, w["conv"], dact, bsz, seq)
    dzb = jnp.concatenate([gdq, gdk, gdv, dgg, dqkv, dgz, dga, dgb], axis=1)
    dzs = jnp.concatenate([dlr, dza, dzb_], axis=1)
    dwb = _mm(h, dzb, ta=True, out_dtypes=(BF16,), name="d_w_in")
    dws = _mm(h, dzs, ta=True, out_dtypes=(BF16,), name="d_w_in_gates")
    by_chip = _w_in_grad_shards(dwb, dws).reshape(N_DEV // 2, 2, D_MODEL, D_IN_SHARD)
    from_sibling = _exchange(_pair_rider(by_chip), "pair_d_w_in")
    mine = lax.dynamic_index_in_dim(by_chip, lax.axis_index("c"), axis=1, keepdims=False)
    chip_sums = _add_blocks(mine, from_sibling)
    dh_gates = _mm(dzs, w["ws"], tb=True, name="d_h_mix_gates")
    dh, recv_in = _mm(dzb, w["wb"], tb=True, epilogue=_add_residual, extras=(dh_gates,), name="d_h_mix",
                      rider=_chip_scatter_rider(chip_sums))
    gx, dg_mix = _rms_bwd(x0, w["g_mix"], dh, dx1, "rms_mix_bwd")

    dgla_w2 = dw2[:, :GLA_LOWRANK, :].transpose(1, 0, 2).reshape(GLA_LOWRANK, GLA_QK)
    return dict(
        loss=loss[0, 0], grad_x=gx.reshape(x.shape), recv_in=recv_in, recv_up=recv_up, recv_rows=recv_rows,
        w_ple_proj=dw_pp,
        g_mix=dg_mix, gla_b=dgla_b.reshape(1, GLA_QK), gla_norm=dgla_norm, dn_norm=ddn, g_mlp=dg_mlp, g_ple=dg_ple,
        g_final=dg_final, gla_w2=dgla_w2, dn_conv=dconv,
        a_log_lanes=dal.reshape(DN_HEADS, LANE), dt_bias_lanes=ddt.reshape(DN_HEADS, LANE))


def _first_weights(g_mix, w_in, gla_w2, gla_b, gla_norm, dn_conv, dn_a_log, dn_dt_bias, dn_norm, g_mlp, g_ple,
                   w_ple_proj, g_final):
    wb, ws = _w_in_from_shards(_all_gather(w_in[0].astype(BF16), "gather_w_in"))
    w_pp = _all_gather(w_ple_proj[0].astype(BF16), "gather_w_ple_proj")
    small = _all_gather(_pack_rows([gla_w2[0], dn_conv[0]]), "gather_w_small")
    n_w2 = GLA_LOWRANK * GLA_QK // N_DEV // LANE
    n_cv = DN_CONV * DN_QKV // N_DEV // LANE
    w2 = small[:, :n_w2].reshape(N_DEV, GLA_LOWRANK, GLA_QK // N_DEV).transpose(1, 0, 2).reshape(GLA_LOWRANK, GLA_QK)
    conv = small[:, n_w2:n_w2 + n_cv].reshape(N_DEV, DN_CONV, DN_QKV // N_DEV).transpose(1, 0, 2).reshape(DN_CONV, DN_QKV)

    return dict(
        wb=wb, ws=ws, w_pp=_from_cols(w_pp),
        w2p=jnp.pad(w2, ((0, LANE - GLA_LOWRANK), (0, 0))), conv=conv,
        alog_b=jnp.broadcast_to(dn_a_log[0][:, None, None], (DN_HEADS, 1, LANE)),
        dtb_b=jnp.broadcast_to(dn_dt_bias[0][:, None, None], (DN_HEADS, 1, LANE)),
        g_mix=g_mix, gla_b=gla_b, gla_norm=gla_norm, dn_norm=dn_norm, g_mlp=g_mlp, g_ple=g_ple,
        g_final=g_final.reshape(1, D_MODEL))


def kernel(x, p, g_mix, w_in, gla_w2, gla_b, gla_norm, dn_conv, dn_a_log, dn_dt_bias, dn_norm, w_out, g_mlp, w_up, w_down, g_ple, w_ple_gate, w_ple_proj, g_final, loss_target, m_g_mix, m_w_in, m_gla_w2, m_gla_b, m_gla_norm, m_dn_conv, m_dn_a_log, m_dn_dt_bias, m_dn_norm, m_w_out, m_g_mlp, m_w_up, m_w_down, m_g_ple, m_w_ple_gate, m_w_ple_proj, m_g_final, v_g_mix, v_w_in, v_gla_w2, v_gla_b, v_gla_norm, v_dn_conv, v_dn_a_log, v_dn_dt_bias, v_dn_norm, v_w_out, v_g_mlp, v_w_up, v_w_down, v_g_ple, v_w_ple_gate, v_w_ple_proj, v_g_final):
    names = ["g_mix", "w_in", "gla_w2", "gla_b", "gla_norm", "dn_conv", "dn_a_log", "dn_dt_bias", "dn_norm", "w_out",
             "g_mlp", "w_up", "w_down", "g_ple", "w_ple_gate", "w_ple_proj", "g_final"]
    ws = dict(zip(names, (g_mix, w_in, gla_w2, gla_b, gla_norm, dn_conv, dn_a_log, dn_dt_bias, dn_norm, w_out, g_mlp,
                          w_up, w_down, g_ple, w_ple_gate, w_ple_proj, g_final)))
    ms = dict(zip(names, (m_g_mix, m_w_in, m_gla_w2, m_gla_b, m_gla_norm, m_dn_conv, m_dn_a_log, m_dn_dt_bias,
                          m_dn_norm, m_w_out, m_g_mlp, m_w_up, m_w_down, m_g_ple, m_w_ple_gate, m_w_ple_proj,
                          m_g_final)))
    vs = dict(zip(names, (v_g_mix, v_w_in, v_gla_w2, v_gla_b, v_gla_norm, v_dn_conv, v_dn_a_log, v_dn_dt_bias,
                          v_dn_norm, v_w_out, v_g_mlp, v_w_up, v_w_down, v_g_ple, v_w_ple_gate, v_w_ple_proj,
                          v_g_final)))
    me = 4 * lax.axis_index("x") + 2 * lax.axis_index("y") + lax.axis_index("c")

    first = _first_weights(g_mix, w_in, gla_w2, gla_b, gla_norm, dn_conv, dn_a_log, dn_dt_bias, dn_norm, g_mlp,
                           g_ple, w_ple_proj, g_final)
    rows_shard = jnp.concatenate([w_out[0], w_ple_gate[0], w_down[0]], axis=0).astype(BF16)
    r = _local_step(x, p[0], loss_target, first, w_up[0].astype(BF16), rows_shard)
    loss = lax.psum(r["loss"], ("x", "y", "c"))

    grads, deltas, new_m, new_v = {}, {}, {}, {}

    def big(name, parts, row0=0):
        g, d, nm, nv = _adamw_reduce(ws[name][0], ms[name][0], vs[name][0], parts, row0, "adamw_" + name)
        grads[name], deltas[name], new_m[name], new_v[name] = g[None], d[None], nm[None], nv[None]

    big("w_in", r["recv_in"])
    big("w_up", r["recv_up"])
    big("w_ple_proj", _all_to_all(_by_cols(r["w_ple_proj"]), "scatter_d_w_ple_proj"))
    big("w_out", r["recv_rows"], 0)
    big("w_ple_gate", r["recv_rows"], ROWS_OUT)
    big("w_down", r["recv_rows"], 2 * ROWS_OUT)

    vec_names = ["g_mix", "gla_b", "gla_norm", "dn_norm", "g_mlp", "g_ple", "g_final"]
    packed = _pack_rows([r[n] for n in vec_names] + [r["gla_w2"], r["dn_conv"]])
    lane_rows = packed.shape[0]
    packed = jnp.concatenate([packed, r["a_log_lanes"], r["dt_bias_lanes"]], axis=0)
    total = _small_reduce(_all_gather(packed, "gather_small_grads"), lane_rows)
    parts = _unpack_rows(total, [r[n].shape for n in vec_names] + [r["gla_w2"].shape, r["dn_conv"].shape])
    sg = dict(zip(vec_names, parts[:len(vec_names)]))
    sg["g_final"] = sg["g_final"].reshape(D_MODEL)
    sg["gla_w2"] = lax.dynamic_slice_in_dim(parts[-2], me * (GLA_QK // N_DEV), GLA_QK // N_DEV, axis=1)
    sg["dn_conv"] = lax.dynamic_slice_in_dim(parts[-1], me * (DN_QKV // N_DEV), DN_QKV // N_DEV, axis=1)
    sg["dn_a_log"] = total[lane_rows:lane_rows + DN_HEADS, 0]
    sg["dn_dt_bias"] = total[lane_rows + DN_HEADS:lane_rows + 2 * DN_HEADS, 0]
    small_names = vec_names + ["gla_w2", "dn_conv", "dn_a_log", "dn_dt_bias"]
    shapes = [ws[n].shape for n in small_names]
    d_s, m_s, v_s = _adamw_small(_pack_rows([ws[n] for n in small_names]), _pack_rows([ms[n] for n in small_names]),
                                 _pack_rows([vs[n] for n in small_names]), _pack_rows([sg[n] for n in small_names]))
    for n, d, nm, nv in zip(small_names, _unpack_rows(d_s, shapes), _unpack_rows(m_s, shapes), _unpack_rows(v_s, shapes)):
        grads[n], deltas[n], new_m[n], new_v[n] = sg[n].reshape(ws[n].shape), d, nm, nv

    return (loss, r["grad_x"], *[grads[n] for n in names], *[deltas[n] for n in names],
            *[new_m[n] for n in names], *[new_v[n] for n in names])
```

```python
import functools
import math

import jax
import jax.numpy as jnp
from jax import lax
from jax.experimental import pallas as pl
from jax.experimental.pallas import tpu as pltpu

F32 = jnp.float32
BF16 = jnp.bfloat16
HIGHEST = lax.Precision.HIGHEST

N_DEV = 8
D_MODEL = 2048
CHUNK = 64
PLE_DIM = 256
EPS = 1e-6
GLA_HEADS = 4
GLA_DK = 256
GLA_DV = 512
GLA_LOWRANK = 16
GLA_TAU = 16.0
DN_HEADS = 16
DN_D = 128
DN_CONV = 4
D_FF = 4 * D_MODEL
GLA_QK = GLA_HEADS * GLA_DK
GLA_V = GLA_HEADS * GLA_DV
DN_QKV = 3 * DN_HEADS * DN_D
D_IN = 2 * GLA_QK + 2 * GLA_V + GLA_LOWRANK + DN_QKV + D_MODEL + 2 * DN_HEADS + 2 * D_MODEL
D_IN_SHARD = D_IN // N_DEV

ADAM_LR = 0.001
ADAM_B1 = 0.9
ADAM_B2 = 0.999
ADAM_EPS = 1e-08
ADAM_WD = 0.01
ADAM_STEP = 10

LANE = 128
ZB_GQ, ZB_GK, ZB_GV, ZB_GG = 0, 1024, 2048, 4096
ZB_DQKV, ZB_DZ, ZB_GA, ZB_GB = 6144, 12288, 14336, 16384
ZB_W = 18432
ZS_LR, ZS_DA, ZS_DB = 0, 128, 256
ZS_W = 384
WI_LR = 2 * GLA_QK + 2 * GLA_V
WI_DQKV = WI_LR + GLA_LOWRANK
WI_DA = WI_DQKV + DN_QKV + D_MODEL
WI_DB = WI_DA + DN_HEADS
WI_GA = WI_DB + DN_HEADS

VMEM_LIMIT = 56 * 1024 * 1024

NN = (((1,), (0,)), ((), ()))
NT = (((1,), (1,)), ((), ()))
TN = (((0,), (0,)), ((), ()))


def _bdot(a, b, dims):
    return lax.dot_general(a.astype(BF16), b.astype(BF16), dims, preferred_element_type=F32)


def _split3(x):
    hi = x.astype(BF16)
    rest = x - hi.astype(F32)
    mid = rest.astype(BF16)
    return hi, mid, (rest - mid.astype(F32)).astype(BF16)


def _dot01(x, m, dims, x_first):
    m = m.astype(BF16)
    out = None
    for piece in _split3(x):
        d = lax.dot_general(piece, m, dims, preferred_element_type=F32) if x_first else \
            lax.dot_general(m, piece, dims, preferred_element_type=F32)
        out = d if out is None else out + d
    return out


@functools.partial(jax.custom_vjp, nondiff_argnums=(2, 3))
def _pick_dot(x, m, dims, dims_t):
    return _dot01(x, m, dims, True)


def _pick_dot_fwd(x, m, dims, dims_t):
    return _dot01(x, m, dims, True), m


def _pick_dot_bwd(dims, dims_t, m, ct):
    return _dot01(ct, m, dims_t, True), jnp.zeros_like(m)


_pick_dot.defvjp(_pick_dot_fwd, _pick_dot_bwd)


@functools.partial(jax.custom_vjp, nondiff_argnums=(2, 3))
def _left_dot(m, x, dims, dims_t):
    return _dot01(x, m, dims, False)


def _left_dot_fwd(m, x, dims, dims_t):
    return _dot01(x, m, dims, False), m


def _left_dot_bwd(dims, dims_t, m, ct):
    return jnp.zeros_like(m), _dot01(ct, m, dims_t, False)


_left_dot.defvjp(_left_dot_fwd, _left_dot_bwd)


def _dot3(a, b, dims):
    ah, bh = a.astype(BF16), b.astype(BF16)
    al, bl = (a - ah.astype(F32)).astype(BF16), (b - bh.astype(F32)).astype(BF16)
    dot = functools.partial(lax.dot_general, dimension_numbers=dims, preferred_element_type=F32)
    return dot(ah, bh) + (dot(ah, bl) + dot(al, bh))


def _sigmoid(x):
    return 1.0 / (1.0 + jnp.exp(-x))


def _silu(x):
    return x * _sigmoid(x)


def _softplus(x):
    return jnp.maximum(x, 0.0) + jnp.log(1.0 + jnp.exp(-jnp.abs(x)))


def _iota2(shape, dim):
    return lax.broadcasted_iota(jnp.int32, shape, dim)


def _cparams(sem=None):
    return pltpu.CompilerParams(dimension_semantics=sem, vmem_limit_bytes=VMEM_LIMIT)


BNN = (((2,), (1,)), ((0,), (0,)))
BNT = (((2,), (2,)), ((0,), (0,)))
BTN = (((1,), (1,)), ((0,), (0,)))


def _gla_chunk(st, q, k, v, lr, w2, b, gg, ga, gn):
    hb, c, _ = q.shape
    incl = (_iota2((c, c), 0) >= _iota2((c, c), 1))[None]
    tri = jnp.broadcast_to(incl.astype(F32), (hb, c, c))
    q = q.astype(F32) * (GLA_DK ** -0.5)
    k = k.astype(F32)
    v = v.astype(F32)
    lr_b = jnp.broadcast_to(lr[None], (hb,) + lr.shape)
    lf = -_softplus(-(_bdot(lr_b, w2, BNN) + b)) / GLA_TAU
    bcum = _left_dot(tri, lf, BNN, BTN)
    blast = jnp.sum(lf, axis=1, keepdims=True)
    q_in = q * jnp.exp(bcum)
    k_in = k * jnp.exp(-bcum)
    a = jnp.where(incl, _bdot(q_in, k_in, BNT), 0.0)
    o = _bdot(a, v, BNN) + _bdot(q_in, st, BNT)
    k_dec = k * jnp.exp(blast - bcum)
    st_new = st * jnp.exp(blast) + _bdot(v, k_dec, BTN)
    on = o * lax.rsqrt(jnp.mean(o * o, axis=-1, keepdims=True) + EPS) * gn
    res = _sigmoid(ga.astype(F32)) * on * _silu(gg.astype(F32))
    return res, st_new


def _tri_inv_raw(a):
    _, c, _ = a.shape
    eye = (_iota2((c, c), 0) == _iota2((c, c), 1)).astype(F32)[None]
    x = a
    p = eye - a
    for _ in range(5):
        x = _dot3(x, x, BNN)
        p = _dot3(p, eye + x, BNN)
    return p


def _tri_inv_bwd(t, dt):
    return (-_dot3(_dot3(t, dt, BTN), t, BNT),)


@jax.custom_vjp
def _tri_inv_given(a, t):
    return t


def _tri_inv_given_fwd(a, t):
    return t, t


def _tri_inv_given_bwd(t, dt):
    return _tri_inv_bwd(t, dt) + (jnp.zeros_like(t),)


_tri_inv_given.defvjp(_tri_inv_given_fwd, _tri_inv_given_bwd)


@functools.partial(jax.custom_vjp, nondiff_argnums=(1,))
def _column_on_lanes(z, j):
    picked = jnp.where(_iota2(z.shape, 1) == j, z, 0.0)
    return jnp.broadcast_to(jnp.sum(picked, axis=1, keepdims=True), z.shape)


def _column_on_lanes_fwd(z, j):
    return _column_on_lanes(z, j), None


def _column_on_lanes_bwd(j, _, ct):
    total = jnp.broadcast_to(jnp.sum(ct, axis=1, keepdims=True), ct.shape)
    return (jnp.where(_iota2(ct.shape, 1) == j, total, 0.0),)


_column_on_lanes.defvjp(_column_on_lanes_fwd, _column_on_lanes_bwd)


def _dn_chunk(s, qr, kr, vr, za, zb, alog, dtb, gz, gb, dn, t_saved=None):
    hb, c, _ = qr.shape
    row, col = _iota2((c, c), 0), _iota2((c, c), 1)
    incl = (row >= col)[None]
    strict = (row > col)[None]
    tri = jnp.broadcast_to(incl.astype(F32), (hb, c, c))
    eye = (row == col).astype(F32)[None]
    ones_cc = jnp.ones((hb, c, c), F32)
    lane0 = (lax.broadcasted_iota(jnp.int32, (hb, LANE, c), 1) == 0).astype(F32)

    def l2n(t):
        return t * lax.rsqrt(jnp.sum(t * t, axis=-1, keepdims=True) + EPS)

    q = l2n(qr.astype(F32)) * (DN_D ** -0.5)
    k = l2n(kr.astype(F32))
    v = vr.astype(F32)
    za_b = jnp.concatenate([_column_on_lanes(za, j)[None] for j in range(hb)], axis=0)
    zb_b = jnp.concatenate([_column_on_lanes(zb, j)[None] for j in range(hb)], axis=0)
    g = -jnp.exp(alog) * _softplus(za_b + dtb)
    beta = _sigmoid(zb_b)
    gcum = _left_dot(tri, g, BNN, BTN)
    glast = jnp.sum(g, axis=1, keepdims=True)
    cm = _pick_dot(gcum, lane0, BNN, BNT)
    rm = _left_dot(ones_cc, cm * eye, BNN, BTN)
    dec = jnp.exp(jnp.where(incl, cm - rm, -1e30))
    kb = k * beta
    a = jnp.where(strict, _bdot(kb, k, BNT) * dec, 0.0)
    t = _tri_inv_raw(a) if t_saved is None else _tri_inv_given(a, t_saved)
    egc = jnp.exp(gcum)
    u = _bdot(t, v * beta, BNN)
    w = _bdot(t, kb * egc, BNN)
    attn = jnp.where(incl, _bdot(q, k, BNT) * dec, 0.0)
    q_dec = q * egc
    k_dec = k * jnp.exp(glast - gcum)
    v_new = u - _bdot(w, s, BNN)
    o = _bdot(q_dec, s, BNN) + _bdot(attn, v_new, BNN)
    s_new = s * jnp.exp(glast) + _bdot(k_dec, v_new, BTN)
    on = o * lax.rsqrt(jnp.mean(o * o, axis=-1, keepdims=True) + EPS) * dn
    res = _sigmoid(gb.astype(F32)) * on * _silu(gz.astype(F32))
    return (res, s_new, t) if t_saved is None else (res, s_new)


def _heads(ref, n_heads, width):
    return jnp.stack([ref[:, j * width:(j + 1) * width] for j in range(n_heads)], axis=0)


def _gla_specs(nc, reverse):
    def rows(b, n):
        return b * nc + ((nc - 1 - n) if reverse else n)

    qk = lambda base: pl.BlockSpec((CHUNK, GLA_QK), lambda b, n: (rows(b, n), base // GLA_QK))
    vv = lambda base: pl.BlockSpec((CHUNK, GLA_V), lambda b, n: (rows(b, n), base // GLA_V))
    lr = lambda c: pl.BlockSpec((CHUNK, LANE), lambda b, n: (rows(b, n), c))
    full = lambda shape: pl.BlockSpec(shape, lambda b, n: (0,) * len(shape))
    return rows, qk, vv, lr, full


def _gla_inputs(q_ref, k_ref, v_ref, gg_ref, ga_ref, lr_ref, w2_ref, b_ref, gn_ref):
    return (_heads(q_ref, GLA_HEADS, GLA_DK), _heads(k_ref, GLA_HEADS, GLA_DK), _heads(v_ref, GLA_HEADS, GLA_DV),
            lr_ref[...], _heads(w2_ref, GLA_HEADS, GLA_DK), _heads(b_ref, GLA_HEADS, GLA_DK),
            _heads(gg_ref, GLA_HEADS, GLA_DV), _heads(ga_ref, GLA_HEADS, GLA_DV), gn_ref[...])


def _gla_fwd(zb, zs, w2p, gla_b, gla_norm, bsz, nc):
    t = zb.shape[0]
    rows, qk, vv, lr, full = _gla_specs(nc, False)

    def body(q_ref, k_ref, v_ref, gg_ref, ga_ref, lr_ref, w2_ref, b_ref, gn_ref, o_ref, st_ref, state):
        st = jnp.where(pl.program_id(1) > 0, state[...], 0.0)
        st_ref[0, 0] = st
        res, st_new = _gla_chunk(st, *_gla_inputs(q_ref, k_ref, v_ref, gg_ref, ga_ref, lr_ref, w2_ref, b_ref, gn_ref))
        for j in range(GLA_HEADS):
            o_ref[:, j * GLA_DV:(j + 1) * GLA_DV] = res[j]
        state[...] = st_new

    return pl.pallas_call(
        body, name="gla_fwd", grid=(bsz, nc),
        in_specs=[qk(ZB_GQ), qk(ZB_GK), vv(ZB_GV), vv(ZB_GG), vv(ZB_GA), lr(ZS_LR // LANE),
                  full((LANE, GLA_QK)), full((1, GLA_QK)), full((1, GLA_DV))],
        out_specs=[vv(0), pl.BlockSpec((1, 1, GLA_HEADS, GLA_DV, GLA_DK), lambda b, n: (b, n, 0, 0, 0))],
        out_shape=[jax.ShapeDtypeStruct((t, GLA_V), F32),
                   jax.ShapeDtypeStruct((bsz, nc, GLA_HEADS, GLA_DV, GLA_DK), F32)],
        scratch_shapes=[pltpu.VMEM((GLA_HEADS, GLA_DV, GLA_DK), F32)],
        compiler_params=_cparams(("arbitrary", "arbitrary")),
    )(zb, zb, zb, zb, zb, zs, w2p, gla_b, gla_norm)


def _gla_bwd(zb, zs, w2p, gla_b, gla_norm, states, dmixed, bsz, nc, rider=None):
    t = zb.shape[0]
    rows, qk, vv, lr, full = _gla_specs(nc, True)

    def body(q_ref, k_ref, v_ref, gg_ref, ga_ref, lr_ref, w2_ref, b_ref, gn_ref, st_ref, dm_ref,
             dq_ref, dk_ref, dv_ref, dgg_ref, dga_ref, dlr_ref, dw2_ref, db_ref, dgn_ref, dstate):
        b, n = pl.program_id(0), pl.program_id(1)

        @pl.when((b == 0) & (n == 0))
        def _():
            dw2_ref[...] = jnp.zeros_like(dw2_ref)
            db_ref[...] = jnp.zeros_like(db_ref)
            dgn_ref[...] = jnp.zeros_like(dgn_ref)

        _, vjp = jax.vjp(_gla_chunk, st_ref[0, 0],
                         *_gla_inputs(q_ref, k_ref, v_ref, gg_ref, ga_ref, lr_ref, w2_ref, b_ref, gn_ref))
        dst_in = jnp.where(n > 0, dstate[...], 0.0)
        dst, dq, dk, dv, dlr, dw2, db, dgg, dga, dgn = vjp((_heads(dm_ref, GLA_HEADS, GLA_DV).astype(F32), dst_in))
        dstate[...] = dst
        for j in range(GLA_HEADS):
            dq_ref[:, j * GLA_DK:(j + 1) * GLA_DK] = dq[j].astype(dq_ref.dtype)
            dk_ref[:, j * GLA_DK:(j + 1) * GLA_DK] = dk[j].astype(dk_ref.dtype)
            dv_ref[:, j * GLA_DV:(j + 1) * GLA_DV] = dv[j].astype(dv_ref.dtype)
            dgg_ref[:, j * GLA_DV:(j + 1) * GLA_DV] = dgg[j].astype(dgg_ref.dtype)
            dga_ref[:, j * GLA_DV:(j + 1) * GLA_DV] = dga[j].astype(dga_ref.dtype)
        dlr_ref[...] = dlr
        dw2_ref[...] += dw2
        db_ref[...] += db
        dgn_ref[...] += dgn

    return _hosted_call(
        body, rider, name="gla_bwd", grid=(bsz, nc),
        in_specs=[qk(ZB_GQ), qk(ZB_GK), vv(ZB_GV), vv(ZB_GG), vv(ZB_GA), lr(ZS_LR // LANE),
                  full((LANE, GLA_QK)), full((1, GLA_QK)), full((1, GLA_DV)),
                  pl.BlockSpec((1, 1, GLA_HEADS, GLA_DV, GLA_DK), lambda b, n: (b, nc - 1 - n, 0, 0, 0)),
                  vv(0)],
        out_specs=[qk(0), qk(0), vv(0), vv(0), vv(0), lr(0),
                   full((GLA_HEADS, LANE, GLA_DK)), full((GLA_HEADS, 1, GLA_DK)), full((1, GLA_DV))],
        out_shape=[jax.ShapeDtypeStruct((t, GLA_QK), BF16), jax.ShapeDtypeStruct((t, GLA_QK), BF16),
                   jax.ShapeDtypeStruct((t, GLA_V), BF16), jax.ShapeDtypeStruct((t, GLA_V), BF16),
                   jax.ShapeDtypeStruct((t, GLA_V), BF16), jax.ShapeDtypeStruct((t, LANE), F32),
                   jax.ShapeDtypeStruct((GLA_HEADS, LANE, GLA_DK), F32),
                   jax.ShapeDtypeStruct((GLA_HEADS, 1, GLA_DK), F32),
                   jax.ShapeDtypeStruct((1, GLA_DV), F32)],
        scratch_shapes=[pltpu.VMEM((GLA_HEADS, GLA_DV, GLA_DK), F32)],
        args=(zb, zb, zb, zb, zb, zs, w2p, gla_b, gla_norm, states, dmixed))


DN_HB = DN_HEADS


def _dn_specs(nc, reverse):
    wide = DN_HB * DN_D

    def rows(b, n, h):
        return b * nc + ((nc - 1 - n) if reverse else n)

    def col(base):
        return pl.BlockSpec((CHUNK, wide), lambda b, n, h: (rows(b, n, h), base // wide + h))

    def fixed(c):
        return pl.BlockSpec((CHUNK, LANE), lambda b, n, h: (rows(b, n, h), c))

    head = pl.BlockSpec((DN_HB, 1, LANE), lambda b, n, h: (h, 0, 0))
    return rows, col, fixed, head


def _lanes(j):
    return slice(j * DN_D, (j + 1) * DN_D)


def _by_head(ref):
    return jnp.stack([ref[:, _lanes(j)] for j in range(DN_HB)], axis=0)


def _dn_fwd(act, zb, zs, alog_b, dtb_b, dn_norm, mix_gla, bsz, nc, rider=None):
    t = zb.shape[0]
    rows, col, fixed, head = _dn_specs(nc, False)

    def body(q_ref, k_ref, v_ref, za_ref, zb_ref, al_ref, dt_ref, gz_ref, gb_ref, dn_ref, mg_ref,
             o_ref, st_ref, ti_ref, state):
        s = jnp.where(pl.program_id(1) > 0, state[...], 0.0)
        st_ref[0, 0] = s
        res, s_new, t_inv = _dn_chunk(s, _by_head(q_ref), _by_head(k_ref), _by_head(v_ref), za_ref[...], zb_ref[...],
                                      al_ref[...], dt_ref[...], _by_head(gz_ref), _by_head(gb_ref), dn_ref[...])
        ti_ref[0, 0] = t_inv
        for j in range(DN_HB):
            o_ref[:, _lanes(j)] = (res[j] + mg_ref[:, _lanes(j)]).astype(o_ref.dtype)
        state[...] = s_new

    return _hosted_call(
        body, rider, name="dn_fwd", grid=(bsz, nc, DN_HEADS // DN_HB),
        in_specs=[col(0), col(DN_HEADS * DN_D), col(2 * DN_HEADS * DN_D),
                  fixed(ZS_DA // LANE), fixed(ZS_DB // LANE), head, head,
                  col(ZB_DZ), col(ZB_GB), pl.BlockSpec((1, DN_D), lambda b, n, h: (0, 0)), col(0)],
        out_specs=[col(0), pl.BlockSpec((1, 1, DN_HB, DN_D, DN_D), lambda b, n, h: (b, n, h, 0, 0)),
                   pl.BlockSpec((1, 1, DN_HB, CHUNK, CHUNK), lambda b, n, h: (b, n, h, 0, 0))],
        out_shape=[jax.ShapeDtypeStruct((t, D_MODEL), BF16),
                   jax.ShapeDtypeStruct((bsz, nc, DN_HEADS, DN_D, DN_D), F32),
                   jax.ShapeDtypeStruct((bsz, nc, DN_HEADS, CHUNK, CHUNK), F32)],
        scratch_shapes=[pltpu.VMEM((DN_HEADS, DN_D, DN_D), F32)],
        args=(act, act, act, zs, zs, alog_b, dtb_b, zb, zb, dn_norm, mix_gla))


def _dn_bwd(act, zb, zs, alog_b, dtb_b, dn_norm, states, t_invs, dmixed, bsz, nc, rider=None):
    t = zb.shape[0]
    rows, col, fixed, head = _dn_specs(nc, True)

    def body(q_ref, k_ref, v_ref, za_ref, zb_ref, al_ref, dt_ref, gz_ref, gb_ref, dn_ref, st_ref, ti_ref, dm_ref,
             dact_ref, dza_ref, dzb_ref, dgz_ref, dgb_ref, dal_ref, ddt_ref, ddn_ref, dstate):
        b, n = pl.program_id(0), pl.program_id(1)

        @pl.when((b == 0) & (n == 0))
        def _():
            dal_ref[...] = jnp.zeros_like(dal_ref)
            ddt_ref[...] = jnp.zeros_like(ddt_ref)
            ddn_ref[...] = jnp.zeros_like(ddn_ref)

        fn = functools.partial(_dn_chunk, t_saved=ti_ref[0, 0])
        _, vjp = jax.vjp(fn, st_ref[0, 0], _by_head(q_ref), _by_head(k_ref), _by_head(v_ref), za_ref[...],
                         zb_ref[...], al_ref[...], dt_ref[...], _by_head(gz_ref), _by_head(gb_ref), dn_ref[...])
        ds_in = jnp.where(n > 0, dstate[...], 0.0)
        ds, dq, dk, dv, dza, dzb, dal, ddt, dgz, dgb, ddn = vjp((_by_head(dm_ref).astype(F32), ds_in))
        dstate[...] = ds
        for j in range(DN_HB):
            for part, d in enumerate((dq, dk, dv)):
                dact_ref[:, pl.ds(part * DN_HEADS * DN_D + j * DN_D, DN_D)] = d[j]
            dgz_ref[:, _lanes(j)] = dgz[j].astype(dgz_ref.dtype)
            dgb_ref[:, _lanes(j)] = dgb[j].astype(dgb_ref.dtype)
        dal_ref[...] += dal
        ddt_ref[...] += ddt
        dza_ref[...] = dza
        dzb_ref[...] = dzb
        ddn_ref[...] += ddn

    full = lambda shape: pl.BlockSpec(shape, lambda b, n, h: (0,) * len(shape))
    return _hosted_call(
        body, rider, name="dn_bwd", grid=(bsz, nc, DN_HEADS // DN_HB),
        in_specs=[col(0), col(DN_HEADS * DN_D), col(2 * DN_HEADS * DN_D),
                  fixed(ZS_DA // LANE), fixed(ZS_DB // LANE), head, head,
                  col(ZB_DZ), col(ZB_GB), pl.BlockSpec((1, DN_D), lambda b, n, h: (0, 0)),
                  pl.BlockSpec((1, 1, DN_HB, DN_D, DN_D), lambda b, n, h: (b, nc - 1 - n, h, 0, 0)),
                  pl.BlockSpec((1, 1, DN_HB, CHUNK, CHUNK), lambda b, n, h: (b, nc - 1 - n, h, 0, 0)), col(0)],
        out_specs=[pl.BlockSpec((CHUNK, DN_QKV), lambda b, n, h: (rows(b, n, h), 0)), fixed(0), fixed(0), col(0), col(0),
                   full((DN_HEADS, 1, LANE)), full((DN_HEADS, 1, LANE)), full((1, DN_D))],
        out_shape=[jax.ShapeDtypeStruct((t, DN_QKV), F32),
                   jax.ShapeDtypeStruct((t, LANE), F32), jax.ShapeDtypeStruct((t, LANE), F32),
                   jax.ShapeDtypeStruct((t, D_MODEL), BF16), jax.ShapeDtypeStruct((t, D_MODEL), BF16),
                   jax.ShapeDtypeStruct((DN_HEADS, 1, LANE), F32), jax.ShapeDtypeStruct((DN_HEADS, 1, LANE), F32),
                   jax.ShapeDtypeStruct((1, DN_D), F32)],
        scratch_shapes=[pltpu.VMEM((DN_HEADS, DN_D, DN_D), F32)],
        args=(act, act, act, zs, zs, alog_b, dtb_b, zb, zb, dn_norm, states, t_invs, dmixed))


MM_VMEM_BUDGET = 40 * 1024 * 1024
MM_TILE_PREF = (1024, 1024, 2048)


def _divisor_tile(n, cap):
    if n <= cap:
        return n
    for c in range(cap - cap % LANE, 0, -LANE):
        if n % c == 0:
            return c
    return n


def _mm_tiles(m, n, kd, a_bytes, b_bytes, mn_bytes):
    tm, tn, tk = (_divisor_tile(d, c) for d, c in zip((m, n, kd), MM_TILE_PREF))

    def need(tm, tn, tk):
        acc = 0 if tk == kd else 4 * tm * tn
        return 2 * (tm * tk * a_bytes + tk * tn * b_bytes + tm * tn * mn_bytes) + acc + 4 * tm * tn

    while need(tm, tn, tk) > MM_VMEM_BUDGET:
        if tk > 512 and tk * max(tm * a_bytes, tn * b_bytes) >= tm * tn * mn_bytes:
            tk = _divisor_tile(kd, tk // 2)
        elif tn >= tm and tn > LANE:
            tn = _divisor_tile(n, tn // 2)
        else:
            tm = _divisor_tile(m, tm // 2)
    return tm, tn, tk


def _mm(a, b, *, ta=False, tb=False, out_dtypes=(F32,), epilogue=None, extras=(), name, rider=None):
    m, kd = (a.shape[1], a.shape[0]) if ta else a.shape
    n = b.shape[0] if tb else b.shape[1]
    mn_bytes = sum(e.dtype.itemsize for e in extras) + sum(jnp.dtype(dt).itemsize for dt in out_dtypes)
    tm, tn, tk = _mm_tiles(m, n, kd, a.dtype.itemsize, b.dtype.itemsize, mn_bytes)
    nk = kd // tk
    n_ex = len(extras)
    dims = (((0,) if ta else (1,), (1,) if tb else (0,)), ((), ()))

    def finish(acc, ex_refs, out_refs):
        outs = (acc,) if epilogue is None else epilogue(acc, *[r[...] for r in ex_refs])
        for r, o in zip(out_refs, outs):
            r[...] = o.astype(r.dtype)

    def partial_product(a_ref, b_ref):
        return lax.dot_general(a_ref[...].astype(BF16), b_ref[...].astype(BF16), dims, preferred_element_type=F32)

    def body_single(*refs):
        finish(partial_product(refs[0], refs[1]), refs[2:2 + n_ex], refs[2 + n_ex:])

    def body_acc(*refs):
        acc = refs[-1]
        k = pl.program_id(2)

        @pl.when(k == 0)
        def _():
            acc[...] = partial_product(refs[0], refs[1])

        @pl.when(k > 0)
        def _():
            acc[...] += partial_product(refs[0], refs[1])

        @pl.when(k == nk - 1)
        def _():
            finish(acc[...], refs[2:2 + n_ex], refs[2 + n_ex:-1])

    a_spec = pl.BlockSpec((tk, tm), lambda i, j, k: (k, i)) if ta else pl.BlockSpec((tm, tk), lambda i, j, k: (i, k))
    b_spec = pl.BlockSpec((tn, tk), lambda i, j, k: (j, k)) if tb else pl.BlockSpec((tk, tn), lambda i, j, k: (k, j))
    mn_spec = pl.BlockSpec((tm, tn), lambda i, j, k: (i, j))
    outs = _hosted_call(
        body_single if nk == 1 else body_acc, rider, name=name, grid=(m // tm, n // tn, nk),
        in_specs=[a_spec, b_spec] + [mn_spec] * n_ex,
        out_specs=[mn_spec] * len(out_dtypes),
        out_shape=[jax.ShapeDtypeStruct((m, n), dt) for dt in out_dtypes],
        scratch_shapes=[] if nk == 1 else [pltpu.VMEM((tm, tn), F32)],
        args=(a, b, *extras))
    return outs[0] if len(outs) == 1 else outs


ROW_BLOCK = 256


def _row_spec(width=D_MODEL):
    return pl.BlockSpec((ROW_BLOCK, width), lambda i: (i, 0))


def _vec_spec(width=D_MODEL):
    return pl.BlockSpec((1, width), lambda i: (0, 0))


def _rms_fwd(x, g, name):
    def body(x_ref, g_ref, h_ref):
        xf = x_ref[...]
        h_ref[...] = (xf * lax.rsqrt(jnp.mean(xf * xf, axis=-1, keepdims=True) + EPS) * g_ref[...]).astype(BF16)

    t = x.shape[0]
    return pl.pallas_call(
        body, name=name, grid=(t // ROW_BLOCK,), in_specs=[_row_spec(), _vec_spec()], out_specs=_row_spec(),
        out_shape=jax.ShapeDtypeStruct((t, D_MODEL), BF16), compiler_params=_cparams(("parallel",)),
    )(x, g)


def _rms_bwd_math(xf, g, dh):
    rstd = lax.rsqrt(jnp.mean(xf * xf, axis=-1, keepdims=True) + EPS)
    xhat = xf * rstd
    dxhat = dh * g
    dx = rstd * (dxhat - xhat * jnp.mean(dxhat * xhat, axis=-1, keepdims=True))
    dg = jnp.sum(dh * xhat, axis=0, keepdims=True)
    return dx, dg


def _rms_bwd(x, g, dh, dres, name):
    def body(x_ref, g_ref, dh_ref, dres_ref, dx_ref, dg_ref):
        dx, dg = _rms_bwd_math(x_ref[...], g_ref[...], dh_ref[...].astype(F32))
        dx_ref[...] = dres_ref[...] + dx

        @pl.when(pl.program_id(0) == 0)
        def _():
            dg_ref[...] = jnp.zeros_like(dg_ref)

        dg_ref[...] += dg

    t = x.shape[0]
    return pl.pallas_call(
        body, name=name, grid=(t // ROW_BLOCK,),
        in_specs=[_row_spec(), _vec_spec(), _row_spec(), _row_spec()], out_specs=[_row_spec(), _vec_spec()],
        out_shape=[jax.ShapeDtypeStruct((t, D_MODEL), F32), jax.ShapeDtypeStruct((1, D_MODEL), F32)],
        compiler_params=_cparams(("arbitrary",)),
    )(x, g, dh, dres)


def _loss_head(x3, g, target):
    def body(x_ref, g_ref, t_ref, dx_ref, dg_ref, loss_ref):
        xf, gg = x_ref[...], g_ref[...]
        rstd = lax.rsqrt(jnp.mean(xf * xf, axis=-1, keepdims=True) + EPS)
        err = xf * rstd * gg - t_ref[...]
        dx, dg = _rms_bwd_math(xf, gg, err * (1.0 / D_MODEL))
        dx_ref[...] = dx

        @pl.when(pl.program_id(0) == 0)
        def _():
            dg_ref[...] = jnp.zeros_like(dg_ref)
            loss_ref[...] = jnp.zeros_like(loss_ref)

        dg_ref[...] += dg
        part = jnp.sum(jnp.sum(err * err, axis=-1, keepdims=True), axis=0, keepdims=True) * (0.5 / D_MODEL)
        loss_ref[...] += jnp.broadcast_to(part, loss_ref.shape)

    t = x3.shape[0]
    return pl.pallas_call(
        body, name="loss_head", grid=(t // ROW_BLOCK,),
        in_specs=[_row_spec(), _vec_spec(), _row_spec()], out_specs=[_row_spec(), _vec_spec(), _vec_spec(LANE)],
        out_shape=[jax.ShapeDtypeStruct((t, D_MODEL), F32), jax.ShapeDtypeStruct((1, D_MODEL), F32),
                   jax.ShapeDtypeStruct((1, LANE), F32)],
        compiler_params=_cparams(("arbitrary",)),
    )(x3, g, target)


def _ple_bwd(dx3, gpre, pp):
    def body(dx_ref, gp_ref, pp_ref, dgp_ref, dpp_ref):
        dx, sg = dx_ref[...], _sigmoid(gp_ref[...])
        dpp_ref[...] = (dx * sg).astype(BF16)
        dgp_ref[...] = (dx * pp_ref[...] * sg * (1.0 - sg)).astype(BF16)

    t = dx3.shape[0]
    return pl.pallas_call(
        body, name="ple_bwd", grid=(t // ROW_BLOCK,), in_specs=[_row_spec()] * 3, out_specs=[_row_spec()] * 2,
        out_shape=[jax.ShapeDtypeStruct((t, D_MODEL), BF16)] * 2, compiler_params=_cparams(("parallel",)),
    )(dx3, gpre, pp)


CONV_COLS = 256


def _shift_down(x, s):
    if s == 0:
        return x
    return jnp.where(_iota2(x.shape, 0) >= s, pltpu.roll(x, s, 0), 0.0)


def _shift_up(x, s):
    if s == 0:
        return x
    rows = x.shape[0]
    return jnp.where(_iota2(x.shape, 0) < rows - s, pltpu.roll(x, rows - s, 0), 0.0)


def _conv_pre(xf, w):
    return sum(_shift_down(xf, DN_CONV - 1 - j) * w[j:j + 1, :] for j in range(DN_CONV))


def _conv_fwd(zb, conv_w, bsz, seq):
    def body(x_ref, w_ref, y_ref):
        y_ref[...] = _silu(_conv_pre(x_ref[...], w_ref[...]))

    nblk = DN_QKV // CONV_COLS
    return pl.pallas_call(
        body, name="conv_fwd", grid=(bsz, nblk),
        in_specs=[pl.BlockSpec((seq, CONV_COLS), lambda b, j: (b, ZB_DQKV // CONV_COLS + j)),
                  pl.BlockSpec((DN_CONV, CONV_COLS), lambda b, j: (0, j))],
        out_specs=pl.BlockSpec((seq, CONV_COLS), lambda b, j: (b, j)),
        out_shape=jax.ShapeDtypeStruct((bsz * seq, DN_QKV), F32),
        compiler_params=_cparams(("parallel", "parallel")),
    )(zb, conv_w)


def _conv_bwd(zb, conv_w, dact, bsz, seq):
    def body(x_ref, w_ref, dy_ref, dx_ref, dw_ref):
        xf, w = x_ref[...], w_ref[...]
        c = _conv_pre(xf, w)
        sg = _sigmoid(c)
        dc = dy_ref[...].astype(F32) * sg * (1.0 + c * (1.0 - sg))
        dx = sum(_shift_up(dc, DN_CONV - 1 - j) * w[j:j + 1, :] for j in range(DN_CONV))
        dx_ref[...] = dx.astype(BF16)
        dw = jnp.concatenate(
            [jnp.sum(dc * _shift_down(xf, DN_CONV - 1 - j), axis=0, keepdims=True) for j in range(DN_CONV)], axis=0)

        @pl.when(pl.program_id(1) == 0)
        def _():
            dw_ref[...] = jnp.zeros_like(dw_ref)

        dw_ref[...] += dw

    nblk = DN_QKV // CONV_COLS
    return pl.pallas_call(
        body, name="conv_bwd", grid=(nblk, bsz),
        in_specs=[pl.BlockSpec((seq, CONV_COLS), lambda j, b: (b, ZB_DQKV // CONV_COLS + j)),
                  pl.BlockSpec((DN_CONV, CONV_COLS), lambda j, b: (0, j)),
                  pl.BlockSpec((seq, CONV_COLS), lambda j, b: (b, j))],
        out_specs=[pl.BlockSpec((seq, CONV_COLS), lambda j, b: (b, j)),
                   pl.BlockSpec((DN_CONV, CONV_COLS), lambda j, b: (0, j))],
        out_shape=[jax.ShapeDtypeStruct((bsz * seq, DN_QKV), BF16), jax.ShapeDtypeStruct((DN_CONV, DN_QKV), F32)],
        compiler_params=_cparams(("parallel", "arbitrary")),
    )(zb, conv_w, dact)


MESH_IDS = pl.DeviceIdType.MESH
ANY_SPEC = pl.BlockSpec(memory_space=pl.ANY)


COMM_SCRATCH = (pltpu.SemaphoreType.DMA((7,)), pltpu.SemaphoreType.DMA((7,)), pltpu.SemaphoreType.DMA)


def _gather_phases(x_ref, out_ref, send_sems, recv_sems, local_sem):
    mx, my, mc = lax.axis_index("x"), lax.axis_index("y"), lax.axis_index("c")
    me, sibling = (mx, my, mc), (mx, my, 1 - mc)
    chips = [(1 - mx, my), (mx, 1 - my), (1 - mx, 1 - my)]

    def slot(px, py, pc):
        return out_ref.at[4 * px + 2 * py + pc]

    def copy(k, block, to, src=None):
        return pltpu.make_async_remote_copy(
            src_ref=slot(*block) if src is None else src, dst_ref=slot(*block),
            send_sem=send_sems.at[k], recv_sem=recv_sems.at[k], device_id=to, device_id_type=MESH_IDS)

    def mine():
        return pltpu.make_async_copy(x_ref, slot(*me), local_sem)

    def first():
        return [copy(0, me, sibling, src=x_ref)] + [copy(1 + j, me, (*chip, mc), src=x_ref)
                                                    for j, chip in enumerate(chips)]

    def passed():
        return [copy(4 + j, (*chip, mc), sibling) for j, chip in enumerate(chips)]

    def start():
        mine().start()
        for cp in first():
            cp.start()

    def forward():
        for j, (chip, cp) in enumerate(zip(chips, passed())):
            copy(1 + j, (*chip, mc), me).wait_recv()
            cp.start()

    def finish():
        copy(0, sibling, me).wait_recv()
        for j, chip in enumerate(chips):
            copy(4 + j, (*chip, 1 - mc), me).wait_recv()
        for cp in first() + passed():
            cp.wait_send()
        mine().wait()

    return start, forward, finish


def _scatter_phases(x_ref, out_ref, send_sems, recv_sems, local_sem, among_chips=False):
    mx, my, mc = lax.axis_index("x"), lax.axis_index("y"), lax.axis_index("c")
    n_peers = 4 if among_chips else N_DEV
    me = 2 * mx + my if among_chips else 4 * mx + 2 * my + mc

    def peer(k):
        if among_chips:
            return (mx ^ ((k >> 1) & 1), my ^ (k & 1), mc)
        return (mx ^ ((k >> 2) & 1), my ^ ((k >> 1) & 1), mc ^ (k & 1))

    def slot_of(k):
        px, py, pc = peer(k)
        return 2 * px + py if among_chips else 4 * px + 2 * py + pc

    def copy(k, src_slot, dst_slot):
        return pltpu.make_async_remote_copy(
            src_ref=x_ref.at[src_slot], dst_ref=out_ref.at[dst_slot],
            send_sem=send_sems.at[k - 1], recv_sem=recv_sems.at[k - 1],
            device_id=peer(k), device_id_type=MESH_IDS)

    def sends():
        return [copy(k, slot_of(k), me) for k in range(1, n_peers)]

    def mine():
        return pltpu.make_async_copy(x_ref.at[me], out_ref.at[me], local_sem)

    def start():
        mine().start()
        for cp in sends():
            cp.start()

    def forward():
        pass

    def finish():
        for k in range(1, n_peers):
            copy(k, me, slot_of(k)).wait_recv()
        for cp in sends():
            cp.wait_send()
        mine().wait()

    return start, forward, finish


def _pair_phases(x_ref, out_ref, send_sems, recv_sems, local_sem):
    mx, my, mc = lax.axis_index("x"), lax.axis_index("y"), lax.axis_index("c")

    def copy(side):
        return pltpu.make_async_remote_copy(
            src_ref=x_ref.at[:, side], dst_ref=out_ref, send_sem=send_sems.at[0], recv_sem=recv_sems.at[0],
            device_id=(mx, my, 1 - mc), device_id_type=MESH_IDS)

    def start():
        copy(1 - mc).start()

    def forward():
        pass

    def finish():
        copy(mc).wait_recv()
        copy(1 - mc).wait_send()

    return start, forward, finish


class _Rider:
    def __init__(self, phases, x, out_shape):
        self.phases, self.x, self.out_shape = phases, x, out_shape


def _gather_rider(x):
    return _Rider(_gather_phases, x, jax.ShapeDtypeStruct((N_DEV,) + x.shape, x.dtype))


def _scatter_rider(x):
    return _Rider(_scatter_phases, x, jax.ShapeDtypeStruct(x.shape, x.dtype))


def _chip_scatter_rider(x):
    return _Rider(functools.partial(_scatter_phases, among_chips=True), x, jax.ShapeDtypeStruct(x.shape, x.dtype))


def _pair_rider(x):
    return _Rider(_pair_phases, x, jax.ShapeDtypeStruct((x.shape[0],) + x.shape[2:], x.dtype))


def _exchange(rider, name):
    def body(x_ref, out_ref, send_sems, recv_sems, local_sem):
        for phase in rider.phases(x_ref, out_ref, send_sems, recv_sems, local_sem):
            phase()

    return pl.pallas_call(body, name=name, out_shape=rider.out_shape, in_specs=[ANY_SPEC], out_specs=ANY_SPEC,
                          scratch_shapes=list(COMM_SCRATCH))(rider.x)


def _all_gather(x, name):
    return _exchange(_gather_rider(x), name)


def _all_to_all(x, name):
    return _exchange(_scatter_rider(x), name)


def _hosted_call(body, rider, *, name, grid, in_specs, out_specs, out_shape, scratch_shapes, args):
    if rider is None:
        return pl.pallas_call(body, name=name, grid=grid, in_specs=in_specs, out_specs=out_specs, out_shape=out_shape,
                              scratch_shapes=scratch_shapes, compiler_params=_cparams(("arbitrary",) * len(grid)))(*args)
    n_in, n_out, n_scr = len(in_specs), len(out_specs), len(scratch_shapes)
    total = math.prod(grid)

    def riding(*refs):
        host_in, x_ref = refs[:n_in], refs[n_in]
        host_out, out_ref = refs[n_in + 1:n_in + 1 + n_out], refs[n_in + 1 + n_out]
        host_scr = refs[n_in + 2 + n_out:n_in + 2 + n_out + n_scr]
        start, forward, finish = rider.phases(x_ref, out_ref, *refs[n_in + 2 + n_out + n_scr:])
        step = 0
        for axis, size in enumerate(grid):
            step = step * size + pl.program_id(axis)
        pl.when(step == 0)(start)
        pl.when(step == (3 * total) // 4)(forward)
        body(*host_in, *host_out, *host_scr)
        pl.when(step == total - 1)(finish)

    return pl.pallas_call(
        riding, name=name, grid=grid, in_specs=list(in_specs) + [ANY_SPEC], out_specs=list(out_specs) + [ANY_SPEC],
        out_shape=list(out_shape) + [rider.out_shape], scratch_shapes=list(scratch_shapes) + list(COMM_SCRATCH),
        compiler_params=_cparams(("arbitrary",) * len(grid)))(*args, rider.x)


def _adamw_math(w, g, m, v):
    m = ADAM_B1 * m + (1.0 - ADAM_B1) * g
    v = ADAM_B2 * v + (1.0 - ADAM_B2) * jnp.square(g)
    m_hat = m / (1.0 - ADAM_B1 ** ADAM_STEP)
    v_hat = v / (1.0 - ADAM_B2 ** ADAM_STEP)
    delta = -ADAM_LR * (m_hat / (jnp.sqrt(v_hat) + ADAM_EPS) + ADAM_WD * w)
    return delta, m, v


ADAM_ROWS = 128


def _elementwise_tile(rows, cols):
    if rows % ADAM_ROWS == 0:
        return ADAM_ROWS, cols
    return rows, (2 * LANE if cols % (2 * LANE) == 0 else cols)


def _add_blocks(a, b):
    g, rows, cols = a.shape
    tr, tc = _elementwise_tile(rows, cols)

    def body(a_ref, b_ref, o_ref):
        o_ref[...] = (a_ref[...].astype(F32) + b_ref[...].astype(F32)).astype(o_ref.dtype)

    blk = pl.BlockSpec((None, tr, tc), lambda k, i, j: (k, i, j))
    return pl.pallas_call(body, name="add_pair_blocks", grid=(g, rows // tr, cols // tc), in_specs=[blk, blk],
                          out_specs=blk, out_shape=jax.ShapeDtypeStruct(a.shape, a.dtype),
                          compiler_params=_cparams(("parallel", "parallel", "parallel")))(a, b)


def _adamw_reduce(w, m, v, parts, row0, name):
    rows, cols = w.shape
    n_parts = parts.shape[0]
    tr, tc = _elementwise_tile(rows, cols)
    r0 = row0 // tr

    def body(w_ref, m_ref, v_ref, *refs):
        part_refs, (g_ref, d_ref, nm_ref, nv_ref) = refs[:n_parts], refs[n_parts:]
        g = part_refs[0][...].astype(F32)
        for r in part_refs[1:]:
            g = g + r[...].astype(F32)
        delta, nm, nv = _adamw_math(w_ref[...], g, m_ref[...], v_ref[...])
        g_ref[...] = g
        d_ref[...] = delta
        nm_ref[...] = nm
        nv_ref[...] = nv

    blk = pl.BlockSpec((tr, tc), lambda i, j: (i, j))
    part_specs = [pl.BlockSpec((None, tr, tc), functools.partial(lambda i, j, k: (k, r0 + i, j), k=k))
                  for k in range(n_parts)]
    return pl.pallas_call(
        body, name=name, grid=(rows // tr, cols // tc), in_specs=[blk] * 3 + part_specs, out_specs=[blk] * 4,
        out_shape=[jax.ShapeDtypeStruct(w.shape, F32)] * 4, compiler_params=_cparams(("parallel", "parallel")),
    )(w, m, v, *([parts] * n_parts))


def _small_reduce(gathered, lane_sum_from):
    r = gathered.shape[1]

    def body(g_ref, o_ref):
        g = g_ref[0]
        for k in range(1, N_DEV):
            g = g + g_ref[k]
        tot = jnp.broadcast_to(jnp.sum(g, axis=-1, keepdims=True), g.shape)
        o_ref[...] = jnp.where(_iota2(g.shape, 0) >= lane_sum_from, tot, g)

    return pl.pallas_call(body, name="small_grad_reduce", out_shape=jax.ShapeDtypeStruct((r, LANE), F32))(gathered)


def _adamw_small(w, m, v, g):
    def body(w_ref, m_ref, v_ref, g_ref, d_ref, nm_ref, nv_ref):
        d_ref[...], nm_ref[...], nv_ref[...] = _adamw_math(w_ref[...], g_ref[...], m_ref[...], v_ref[...])

    return pl.pallas_call(body, name="adamw_small", out_shape=[jax.ShapeDtypeStruct(w.shape, F32)] * 3)(w, m, v, g)


def _pack_rows(arrays):
    rows = [jnp.pad(a.reshape(-1), (0, -a.size % LANE)).reshape(-1, LANE) for a in arrays]
    out = jnp.concatenate(rows, axis=0)
    return jnp.pad(out, ((0, -out.shape[0] % 8), (0, 0)))


def _unpack_rows(packed, shapes):
    out, r = [], 0
    for shp in shapes:
        size = math.prod(shp)
        nrows = -(-size // LANE)
        out.append(packed[r:r + nrows].reshape(-1)[:size].reshape(shp))
        r += nrows
    return out


def _add_residual(acc, res):
    return (res + acc,)


def _by_cols(g):
    return g.reshape(g.shape[0], N_DEV, -1).transpose(1, 0, 2)


def _from_cols(blocks):
    return blocks.transpose(1, 0, 2).reshape(blocks.shape[1], -1)


ROWS_OUT = D_MODEL // N_DEV

W_IN_SEGMENTS = ((0, WI_LR, "big", 0), (WI_LR, WI_DQKV, "gates", ZS_LR), (WI_DQKV, WI_DA, "big", ZB_DQKV),
                 (WI_DA, WI_DB, "gates", ZS_DA), (WI_DB, WI_GA, "gates", ZS_DB), (WI_GA, D_IN, "big", ZB_GA))


def _w_in_rows(wt):
    parts = {"big": [], "gates": []}
    for lo, hi, which, _ in W_IN_SEGMENTS:
        parts[which].append(wt[lo:hi])
        if which == "gates":
            parts[which].append(jnp.zeros((LANE - (hi - lo), wt.shape[1]), wt.dtype))
    return jnp.concatenate(parts["big"], axis=0), jnp.concatenate(parts["gates"], axis=0)


def _w_in_grad_rows(dwb_t, dws_t):
    src = {"big": dwb_t, "gates": dws_t}
    return jnp.concatenate([src[which][first:first + hi - lo] for lo, hi, which, first in W_IN_SEGMENTS], axis=0)


def _local_step(x, p, target, w, up_shard, rows_shard):
    bsz, seq, _ = x.shape
    t, nc = bsz * seq, seq // CHUNK
    x0, p2, tgt = x.reshape(t, D_MODEL), p.reshape(t, PLE_DIM), target.reshape(t, D_MODEL)

    h = _rms_fwd(x0, w["g_mix"], "rms_mix")
    zb, up_blocks = _mm(h, w["wb_t"], tb=True, name="in_proj", rider=_gather_rider(up_shard))
    w_up = _from_cols(up_blocks)
    zs = _mm(h, w["ws_t"], tb=True, name="in_proj_gates")
    act = _conv_fwd(zb, w["conv"], bsz, seq)
    mix_gla, gla_states = _gla_fwd(zb, zs, w["w2p"], w["gla_b"], w["gla_norm"], bsz, nc)
    mixed, dn_states, dn_t_invs, row_blocks = _dn_fwd(act, zb, zs, w["alog_b"], w["dtb_b"], w["dn_norm"], mix_gla,
                                                      bsz, nc, rider=_gather_rider(rows_shard))
    w_out = row_blocks[:, :ROWS_OUT].reshape(D_MODEL, D_MODEL)
    w_pg = row_blocks[:, ROWS_OUT:2 * ROWS_OUT].reshape(D_MODEL, D_MODEL)
    w_down = row_blocks[:, 2 * ROWS_OUT:].reshape(D_FF, D_MODEL)
    x1 = _mm(mixed, w_out, epilogue=_add_residual, extras=(x0,), name="out_proj")
    h2 = _rms_fwd(x1, w["g_mlp"], "rms_mlp")
    u, a = _mm(h2, w_up, out_dtypes=(BF16, BF16), name="mlp_up",
               epilogue=lambda acc: (acc, jnp.square(jnp.maximum(acc, 0.0))))
    x2 = _mm(a, w_down, epilogue=_add_residual, extras=(x1,), name="mlp_down")
    h3 = _rms_fwd(x2, w["g_ple"], "rms_ple")
    pp = _mm(p2, w["w_pp"], name="ple_proj")
    gpre, x3 = _mm(h3, w_pg, out_dtypes=(F32, F32), extras=(x2, pp), name="ple_gate",
                   epilogue=lambda acc, res, proj: (acc, res + _sigmoid(acc) * proj))
    dx3, dg_final, loss = _loss_head(x3, w["g_final"], tgt)

    dgpre, dpp = _ple_bwd(dx3, gpre, pp)
    dw_pp = _mm(p2, dpp, ta=True, out_dtypes=(BF16,), name="d_w_ple_proj")
    dw_pg = _mm(h3, dgpre, ta=True, out_dtypes=(BF16,), name="d_w_ple_gate")
    dh3 = _mm(dgpre, w_pg, tb=True, name="d_h_ple")
    dx2, dg_ple = _rms_bwd(x2, w["g_ple"], dh3, dx3, "rms_ple_bwd")
    du = _mm(dx2, w_down, tb=True, out_dtypes=(BF16,), extras=(u,), name="d_mlp_hidden",
             epilogue=lambda acc, uu: (acc * (2.0 * jnp.maximum(uu.astype(F32), 0.0)),))
    dw_down = _mm(a, dx2, ta=True, out_dtypes=(BF16,), name="d_w_down")
    dw_up = _mm(h2, du, ta=True, out_dtypes=(BF16,), name="d_w_up")
    dh2 = _mm(du, w_up, tb=True, name="d_h_mlp")
    dx1, dg_mlp = _rms_bwd(x1, w["g_mlp"], dh2, dx2, "rms_mlp_bwd")
    dmixed = _mm(dx1, w_out, tb=True, out_dtypes=(BF16,), name="d_mixed")
    dw_out = _mm(mixed, dx1, ta=True, out_dtypes=(BF16,), name="d_w_out")

    d_rows = jnp.concatenate([dw_out.reshape(N_DEV, ROWS_OUT, D_MODEL), dw_pg.reshape(N_DEV, ROWS_OUT, D_MODEL),
                              dw_down.reshape(N_DEV, D_FF // N_DEV, D_MODEL)], axis=1)
    (dact, dza, dzb_, dgz, dgb, dal, ddt, ddn, recv_rows) = _dn_bwd(
        act, zb, zs, w["alog_b"], w["dtb_b"], w["dn_norm"], dn_states, dn_t_invs, dmixed, bsz, nc,
        rider=_scatter_rider(d_rows))
    (gdq, gdk, gdv, dgg, dga, dlr, dw2, dgla_b, dgla_norm, recv_up) = _gla_bwd(
        zb, zs, w["w2p"], w["gla_b"], w["gla_norm"], gla_states, dmixed, bsz, nc,
        rider=_scatter_rider(_by_cols(dw_up)))
    dqkv, dconv = _conv_bwd(zb, w["conv"], dact, bsz, seq)
    dzb = jnp.concatenate([gdq, gdk, gdv, dgg, dqkv, dgz, dga, dgb], axis=1)
    dzs = jnp.concatenate([dlr, dza, dzb_], axis=1)
    dwb_t = _mm(dzb, h, ta=True, out_dtypes=(BF16,), name="d_w_in")
    dws_t = _mm(dzs, h, ta=True, out_dtypes=(BF16,), name="d_w_in_gates")
    by_chip = _w_in_grad_rows(dwb_t, dws_t).reshape(N_DEV // 2, 2, D_IN_SHARD, D_MODEL)
    from_sibling = _exchange(_pair_rider(by_chip), "pair_d_w_in")
    mine = lax.dynamic_index_in_dim(by_chip, lax.axis_index("c"), axis=1, keepdims=False)
    chip_sums = _add_blocks(mine, from_sibling)
    dh_gates = _mm(dzs, w["ws_t"], name="d_h_mix_gates")
    dh, recv_in = _mm(dzb, w["wb_t"], epilogue=_add_residual, extras=(dh_gates,), name="d_h_mix",
                      rider=_chip_scatter_rider(chip_sums))
    gx, dg_mix = _rms_bwd(x0, w["g_mix"], dh, dx1, "rms_mix_bwd")

    dgla_w2 = dw2[:, :GLA_LOWRANK, :].transpose(1, 0, 2).reshape(GLA_LOWRANK, GLA_QK)
    return dict(
        loss=loss[0, 0], grad_x=gx.reshape(x.shape), recv_in=recv_in, recv_up=recv_up, recv_rows=recv_rows,
        w_ple_proj=dw_pp,
        g_mix=dg_mix, gla_b=dgla_b.reshape(1, GLA_QK), gla_norm=dgla_norm, dn_norm=ddn, g_mlp=dg_mlp, g_ple=dg_ple,
        g_final=dg_final, gla_w2=dgla_w2, dn_conv=dconv,
        a_log_lanes=dal.reshape(DN_HEADS, LANE), dt_bias_lanes=ddt.reshape(DN_HEADS, LANE))


def _first_weights(g_mix, w_in, gla_w2, gla_b, gla_norm, dn_conv, dn_a_log, dn_dt_bias, dn_norm, g_mlp, g_ple,
                   w_ple_proj, g_final):
    w_in_t = _all_gather(jnp.swapaxes(w_in[0], 0, 1).astype(BF16), "gather_w_in").reshape(D_IN, D_MODEL)
    wb_t, ws_t = _w_in_rows(w_in_t)
    w_pp = _all_gather(w_ple_proj[0].astype(BF16), "gather_w_ple_proj")
    small = _all_gather(_pack_rows([gla_w2[0], dn_conv[0]]), "gather_w_small")
    n_w2 = GLA_LOWRANK * GLA_QK // N_DEV // LANE
    n_cv = DN_CONV * DN_QKV // N_DEV // LANE
    w2 = small[:, :n_w2].reshape(N_DEV, GLA_LOWRANK, GLA_QK // N_DEV).transpose(1, 0, 2).reshape(GLA_LOWRANK, GLA_QK)
    conv = small[:, n_w2:n_w2 + n_cv].reshape(N_DEV, DN_CONV, DN_QKV // N_DEV).transpose(1, 0, 2).reshape(DN_CONV, DN_QKV)

    return dict(
        wb_t=wb_t, ws_t=ws_t, w_pp=_from_cols(w_pp),
        w2p=jnp.pad(w2, ((0, LANE - GLA_LOWRANK), (0, 0))), conv=conv,
        alog_b=jnp.broadcast_to(dn_a_log[0][:, None, None], (DN_HEADS, 1, LANE)),
        dtb_b=jnp.broadcast_to(dn_dt_bias[0][:, None, None], (DN_HEADS, 1, LANE)),
        g_mix=g_mix, gla_b=gla_b, gla_norm=gla_norm, dn_norm=dn_norm, g_mlp=g_mlp, g_ple=g_ple,
        g_final=g_final.reshape(1, D_MODEL))


def kernel(x, p, g_mix, w_in, gla_w2, gla_b, gla_norm, dn_conv, dn_a_log, dn_dt_bias, dn_norm, w_out, g_mlp, w_up, w_down, g_ple, w_ple_gate, w_ple_proj, g_final, loss_target, m_g_mix, m_w_in, m_gla_w2, m_gla_b, m_gla_norm, m_dn_conv, m_dn_a_log, m_dn_dt_bias, m_dn_norm, m_w_out, m_g_mlp, m_w_up, m_w_down, m_g_ple, m_w_ple_gate, m_w_ple_proj, m_g_final, v_g_mix, v_w_in, v_gla_w2, v_gla_b, v_gla_norm, v_dn_conv, v_dn_a_log, v_dn_dt_bias, v_dn_norm, v_w_out, v_g_mlp, v_w_up, v_w_down, v_g_ple, v_w_ple_gate, v_w_ple_proj, v_g_final):
    names = ["g_mix", "w_in", "gla_w2", "gla_b", "gla_norm", "dn_conv", "dn_a_log", "dn_dt_bias", "dn_norm", "w_out",
             "g_mlp", "w_up", "w_down", "g_ple", "w_ple_gate", "w_ple_proj", "g_final"]
    ws = dict(zip(names, (g_mix, w_in, gla_w2, gla_b, gla_norm, dn_conv, dn_a_log, dn_dt_bias, dn_norm, w_out, g_mlp,
                          w_up, w_down, g_ple, w_ple_gate, w_ple_proj, g_final)))
    ms = dict(zip(names, (m_g_mix, m_w_in, m_gla_w2, m_gla_b, m_gla_norm, m_dn_conv, m_dn_a_log, m_dn_dt_bias,
                          m_dn_norm, m_w_out, m_g_mlp, m_w_up, m_w_down, m_g_ple, m_w_ple_gate, m_w_ple_proj,
                          m_g_final)))
    vs = dict(zip(names, (v_g_mix, v_w_in, v_gla_w2, v_gla_b, v_gla_norm, v_dn_conv, v_dn_a_log, v_dn_dt_bias,
                          v_dn_norm, v_w_out, v_g_mlp, v_w_up, v_w_down, v_g_ple, v_w_ple_gate, v_w_ple_proj,
                          v_g_final)))
    me = 4 * lax.axis_index("x") + 2 * lax.axis_index("y") + lax.axis_index("c")

    first = _first_weights(g_mix, w_in, gla_w2, gla_b, gla_norm, dn_conv, dn_a_log, dn_dt_bias, dn_norm, g_mlp,
                           g_ple, w_ple_proj, g_final)
    rows_shard = jnp.concatenate([w_out[0], w_ple_gate[0], w_down[0]], axis=0).astype(BF16)
    r = _local_step(x, p[0], loss_target, first, w_up[0].astype(BF16), rows_shard)
    loss = lax.psum(r["loss"], ("x", "y", "c"))

    grads, deltas, new_m, new_v = {}, {}, {}, {}

    def big(name, parts, row0=0):
        g, d, nm, nv = _adamw_reduce(ws[name][0], ms[name][0], vs[name][0], parts, row0, "adamw_" + name)
        grads[name], deltas[name], new_m[name], new_v[name] = g[None], d[None], nm[None], nv[None]

    t_outs = _adamw_reduce(*[jnp.swapaxes(d["w_in"][0], 0, 1) for d in (ws, ms, vs)], r["recv_in"], 0, "adamw_w_in")
    grads["w_in"], deltas["w_in"], new_m["w_in"], new_v["w_in"] = [jnp.swapaxes(o, 0, 1)[None] for o in t_outs]
    big("w_up", r["recv_up"])
    big("w_ple_proj", _all_to_all(_by_cols(r["w_ple_proj"]), "scatter_d_w_ple_proj"))
    big("w_out", r["recv_rows"], 0)
    big("w_ple_gate", r["recv_rows"], ROWS_OUT)
    big("w_down", r["recv_rows"], 2 * ROWS_OUT)

    vec_names = ["g_mix", "gla_b", "gla_norm", "dn_norm", "g_mlp", "g_ple", "g_final"]
    packed = _pack_rows([r[n] for n in vec_names] + [r["gla_w2"], r["dn_conv"]])
    lane_rows = packed.shape[0]
    packed = jnp.concatenate([packed, r["a_log_lanes"], r["dt_bias_lanes"]], axis=0)
    total = _small_reduce(_all_gather(packed, "gather_small_grads"), lane_rows)
    parts = _unpack_rows(total, [r[n].shape for n in vec_names] + [r["gla_w2"].shape, r["dn_conv"].shape])
    sg = dict(zip(vec_names, parts[:len(vec_names)]))
    sg["g_final"] = sg["g_final"].reshape(D_MODEL)
    sg["gla_w2"] = lax.dynamic_slice_in_dim(parts[-2], me * (GLA_QK // N_DEV), GLA_QK // N_DEV, axis=1)
    sg["dn_conv"] = lax.dynamic_slice_in_dim(parts[-1], me * (DN_QKV // N_DEV), DN_QKV // N_DEV, axis=1)
    sg["dn_a_log"] = total[lane_rows:lane_rows + DN_HEADS, 0]
    sg["dn_dt_bias"] = total[lane_rows + DN_HEADS:lane_rows + 2 * DN_HEADS, 0]
    small_names = vec_names + ["gla_w2", "dn_conv", "dn_a_log", "dn_dt_bias"]
    shapes = [ws[n].shape for n in small_names]
    d_s, m_s, v_s = _adamw_small(_pack_rows([ws[n] for n in small_names]), _pack_rows([ms[n] for n in small_names]),
                                 _pack_rows([vs[n] for n in small_names]), _pack_rows([sg[n] for n in small_names]))
    for n, d, nm, nv in zip(small_names, _unpack_rows(d_s, shapes), _unpack_rows(m_s, shapes), _unpack_rows(v_s, shapes)):
        grads[n], deltas[n], new_m[n], new_v[n] = sg[n].reshape(ws[n].shape), d, nm, nv

    return (loss, r["grad_x"], *[grads[n] for n in names], *[deltas[n] for n in names],
            *[new_m[n] for n in names], *[new_v[n] for n in names])
```

```python
import functools
import math

import jax
import jax.numpy as jnp
from jax import lax
from jax.experimental import pallas as pl
from jax.experimental.pallas import tpu as pltpu

F32 = jnp.float32
BF16 = jnp.bfloat16
HIGHEST = lax.Precision.HIGHEST

N_DEV = 8
D_MODEL = 2048
CHUNK = 64
PLE_DIM = 256
EPS = 1e-6
GLA_HEADS = 4
GLA_DK = 256
GLA_DV = 512
GLA_LOWRANK = 16
GLA_TAU = 16.0
DN_HEADS = 16
DN_D = 128
DN_CONV = 4
D_FF = 4 * D_MODEL
GLA_QK = GLA_HEADS * GLA_DK
GLA_V = GLA_HEADS * GLA_DV
DN_QKV = 3 * DN_HEADS * DN_D
D_IN = 2 * GLA_QK + 2 * GLA_V + GLA_LOWRANK + DN_QKV + D_MODEL + 2 * DN_HEADS + 2 * D_MODEL
D_IN_SHARD = D_IN // N_DEV

ADAM_LR = 0.001
ADAM_B1 = 0.9
ADAM_B2 = 0.999
ADAM_EPS = 1e-08
ADAM_WD = 0.01
ADAM_STEP = 10

LANE = 128
ZB_GQ, ZB_GK, ZB_GV, ZB_GG = 0, 1024, 2048, 4096
ZB_DQKV, ZB_DZ, ZB_GA, ZB_GB = 6144, 12288, 14336, 16384
ZB_W = 18432
ZS_LR, ZS_DA, ZS_DB = 0, 128, 256
ZS_W = 384
WI_LR = 2 * GLA_QK + 2 * GLA_V
WI_DQKV = WI_LR + GLA_LOWRANK
WI_DA = WI_DQKV + DN_QKV + D_MODEL
WI_DB = WI_DA + DN_HEADS
WI_GA = WI_DB + DN_HEADS

VMEM_LIMIT = 56 * 1024 * 1024

NN = (((1,), (0,)), ((), ()))
NT = (((1,), (1,)), ((), ()))
TN = (((0,), (0,)), ((), ()))


def _bdot(a, b, dims):
    return lax.dot_general(a.astype(BF16), b.astype(BF16), dims, preferred_element_type=F32)


def _split3(x):
    hi = x.astype(BF16)
    rest = x - hi.astype(F32)
    mid = rest.astype(BF16)
    return hi, mid, (rest - mid.astype(F32)).astype(BF16)


def _dot01(x, m, dims, x_first):
    m = m.astype(BF16)
    out = None
    for piece in _split3(x):
        d = lax.dot_general(piece, m, dims, preferred_element_type=F32) if x_first else \
            lax.dot_general(m, piece, dims, preferred_element_type=F32)
        out = d if out is None else out + d
    return out


@functools.partial(jax.custom_vjp, nondiff_argnums=(2, 3))
def _pick_dot(x, m, dims, dims_t):
    return _dot01(x, m, dims, True)


def _pick_dot_fwd(x, m, dims, dims_t):
    return _dot01(x, m, dims, True), m


def _pick_dot_bwd(dims, dims_t, m, ct):
    return _dot01(ct, m, dims_t, True), jnp.zeros_like(m)


_pick_dot.defvjp(_pick_dot_fwd, _pick_dot_bwd)


@functools.partial(jax.custom_vjp, nondiff_argnums=(2, 3))
def _left_dot(m, x, dims, dims_t):
    return _dot01(x, m, dims, False)


def _left_dot_fwd(m, x, dims, dims_t):
    return _dot01(x, m, dims, False), m


def _left_dot_bwd(dims, dims_t, m, ct):
    return jnp.zeros_like(m), _dot01(ct, m, dims_t, False)


_left_dot.defvjp(_left_dot_fwd, _left_dot_bwd)


def _dot3(a, b, dims):
    ah, bh = a.astype(BF16), b.astype(BF16)
    al, bl = (a - ah.astype(F32)).astype(BF16), (b - bh.astype(F32)).astype(BF16)
    dot = functools.partial(lax.dot_general, dimension_numbers=dims, preferred_element_type=F32)
    return dot(ah, bh) + (dot(ah, bl) + dot(al, bh))


def _sigmoid(x):
    return 1.0 / (1.0 + jnp.exp(-x))


def _silu(x):
    return x * _sigmoid(x)


def _softplus(x):
    return jnp.maximum(x, 0.0) + jnp.log(1.0 + jnp.exp(-jnp.abs(x)))


def _iota2(shape, dim):
    return lax.broadcasted_iota(jnp.int32, shape, dim)


def _cparams(sem=None):
    return pltpu.CompilerParams(dimension_semantics=sem, vmem_limit_bytes=VMEM_LIMIT)


BNN = (((2,), (1,)), ((0,), (0,)))
BNT = (((2,), (2,)), ((0,), (0,)))
BTN = (((1,), (1,)), ((0,), (0,)))


def _gla_chunk(st, q, k, v, lr, w2, b, gg, ga, gn):
    hb, c, _ = q.shape
    incl = (_iota2((c, c), 0) >= _iota2((c, c), 1))[None]
    tri = jnp.broadcast_to(incl.astype(F32), (hb, c, c))
    q = q.astype(F32) * (GLA_DK ** -0.5)
    k = k.astype(F32)
    v = v.astype(F32)
    lr_b = jnp.broadcast_to(lr[None], (hb,) + lr.shape)
    lf = -_softplus(-(_bdot(lr_b, w2, BNN) + b)) / GLA_TAU
    bcum = _left_dot(tri, lf, BNN, BTN)
    blast = jnp.sum(lf, axis=1, keepdims=True)
    q_in = q * jnp.exp(bcum)
    k_in = k * jnp.exp(-bcum)
    a = jnp.where(incl, _bdot(q_in, k_in, BNT), 0.0)
    o = _bdot(a, v, BNN) + _bdot(q_in, st, BNT)
    k_dec = k * jnp.exp(blast - bcum)
    st_new = st * jnp.exp(blast) + _bdot(v, k_dec, BTN)
    on = o * lax.rsqrt(jnp.mean(o * o, axis=-1, keepdims=True) + EPS) * gn
    res = _sigmoid(ga.astype(F32)) * on * _silu(gg.astype(F32))
    return res, st_new


def _tri_inv_raw(a):
    _, c, _ = a.shape
    eye = (_iota2((c, c), 0) == _iota2((c, c), 1)).astype(F32)[None]
    x = a
    p = eye - a
    for _ in range(5):
        x = _dot3(x, x, BNN)
        p = _dot3(p, eye + x, BNN)
    return p


def _tri_inv_bwd(t, dt):
    return (-_dot3(_dot3(t, dt, BTN), t, BNT),)


@jax.custom_vjp
def _tri_inv_given(a, t):
    return t


def _tri_inv_given_fwd(a, t):
    return t, t


def _tri_inv_given_bwd(t, dt):
    return _tri_inv_bwd(t, dt) + (jnp.zeros_like(t),)


_tri_inv_given.defvjp(_tri_inv_given_fwd, _tri_inv_given_bwd)


@functools.partial(jax.custom_vjp, nondiff_argnums=(1,))
def _column_on_lanes(z, j):
    picked = jnp.where(_iota2(z.shape, 1) == j, z, 0.0)
    return jnp.broadcast_to(jnp.sum(picked, axis=1, keepdims=True), z.shape)


def _column_on_lanes_fwd(z, j):
    return _column_on_lanes(z, j), None


def _column_on_lanes_bwd(j, _, ct):
    total = jnp.broadcast_to(jnp.sum(ct, axis=1, keepdims=True), ct.shape)
    return (jnp.where(_iota2(ct.shape, 1) == j, total, 0.0),)


_column_on_lanes.defvjp(_column_on_lanes_fwd, _column_on_lanes_bwd)


def _dn_chunk(s, qr, kr, vr, za, zb, alog, dtb, gz, gb, dn, t_saved=None):
    hb, c, _ = qr.shape
    row, col = _iota2((c, c), 0), _iota2((c, c), 1)
    incl = (row >= col)[None]
    strict = (row > col)[None]
    tri = jnp.broadcast_to(incl.astype(F32), (hb, c, c))
    eye = (row == col).astype(F32)[None]
    ones_cc = jnp.ones((hb, c, c), F32)
    lane0 = (lax.broadcasted_iota(jnp.int32, (hb, LANE, c), 1) == 0).astype(F32)

    def l2n(t):
        return t * lax.rsqrt(jnp.sum(t * t, axis=-1, keepdims=True) + EPS)

    q = l2n(qr.astype(F32)) * (DN_D ** -0.5)
    k = l2n(kr.astype(F32))
    v = vr.astype(F32)
    za_b = jnp.concatenate([_column_on_lanes(za, j)[None] for j in range(hb)], axis=0)
    zb_b = jnp.concatenate([_column_on_lanes(zb, j)[None] for j in range(hb)], axis=0)
    g = -jnp.exp(alog) * _softplus(za_b + dtb)
    beta = _sigmoid(zb_b)
    gcum = _left_dot(tri, g, BNN, BTN)
    glast = jnp.sum(g, axis=1, keepdims=True)
    cm = _pick_dot(gcum, lane0, BNN, BNT)
    rm = _left_dot(ones_cc, cm * eye, BNN, BTN)
    dec = jnp.exp(jnp.where(incl, cm - rm, -1e30))
    kb = k * beta
    a = jnp.where(strict, _bdot(kb, k, BNT) * dec, 0.0)
    t = _tri_inv_raw(a) if t_saved is None else _tri_inv_given(a, t_saved)
    egc = jnp.exp(gcum)
    u = _bdot(t, v * beta, BNN)
    w = _bdot(t, kb * egc, BNN)
    attn = jnp.where(incl, _bdot(q, k, BNT) * dec, 0.0)
    q_dec = q * egc
    k_dec = k * jnp.exp(glast - gcum)
    v_new = u - _bdot(w, s, BNN)
    o = _bdot(q_dec, s, BNN) + _bdot(attn, v_new, BNN)
    s_new = s * jnp.exp(glast) + _bdot(k_dec, v_new, BTN)
    on = o * lax.rsqrt(jnp.mean(o * o, axis=-1, keepdims=True) + EPS) * dn
    res = _sigmoid(gb.astype(F32)) * on * _silu(gz.astype(F32))
    return (res, s_new, t) if t_saved is None else (res, s_new)


def _heads(ref, n_heads, width):
    return jnp.stack([ref[:, j * width:(j + 1) * width] for j in range(n_heads)], axis=0)


def _gla_specs(nc, reverse):
    def rows(b, n):
        return b * nc + ((nc - 1 - n) if reverse else n)

    qk = lambda base: pl.BlockSpec((CHUNK, GLA_QK), lambda b, n: (rows(b, n), base // GLA_QK))
    vv = lambda base: pl.BlockSpec((CHUNK, GLA_V), lambda b, n: (rows(b, n), base // GLA_V))
    lr = lambda c: pl.BlockSpec((CHUNK, LANE), lambda b, n: (rows(b, n), c))
    full = lambda shape: pl.BlockSpec(shape, lambda b, n: (0,) * len(shape))
    return rows, qk, vv, lr, full


def _gla_inputs(q_ref, k_ref, v_ref, gg_ref, ga_ref, lr_ref, w2_ref, b_ref, gn_ref):
    return (_heads(q_ref, GLA_HEADS, GLA_DK), _heads(k_ref, GLA_HEADS, GLA_DK), _heads(v_ref, GLA_HEADS, GLA_DV),
            lr_ref[...], _heads(w2_ref, GLA_HEADS, GLA_DK), _heads(b_ref, GLA_HEADS, GLA_DK),
            _heads(gg_ref, GLA_HEADS, GLA_DV), _heads(ga_ref, GLA_HEADS, GLA_DV), gn_ref[...])


def _gla_fwd(zb, zs, w2p, gla_b, gla_norm, bsz, nc):
    t = zb.shape[0]
    rows, qk, vv, lr, full = _gla_specs(nc, False)

    def body(q_ref, k_ref, v_ref, gg_ref, ga_ref, lr_ref, w2_ref, b_ref, gn_ref, o_ref, st_ref, state):
        st = jnp.where(pl.program_id(1) > 0, state[...], 0.0)
        st_ref[0, 0] = st
        res, st_new = _gla_chunk(st, *_gla_inputs(q_ref, k_ref, v_ref, gg_ref, ga_ref, lr_ref, w2_ref, b_ref, gn_ref))
        for j in range(GLA_HEADS):
            o_ref[:, j * GLA_DV:(j + 1) * GLA_DV] = res[j]
        state[...] = st_new

    return pl.pallas_call(
        body, name="gla_fwd", grid=(bsz, nc),
        in_specs=[qk(ZB_GQ), qk(ZB_GK), vv(ZB_GV), vv(ZB_GG), vv(ZB_GA), lr(ZS_LR // LANE),
                  full((LANE, GLA_QK)), full((1, GLA_QK)), full((1, GLA_DV))],
        out_specs=[vv(0), pl.BlockSpec((1, 1, GLA_HEADS, GLA_DV, GLA_DK), lambda b, n: (b, n, 0, 0, 0))],
        out_shape=[jax.ShapeDtypeStruct((t, GLA_V), F32),
                   jax.ShapeDtypeStruct((bsz, nc, GLA_HEADS, GLA_DV, GLA_DK), F32)],
        scratch_shapes=[pltpu.VMEM((GLA_HEADS, GLA_DV, GLA_DK), F32)],
        compiler_params=_cparams(("arbitrary", "arbitrary")),
    )(zb, zb, zb, zb, zb, zs, w2p, gla_b, gla_norm)


def _gla_bwd(zb, zs, w2p, gla_b, gla_norm, states, dmixed, bsz, nc, rider=None):
    t = zb.shape[0]
    rows, qk, vv, lr, full = _gla_specs(nc, True)

    def body(q_ref, k_ref, v_ref, gg_ref, ga_ref, lr_ref, w2_ref, b_ref, gn_ref, st_ref, dm_ref,
             dq_ref, dk_ref, dv_ref, dgg_ref, dga_ref, dlr_ref, dw2_ref, db_ref, dgn_ref, dstate):
        b, n = pl.program_id(0), pl.program_id(1)

        @pl.when((b == 0) & (n == 0))
        def _():
            dw2_ref[...] = jnp.zeros_like(dw2_ref)
            db_ref[...] = jnp.zeros_like(db_ref)
            dgn_ref[...] = jnp.zeros_like(dgn_ref)

        _, vjp = jax.vjp(_gla_chunk, st_ref[0, 0],
                         *_gla_inputs(q_ref, k_ref, v_ref, gg_ref, ga_ref, lr_ref, w2_ref, b_ref, gn_ref))
        dst_in = jnp.where(n > 0, dstate[...], 0.0)
        dst, dq, dk, dv, dlr, dw2, db, dgg, dga, dgn = vjp((_heads(dm_ref, GLA_HEADS, GLA_DV).astype(F32), dst_in))
        dstate[...] = dst
        for j in range(GLA_HEADS):
            dq_ref[:, j * GLA_DK:(j + 1) * GLA_DK] = dq[j].astype(dq_ref.dtype)
            dk_ref[:, j * GLA_DK:(j + 1) * GLA_DK] = dk[j].astype(dk_ref.dtype)
            dv_ref[:, j * GLA_DV:(j + 1) * GLA_DV] = dv[j].astype(dv_ref.dtype)
            dgg_ref[:, j * GLA_DV:(j + 1) * GLA_DV] = dgg[j].astype(dgg_ref.dtype)
            dga_ref[:, j * GLA_DV:(j + 1) * GLA_DV] = dga[j].astype(dga_ref.dtype)
        dlr_ref[...] = dlr
        dw2_ref[...] += dw2
        db_ref[...] += db
        dgn_ref[...] += dgn

    return _hosted_call(
        body, rider, name="gla_bwd", grid=(bsz, nc),
        in_specs=[qk(ZB_GQ), qk(ZB_GK), vv(ZB_GV), vv(ZB_GG), vv(ZB_GA), lr(ZS_LR // LANE),
                  full((LANE, GLA_QK)), full((1, GLA_QK)), full((1, GLA_DV)),
                  pl.BlockSpec((1, 1, GLA_HEADS, GLA_DV, GLA_DK), lambda b, n: (b, nc - 1 - n, 0, 0, 0)),
                  vv(0)],
        out_specs=[qk(0), qk(0), vv(0), vv(0), vv(0), lr(0),
                   full((GLA_HEADS, LANE, GLA_DK)), full((GLA_HEADS, 1, GLA_DK)), full((1, GLA_DV))],
        out_shape=[jax.ShapeDtypeStruct((t, GLA_QK), BF16), jax.ShapeDtypeStruct((t, GLA_QK), BF16),
                   jax.ShapeDtypeStruct((t, GLA_V), BF16), jax.ShapeDtypeStruct((t, GLA_V), BF16),
                   jax.ShapeDtypeStruct((t, GLA_V), BF16), jax.ShapeDtypeStruct((t, LANE), F32),
                   jax.ShapeDtypeStruct((GLA_HEADS, LANE, GLA_DK), F32),
                   jax.ShapeDtypeStruct((GLA_HEADS, 1, GLA_DK), F32),
                   jax.ShapeDtypeStruct((1, GLA_DV), F32)],
        scratch_shapes=[pltpu.VMEM((GLA_HEADS, GLA_DV, GLA_DK), F32)],
        args=(zb, zb, zb, zb, zb, zs, w2p, gla_b, gla_norm, states, dmixed))


DN_HB = DN_HEADS


def _dn_specs(nc, reverse):
    wide = DN_HB * DN_D

    def rows(b, n, h):
        return b * nc + ((nc - 1 - n) if reverse else n)

    def col(base):
        return pl.BlockSpec((CHUNK, wide), lambda b, n, h: (rows(b, n, h), base // wide + h))

    def fixed(c):
        return pl.BlockSpec((CHUNK, LANE), lambda b, n, h: (rows(b, n, h), c))

    head = pl.BlockSpec((DN_HB, 1, LANE), lambda b, n, h: (h, 0, 0))
    return rows, col, fixed, head


def _lanes(j):
    return slice(j * DN_D, (j + 1) * DN_D)


def _by_head(ref):
    return jnp.stack([ref[:, _lanes(j)] for j in range(DN_HB)], axis=0)


def _dn_fwd(act, zb, zs, alog_b, dtb_b, dn_norm, mix_gla, bsz, nc, rider=None):
    t = zb.shape[0]
    rows, col, fixed, head = _dn_specs(nc, False)

    def body(q_ref, k_ref, v_ref, za_ref, zb_ref, al_ref, dt_ref, gz_ref, gb_ref, dn_ref, mg_ref,
             o_ref, st_ref, ti_ref, state):
        s = jnp.where(pl.program_id(1) > 0, state[...], 0.0)
        st_ref[0, 0] = s
        res, s_new, t_inv = _dn_chunk(s, _by_head(q_ref), _by_head(k_ref), _by_head(v_ref), za_ref[...], zb_ref[...],
                                      al_ref[...], dt_ref[...], _by_head(gz_ref), _by_head(gb_ref), dn_ref[...])
        ti_ref[0, 0] = t_inv
        for j in range(DN_HB):
            o_ref[:, _lanes(j)] = (res[j] + mg_ref[:, _lanes(j)]).astype(o_ref.dtype)
        state[...] = s_new

    return _hosted_call(
        body, rider, name="dn_fwd", grid=(bsz, nc, DN_HEADS // DN_HB),
        in_specs=[col(0), col(DN_HEADS * DN_D), col(2 * DN_HEADS * DN_D),
                  fixed(ZS_DA // LANE), fixed(ZS_DB // LANE), head, head,
                  col(ZB_DZ), col(ZB_GB), pl.BlockSpec((1, DN_D), lambda b, n, h: (0, 0)), col(0)],
        out_specs=[col(0), pl.BlockSpec((1, 1, DN_HB, DN_D, DN_D), lambda b, n, h: (b, n, h, 0, 0)),
                   pl.BlockSpec((1, 1, DN_HB, CHUNK, CHUNK), lambda b, n, h: (b, n, h, 0, 0))],
        out_shape=[jax.ShapeDtypeStruct((t, D_MODEL), BF16),
                   jax.ShapeDtypeStruct((bsz, nc, DN_HEADS, DN_D, DN_D), F32),
                   jax.ShapeDtypeStruct((bsz, nc, DN_HEADS, CHUNK, CHUNK), F32)],
        scratch_shapes=[pltpu.VMEM((DN_HEADS, DN_D, DN_D), F32)],
        args=(act, act, act, zs, zs, alog_b, dtb_b, zb, zb, dn_norm, mix_gla))


def _dn_bwd(act, zb, zs, alog_b, dtb_b, dn_norm, states, t_invs, dmixed, bsz, nc, rider=None):
    t = zb.shape[0]
    rows, col, fixed, head = _dn_specs(nc, True)

    def body(q_ref, k_ref, v_ref, za_ref, zb_ref, al_ref, dt_ref, gz_ref, gb_ref, dn_ref, st_ref, ti_ref, dm_ref,
             dact_ref, dza_ref, dzb_ref, dgz_ref, dgb_ref, dal_ref, ddt_ref, ddn_ref, dstate):
        b, n = pl.program_id(0), pl.program_id(1)

        @pl.when((b == 0) & (n == 0))
        def _():
            dal_ref[...] = jnp.zeros_like(dal_ref)
            ddt_ref[...] = jnp.zeros_like(ddt_ref)
            ddn_ref[...] = jnp.zeros_like(ddn_ref)

        fn = functools.partial(_dn_chunk, t_saved=ti_ref[0, 0])
        _, vjp = jax.vjp(fn, st_ref[0, 0], _by_head(q_ref), _by_head(k_ref), _by_head(v_ref), za_ref[...],
                         zb_ref[...], al_ref[...], dt_ref[...], _by_head(gz_ref), _by_head(gb_ref), dn_ref[...])
        ds_in = jnp.where(n > 0, dstate[...], 0.0)
        ds, dq, dk, dv, dza, dzb, dal, ddt, dgz, dgb, ddn = vjp((_by_head(dm_ref).astype(F32), ds_in))
        dstate[...] = ds
        for j in range(DN_HB):
            for part, d in enumerate((dq, dk, dv)):
                dact_ref[:, pl.ds(part * DN_HEADS * DN_D + j * DN_D, DN_D)] = d[j]
            dgz_ref[:, _lanes(j)] = dgz[j].astype(dgz_ref.dtype)
            dgb_ref[:, _lanes(j)] = dgb[j].astype(dgb_ref.dtype)
        dal_ref[...] += dal
        ddt_ref[...] += ddt
        dza_ref[...] = dza
        dzb_ref[...] = dzb
        ddn_ref[...] += ddn

    full = lambda shape: pl.BlockSpec(shape, lambda b, n, h: (0,) * len(shape))
    return _hosted_call(
        body, rider, name="dn_bwd", grid=(bsz, nc, DN_HEADS // DN_HB),
        in_specs=[col(0), col(DN_HEADS * DN_D), col(2 * DN_HEADS * DN_D),
                  fixed(ZS_DA // LANE), fixed(ZS_DB // LANE), head, head,
                  col(ZB_DZ), col(ZB_GB), pl.BlockSpec((1, DN_D), lambda b, n, h: (0, 0)),
                  pl.BlockSpec((1, 1, DN_HB, DN_D, DN_D), lambda b, n, h: (b, nc - 1 - n, h, 0, 0)),
                  pl.BlockSpec((1, 1, DN_HB, CHUNK, CHUNK), lambda b, n, h: (b, nc - 1 - n, h, 0, 0)), col(0)],
        out_specs=[pl.BlockSpec((CHUNK, DN_QKV), lambda b, n, h: (rows(b, n, h), 0)), fixed(0), fixed(0), col(0), col(0),
                   full((DN_HEADS, 1, LANE)), full((DN_HEADS, 1, LANE)), full((1, DN_D))],
        out_shape=[jax.ShapeDtypeStruct((t, DN_QKV), F32),
                   jax.ShapeDtypeStruct((t, LANE), F32), jax.ShapeDtypeStruct((t, LANE), F32),
                   jax.ShapeDtypeStruct((t, D_MODEL), BF16), jax.ShapeDtypeStruct((t, D_MODEL), BF16),
                   jax.ShapeDtypeStruct((DN_HEADS, 1, LANE), F32), jax.ShapeDtypeStruct((DN_HEADS, 1, LANE), F32),
                   jax.ShapeDtypeStruct((1, DN_D), F32)],
        scratch_shapes=[pltpu.VMEM((DN_HEADS, DN_D, DN_D), F32)],
        args=(act, act, act, zs, zs, alog_b, dtb_b, zb, zb, dn_norm, states, t_invs, dmixed))


MM_VMEM_BUDGET = 40 * 1024 * 1024
MM_TILE_PREF = (1024, 1024, 2048)


def _divisor_tile(n, cap):
    if n <= cap:
        return n
    for c in range(cap - cap % LANE, 0, -LANE):
        if n % c == 0:
            return c
    return n


def _mm_tiles(m, n, kd, a_bytes, b_bytes, mn_bytes):
    tm, tn, tk = (_divisor_tile(d, c) for d, c in zip((m, n, kd), MM_TILE_PREF))

    def need(tm, tn, tk):
        acc = 0 if tk == kd else 4 * tm * tn
        return 2 * (tm * tk * a_bytes + tk * tn * b_bytes + tm * tn * mn_bytes) + acc + 4 * tm * tn

    while need(tm, tn, tk) > MM_VMEM_BUDGET:
        if tk > 512 and tk * max(tm * a_bytes, tn * b_bytes) >= tm * tn * mn_bytes:
            tk = _divisor_tile(kd, tk // 2)
        elif tn >= tm and tn > LANE:
            tn = _divisor_tile(n, tn // 2)
        else:
            tm = _divisor_tile(m, tm // 2)
    return tm, tn, tk


def _mm(a, b, *, ta=False, tb=False, out_dtypes=(F32,), epilogue=None, extras=(), name, rider=None):
    m, kd = (a.shape[1], a.shape[0]) if ta else a.shape
    n = b.shape[0] if tb else b.shape[1]
    mn_bytes = sum(e.dtype.itemsize for e in extras) + sum(jnp.dtype(dt).itemsize for dt in out_dtypes)
    tm, tn, tk = _mm_tiles(m, n, kd, a.dtype.itemsize, b.dtype.itemsize, mn_bytes)
    nk = kd // tk
    n_ex = len(extras)
    dims = (((0,) if ta else (1,), (1,) if tb else (0,)), ((), ()))

    def finish(acc, ex_refs, out_refs):
        outs = (acc,) if epilogue is None else epilogue(acc, *[r[...] for r in ex_refs])
        for r, o in zip(out_refs, outs):
            r[...] = o.astype(r.dtype)

    def partial_product(a_ref, b_ref):
        return lax.dot_general(a_ref[...].astype(BF16), b_ref[...].astype(BF16), dims, preferred_element_type=F32)

    def body_single(*refs):
        finish(partial_product(refs[0], refs[1]), refs[2:2 + n_ex], refs[2 + n_ex:])

    def body_acc(*refs):
        acc = refs[-1]
        k = pl.program_id(2)

        @pl.when(k == 0)
        def _():
            acc[...] = partial_product(refs[0], refs[1])

        @pl.when(k > 0)
        def _():
            acc[...] += partial_product(refs[0], refs[1])

        @pl.when(k == nk - 1)
        def _():
            finish(acc[...], refs[2:2 + n_ex], refs[2 + n_ex:-1])

    a_spec = pl.BlockSpec((tk, tm), lambda i, j, k: (k, i)) if ta else pl.BlockSpec((tm, tk), lambda i, j, k: (i, k))
    b_spec = pl.BlockSpec((tn, tk), lambda i, j, k: (j, k)) if tb else pl.BlockSpec((tk, tn), lambda i, j, k: (k, j))
    mn_spec = pl.BlockSpec((tm, tn), lambda i, j, k: (i, j))
    outs = _hosted_call(
        body_single if nk == 1 else body_acc, rider, name=name, grid=(m // tm, n // tn, nk),
        in_specs=[a_spec, b_spec] + [mn_spec] * n_ex,
        out_specs=[mn_spec] * len(out_dtypes),
        out_shape=[jax.ShapeDtypeStruct((m, n), dt) for dt in out_dtypes],
        scratch_shapes=[] if nk == 1 else [pltpu.VMEM((tm, tn), F32)],
        args=(a, b, *extras))
    return outs[0] if len(outs) == 1 else outs


ROW_BLOCK = 256


def _row_spec(width=D_MODEL):
    return pl.BlockSpec((ROW_BLOCK, width), lambda i: (i, 0))


def _vec_spec(width=D_MODEL):
    return pl.BlockSpec((1, width), lambda i: (0, 0))


def _rms_fwd(x, g, name):
    def body(x_ref, g_ref, h_ref):
        xf = x_ref[...]
        h_ref[...] = (xf * lax.rsqrt(jnp.mean(xf * xf, axis=-1, keepdims=True) + EPS) * g_ref[...]).astype(BF16)

    t = x.shape[0]
    return pl.pallas_call(
        body, name=name, grid=(t // ROW_BLOCK,), in_specs=[_row_spec(), _vec_spec()], out_specs=_row_spec(),
        out_shape=jax.ShapeDtypeStruct((t, D_MODEL), BF16), compiler_params=_cparams(("parallel",)),
    )(x, g)


def _rms_bwd_math(xf, g, dh):
    rstd = lax.rsqrt(jnp.mean(xf * xf, axis=-1, keepdims=True) + EPS)
    xhat = xf * rstd
    dxhat = dh * g
    dx = rstd * (dxhat - xhat * jnp.mean(dxhat * xhat, axis=-1, keepdims=True))
    dg = jnp.sum(dh * xhat, axis=0, keepdims=True)
    return dx, dg


def _rms_bwd(x, g, dh, dres, name):
    def body(x_ref, g_ref, dh_ref, dres_ref, dx_ref, dg_ref):
        dx, dg = _rms_bwd_math(x_ref[...], g_ref[...], dh_ref[...].astype(F32))
        dx_ref[...] = dres_ref[...] + dx

        @pl.when(pl.program_id(0) == 0)
        def _():
            dg_ref[...] = jnp.zeros_like(dg_ref)

        dg_ref[...] += dg

    t = x.shape[0]
    return pl.pallas_call(
        body, name=name, grid=(t // ROW_BLOCK,),
        in_specs=[_row_spec(), _vec_spec(), _row_spec(), _row_spec()], out_specs=[_row_spec(), _vec_spec()],
        out_shape=[jax.ShapeDtypeStruct((t, D_MODEL), F32), jax.ShapeDtypeStruct((1, D_MODEL), F32)],
        compiler_params=_cparams(("arbitrary",)),
    )(x, g, dh, dres)


def _loss_head(x3, g, target):
    def body(x_ref, g_ref, t_ref, dx_ref, dg_ref, loss_ref):
        xf, gg = x_ref[...], g_ref[...]
        rstd = lax.rsqrt(jnp.mean(xf * xf, axis=-1, keepdims=True) + EPS)
        err = xf * rstd * gg - t_ref[...]
        dx, dg = _rms_bwd_math(xf, gg, err * (1.0 / D_MODEL))
        dx_ref[...] = dx

        @pl.when(pl.program_id(0) == 0)
        def _():
            dg_ref[...] = jnp.zeros_like(dg_ref)
            loss_ref[...] = jnp.zeros_like(loss_ref)

        dg_ref[...] += dg
        part = jnp.sum(jnp.sum(err * err, axis=-1, keepdims=True), axis=0, keepdims=True) * (0.5 / D_MODEL)
        loss_ref[...] += jnp.broadcast_to(part, loss_ref.shape)

    t = x3.shape[0]
    return pl.pallas_call(
        body, name="loss_head", grid=(t // ROW_BLOCK,),
        in_specs=[_row_spec(), _vec_spec(), _row_spec()], out_specs=[_row_spec(), _vec_spec(), _vec_spec(LANE)],
        out_shape=[jax.ShapeDtypeStruct((t, D_MODEL), F32), jax.ShapeDtypeStruct((1, D_MODEL), F32),
                   jax.ShapeDtypeStruct((1, LANE), F32)],
        compiler_params=_cparams(("arbitrary",)),
    )(x3, g, target)


def _ple_bwd(dx3, gpre, pp):
    def body(dx_ref, gp_ref, pp_ref, dgp_ref, dpp_ref):
        dx, sg = dx_ref[...], _sigmoid(gp_ref[...])
        dpp_ref[...] = (dx * sg).astype(BF16)
        dgp_ref[...] = (dx * pp_ref[...] * sg * (1.0 - sg)).astype(BF16)

    t = dx3.shape[0]
    return pl.pallas_call(
        body, name="ple_bwd", grid=(t // ROW_BLOCK,), in_specs=[_row_spec()] * 3, out_specs=[_row_spec()] * 2,
        out_shape=[jax.ShapeDtypeStruct((t, D_MODEL), BF16)] * 2, compiler_params=_cparams(("parallel",)),
    )(dx3, gpre, pp)


CONV_COLS = 256


def _shift_down(x, s):
    if s == 0:
        return x
    return jnp.where(_iota2(x.shape, 0) >= s, pltpu.roll(x, s, 0), 0.0)


def _shift_up(x, s):
    if s == 0:
        return x
    rows = x.shape[0]
    return jnp.where(_iota2(x.shape, 0) < rows - s, pltpu.roll(x, rows - s, 0), 0.0)


def _conv_pre(xf, w):
    return sum(_shift_down(xf, DN_CONV - 1 - j) * w[j:j + 1, :] for j in range(DN_CONV))


def _conv_fwd(zb, conv_w, bsz, seq):
    def body(x_ref, w_ref, y_ref):
        y_ref[...] = _silu(_conv_pre(x_ref[...], w_ref[...]))

    nblk = DN_QKV // CONV_COLS
    return pl.pallas_call(
        body, name="conv_fwd", grid=(bsz, nblk),
        in_specs=[pl.BlockSpec((seq, CONV_COLS), lambda b, j: (b, ZB_DQKV // CONV_COLS + j)),
                  pl.BlockSpec((DN_CONV, CONV_COLS), lambda b, j: (0, j))],
        out_specs=pl.BlockSpec((seq, CONV_COLS), lambda b, j: (b, j)),
        out_shape=jax.ShapeDtypeStruct((bsz * seq, DN_QKV), F32),
        compiler_params=_cparams(("parallel", "parallel")),
    )(zb, conv_w)


def _conv_bwd(zb, conv_w, dact, bsz, seq):
    def body(x_ref, w_ref, dy_ref, dx_ref, dw_ref):
        xf, w = x_ref[...], w_ref[...]
        c = _conv_pre(xf, w)
        sg = _sigmoid(c)
        dc = dy_ref[...].astype(F32) * sg * (1.0 + c * (1.0 - sg))
        dx = sum(_shift_up(dc, DN_CONV - 1 - j) * w[j:j + 1, :] for j in range(DN_CONV))
        dx_ref[...] = dx.astype(BF16)
        dw = jnp.concatenate(
            [jnp.sum(dc * _shift_down(xf, DN_CONV - 1 - j), axis=0, keepdims=True) for j in range(DN_CONV)], axis=0)

        @pl.when(pl.program_id(1) == 0)
        def _():
            dw_ref[...] = jnp.zeros_like(dw_ref)

        dw_ref[...] += dw

    nblk = DN_QKV // CONV_COLS
    return pl.pallas_call(
        body, name="conv_bwd", grid=(nblk, bsz),
        in_specs=[pl.BlockSpec((seq, CONV_COLS), lambda j, b: (b, ZB_DQKV // CONV_COLS + j)),
                  pl.BlockSpec((DN_CONV, CONV_COLS), lambda j, b: (0, j)),
                  pl.BlockSpec((seq, CONV_COLS), lambda j, b: (b, j))],
        out_specs=[pl.BlockSpec((seq, CONV_COLS), lambda j, b: (b, j)),
                   pl.BlockSpec((DN_CONV, CONV_COLS), lambda j, b: (0, j))],
        out_shape=[jax.ShapeDtypeStruct((bsz * seq, DN_QKV), BF16), jax.ShapeDtypeStruct((DN_CONV, DN_QKV), F32)],
        compiler_params=_cparams(("parallel", "arbitrary")),
    )(zb, conv_w, dact)


MESH_IDS = pl.DeviceIdType.MESH
ANY_SPEC = pl.BlockSpec(memory_space=pl.ANY)


COMM_SCRATCH = (pltpu.SemaphoreType.DMA((7,)), pltpu.SemaphoreType.DMA((7,)), pltpu.SemaphoreType.DMA)


def _gather_phases(x_ref, out_ref, send_sems, recv_sems, local_sem):
    mx, my, mc = lax.axis_index("x"), lax.axis_index("y"), lax.axis_index("c")
    me, sibling = (mx, my, mc), (mx, my, 1 - mc)
    chips = [(1 - mx, my), (mx, 1 - my), (1 - mx, 1 - my)]

    def slot(px, py, pc):
        return out_ref.at[4 * px + 2 * py + pc]

    def copy(k, block, to, src=None):
        return pltpu.make_async_remote_copy(
            src_ref=slot(*block) if src is None else src, dst_ref=slot(*block),
            send_sem=send_sems.at[k], recv_sem=recv_sems.at[k], device_id=to, device_id_type=MESH_IDS)

    def mine():
        return pltpu.make_async_copy(x_ref, slot(*me), local_sem)

    def first():
        return [copy(0, me, sibling, src=x_ref)] + [copy(1 + j, me, (*chip, mc), src=x_ref)
                                                    for j, chip in enumerate(chips)]

    def passed():
        return [copy(4 + j, (*chip, mc), sibling) for j, chip in enumerate(chips)]

    def start():
        mine().start()
        for cp in first():
            cp.start()

    def forward():
        for j, (chip, cp) in enumerate(zip(chips, passed())):
            copy(1 + j, (*chip, mc), me).wait_recv()
            cp.start()

    def finish():
        copy(0, sibling, me).wait_recv()
        for j, chip in enumerate(chips):
            copy(4 + j, (*chip, 1 - mc), me).wait_recv()
        for cp in first() + passed():
            cp.wait_send()
        mine().wait()

    return start, forward, finish


def _scatter_phases(x_ref, out_ref, send_sems, recv_sems, local_sem, among_chips=False):
    mx, my, mc = lax.axis_index("x"), lax.axis_index("y"), lax.axis_index("c")
    n_peers = 4 if among_chips else N_DEV
    me = 2 * mx + my if among_chips else 4 * mx + 2 * my + mc

    def peer(k):
        if among_chips:
            return (mx ^ ((k >> 1) & 1), my ^ (k & 1), mc)
        return (mx ^ ((k >> 2) & 1), my ^ ((k >> 1) & 1), mc ^ (k & 1))

    def slot_of(k):
        px, py, pc = peer(k)
        return 2 * px + py if among_chips else 4 * px + 2 * py + pc

    def copy(k, src_slot, dst_slot):
        return pltpu.make_async_remote_copy(
            src_ref=x_ref.at[src_slot], dst_ref=out_ref.at[dst_slot],
            send_sem=send_sems.at[k - 1], recv_sem=recv_sems.at[k - 1],
            device_id=peer(k), device_id_type=MESH_IDS)

    def sends():
        return [copy(k, slot_of(k), me) for k in range(1, n_peers)]

    def mine():
        return pltpu.make_async_copy(x_ref.at[me], out_ref.at[me], local_sem)

    def start():
        mine().start()
        for cp in sends():
            cp.start()

    def forward():
        pass

    def finish():
        for k in range(1, n_peers):
            copy(k, me, slot_of(k)).wait_recv()
        for cp in sends():
            cp.wait_send()
        mine().wait()

    return start, forward, finish


def _pair_phases(x_ref, out_ref, send_sems, recv_sems, local_sem):
    mx, my, mc = lax.axis_index("x"), lax.axis_index("y"), lax.axis_index("c")

    def copy(side):
        return pltpu.make_async_remote_copy(
            src_ref=x_ref.at[:, side], dst_ref=out_ref, send_sem=send_sems.at[0], recv_sem=recv_sems.at[0],
            device_id=(mx, my, 1 - mc), device_id_type=MESH_IDS)

    def start():
        copy(1 - mc).start()

    def forward():
        pass

    def finish():
        copy(mc).wait_recv()
        copy(1 - mc).wait_send()

    return start, forward, finish


class _Rider:
    def __init__(self, phases, x, out_shape):
        self.phases, self.x, self.out_shape = phases, x, out_shape


def _gather_rider(x):
    return _Rider(_gather_phases, x, jax.ShapeDtypeStruct((N_DEV,) + x.shape, x.dtype))


def _scatter_rider(x):
    return _Rider(_scatter_phases, x, jax.ShapeDtypeStruct(x.shape, x.dtype))


def _chip_scatter_rider(x):
    return _Rider(functools.partial(_scatter_phases, among_chips=True), x, jax.ShapeDtypeStruct(x.shape, x.dtype))


def _pair_rider(x):
    return _Rider(_pair_phases, x, jax.ShapeDtypeStruct((x.shape[0],) + x.shape[2:], x.dtype))


def _exchange(rider, name):
    def body(x_ref, out_ref, send_sems, recv_sems, local_sem):
        for phase in rider.phases(x_ref, out_ref, send_sems, recv_sems, local_sem):
            phase()

    return pl.pallas_call(body, name=name, out_shape=rider.out_shape, in_specs=[ANY_SPEC], out_specs=ANY_SPEC,
                          scratch_shapes=list(COMM_SCRATCH))(rider.x)


def _all_gather(x, name):
    return _exchange(_gather_rider(x), name)


def _all_to_all(x, name):
    return _exchange(_scatter_rider(x), name)


def _hosted_call(body, rider, *, name, grid, in_specs, out_specs, out_shape, scratch_shapes, args):
    if rider is None:
        return pl.pallas_call(body, name=name, grid=grid, in_specs=in_specs, out_specs=out_specs, out_shape=out_shape,
                              scratch_shapes=scratch_shapes, compiler_params=_cparams(("arbitrary",) * len(grid)))(*args)
    n_in, n_out, n_scr = len(in_specs), len(out_specs), len(scratch_shapes)
    total = math.prod(grid)

    def riding(*refs):
        host_in, x_ref = refs[:n_in], refs[n_in]
        host_out, out_ref = refs[n_in + 1:n_in + 1 + n_out], refs[n_in + 1 + n_out]
        host_scr = refs[n_in + 2 + n_out:n_in + 2 + n_out + n_scr]
        start, forward, finish = rider.phases(x_ref, out_ref, *refs[n_in + 2 + n_out + n_scr:])
        step = 0
        for axis, size in enumerate(grid):
            step = step * size + pl.program_id(axis)
        pl.when(step == 0)(start)
        pl.when(step == (3 * total) // 4)(forward)
        body(*host_in, *host_out, *host_scr)
        pl.when(step == total - 1)(finish)

    return pl.pallas_call(
        riding, name=name, grid=grid, in_specs=list(in_specs) + [ANY_SPEC], out_specs=list(out_specs) + [ANY_SPEC],
        out_shape=list(out_shape) + [rider.out_shape], scratch_shapes=list(scratch_shapes) + list(COMM_SCRATCH),
        compiler_params=_cparams(("arbitrary",) * len(grid)))(*args, rider.x)


def _adamw_math(w, g, m, v):
    m = ADAM_B1 * m + (1.0 - ADAM_B1) * g
    v = ADAM_B2 * v + (1.0 - ADAM_B2) * jnp.square(g)
    m_hat = m / (1.0 - ADAM_B1 ** ADAM_STEP)
    v_hat = v / (1.0 - ADAM_B2 ** ADAM_STEP)
    delta = -ADAM_LR * (m_hat / (jnp.sqrt(v_hat) + ADAM_EPS) + ADAM_WD * w)
    return delta, m, v


ADAM_ROWS = 128


def _elementwise_tile(rows, cols):
    if rows % ADAM_ROWS == 0:
        return ADAM_ROWS, cols
    return rows, (2 * LANE if cols % (2 * LANE) == 0 else cols)


def _add_blocks(a, b):
    g, rows, cols = a.shape
    tr, tc = _elementwise_tile(rows, cols)

    def body(a_ref, b_ref, o_ref):
        o_ref[...] = (a_ref[...].astype(F32) + b_ref[...].astype(F32)).astype(o_ref.dtype)

    blk = pl.BlockSpec((None, tr, tc), lambda k, i, j: (k, i, j))
    return pl.pallas_call(body, name="add_pair_blocks", grid=(g, rows // tr, cols // tc), in_specs=[blk, blk],
                          out_specs=blk, out_shape=jax.ShapeDtypeStruct(a.shape, a.dtype),
                          compiler_params=_cparams(("parallel", "parallel", "parallel")))(a, b)


def _adamw_reduce(w, m, v, parts, row0, name):
    rows, cols = w.shape
    n_parts = parts.shape[0]
    tr, tc = _elementwise_tile(rows, cols)
    r0 = row0 // tr

    def body(w_ref, m_ref, v_ref, *refs):
        part_refs, (g_ref, d_ref, nm_ref, nv_ref) = refs[:n_parts], refs[n_parts:]
        g = part_refs[0][...].astype(F32)
        for r in part_refs[1:]:
            g = g + r[...].astype(F32)
        delta, nm, nv = _adamw_math(w_ref[...], g, m_ref[...], v_ref[...])
        g_ref[...] = g
        d_ref[...] = delta
        nm_ref[...] = nm
        nv_ref[...] = nv

    blk = pl.BlockSpec((tr, tc), lambda i, j: (i, j))
    part_specs = [pl.BlockSpec((None, tr, tc), functools.partial(lambda i, j, k: (k, r0 + i, j), k=k))
                  for k in range(n_parts)]
    return pl.pallas_call(
        body, name=name, grid=(rows // tr, cols // tc), in_specs=[blk] * 3 + part_specs, out_specs=[blk] * 4,
        out_shape=[jax.ShapeDtypeStruct(w.shape, F32)] * 4, compiler_params=_cparams(("parallel", "parallel")),
    )(w, m, v, *([parts] * n_parts))


def _small_reduce(gathered, lane_sum_from):
    r = gathered.shape[1]

    def body(g_ref, o_ref):
        g = g_ref[0]
        for k in range(1, N_DEV):
            g = g + g_ref[k]
        tot = jnp.broadcast_to(jnp.sum(g, axis=-1, keepdims=True), g.shape)
        o_ref[...] = jnp.where(_iota2(g.shape, 0) >= lane_sum_from, tot, g)

    return pl.pallas_call(body, name="small_grad_reduce", out_shape=jax.ShapeDtypeStruct((r, LANE), F32))(gathered)


def _adamw_small(w, m, v, g):
    def body(w_ref, m_ref, v_ref, g_ref, d_ref, nm_ref, nv_ref):
        d_ref[...], nm_ref[...], nv_ref[...] = _adamw_math(w_ref[...], g_ref[...], m_ref[...], v_ref[...])

    return pl.pallas_call(body, name="adamw_small", out_shape=[jax.ShapeDtypeStruct(w.shape, F32)] * 3)(w, m, v, g)


def _pack_rows(arrays):
    rows = [jnp.pad(a.reshape(-1), (0, -a.size % LANE)).reshape(-1, LANE) for a in arrays]
    out = jnp.concatenate(rows, axis=0)
    return jnp.pad(out, ((0, -out.shape[0] % 8), (0, 0)))


def _unpack_rows(packed, shapes):
    out, r = [], 0
    for shp in shapes:
        size = math.prod(shp)
        nrows = -(-size // LANE)
        out.append(packed[r:r + nrows].reshape(-1)[:size].reshape(shp))
        r += nrows
    return out


def _add_residual(acc, res):
    return (res + acc,)


def _by_cols(g):
    return g.reshape(g.shape[0], N_DEV, -1).transpose(1, 0, 2)


def _from_cols(blocks):
    return blocks.transpose(1, 0, 2).reshape(blocks.shape[1], -1)


ROWS_OUT = D_MODEL // N_DEV

W_IN_SEGMENTS = ((0, WI_LR, "big", 0), (WI_LR, WI_DQKV, "gates", ZS_LR), (WI_DQKV, WI_DA, "big", ZB_DQKV),
                 (WI_DA, WI_DB, "gates", ZS_DA), (WI_DB, WI_GA, "gates", ZS_DB), (WI_GA, D_IN, "big", ZB_GA))


def _shard_pieces(lo, hi):
    out = []
    for j in range(N_DEV):
        a, b = max(lo, j * D_IN_SHARD), min(hi, (j + 1) * D_IN_SHARD)
        if a < b:
            out.append((j, a - j * D_IN_SHARD, b - a))
    return out


def _w_in_rows(blocks):
    parts = {"big": [], "gates": []}
    for lo, hi, which, _ in W_IN_SEGMENTS:
        parts[which] += [blocks[j, r0:r0 + n] for j, r0, n in _shard_pieces(lo, hi)]
        if which == "gates":
            parts[which].append(jnp.zeros((LANE - (hi - lo), blocks.shape[2]), blocks.dtype))
    return jnp.concatenate(parts["big"], axis=0), jnp.concatenate(parts["gates"], axis=0)


def _w_in_grad_blocks(dwb_t, dws_t):
    src = {"big": dwb_t, "gates": dws_t}
    blocks = [[] for _ in range(N_DEV)]
    for lo, hi, which, first in W_IN_SEGMENTS:
        for j, r0, n in _shard_pieces(lo, hi):
            at = first + j * D_IN_SHARD + r0 - lo
            blocks[j].append(src[which][at:at + n])
    return jnp.concatenate([b[None] for b in map(functools.partial(jnp.concatenate, axis=0), blocks)], axis=0)


def _local_step(x, p, target, w, up_shard, rows_shard):
    bsz, seq, _ = x.shape
    t, nc = bsz * seq, seq // CHUNK
    x0, p2, tgt = x.reshape(t, D_MODEL), p.reshape(t, PLE_DIM), target.reshape(t, D_MODEL)

    h = _rms_fwd(x0, w["g_mix"], "rms_mix")
    zb, up_blocks = _mm(h, w["wb_t"], tb=True, name="in_proj", rider=_gather_rider(up_shard))
    w_up = _from_cols(up_blocks)
    zs = _mm(h, w["ws_t"], tb=True, name="in_proj_gates")
    act = _conv_fwd(zb, w["conv"], bsz, seq)
    mix_gla, gla_states = _gla_fwd(zb, zs, w["w2p"], w["gla_b"], w["gla_norm"], bsz, nc)
    mixed, dn_states, dn_t_invs, row_blocks = _dn_fwd(act, zb, zs, w["alog_b"], w["dtb_b"], w["dn_norm"], mix_gla,
                                                      bsz, nc, rider=_gather_rider(rows_shard))
    w_out = row_blocks[:, :ROWS_OUT].reshape(D_MODEL, D_MODEL)
    w_pg = row_blocks[:, ROWS_OUT:2 * ROWS_OUT].reshape(D_MODEL, D_MODEL)
    w_down = row_blocks[:, 2 * ROWS_OUT:].reshape(D_FF, D_MODEL)
    x1 = _mm(mixed, w_out, epilogue=_add_residual, extras=(x0,), name="out_proj")
    h2 = _rms_fwd(x1, w["g_mlp"], "rms_mlp")
    u, a = _mm(h2, w_up, out_dtypes=(BF16, BF16), name="mlp_up",
               epilogue=lambda acc: (acc, jnp.square(jnp.maximum(acc, 0.0))))
    x2 = _mm(a, w_down, epilogue=_add_residual, extras=(x1,), name="mlp_down")
    h3 = _rms_fwd(x2, w["g_ple"], "rms_ple")
    pp = _mm(p2, w["w_pp"], name="ple_proj")
    gpre, x3 = _mm(h3, w_pg, out_dtypes=(F32, F32), extras=(x2, pp), name="ple_gate",
                   epilogue=lambda acc, res, proj: (acc, res + _sigmoid(acc) * proj))
    dx3, dg_final, loss = _loss_head(x3, w["g_final"], tgt)

    dgpre, dpp = _ple_bwd(dx3, gpre, pp)
    dw_pp = _mm(p2, dpp, ta=True, out_dtypes=(BF16,), name="d_w_ple_proj")
    dw_pg = _mm(h3, dgpre, ta=True, out_dtypes=(BF16,), name="d_w_ple_gate")
    dh3 = _mm(dgpre, w_pg, tb=True, name="d_h_ple")
    dx2, dg_ple = _rms_bwd(x2, w["g_ple"], dh3, dx3, "rms_ple_bwd")
    du = _mm(dx2, w_down, tb=True, out_dtypes=(BF16,), extras=(u,), name="d_mlp_hidden",
             epilogue=lambda acc, uu: (acc * (2.0 * jnp.maximum(uu.astype(F32), 0.0)),))
    dw_down = _mm(a, dx2, ta=True, out_dtypes=(BF16,), name="d_w_down")
    dw_up = _mm(h2, du, ta=True, out_dtypes=(BF16,), name="d_w_up")
    dh2 = _mm(du, w_up, tb=True, name="d_h_mlp")
    dx1, dg_mlp = _rms_bwd(x1, w["g_mlp"], dh2, dx2, "rms_mlp_bwd")
    dmixed = _mm(dx1, w_out, tb=True, out_dtypes=(BF16,), name="d_mixed")
    dw_out = _mm(mixed, dx1, ta=True, out_dtypes=(BF16,), name="d_w_out")

    d_rows = jnp.concatenate([dw_out.reshape(N_DEV, ROWS_OUT, D_MODEL), dw_pg.reshape(N_DEV, ROWS_OUT, D_MODEL),
                              dw_down.reshape(N_DEV, D_FF // N_DEV, D_MODEL)], axis=1)
    (dact, dza, dzb_, dgz, dgb, dal, ddt, ddn, recv_rows) = _dn_bwd(
        act, zb, zs, w["alog_b"], w["dtb_b"], w["dn_norm"], dn_states, dn_t_invs, dmixed, bsz, nc,
        rider=_scatter_rider(d_rows))
    (gdq, gdk, gdv, dgg, dga, dlr, dw2, dgla_b, dgla_norm, recv_up) = _gla_bwd(
        zb, zs, w["w2p"], w["gla_b"], w["gla_norm"], gla_states, dmixed, bsz, nc,
        rider=_scatter_rider(_by_cols(dw_up)))
    dqkv, dconv = _conv_bwd(zb, w["conv"], dact, bsz, seq)
    dzb = jnp.concatenate([gdq, gdk, gdv, dgg, dqkv, dgz, dga, dgb], axis=1)
    dzs = jnp.concatenate([dlr, dza, dzb_], axis=1)
    dwb_t = _mm(dzb, h, ta=True, out_dtypes=(BF16,), name="d_w_in")
    dws_t = _mm(dzs, h, ta=True, out_dtypes=(BF16,), name="d_w_in_gates")
    by_chip = _w_in_grad_blocks(dwb_t, dws_t).reshape(N_DEV // 2, 2, D_IN_SHARD, D_MODEL)
    from_sibling = _exchange(_pair_rider(by_chip), "pair_d_w_in")
    mine = lax.dynamic_index_in_dim(by_chip, lax.axis_index("c"), axis=1, keepdims=False)
    chip_sums = _add_blocks(mine, from_sibling)
    dh_gates = _mm(dzs, w["ws_t"], name="d_h_mix_gates")
    dh, recv_in = _mm(dzb, w["wb_t"], epilogue=_add_residual, extras=(dh_gates,), name="d_h_mix",
                      rider=_chip_scatter_rider(chip_sums))
    gx, dg_mix = _rms_bwd(x0, w["g_mix"], dh, dx1, "rms_mix_bwd")

    dgla_w2 = dw2[:, :GLA_LOWRANK, :].transpose(1, 0, 2).reshape(GLA_LOWRANK, GLA_QK)
    return dict(
        loss=loss[0, 0], grad_x=gx.reshape(x.shape), recv_in=recv_in, recv_up=recv_up, recv_rows=recv_rows,
        w_ple_proj=dw_pp,
        g_mix=dg_mix, gla_b=dgla_b.reshape(1, GLA_QK), gla_norm=dgla_norm, dn_norm=ddn, g_mlp=dg_mlp, g_ple=dg_ple,
        g_final=dg_final, gla_w2=dgla_w2, dn_conv=dconv,
        a_log_lanes=dal.reshape(DN_HEADS, LANE), dt_bias_lanes=ddt.reshape(DN_HEADS, LANE))


def _first_weights(g_mix, w_in, gla_w2, gla_b, gla_norm, dn_conv, dn_a_log, dn_dt_bias, dn_norm, g_mlp, g_ple,
                   w_ple_proj, g_final):
    wb_t, ws_t = _w_in_rows(_all_gather(jnp.swapaxes(w_in[0], 0, 1).astype(BF16), "gather_w_in"))
    w_pp = _all_gather(w_ple_proj[0].astype(BF16), "gather_w_ple_proj")
    small = _all_gather(_pack_rows([gla_w2[0], dn_conv[0]]), "gather_w_small")
    n_w2 = GLA_LOWRANK * GLA_QK // N_DEV // LANE
    n_cv = DN_CONV * DN_QKV // N_DEV // LANE
    w2 = small[:, :n_w2].reshape(N_DEV, GLA_LOWRANK, GLA_QK // N_DEV).transpose(1, 0, 2).reshape(GLA_LOWRANK, GLA_QK)
    conv = small[:, n_w2:n_w2 + n_cv].reshape(N_DEV, DN_CONV, DN_QKV // N_DEV).transpose(1, 0, 2).reshape(DN_CONV, DN_QKV)

    return dict(
        wb_t=wb_t, ws_t=ws_t, w_pp=_from_cols(w_pp),
        w2p=jnp.pad(w2, ((0, LANE - GLA_LOWRANK), (0, 0))), conv=conv,
        alog_b=jnp.broadcast_to(dn_a_log[0][:, None, None], (DN_HEADS, 1, LANE)),
        dtb_b=jnp.broadcast_to(dn_dt_bias[0][:, None, None], (DN_HEADS, 1, LANE)),
        g_mix=g_mix, gla_b=gla_b, gla_norm=gla_norm, dn_norm=dn_norm, g_mlp=g_mlp, g_ple=g_ple,
        g_final=g_final.reshape(1, D_MODEL))


def kernel(x, p, g_mix, w_in, gla_w2, gla_b, gla_norm, dn_conv, dn_a_log, dn_dt_bias, dn_norm, w_out, g_mlp, w_up, w_down, g_ple, w_ple_gate, w_ple_proj, g_final, loss_target, m_g_mix, m_w_in, m_gla_w2, m_gla_b, m_gla_norm, m_dn_conv, m_dn_a_log, m_dn_dt_bias, m_dn_norm, m_w_out, m_g_mlp, m_w_up, m_w_down, m_g_ple, m_w_ple_gate, m_w_ple_proj, m_g_final, v_g_mix, v_w_in, v_gla_w2, v_gla_b, v_gla_norm, v_dn_conv, v_dn_a_log, v_dn_dt_bias, v_dn_norm, v_w_out, v_g_mlp, v_w_up, v_w_down, v_g_ple, v_w_ple_gate, v_w_ple_proj, v_g_final):
    names = ["g_mix", "w_in", "gla_w2", "gla_b", "gla_norm", "dn_conv", "dn_a_log", "dn_dt_bias", "dn_norm", "w_out",
             "g_mlp", "w_up", "w_down", "g_ple", "w_ple_gate", "w_ple_proj", "g_final"]
    ws = dict(zip(names, (g_mix, w_in, gla_w2, gla_b, gla_norm, dn_conv, dn_a_log, dn_dt_bias, dn_norm, w_out, g_mlp,
                          w_up, w_down, g_ple, w_ple_gate, w_ple_proj, g_final)))
    ms = dict(zip(names, (m_g_mix, m_w_in, m_gla_w2, m_gla_b, m_gla_norm, m_dn_conv, m_dn_a_log, m_dn_dt_bias,
                          m_dn_norm, m_w_out, m_g_mlp, m_w_up, m_w_down, m_g_ple, m_w_ple_gate, m_w_ple_proj,
                          m_g_final)))
    vs = dict(zip(names, (v_g_mix, v_w_in, v_gla_w2, v_gla_b, v_gla_norm, v_dn_conv, v_dn_a_log, v_dn_dt_bias,
                          v_dn_norm, v_w_out, v_g_mlp, v_w_up, v_w_down, v_g_ple, v_w_ple_gate, v_w_ple_proj,
                          v_g_final)))
    me = 4 * lax.axis_index("x") + 2 * lax.axis_index("y") + lax.axis_index("c")

    first = _first_weights(g_mix, w_in, gla_w2, gla_b, gla_norm, dn_conv, dn_a_log, dn_dt_bias, dn_norm, g_mlp,
                           g_ple, w_ple_proj, g_final)
    rows_shard = jnp.concatenate([w_out[0], w_ple_gate[0], w_down[0]], axis=0).astype(BF16)
    r = _local_step(x, p[0], loss_target, first, w_up[0].astype(BF16), rows_shard)
    loss = lax.psum(r["loss"], ("x", "y", "c"))

    grads, deltas, new_m, new_v = {}, {}, {}, {}

    def big(name, parts, row0=0):
        g, d, nm, nv = _adamw_reduce(ws[name][0], ms[name][0], vs[name][0], parts, row0, "adamw_" + name)
        grads[name], deltas[name], new_m[name], new_v[name] = g[None], d[None], nm[None], nv[None]

    t_outs = _adamw_reduce(*[jnp.swapaxes(d["w_in"][0], 0, 1) for d in (ws, ms, vs)], r["recv_in"], 0, "adamw_w_in")
    grads["w_in"], deltas["w_in"], new_m["w_in"], new_v["w_in"] = [jnp.swapaxes(o, 0, 1)[None] for o in t_outs]
    big("w_up", r["recv_up"])
    big("w_ple_proj", _all_to_all(_by_cols(r["w_ple_proj"]), "scatter_d_w_ple_proj"))
    big("w_out", r["recv_rows"], 0)
    big("w_ple_gate", r["recv_rows"], ROWS_OUT)
    big("w_down", r["recv_rows"], 2 * ROWS_OUT)

    vec_names = ["g_mix", "gla_b", "gla_norm", "dn_norm", "g_mlp", "g_ple", "g_final"]
    packed = _pack_rows([r[n] for n in vec_names] + [r["gla_w2"], r["dn_conv"]])
    lane_rows = packed.shape[0]
    packed = jnp.concatenate([packed, r["a_log_lanes"], r["dt_bias_lanes"]], axis=0)
    total = _small_reduce(_all_gather(packed, "gather_small_grads"), lane_rows)
    parts = _unpack_rows(total, [r[n].shape for n in vec_names] + [r["gla_w2"].shape, r["dn_conv"].shape])
    sg = dict(zip(vec_names, parts[:len(vec_names)]))
    sg["g_final"] = sg["g_final"].reshape(D_MODEL)
    sg["gla_w2"] = lax.dynamic_slice_in_dim(parts[-2], me * (GLA_QK // N_DEV), GLA_QK // N_DEV, axis=1)
    sg["dn_conv"] = lax.dynamic_slice_in_dim(parts[-1], me * (DN_QKV // N_DEV), DN_QKV // N_DEV, axis=1)
    sg["dn_a_log"] = total[lane_rows:lane_rows + DN_HEADS, 0]
    sg["dn_dt_bias"] = total[lane_rows + DN_HEADS:lane_rows + 2 * DN_HEADS, 0]
    small_names = vec_names + ["gla_w2", "dn_conv", "dn_a_log", "dn_dt_bias"]
    shapes = [ws[n].shape for n in small_names]
    d_s, m_s, v_s = _adamw_small(_pack_rows([ws[n] for n in small_names]), _pack_rows([ms[n] for n in small_names]),
                                 _pack_rows([vs[n] for n in small_names]), _pack_rows([sg[n] for n in small_names]))
    for n, d, nm, nv in zip(small_names, _unpack_rows(d_s, shapes), _unpack_rows(m_s, shapes), _unpack_rows(v_s, shapes)):
        grads[n], deltas[n], new_m[n], new_v[n] = sg[n].reshape(ws[n].shape), d, nm, nv

    return (loss, r["grad_x"], *[grads[n] for n in names], *[deltas[n] for n in names],
            *[new_m[n] for n in names], *[new_v[n] for n in names])
```

```python
import functools
import math

import jax
import jax.numpy as jnp
from jax import lax
from jax.experimental import pallas as pl
from jax.experimental.pallas import tpu as pltpu

F32 = jnp.float32
BF16 = jnp.bfloat16
HIGHEST = lax.Precision.HIGHEST

N_DEV = 8
D_MODEL = 2048
CHUNK = 64
PLE_DIM = 256
EPS = 1e-6
GLA_HEADS = 4
GLA_DK = 256
GLA_DV = 512
GLA_LOWRANK = 16
GLA_TAU = 16.0
DN_HEADS = 16
DN_D = 128
DN_CONV = 4
D_FF = 4 * D_MODEL
GLA_QK = GLA_HEADS * GLA_DK
GLA_V = GLA_HEADS * GLA_DV
DN_QKV = 3 * DN_HEADS * DN_D
D_IN = 2 * GLA_QK + 2 * GLA_V + GLA_LOWRANK + DN_QKV + D_MODEL + 2 * DN_HEADS + 2 * D_MODEL
D_IN_SHARD = D_IN // N_DEV

ADAM_LR = 0.001
ADAM_B1 = 0.9
ADAM_B2 = 0.999
ADAM_EPS = 1e-08
ADAM_WD = 0.01
ADAM_STEP = 10

LANE = 128
ZB_GQ, ZB_GK, ZB_GV, ZB_GG = 0, 1024, 2048, 4096
ZB_DQKV, ZB_DZ, ZB_GA, ZB_GB = 6144, 12288, 14336, 16384
ZB_W = 18432
ZS_LR, ZS_DA, ZS_DB = 0, 128, 256
ZS_W = 384
WI_LR = 2 * GLA_QK + 2 * GLA_V
WI_DQKV = WI_LR + GLA_LOWRANK
WI_DA = WI_DQKV + DN_QKV + D_MODEL
WI_DB = WI_DA + DN_HEADS
WI_GA = WI_DB + DN_HEADS

VMEM_LIMIT = 56 * 1024 * 1024

NN = (((1,), (0,)), ((), ()))
NT = (((1,), (1,)), ((), ()))
TN = (((0,), (0,)), ((), ()))


def _bdot(a, b, dims):
    return lax.dot_general(a.astype(BF16), b.astype(BF16), dims, preferred_element_type=F32)


def _split3(x):
    hi = x.astype(BF16)
    rest = x - hi.astype(F32)
    mid = rest.astype(BF16)
    return hi, mid, (rest - mid.astype(F32)).astype(BF16)


def _dot01(x, m, dims, x_first):
    m = m.astype(BF16)
    out = None
    for piece in _split3(x):
        d = lax.dot_general(piece, m, dims, preferred_element_type=F32) if x_first else \
            lax.dot_general(m, piece, dims, preferred_element_type=F32)
        out = d if out is None else out + d
    return out


@functools.partial(jax.custom_vjp, nondiff_argnums=(2, 3))
def _pick_dot(x, m, dims, dims_t):
    return _dot01(x, m, dims, True)


def _pick_dot_fwd(x, m, dims, dims_t):
    return _dot01(x, m, dims, True), m


def _pick_dot_bwd(dims, dims_t, m, ct):
    return _dot01(ct, m, dims_t, True), jnp.zeros_like(m)


_pick_dot.defvjp(_pick_dot_fwd, _pick_dot_bwd)


@functools.partial(jax.custom_vjp, nondiff_argnums=(2, 3))
def _left_dot(m, x, dims, dims_t):
    return _dot01(x, m, dims, False)


def _left_dot_fwd(m, x, dims, dims_t):
    return _dot01(x, m, dims, False), m


def _left_dot_bwd(dims, dims_t, m, ct):
    return jnp.zeros_like(m), _dot01(ct, m, dims_t, False)


_left_dot.defvjp(_left_dot_fwd, _left_dot_bwd)


def _dot3(a, b, dims):
    ah, bh = a.astype(BF16), b.astype(BF16)
    al, bl = (a - ah.astype(F32)).astype(BF16), (b - bh.astype(F32)).astype(BF16)
    dot = functools.partial(lax.dot_general, dimension_numbers=dims, preferred_element_type=F32)
    return dot(ah, bh) + (dot(ah, bl) + dot(al, bh))


def _sigmoid(x):
    return 1.0 / (1.0 + jnp.exp(-x))


def _silu(x):
    return x * _sigmoid(x)


def _softplus(x):
    return jnp.maximum(x, 0.0) + jnp.log(1.0 + jnp.exp(-jnp.abs(x)))


def _iota2(shape, dim):
    return lax.broadcasted_iota(jnp.int32, shape, dim)


def _cparams(sem=None):
    return pltpu.CompilerParams(dimension_semantics=sem, vmem_limit_bytes=VMEM_LIMIT)


BNN = (((2,), (1,)), ((0,), (0,)))
BNT = (((2,), (2,)), ((0,), (0,)))
BTN = (((1,), (1,)), ((0,), (0,)))


def _gla_chunk(st, q, k, v, lr, w2, b, gg, ga, gn):
    hb, c, _ = q.shape
    incl = (_iota2((c, c), 0) >= _iota2((c, c), 1))[None]
    tri = jnp.broadcast_to(incl.astype(F32), (hb, c, c))
    q = q.astype(F32) * (GLA_DK ** -0.5)
    k = k.astype(F32)
    v = v.astype(F32)
    lr_b = jnp.broadcast_to(lr[None], (hb,) + lr.shape)
    lf = -_softplus(-(_bdot(lr_b, w2, BNN) + b)) / GLA_TAU
    bcum = _left_dot(tri, lf, BNN, BTN)
    blast = jnp.sum(lf, axis=1, keepdims=True)
    q_in = q * jnp.exp(bcum)
    k_in = k * jnp.exp(-bcum)
    a = jnp.where(incl, _bdot(q_in, k_in, BNT), 0.0)
    o = _bdot(a, v, BNN) + _bdot(q_in, st, BNT)
    k_dec = k * jnp.exp(blast - bcum)
    st_new = st * jnp.exp(blast) + _bdot(v, k_dec, BTN)
    on = o * lax.rsqrt(jnp.mean(o * o, axis=-1, keepdims=True) + EPS) * gn
    res = _sigmoid(ga.astype(F32)) * on * _silu(gg.astype(F32))
    return res, st_new


def _tri_inv_raw(a):
    _, c, _ = a.shape
    eye = (_iota2((c, c), 0) == _iota2((c, c), 1)).astype(F32)[None]
    x = a
    p = eye - a
    for _ in range(5):
        x = _dot3(x, x, BNN)
        p = _dot3(p, eye + x, BNN)
    return p


def _tri_inv_bwd(t, dt):
    return (-_dot3(_dot3(t, dt, BTN), t, BNT),)


@jax.custom_vjp
def _tri_inv_given(a, t):
    return t


def _tri_inv_given_fwd(a, t):
    return t, t


def _tri_inv_given_bwd(t, dt):
    return _tri_inv_bwd(t, dt) + (jnp.zeros_like(t),)


_tri_inv_given.defvjp(_tri_inv_given_fwd, _tri_inv_given_bwd)


@functools.partial(jax.custom_vjp, nondiff_argnums=(1,))
def _column_on_lanes(z, j):
    picked = jnp.where(_iota2(z.shape, 1) == j, z, 0.0)
    return jnp.broadcast_to(jnp.sum(picked, axis=1, keepdims=True), z.shape)


def _column_on_lanes_fwd(z, j):
    return _column_on_lanes(z, j), None


def _column_on_lanes_bwd(j, _, ct):
    total = jnp.broadcast_to(jnp.sum(ct, axis=1, keepdims=True), ct.shape)
    return (jnp.where(_iota2(ct.shape, 1) == j, total, 0.0),)


_column_on_lanes.defvjp(_column_on_lanes_fwd, _column_on_lanes_bwd)


def _dn_chunk(s, qr, kr, vr, za, zb, alog, dtb, gz, gb, dn, t_saved=None):
    hb, c, _ = qr.shape
    row, col = _iota2((c, c), 0), _iota2((c, c), 1)
    incl = (row >= col)[None]
    strict = (row > col)[None]
    tri = jnp.broadcast_to(incl.astype(F32), (hb, c, c))
    eye = (row == col).astype(F32)[None]
    ones_cc = jnp.ones((hb, c, c), F32)
    lane0 = (lax.broadcasted_iota(jnp.int32, (hb, LANE, c), 1) == 0).astype(F32)

    def l2n(t):
        return t * lax.rsqrt(jnp.sum(t * t, axis=-1, keepdims=True) + EPS)

    q = l2n(qr.astype(F32)) * (DN_D ** -0.5)
    k = l2n(kr.astype(F32))
    v = vr.astype(F32)
    za_b = jnp.concatenate([_column_on_lanes(za, j)[None] for j in range(hb)], axis=0)
    zb_b = jnp.concatenate([_column_on_lanes(zb, j)[None] for j in range(hb)], axis=0)
    g = -jnp.exp(alog) * _softplus(za_b + dtb)
    beta = _sigmoid(zb_b)
    gcum = _left_dot(tri, g, BNN, BTN)
    glast = jnp.sum(g, axis=1, keepdims=True)
    cm = _pick_dot(gcum, lane0, BNN, BNT)
    rm = _left_dot(ones_cc, cm * eye, BNN, BTN)
    dec = jnp.exp(jnp.where(incl, cm - rm, -1e30))
    kb = k * beta
    a = jnp.where(strict, _bdot(kb, k, BNT) * dec, 0.0)
    t = _tri_inv_raw(a) if t_saved is None else _tri_inv_given(a, t_saved)
    egc = jnp.exp(gcum)
    u = _bdot(t, v * beta, BNN)
    w = _bdot(t, kb * egc, BNN)
    attn = jnp.where(incl, _bdot(q, k, BNT) * dec, 0.0)
    q_dec = q * egc
    k_dec = k * jnp.exp(glast - gcum)
    v_new = u - _bdot(w, s, BNN)
    o = _bdot(q_dec, s, BNN) + _bdot(attn, v_new, BNN)
    s_new = s * jnp.exp(glast) + _bdot(k_dec, v_new, BTN)
    on = o * lax.rsqrt(jnp.mean(o * o, axis=-1, keepdims=True) + EPS) * dn
    res = _sigmoid(gb.astype(F32)) * on * _silu(gz.astype(F32))
    return (res, s_new, t) if t_saved is None else (res, s_new)


def _heads(ref, n_heads, width):
    return jnp.stack([ref[:, j * width:(j + 1) * width] for j in range(n_heads)], axis=0)


def _gla_specs(nc, reverse):
    def rows(b, n):
        return b * nc + ((nc - 1 - n) if reverse else n)

    qk = lambda base: pl.BlockSpec((CHUNK, GLA_QK), lambda b, n: (rows(b, n), base // GLA_QK))
    vv = lambda base: pl.BlockSpec((CHUNK, GLA_V), lambda b, n: (rows(b, n), base // GLA_V))
    lr = lambda c: pl.BlockSpec((CHUNK, LANE), lambda b, n: (rows(b, n), c))
    full = lambda shape: pl.BlockSpec(shape, lambda b, n: (0,) * len(shape))
    return rows, qk, vv, lr, full


def _gla_inputs(q_ref, k_ref, v_ref, gg_ref, ga_ref, lr_ref, w2_ref, b_ref, gn_ref):
    return (_heads(q_ref, GLA_HEADS, GLA_DK), _heads(k_ref, GLA_HEADS, GLA_DK), _heads(v_ref, GLA_HEADS, GLA_DV),
            lr_ref[...], _heads(w2_ref, GLA_HEADS, GLA_DK), _heads(b_ref, GLA_HEADS, GLA_DK),
            _heads(gg_ref, GLA_HEADS, GLA_DV), _heads(ga_ref, GLA_HEADS, GLA_DV), gn_ref[...])


def _gla_fwd(zb, zs, w2p, gla_b, gla_norm, bsz, nc):
    t = zb.shape[0]
    rows, qk, vv, lr, full = _gla_specs(nc, False)

    def body(q_ref, k_ref, v_ref, gg_ref, ga_ref, lr_ref, w2_ref, b_ref, gn_ref, o_ref, st_ref, state):
        st = jnp.where(pl.program_id(1) > 0, state[...], 0.0)
        st_ref[0, 0] = st
        res, st_new = _gla_chunk(st, *_gla_inputs(q_ref, k_ref, v_ref, gg_ref, ga_ref, lr_ref, w2_ref, b_ref, gn_ref))
        for j in range(GLA_HEADS):
            o_ref[:, j * GLA_DV:(j + 1) * GLA_DV] = res[j]
        state[...] = st_new

    return pl.pallas_call(
        body, name="gla_fwd", grid=(bsz, nc),
        in_specs=[qk(ZB_GQ), qk(ZB_GK), vv(ZB_GV), vv(ZB_GG), vv(ZB_GA), lr(ZS_LR // LANE),
                  full((LANE, GLA_QK)), full((1, GLA_QK)), full((1, GLA_DV))],
        out_specs=[vv(0), pl.BlockSpec((1, 1, GLA_HEADS, GLA_DV, GLA_DK), lambda b, n: (b, n, 0, 0, 0))],
        out_shape=[jax.ShapeDtypeStruct((t, GLA_V), F32),
                   jax.ShapeDtypeStruct((bsz, nc, GLA_HEADS, GLA_DV, GLA_DK), F32)],
        scratch_shapes=[pltpu.VMEM((GLA_HEADS, GLA_DV, GLA_DK), F32)],
        compiler_params=_cparams(("arbitrary", "arbitrary")),
    )(zb, zb, zb, zb, zb, zs, w2p, gla_b, gla_norm)


def _gla_bwd(zb, zs, w2p, gla_b, gla_norm, states, dmixed, bsz, nc, rider=None):
    t = zb.shape[0]
    rows, qk, vv, lr, full = _gla_specs(nc, True)

    def body(q_ref, k_ref, v_ref, gg_ref, ga_ref, lr_ref, w2_ref, b_ref, gn_ref, st_ref, dm_ref,
             dq_ref, dk_ref, dv_ref, dgg_ref, dga_ref, dlr_ref, dw2_ref, db_ref, dgn_ref, dstate):
        b, n = pl.program_id(0), pl.program_id(1)

        @pl.when((b == 0) & (n == 0))
        def _():
            dw2_ref[...] = jnp.zeros_like(dw2_ref)
            db_ref[...] = jnp.zeros_like(db_ref)
            dgn_ref[...] = jnp.zeros_like(dgn_ref)

        _, vjp = jax.vjp(_gla_chunk, st_ref[0, 0],
                         *_gla_inputs(q_ref, k_ref, v_ref, gg_ref, ga_ref, lr_ref, w2_ref, b_ref, gn_ref))
        dst_in = jnp.where(n > 0, dstate[...], 0.0)
        dst, dq, dk, dv, dlr, dw2, db, dgg, dga, dgn = vjp((_heads(dm_ref, GLA_HEADS, GLA_DV).astype(F32), dst_in))
        dstate[...] = dst
        for j in range(GLA_HEADS):
            dq_ref[:, j * GLA_DK:(j + 1) * GLA_DK] = dq[j].astype(dq_ref.dtype)
            dk_ref[:, j * GLA_DK:(j + 1) * GLA_DK] = dk[j].astype(dk_ref.dtype)
            dv_ref[:, j * GLA_DV:(j + 1) * GLA_DV] = dv[j].astype(dv_ref.dtype)
            dgg_ref[:, j * GLA_DV:(j + 1) * GLA_DV] = dgg[j].astype(dgg_ref.dtype)
            dga_ref[:, j * GLA_DV:(j + 1) * GLA_DV] = dga[j].astype(dga_ref.dtype)
        dlr_ref[...] = dlr
        dw2_ref[...] += dw2
        db_ref[...] += db
        dgn_ref[...] += dgn

    return _hosted_call(
        body, rider, name="gla_bwd", grid=(bsz, nc),
        in_specs=[qk(ZB_GQ), qk(ZB_GK), vv(ZB_GV), vv(ZB_GG), vv(ZB_GA), lr(ZS_LR // LANE),
                  full((LANE, GLA_QK)), full((1, GLA_QK)), full((1, GLA_DV)),
                  pl.BlockSpec((1, 1, GLA_HEADS, GLA_DV, GLA_DK), lambda b, n: (b, nc - 1 - n, 0, 0, 0)),
                  vv(0)],
        out_specs=[qk(0), qk(0), vv(0), vv(0), vv(0), lr(0),
                   full((GLA_HEADS, LANE, GLA_DK)), full((GLA_HEADS, 1, GLA_DK)), full((1, GLA_DV))],
        out_shape=[jax.ShapeDtypeStruct((t, GLA_QK), BF16), jax.ShapeDtypeStruct((t, GLA_QK), BF16),
                   jax.ShapeDtypeStruct((t, GLA_V), BF16), jax.ShapeDtypeStruct((t, GLA_V), BF16),
                   jax.ShapeDtypeStruct((t, GLA_V), BF16), jax.ShapeDtypeStruct((t, LANE), F32),
                   jax.ShapeDtypeStruct((GLA_HEADS, LANE, GLA_DK), F32),
                   jax.ShapeDtypeStruct((GLA_HEADS, 1, GLA_DK), F32),
                   jax.ShapeDtypeStruct((1, GLA_DV), F32)],
        scratch_shapes=[pltpu.VMEM((GLA_HEADS, GLA_DV, GLA_DK), F32)],
        args=(zb, zb, zb, zb, zb, zs, w2p, gla_b, gla_norm, states, dmixed))


DN_HB = DN_HEADS


def _dn_specs(nc, reverse):
    wide = DN_HB * DN_D

    def rows(b, n, h):
        return b * nc + ((nc - 1 - n) if reverse else n)

    def col(base):
        return pl.BlockSpec((CHUNK, wide), lambda b, n, h: (rows(b, n, h), base // wide + h))

    def fixed(c):
        return pl.BlockSpec((CHUNK, LANE), lambda b, n, h: (rows(b, n, h), c))

    head = pl.BlockSpec((DN_HB, 1, LANE), lambda b, n, h: (h, 0, 0))
    return rows, col, fixed, head


def _lanes(j):
    return slice(j * DN_D, (j + 1) * DN_D)


def _by_head(ref):
    return jnp.stack([ref[:, _lanes(j)] for j in range(DN_HB)], axis=0)


def _dn_fwd(act, zb, zs, alog_b, dtb_b, dn_norm, mix_gla, bsz, nc, rider=None):
    t = zb.shape[0]
    rows, col, fixed, head = _dn_specs(nc, False)

    def body(q_ref, k_ref, v_ref, za_ref, zb_ref, al_ref, dt_ref, gz_ref, gb_ref, dn_ref, mg_ref,
             o_ref, st_ref, ti_ref, state):
        s = jnp.where(pl.program_id(1) > 0, state[...], 0.0)
        st_ref[0, 0] = s
        res, s_new, t_inv = _dn_chunk(s, _by_head(q_ref), _by_head(k_ref), _by_head(v_ref), za_ref[...], zb_ref[...],
                                      al_ref[...], dt_ref[...], _by_head(gz_ref), _by_head(gb_ref), dn_ref[...])
        ti_ref[0, 0] = t_inv
        for j in range(DN_HB):
            o_ref[:, _lanes(j)] = (res[j] + mg_ref[:, _lanes(j)]).astype(o_ref.dtype)
        state[...] = s_new

    return _hosted_call(
        body, rider, name="dn_fwd", grid=(bsz, nc, DN_HEADS // DN_HB),
        in_specs=[col(0), col(DN_HEADS * DN_D), col(2 * DN_HEADS * DN_D),
                  fixed(ZS_DA // LANE), fixed(ZS_DB // LANE), head, head,
                  col(ZB_DZ), col(ZB_GB), pl.BlockSpec((1, DN_D), lambda b, n, h: (0, 0)), col(0)],
        out_specs=[col(0), pl.BlockSpec((1, 1, DN_HB, DN_D, DN_D), lambda b, n, h: (b, n, h, 0, 0)),
                   pl.BlockSpec((1, 1, DN_HB, CHUNK, CHUNK), lambda b, n, h: (b, n, h, 0, 0))],
        out_shape=[jax.ShapeDtypeStruct((t, D_MODEL), BF16),
                   jax.ShapeDtypeStruct((bsz, nc, DN_HEADS, DN_D, DN_D), F32),
                   jax.ShapeDtypeStruct((bsz, nc, DN_HEADS, CHUNK, CHUNK), F32)],
        scratch_shapes=[pltpu.VMEM((DN_HEADS, DN_D, DN_D), F32)],
        args=(act, act, act, zs, zs, alog_b, dtb_b, zb, zb, dn_norm, mix_gla))


def _dn_bwd(act, zb, zs, alog_b, dtb_b, dn_norm, states, t_invs, dmixed, bsz, nc, rider=None):
    t = zb.shape[0]
    rows, col, fixed, head = _dn_specs(nc, True)

    def body(q_ref, k_ref, v_ref, za_ref, zb_ref, al_ref, dt_ref, gz_ref, gb_ref, dn_ref, st_ref, ti_ref, dm_ref,
             dact_ref, dza_ref, dzb_ref, dgz_ref, dgb_ref, dal_ref, ddt_ref, ddn_ref, dstate):
        b, n = pl.program_id(0), pl.program_id(1)

        @pl.when((b == 0) & (n == 0))
        def _():
            dal_ref[...] = jnp.zeros_like(dal_ref)
            ddt_ref[...] = jnp.zeros_like(ddt_ref)
            ddn_ref[...] = jnp.zeros_like(ddn_ref)

        fn = functools.partial(_dn_chunk, t_saved=ti_ref[0, 0])
        _, vjp = jax.vjp(fn, st_ref[0, 0], _by_head(q_ref), _by_head(k_ref), _by_head(v_ref), za_ref[...],
                         zb_ref[...], al_ref[...], dt_ref[...], _by_head(gz_ref), _by_head(gb_ref), dn_ref[...])
        ds_in = jnp.where(n > 0, dstate[...], 0.0)
        ds, dq, dk, dv, dza, dzb, dal, ddt, dgz, dgb, ddn = vjp((_by_head(dm_ref).astype(F32), ds_in))
        dstate[...] = ds
        for j in range(DN_HB):
            for part, d in enumerate((dq, dk, dv)):
                dact_ref[:, pl.ds(part * DN_HEADS * DN_D + j * DN_D, DN_D)] = d[j]
            dgz_ref[:, _lanes(j)] = dgz[j].astype(dgz_ref.dtype)
            dgb_ref[:, _lanes(j)] = dgb[j].astype(dgb_ref.dtype)
        dal_ref[...] += dal
        ddt_ref[...] += ddt
        dza_ref[...] = dza
        dzb_ref[...] = dzb
        ddn_ref[...] += ddn

    full = lambda shape: pl.BlockSpec(shape, lambda b, n, h: (0,) * len(shape))
    return _hosted_call(
        body, rider, name="dn_bwd", grid=(bsz, nc, DN_HEADS // DN_HB),
        in_specs=[col(0), col(DN_HEADS * DN_D), col(2 * DN_HEADS * DN_D),
                  fixed(ZS_DA // LANE), fixed(ZS_DB // LANE), head, head,
                  col(ZB_DZ), col(ZB_GB), pl.BlockSpec((1, DN_D), lambda b, n, h: (0, 0)),
                  pl.BlockSpec((1, 1, DN_HB, DN_D, DN_D), lambda b, n, h: (b, nc - 1 - n, h, 0, 0)),
                  pl.BlockSpec((1, 1, DN_HB, CHUNK, CHUNK), lambda b, n, h: (b, nc - 1 - n, h, 0, 0)), col(0)],
        out_specs=[pl.BlockSpec((CHUNK, DN_QKV), lambda b, n, h: (rows(b, n, h), 0)), fixed(0), fixed(0), col(0), col(0),
                   full((DN_HEADS, 1, LANE)), full((DN_HEADS, 1, LANE)), full((1, DN_D))],
        out_shape=[jax.ShapeDtypeStruct((t, DN_QKV), F32),
                   jax.ShapeDtypeStruct((t, LANE), F32), jax.ShapeDtypeStruct((t, LANE), F32),
                   jax.ShapeDtypeStruct((t, D_MODEL), BF16), jax.ShapeDtypeStruct((t, D_MODEL), BF16),
                   jax.ShapeDtypeStruct((DN_HEADS, 1, LANE), F32), jax.ShapeDtypeStruct((DN_HEADS, 1, LANE), F32),
                   jax.ShapeDtypeStruct((1, DN_D), F32)],
        scratch_shapes=[pltpu.VMEM((DN_HEADS, DN_D, DN_D), F32)],
        args=(act, act, act, zs, zs, alog_b, dtb_b, zb, zb, dn_norm, states, t_invs, dmixed))


MM_VMEM_BUDGET = 40 * 1024 * 1024
MM_TILE_PREF = (1024, 1024, 2048)


def _divisor_tile(n, cap):
    if n <= cap:
        return n
    for c in range(cap - cap % LANE, 0, -LANE):
        if n % c == 0:
            return c
    return n


def _mm_tiles(m, n, kd, a_bytes, b_bytes, mn_bytes):
    tm, tn, tk = (_divisor_tile(d, c) for d, c in zip((m, n, kd), MM_TILE_PREF))

    def need(tm, tn, tk):
        acc = 0 if tk == kd else 4 * tm * tn
        return 2 * (tm * tk * a_bytes + tk * tn * b_bytes + tm * tn * mn_bytes) + acc + 4 * tm * tn

    while need(tm, tn, tk) > MM_VMEM_BUDGET:
        if tk > 512 and tk * max(tm * a_bytes, tn * b_bytes) >= tm * tn * mn_bytes:
            tk = _divisor_tile(kd, tk // 2)
        elif tn >= tm and tn > LANE:
            tn = _divisor_tile(n, tn // 2)
        else:
            tm = _divisor_tile(m, tm // 2)
    return tm, tn, tk


def _w_in_row_of(tile, tile_rows):
    skipped = jnp.where(tile >= ZB_GA // tile_rows, WI_GA - ZB_GA, jnp.where(tile >= ZB_DQKV // tile_rows,
                                                                             WI_DQKV - ZB_DQKV, 0))
    return pl.multiple_of(tile * tile_rows + skipped, 16)


def _mm(a, b, *, ta=False, tb=False, out_dtypes=(F32,), epilogue=None, extras=(), name, rider=None,
        b_is_w_in_t=False, out_is_w_in_t=False):
    m, kd = (a.shape[1], a.shape[0]) if ta else a.shape
    n = b.shape[0] if tb else b.shape[1]
    if b_is_w_in_t:
        n, kd = (ZB_W, kd) if tb else (n, ZB_W)
    mn_bytes = sum(e.dtype.itemsize for e in extras) + sum(jnp.dtype(dt).itemsize for dt in out_dtypes)
    tm, tn, tk = _mm_tiles(m, n, kd, a.dtype.itemsize, b.dtype.itemsize, mn_bytes)
    nk = kd // tk
    n_ex = len(extras)
    dims = (((0,) if ta else (1,), (1,) if tb else (0,)), ((), ()))

    def finish(acc, ex_refs, out_refs):
        outs = (acc,) if epilogue is None else epilogue(acc, *[r[...] for r in ex_refs])
        for r, o in zip(out_refs, outs):
            r[...] = o.astype(r.dtype)

    def partial_product(a_ref, b_ref):
        return lax.dot_general(a_ref[...].astype(BF16), b_ref[...].astype(BF16), dims, preferred_element_type=F32)

    def body_single(*refs):
        finish(partial_product(refs[0], refs[1]), refs[2:2 + n_ex], refs[2 + n_ex:])

    def body_acc(*refs):
        acc = refs[-1]
        k = pl.program_id(2)

        @pl.when(k == 0)
        def _():
            acc[...] = partial_product(refs[0], refs[1])

        @pl.when(k > 0)
        def _():
            acc[...] += partial_product(refs[0], refs[1])

        @pl.when(k == nk - 1)
        def _():
            finish(acc[...], refs[2:2 + n_ex], refs[2 + n_ex:-1])

    a_spec = pl.BlockSpec((tk, tm), lambda i, j, k: (k, i)) if ta else pl.BlockSpec((tm, tk), lambda i, j, k: (i, k))
    b_spec = pl.BlockSpec((tn, tk), lambda i, j, k: (j, k)) if tb else pl.BlockSpec((tk, tn), lambda i, j, k: (k, j))
    mn_spec = pl.BlockSpec((tm, tn), lambda i, j, k: (i, j))
    out_spec, out_rows = mn_spec, m
    if b_is_w_in_t and tb:
        b_spec = pl.BlockSpec((pl.Element(tn), pl.Element(tk)),
                              lambda i, j, k: (_w_in_row_of(j, tn), pl.multiple_of(k * tk, LANE)))
    elif b_is_w_in_t:
        b_spec = pl.BlockSpec((pl.Element(tk), pl.Element(tn)),
                              lambda i, j, k: (_w_in_row_of(k, tk), pl.multiple_of(j * tn, LANE)))
    if out_is_w_in_t:
        out_spec, out_rows = pl.BlockSpec((pl.Element(tm), pl.Element(tn)),
                                          lambda i, j, k: (_w_in_row_of(i, tm), pl.multiple_of(j * tn, LANE))), D_IN
    outs = _hosted_call(
        body_single if nk == 1 else body_acc, rider, name=name, grid=(m // tm, n // tn, nk),
        in_specs=[a_spec, b_spec] + [mn_spec] * n_ex,
        out_specs=[out_spec] * len(out_dtypes),
        out_shape=[jax.ShapeDtypeStruct((out_rows, n), dt) for dt in out_dtypes],
        scratch_shapes=[] if nk == 1 else [pltpu.VMEM((tm, tn), F32)],
        args=(a, b, *extras))
    return outs[0] if len(outs) == 1 else outs


ROW_BLOCK = 256


def _row_spec(width=D_MODEL):
    return pl.BlockSpec((ROW_BLOCK, width), lambda i: (i, 0))


def _vec_spec(width=D_MODEL):
    return pl.BlockSpec((1, width), lambda i: (0, 0))


def _rms_fwd(x, g, name):
    def body(x_ref, g_ref, h_ref):
        xf = x_ref[...]
        h_ref[...] = (xf * lax.rsqrt(jnp.mean(xf * xf, axis=-1, keepdims=True) + EPS) * g_ref[...]).astype(BF16)

    t = x.shape[0]
    return pl.pallas_call(
        body, name=name, grid=(t // ROW_BLOCK,), in_specs=[_row_spec(), _vec_spec()], out_specs=_row_spec(),
        out_shape=jax.ShapeDtypeStruct((t, D_MODEL), BF16), compiler_params=_cparams(("parallel",)),
    )(x, g)


def _rms_bwd_math(xf, g, dh):
    rstd = lax.rsqrt(jnp.mean(xf * xf, axis=-1, keepdims=True) + EPS)
    xhat = xf * rstd
    dxhat = dh * g
    dx = rstd * (dxhat - xhat * jnp.mean(dxhat * xhat, axis=-1, keepdims=True))
    dg = jnp.sum(dh * xhat, axis=0, keepdims=True)
    return dx, dg


def _rms_bwd(x, g, dh, dres, name):
    def body(x_ref, g_ref, dh_ref, dres_ref, dx_ref, dg_ref):
        dx, dg = _rms_bwd_math(x_ref[...], g_ref[...], dh_ref[...].astype(F32))
        dx_ref[...] = dres_ref[...] + dx

        @pl.when(pl.program_id(0) == 0)
        def _():
            dg_ref[...] = jnp.zeros_like(dg_ref)

        dg_ref[...] += dg

    t = x.shape[0]
    return pl.pallas_call(
        body, name=name, grid=(t // ROW_BLOCK,),
        in_specs=[_row_spec(), _vec_spec(), _row_spec(), _row_spec()], out_specs=[_row_spec(), _vec_spec()],
        out_shape=[jax.ShapeDtypeStruct((t, D_MODEL), F32), jax.ShapeDtypeStruct((1, D_MODEL), F32)],
        compiler_params=_cparams(("arbitrary",)),
    )(x, g, dh, dres)


def _loss_head(x3, g, target):
    def body(x_ref, g_ref, t_ref, dx_ref, dg_ref, loss_ref):
        xf, gg = x_ref[...], g_ref[...]
        rstd = lax.rsqrt(jnp.mean(xf * xf, axis=-1, keepdims=True) + EPS)
        err = xf * rstd * gg - t_ref[...]
        dx, dg = _rms_bwd_math(xf, gg, err * (1.0 / D_MODEL))
        dx_ref[...] = dx

        @pl.when(pl.program_id(0) == 0)
        def _():
            dg_ref[...] = jnp.zeros_like(dg_ref)
            loss_ref[...] = jnp.zeros_like(loss_ref)

        dg_ref[...] += dg
        part = jnp.sum(jnp.sum(err * err, axis=-1, keepdims=True), axis=0, keepdims=True) * (0.5 / D_MODEL)
        loss_ref[...] += jnp.broadcast_to(part, loss_ref.shape)

    t = x3.shape[0]
    return pl.pallas_call(
        body, name="loss_head", grid=(t // ROW_BLOCK,),
        in_specs=[_row_spec(), _vec_spec(), _row_spec()], out_specs=[_row_spec(), _vec_spec(), _vec_spec(LANE)],
        out_shape=[jax.ShapeDtypeStruct((t, D_MODEL), F32), jax.ShapeDtypeStruct((1, D_MODEL), F32),
                   jax.ShapeDtypeStruct((1, LANE), F32)],
        compiler_params=_cparams(("arbitrary",)),
    )(x3, g, target)


def _ple_bwd(dx3, gpre, pp):
    def body(dx_ref, gp_ref, pp_ref, dgp_ref, dpp_ref):
        dx, sg = dx_ref[...], _sigmoid(gp_ref[...])
        dpp_ref[...] = (dx * sg).astype(BF16)
        dgp_ref[...] = (dx * pp_ref[...] * sg * (1.0 - sg)).astype(BF16)

    t = dx3.shape[0]
    return pl.pallas_call(
        body, name="ple_bwd", grid=(t // ROW_BLOCK,), in_specs=[_row_spec()] * 3, out_specs=[_row_spec()] * 2,
        out_shape=[jax.ShapeDtypeStruct((t, D_MODEL), BF16)] * 2, compiler_params=_cparams(("parallel",)),
    )(dx3, gpre, pp)


CONV_COLS = 256


def _shift_down(x, s):
    if s == 0:
        return x
    return jnp.where(_iota2(x.shape, 0) >= s, pltpu.roll(x, s, 0), 0.0)


def _shift_up(x, s):
    if s == 0:
        return x
    rows = x.shape[0]
    return jnp.where(_iota2(x.shape, 0) < rows - s, pltpu.roll(x, rows - s, 0), 0.0)


def _conv_pre(xf, w):
    return sum(_shift_down(xf, DN_CONV - 1 - j) * w[j:j + 1, :] for j in range(DN_CONV))


def _conv_fwd(zb, conv_w, bsz, seq):
    def body(x_ref, w_ref, y_ref):
        y_ref[...] = _silu(_conv_pre(x_ref[...], w_ref[...]))

    nblk = DN_QKV // CONV_COLS
    return pl.pallas_call(
        body, name="conv_fwd", grid=(bsz, nblk),
        in_specs=[pl.BlockSpec((seq, CONV_COLS), lambda b, j: (b, ZB_DQKV // CONV_COLS + j)),
                  pl.BlockSpec((DN_CONV, CONV_COLS), lambda b, j: (0, j))],
        out_specs=pl.BlockSpec((seq, CONV_COLS), lambda b, j: (b, j)),
        out_shape=jax.ShapeDtypeStruct((bsz * seq, DN_QKV), F32),
        compiler_params=_cparams(("parallel", "parallel")),
    )(zb, conv_w)


def _conv_bwd(zb, conv_w, dact, bsz, seq):
    def body(x_ref, w_ref, dy_ref, dx_ref, dw_ref):
        xf, w = x_ref[...], w_ref[...]
        c = _conv_pre(xf, w)
        sg = _sigmoid(c)
        dc = dy_ref[...].astype(F32) * sg * (1.0 + c * (1.0 - sg))
        dx = sum(_shift_up(dc, DN_CONV - 1 - j) * w[j:j + 1, :] for j in range(DN_CONV))
        dx_ref[...] = dx.astype(BF16)
        dw = jnp.concatenate(
            [jnp.sum(dc * _shift_down(xf, DN_CONV - 1 - j), axis=0, keepdims=True) for j in range(DN_CONV)], axis=0)

        @pl.when(pl.program_id(1) == 0)
        def _():
            dw_ref[...] = jnp.zeros_like(dw_ref)

        dw_ref[...] += dw

    nblk = DN_QKV // CONV_COLS
    return pl.pallas_call(
        body, name="conv_bwd", grid=(nblk, bsz),
        in_specs=[pl.BlockSpec((seq, CONV_COLS), lambda j, b: (b, ZB_DQKV // CONV_COLS + j)),
                  pl.BlockSpec((DN_CONV, CONV_COLS), lambda j, b: (0, j)),
                  pl.BlockSpec((seq, CONV_COLS), lambda j, b: (b, j))],
        out_specs=[pl.BlockSpec((seq, CONV_COLS), lambda j, b: (b, j)),
                   pl.BlockSpec((DN_CONV, CONV_COLS), lambda j, b: (0, j))],
        out_shape=[jax.ShapeDtypeStruct((bsz * seq, DN_QKV), BF16), jax.ShapeDtypeStruct((DN_CONV, DN_QKV), F32)],
        compiler_params=_cparams(("parallel", "arbitrary")),
    )(zb, conv_w, dact)


MESH_IDS = pl.DeviceIdType.MESH
ANY_SPEC = pl.BlockSpec(memory_space=pl.ANY)


COMM_SCRATCH = (pltpu.SemaphoreType.DMA((7,)), pltpu.SemaphoreType.DMA((7,)), pltpu.SemaphoreType.DMA)


def _gather_phases(x_ref, out_ref, send_sems, recv_sems, local_sem):
    mx, my, mc = lax.axis_index("x"), lax.axis_index("y"), lax.axis_index("c")
    me, sibling = (mx, my, mc), (mx, my, 1 - mc)
    chips = [(1 - mx, my), (mx, 1 - my), (1 - mx, 1 - my)]

    def slot(px, py, pc):
        return out_ref.at[4 * px + 2 * py + pc]

    def copy(k, block, to, src=None):
        return pltpu.make_async_remote_copy(
            src_ref=slot(*block) if src is None else src, dst_ref=slot(*block),
            send_sem=send_sems.at[k], recv_sem=recv_sems.at[k], device_id=to, device_id_type=MESH_IDS)

    def mine():
        return pltpu.make_async_copy(x_ref, slot(*me), local_sem)

    def first():
        return [copy(0, me, sibling, src=x_ref)] + [copy(1 + j, me, (*chip, mc), src=x_ref)
                                                    for j, chip in enumerate(chips)]

    def passed():
        return [copy(4 + j, (*chip, mc), sibling) for j, chip in enumerate(chips)]

    def start():
        mine().start()
        for cp in first():
            cp.start()

    def forward():
        for j, (chip, cp) in enumerate(zip(chips, passed())):
            copy(1 + j, (*chip, mc), me).wait_recv()
            cp.start()

    def finish():
        copy(0, sibling, me).wait_recv()
        for j, chip in enumerate(chips):
            copy(4 + j, (*chip, 1 - mc), me).wait_recv()
        for cp in first() + passed():
            cp.wait_send()
        mine().wait()

    return start, forward, finish


def _scatter_phases(x_ref, out_ref, send_sems, recv_sems, local_sem, among_chips=False):
    mx, my, mc = lax.axis_index("x"), lax.axis_index("y"), lax.axis_index("c")
    n_peers = 4 if among_chips else N_DEV
    me = 2 * mx + my if among_chips else 4 * mx + 2 * my + mc

    def peer(k):
        if among_chips:
            return (mx ^ ((k >> 1) & 1), my ^ (k & 1), mc)
        return (mx ^ ((k >> 2) & 1), my ^ ((k >> 1) & 1), mc ^ (k & 1))

    def slot_of(k):
        px, py, pc = peer(k)
        return 2 * px + py if among_chips else 4 * px + 2 * py + pc

    def copy(k, src_slot, dst_slot):
        return pltpu.make_async_remote_copy(
            src_ref=x_ref.at[src_slot], dst_ref=out_ref.at[dst_slot],
            send_sem=send_sems.at[k - 1], recv_sem=recv_sems.at[k - 1],
            device_id=peer(k), device_id_type=MESH_IDS)

    def sends():
        return [copy(k, slot_of(k), me) for k in range(1, n_peers)]

    def mine():
        return pltpu.make_async_copy(x_ref.at[me], out_ref.at[me], local_sem)

    def start():
        mine().start()
        for cp in sends():
            cp.start()

    def forward():
        pass

    def finish():
        for k in range(1, n_peers):
            copy(k, me, slot_of(k)).wait_recv()
        for cp in sends():
            cp.wait_send()
        mine().wait()

    return start, forward, finish


def _pair_phases(x_ref, out_ref, send_sems, recv_sems, local_sem):
    mx, my, mc = lax.axis_index("x"), lax.axis_index("y"), lax.axis_index("c")

    def copy(side):
        return pltpu.make_async_remote_copy(
            src_ref=x_ref.at[:, side], dst_ref=out_ref, send_sem=send_sems.at[0], recv_sem=recv_sems.at[0],
            device_id=(mx, my, 1 - mc), device_id_type=MESH_IDS)

    def start():
        copy(1 - mc).start()

    def forward():
        pass

    def finish():
        copy(mc).wait_recv()
        copy(1 - mc).wait_send()

    return start, forward, finish


class _Rider:
    def __init__(self, phases, x, out_shape):
        self.phases, self.x, self.out_shape = phases, x, out_shape


def _gather_rider(x):
    return _Rider(_gather_phases, x, jax.ShapeDtypeStruct((N_DEV,) + x.shape, x.dtype))


def _scatter_rider(x):
    return _Rider(_scatter_phases, x, jax.ShapeDtypeStruct(x.shape, x.dtype))


def _chip_scatter_rider(x):
    return _Rider(functools.partial(_scatter_phases, among_chips=True), x, jax.ShapeDtypeStruct(x.shape, x.dtype))


def _pair_rider(x):
    return _Rider(_pair_phases, x, jax.ShapeDtypeStruct((x.shape[0],) + x.shape[2:], x.dtype))


def _exchange(rider, name):
    def body(x_ref, out_ref, send_sems, recv_sems, local_sem):
        for phase in rider.phases(x_ref, out_ref, send_sems, recv_sems, local_sem):
            phase()

    return pl.pallas_call(body, name=name, out_shape=rider.out_shape, in_specs=[ANY_SPEC], out_specs=ANY_SPEC,
                          scratch_shapes=list(COMM_SCRATCH))(rider.x)


def _all_gather(x, name):
    return _exchange(_gather_rider(x), name)


def _all_to_all(x, name):
    return _exchange(_scatter_rider(x), name)


def _hosted_call(body, rider, *, name, grid, in_specs, out_specs, out_shape, scratch_shapes, args):
    if rider is None:
        return pl.pallas_call(body, name=name, grid=grid, in_specs=in_specs, out_specs=out_specs, out_shape=out_shape,
                              scratch_shapes=scratch_shapes, compiler_params=_cparams(("arbitrary",) * len(grid)))(*args)
    n_in, n_out, n_scr = len(in_specs), len(out_specs), len(scratch_shapes)
    total = math.prod(grid)

    def riding(*refs):
        host_in, x_ref = refs[:n_in], refs[n_in]
        host_out, out_ref = refs[n_in + 1:n_in + 1 + n_out], refs[n_in + 1 + n_out]
        host_scr = refs[n_in + 2 + n_out:n_in + 2 + n_out + n_scr]
        start, forward, finish = rider.phases(x_ref, out_ref, *refs[n_in + 2 + n_out + n_scr:])
        step = 0
        for axis, size in enumerate(grid):
            step = step * size + pl.program_id(axis)
        pl.when(step == 0)(start)
        pl.when(step == (3 * total) // 4)(forward)
        body(*host_in, *host_out, *host_scr)
        pl.when(step == total - 1)(finish)

    return pl.pallas_call(
        riding, name=name, grid=grid, in_specs=list(in_specs) + [ANY_SPEC], out_specs=list(out_specs) + [ANY_SPEC],
        out_shape=list(out_shape) + [rider.out_shape], scratch_shapes=list(scratch_shapes) + list(COMM_SCRATCH),
        compiler_params=_cparams(("arbitrary",) * len(grid)))(*args, rider.x)


def _adamw_math(w, g, m, v):
    m = ADAM_B1 * m + (1.0 - ADAM_B1) * g
    v = ADAM_B2 * v + (1.0 - ADAM_B2) * jnp.square(g)
    m_hat = m / (1.0 - ADAM_B1 ** ADAM_STEP)
    v_hat = v / (1.0 - ADAM_B2 ** ADAM_STEP)
    delta = -ADAM_LR * (m_hat / (jnp.sqrt(v_hat) + ADAM_EPS) + ADAM_WD * w)
    return delta, m, v


ADAM_ROWS = 128


def _elementwise_tile(rows, cols):
    if rows % ADAM_ROWS == 0:
        return ADAM_ROWS, cols
    return rows, (2 * LANE if cols % (2 * LANE) == 0 else cols)


def _add_blocks(a, b):
    g, rows, cols = a.shape
    tr, tc = _elementwise_tile(rows, cols)

    def body(a_ref, b_ref, o_ref):
        o_ref[...] = (a_ref[...].astype(F32) + b_ref[...].astype(F32)).astype(o_ref.dtype)

    blk = pl.BlockSpec((None, tr, tc), lambda k, i, j: (k, i, j))
    return pl.pallas_call(body, name="add_pair_blocks", grid=(g, rows // tr, cols // tc), in_specs=[blk, blk],
                          out_specs=blk, out_shape=jax.ShapeDtypeStruct(a.shape, a.dtype),
                          compiler_params=_cparams(("parallel", "parallel", "parallel")))(a, b)


def _adamw_reduce(w, m, v, parts, row0, name):
    rows, cols = w.shape
    n_parts = parts.shape[0]
    tr, tc = _elementwise_tile(rows, cols)
    r0 = row0 // tr

    def body(w_ref, m_ref, v_ref, *refs):
        part_refs, (g_ref, d_ref, nm_ref, nv_ref) = refs[:n_parts], refs[n_parts:]
        g = part_refs[0][...].astype(F32)
        for r in part_refs[1:]:
            g = g + r[...].astype(F32)
        delta, nm, nv = _adamw_math(w_ref[...], g, m_ref[...], v_ref[...])
        g_ref[...] = g
        d_ref[...] = delta
        nm_ref[...] = nm
        nv_ref[...] = nv

    blk = pl.BlockSpec((tr, tc), lambda i, j: (i, j))
    part_specs = [pl.BlockSpec((None, tr, tc), functools.partial(lambda i, j, k: (k, r0 + i, j), k=k))
                  for k in range(n_parts)]
    return pl.pallas_call(
        body, name=name, grid=(rows // tr, cols // tc), in_specs=[blk] * 3 + part_specs, out_specs=[blk] * 4,
        out_shape=[jax.ShapeDtypeStruct(w.shape, F32)] * 4, compiler_params=_cparams(("parallel", "parallel")),
    )(w, m, v, *([parts] * n_parts))


def _small_reduce(gathered, lane_sum_from):
    r = gathered.shape[1]

    def body(g_ref, o_ref):
        g = g_ref[0]
        for k in range(1, N_DEV):
            g = g + g_ref[k]
        tot = jnp.broadcast_to(jnp.sum(g, axis=-1, keepdims=True), g.shape)
        o_ref[...] = jnp.where(_iota2(g.shape, 0) >= lane_sum_from, tot, g)

    return pl.pallas_call(body, name="small_grad_reduce", out_shape=jax.ShapeDtypeStruct((r, LANE), F32))(gathered)


def _adamw_small(w, m, v, g):
    def body(w_ref, m_ref, v_ref, g_ref, d_ref, nm_ref, nv_ref):
        d_ref[...], nm_ref[...], nv_ref[...] = _adamw_math(w_ref[...], g_ref[...], m_ref[...], v_ref[...])

    return pl.pallas_call(body, name="adamw_small", out_shape=[jax.ShapeDtypeStruct(w.shape, F32)] * 3)(w, m, v, g)


def _pack_rows(arrays):
    rows = [jnp.pad(a.reshape(-1), (0, -a.size % LANE)).reshape(-1, LANE) for a in arrays]
    out = jnp.concatenate(rows, axis=0)
    return jnp.pad(out, ((0, -out.shape[0] % 8), (0, 0)))


def _unpack_rows(packed, shapes):
    out, r = [], 0
    for shp in shapes:
        size = math.prod(shp)
        nrows = -(-size // LANE)
        out.append(packed[r:r + nrows].reshape(-1)[:size].reshape(shp))
        r += nrows
    return out


def _add_residual(acc, res):
    return (res + acc,)


def _by_cols(g):
    return g.reshape(g.shape[0], N_DEV, -1).transpose(1, 0, 2)


def _from_cols(blocks):
    return blocks.transpose(1, 0, 2).reshape(blocks.shape[1], -1)


ROWS_OUT = D_MODEL // N_DEV

W_IN_SEGMENTS = ((0, WI_LR, "big", 0), (WI_LR, WI_DQKV, "gates", ZS_LR), (WI_DQKV, WI_DA, "big", ZB_DQKV),
                 (WI_DA, WI_DB, "gates", ZS_DA), (WI_DB, WI_GA, "gates", ZS_DB), (WI_GA, D_IN, "big", ZB_GA))


def _w_in_gate_rows(wt):
    parts = []
    for lo, hi, which, _ in W_IN_SEGMENTS:
        if which == "gates":
            parts += [wt[lo:hi], jnp.zeros((LANE - (hi - lo), wt.shape[1]), wt.dtype)]
    return jnp.concatenate(parts, axis=0)


def _fill_gate_rows(dw_in_t, dws_t):
    for lo, hi, which, first in W_IN_SEGMENTS:
        if which == "gates":
            dw_in_t = lax.dynamic_update_slice(dw_in_t, dws_t[first:first + hi - lo], (lo, 0))
    return dw_in_t


def _local_step(x, p, target, w, up_shard, rows_shard):
    bsz, seq, _ = x.shape
    t, nc = bsz * seq, seq // CHUNK
    x0, p2, tgt = x.reshape(t, D_MODEL), p.reshape(t, PLE_DIM), target.reshape(t, D_MODEL)

    h = _rms_fwd(x0, w["g_mix"], "rms_mix")
    zb, up_blocks = _mm(h, w["w_in_t"], tb=True, b_is_w_in_t=True, name="in_proj", rider=_gather_rider(up_shard))
    w_up = _from_cols(up_blocks)
    zs = _mm(h, w["ws_t"], tb=True, name="in_proj_gates")
    act = _conv_fwd(zb, w["conv"], bsz, seq)
    mix_gla, gla_states = _gla_fwd(zb, zs, w["w2p"], w["gla_b"], w["gla_norm"], bsz, nc)
    mixed, dn_states, dn_t_invs, row_blocks = _dn_fwd(act, zb, zs, w["alog_b"], w["dtb_b"], w["dn_norm"], mix_gla,
                                                      bsz, nc, rider=_gather_rider(rows_shard))
    w_out = row_blocks[:, :ROWS_OUT].reshape(D_MODEL, D_MODEL)
    w_pg = row_blocks[:, ROWS_OUT:2 * ROWS_OUT].reshape(D_MODEL, D_MODEL)
    w_down = row_blocks[:, 2 * ROWS_OUT:].reshape(D_FF, D_MODEL)
    x1 = _mm(mixed, w_out, epilogue=_add_residual, extras=(x0,), name="out_proj")
    h2 = _rms_fwd(x1, w["g_mlp"], "rms_mlp")
    u, a = _mm(h2, w_up, out_dtypes=(BF16, BF16), name="mlp_up",
               epilogue=lambda acc: (acc, jnp.square(jnp.maximum(acc, 0.0))))
    x2 = _mm(a, w_down, epilogue=_add_residual, extras=(x1,), name="mlp_down")
    h3 = _rms_fwd(x2, w["g_ple"], "rms_ple")
    pp = _mm(p2, w["w_pp"], name="ple_proj")
    gpre, x3 = _mm(h3, w_pg, out_dtypes=(F32, F32), extras=(x2, pp), name="ple_gate",
                   epilogue=lambda acc, res, proj: (acc, res + _sigmoid(acc) * proj))
    dx3, dg_final, loss = _loss_head(x3, w["g_final"], tgt)

    dgpre, dpp = _ple_bwd(dx3, gpre, pp)
    dw_pp = _mm(p2, dpp, ta=True, out_dtypes=(BF16,), name="d_w_ple_proj")
    dw_pg = _mm(h3, dgpre, ta=True, out_dtypes=(BF16,), name="d_w_ple_gate")
    dh3 = _mm(dgpre, w_pg, tb=True, name="d_h_ple")
    dx2, dg_ple = _rms_bwd(x2, w["g_ple"], dh3, dx3, "rms_ple_bwd")
    du = _mm(dx2, w_down, tb=True, out_dtypes=(BF16,), extras=(u,), name="d_mlp_hidden",
             epilogue=lambda acc, uu: (acc * (2.0 * jnp.maximum(uu.astype(F32), 0.0)),))
    dw_down = _mm(a, dx2, ta=True, out_dtypes=(BF16,), name="d_w_down")
    dw_up = _mm(h2, du, ta=True, out_dtypes=(BF16,), name="d_w_up")
    dh2 = _mm(du, w_up, tb=True, name="d_h_mlp")
    dx1, dg_mlp = _rms_bwd(x1, w["g_mlp"], dh2, dx2, "rms_mlp_bwd")
    dmixed = _mm(dx1, w_out, tb=True, out_dtypes=(BF16,), name="d_mixed")
    dw_out = _mm(mixed, dx1, ta=True, out_dtypes=(BF16,), name="d_w_out")

    d_rows = jnp.concatenate([dw_out.reshape(N_DEV, ROWS_OUT, D_MODEL), dw_pg.reshape(N_DEV, ROWS_OUT, D_MODEL),
                              dw_down.reshape(N_DEV, D_FF // N_DEV, D_MODEL)], axis=1)
    (dact, dza, dzb_, dgz, dgb, dal, ddt, ddn, recv_rows) = _dn_bwd(
        act, zb, zs, w["alog_b"], w["dtb_b"], w["dn_norm"], dn_states, dn_t_invs, dmixed, bsz, nc,
        rider=_scatter_rider(d_rows))
    (gdq, gdk, gdv, dgg, dga, dlr, dw2, dgla_b, dgla_norm, recv_up) = _gla_bwd(
        zb, zs, w["w2p"], w["gla_b"], w["gla_norm"], gla_states, dmixed, bsz, nc,
        rider=_scatter_rider(_by_cols(dw_up)))
    dqkv, dconv = _conv_bwd(zb, w["conv"], dact, bsz, seq)
    dzb = jnp.concatenate([gdq, gdk, gdv, dgg, dqkv, dgz, dga, dgb], axis=1)
    dzs = jnp.concatenate([dlr, dza, dzb_], axis=1)
    dw_in_t = _mm(dzb, h, ta=True, out_dtypes=(BF16,), out_is_w_in_t=True, name="d_w_in")
    dws_t = _mm(dzs, h, ta=True, out_dtypes=(BF16,), name="d_w_in_gates")
    by_chip = _fill_gate_rows(dw_in_t, dws_t).reshape(N_DEV // 2, 2, D_IN_SHARD, D_MODEL)
    from_sibling = _exchange(_pair_rider(by_chip), "pair_d_w_in")
    mine = lax.dynamic_index_in_dim(by_chip, lax.axis_index("c"), axis=1, keepdims=False)
    chip_sums = _add_blocks(mine, from_sibling)
    dh_gates = _mm(dzs, w["ws_t"], name="d_h_mix_gates")
    dh, recv_in = _mm(dzb, w["w_in_t"], b_is_w_in_t=True, epilogue=_add_residual, extras=(dh_gates,), name="d_h_mix",
                      rider=_chip_scatter_rider(chip_sums))
    gx, dg_mix = _rms_bwd(x0, w["g_mix"], dh, dx1, "rms_mix_bwd")

    dgla_w2 = dw2[:, :GLA_LOWRANK, :].transpose(1, 0, 2).reshape(GLA_LOWRANK, GLA_QK)
    return dict(
        loss=loss[0, 0], grad_x=gx.reshape(x.shape), recv_in=recv_in, recv_up=recv_up, recv_rows=recv_rows,
        w_ple_proj=dw_pp,
        g_mix=dg_mix, gla_b=dgla_b.reshape(1, GLA_QK), gla_norm=dgla_norm, dn_norm=ddn, g_mlp=dg_mlp, g_ple=dg_ple,
        g_final=dg_final, gla_w2=dgla_w2, dn_conv=dconv,
        a_log_lanes=dal.reshape(DN_HEADS, LANE), dt_bias_lanes=ddt.reshape(DN_HEADS, LANE))


def _first_weights(g_mix, w_in, gla_w2, gla_b, gla_norm, dn_conv, dn_a_log, dn_dt_bias, dn_norm, g_mlp, g_ple,
                   w_ple_proj, g_final):
    w_in_t = _all_gather(jnp.swapaxes(w_in[0], 0, 1).astype(BF16), "gather_w_in").reshape(D_IN, D_MODEL)
    w_pp = _all_gather(w_ple_proj[0].astype(BF16), "gather_w_ple_proj")
    small = _all_gather(_pack_rows([gla_w2[0], dn_conv[0]]), "gather_w_small")
    n_w2 = GLA_LOWRANK * GLA_QK // N_DEV // LANE
    n_cv = DN_CONV * DN_QKV // N_DEV // LANE
    w2 = small[:, :n_w2].reshape(N_DEV, GLA_LOWRANK, GLA_QK // N_DEV).transpose(1, 0, 2).reshape(GLA_LOWRANK, GLA_QK)
    conv = small[:, n_w2:n_w2 + n_cv].reshape(N_DEV, DN_CONV, DN_QKV // N_DEV).transpose(1, 0, 2).reshape(DN_CONV, DN_QKV)

    return dict(
        w_in_t=w_in_t, ws_t=_w_in_gate_rows(w_in_t), w_pp=_from_cols(w_pp),
        w2p=jnp.pad(w2, ((0, LANE - GLA_LOWRANK), (0, 0))), conv=conv,
        alog_b=jnp.broadcast_to(dn_a_log[0][:, None, None], (DN_HEADS, 1, LANE)),
        dtb_b=jnp.broadcast_to(dn_dt_bias[0][:, None, None], (DN_HEADS, 1, LANE)),
        g_mix=g_mix, gla_b=gla_b, gla_norm=gla_norm, dn_norm=dn_norm, g_mlp=g_mlp, g_ple=g_ple,
        g_final=g_final.reshape(1, D_MODEL))


def kernel(x, p, g_mix, w_in, gla_w2, gla_b, gla_norm, dn_conv, dn_a_log, dn_dt_bias, dn_norm, w_out, g_mlp, w_up, w_down, g_ple, w_ple_gate, w_ple_proj, g_final, loss_target, m_g_mix, m_w_in, m_gla_w2, m_gla_b, m_gla_norm, m_dn_conv, m_dn_a_log, m_dn_dt_bias, m_dn_norm, m_w_out, m_g_mlp, m_w_up, m_w_down, m_g_ple, m_w_ple_gate, m_w_ple_proj, m_g_final, v_g_mix, v_w_in, v_gla_w2, v_gla_b, v_gla_norm, v_dn_conv, v_dn_a_log, v_dn_dt_bias, v_dn_norm, v_w_out, v_g_mlp, v_w_up, v_w_down, v_g_ple, v_w_ple_gate, v_w_ple_proj, v_g_final):
    names = ["g_mix", "w_in", "gla_w2", "gla_b", "gla_norm", "dn_conv", "dn_a_log", "dn_dt_bias", "dn_norm", "w_out",
             "g_mlp", "w_up", "w_down", "g_ple", "w_ple_gate", "w_ple_proj", "g_final"]
    ws = dict(zip(names, (g_mix, w_in, gla_w2, gla_b, gla_norm, dn_conv, dn_a_log, dn_dt_bias, dn_norm, w_out, g_mlp,
                          w_up, w_down, g_ple, w_ple_gate, w_ple_proj, g_final)))
    ms = dict(zip(names, (m_g_mix, m_w_in, m_gla_w2, m_gla_b, m_gla_norm, m_dn_conv, m_dn_a_log, m_dn_dt_bias,
                          m_dn_norm, m_w_out, m_g_mlp, m_w_up, m_w_down, m_g_ple, m_w_ple_gate, m_w_ple_proj,
                          m_g_final)))
    vs = dict(zip(names, (v_g_mix, v_w_in, v_gla_w2, v_gla_b, v_gla_norm, v_dn_conv, v_dn_a_log, v_dn_dt_bias,
                          v_dn_norm, v_w_out, v_g_mlp, v_w_up, v_w_down, v_g_ple, v_w_ple_gate, v_w_ple_proj,
                          v_g_final)))
    me = 4 * lax.axis_index("x") + 2 * lax.axis_index("y") + lax.axis_index("c")

    first = _first_weights(g_mix, w_in, gla_w2, gla_b, gla_norm, dn_conv, dn_a_log, dn_dt_bias, dn_norm, g_mlp,
                           g_ple, w_ple_proj, g_final)
    rows_shard = jnp.concatenate([w_out[0], w_ple_gate[0], w_down[0]], axis=0).astype(BF16)
    r = _local_step(x, p[0], loss_target, first, w_up[0].astype(BF16), rows_shard)
    loss = lax.psum(r["loss"], ("x", "y", "c"))

    grads, deltas, new_m, new_v = {}, {}, {}, {}

    def big(name, parts, row0=0):
        g, d, nm, nv = _adamw_reduce(ws[name][0], ms[name][0], vs[name][0], parts, row0, "adamw_" + name)
        grads[name], deltas[name], new_m[name], new_v[name] = g[None], d[None], nm[None], nv[None]

    t_outs = _adamw_reduce(*[jnp.swapaxes(d["w_in"][0], 0, 1) for d in (ws, ms, vs)], r["recv_in"], 0, "adamw_w_in")
    grads["w_in"], deltas["w_in"], new_m["w_in"], new_v["w_in"] = [jnp.swapaxes(o, 0, 1)[None] for o in t_outs]
    big("w_up", r["recv_up"])
    big("w_ple_proj", _all_to_all(_by_cols(r["w_ple_proj"]), "scatter_d_w_ple_proj"))
    big("w_out", r["recv_rows"], 0)
    big("w_ple_gate", r["recv_rows"], ROWS_OUT)
    big("w_down", r["recv_rows"], 2 * ROWS_OUT)

    vec_names = ["g_mix", "gla_b", "gla_norm", "dn_norm", "g_mlp", "g_ple", "g_final"]
    packed = _pack_rows([r[n] for n in vec_names] + [r["gla_w2"], r["dn_conv"]])
    lane_rows = packed.shape[0]
    packed = jnp.concatenate([packed, r["a_log_lanes"], r["dt_bias_lanes"]], axis=0)
    total = _small_reduce(_all_gather(packed, "gather_small_grads"), lane_rows)
    parts = _unpack_rows(total, [r[n].shape for n in vec_names] + [r["gla_w2"].shape, r["dn_conv"].shape])
    sg = dict(zip(vec_names, parts[:len(vec_names)]))
    sg["g_final"] = sg["g_final"].reshape(D_MODEL)
    sg["gla_w2"] = lax.dynamic_slice_in_dim(parts[-2], me * (GLA_QK // N_DEV), GLA_QK // N_DEV, axis=1)
    sg["dn_conv"] = lax.dynamic_slice_in_dim(parts[-1], me * (DN_QKV // N_DEV), DN_QKV // N_DEV, axis=1)
    sg["dn_a_log"] = total[lane_rows:lane_rows + DN_HEADS, 0]
    sg["dn_dt_bias"] = total[lane_rows + DN_HEADS:lane_rows + 2 * DN_HEADS, 0]
    small_names = vec_names + ["gla_w2", "dn_conv", "dn_a_log", "dn_dt_bias"]
    shapes = [ws[n].shape for n in small_names]
    d_s, m_s, v_s = _adamw_small(_pack_rows([ws[n] for n in small_names]), _pack_rows([ms[n] for n in small_names]),
                                 _pack_rows([vs[n] for n in small_names]), _pack_rows([sg[n] for n in small_names]))
    for n, d, nm, nv in zip(small_names, _unpack_rows(d_s, shapes), _unpack_rows(m_s, shapes), _unpack_rows(v_s, shapes)):
        grads[n], deltas[n], new_m[n], new_v[n] = sg[n].reshape(ws[n].shape), d, nm, nv

    return (loss, r["grad_x"], *[grads[n] for n in names], *[deltas[n] for n in names],
            *[new_m[n] for n in names], *[new_v[n] for n in names])
```

```python
import functools
import math

import jax
import jax.numpy as jnp
from jax import lax
from jax.experimental import pallas as pl
from jax.experimental.pallas import tpu as pltpu

F32 = jnp.float32
BF16 = jnp.bfloat16

N_DEV = 8
D_MODEL = 2048
CHUNK = 64
PLE_DIM = 256
EPS = 1e-6
GLA_HEADS = 4
GLA_DK = 256
GLA_DV = 512
GLA_LOWRANK = 16
GLA_TAU = 16.0
DN_HEADS = 16
DN_D = 128
DN_CONV = 4
D_FF = 4 * D_MODEL
GLA_QK = GLA_HEADS * GLA_DK
GLA_V = GLA_HEADS * GLA_DV
DN_QKV = 3 * DN_HEADS * DN_D
D_IN = 2 * GLA_QK + 2 * GLA_V + GLA_LOWRANK + DN_QKV + D_MODEL + 2 * DN_HEADS + 2 * D_MODEL
D_IN_SHARD = D_IN // N_DEV

ADAM_LR = 0.001
ADAM_B1 = 0.9
ADAM_B2 = 0.999
ADAM_EPS = 1e-08
ADAM_WD = 0.01
ADAM_STEP = 10

LANE = 128
ZB_GQ, ZB_GK, ZB_GV, ZB_GG = 0, 1024, 2048, 4096
ZB_DQKV, ZB_DZ, ZB_GA, ZB_GB = 6144, 12288, 14336, 16384
ZB_W = 18432
ZS_LR, ZS_DA, ZS_DB = 0, 128, 256
ZS_W = 384
WI_LR = 2 * GLA_QK + 2 * GLA_V
WI_DQKV = WI_LR + GLA_LOWRANK
WI_DA = WI_DQKV + DN_QKV + D_MODEL
WI_DB = WI_DA + DN_HEADS
WI_GA = WI_DB + DN_HEADS

VMEM_LIMIT = 56 * 1024 * 1024

def _bdot(a, b, dims):
    return lax.dot_general(a.astype(BF16), b.astype(BF16), dims, preferred_element_type=F32)


def _split3(x):
    hi = x.astype(BF16)
    rest = x - hi.astype(F32)
    mid = rest.astype(BF16)
    return hi, mid, (rest - mid.astype(F32)).astype(BF16)


def _dot01(x, m, dims, x_first):
    m = m.astype(BF16)
    out = None
    for piece in _split3(x):
        d = lax.dot_general(piece, m, dims, preferred_element_type=F32) if x_first else \
            lax.dot_general(m, piece, dims, preferred_element_type=F32)
        out = d if out is None else out + d
    return out


@functools.partial(jax.custom_vjp, nondiff_argnums=(2, 3))
def _pick_dot(x, m, dims, dims_t):
    return _dot01(x, m, dims, True)


def _pick_dot_fwd(x, m, dims, dims_t):
    return _dot01(x, m, dims, True), m


def _pick_dot_bwd(dims, dims_t, m, ct):
    return _dot01(ct, m, dims_t, True), jnp.zeros_like(m)


_pick_dot.defvjp(_pick_dot_fwd, _pick_dot_bwd)


@functools.partial(jax.custom_vjp, nondiff_argnums=(2, 3))
def _left_dot(m, x, dims, dims_t):
    return _dot01(x, m, dims, False)


def _left_dot_fwd(m, x, dims, dims_t):
    return _dot01(x, m, dims, False), m


def _left_dot_bwd(dims, dims_t, m, ct):
    return jnp.zeros_like(m), _dot01(ct, m, dims_t, False)


_left_dot.defvjp(_left_dot_fwd, _left_dot_bwd)


def _dot3(a, b, dims):
    ah, bh = a.astype(BF16), b.astype(BF16)
    al, bl = (a - ah.astype(F32)).astype(BF16), (b - bh.astype(F32)).astype(BF16)
    dot = functools.partial(lax.dot_general, dimension_numbers=dims, preferred_element_type=F32)
    return dot(ah, bh) + (dot(ah, bl) + dot(al, bh))


def _sigmoid(x):
    return 1.0 / (1.0 + jnp.exp(-x))


def _silu(x):
    return x * _sigmoid(x)


def _softplus(x):
    return jnp.maximum(x, 0.0) + jnp.log(1.0 + jnp.exp(-jnp.abs(x)))


def _iota2(shape, dim):
    return lax.broadcasted_iota(jnp.int32, shape, dim)


def _cparams(sem=None):
    return pltpu.CompilerParams(dimension_semantics=sem, vmem_limit_bytes=VMEM_LIMIT)


BNN = (((2,), (1,)), ((0,), (0,)))
BNT = (((2,), (2,)), ((0,), (0,)))
BTN = (((1,), (1,)), ((0,), (0,)))


def _gla_chunk(st, q, k, v, lr, w2, b, gg, ga, gn):
    hb, c, _ = q.shape
    incl = (_iota2((c, c), 0) >= _iota2((c, c), 1))[None]
    tri = jnp.broadcast_to(incl.astype(F32), (hb, c, c))
    q = q.astype(F32) * (GLA_DK ** -0.5)
    k = k.astype(F32)
    v = v.astype(F32)
    lr_b = jnp.broadcast_to(lr[None], (hb,) + lr.shape)
    lf = -_softplus(-(_bdot(lr_b, w2, BNN) + b)) / GLA_TAU
    bcum = _left_dot(tri, lf, BNN, BTN)
    blast = jnp.sum(lf, axis=1, keepdims=True)
    q_in = q * jnp.exp(bcum)
    k_in = k * jnp.exp(-bcum)
    a = jnp.where(incl, _bdot(q_in, k_in, BNT), 0.0)
    o = _bdot(a, v, BNN) + _bdot(q_in, st, BNT)
    k_dec = k * jnp.exp(blast - bcum)
    st_new = st * jnp.exp(blast) + _bdot(v, k_dec, BTN)
    on = o * lax.rsqrt(jnp.mean(o * o, axis=-1, keepdims=True) + EPS) * gn
    res = _sigmoid(ga.astype(F32)) * on * _silu(gg.astype(F32))
    return res, st_new


def _tri_inv_raw(a):
    _, c, _ = a.shape
    eye = (_iota2((c, c), 0) == _iota2((c, c), 1)).astype(F32)[None]
    x = a
    p = eye - a
    for _ in range(5):
        x = _dot3(x, x, BNN)
        p = _dot3(p, eye + x, BNN)
    return p


def _tri_inv_bwd(t, dt):
    return (-_dot3(_dot3(t, dt, BTN), t, BNT),)


@jax.custom_vjp
def _tri_inv_given(a, t):
    return t


def _tri_inv_given_fwd(a, t):
    return t, t


def _tri_inv_given_bwd(t, dt):
    return _tri_inv_bwd(t, dt) + (jnp.zeros_like(t),)


_tri_inv_given.defvjp(_tri_inv_given_fwd, _tri_inv_given_bwd)


@functools.partial(jax.custom_vjp, nondiff_argnums=(1,))
def _column_on_lanes(z, j):
    picked = jnp.where(_iota2(z.shape, 1) == j, z, 0.0)
    return jnp.broadcast_to(jnp.sum(picked, axis=1, keepdims=True), z.shape)


def _column_on_lanes_fwd(z, j):
    return _column_on_lanes(z, j), None


def _column_on_lanes_bwd(j, _, ct):
    total = jnp.broadcast_to(jnp.sum(ct, axis=1, keepdims=True), ct.shape)
    return (jnp.where(_iota2(ct.shape, 1) == j, total, 0.0),)


_column_on_lanes.defvjp(_column_on_lanes_fwd, _column_on_lanes_bwd)


def _dn_chunk(s, qr, kr, vr, za, zb, alog, dtb, gz, gb, dn, t_saved=None):
    hb, c, _ = qr.shape
    row, col = _iota2((c, c), 0), _iota2((c, c), 1)
    incl = (row >= col)[None]
    strict = (row > col)[None]
    tri = jnp.broadcast_to(incl.astype(F32), (hb, c, c))
    eye = (row == col).astype(F32)[None]
    ones_cc = jnp.ones((hb, c, c), F32)
    lane0 = (lax.broadcasted_iota(jnp.int32, (hb, LANE, c), 1) == 0).astype(F32)

    def l2n(t):
        return t * lax.rsqrt(jnp.sum(t * t, axis=-1, keepdims=True) + EPS)

    q = l2n(qr.astype(F32)) * (DN_D ** -0.5)
    k = l2n(kr.astype(F32))
    v = vr.astype(F32)
    za_b = jnp.concatenate([_column_on_lanes(za, j)[None] for j in range(hb)], axis=0)
    zb_b = jnp.concatenate([_column_on_lanes(zb, j)[None] for j in range(hb)], axis=0)
    g = -jnp.exp(alog) * _softplus(za_b + dtb)
    beta = _sigmoid(zb_b)
    gcum = _left_dot(tri, g, BNN, BTN)
    glast = jnp.sum(g, axis=1, keepdims=True)
    cm = _pick_dot(gcum, lane0, BNN, BNT)
    rm = _left_dot(ones_cc, cm * eye, BNN, BTN)
    dec = jnp.exp(jnp.where(incl, cm - rm, -1e30))
    kb = k * beta
    a = jnp.where(strict, _bdot(kb, k, BNT) * dec, 0.0)
    t = _tri_inv_raw(a) if t_saved is None else _tri_inv_given(a, t_saved)
    egc = jnp.exp(gcum)
    u = _bdot(t, v * beta, BNN)
    w = _bdot(t, kb * egc, BNN)
    attn = jnp.where(incl, _bdot(q, k, BNT) * dec, 0.0)
    q_dec = q * egc
    k_dec = k * jnp.exp(glast - gcum)
    v_new = u - _bdot(w, s, BNN)
    o = _bdot(q_dec, s, BNN) + _bdot(attn, v_new, BNN)
    s_new = s * jnp.exp(glast) + _bdot(k_dec, v_new, BTN)
    on = o * lax.rsqrt(jnp.mean(o * o, axis=-1, keepdims=True) + EPS) * dn
    res = _sigmoid(gb.astype(F32)) * on * _silu(gz.astype(F32))
    return (res, s_new, t) if t_saved is None else (res, s_new)


def _heads(ref, n_heads, width):
    return jnp.stack([ref[:, j * width:(j + 1) * width] for j in range(n_heads)], axis=0)


def _gla_specs(nc, reverse):
    def rows(b, n):
        return b * nc + ((nc - 1 - n) if reverse else n)

    qk = lambda base: pl.BlockSpec((CHUNK, GLA_QK), lambda b, n: (rows(b, n), base // GLA_QK))
    vv = lambda base: pl.BlockSpec((CHUNK, GLA_V), lambda b, n: (rows(b, n), base // GLA_V))
    lr = lambda c: pl.BlockSpec((CHUNK, LANE), lambda b, n: (rows(b, n), c))
    full = lambda shape: pl.BlockSpec(shape, lambda b, n: (0,) * len(shape))
    return rows, qk, vv, lr, full


def _gla_inputs(q_ref, k_ref, v_ref, gg_ref, ga_ref, lr_ref, w2_ref, b_ref, gn_ref):
    return (_heads(q_ref, GLA_HEADS, GLA_DK), _heads(k_ref, GLA_HEADS, GLA_DK), _heads(v_ref, GLA_HEADS, GLA_DV),
            lr_ref[...], _heads(w2_ref, GLA_HEADS, GLA_DK), _heads(b_ref, GLA_HEADS, GLA_DK),
            _heads(gg_ref, GLA_HEADS, GLA_DV), _heads(ga_ref, GLA_HEADS, GLA_DV), gn_ref[...])


def _gla_fwd(zb, zs, w2p, gla_b, gla_norm, bsz, nc):
    t = zb.shape[0]
    rows, qk, vv, lr, full = _gla_specs(nc, False)

    def body(q_ref, k_ref, v_ref, gg_ref, ga_ref, lr_ref, w2_ref, b_ref, gn_ref, o_ref, st_ref, state):
        st = jnp.where(pl.program_id(1) > 0, state[...], 0.0)
        st_ref[0, 0] = st.astype(st_ref.dtype)
        res, st_new = _gla_chunk(st, *_gla_inputs(q_ref, k_ref, v_ref, gg_ref, ga_ref, lr_ref, w2_ref, b_ref, gn_ref))
        for j in range(GLA_HEADS):
            o_ref[:, j * GLA_DV:(j + 1) * GLA_DV] = res[j]
        state[...] = st_new

    return pl.pallas_call(
        body, name="gla_fwd", grid=(bsz, nc),
        in_specs=[qk(ZB_GQ), qk(ZB_GK), vv(ZB_GV), vv(ZB_GG), vv(ZB_GA), lr(ZS_LR // LANE),
                  full((LANE, GLA_QK)), full((1, GLA_QK)), full((1, GLA_DV))],
        out_specs=[vv(0), pl.BlockSpec((1, 1, GLA_HEADS, GLA_DV, GLA_DK), lambda b, n: (b, n, 0, 0, 0))],
        out_shape=[jax.ShapeDtypeStruct((t, GLA_V), F32),
                   jax.ShapeDtypeStruct((bsz, nc, GLA_HEADS, GLA_DV, GLA_DK), BF16)],
        scratch_shapes=[pltpu.VMEM((GLA_HEADS, GLA_DV, GLA_DK), F32)],
        compiler_params=_cparams(("arbitrary", "arbitrary")),
    )(zb, zb, zb, zb, zb, zs, w2p, gla_b, gla_norm)


def _gla_bwd(zb, zs, w2p, gla_b, gla_norm, states, dmixed, bsz, nc, rider=None):
    t = zb.shape[0]
    rows, qk, vv, lr, full = _gla_specs(nc, True)

    def body(q_ref, k_ref, v_ref, gg_ref, ga_ref, lr_ref, w2_ref, b_ref, gn_ref, st_ref, dm_ref,
             dq_ref, dk_ref, dv_ref, dgg_ref, dga_ref, dlr_ref, dw2_ref, db_ref, dgn_ref, dstate):
        b, n = pl.program_id(0), pl.program_id(1)

        @pl.when((b == 0) & (n == 0))
        def _():
            dw2_ref[...] = jnp.zeros_like(dw2_ref)
            db_ref[...] = jnp.zeros_like(db_ref)
            dgn_ref[...] = jnp.zeros_like(dgn_ref)

        _, vjp = jax.vjp(_gla_chunk, st_ref[0, 0].astype(F32),
                         *_gla_inputs(q_ref, k_ref, v_ref, gg_ref, ga_ref, lr_ref, w2_ref, b_ref, gn_ref))
        dst_in = jnp.where(n > 0, dstate[...], 0.0)
        dst, dq, dk, dv, dlr, dw2, db, dgg, dga, dgn = vjp((_heads(dm_ref, GLA_HEADS, GLA_DV).astype(F32), dst_in))
        dstate[...] = dst
        for j in range(GLA_HEADS):
            dq_ref[:, j * GLA_DK:(j + 1) * GLA_DK] = dq[j].astype(dq_ref.dtype)
            dk_ref[:, j * GLA_DK:(j + 1) * GLA_DK] = dk[j].astype(dk_ref.dtype)
            dv_ref[:, j * GLA_DV:(j + 1) * GLA_DV] = dv[j].astype(dv_ref.dtype)
            dgg_ref[:, j * GLA_DV:(j + 1) * GLA_DV] = dgg[j].astype(dgg_ref.dtype)
            dga_ref[:, j * GLA_DV:(j + 1) * GLA_DV] = dga[j].astype(dga_ref.dtype)
        dlr_ref[...] = dlr
        dw2_ref[...] += dw2
        db_ref[...] += db
        dgn_ref[...] += dgn

    return _hosted_call(
        body, rider, name="gla_bwd", grid=(bsz, nc),
        in_specs=[qk(ZB_GQ), qk(ZB_GK), vv(ZB_GV), vv(ZB_GG), vv(ZB_GA), lr(ZS_LR // LANE),
                  full((LANE, GLA_QK)), full((1, GLA_QK)), full((1, GLA_DV)),
                  pl.BlockSpec((1, 1, GLA_HEADS, GLA_DV, GLA_DK), lambda b, n: (b, nc - 1 - n, 0, 0, 0)),
                  vv(0)],
        out_specs=[qk(0), qk(0), vv(0), vv(0), vv(0), lr(0),
                   full((GLA_HEADS, LANE, GLA_DK)), full((GLA_HEADS, 1, GLA_DK)), full((1, GLA_DV))],
        out_shape=[jax.ShapeDtypeStruct((t, GLA_QK), BF16), jax.ShapeDtypeStruct((t, GLA_QK), BF16),
                   jax.ShapeDtypeStruct((t, GLA_V), BF16), jax.ShapeDtypeStruct((t, GLA_V), BF16),
                   jax.ShapeDtypeStruct((t, GLA_V), BF16), jax.ShapeDtypeStruct((t, LANE), F32),
                   jax.ShapeDtypeStruct((GLA_HEADS, LANE, GLA_DK), F32),
                   jax.ShapeDtypeStruct((GLA_HEADS, 1, GLA_DK), F32),
                   jax.ShapeDtypeStruct((1, GLA_DV), F32)],
        scratch_shapes=[pltpu.VMEM((GLA_HEADS, GLA_DV, GLA_DK), F32)],
        args=(zb, zb, zb, zb, zb, zs, w2p, gla_b, gla_norm, states, dmixed))


DN_HB = DN_HEADS


def _dn_specs(nc, reverse):
    wide = DN_HB * DN_D

    def rows(b, n, h):
        return b * nc + ((nc - 1 - n) if reverse else n)

    def col(base):
        return pl.BlockSpec((CHUNK, wide), lambda b, n, h: (rows(b, n, h), base // wide + h))

    def fixed(c):
        return pl.BlockSpec((CHUNK, LANE), lambda b, n, h: (rows(b, n, h), c))

    head = pl.BlockSpec((DN_HB, 1, LANE), lambda b, n, h: (h, 0, 0))
    return rows, col, fixed, head


def _lanes(j):
    return slice(j * DN_D, (j + 1) * DN_D)


def _by_head(ref):
    return jnp.stack([ref[:, _lanes(j)] for j in range(DN_HB)], axis=0)


def _dn_fwd(act, zb, zs, alog_b, dtb_b, dn_norm, mix_gla, bsz, nc, rider=None):
    t = zb.shape[0]
    rows, col, fixed, head = _dn_specs(nc, False)

    def body(q_ref, k_ref, v_ref, za_ref, zb_ref, al_ref, dt_ref, gz_ref, gb_ref, dn_ref, mg_ref,
             o_ref, st_ref, ti_ref, state):
        s = jnp.where(pl.program_id(1) > 0, state[...], 0.0)
        st_ref[0, 0] = s
        res, s_new, t_inv = _dn_chunk(s, _by_head(q_ref), _by_head(k_ref), _by_head(v_ref), za_ref[...], zb_ref[...],
                                      al_ref[...], dt_ref[...], _by_head(gz_ref), _by_head(gb_ref), dn_ref[...])
        ti_ref[0, 0] = t_inv
        for j in range(DN_HB):
            o_ref[:, _lanes(j)] = (res[j] + mg_ref[:, _lanes(j)]).astype(o_ref.dtype)
        state[...] = s_new

    return _hosted_call(
        body, rider, name="dn_fwd", grid=(bsz, nc, DN_HEADS // DN_HB),
        in_specs=[col(0), col(DN_HEADS * DN_D), col(2 * DN_HEADS * DN_D),
                  fixed(ZS_DA // LANE), fixed(ZS_DB // LANE), head, head,
                  col(ZB_DZ), col(ZB_GB), pl.BlockSpec((1, DN_D), lambda b, n, h: (0, 0)), col(0)],
        out_specs=[col(0), pl.BlockSpec((1, 1, DN_HB, DN_D, DN_D), lambda b, n, h: (b, n, h, 0, 0)),
                   pl.BlockSpec((1, 1, DN_HB, CHUNK, CHUNK), lambda b, n, h: (b, n, h, 0, 0))],
        out_shape=[jax.ShapeDtypeStruct((t, D_MODEL), BF16),
                   jax.ShapeDtypeStruct((bsz, nc, DN_HEADS, DN_D, DN_D), F32),
                   jax.ShapeDtypeStruct((bsz, nc, DN_HEADS, CHUNK, CHUNK), F32)],
        scratch_shapes=[pltpu.VMEM((DN_HEADS, DN_D, DN_D), F32)],
        args=(act, act, act, zs, zs, alog_b, dtb_b, zb, zb, dn_norm, mix_gla))


def _dn_bwd(act, zb, zs, alog_b, dtb_b, dn_norm, states, t_invs, dmixed, bsz, nc, rider=None):
    t = zb.shape[0]
    rows, col, fixed, head = _dn_specs(nc, True)

    def body(q_ref, k_ref, v_ref, za_ref, zb_ref, al_ref, dt_ref, gz_ref, gb_ref, dn_ref, st_ref, ti_ref, dm_ref,
             dact_ref, dza_ref, dzb_ref, dgz_ref, dgb_ref, dal_ref, ddt_ref, ddn_ref, dstate):
        b, n = pl.program_id(0), pl.program_id(1)

        @pl.when((b == 0) & (n == 0))
        def _():
            dal_ref[...] = jnp.zeros_like(dal_ref)
            ddt_ref[...] = jnp.zeros_like(ddt_ref)
            ddn_ref[...] = jnp.zeros_like(ddn_ref)

        fn = functools.partial(_dn_chunk, t_saved=ti_ref[0, 0])
        _, vjp = jax.vjp(fn, st_ref[0, 0], _by_head(q_ref), _by_head(k_ref), _by_head(v_ref), za_ref[...],
                         zb_ref[...], al_ref[...], dt_ref[...], _by_head(gz_ref), _by_head(gb_ref), dn_ref[...])
        ds_in = jnp.where(n > 0, dstate[...], 0.0)
        ds, dq, dk, dv, dza, dzb, dal, ddt, dgz, dgb, ddn = vjp((_by_head(dm_ref).astype(F32), ds_in))
        dstate[...] = ds
        for j in range(DN_HB):
            for part, d in enumerate((dq, dk, dv)):
                dact_ref[:, pl.ds(part * DN_HEADS * DN_D + j * DN_D, DN_D)] = d[j]
            dgz_ref[:, _lanes(j)] = dgz[j].astype(dgz_ref.dtype)
            dgb_ref[:, _lanes(j)] = dgb[j].astype(dgb_ref.dtype)
        dal_ref[...] += dal
        ddt_ref[...] += ddt
        dza_ref[...] = dza
        dzb_ref[...] = dzb
        ddn_ref[...] += ddn

    full = lambda shape: pl.BlockSpec(shape, lambda b, n, h: (0,) * len(shape))
    return _hosted_call(
        body, rider, name="dn_bwd", grid=(bsz, nc, DN_HEADS // DN_HB),
        in_specs=[col(0), col(DN_HEADS * DN_D), col(2 * DN_HEADS * DN_D),
                  fixed(ZS_DA // LANE), fixed(ZS_DB // LANE), head, head,
                  col(ZB_DZ), col(ZB_GB), pl.BlockSpec((1, DN_D), lambda b, n, h: (0, 0)),
                  pl.BlockSpec((1, 1, DN_HB, DN_D, DN_D), lambda b, n, h: (b, nc - 1 - n, h, 0, 0)),
                  pl.BlockSpec((1, 1, DN_HB, CHUNK, CHUNK), lambda b, n, h: (b, nc - 1 - n, h, 0, 0)), col(0)],
        out_specs=[pl.BlockSpec((CHUNK, DN_QKV), lambda b, n, h: (rows(b, n, h), 0)), fixed(0), fixed(0), col(0), col(0),
                   full((DN_HEADS, 1, LANE)), full((DN_HEADS, 1, LANE)), full((1, DN_D))],
        out_shape=[jax.ShapeDtypeStruct((t, DN_QKV), F32),
                   jax.ShapeDtypeStruct((t, LANE), F32), jax.ShapeDtypeStruct((t, LANE), F32),
                   jax.ShapeDtypeStruct((t, D_MODEL), BF16), jax.ShapeDtypeStruct((t, D_MODEL), BF16),
                   jax.ShapeDtypeStruct((DN_HEADS, 1, LANE), F32), jax.ShapeDtypeStruct((DN_HEADS, 1, LANE), F32),
                   jax.ShapeDtypeStruct((1, DN_D), F32)],
        scratch_shapes=[pltpu.VMEM((DN_HEADS, DN_D, DN_D), F32)],
        args=(act, act, act, zs, zs, alog_b, dtb_b, zb, zb, dn_norm, states, t_invs, dmixed))


MM_VMEM_BUDGET = 40 * 1024 * 1024
MM_TILE_PREF = (1024, 1024, 2048)


def _divisor_tile(n, cap):
    if n <= cap:
        return n
    for c in range(cap - cap % LANE, 0, -LANE):
        if n % c == 0:
            return c
    return n


def _mm_tiles(m, n, kd, a_bytes, b_bytes, mn_bytes):
    tm, tn, tk = (_divisor_tile(d, c) for d, c in zip((m, n, kd), MM_TILE_PREF))

    def need(tm, tn, tk):
        acc = 0 if tk == kd else 4 * tm * tn
        return 2 * (tm * tk * a_bytes + tk * tn * b_bytes + tm * tn * mn_bytes) + acc + 4 * tm * tn

    while need(tm, tn, tk) > MM_VMEM_BUDGET:
        if tk > 512 and tk * max(tm * a_bytes, tn * b_bytes) >= tm * tn * mn_bytes:
            tk = _divisor_tile(kd, tk // 2)
        elif tn >= tm and tn > LANE:
            tn = _divisor_tile(n, tn // 2)
        else:
            tm = _divisor_tile(m, tm // 2)
    return tm, tn, tk


def _w_in_row_of(tile, tile_rows):
    skipped = jnp.where(tile >= ZB_GA // tile_rows, WI_GA - ZB_GA, jnp.where(tile >= ZB_DQKV // tile_rows,
                                                                             WI_DQKV - ZB_DQKV, 0))
    return pl.multiple_of(tile * tile_rows + skipped, 16)


def _mm(a, b, *, ta=False, tb=False, out_dtypes=(F32,), epilogue=None, extras=(), name, rider=None,
        b_is_w_in_t=False, out_is_w_in_t=False):
    m, kd = (a.shape[1], a.shape[0]) if ta else a.shape
    n = b.shape[0] if tb else b.shape[1]
    if b_is_w_in_t:
        n, kd = (ZB_W, kd) if tb else (n, ZB_W)
    mn_bytes = sum(e.dtype.itemsize for e in extras) + sum(jnp.dtype(dt).itemsize for dt in out_dtypes)
    tm, tn, tk = _mm_tiles(m, n, kd, a.dtype.itemsize, b.dtype.itemsize, mn_bytes)
    nk = kd // tk
    n_ex = len(extras)
    dims = (((0,) if ta else (1,), (1,) if tb else (0,)), ((), ()))

    def finish(acc, ex_refs, out_refs):
        outs = (acc,) if epilogue is None else epilogue(acc, *[r[...] for r in ex_refs])
        for r, o in zip(out_refs, outs):
            r[...] = o.astype(r.dtype)

    def partial_product(a_ref, b_ref):
        return lax.dot_general(a_ref[...].astype(BF16), b_ref[...].astype(BF16), dims, preferred_element_type=F32)

    def body_single(*refs):
        finish(partial_product(refs[0], refs[1]), refs[2:2 + n_ex], refs[2 + n_ex:])

    def body_acc(*refs):
        acc = refs[-1]
        k = pl.program_id(2)

        @pl.when(k == 0)
        def _():
            acc[...] = partial_product(refs[0], refs[1])

        @pl.when(k > 0)
        def _():
            acc[...] += partial_product(refs[0], refs[1])

        @pl.when(k == nk - 1)
        def _():
            finish(acc[...], refs[2:2 + n_ex], refs[2 + n_ex:-1])

    a_spec = pl.BlockSpec((tk, tm), lambda i, j, k: (k, i)) if ta else pl.BlockSpec((tm, tk), lambda i, j, k: (i, k))
    b_spec = pl.BlockSpec((tn, tk), lambda i, j, k: (j, k)) if tb else pl.BlockSpec((tk, tn), lambda i, j, k: (k, j))
    mn_spec = pl.BlockSpec((tm, tn), lambda i, j, k: (i, j))
    out_spec, out_rows = mn_spec, m
    if b_is_w_in_t and tb:
        b_spec = pl.BlockSpec((pl.Element(tn), pl.Element(tk)),
                              lambda i, j, k: (_w_in_row_of(j, tn), pl.multiple_of(k * tk, LANE)))
    elif b_is_w_in_t:
        b_spec = pl.BlockSpec((pl.Element(tk), pl.Element(tn)),
                              lambda i, j, k: (_w_in_row_of(k, tk), pl.multiple_of(j * tn, LANE)))
    if out_is_w_in_t:
        out_spec, out_rows = pl.BlockSpec((pl.Element(tm), pl.Element(tn)),
                                          lambda i, j, k: (_w_in_row_of(i, tm), pl.multiple_of(j * tn, LANE))), D_IN
    outs = _hosted_call(
        body_single if nk == 1 else body_acc, rider, name=name, grid=(m // tm, n // tn, nk),
        in_specs=[a_spec, b_spec] + [mn_spec] * n_ex,
        out_specs=[out_spec] * len(out_dtypes),
        out_shape=[jax.ShapeDtypeStruct((out_rows, n), dt) for dt in out_dtypes],
        scratch_shapes=[] if nk == 1 else [pltpu.VMEM((tm, tn), F32)],
        args=(a, b, *extras))
    return outs[0] if len(outs) == 1 else outs


ROW_BLOCK = 256


def _row_spec(width=D_MODEL):
    return pl.BlockSpec((ROW_BLOCK, width), lambda i: (i, 0))


def _vec_spec(width=D_MODEL):
    return pl.BlockSpec((1, width), lambda i: (0, 0))


def _rms_fwd(x, g, name):
    def body(x_ref, g_ref, h_ref):
        xf = x_ref[...]
        h_ref[...] = (xf * lax.rsqrt(jnp.mean(xf * xf, axis=-1, keepdims=True) + EPS) * g_ref[...]).astype(BF16)

    t = x.shape[0]
    return pl.pallas_call(
        body, name=name, grid=(t // ROW_BLOCK,), in_specs=[_row_spec(), _vec_spec()], out_specs=_row_spec(),
        out_shape=jax.ShapeDtypeStruct((t, D_MODEL), BF16), compiler_params=_cparams(("parallel",)),
    )(x, g)


def _rms_bwd_math(xf, g, dh):
    rstd = lax.rsqrt(jnp.mean(xf * xf, axis=-1, keepdims=True) + EPS)
    xhat = xf * rstd
    dxhat = dh * g
    dx = rstd * (dxhat - xhat * jnp.mean(dxhat * xhat, axis=-1, keepdims=True))
    dg = jnp.sum(dh * xhat, axis=0, keepdims=True)
    return dx, dg


def _rms_bwd(x, g, dh, dres, name):
    def body(x_ref, g_ref, dh_ref, dres_ref, dx_ref, dg_ref):
        dx, dg = _rms_bwd_math(x_ref[...], g_ref[...], dh_ref[...].astype(F32))
        dx_ref[...] = dres_ref[...] + dx

        @pl.when(pl.program_id(0) == 0)
        def _():
            dg_ref[...] = jnp.zeros_like(dg_ref)

        dg_ref[...] += dg

    t = x.shape[0]
    return pl.pallas_call(
        body, name=name, grid=(t // ROW_BLOCK,),
        in_specs=[_row_spec(), _vec_spec(), _row_spec(), _row_spec()], out_specs=[_row_spec(), _vec_spec()],
        out_shape=[jax.ShapeDtypeStruct((t, D_MODEL), F32), jax.ShapeDtypeStruct((1, D_MODEL), F32)],
        compiler_params=_cparams(("arbitrary",)),
    )(x, g, dh, dres)


def _loss_head(x3, g, target):
    def body(x_ref, g_ref, t_ref, dx_ref, dg_ref, loss_ref):
        xf, gg = x_ref[...], g_ref[...]
        rstd = lax.rsqrt(jnp.mean(xf * xf, axis=-1, keepdims=True) + EPS)
        err = xf * rstd * gg - t_ref[...]
        dx, dg = _rms_bwd_math(xf, gg, err * (1.0 / D_MODEL))
        dx_ref[...] = dx

        @pl.when(pl.program_id(0) == 0)
        def _():
            dg_ref[...] = jnp.zeros_like(dg_ref)
            loss_ref[...] = jnp.zeros_like(loss_ref)

        dg_ref[...] += dg
        part = jnp.sum(jnp.sum(err * err, axis=-1, keepdims=True), axis=0, keepdims=True) * (0.5 / D_MODEL)
        loss_ref[...] += jnp.broadcast_to(part, loss_ref.shape)

    t = x3.shape[0]
    return pl.pallas_call(
        body, name="loss_head", grid=(t // ROW_BLOCK,),
        in_specs=[_row_spec(), _vec_spec(), _row_spec()], out_specs=[_row_spec(), _vec_spec(), _vec_spec(LANE)],
        out_shape=[jax.ShapeDtypeStruct((t, D_MODEL), F32), jax.ShapeDtypeStruct((1, D_MODEL), F32),
                   jax.ShapeDtypeStruct((1, LANE), F32)],
        compiler_params=_cparams(("arbitrary",)),
    )(x3, g, target)


def _ple_bwd(dx3, gpre, pp):
    def body(dx_ref, gp_ref, pp_ref, dgp_ref, dpp_ref):
        dx, sg = dx_ref[...], _sigmoid(gp_ref[...])
        dpp_ref[...] = (dx * sg).astype(BF16)
        dgp_ref[...] = (dx * pp_ref[...] * sg * (1.0 - sg)).astype(BF16)

    t = dx3.shape[0]
    return pl.pallas_call(
        body, name="ple_bwd", grid=(t // ROW_BLOCK,), in_specs=[_row_spec()] * 3, out_specs=[_row_spec()] * 2,
        out_shape=[jax.ShapeDtypeStruct((t, D_MODEL), BF16)] * 2, compiler_params=_cparams(("parallel",)),
    )(dx3, gpre, pp)


CONV_COLS = 256


def _shift_down(x, s):
    if s == 0:
        return x
    return jnp.where(_iota2(x.shape, 0) >= s, pltpu.roll(x, s, 0), 0.0)


def _shift_up(x, s):
    if s == 0:
        return x
    rows = x.shape[0]
    return jnp.where(_iota2(x.shape, 0) < rows - s, pltpu.roll(x, rows - s, 0), 0.0)


def _conv_taps(xf):
    return [_shift_down(xf, DN_CONV - 1 - j) for j in range(DN_CONV)]


def _conv_pre(taps, w):
    return sum(tap * w[j:j + 1, :] for j, tap in enumerate(taps))


def _conv_fwd(zb, conv_w, bsz, seq):
    def body(x_ref, w_ref, y_ref):
        y_ref[...] = _silu(_conv_pre(_conv_taps(x_ref[...]), w_ref[...]))

    nblk = DN_QKV // CONV_COLS
    return pl.pallas_call(
        body, name="conv_fwd", grid=(bsz, nblk),
        in_specs=[pl.BlockSpec((seq, CONV_COLS), lambda b, j: (b, ZB_DQKV // CONV_COLS + j)),
                  pl.BlockSpec((DN_CONV, CONV_COLS), lambda b, j: (0, j))],
        out_specs=pl.BlockSpec((seq, CONV_COLS), lambda b, j: (b, j)),
        out_shape=jax.ShapeDtypeStruct((bsz * seq, DN_QKV), F32),
        compiler_params=_cparams(("parallel", "parallel")),
    )(zb, conv_w)


def _conv_bwd(zb, conv_w, dact, bsz, seq):
    def body(x_ref, w_ref, dy_ref, dx_ref, dw_ref):
        taps, w = _conv_taps(x_ref[...]), w_ref[...]
        c = _conv_pre(taps, w)
        sg = _sigmoid(c)
        dc = dy_ref[...].astype(F32) * sg * (1.0 + c * (1.0 - sg))
        dx = sum(_shift_up(dc, DN_CONV - 1 - j) * w[j:j + 1, :] for j in range(DN_CONV))
        dx_ref[...] = dx.astype(BF16)
        dw = jnp.concatenate([jnp.sum(dc * tap, axis=0, keepdims=True) for tap in taps], axis=0)

        @pl.when(pl.program_id(1) == 0)
        def _():
            dw_ref[...] = jnp.zeros_like(dw_ref)

        dw_ref[...] += dw

    nblk = DN_QKV // CONV_COLS
    return pl.pallas_call(
        body, name="conv_bwd", grid=(nblk, bsz),
        in_specs=[pl.BlockSpec((seq, CONV_COLS), lambda j, b: (b, ZB_DQKV // CONV_COLS + j)),
                  pl.BlockSpec((DN_CONV, CONV_COLS), lambda j, b: (0, j)),
                  pl.BlockSpec((seq, CONV_COLS), lambda j, b: (b, j))],
        out_specs=[pl.BlockSpec((seq, CONV_COLS), lambda j, b: (b, j)),
                   pl.BlockSpec((DN_CONV, CONV_COLS), lambda j, b: (0, j))],
        out_shape=[jax.ShapeDtypeStruct((bsz * seq, DN_QKV), BF16), jax.ShapeDtypeStruct((DN_CONV, DN_QKV), F32)],
        compiler_params=_cparams(("parallel", "arbitrary")),
    )(zb, conv_w, dact)


MESH_IDS = pl.DeviceIdType.MESH
ANY_SPEC = pl.BlockSpec(memory_space=pl.ANY)


COMM_SCRATCH = (pltpu.SemaphoreType.DMA((7,)), pltpu.SemaphoreType.DMA((7,)), pltpu.SemaphoreType.DMA)


def _gather_phases(x_ref, out_ref, send_sems, recv_sems, local_sem):
    mx, my, mc = lax.axis_index("x"), lax.axis_index("y"), lax.axis_index("c")
    me, sibling = (mx, my, mc), (mx, my, 1 - mc)
    chips = [(1 - mx, my), (mx, 1 - my), (1 - mx, 1 - my)]

    def slot(px, py, pc):
        return out_ref.at[4 * px + 2 * py + pc]

    def copy(k, block, to, src=None):
        return pltpu.make_async_remote_copy(
            src_ref=slot(*block) if src is None else src, dst_ref=slot(*block),
            send_sem=send_sems.at[k], recv_sem=recv_sems.at[k], device_id=to, device_id_type=MESH_IDS)

    def mine():
        return pltpu.make_async_copy(x_ref, slot(*me), local_sem)

    def first():
        return [copy(0, me, sibling, src=x_ref)] + [copy(1 + j, me, (*chip, mc), src=x_ref)
                                                    for j, chip in enumerate(chips)]

    def passed():
        return [copy(4 + j, (*chip, mc), sibling) for j, chip in enumerate(chips)]

    def start():
        mine().start()
        for cp in first():
            cp.start()

    def forward():
        for j, (chip, cp) in enumerate(zip(chips, passed())):
            copy(1 + j, (*chip, mc), me).wait_recv()
            cp.start()

    def finish():
        copy(0, sibling, me).wait_recv()
        for j, chip in enumerate(chips):
            copy(4 + j, (*chip, 1 - mc), me).wait_recv()
        for cp in first() + passed():
            cp.wait_send()
        mine().wait()

    return start, forward, finish


def _scatter_phases(x_ref, out_ref, send_sems, recv_sems, local_sem, among_chips=False):
    mx, my, mc = lax.axis_index("x"), lax.axis_index("y"), lax.axis_index("c")
    n_peers = 4 if among_chips else N_DEV
    me = 2 * mx + my if among_chips else 4 * mx + 2 * my + mc

    def peer(k):
        if among_chips:
            return (mx ^ ((k >> 1) & 1), my ^ (k & 1), mc)
        return (mx ^ ((k >> 2) & 1), my ^ ((k >> 1) & 1), mc ^ (k & 1))

    def slot_of(k):
        px, py, pc = peer(k)
        return 2 * px + py if among_chips else 4 * px + 2 * py + pc

    def copy(k, src_slot, dst_slot):
        return pltpu.make_async_remote_copy(
            src_ref=x_ref.at[src_slot], dst_ref=out_ref.at[dst_slot],
            send_sem=send_sems.at[k - 1], recv_sem=recv_sems.at[k - 1],
            device_id=peer(k), device_id_type=MESH_IDS)

    def sends():
        return [copy(k, slot_of(k), me) for k in range(1, n_peers)]

    def mine():
        return pltpu.make_async_copy(x_ref.at[me], out_ref.at[me], local_sem)

    def start():
        mine().start()
        for cp in sends():
            cp.start()

    def forward():
        pass

    def finish():
        for k in range(1, n_peers):
            copy(k, me, slot_of(k)).wait_recv()
        for cp in sends():
            cp.wait_send()
        mine().wait()

    return start, forward, finish


def _pair_phases(x_ref, out_ref, send_sems, recv_sems, local_sem):
    mx, my, mc = lax.axis_index("x"), lax.axis_index("y"), lax.axis_index("c")

    def copy(side):
        return pltpu.make_async_remote_copy(
            src_ref=x_ref.at[:, side], dst_ref=out_ref, send_sem=send_sems.at[0], recv_sem=recv_sems.at[0],
            device_id=(mx, my, 1 - mc), device_id_type=MESH_IDS)

    def start():
        copy(1 - mc).start()

    def forward():
        pass

    def finish():
        copy(mc).wait_recv()
        copy(1 - mc).wait_send()

    return start, forward, finish


class _Rider:
    def __init__(self, phases, x, out_shape):
        self.phases, self.x, self.out_shape = phases, x, out_shape


def _gather_rider(x):
    return _Rider(_gather_phases, x, jax.ShapeDtypeStruct((N_DEV,) + x.shape, x.dtype))


def _scatter_rider(x):
    return _Rider(_scatter_phases, x, jax.ShapeDtypeStruct(x.shape, x.dtype))


def _chip_scatter_rider(x):
    return _Rider(functools.partial(_scatter_phases, among_chips=True), x, jax.ShapeDtypeStruct(x.shape, x.dtype))


def _pair_rider(x):
    return _Rider(_pair_phases, x, jax.ShapeDtypeStruct((x.shape[0],) + x.shape[2:], x.dtype))


def _exchange(rider, name):
    def body(x_ref, out_ref, send_sems, recv_sems, local_sem):
        for phase in rider.phases(x_ref, out_ref, send_sems, recv_sems, local_sem):
            phase()

    return pl.pallas_call(body, name=name, out_shape=rider.out_shape, in_specs=[ANY_SPEC], out_specs=ANY_SPEC,
                          scratch_shapes=list(COMM_SCRATCH))(rider.x)


def _all_gather(x, name):
    return _exchange(_gather_rider(x), name)


def _all_to_all(x, name):
    return _exchange(_scatter_rider(x), name)


def _hosted_call(body, rider, *, name, grid, in_specs, out_specs, out_shape, scratch_shapes, args):
    if rider is None:
        return pl.pallas_call(body, name=name, grid=grid, in_specs=in_specs, out_specs=out_specs, out_shape=out_shape,
                              scratch_shapes=scratch_shapes, compiler_params=_cparams(("arbitrary",) * len(grid)))(*args)
    n_in, n_out, n_scr = len(in_specs), len(out_specs), len(scratch_shapes)
    total = math.prod(grid)

    def riding(*refs):
        host_in, x_ref = refs[:n_in], refs[n_in]
        host_out, out_ref = refs[n_in + 1:n_in + 1 + n_out], refs[n_in + 1 + n_out]
        host_scr = refs[n_in + 2 + n_out:n_in + 2 + n_out + n_scr]
        start, forward, finish = rider.phases(x_ref, out_ref, *refs[n_in + 2 + n_out + n_scr:])
        step = 0
        for axis, size in enumerate(grid):
            step = step * size + pl.program_id(axis)
        pl.when(step == 0)(start)
        pl.when(step == (3 * total) // 4)(forward)
        body(*host_in, *host_out, *host_scr)
        pl.when(step == total - 1)(finish)

    return pl.pallas_call(
        riding, name=name, grid=grid, in_specs=list(in_specs) + [ANY_SPEC], out_specs=list(out_specs) + [ANY_SPEC],
        out_shape=list(out_shape) + [rider.out_shape], scratch_shapes=list(scratch_shapes) + list(COMM_SCRATCH),
        compiler_params=_cparams(("arbitrary",) * len(grid)))(*args, rider.x)


def _adamw_math(w, g, m, v):
    m = ADAM_B1 * m + (1.0 - ADAM_B1) * g
    v = ADAM_B2 * v + (1.0 - ADAM_B2) * jnp.square(g)
    m_hat = m / (1.0 - ADAM_B1 ** ADAM_STEP)
    v_hat = v / (1.0 - ADAM_B2 ** ADAM_STEP)
    delta = -ADAM_LR * (m_hat / (jnp.sqrt(v_hat) + ADAM_EPS) + ADAM_WD * w)
    return delta, m, v


ADAM_ROWS = 128


def _elementwise_tile(rows, cols):
    if rows % ADAM_ROWS == 0:
        return ADAM_ROWS, cols
    return rows, (2 * LANE if cols % (2 * LANE) == 0 else cols)


def _add_blocks(a, b):
    g, rows, cols = a.shape
    tr, tc = _elementwise_tile(rows, cols)

    def body(a_ref, b_ref, o_ref):
        o_ref[...] = (a_ref[...].astype(F32) + b_ref[...].astype(F32)).astype(o_ref.dtype)

    blk = pl.BlockSpec((None, tr, tc), lambda k, i, j: (k, i, j))
    return pl.pallas_call(body, name="add_pair_blocks", grid=(g, rows // tr, cols // tc), in_specs=[blk, blk],
                          out_specs=blk, out_shape=jax.ShapeDtypeStruct(a.shape, a.dtype),
                          compiler_params=_cparams(("parallel", "parallel", "parallel")))(a, b)


def _adamw_reduce(w, m, v, parts, row0, name):
    rows, cols = w.shape
    n_parts = parts.shape[0]
    tr, tc = _elementwise_tile(rows, cols)
    r0 = row0 // tr

    def body(w_ref, m_ref, v_ref, *refs):
        part_refs, (g_ref, d_ref, nm_ref, nv_ref) = refs[:n_parts], refs[n_parts:]
        g = part_refs[0][...].astype(F32)
        for r in part_refs[1:]:
            g = g + r[...].astype(F32)
        delta, nm, nv = _adamw_math(w_ref[...], g, m_ref[...], v_ref[...])
        g_ref[...] = g
        d_ref[...] = delta
        nm_ref[...] = nm
        nv_ref[...] = nv

    blk = pl.BlockSpec((tr, tc), lambda i, j: (i, j))
    part_specs = [pl.BlockSpec((None, tr, tc), functools.partial(lambda i, j, k: (k, r0 + i, j), k=k))
                  for k in range(n_parts)]
    return pl.pallas_call(
        body, name=name, grid=(rows // tr, cols // tc), in_specs=[blk] * 3 + part_specs, out_specs=[blk] * 4,
        out_shape=[jax.ShapeDtypeStruct(w.shape, F32)] * 4, compiler_params=_cparams(("parallel", "parallel")),
    )(w, m, v, *([parts] * n_parts))


def _small_reduce(gathered, lane_sum_from):
    r = gathered.shape[1]

    def body(g_ref, o_ref):
        g = g_ref[0]
        for k in range(1, N_DEV):
            g = g + g_ref[k]
        tot = jnp.broadcast_to(jnp.sum(g, axis=-1, keepdims=True), g.shape)
        o_ref[...] = jnp.where(_iota2(g.shape, 0) >= lane_sum_from, tot, g)

    return pl.pallas_call(body, name="small_grad_reduce", out_shape=jax.ShapeDtypeStruct((r, LANE), F32))(gathered)


def _adamw_small(w, m, v, g):
    def body(w_ref, m_ref, v_ref, g_ref, d_ref, nm_ref, nv_ref):
        d_ref[...], nm_ref[...], nv_ref[...] = _adamw_math(w_ref[...], g_ref[...], m_ref[...], v_ref[...])

    return pl.pallas_call(body, name="adamw_small", out_shape=[jax.ShapeDtypeStruct(w.shape, F32)] * 3)(w, m, v, g)


def _pack_rows(arrays):
    rows = [jnp.pad(a.reshape(-1), (0, -a.size % LANE)).reshape(-1, LANE) for a in arrays]
    out = jnp.concatenate(rows, axis=0)
    return jnp.pad(out, ((0, -out.shape[0] % 8), (0, 0)))


def _unpack_rows(packed, shapes):
    out, r = [], 0
    for shp in shapes:
        size = math.prod(shp)
        nrows = -(-size // LANE)
        out.append(packed[r:r + nrows].reshape(-1)[:size].reshape(shp))
        r += nrows
    return out


def _add_residual(acc, res):
    return (res + acc,)


def _by_cols(g):
    return g.reshape(g.shape[0], N_DEV, -1).transpose(1, 0, 2)


def _from_cols(blocks):
    return blocks.transpose(1, 0, 2).reshape(blocks.shape[1], -1)


ROWS_OUT = D_MODEL // N_DEV

W_IN_SEGMENTS = ((0, WI_LR, "big", 0), (WI_LR, WI_DQKV, "gates", ZS_LR), (WI_DQKV, WI_DA, "big", ZB_DQKV),
                 (WI_DA, WI_DB, "gates", ZS_DA), (WI_DB, WI_GA, "gates", ZS_DB), (WI_GA, D_IN, "big", ZB_GA))


def _w_in_gate_rows(wt):
    parts = []
    for lo, hi, which, _ in W_IN_SEGMENTS:
        if which == "gates":
            parts += [wt[lo:hi], jnp.zeros((LANE - (hi - lo), wt.shape[1]), wt.dtype)]
    return jnp.concatenate(parts, axis=0)


def _fill_gate_rows(dw_in_t, dws_t):
    for lo, hi, which, first in W_IN_SEGMENTS:
        if which == "gates":
            dw_in_t = lax.dynamic_update_slice(dw_in_t, dws_t[first:first + hi - lo], (lo, 0))
    return dw_in_t


def _local_step(x, p, target, w, up_shard, rows_shard):
    bsz, seq, _ = x.shape
    t, nc = bsz * seq, seq // CHUNK
    x0, p2, tgt = x.reshape(t, D_MODEL), p.reshape(t, PLE_DIM), target.reshape(t, D_MODEL)

    h = _rms_fwd(x0, w["g_mix"], "rms_mix")
    zb, up_blocks = _mm(h, w["w_in_t"], tb=True, b_is_w_in_t=True, name="in_proj", rider=_gather_rider(up_shard))
    w_up = _from_cols(up_blocks)
    zs = _mm(h, w["ws_t"], tb=True, name="in_proj_gates")
    act = _conv_fwd(zb, w["conv"], bsz, seq)
    mix_gla, gla_states = _gla_fwd(zb, zs, w["w2p"], w["gla_b"], w["gla_norm"], bsz, nc)
    mixed, dn_states, dn_t_invs, row_blocks = _dn_fwd(act, zb, zs, w["alog_b"], w["dtb_b"], w["dn_norm"], mix_gla,
                                                      bsz, nc, rider=_gather_rider(rows_shard))
    w_out = row_blocks[:, :ROWS_OUT].reshape(D_MODEL, D_MODEL)
    w_pg = row_blocks[:, ROWS_OUT:2 * ROWS_OUT].reshape(D_MODEL, D_MODEL)
    w_down = row_blocks[:, 2 * ROWS_OUT:].reshape(D_FF, D_MODEL)
    x1 = _mm(mixed, w_out, epilogue=_add_residual, extras=(x0,), name="out_proj")
    h2 = _rms_fwd(x1, w["g_mlp"], "rms_mlp")
    u, a = _mm(h2, w_up, out_dtypes=(BF16, BF16), name="mlp_up",
               epilogue=lambda acc: (acc, jnp.square(jnp.maximum(acc, 0.0))))
    x2 = _mm(a, w_down, epilogue=_add_residual, extras=(x1,), name="mlp_down")
    h3 = _rms_fwd(x2, w["g_ple"], "rms_ple")
    pp = _mm(p2, w["w_pp"], name="ple_proj")
    gpre, x3 = _mm(h3, w_pg, out_dtypes=(F32, F32), extras=(x2, pp), name="ple_gate",
                   epilogue=lambda acc, res, proj: (acc, res + _sigmoid(acc) * proj))
    dx3, dg_final, loss = _loss_head(x3, w["g_final"], tgt)

    dgpre, dpp = _ple_bwd(dx3, gpre, pp)
    dw_pp = _mm(p2, dpp, ta=True, out_dtypes=(BF16,), name="d_w_ple_proj")
    dw_pg = _mm(h3, dgpre, ta=True, out_dtypes=(BF16,), name="d_w_ple_gate")
    dh3 = _mm(dgpre, w_pg, tb=True, name="d_h_ple")
    dx2, dg_ple = _rms_bwd(x2, w["g_ple"], dh3, dx3, "rms_ple_bwd")
    du = _mm(dx2, w_down, tb=True, out_dtypes=(BF16,), extras=(u,), name="d_mlp_hidden",
             epilogue=lambda acc, uu: (acc * (2.0 * jnp.maximum(uu.astype(F32), 0.0)),))
    dw_down = _mm(a, dx2, ta=True, out_dtypes=(BF16,), name="d_w_down")
    dw_up = _mm(h2, du, ta=True, out_dtypes=(BF16,), name="d_w_up")
    dh2 = _mm(du, w_up, tb=True, name="d_h_mlp")
    dx1, dg_mlp = _rms_bwd(x1, w["g_mlp"], dh2, dx2, "rms_mlp_bwd")
    dmixed = _mm(dx1, w_out, tb=True, out_dtypes=(BF16,), name="d_mixed")
    dw_out = _mm(mixed, dx1, ta=True, out_dtypes=(BF16,), name="d_w_out")

    d_rows = jnp.concatenate([dw_out.reshape(N_DEV, ROWS_OUT, D_MODEL), dw_pg.reshape(N_DEV, ROWS_OUT, D_MODEL),
                              dw_down.reshape(N_DEV, D_FF // N_DEV, D_MODEL)], axis=1)
    (dact, dza, dzb_, dgz, dgb, dal, ddt, ddn, recv_rows) = _dn_bwd(
        act, zb, zs, w["alog_b"], w["dtb_b"], w["dn_norm"], dn_states, dn_t_invs, dmixed, bsz, nc,
        rider=_scatter_rider(d_rows))
    (gdq, gdk, gdv, dgg, dga, dlr, dw2, dgla_b, dgla_norm, recv_up) = _gla_bwd(
        zb, zs, w["w2p"], w["gla_b"], w["gla_norm"], gla_states, dmixed, bsz, nc,
        rider=_scatter_rider(_by_cols(dw_up)))
    dqkv, dconv = _conv_bwd(zb, w["conv"], dact, bsz, seq)
    dzb = jnp.concatenate([gdq, gdk, gdv, dgg, dqkv, dgz, dga, dgb], axis=1)
    dzs = jnp.concatenate([dlr, dza, dzb_], axis=1)
    dw_in_t = _mm(dzb, h, ta=True, out_dtypes=(BF16,), out_is_w_in_t=True, name="d_w_in")
    dws_t = _mm(dzs, h, ta=True, out_dtypes=(BF16,), name="d_w_in_gates")
    by_chip = _fill_gate_rows(dw_in_t, dws_t).reshape(N_DEV // 2, 2, D_IN_SHARD, D_MODEL)
    from_sibling = _exchange(_pair_rider(by_chip), "pair_d_w_in")
    mine = lax.dynamic_index_in_dim(by_chip, lax.axis_index("c"), axis=1, keepdims=False)
    chip_sums = _add_blocks(mine, from_sibling)
    dh_gates = _mm(dzs, w["ws_t"], name="d_h_mix_gates")
    dh, recv_in = _mm(dzb, w["w_in_t"], b_is_w_in_t=True, epilogue=_add_residual, extras=(dh_gates,), name="d_h_mix",
                      rider=_chip_scatter_rider(chip_sums))
    gx, dg_mix = _rms_bwd(x0, w["g_mix"], dh, dx1, "rms_mix_bwd")

    dgla_w2 = dw2[:, :GLA_LOWRANK, :].transpose(1, 0, 2).reshape(GLA_LOWRANK, GLA_QK)
    return dict(
        loss=loss[0, 0], grad_x=gx.reshape(x.shape), recv_in=recv_in, recv_up=recv_up, recv_rows=recv_rows,
        w_ple_proj=dw_pp,
        g_mix=dg_mix, gla_b=dgla_b.reshape(1, GLA_QK), gla_norm=dgla_norm, dn_norm=ddn, g_mlp=dg_mlp, g_ple=dg_ple,
        g_final=dg_final, gla_w2=dgla_w2, dn_conv=dconv,
        a_log_lanes=dal.reshape(DN_HEADS, LANE), dt_bias_lanes=ddt.reshape(DN_HEADS, LANE))


def _first_weights(g_mix, w_in, gla_w2, gla_b, gla_norm, dn_conv, dn_a_log, dn_dt_bias, dn_norm, g_mlp, g_ple,
                   w_ple_proj, g_final):
    w_in_t = _all_gather(jnp.swapaxes(w_in[0], 0, 1).astype(BF16), "gather_w_in").reshape(D_IN, D_MODEL)
    w_pp = _all_gather(w_ple_proj[0].astype(BF16), "gather_w_ple_proj")
    small = _all_gather(_pack_rows([gla_w2[0], dn_conv[0]]), "gather_w_small")
    n_w2 = GLA_LOWRANK * GLA_QK // N_DEV // LANE
    n_cv = DN_CONV * DN_QKV // N_DEV // LANE
    w2 = small[:, :n_w2].reshape(N_DEV, GLA_LOWRANK, GLA_QK // N_DEV).transpose(1, 0, 2).reshape(GLA_LOWRANK, GLA_QK)
    conv = small[:, n_w2:n_w2 + n_cv].reshape(N_DEV, DN_CONV, DN_QKV // N_DEV).transpose(1, 0, 2).reshape(DN_CONV, DN_QKV)

    return dict(
        w_in_t=w_in_t, ws_t=_w_in_gate_rows(w_in_t), w_pp=_from_cols(w_pp),
        w2p=jnp.pad(w2, ((0, LANE - GLA_LOWRANK), (0, 0))), conv=conv,
        alog_b=jnp.broadcast_to(dn_a_log[0][:, None, None], (DN_HEADS, 1, LANE)),
        dtb_b=jnp.broadcast_to(dn_dt_bias[0][:, None, None], (DN_HEADS, 1, LANE)),
        g_mix=g_mix, gla_b=gla_b, gla_norm=gla_norm, dn_norm=dn_norm, g_mlp=g_mlp, g_ple=g_ple,
        g_final=g_final.reshape(1, D_MODEL))


def kernel(x, p, g_mix, w_in, gla_w2, gla_b, gla_norm, dn_conv, dn_a_log, dn_dt_bias, dn_norm, w_out, g_mlp, w_up, w_down, g_ple, w_ple_gate, w_ple_proj, g_final, loss_target, m_g_mix, m_w_in, m_gla_w2, m_gla_b, m_gla_norm, m_dn_conv, m_dn_a_log, m_dn_dt_bias, m_dn_norm, m_w_out, m_g_mlp, m_w_up, m_w_down, m_g_ple, m_w_ple_gate, m_w_ple_proj, m_g_final, v_g_mix, v_w_in, v_gla_w2, v_gla_b, v_gla_norm, v_dn_conv, v_dn_a_log, v_dn_dt_bias, v_dn_norm, v_w_out, v_g_mlp, v_w_up, v_w_down, v_g_ple, v_w_ple_gate, v_w_ple_proj, v_g_final):
    names = ["g_mix", "w_in", "gla_w2", "gla_b", "gla_norm", "dn_conv", "dn_a_log", "dn_dt_bias", "dn_norm", "w_out",
             "g_mlp", "w_up", "w_down", "g_ple", "w_ple_gate", "w_ple_proj", "g_final"]
    ws = dict(zip(names, (g_mix, w_in, gla_w2, gla_b, gla_norm, dn_conv, dn_a_log, dn_dt_bias, dn_norm, w_out, g_mlp,
                          w_up, w_down, g_ple, w_ple_gate, w_ple_proj, g_final)))
    ms = dict(zip(names, (m_g_mix, m_w_in, m_gla_w2, m_gla_b, m_gla_norm, m_dn_conv, m_dn_a_log, m_dn_dt_bias,
                          m_dn_norm, m_w_out, m_g_mlp, m_w_up, m_w_down, m_g_ple, m_w_ple_gate, m_w_ple_proj,
                          m_g_final)))
    vs = dict(zip(names, (v_g_mix, v_w_in, v_gla_w2, v_gla_b, v_gla_norm, v_dn_conv, v_dn_a_log, v_dn_dt_bias,
                          v_dn_norm, v_w_out, v_g_mlp, v_w_up, v_w_down, v_g_ple, v_w_ple_gate, v_w_ple_proj,
                          v_g_final)))
    me = 4 * lax.axis_index("x") + 2 * lax.axis_index("y") + lax.axis_index("c")

    first = _first_weights(g_mix, w_in, gla_w2, gla_b, gla_norm, dn_conv, dn_a_log, dn_dt_bias, dn_norm, g_mlp,
                           g_ple, w_ple_proj, g_final)
    rows_shard = jnp.concatenate([w_out[0], w_ple_gate[0], w_down[0]], axis=0).astype(BF16)
    r = _local_step(x, p[0], loss_target, first, w_up[0].astype(BF16), rows_shard)
    loss = lax.psum(r["loss"], ("x", "y", "c"))

    grads, deltas, new_m, new_v = {}, {}, {}, {}

    def big(name, parts, row0=0):
        g, d, nm, nv = _adamw_reduce(ws[name][0], ms[name][0], vs[name][0], parts, row0, "adamw_" + name)
        grads[name], deltas[name], new_m[name], new_v[name] = g[None], d[None], nm[None], nv[None]

    t_outs = _adamw_reduce(*[jnp.swapaxes(d["w_in"][0], 0, 1) for d in (ws, ms, vs)], r["recv_in"], 0, "adamw_w_in")
    grads["w_in"], deltas["w_in"], new_m["w_in"], new_v["w_in"] = [jnp.swapaxes(o, 0, 1)[None] for o in t_outs]
    big("w_up", r["recv_up"])
    big("w_ple_proj", _all_to_all(_by_cols(r["w_ple_proj"]), "scatter_d_w_ple_proj"))
    big("w_out", r["recv_rows"], 0)
    big("w_ple_gate", r["recv_rows"], ROWS_OUT)
    big("w_down", r["recv_rows"], 2 * ROWS_OUT)

    vec_names = ["g_mix", "gla_b", "gla_norm", "dn_norm", "g_mlp", "g_ple", "g_final"]
    packed = _pack_rows([r[n] for n in vec_names] + [r["gla_w2"], r["dn_conv"]])
    lane_rows = packed.shape[0]
    packed = jnp.concatenate([packed, r["a_log_lanes"], r["dt_bias_lanes"]], axis=0)
    total = _small_reduce(_all_gather(packed, "gather_small_grads"), lane_rows)
    parts = _unpack_rows(total, [r[n].shape for n in vec_names] + [r["gla_w2"].shape, r["dn_conv"].shape])
    sg = dict(zip(vec_names, parts[:len(vec_names)]))
    sg["g_final"] = sg["g_final"].reshape(D_MODEL)
    sg["gla_w2"] = lax.dynamic_slice_in_dim(parts[-2], me * (GLA_QK // N_DEV), GLA_QK // N_DEV, axis=1)
    sg["dn_conv"] = lax.dynamic_slice_in_dim(parts[-1], me * (DN_QKV // N_DEV), DN_QKV // N_DEV, axis=1)
    sg["dn_a_log"] = total[lane_rows:lane_rows + DN_HEADS, 0]
    sg["dn_dt_bias"] = total[lane_rows + DN_HEADS:lane_rows + 2 * DN_HEADS, 0]
    small_names = vec_names + ["gla_w2", "dn_conv", "dn_a_log", "dn_dt_bias"]
    shapes = [ws[n].shape for n in small_names]
    d_s, m_s, v_s = _adamw_small(_pack_rows([ws[n] for n in small_names]), _pack_rows([ms[n] for n in small_names]),
                                 _pack_rows([vs[n] for n in small_names]), _pack_rows([sg[n] for n in small_names]))
    for n, d, nm, nv in zip(small_names, _unpack_rows(d_s, shapes), _unpack_rows(m_s, shapes), _unpack_rows(v_s, shapes)):
        grads[n], deltas[n], new_m[n], new_v[n] = sg[n].reshape(ws[n].shape), d, nm, nv

    return (loss, r["grad_x"], *[grads[n] for n in names], *[deltas[n] for n in names],
            *[new_m[n] for n in names], *[new_v[n] for n in names])
```

```python
import functools
import math

import jax
import jax.numpy as jnp
from jax import lax
from jax.experimental import pallas as pl
from jax.experimental.pallas import tpu as pltpu

F32 = jnp.float32
BF16 = jnp.bfloat16

N_DEV = 8
D_MODEL = 2048
CHUNK = 64
PLE_DIM = 256
EPS = 1e-6
GLA_HEADS = 4
GLA_DK = 256
GLA_DV = 512
GLA_LOWRANK = 16
GLA_TAU = 16.0
DN_HEADS = 16
DN_D = 128
DN_CONV = 4
D_FF = 4 * D_MODEL
GLA_QK = GLA_HEADS * GLA_DK
GLA_V = GLA_HEADS * GLA_DV
DN_QKV = 3 * DN_HEADS * DN_D
D_IN = 2 * GLA_QK + 2 * GLA_V + GLA_LOWRANK + DN_QKV + D_MODEL + 2 * DN_HEADS + 2 * D_MODEL
D_IN_SHARD = D_IN // N_DEV

ADAM_LR = 0.001
ADAM_B1 = 0.9
ADAM_B2 = 0.999
ADAM_EPS = 1e-08
ADAM_WD = 0.01
ADAM_STEP = 10

LANE = 128
ZB_GQ, ZB_GK, ZB_GV, ZB_GG = 0, 1024, 2048, 4096
ZB_DQKV, ZB_DZ, ZB_GA, ZB_GB = 6144, 12288, 14336, 16384
ZB_W = 18432
ZS_LR, ZS_DA, ZS_DB = 0, 128, 256
ZS_W = 384
WI_LR = 2 * GLA_QK + 2 * GLA_V
WI_DQKV = WI_LR + GLA_LOWRANK
WI_DA = WI_DQKV + DN_QKV + D_MODEL
WI_DB = WI_DA + DN_HEADS
WI_GA = WI_DB + DN_HEADS

VMEM_LIMIT = 56 * 1024 * 1024

def _bdot(a, b, dims):
    return lax.dot_general(a.astype(BF16), b.astype(BF16), dims, preferred_element_type=F32)


def _split3(x):
    hi = x.astype(BF16)
    rest = x - hi.astype(F32)
    mid = rest.astype(BF16)
    return hi, mid, (rest - mid.astype(F32)).astype(BF16)


def _dot01(x, m, dims, x_first):
    m = m.astype(BF16)
    out = None
    for piece in _split3(x):
        d = lax.dot_general(piece, m, dims, preferred_element_type=F32) if x_first else \
            lax.dot_general(m, piece, dims, preferred_element_type=F32)
        out = d if out is None else out + d
    return out


@functools.partial(jax.custom_vjp, nondiff_argnums=(2, 3))
def _pick_dot(x, m, dims, dims_t):
    return _dot01(x, m, dims, True)


def _pick_dot_fwd(x, m, dims, dims_t):
    return _dot01(x, m, dims, True), m


def _pick_dot_bwd(dims, dims_t, m, ct):
    return _dot01(ct, m, dims_t, True), jnp.zeros_like(m)


_pick_dot.defvjp(_pick_dot_fwd, _pick_dot_bwd)


@functools.partial(jax.custom_vjp, nondiff_argnums=(2, 3))
def _left_dot(m, x, dims, dims_t):
    return _dot01(x, m, dims, False)


def _left_dot_fwd(m, x, dims, dims_t):
    return _dot01(x, m, dims, False), m


def _left_dot_bwd(dims, dims_t, m, ct):
    return jnp.zeros_like(m), _dot01(ct, m, dims_t, False)


_left_dot.defvjp(_left_dot_fwd, _left_dot_bwd)


def _dot3(a, b, dims):
    ah, bh = a.astype(BF16), b.astype(BF16)
    al, bl = (a - ah.astype(F32)).astype(BF16), (b - bh.astype(F32)).astype(BF16)
    dot = functools.partial(lax.dot_general, dimension_numbers=dims, preferred_element_type=F32)
    return dot(ah, bh) + (dot(ah, bl) + dot(al, bh))


def _sigmoid(x):
    return 1.0 / (1.0 + jnp.exp(-x))


def _silu(x):
    return x * _sigmoid(x)


def _softplus(x):
    return jnp.maximum(x, 0.0) + jnp.log(1.0 + jnp.exp(-jnp.abs(x)))


def _iota2(shape, dim):
    return lax.broadcasted_iota(jnp.int32, shape, dim)


def _cparams(sem=None):
    return pltpu.CompilerParams(dimension_semantics=sem, vmem_limit_bytes=VMEM_LIMIT)


BNN = (((2,), (1,)), ((0,), (0,)))
BNT = (((2,), (2,)), ((0,), (0,)))
BTN = (((1,), (1,)), ((0,), (0,)))


def _gla_chunk(st, q, k, v, lr, w2, b, gg, ga, gn):
    hb, c, _ = q.shape
    incl = (_iota2((c, c), 0) >= _iota2((c, c), 1))[None]
    tri = jnp.broadcast_to(incl.astype(F32), (hb, c, c))
    q = q.astype(F32) * (GLA_DK ** -0.5)
    k = k.astype(F32)
    v = v.astype(F32)
    lr_b = jnp.broadcast_to(lr[None], (hb,) + lr.shape)
    lf = -_softplus(-(_bdot(lr_b, w2, BNN) + b)) / GLA_TAU
    bcum = _left_dot(tri, lf, BNN, BTN)
    blast = jnp.sum(lf, axis=1, keepdims=True)
    q_in = q * jnp.exp(bcum)
    k_in = k * jnp.exp(-bcum)
    a = jnp.where(incl, _bdot(q_in, k_in, BNT), 0.0)
    o = _bdot(a, v, BNN) + _bdot(q_in, st, BNT)
    k_dec = k * jnp.exp(blast - bcum)
    st_new = st * jnp.exp(blast) + _bdot(v, k_dec, BTN)
    on = o * lax.rsqrt(jnp.mean(o * o, axis=-1, keepdims=True) + EPS) * gn
    res = _sigmoid(ga.astype(F32)) * on * _silu(gg.astype(F32))
    return res, st_new


def _tri_inv_raw(a):
    _, c, _ = a.shape
    eye = (_iota2((c, c), 0) == _iota2((c, c), 1)).astype(F32)[None]
    x = a
    p = eye - a
    for _ in range(5):
        x = _dot3(x, x, BNN)
        p = _dot3(p, eye + x, BNN)
    return p


def _tri_inv_bwd(t, dt):
    return (-_dot3(_dot3(t, dt, BTN), t, BNT),)


@jax.custom_vjp
def _tri_inv_given(a, t):
    return t


def _tri_inv_given_fwd(a, t):
    return t, t


def _tri_inv_given_bwd(t, dt):
    return _tri_inv_bwd(t, dt) + (jnp.zeros_like(t),)


_tri_inv_given.defvjp(_tri_inv_given_fwd, _tri_inv_given_bwd)


@functools.partial(jax.custom_vjp, nondiff_argnums=(1,))
def _column_on_lanes(z, j):
    picked = jnp.where(_iota2(z.shape, 1) == j, z, 0.0)
    return jnp.broadcast_to(jnp.sum(picked, axis=1, keepdims=True), z.shape)


def _column_on_lanes_fwd(z, j):
    return _column_on_lanes(z, j), None


def _column_on_lanes_bwd(j, _, ct):
    total = jnp.broadcast_to(jnp.sum(ct, axis=1, keepdims=True), ct.shape)
    return (jnp.where(_iota2(ct.shape, 1) == j, total, 0.0),)


_column_on_lanes.defvjp(_column_on_lanes_fwd, _column_on_lanes_bwd)


def _dn_chunk(s, qr, kr, vr, za, zb, alog, dtb, gz, gb, dn, t_saved=None):
    hb, c, _ = qr.shape
    row, col = _iota2((c, c), 0), _iota2((c, c), 1)
    incl = (row >= col)[None]
    strict = (row > col)[None]
    tri = jnp.broadcast_to(incl.astype(F32), (hb, c, c))
    eye = (row == col).astype(F32)[None]
    ones_cc = jnp.ones((hb, c, c), F32)
    lane0 = (lax.broadcasted_iota(jnp.int32, (hb, LANE, c), 1) == 0).astype(F32)

    def l2n(t):
        return t * lax.rsqrt(jnp.sum(t * t, axis=-1, keepdims=True) + EPS)

    q = l2n(qr.astype(F32)) * (DN_D ** -0.5)
    k = l2n(kr.astype(F32))
    v = vr.astype(F32)
    g_heads = -jnp.exp(alog) * _softplus(za + dtb)
    beta_heads = _sigmoid(zb)
    g = jnp.concatenate([_column_on_lanes(g_heads, j)[None] for j in range(hb)], axis=0)
    beta = jnp.concatenate([_column_on_lanes(beta_heads, j)[None] for j in range(hb)], axis=0)
    gcum = _left_dot(tri, g, BNN, BTN)
    glast = jnp.sum(g, axis=1, keepdims=True)
    cm = _pick_dot(gcum, lane0, BNN, BNT)
    rm = _left_dot(ones_cc, cm * eye, BNN, BTN)
    dec = jnp.exp(jnp.where(incl, cm - rm, -1e30))
    kb = k * beta
    a = jnp.where(strict, _bdot(kb, k, BNT) * dec, 0.0)
    t = _tri_inv_raw(a) if t_saved is None else _tri_inv_given(a, t_saved)
    egc = jnp.exp(gcum)
    u = _bdot(t, v * beta, BNN)
    w = _bdot(t, kb * egc, BNN)
    attn = jnp.where(incl, _bdot(q, k, BNT) * dec, 0.0)
    q_dec = q * egc
    k_dec = k * jnp.exp(glast - gcum)
    v_new = u - _bdot(w, s, BNN)
    o = _bdot(q_dec, s, BNN) + _bdot(attn, v_new, BNN)
    s_new = s * jnp.exp(glast) + _bdot(k_dec, v_new, BTN)
    on = o * lax.rsqrt(jnp.mean(o * o, axis=-1, keepdims=True) + EPS) * dn
    res = _sigmoid(gb.astype(F32)) * on * _silu(gz.astype(F32))
    return (res, s_new, t) if t_saved is None else (res, s_new)


def _heads(ref, n_heads, width):
    return jnp.stack([ref[:, j * width:(j + 1) * width] for j in range(n_heads)], axis=0)


def _gla_specs(nc, reverse):
    def rows(b, n):
        return b * nc + ((nc - 1 - n) if reverse else n)

    qk = lambda base: pl.BlockSpec((CHUNK, GLA_QK), lambda b, n: (rows(b, n), base // GLA_QK))
    vv = lambda base: pl.BlockSpec((CHUNK, GLA_V), lambda b, n: (rows(b, n), base // GLA_V))
    lr = lambda c: pl.BlockSpec((CHUNK, LANE), lambda b, n: (rows(b, n), c))
    full = lambda shape: pl.BlockSpec(shape, lambda b, n: (0,) * len(shape))
    return rows, qk, vv, lr, full


def _gla_inputs(q_ref, k_ref, v_ref, gg_ref, ga_ref, lr_ref, w2_ref, b_ref, gn_ref):
    return (_heads(q_ref, GLA_HEADS, GLA_DK), _heads(k_ref, GLA_HEADS, GLA_DK), _heads(v_ref, GLA_HEADS, GLA_DV),
            lr_ref[...], _heads(w2_ref, GLA_HEADS, GLA_DK), _heads(b_ref, GLA_HEADS, GLA_DK),
            _heads(gg_ref, GLA_HEADS, GLA_DV), _heads(ga_ref, GLA_HEADS, GLA_DV), gn_ref[...])


def _gla_fwd(zb, zs, w2p, gla_b, gla_norm, bsz, nc):
    t = zb.shape[0]
    rows, qk, vv, lr, full = _gla_specs(nc, False)

    def body(q_ref, k_ref, v_ref, gg_ref, ga_ref, lr_ref, w2_ref, b_ref, gn_ref, o_ref, st_ref, state):
        st = jnp.where(pl.program_id(1) > 0, state[...], 0.0)
        st_ref[0, 0] = st
        res, st_new = _gla_chunk(st, *_gla_inputs(q_ref, k_ref, v_ref, gg_ref, ga_ref, lr_ref, w2_ref, b_ref, gn_ref))
        for j in range(GLA_HEADS):
            o_ref[:, j * GLA_DV:(j + 1) * GLA_DV] = res[j]
        state[...] = st_new

    return pl.pallas_call(
        body, name="gla_fwd", grid=(bsz, nc),
        in_specs=[qk(ZB_GQ), qk(ZB_GK), vv(ZB_GV), vv(ZB_GG), vv(ZB_GA), lr(ZS_LR // LANE),
                  full((LANE, GLA_QK)), full((1, GLA_QK)), full((1, GLA_DV))],
        out_specs=[vv(0), pl.BlockSpec((1, 1, GLA_HEADS, GLA_DV, GLA_DK), lambda b, n: (b, n, 0, 0, 0))],
        out_shape=[jax.ShapeDtypeStruct((t, GLA_V), F32),
                   jax.ShapeDtypeStruct((bsz, nc, GLA_HEADS, GLA_DV, GLA_DK), F32)],
        scratch_shapes=[pltpu.VMEM((GLA_HEADS, GLA_DV, GLA_DK), F32)],
        compiler_params=_cparams(("arbitrary", "arbitrary")),
    )(zb, zb, zb, zb, zb, zs, w2p, gla_b, gla_norm)


def _gla_bwd(zb, zs, w2p, gla_b, gla_norm, states, dmixed, bsz, nc, rider=None):
    t = zb.shape[0]
    rows, qk, vv, lr, full = _gla_specs(nc, True)

    def body(q_ref, k_ref, v_ref, gg_ref, ga_ref, lr_ref, w2_ref, b_ref, gn_ref, st_ref, dm_ref,
             dq_ref, dk_ref, dv_ref, dgg_ref, dga_ref, dlr_ref, dw2_ref, db_ref, dgn_ref, dstate):
        b, n = pl.program_id(0), pl.program_id(1)

        @pl.when((b == 0) & (n == 0))
        def _():
            dw2_ref[...] = jnp.zeros_like(dw2_ref)
            db_ref[...] = jnp.zeros_like(db_ref)
            dgn_ref[...] = jnp.zeros_like(dgn_ref)

        _, vjp = jax.vjp(_gla_chunk, st_ref[0, 0],
                         *_gla_inputs(q_ref, k_ref, v_ref, gg_ref, ga_ref, lr_ref, w2_ref, b_ref, gn_ref))
        dst_in = jnp.where(n > 0, dstate[...], 0.0)
        dst, dq, dk, dv, dlr, dw2, db, dgg, dga, dgn = vjp((_heads(dm_ref, GLA_HEADS, GLA_DV).astype(F32), dst_in))
        dstate[...] = dst
        for j in range(GLA_HEADS):
            dq_ref[:, j * GLA_DK:(j + 1) * GLA_DK] = dq[j].astype(dq_ref.dtype)
            dk_ref[:, j * GLA_DK:(j + 1) * GLA_DK] = dk[j].astype(dk_ref.dtype)
            dv_ref[:, j * GLA_DV:(j + 1) * GLA_DV] = dv[j].astype(dv_ref.dtype)
            dgg_ref[:, j * GLA_DV:(j + 1) * GLA_DV] = dgg[j].astype(dgg_ref.dtype)
            dga_ref[:, j * GLA_DV:(j + 1) * GLA_DV] = dga[j].astype(dga_ref.dtype)
        dlr_ref[...] = dlr
        dw2_ref[...] += dw2
        db_ref[...] += db
        dgn_ref[...] += dgn

    return _hosted_call(
        body, rider, name="gla_bwd", grid=(bsz, nc),
        in_specs=[qk(ZB_GQ), qk(ZB_GK), vv(ZB_GV), vv(ZB_GG), vv(ZB_GA), lr(ZS_LR // LANE),
                  full((LANE, GLA_QK)), full((1, GLA_QK)), full((1, GLA_DV)),
                  pl.BlockSpec((1, 1, GLA_HEADS, GLA_DV, GLA_DK), lambda b, n: (b, nc - 1 - n, 0, 0, 0)),
                  vv(0)],
        out_specs=[qk(0), qk(0), vv(0), vv(0), vv(0), lr(0),
                   full((GLA_HEADS, LANE, GLA_DK)), full((GLA_HEADS, 1, GLA_DK)), full((1, GLA_DV))],
        out_shape=[jax.ShapeDtypeStruct((t, GLA_QK), BF16), jax.ShapeDtypeStruct((t, GLA_QK), BF16),
                   jax.ShapeDtypeStruct((t, GLA_V), BF16), jax.ShapeDtypeStruct((t, GLA_V), BF16),
                   jax.ShapeDtypeStruct((t, GLA_V), BF16), jax.ShapeDtypeStruct((t, LANE), F32),
                   jax.ShapeDtypeStruct((GLA_HEADS, LANE, GLA_DK), F32),
                   jax.ShapeDtypeStruct((GLA_HEADS, 1, GLA_DK), F32),
                   jax.ShapeDtypeStruct((1, GLA_DV), F32)],
        scratch_shapes=[pltpu.VMEM((GLA_HEADS, GLA_DV, GLA_DK), F32)],
        args=(zb, zb, zb, zb, zb, zs, w2p, gla_b, gla_norm, states, dmixed))


DN_HB = DN_HEADS


def _dn_specs(nc, reverse):
    wide = DN_HB * DN_D

    def rows(b, n, h):
        return b * nc + ((nc - 1 - n) if reverse else n)

    def col(base):
        return pl.BlockSpec((CHUNK, wide), lambda b, n, h: (rows(b, n, h), base // wide + h))

    def fixed(c):
        return pl.BlockSpec((CHUNK, LANE), lambda b, n, h: (rows(b, n, h), c))

    head = pl.BlockSpec((1, LANE), lambda b, n, h: (0, 0))
    return rows, col, fixed, head


def _lanes(j):
    return slice(j * DN_D, (j + 1) * DN_D)


def _by_head(ref):
    return jnp.stack([ref[:, _lanes(j)] for j in range(DN_HB)], axis=0)


def _dn_fwd(act, zb, zs, alog_b, dtb_b, dn_norm, mix_gla, bsz, nc, rider=None):
    t = zb.shape[0]
    rows, col, fixed, head = _dn_specs(nc, False)

    def body(q_ref, k_ref, v_ref, za_ref, zb_ref, al_ref, dt_ref, gz_ref, gb_ref, dn_ref, mg_ref,
             o_ref, st_ref, ti_ref, state):
        s = jnp.where(pl.program_id(1) > 0, state[...], 0.0)
        st_ref[0, 0] = s
        res, s_new, t_inv = _dn_chunk(s, _by_head(q_ref), _by_head(k_ref), _by_head(v_ref), za_ref[...], zb_ref[...],
                                      al_ref[...], dt_ref[...], _by_head(gz_ref), _by_head(gb_ref), dn_ref[...])
        ti_ref[0, 0] = t_inv
        for j in range(DN_HB):
            o_ref[:, _lanes(j)] = (res[j] + mg_ref[:, _lanes(j)]).astype(o_ref.dtype)
        state[...] = s_new

    return _hosted_call(
        body, rider, name="dn_fwd", grid=(bsz, nc, DN_HEADS // DN_HB),
        in_specs=[col(0), col(DN_HEADS * DN_D), col(2 * DN_HEADS * DN_D),
                  fixed(ZS_DA // LANE), fixed(ZS_DB // LANE), head, head,
                  col(ZB_DZ), col(ZB_GB), pl.BlockSpec((1, DN_D), lambda b, n, h: (0, 0)), col(0)],
        out_specs=[col(0), pl.BlockSpec((1, 1, DN_HB, DN_D, DN_D), lambda b, n, h: (b, n, h, 0, 0)),
                   pl.BlockSpec((1, 1, DN_HB, CHUNK, CHUNK), lambda b, n, h: (b, n, h, 0, 0))],
        out_shape=[jax.ShapeDtypeStruct((t, D_MODEL), BF16),
                   jax.ShapeDtypeStruct((bsz, nc, DN_HEADS, DN_D, DN_D), F32),
                   jax.ShapeDtypeStruct((bsz, nc, DN_HEADS, CHUNK, CHUNK), F32)],
        scratch_shapes=[pltpu.VMEM((DN_HEADS, DN_D, DN_D), F32)],
        args=(act, act, act, zs, zs, alog_b, dtb_b, zb, zb, dn_norm, mix_gla))


def _dn_bwd(act, zb, zs, alog_b, dtb_b, dn_norm, states, t_invs, dmixed, bsz, nc, rider=None):
    t = zb.shape[0]
    rows, col, fixed, head = _dn_specs(nc, True)

    def body(q_ref, k_ref, v_ref, za_ref, zb_ref, al_ref, dt_ref, gz_ref, gb_ref, dn_ref, st_ref, ti_ref, dm_ref,
             dact_ref, dza_ref, dzb_ref, dgz_ref, dgb_ref, dal_ref, ddt_ref, ddn_ref, dstate):
        b, n = pl.program_id(0), pl.program_id(1)

        @pl.when((b == 0) & (n == 0))
        def _():
            dal_ref[...] = jnp.zeros_like(dal_ref)
            ddt_ref[...] = jnp.zeros_like(ddt_ref)
            ddn_ref[...] = jnp.zeros_like(ddn_ref)

        fn = functools.partial(_dn_chunk, t_saved=ti_ref[0, 0])
        _, vjp = jax.vjp(fn, st_ref[0, 0], _by_head(q_ref), _by_head(k_ref), _by_head(v_ref), za_ref[...],
                         zb_ref[...], al_ref[...], dt_ref[...], _by_head(gz_ref), _by_head(gb_ref), dn_ref[...])
        ds_in = jnp.where(n > 0, dstate[...], 0.0)
        ds, dq, dk, dv, dza, dzb, dal, ddt, dgz, dgb, ddn = vjp((_by_head(dm_ref).astype(F32), ds_in))
        dstate[...] = ds
        for j in range(DN_HB):
            for part, d in enumerate((dq, dk, dv)):
                dact_ref[:, pl.ds(part * DN_HEADS * DN_D + j * DN_D, DN_D)] = d[j]
            dgz_ref[:, _lanes(j)] = dgz[j].astype(dgz_ref.dtype)
            dgb_ref[:, _lanes(j)] = dgb[j].astype(dgb_ref.dtype)
        dal_ref[...] += dal
        ddt_ref[...] += ddt
        dza_ref[...] = dza
        dzb_ref[...] = dzb
        ddn_ref[...] += ddn

    full = lambda shape: pl.BlockSpec(shape, lambda b, n, h: (0,) * len(shape))
    return _hosted_call(
        body, rider, name="dn_bwd", grid=(bsz, nc, DN_HEADS // DN_HB),
        in_specs=[col(0), col(DN_HEADS * DN_D), col(2 * DN_HEADS * DN_D),
                  fixed(ZS_DA // LANE), fixed(ZS_DB // LANE), head, head,
                  col(ZB_DZ), col(ZB_GB), pl.BlockSpec((1, DN_D), lambda b, n, h: (0, 0)),
                  pl.BlockSpec((1, 1, DN_HB, DN_D, DN_D), lambda b, n, h: (b, nc - 1 - n, h, 0, 0)),
                  pl.BlockSpec((1, 1, DN_HB, CHUNK, CHUNK), lambda b, n, h: (b, nc - 1 - n, h, 0, 0)), col(0)],
        out_specs=[pl.BlockSpec((CHUNK, DN_QKV), lambda b, n, h: (rows(b, n, h), 0)), fixed(0), fixed(0), col(0), col(0),
                   full((1, LANE)), full((1, LANE)), full((1, DN_D))],
        out_shape=[jax.ShapeDtypeStruct((t, DN_QKV), F32),
                   jax.ShapeDtypeStruct((t, LANE), F32), jax.ShapeDtypeStruct((t, LANE), F32),
                   jax.ShapeDtypeStruct((t, D_MODEL), BF16), jax.ShapeDtypeStruct((t, D_MODEL), BF16),
                   jax.ShapeDtypeStruct((1, LANE), F32), jax.ShapeDtypeStruct((1, LANE), F32),
                   jax.ShapeDtypeStruct((1, DN_D), F32)],
        scratch_shapes=[pltpu.VMEM((DN_HEADS, DN_D, DN_D), F32)],
        args=(act, act, act, zs, zs, alog_b, dtb_b, zb, zb, dn_norm, states, t_invs, dmixed))


MM_VMEM_BUDGET = 40 * 1024 * 1024
MM_TILE_PREF = (1024, 1024, 2048)


def _divisor_tile(n, cap):
    if n <= cap:
        return n
    for c in range(cap - cap % LANE, 0, -LANE):
        if n % c == 0:
            return c
    return n


def _mm_tiles(m, n, kd, a_bytes, b_bytes, mn_bytes):
    tm, tn, tk = (_divisor_tile(d, c) for d, c in zip((m, n, kd), MM_TILE_PREF))

    def need(tm, tn, tk):
        acc = 0 if tk == kd else 4 * tm * tn
        return 2 * (tm * tk * a_bytes + tk * tn * b_bytes + tm * tn * mn_bytes) + acc + 4 * tm * tn

    while need(tm, tn, tk) > MM_VMEM_BUDGET:
        if tk > 512 and tk * max(tm * a_bytes, tn * b_bytes) >= tm * tn * mn_bytes:
            tk = _divisor_tile(kd, tk // 2)
        elif tn >= tm and tn > LANE:
            tn = _divisor_tile(n, tn // 2)
        else:
            tm = _divisor_tile(m, tm // 2)
    return tm, tn, tk


def _w_in_row_of(tile, tile_rows):
    skipped = jnp.where(tile >= ZB_GA // tile_rows, WI_GA - ZB_GA, jnp.where(tile >= ZB_DQKV // tile_rows,
                                                                             WI_DQKV - ZB_DQKV, 0))
    return pl.multiple_of(tile * tile_rows + skipped, 16)


def _mm(a, b, *, ta=False, tb=False, out_dtypes=(F32,), epilogue=None, extras=(), name, rider=None,
        b_is_w_in_t=False, out_is_w_in_t=False):
    m, kd = (a.shape[1], a.shape[0]) if ta else a.shape
    n = b.shape[0] if tb else b.shape[1]
    if b_is_w_in_t:
        n, kd = (ZB_W, kd) if tb else (n, ZB_W)
    mn_bytes = sum(e.dtype.itemsize for e in extras) + sum(jnp.dtype(dt).itemsize for dt in out_dtypes)
    tm, tn, tk = _mm_tiles(m, n, kd, a.dtype.itemsize, b.dtype.itemsize, mn_bytes)
    nk = kd // tk
    n_ex = len(extras)
    dims = (((0,) if ta else (1,), (1,) if tb else (0,)), ((), ()))

    def finish(acc, ex_refs, out_refs):
        outs = (acc,) if epilogue is None else epilogue(acc, *[r[...] for r in ex_refs])
        for r, o in zip(out_refs, outs):
            r[...] = o.astype(r.dtype)

    def partial_product(a_ref, b_ref):
        return lax.dot_general(a_ref[...].astype(BF16), b_ref[...].astype(BF16), dims, preferred_element_type=F32)

    def body_single(*refs):
        finish(partial_product(refs[0], refs[1]), refs[2:2 + n_ex], refs[2 + n_ex:])

    def body_acc(*refs):
        acc = refs[-1]
        k = pl.program_id(2)

        @pl.when(k == 0)
        def _():
            acc[...] = partial_product(refs[0], refs[1])

        @pl.when(k > 0)
        def _():
            acc[...] += partial_product(refs[0], refs[1])

        @pl.when(k == nk - 1)
        def _():
            finish(acc[...], refs[2:2 + n_ex], refs[2 + n_ex:-1])

    a_spec = pl.BlockSpec((tk, tm), lambda i, j, k: (k, i)) if ta else pl.BlockSpec((tm, tk), lambda i, j, k: (i, k))
    b_spec = pl.BlockSpec((tn, tk), lambda i, j, k: (j, k)) if tb else pl.BlockSpec((tk, tn), lambda i, j, k: (k, j))
    mn_spec = pl.BlockSpec((tm, tn), lambda i, j, k: (i, j))
    out_spec, out_rows = mn_spec, m
    if b_is_w_in_t and tb:
        b_spec = pl.BlockSpec((pl.Element(tn), pl.Element(tk)),
                              lambda i, j, k: (_w_in_row_of(j, tn), pl.multiple_of(k * tk, LANE)))
    elif b_is_w_in_t:
        b_spec = pl.BlockSpec((pl.Element(tk), pl.Element(tn)),
                              lambda i, j, k: (_w_in_row_of(k, tk), pl.multiple_of(j * tn, LANE)))
    if out_is_w_in_t:
        out_spec, out_rows = pl.BlockSpec((pl.Element(tm), pl.Element(tn)),
                                          lambda i, j, k: (_w_in_row_of(i, tm), pl.multiple_of(j * tn, LANE))), D_IN
    outs = _hosted_call(
        body_single if nk == 1 else body_acc, rider, name=name, grid=(m // tm, n // tn, nk),
        in_specs=[a_spec, b_spec] + [mn_spec] * n_ex,
        out_specs=[out_spec] * len(out_dtypes),
        out_shape=[jax.ShapeDtypeStruct((out_rows, n), dt) for dt in out_dtypes],
        scratch_shapes=[] if nk == 1 else [pltpu.VMEM((tm, tn), F32)],
        args=(a, b, *extras))
    return outs[0] if len(outs) == 1 else outs


ROW_BLOCK = 256


def _row_spec(width=D_MODEL):
    return pl.BlockSpec((ROW_BLOCK, width), lambda i: (i, 0))


def _vec_spec(width=D_MODEL):
    return pl.BlockSpec((1, width), lambda i: (0, 0))


def _rms_fwd(x, g, name):
    def body(x_ref, g_ref, h_ref):
        xf = x_ref[...]
        h_ref[...] = (xf * lax.rsqrt(jnp.mean(xf * xf, axis=-1, keepdims=True) + EPS) * g_ref[...]).astype(BF16)

    t = x.shape[0]
    return pl.pallas_call(
        body, name=name, grid=(t // ROW_BLOCK,), in_specs=[_row_spec(), _vec_spec()], out_specs=_row_spec(),
        out_shape=jax.ShapeDtypeStruct((t, D_MODEL), BF16), compiler_params=_cparams(("parallel",)),
    )(x, g)


def _rms_bwd_math(xf, g, dh):
    rstd = lax.rsqrt(jnp.mean(xf * xf, axis=-1, keepdims=True) + EPS)
    xhat = xf * rstd
    dxhat = dh * g
    dx = rstd * (dxhat - xhat * jnp.mean(dxhat * xhat, axis=-1, keepdims=True))
    dg = jnp.sum(dh * xhat, axis=0, keepdims=True)
    return dx, dg


def _rms_bwd(x, g, dh, dres, name):
    def body(x_ref, g_ref, dh_ref, dres_ref, dx_ref, dg_ref):
        dx, dg = _rms_bwd_math(x_ref[...], g_ref[...], dh_ref[...].astype(F32))
        dx_ref[...] = dres_ref[...] + dx

        @pl.when(pl.program_id(0) == 0)
        def _():
            dg_ref[...] = jnp.zeros_like(dg_ref)

        dg_ref[...] += dg

    t = x.shape[0]
    return pl.pallas_call(
        body, name=name, grid=(t // ROW_BLOCK,),
        in_specs=[_row_spec(), _vec_spec(), _row_spec(), _row_spec()], out_specs=[_row_spec(), _vec_spec()],
        out_shape=[jax.ShapeDtypeStruct((t, D_MODEL), F32), jax.ShapeDtypeStruct((1, D_MODEL), F32)],
        compiler_params=_cparams(("arbitrary",)),
    )(x, g, dh, dres)


def _loss_head(x3, g, target):
    def body(x_ref, g_ref, t_ref, dx_ref, dg_ref, loss_ref):
        xf, gg = x_ref[...], g_ref[...]
        rstd = lax.rsqrt(jnp.mean(xf * xf, axis=-1, keepdims=True) + EPS)
        err = xf * rstd * gg - t_ref[...]
        dx, dg = _rms_bwd_math(xf, gg, err * (1.0 / D_MODEL))
        dx_ref[...] = dx

        @pl.when(pl.program_id(0) == 0)
        def _():
            dg_ref[...] = jnp.zeros_like(dg_ref)
            loss_ref[...] = jnp.zeros_like(loss_ref)

        dg_ref[...] += dg
        part = jnp.sum(jnp.sum(err * err, axis=-1, keepdims=True), axis=0, keepdims=True) * (0.5 / D_MODEL)
        loss_ref[...] += jnp.broadcast_to(part, loss_ref.shape)

    t = x3.shape[0]
    return pl.pallas_call(
        body, name="loss_head", grid=(t // ROW_BLOCK,),
        in_specs=[_row_spec(), _vec_spec(), _row_spec()], out_specs=[_row_spec(), _vec_spec(), _vec_spec(LANE)],
        out_shape=[jax.ShapeDtypeStruct((t, D_MODEL), F32), jax.ShapeDtypeStruct((1, D_MODEL), F32),
                   jax.ShapeDtypeStruct((1, LANE), F32)],
        compiler_params=_cparams(("arbitrary",)),
    )(x3, g, target)


def _ple_bwd(dx3, gpre, pp):
    def body(dx_ref, gp_ref, pp_ref, dgp_ref, dpp_ref):
        dx, sg = dx_ref[...], _sigmoid(gp_ref[...])
        dpp_ref[...] = (dx * sg).astype(BF16)
        dgp_ref[...] = (dx * pp_ref[...] * sg * (1.0 - sg)).astype(BF16)

    t = dx3.shape[0]
    return pl.pallas_call(
        body, name="ple_bwd", grid=(t // ROW_BLOCK,), in_specs=[_row_spec()] * 3, out_specs=[_row_spec()] * 2,
        out_shape=[jax.ShapeDtypeStruct((t, D_MODEL), BF16)] * 2, compiler_params=_cparams(("parallel",)),
    )(dx3, gpre, pp)


CONV_COLS = 256


def _shift_down(x, s):
    if s == 0:
        return x
    return jnp.where(_iota2(x.shape, 0) >= s, pltpu.roll(x, s, 0), 0.0)


def _shift_up(x, s):
    if s == 0:
        return x
    rows = x.shape[0]
    return jnp.where(_iota2(x.shape, 0) < rows - s, pltpu.roll(x, rows - s, 0), 0.0)


def _conv_taps(xf):
    return [_shift_down(xf, DN_CONV - 1 - j) for j in range(DN_CONV)]


def _conv_pre(taps, w):
    return sum(tap * w[j:j + 1, :] for j, tap in enumerate(taps))


def _conv_fwd(zb, conv_w, bsz, seq):
    def body(x_ref, w_ref, y_ref):
        y_ref[...] = _silu(_conv_pre(_conv_taps(x_ref[...]), w_ref[...]))

    nblk = DN_QKV // CONV_COLS
    return pl.pallas_call(
        body, name="conv_fwd", grid=(bsz, nblk),
        in_specs=[pl.BlockSpec((seq, CONV_COLS), lambda b, j: (b, ZB_DQKV // CONV_COLS + j)),
                  pl.BlockSpec((DN_CONV, CONV_COLS), lambda b, j: (0, j))],
        out_specs=pl.BlockSpec((seq, CONV_COLS), lambda b, j: (b, j)),
        out_shape=jax.ShapeDtypeStruct((bsz * seq, DN_QKV), F32),
        compiler_params=_cparams(("parallel", "parallel")),
    )(zb, conv_w)


def _conv_bwd(zb, conv_w, dact, bsz, seq):
    def body(x_ref, w_ref, dy_ref, dx_ref, dw_ref):
        taps, w = _conv_taps(x_ref[...]), w_ref[...]
        c = _conv_pre(taps, w)
        sg = _sigmoid(c)
        dc = dy_ref[...].astype(F32) * sg * (1.0 + c * (1.0 - sg))
        dx = sum(_shift_up(dc, DN_CONV - 1 - j) * w[j:j + 1, :] for j in range(DN_CONV))
        dx_ref[...] = dx.astype(BF16)
        dw = jnp.concatenate([jnp.sum(dc * tap, axis=0, keepdims=True) for tap in taps], axis=0)

        @pl.when(pl.program_id(1) == 0)
        def _():
            dw_ref[...] = jnp.zeros_like(dw_ref)

        dw_ref[...] += dw

    nblk = DN_QKV // CONV_COLS
    return pl.pallas_call(
        body, name="conv_bwd", grid=(nblk, bsz),
        in_specs=[pl.BlockSpec((seq, CONV_COLS), lambda j, b: (b, ZB_DQKV // CONV_COLS + j)),
                  pl.BlockSpec((DN_CONV, CONV_COLS), lambda j, b: (0, j)),
                  pl.BlockSpec((seq, CONV_COLS), lambda j, b: (b, j))],
        out_specs=[pl.BlockSpec((seq, CONV_COLS), lambda j, b: (b, j)),
                   pl.BlockSpec((DN_CONV, CONV_COLS), lambda j, b: (0, j))],
        out_shape=[jax.ShapeDtypeStruct((bsz * seq, DN_QKV), BF16), jax.ShapeDtypeStruct((DN_CONV, DN_QKV), F32)],
        compiler_params=_cparams(("parallel", "arbitrary")),
    )(zb, conv_w, dact)


MESH_IDS = pl.DeviceIdType.MESH
ANY_SPEC = pl.BlockSpec(memory_space=pl.ANY)


COMM_SCRATCH = (pltpu.SemaphoreType.DMA((7,)), pltpu.SemaphoreType.DMA((7,)), pltpu.SemaphoreType.DMA)


def _gather_phases(x_ref, out_ref, send_sems, recv_sems, local_sem):
    mx, my, mc = lax.axis_index("x"), lax.axis_index("y"), lax.axis_index("c")
    me, sibling = (mx, my, mc), (mx, my, 1 - mc)
    chips = [(1 - mx, my), (mx, 1 - my), (1 - mx, 1 - my)]

    def slot(px, py, pc):
        return out_ref.at[4 * px + 2 * py + pc]

    def copy(k, block, to, src=None):
        return pltpu.make_async_remote_copy(
            src_ref=slot(*block) if src is None else src, dst_ref=slot(*block),
            send_sem=send_sems.at[k], recv_sem=recv_sems.at[k], device_id=to, device_id_type=MESH_IDS)

    def mine():
        return pltpu.make_async_copy(x_ref, slot(*me), local_sem)

    def first():
        return [copy(0, me, sibling, src=x_ref)] + [copy(1 + j, me, (*chip, mc), src=x_ref)
                                                    for j, chip in enumerate(chips)]

    def passed():
        return [copy(4 + j, (*chip, mc), sibling) for j, chip in enumerate(chips)]

    def start():
        mine().start()
        for cp in first():
            cp.start()

    def forward():
        for j, (chip, cp) in enumerate(zip(chips, passed())):
            copy(1 + j, (*chip, mc), me).wait_recv()
            cp.start()

    def finish():
        copy(0, sibling, me).wait_recv()
        for j, chip in enumerate(chips):
            copy(4 + j, (*chip, 1 - mc), me).wait_recv()
        for cp in first() + passed():
            cp.wait_send()
        mine().wait()

    return start, forward, finish


def _scatter_phases(x_ref, out_ref, send_sems, recv_sems, local_sem, among_chips=False):
    mx, my, mc = lax.axis_index("x"), lax.axis_index("y"), lax.axis_index("c")
    n_peers = 4 if among_chips else N_DEV
    me = 2 * mx + my if among_chips else 4 * mx + 2 * my + mc

    def peer(k):
        if among_chips:
            return (mx ^ ((k >> 1) & 1), my ^ (k & 1), mc)
        return (mx ^ ((k >> 2) & 1), my ^ ((k >> 1) & 1), mc ^ (k & 1))

    def slot_of(k):
        px, py, pc = peer(k)
        return 2 * px + py if among_chips else 4 * px + 2 * py + pc

    def copy(k, src_slot, dst_slot):
        return pltpu.make_async_remote_copy(
            src_ref=x_ref.at[src_slot], dst_ref=out_ref.at[dst_slot],
            send_sem=send_sems.at[k - 1], recv_sem=recv_sems.at[k - 1],
            device_id=peer(k), device_id_type=MESH_IDS)

    def sends():
        return [copy(k, slot_of(k), me) for k in range(1, n_peers)]

    def mine():
        return pltpu.make_async_copy(x_ref.at[me], out_ref.at[me], local_sem)

    def start():
        mine().start()
        for cp in sends():
            cp.start()

    def forward():
        pass

    def finish():
        for k in range(1, n_peers):
            copy(k, me, slot_of(k)).wait_recv()
        for cp in sends():
            cp.wait_send()
        mine().wait()

    return start, forward, finish


def _pair_phases(x_ref, out_ref, send_sems, recv_sems, local_sem):
    mx, my, mc = lax.axis_index("x"), lax.axis_index("y"), lax.axis_index("c")

    def copy(side):
        return pltpu.make_async_remote_copy(
            src_ref=x_ref.at[:, side], dst_ref=out_ref, send_sem=send_sems.at[0], recv_sem=recv_sems.at[0],
            device_id=(mx, my, 1 - mc), device_id_type=MESH_IDS)

    def start():
        copy(1 - mc).start()

    def forward():
        pass

    def finish():
        copy(mc).wait_recv()
        copy(1 - mc).wait_send()

    return start, forward, finish


class _Rider:
    def __init__(self, phases, x, out_shape):
        self.phases, self.x, self.out_shape = phases, x, out_shape


def _gather_rider(x):
    return _Rider(_gather_phases, x, jax.ShapeDtypeStruct((N_DEV,) + x.shape, x.dtype))


def _scatter_rider(x):
    return _Rider(_scatter_phases, x, jax.ShapeDtypeStruct(x.shape, x.dtype))


def _chip_scatter_rider(x):
    return _Rider(functools.partial(_scatter_phases, among_chips=True), x, jax.ShapeDtypeStruct(x.shape, x.dtype))


def _pair_rider(x):
    return _Rider(_pair_phases, x, jax.ShapeDtypeStruct((x.shape[0],) + x.shape[2:], x.dtype))


def _exchange(rider, name):
    def body(x_ref, out_ref, send_sems, recv_sems, local_sem):
        for phase in rider.phases(x_ref, out_ref, send_sems, recv_sems, local_sem):
            phase()

    return pl.pallas_call(body, name=name, out_shape=rider.out_shape, in_specs=[ANY_SPEC], out_specs=ANY_SPEC,
                          scratch_shapes=list(COMM_SCRATCH))(rider.x)


def _all_gather(x, name):
    return _exchange(_gather_rider(x), name)


def _all_to_all(x, name):
    return _exchange(_scatter_rider(x), name)


def _hosted_call(body, rider, *, name, grid, in_specs, out_specs, out_shape, scratch_shapes, args):
    if rider is None:
        return pl.pallas_call(body, name=name, grid=grid, in_specs=in_specs, out_specs=out_specs, out_shape=out_shape,
                              scratch_shapes=scratch_shapes, compiler_params=_cparams(("arbitrary",) * len(grid)))(*args)
    n_in, n_out, n_scr = len(in_specs), len(out_specs), len(scratch_shapes)
    total = math.prod(grid)

    def riding(*refs):
        host_in, x_ref = refs[:n_in], refs[n_in]
        host_out, out_ref = refs[n_in + 1:n_in + 1 + n_out], refs[n_in + 1 + n_out]
        host_scr = refs[n_in + 2 + n_out:n_in + 2 + n_out + n_scr]
        start, forward, finish = rider.phases(x_ref, out_ref, *refs[n_in + 2 + n_out + n_scr:])
        step = 0
        for axis, size in enumerate(grid):
            step = step * size + pl.program_id(axis)
        pl.when(step == 0)(start)
        pl.when(step == (3 * total) // 4)(forward)
        body(*host_in, *host_out, *host_scr)
        pl.when(step == total - 1)(finish)

    return pl.pallas_call(
        riding, name=name, grid=grid, in_specs=list(in_specs) + [ANY_SPEC], out_specs=list(out_specs) + [ANY_SPEC],
        out_shape=list(out_shape) + [rider.out_shape], scratch_shapes=list(scratch_shapes) + list(COMM_SCRATCH),
        compiler_params=_cparams(("arbitrary",) * len(grid)))(*args, rider.x)


def _adamw_math(w, g, m, v):
    m = ADAM_B1 * m + (1.0 - ADAM_B1) * g
    v = ADAM_B2 * v + (1.0 - ADAM_B2) * jnp.square(g)
    m_hat = m / (1.0 - ADAM_B1 ** ADAM_STEP)
    v_hat = v / (1.0 - ADAM_B2 ** ADAM_STEP)
    delta = -ADAM_LR * (m_hat / (jnp.sqrt(v_hat) + ADAM_EPS) + ADAM_WD * w)
    return delta, m, v


ADAM_ROWS = 128


def _elementwise_tile(rows, cols):
    if rows % ADAM_ROWS == 0:
        return ADAM_ROWS, cols
    return rows, (2 * LANE if cols % (2 * LANE) == 0 else cols)


def _add_blocks(a, b):
    g, rows, cols = a.shape
    tr, tc = _elementwise_tile(rows, cols)

    def body(a_ref, b_ref, o_ref):
        o_ref[...] = (a_ref[...].astype(F32) + b_ref[...].astype(F32)).astype(o_ref.dtype)

    blk = pl.BlockSpec((None, tr, tc), lambda k, i, j: (k, i, j))
    return pl.pallas_call(body, name="add_pair_blocks", grid=(g, rows // tr, cols // tc), in_specs=[blk, blk],
                          out_specs=blk, out_shape=jax.ShapeDtypeStruct(a.shape, a.dtype),
                          compiler_params=_cparams(("parallel", "parallel", "parallel")))(a, b)


def _adamw_reduce(w, m, v, parts, row0, name):
    rows, cols = w.shape
    n_parts = parts.shape[0]
    tr, tc = _elementwise_tile(rows, cols)
    r0 = row0 // tr

    def body(w_ref, m_ref, v_ref, *refs):
        part_refs, (g_ref, d_ref, nm_ref, nv_ref) = refs[:n_parts], refs[n_parts:]
        g = part_refs[0][...].astype(F32)
        for r in part_refs[1:]:
            g = g + r[...].astype(F32)
        delta, nm, nv = _adamw_math(w_ref[...], g, m_ref[...], v_ref[...])
        g_ref[...] = g
        d_ref[...] = delta
        nm_ref[...] = nm
        nv_ref[...] = nv

    blk = pl.BlockSpec((tr, tc), lambda i, j: (i, j))
    part_specs = [pl.BlockSpec((None, tr, tc), functools.partial(lambda i, j, k: (k, r0 + i, j), k=k))
                  for k in range(n_parts)]
    return pl.pallas_call(
        body, name=name, grid=(rows // tr, cols // tc), in_specs=[blk] * 3 + part_specs, out_specs=[blk] * 4,
        out_shape=[jax.ShapeDtypeStruct(w.shape, F32)] * 4, compiler_params=_cparams(("parallel", "parallel")),
    )(w, m, v, *([parts] * n_parts))


def _small_reduce(gathered):
    r = gathered.shape[1]

    def body(g_ref, o_ref):
        g = g_ref[0]
        for k in range(1, N_DEV):
            g = g + g_ref[k]
        o_ref[...] = g

    return pl.pallas_call(body, name="small_grad_reduce", out_shape=jax.ShapeDtypeStruct((r, LANE), F32))(gathered)


def _adamw_small(w, m, v, g):
    def body(w_ref, m_ref, v_ref, g_ref, d_ref, nm_ref, nv_ref):
        d_ref[...], nm_ref[...], nv_ref[...] = _adamw_math(w_ref[...], g_ref[...], m_ref[...], v_ref[...])

    return pl.pallas_call(body, name="adamw_small", out_shape=[jax.ShapeDtypeStruct(w.shape, F32)] * 3)(w, m, v, g)


def _pack_rows(arrays):
    rows = [jnp.pad(a.reshape(-1), (0, -a.size % LANE)).reshape(-1, LANE) for a in arrays]
    out = jnp.concatenate(rows, axis=0)
    return jnp.pad(out, ((0, -out.shape[0] % 8), (0, 0)))


def _unpack_rows(packed, shapes):
    out, r = [], 0
    for shp in shapes:
        size = math.prod(shp)
        nrows = -(-size // LANE)
        out.append(packed[r:r + nrows].reshape(-1)[:size].reshape(shp))
        r += nrows
    return out


def _add_residual(acc, res):
    return (res + acc,)


def _by_cols(g):
    return g.reshape(g.shape[0], N_DEV, -1).transpose(1, 0, 2)


def _from_cols(blocks):
    return blocks.transpose(1, 0, 2).reshape(blocks.shape[1], -1)


ROWS_OUT = D_MODEL // N_DEV

W_IN_SEGMENTS = ((0, WI_LR, "big", 0), (WI_LR, WI_DQKV, "gates", ZS_LR), (WI_DQKV, WI_DA, "big", ZB_DQKV),
                 (WI_DA, WI_DB, "gates", ZS_DA), (WI_DB, WI_GA, "gates", ZS_DB), (WI_GA, D_IN, "big", ZB_GA))


def _w_in_gate_rows(wt):
    parts = []
    for lo, hi, which, _ in W_IN_SEGMENTS:
        if which == "gates":
            parts += [wt[lo:hi], jnp.zeros((LANE - (hi - lo), wt.shape[1]), wt.dtype)]
    return jnp.concatenate(parts, axis=0)


def _fill_gate_rows(dw_in_t, dws_t):
    for lo, hi, which, first in W_IN_SEGMENTS:
        if which == "gates":
            dw_in_t = lax.dynamic_update_slice(dw_in_t, dws_t[first:first + hi - lo], (lo, 0))
    return dw_in_t


def _local_step(x, p, target, w, up_shard, rows_shard):
    bsz, seq, _ = x.shape
    t, nc = bsz * seq, seq // CHUNK
    x0, p2, tgt = x.reshape(t, D_MODEL), p.reshape(t, PLE_DIM), target.reshape(t, D_MODEL)

    h = _rms_fwd(x0, w["g_mix"], "rms_mix")
    zb, up_blocks = _mm(h, w["w_in_t"], tb=True, b_is_w_in_t=True, name="in_proj", rider=_gather_rider(up_shard))
    w_up = _from_cols(up_blocks)
    zs = _mm(h, w["ws_t"], tb=True, name="in_proj_gates")
    act = _conv_fwd(zb, w["conv"], bsz, seq)
    mix_gla, gla_states = _gla_fwd(zb, zs, w["w2p"], w["gla_b"], w["gla_norm"], bsz, nc)
    mixed, dn_states, dn_t_invs, row_blocks = _dn_fwd(act, zb, zs, w["alog_b"], w["dtb_b"], w["dn_norm"], mix_gla,
                                                      bsz, nc, rider=_gather_rider(rows_shard))
    w_out = row_blocks[:, :ROWS_OUT].reshape(D_MODEL, D_MODEL)
    w_pg = row_blocks[:, ROWS_OUT:2 * ROWS_OUT].reshape(D_MODEL, D_MODEL)
    w_down = row_blocks[:, 2 * ROWS_OUT:].reshape(D_FF, D_MODEL)
    x1 = _mm(mixed, w_out, epilogue=_add_residual, extras=(x0,), name="out_proj")
    h2 = _rms_fwd(x1, w["g_mlp"], "rms_mlp")
    u, a = _mm(h2, w_up, out_dtypes=(BF16, BF16), name="mlp_up",
               epilogue=lambda acc: (acc, jnp.square(jnp.maximum(acc, 0.0))))
    x2 = _mm(a, w_down, epilogue=_add_residual, extras=(x1,), name="mlp_down")
    h3 = _rms_fwd(x2, w["g_ple"], "rms_ple")
    pp = _mm(p2, w["w_pp"], name="ple_proj")
    gpre, x3 = _mm(h3, w_pg, out_dtypes=(F32, F32), extras=(x2, pp), name="ple_gate",
                   epilogue=lambda acc, res, proj: (acc, res + _sigmoid(acc) * proj))
    dx3, dg_final, loss = _loss_head(x3, w["g_final"], tgt)

    dgpre, dpp = _ple_bwd(dx3, gpre, pp)
    dw_pp = _mm(p2, dpp, ta=True, out_dtypes=(BF16,), name="d_w_ple_proj")
    dw_pg = _mm(h3, dgpre, ta=True, out_dtypes=(BF16,), name="d_w_ple_gate")
    dh3 = _mm(dgpre, w_pg, tb=True, name="d_h_ple")
    dx2, dg_ple = _rms_bwd(x2, w["g_ple"], dh3, dx3, "rms_ple_bwd")
    du = _mm(dx2, w_down, tb=True, out_dtypes=(BF16,), extras=(u,), name="d_mlp_hidden",
             epilogue=lambda acc, uu: (acc * (2.0 * jnp.maximum(uu.astype(F32), 0.0)),))
    dw_down = _mm(a, dx2, ta=True, out_dtypes=(BF16,), name="d_w_down")
    dw_up = _mm(h2, du, ta=True, out_dtypes=(BF16,), name="d_w_up")
    dh2 = _mm(du, w_up, tb=True, name="d_h_mlp")
    dx1, dg_mlp = _rms_bwd(x1, w["g_mlp"], dh2, dx2, "rms_mlp_bwd")
    dmixed = _mm(dx1, w_out, tb=True, out_dtypes=(BF16,), name="d_mixed")
    dw_out = _mm(mixed, dx1, ta=True, out_dtypes=(BF16,), name="d_w_out")

    d_rows = jnp.concatenate([dw_out.reshape(N_DEV, ROWS_OUT, D_MODEL), dw_pg.reshape(N_DEV, ROWS_OUT, D_MODEL),
                              dw_down.reshape(N_DEV, D_FF // N_DEV, D_MODEL)], axis=1)
    (dact, dza, dzb_, dgz, dgb, dal, ddt, ddn, recv_rows) = _dn_bwd(
        act, zb, zs, w["alog_b"], w["dtb_b"], w["dn_norm"], dn_states, dn_t_invs, dmixed, bsz, nc,
        rider=_scatter_rider(d_rows))
    (gdq, gdk, gdv, dgg, dga, dlr, dw2, dgla_b, dgla_norm, recv_up) = _gla_bwd(
        zb, zs, w["w2p"], w["gla_b"], w["gla_norm"], gla_states, dmixed, bsz, nc,
        rider=_scatter_rider(_by_cols(dw_up)))
    dqkv, dconv = _conv_bwd(zb, w["conv"], dact, bsz, seq)
    dzb = jnp.concatenate([gdq, gdk, gdv, dgg, dqkv, dgz, dga, dgb], axis=1)
    dzs = jnp.concatenate([dlr, dza, dzb_], axis=1)
    dw_in_t = _mm(dzb, h, ta=True, out_dtypes=(BF16,), out_is_w_in_t=True, name="d_w_in")
    dws_t = _mm(dzs, h, ta=True, out_dtypes=(BF16,), name="d_w_in_gates")
    by_chip = _fill_gate_rows(dw_in_t, dws_t).reshape(N_DEV // 2, 2, D_IN_SHARD, D_MODEL)
    from_sibling = _exchange(_pair_rider(by_chip), "pair_d_w_in")
    mine = lax.dynamic_index_in_dim(by_chip, lax.axis_index("c"), axis=1, keepdims=False)
    chip_sums = _add_blocks(mine, from_sibling)
    dh_gates = _mm(dzs, w["ws_t"], name="d_h_mix_gates")
    dh, recv_in = _mm(dzb, w["w_in_t"], b_is_w_in_t=True, epilogue=_add_residual, extras=(dh_gates,), name="d_h_mix",
                      rider=_chip_scatter_rider(chip_sums))
    gx, dg_mix = _rms_bwd(x0, w["g_mix"], dh, dx1, "rms_mix_bwd")

    dgla_w2 = dw2[:, :GLA_LOWRANK, :].transpose(1, 0, 2).reshape(GLA_LOWRANK, GLA_QK)
    return dict(
        loss=loss[0, 0], grad_x=gx.reshape(x.shape), recv_in=recv_in, recv_up=recv_up, recv_rows=recv_rows,
        w_ple_proj=dw_pp,
        g_mix=dg_mix, gla_b=dgla_b.reshape(1, GLA_QK), gla_norm=dgla_norm, dn_norm=ddn, g_mlp=dg_mlp, g_ple=dg_ple,
        g_final=dg_final, gla_w2=dgla_w2, dn_conv=dconv,
        dn_a_log=dal[:, :DN_HEADS], dn_dt_bias=ddt[:, :DN_HEADS])


def _first_weights(g_mix, w_in, gla_w2, gla_b, gla_norm, dn_conv, dn_a_log, dn_dt_bias, dn_norm, g_mlp, g_ple,
                   w_ple_proj, g_final):
    w_in_t = _all_gather(jnp.swapaxes(w_in[0], 0, 1).astype(BF16), "gather_w_in").reshape(D_IN, D_MODEL)
    w_pp = _all_gather(w_ple_proj[0].astype(BF16), "gather_w_ple_proj")
    small = _all_gather(_pack_rows([gla_w2[0], dn_conv[0]]), "gather_w_small")
    n_w2 = GLA_LOWRANK * GLA_QK // N_DEV // LANE
    n_cv = DN_CONV * DN_QKV // N_DEV // LANE
    w2 = small[:, :n_w2].reshape(N_DEV, GLA_LOWRANK, GLA_QK // N_DEV).transpose(1, 0, 2).reshape(GLA_LOWRANK, GLA_QK)
    conv = small[:, n_w2:n_w2 + n_cv].reshape(N_DEV, DN_CONV, DN_QKV // N_DEV).transpose(1, 0, 2).reshape(DN_CONV, DN_QKV)

    return dict(
        w_in_t=w_in_t, ws_t=_w_in_gate_rows(w_in_t), w_pp=_from_cols(w_pp),
        w2p=jnp.pad(w2, ((0, LANE - GLA_LOWRANK), (0, 0))), conv=conv,
        alog_b=jnp.pad(dn_a_log, ((0, 0), (0, LANE - DN_HEADS))),
        dtb_b=jnp.pad(dn_dt_bias, ((0, 0), (0, LANE - DN_HEADS))),
        g_mix=g_mix, gla_b=gla_b, gla_norm=gla_norm, dn_norm=dn_norm, g_mlp=g_mlp, g_ple=g_ple,
        g_final=g_final.reshape(1, D_MODEL))


def kernel(x, p, g_mix, w_in, gla_w2, gla_b, gla_norm, dn_conv, dn_a_log, dn_dt_bias, dn_norm, w_out, g_mlp, w_up, w_down, g_ple, w_ple_gate, w_ple_proj, g_final, loss_target, m_g_mix, m_w_in, m_gla_w2, m_gla_b, m_gla_norm, m_dn_conv, m_dn_a_log, m_dn_dt_bias, m_dn_norm, m_w_out, m_g_mlp, m_w_up, m_w_down, m_g_ple, m_w_ple_gate, m_w_ple_proj, m_g_final, v_g_mix, v_w_in, v_gla_w2, v_gla_b, v_gla_norm, v_dn_conv, v_dn_a_log, v_dn_dt_bias, v_dn_norm, v_w_out, v_g_mlp, v_w_up, v_w_down, v_g_ple, v_w_ple_gate, v_w_ple_proj, v_g_final):
    names = ["g_mix", "w_in", "gla_w2", "gla_b", "gla_norm", "dn_conv", "dn_a_log", "dn_dt_bias", "dn_norm", "w_out",
             "g_mlp", "w_up", "w_down", "g_ple", "w_ple_gate", "w_ple_proj", "g_final"]
    ws = dict(zip(names, (g_mix, w_in, gla_w2, gla_b, gla_norm, dn_conv, dn_a_log, dn_dt_bias, dn_norm, w_out, g_mlp,
                          w_up, w_down, g_ple, w_ple_gate, w_ple_proj, g_final)))
    ms = dict(zip(names, (m_g_mix, m_w_in, m_gla_w2, m_gla_b, m_gla_norm, m_dn_conv, m_dn_a_log, m_dn_dt_bias,
                          m_dn_norm, m_w_out, m_g_mlp, m_w_up, m_w_down, m_g_ple, m_w_ple_gate, m_w_ple_proj,
                          m_g_final)))
    vs = dict(zip(names, (v_g_mix, v_w_in, v_gla_w2, v_gla_b, v_gla_norm, v_dn_conv, v_dn_a_log, v_dn_dt_bias,
                          v_dn_norm, v_w_out, v_g_mlp, v_w_up, v_w_down, v_g_ple, v_w_ple_gate, v_w_ple_proj,
                          v_g_final)))
    me = 4 * lax.axis_index("x") + 2 * lax.axis_index("y") + lax.axis_index("c")

    first = _first_weights(g_mix, w_in, gla_w2, gla_b, gla_norm, dn_conv, dn_a_log, dn_dt_bias, dn_norm, g_mlp,
                           g_ple, w_ple_proj, g_final)
    rows_shard = jnp.concatenate([w_out[0], w_ple_gate[0], w_down[0]], axis=0).astype(BF16)
    r = _local_step(x, p[0], loss_target, first, w_up[0].astype(BF16), rows_shard)
    loss = lax.psum(r["loss"], ("x", "y", "c"))

    grads, deltas, new_m, new_v = {}, {}, {}, {}

    def big(name, parts, row0=0):
        g, d, nm, nv = _adamw_reduce(ws[name][0], ms[name][0], vs[name][0], parts, row0, "adamw_" + name)
        grads[name], deltas[name], new_m[name], new_v[name] = g[None], d[None], nm[None], nv[None]

    t_outs = _adamw_reduce(*[jnp.swapaxes(d["w_in"][0], 0, 1) for d in (ws, ms, vs)], r["recv_in"], 0, "adamw_w_in")
    grads["w_in"], deltas["w_in"], new_m["w_in"], new_v["w_in"] = [jnp.swapaxes(o, 0, 1)[None] for o in t_outs]
    big("w_up", r["recv_up"])
    big("w_ple_proj", _all_to_all(_by_cols(r["w_ple_proj"]), "scatter_d_w_ple_proj"))
    big("w_out", r["recv_rows"], 0)
    big("w_ple_gate", r["recv_rows"], ROWS_OUT)
    big("w_down", r["recv_rows"], 2 * ROWS_OUT)

    vec_names = ["g_mix", "gla_b", "gla_norm", "dn_norm", "g_mlp", "g_ple", "g_final", "dn_a_log", "dn_dt_bias"]
    gathered_names = vec_names + ["gla_w2", "dn_conv"]
    total = _small_reduce(_all_gather(_pack_rows([r[n] for n in gathered_names]), "gather_small_grads"))
    sg = dict(zip(gathered_names, _unpack_rows(total, [r[n].shape for n in gathered_names])))
    sg["g_final"] = sg["g_final"].reshape(D_MODEL)
    sg["gla_w2"] = lax.dynamic_slice_in_dim(sg["gla_w2"], me * (GLA_QK // N_DEV), GLA_QK // N_DEV, axis=1)
    sg["dn_conv"] = lax.dynamic_slice_in_dim(sg["dn_conv"], me * (DN_QKV // N_DEV), DN_QKV // N_DEV, axis=1)
    small_names = gathered_names
    shapes = [ws[n].shape for n in small_names]
    d_s, m_s, v_s = _adamw_small(_pack_rows([ws[n] for n in small_names]), _pack_rows([ms[n] for n in small_names]),
                                 _pack_rows([vs[n] for n in small_names]), _pack_rows([sg[n] for n in small_names]))
    for n, d, nm, nv in zip(small_names, _unpack_rows(d_s, shapes), _unpack_rows(m_s, shapes), _unpack_rows(v_s, shapes)):
        grads[n], deltas[n], new_m[n], new_v[n] = sg[n].reshape(ws[n].shape), d, nm, nv

    return (loss, r["grad_x"], *[grads[n] for n in names], *[deltas[n] for n in names],
            *[new_m[n] for n in names], *[new_v[n] for n in names])
```

```python
import functools
import math

import jax
import jax.numpy as jnp
from jax import lax
from jax.experimental import pallas as pl
from jax.experimental.pallas import tpu as pltpu

F32 = jnp.float32
BF16 = jnp.bfloat16

N_DEV = 8
D_MODEL = 2048
CHUNK = 64
PLE_DIM = 256
EPS = 1e-6
GLA_HEADS = 4
GLA_DK = 256
GLA_DV = 512
GLA_LOWRANK = 16
GLA_TAU = 16.0
DN_HEADS = 16
DN_D = 128
DN_CONV = 4
D_FF = 4 * D_MODEL
GLA_QK = GLA_HEADS * GLA_DK
GLA_V = GLA_HEADS * GLA_DV
DN_QKV = 3 * DN_HEADS * DN_D
D_IN = 2 * GLA_QK + 2 * GLA_V + GLA_LOWRANK + DN_QKV + D_MODEL + 2 * DN_HEADS + 2 * D_MODEL
D_IN_SHARD = D_IN // N_DEV

ADAM_LR = 0.001
ADAM_B1 = 0.9
ADAM_B2 = 0.999
ADAM_EPS = 1e-08
ADAM_WD = 0.01
ADAM_STEP = 10

LANE = 128
ZB_GQ, ZB_GK, ZB_GV, ZB_GG = 0, 1024, 2048, 4096
ZB_DQKV, ZB_DZ, ZB_GA, ZB_GB = 6144, 12288, 14336, 16384
ZB_W = 18432
ZS_LR, ZS_DA, ZS_DB = 0, 128, 256
ZS_W = 384
WI_LR = 2 * GLA_QK + 2 * GLA_V
WI_DQKV = WI_LR + GLA_LOWRANK
WI_DA = WI_DQKV + DN_QKV + D_MODEL
WI_DB = WI_DA + DN_HEADS
WI_GA = WI_DB + DN_HEADS

VMEM_LIMIT = 56 * 1024 * 1024

def _bdot(a, b, dims):
    return lax.dot_general(a.astype(BF16), b.astype(BF16), dims, preferred_element_type=F32)


def _split3(x):
    hi = x.astype(BF16)
    rest = x - hi.astype(F32)
    mid = rest.astype(BF16)
    return hi, mid, (rest - mid.astype(F32)).astype(BF16)


def _dot01(x, m, dims, x_first):
    m = m.astype(BF16)
    out = None
    for piece in _split3(x):
        d = lax.dot_general(piece, m, dims, preferred_element_type=F32) if x_first else \
            lax.dot_general(m, piece, dims, preferred_element_type=F32)
        out = d if out is None else out + d
    return out


@functools.partial(jax.custom_vjp, nondiff_argnums=(2, 3))
def _pick_dot(x, m, dims, dims_t):
    return _dot01(x, m, dims, True)


def _pick_dot_fwd(x, m, dims, dims_t):
    return _dot01(x, m, dims, True), m


def _pick_dot_bwd(dims, dims_t, m, ct):
    return _dot01(ct, m, dims_t, True), jnp.zeros_like(m)


_pick_dot.defvjp(_pick_dot_fwd, _pick_dot_bwd)


@functools.partial(jax.custom_vjp, nondiff_argnums=(2, 3))
def _left_dot(m, x, dims, dims_t):
    return _dot01(x, m, dims, False)


def _left_dot_fwd(m, x, dims, dims_t):
    return _dot01(x, m, dims, False), m


def _left_dot_bwd(dims, dims_t, m, ct):
    return jnp.zeros_like(m), _dot01(ct, m, dims_t, False)


_left_dot.defvjp(_left_dot_fwd, _left_dot_bwd)


def _dot3(a, b, dims):
    ah, bh = a.astype(BF16), b.astype(BF16)
    al, bl = (a - ah.astype(F32)).astype(BF16), (b - bh.astype(F32)).astype(BF16)
    dot = functools.partial(lax.dot_general, dimension_numbers=dims, preferred_element_type=F32)
    return dot(ah, bh) + (dot(ah, bl) + dot(al, bh))


def _sigmoid(x):
    return 1.0 / (1.0 + jnp.exp(-x))


def _silu(x):
    return x * _sigmoid(x)


def _softplus(x):
    return jnp.maximum(x, 0.0) + jnp.log(1.0 + jnp.exp(-jnp.abs(x)))


def _iota2(shape, dim):
    return lax.broadcasted_iota(jnp.int32, shape, dim)


def _cparams(sem=None):
    return pltpu.CompilerParams(dimension_semantics=sem, vmem_limit_bytes=VMEM_LIMIT)


BNN = (((2,), (1,)), ((0,), (0,)))
BNT = (((2,), (2,)), ((0,), (0,)))
BTN = (((1,), (1,)), ((0,), (0,)))


def _gla_chunk(st, q, k, v, lr, w2, b, gg, ga, gn):
    hb, c, _ = q.shape
    incl = (_iota2((c, c), 0) >= _iota2((c, c), 1))[None]
    tri = jnp.broadcast_to(incl.astype(F32), (hb, c, c))
    q = q.astype(F32) * (GLA_DK ** -0.5)
    k = k.astype(F32)
    v = v.astype(F32)
    lr_b = jnp.broadcast_to(lr[None], (hb,) + lr.shape)
    lf = -_softplus(-(_bdot(lr_b, w2, BNN) + b)) / GLA_TAU
    bcum = _left_dot(tri, lf, BNN, BTN)
    blast = jnp.sum(lf, axis=1, keepdims=True)
    q_in = q * jnp.exp(bcum)
    k_in = k * jnp.exp(-bcum)
    a = jnp.where(incl, _bdot(q_in, k_in, BNT), 0.0)
    o = _bdot(a, v, BNN) + _bdot(q_in, st, BNT)
    k_dec = k * jnp.exp(blast - bcum)
    st_new = st * jnp.exp(blast) + _bdot(v, k_dec, BTN)
    on = o * lax.rsqrt(jnp.mean(o * o, axis=-1, keepdims=True) + EPS) * gn
    res = _sigmoid(ga.astype(F32)) * on * _silu(gg.astype(F32))
    return res, st_new


def _tri_inv_raw(a):
    _, c, _ = a.shape
    eye = (_iota2((c, c), 0) == _iota2((c, c), 1)).astype(F32)[None]
    x = a
    p = eye - a
    for _ in range(5):
        x = _dot3(x, x, BNN)
        p = _dot3(p, eye + x, BNN)
    return p


def _tri_inv_bwd(t, dt):
    return (-_dot3(_dot3(t, dt, BTN), t, BNT),)


@jax.custom_vjp
def _tri_inv_given(a, t):
    return t


def _tri_inv_given_fwd(a, t):
    return t, t


def _tri_inv_given_bwd(t, dt):
    return _tri_inv_bwd(t, dt) + (jnp.zeros_like(t),)


_tri_inv_given.defvjp(_tri_inv_given_fwd, _tri_inv_given_bwd)


@functools.partial(jax.custom_vjp, nondiff_argnums=(1,))
def _column_on_lanes(z, j):
    picked = jnp.where(_iota2(z.shape, 1) == j, z, 0.0)
    return jnp.broadcast_to(jnp.sum(picked, axis=1, keepdims=True), z.shape)


def _column_on_lanes_fwd(z, j):
    return _column_on_lanes(z, j), None


def _column_on_lanes_bwd(j, _, ct):
    total = jnp.broadcast_to(jnp.sum(ct, axis=1, keepdims=True), ct.shape)
    return (jnp.where(_iota2(ct.shape, 1) == j, total, 0.0),)


_column_on_lanes.defvjp(_column_on_lanes_fwd, _column_on_lanes_bwd)


def _dn_chunk(s, qr, kr, vr, za, zb, alog, dtb, gz, gb, dn, t_saved=None):
    hb, c, _ = qr.shape
    row, col = _iota2((c, c), 0), _iota2((c, c), 1)
    incl = (row >= col)[None]
    strict = (row > col)[None]
    tri = jnp.broadcast_to(incl.astype(F32), (hb, c, c))
    eye = (row == col).astype(F32)[None]
    ones_cc = jnp.ones((hb, c, c), F32)
    lane0 = (lax.broadcasted_iota(jnp.int32, (hb, LANE, c), 1) == 0).astype(F32)

    def l2n(t):
        return t * lax.rsqrt(jnp.sum(t * t, axis=-1, keepdims=True) + EPS)

    q = l2n(qr.astype(F32)) * (DN_D ** -0.5)
    k = l2n(kr.astype(F32))
    v = vr.astype(F32)
    g_heads = -jnp.exp(alog) * _softplus(za + dtb)
    beta_heads = _sigmoid(zb)
    g = jnp.concatenate([_column_on_lanes(g_heads, j)[None] for j in range(hb)], axis=0)
    beta = jnp.concatenate([_column_on_lanes(beta_heads, j)[None] for j in range(hb)], axis=0)
    gcum = _left_dot(tri, g, BNN, BTN)
    glast = jnp.sum(g, axis=1, keepdims=True)
    cm = _pick_dot(gcum, lane0, BNN, BNT)
    rm = _left_dot(ones_cc, cm * eye, BNN, BTN)
    dec = jnp.exp(jnp.where(incl, cm - rm, -1e30))
    kb = k * beta
    a = jnp.where(strict, _bdot(kb, k, BNT) * dec, 0.0)
    t = _tri_inv_raw(a) if t_saved is None else _tri_inv_given(a, t_saved)
    egc = jnp.exp(gcum)
    u = _bdot(t, v * beta, BNN)
    w = _bdot(t, kb * egc, BNN)
    attn = jnp.where(incl, _bdot(q, k, BNT) * dec, 0.0)
    q_dec = q * egc
    k_dec = k * jnp.exp(glast - gcum)
    v_new = u - _bdot(w, s, BNN)
    o = _bdot(q_dec, s, BNN) + _bdot(attn, v_new, BNN)
    s_new = s * jnp.exp(glast) + _bdot(k_dec, v_new, BTN)
    on = o * lax.rsqrt(jnp.mean(o * o, axis=-1, keepdims=True) + EPS) * dn
    res = _sigmoid(gb.astype(F32)) * on * _silu(gz.astype(F32))
    return (res, s_new, t) if t_saved is None else (res, s_new)


def _heads(ref, n_heads, width):
    return jnp.stack([ref[:, j * width:(j + 1) * width] for j in range(n_heads)], axis=0)


def _gla_specs(nc, reverse):
    def rows(b, n):
        return b * nc + ((nc - 1 - n) if reverse else n)

    qk = lambda base: pl.BlockSpec((CHUNK, GLA_QK), lambda b, n: (rows(b, n), base // GLA_QK))
    vv = lambda base: pl.BlockSpec((CHUNK, GLA_V), lambda b, n: (rows(b, n), base // GLA_V))
    lr = lambda c: pl.BlockSpec((CHUNK, LANE), lambda b, n: (rows(b, n), c))
    full = lambda shape: pl.BlockSpec(shape, lambda b, n: (0,) * len(shape))
    return rows, qk, vv, lr, full


def _gla_inputs(q_ref, k_ref, v_ref, gg_ref, ga_ref, lr_ref, w2_ref, b_ref, gn_ref):
    return (_heads(q_ref, GLA_HEADS, GLA_DK), _heads(k_ref, GLA_HEADS, GLA_DK), _heads(v_ref, GLA_HEADS, GLA_DV),
            lr_ref[...], _heads(w2_ref, GLA_HEADS, GLA_DK), _heads(b_ref, GLA_HEADS, GLA_DK),
            _heads(gg_ref, GLA_HEADS, GLA_DV), _heads(ga_ref, GLA_HEADS, GLA_DV), gn_ref[...])


def _gla_fwd(zb, zs, w2p, gla_b, gla_norm, bsz, nc):
    t = zb.shape[0]
    rows, qk, vv, lr, full = _gla_specs(nc, False)

    def body(q_ref, k_ref, v_ref, gg_ref, ga_ref, lr_ref, w2_ref, b_ref, gn_ref, o_ref, st_ref, state):
        st = jnp.where(pl.program_id(1) > 0, state[...], 0.0)
        st_ref[0, 0] = st
        res, st_new = _gla_chunk(st, *_gla_inputs(q_ref, k_ref, v_ref, gg_ref, ga_ref, lr_ref, w2_ref, b_ref, gn_ref))
        for j in range(GLA_HEADS):
            o_ref[:, j * GLA_DV:(j + 1) * GLA_DV] = res[j]
        state[...] = st_new

    return pl.pallas_call(
        body, name="gla_fwd", grid=(bsz, nc),
        in_specs=[qk(ZB_GQ), qk(ZB_GK), vv(ZB_GV), vv(ZB_GG), vv(ZB_GA), lr(ZS_LR // LANE),
                  full((LANE, GLA_QK)), full((1, GLA_QK)), full((1, GLA_DV))],
        out_specs=[vv(0), pl.BlockSpec((1, 1, GLA_HEADS, GLA_DV, GLA_DK), lambda b, n: (b, n, 0, 0, 0))],
        out_shape=[jax.ShapeDtypeStruct((t, GLA_V), F32),
                   jax.ShapeDtypeStruct((bsz, nc, GLA_HEADS, GLA_DV, GLA_DK), F32)],
        scratch_shapes=[pltpu.VMEM((GLA_HEADS, GLA_DV, GLA_DK), F32)],
        compiler_params=_cparams(("arbitrary", "arbitrary")),
    )(zb, zb, zb, zb, zb, zs, w2p, gla_b, gla_norm)


def _gla_bwd(zb, zs, w2p, gla_b, gla_norm, states, dmixed, bsz, nc, rider=None):
    t = zb.shape[0]
    rows, qk, vv, lr, full = _gla_specs(nc, True)

    def body(q_ref, k_ref, v_ref, gg_ref, ga_ref, lr_ref, w2_ref, b_ref, gn_ref, st_ref, dm_ref,
             dq_ref, dk_ref, dv_ref, dgg_ref, dga_ref, dlr_ref, dw2_ref, db_ref, dgn_ref, dstate):
        b, n = pl.program_id(0), pl.program_id(1)

        @pl.when((b == 0) & (n == 0))
        def _():
            dw2_ref[...] = jnp.zeros_like(dw2_ref)
            db_ref[...] = jnp.zeros_like(db_ref)
            dgn_ref[...] = jnp.zeros_like(dgn_ref)

        _, vjp = jax.vjp(_gla_chunk, st_ref[0, 0],
                         *_gla_inputs(q_ref, k_ref, v_ref, gg_ref, ga_ref, lr_ref, w2_ref, b_ref, gn_ref))
        dst_in = jnp.where(n > 0, dstate[...], 0.0)
        dst, dq, dk, dv, dlr, dw2, db, dgg, dga, dgn = vjp((_heads(dm_ref, GLA_HEADS, GLA_DV).astype(F32), dst_in))
        dstate[...] = dst
        for j in range(GLA_HEADS):
            dq_ref[:, j * GLA_DK:(j + 1) * GLA_DK] = dq[j].astype(dq_ref.dtype)
            dk_ref[:, j * GLA_DK:(j + 1) * GLA_DK] = dk[j].astype(dk_ref.dtype)
            dv_ref[:, j * GLA_DV:(j + 1) * GLA_DV] = dv[j].astype(dv_ref.dtype)
            dgg_ref[:, j * GLA_DV:(j + 1) * GLA_DV] = dgg[j].astype(dgg_ref.dtype)
            dga_ref[:, j * GLA_DV:(j + 1) * GLA_DV] = dga[j].astype(dga_ref.dtype)
        dlr_ref[...] = dlr
        dw2_ref[...] += dw2
        db_ref[...] += db
        dgn_ref[...] += dgn

    return _hosted_call(
        body, rider, name="gla_bwd", grid=(bsz, nc),
        in_specs=[qk(ZB_GQ), qk(ZB_GK), vv(ZB_GV), vv(ZB_GG), vv(ZB_GA), lr(ZS_LR // LANE),
                  full((LANE, GLA_QK)), full((1, GLA_QK)), full((1, GLA_DV)),
                  pl.BlockSpec((1, 1, GLA_HEADS, GLA_DV, GLA_DK), lambda b, n: (b, nc - 1 - n, 0, 0, 0)),
                  vv(0)],
        out_specs=[qk(0), qk(0), vv(0), vv(0), vv(0), lr(0),
                   full((GLA_HEADS, LANE, GLA_DK)), full((GLA_HEADS, 1, GLA_DK)), full((1, GLA_DV))],
        out_shape=[jax.ShapeDtypeStruct((t, GLA_QK), BF16), jax.ShapeDtypeStruct((t, GLA_QK), BF16),
                   jax.ShapeDtypeStruct((t, GLA_V), BF16), jax.ShapeDtypeStruct((t, GLA_V), BF16),
                   jax.ShapeDtypeStruct((t, GLA_V), BF16), jax.ShapeDtypeStruct((t, LANE), F32),
                   jax.ShapeDtypeStruct((GLA_HEADS, LANE, GLA_DK), F32),
                   jax.ShapeDtypeStruct((GLA_HEADS, 1, GLA_DK), F32),
                   jax.ShapeDtypeStruct((1, GLA_DV), F32)],
        scratch_shapes=[pltpu.VMEM((GLA_HEADS, GLA_DV, GLA_DK), F32)],
        args=(zb, zb, zb, zb, zb, zs, w2p, gla_b, gla_norm, states, dmixed))


DN_HB = DN_HEADS


def _dn_specs(nc, reverse):
    wide = DN_HB * DN_D

    def rows(b, n, h):
        return b * nc + ((nc - 1 - n) if reverse else n)

    def col(base):
        return pl.BlockSpec((CHUNK, wide), lambda b, n, h: (rows(b, n, h), base // wide + h))

    def fixed(c):
        return pl.BlockSpec((CHUNK, LANE), lambda b, n, h: (rows(b, n, h), c))

    head = pl.BlockSpec((1, LANE), lambda b, n, h: (0, 0))
    return rows, col, fixed, head


def _lanes(j):
    return slice(j * DN_D, (j + 1) * DN_D)


def _by_head(ref):
    return jnp.stack([ref[:, _lanes(j)] for j in range(DN_HB)], axis=0)


def _dn_fwd(act, zb, zs, alog_b, dtb_b, dn_norm, mix_gla, bsz, nc, rider=None):
    t = zb.shape[0]
    rows, col, fixed, head = _dn_specs(nc, False)

    def body(q_ref, k_ref, v_ref, za_ref, zb_ref, al_ref, dt_ref, gz_ref, gb_ref, dn_ref, mg_ref,
             o_ref, st_ref, ti_ref, state):
        s = jnp.where(pl.program_id(1) > 0, state[...], 0.0)
        st_ref[0, 0] = s
        res, s_new, t_inv = _dn_chunk(s, _by_head(q_ref), _by_head(k_ref), _by_head(v_ref), za_ref[...], zb_ref[...],
                                      al_ref[...], dt_ref[...], _by_head(gz_ref), _by_head(gb_ref), dn_ref[...])
        ti_ref[0, 0] = t_inv
        for j in range(DN_HB):
            o_ref[:, _lanes(j)] = (res[j] + mg_ref[:, _lanes(j)]).astype(o_ref.dtype)
        state[...] = s_new

    return _hosted_call(
        body, rider, name="dn_fwd", grid=(bsz, nc, DN_HEADS // DN_HB),
        in_specs=[col(0), col(DN_HEADS * DN_D), col(2 * DN_HEADS * DN_D),
                  fixed(ZS_DA // LANE), fixed(ZS_DB // LANE), head, head,
                  col(ZB_DZ), col(ZB_GB), pl.BlockSpec((1, DN_D), lambda b, n, h: (0, 0)), col(0)],
        out_specs=[col(0), pl.BlockSpec((1, 1, DN_HB, DN_D, DN_D), lambda b, n, h: (b, n, h, 0, 0)),
                   pl.BlockSpec((1, 1, DN_HB, CHUNK, CHUNK), lambda b, n, h: (b, n, h, 0, 0))],
        out_shape=[jax.ShapeDtypeStruct((t, D_MODEL), BF16),
                   jax.ShapeDtypeStruct((bsz, nc, DN_HEADS, DN_D, DN_D), F32),
                   jax.ShapeDtypeStruct((bsz, nc, DN_HEADS, CHUNK, CHUNK), F32)],
        scratch_shapes=[pltpu.VMEM((DN_HEADS, DN_D, DN_D), F32)],
        args=(act, act, act, zs, zs, alog_b, dtb_b, zb, zb, dn_norm, mix_gla))


def _dn_bwd(act, zb, zs, alog_b, dtb_b, dn_norm, states, t_invs, dmixed, bsz, nc, rider=None):
    t = zb.shape[0]
    rows, col, fixed, head = _dn_specs(nc, True)

    def body(q_ref, k_ref, v_ref, za_ref, zb_ref, al_ref, dt_ref, gz_ref, gb_ref, dn_ref, st_ref, ti_ref, dm_ref,
             dact_ref, dza_ref, dzb_ref, dgz_ref, dgb_ref, dal_ref, ddt_ref, ddn_ref, dstate):
        b, n = pl.program_id(0), pl.program_id(1)

        @pl.when((b == 0) & (n == 0))
        def _():
            dal_ref[...] = jnp.zeros_like(dal_ref)
            ddt_ref[...] = jnp.zeros_like(ddt_ref)
            ddn_ref[...] = jnp.zeros_like(ddn_ref)

        fn = functools.partial(_dn_chunk, t_saved=ti_ref[0, 0])
        _, vjp = jax.vjp(fn, st_ref[0, 0], _by_head(q_ref), _by_head(k_ref), _by_head(v_ref), za_ref[...],
                         zb_ref[...], al_ref[...], dt_ref[...], _by_head(gz_ref), _by_head(gb_ref), dn_ref[...])
        ds_in = jnp.where(n > 0, dstate[...], 0.0)
        ds, dq, dk, dv, dza, dzb, dal, ddt, dgz, dgb, ddn = vjp((_by_head(dm_ref).astype(F32), ds_in))
        dstate[...] = ds
        for j in range(DN_HB):
            for part, d in enumerate((dq, dk, dv)):
                dact_ref[:, pl.ds(part * DN_HEADS * DN_D + j * DN_D, DN_D)] = d[j]
            dgz_ref[:, _lanes(j)] = dgz[j].astype(dgz_ref.dtype)
            dgb_ref[:, _lanes(j)] = dgb[j].astype(dgb_ref.dtype)
        dal_ref[...] += dal
        ddt_ref[...] += ddt
        dza_ref[...] = dza
        dzb_ref[...] = dzb
        ddn_ref[...] += ddn

    full = lambda shape: pl.BlockSpec(shape, lambda b, n, h: (0,) * len(shape))
    return _hosted_call(
        body, rider, name="dn_bwd", grid=(bsz, nc, DN_HEADS // DN_HB),
        in_specs=[col(0), col(DN_HEADS * DN_D), col(2 * DN_HEADS * DN_D),
                  fixed(ZS_DA // LANE), fixed(ZS_DB // LANE), head, head,
                  col(ZB_DZ), col(ZB_GB), pl.BlockSpec((1, DN_D), lambda b, n, h: (0, 0)),
                  pl.BlockSpec((1, 1, DN_HB, DN_D, DN_D), lambda b, n, h: (b, nc - 1 - n, h, 0, 0)),
                  pl.BlockSpec((1, 1, DN_HB, CHUNK, CHUNK), lambda b, n, h: (b, nc - 1 - n, h, 0, 0)), col(0)],
        out_specs=[pl.BlockSpec((CHUNK, DN_QKV), lambda b, n, h: (rows(b, n, h), 0)), fixed(0), fixed(0), col(0), col(0),
                   full((1, LANE)), full((1, LANE)), full((1, DN_D))],
        out_shape=[jax.ShapeDtypeStruct((t, DN_QKV), F32),
                   jax.ShapeDtypeStruct((t, LANE), F32), jax.ShapeDtypeStruct((t, LANE), F32),
                   jax.ShapeDtypeStruct((t, D_MODEL), BF16), jax.ShapeDtypeStruct((t, D_MODEL), BF16),
                   jax.ShapeDtypeStruct((1, LANE), F32), jax.ShapeDtypeStruct((1, LANE), F32),
                   jax.ShapeDtypeStruct((1, DN_D), F32)],
        scratch_shapes=[pltpu.VMEM((DN_HEADS, DN_D, DN_D), F32)],
        args=(act, act, act, zs, zs, alog_b, dtb_b, zb, zb, dn_norm, states, t_invs, dmixed))


MM_VMEM_BUDGET = 40 * 1024 * 1024
MM_TILE_PREF = (1024, 1024, 2048)


def _divisor_tile(n, cap):
    if n <= cap:
        return n
    for c in range(cap - cap % LANE, 0, -LANE):
        if n % c == 0:
            return c
    return n


def _mm_tiles(m, n, kd, a_bytes, b_bytes, mn_bytes):
    tm, tn, tk = (_divisor_tile(d, c) for d, c in zip((m, n, kd), MM_TILE_PREF))

    def need(tm, tn, tk):
        acc = 0 if tk == kd else 4 * tm * tn
        return 2 * (tm * tk * a_bytes + tk * tn * b_bytes + tm * tn * mn_bytes) + acc + 4 * tm * tn

    while need(tm, tn, tk) > MM_VMEM_BUDGET:
        if tk > 512 and tk * max(tm * a_bytes, tn * b_bytes) >= tm * tn * mn_bytes:
            tk = _divisor_tile(kd, tk // 2)
        elif tn >= tm and tn > LANE:
            tn = _divisor_tile(n, tn // 2)
        else:
            tm = _divisor_tile(m, tm // 2)
    return tm, tn, tk


def _w_in_row_of(tile, tile_rows):
    skipped = jnp.where(tile >= ZB_GA // tile_rows, WI_GA - ZB_GA, jnp.where(tile >= ZB_DQKV // tile_rows,
                                                                             WI_DQKV - ZB_DQKV, 0))
    return pl.multiple_of(tile * tile_rows + skipped, 16)


def _mm(a, b, *, ta=False, tb=False, out_dtypes=(F32,), epilogue=None, extras=(), name, rider=None,
        b_is_w_in_t=False, out_is_w_in_t=False):
    m, kd = (a.shape[1], a.shape[0]) if ta else a.shape
    n = b.shape[0] if tb else b.shape[1]
    if b_is_w_in_t:
        n, kd = (ZB_W, kd) if tb else (n, ZB_W)
    mn_bytes = sum(e.dtype.itemsize for e in extras) + sum(jnp.dtype(dt).itemsize for dt in out_dtypes)
    tm, tn, tk = _mm_tiles(m, n, kd, a.dtype.itemsize, b.dtype.itemsize, mn_bytes)
    nk = kd // tk
    n_ex = len(extras)
    dims = (((0,) if ta else (1,), (1,) if tb else (0,)), ((), ()))

    def finish(acc, ex_refs, out_refs):
        outs = (acc,) if epilogue is None else epilogue(acc, *[r[...] for r in ex_refs])
        for r, o in zip(out_refs, outs):
            r[...] = o.astype(r.dtype)

    def partial_product(a_ref, b_ref):
        return lax.dot_general(a_ref[...].astype(BF16), b_ref[...].astype(BF16), dims, preferred_element_type=F32)

    def body_single(*refs):
        finish(partial_product(refs[0], refs[1]), refs[2:2 + n_ex], refs[2 + n_ex:])

    def body_acc(*refs):
        acc = refs[-1]
        k = pl.program_id(2)

        @pl.when(k == 0)
        def _():
            acc[...] = partial_product(refs[0], refs[1])

        @pl.when(k > 0)
        def _():
            acc[...] += partial_product(refs[0], refs[1])

        @pl.when(k == nk - 1)
        def _():
            finish(acc[...], refs[2:2 + n_ex], refs[2 + n_ex:-1])

    a_spec = pl.BlockSpec((tk, tm), lambda i, j, k: (k, i)) if ta else pl.BlockSpec((tm, tk), lambda i, j, k: (i, k))
    b_spec = pl.BlockSpec((tn, tk), lambda i, j, k: (j, k)) if tb else pl.BlockSpec((tk, tn), lambda i, j, k: (k, j))
    mn_spec = pl.BlockSpec((tm, tn), lambda i, j, k: (i, j))
    out_spec, out_rows = mn_spec, m
    if b_is_w_in_t and tb:
        b_spec = pl.BlockSpec((pl.Element(tn), pl.Element(tk)),
                              lambda i, j, k: (_w_in_row_of(j, tn), pl.multiple_of(k * tk, LANE)))
    elif b_is_w_in_t:
        b_spec = pl.BlockSpec((pl.Element(tk), pl.Element(tn)),
                              lambda i, j, k: (_w_in_row_of(k, tk), pl.multiple_of(j * tn, LANE)))
    if out_is_w_in_t:
        out_spec, out_rows = pl.BlockSpec((pl.Element(tm), pl.Element(tn)),
                                          lambda i, j, k: (_w_in_row_of(i, tm), pl.multiple_of(j * tn, LANE))), D_IN
    outs = _hosted_call(
        body_single if nk == 1 else body_acc, rider, name=name, grid=(m // tm, n // tn, nk),
        in_specs=[a_spec, b_spec] + [mn_spec] * n_ex,
        out_specs=[out_spec] * len(out_dtypes),
        out_shape=[jax.ShapeDtypeStruct((out_rows, n), dt) for dt in out_dtypes],
        scratch_shapes=[] if nk == 1 else [pltpu.VMEM((tm, tn), F32)],
        args=(a, b, *extras))
    return outs[0] if len(outs) == 1 else outs


ROW_BLOCK = 256


def _row_spec(width=D_MODEL):
    return pl.BlockSpec((ROW_BLOCK, width), lambda i: (i, 0))


def _vec_spec(width=D_MODEL):
    return pl.BlockSpec((1, width), lambda i: (0, 0))


def _rms_fwd(x, g, name):
    def body(x_ref, g_ref, h_ref):
        xf = x_ref[...]
        h_ref[...] = (xf * lax.rsqrt(jnp.mean(xf * xf, axis=-1, keepdims=True) + EPS) * g_ref[...]).astype(BF16)

    t = x.shape[0]
    return pl.pallas_call(
        body, name=name, grid=(t // ROW_BLOCK,), in_specs=[_row_spec(), _vec_spec()], out_specs=_row_spec(),
        out_shape=jax.ShapeDtypeStruct((t, D_MODEL), BF16), compiler_params=_cparams(("parallel",)),
    )(x, g)


def _rms_bwd_math(xf, g, dh):
    rstd = lax.rsqrt(jnp.mean(xf * xf, axis=-1, keepdims=True) + EPS)
    xhat = xf * rstd
    dxhat = dh * g
    dx = rstd * (dxhat - xhat * jnp.mean(dxhat * xhat, axis=-1, keepdims=True))
    dg = jnp.sum(dh * xhat, axis=0, keepdims=True)
    return dx, dg


def _rms_bwd(x, g, dh, dres, name):
    def body(x_ref, g_ref, dh_ref, dres_ref, dx_ref, dg_ref):
        dx, dg = _rms_bwd_math(x_ref[...], g_ref[...], dh_ref[...].astype(F32))
        dx_ref[...] = dres_ref[...] + dx

        @pl.when(pl.program_id(0) == 0)
        def _():
            dg_ref[...] = jnp.zeros_like(dg_ref)

        dg_ref[...] += dg

    t = x.shape[0]
    return pl.pallas_call(
        body, name=name, grid=(t // ROW_BLOCK,),
        in_specs=[_row_spec(), _vec_spec(), _row_spec(), _row_spec()], out_specs=[_row_spec(), _vec_spec()],
        out_shape=[jax.ShapeDtypeStruct((t, D_MODEL), F32), jax.ShapeDtypeStruct((1, D_MODEL), F32)],
        compiler_params=_cparams(("arbitrary",)),
    )(x, g, dh, dres)


def _loss_head(x3, g, target):
    def body(x_ref, g_ref, t_ref, dx_ref, dg_ref, loss_ref):
        xf, gg = x_ref[...], g_ref[...]
        rstd = lax.rsqrt(jnp.mean(xf * xf, axis=-1, keepdims=True) + EPS)
        err = xf * rstd * gg - t_ref[...]
        dx, dg = _rms_bwd_math(xf, gg, err * (1.0 / D_MODEL))
        dx_ref[...] = dx

        @pl.when(pl.program_id(0) == 0)
        def _():
            dg_ref[...] = jnp.zeros_like(dg_ref)
            loss_ref[...] = jnp.zeros_like(loss_ref)

        dg_ref[...] += dg
        part = jnp.sum(jnp.sum(err * err, axis=-1, keepdims=True), axis=0, keepdims=True) * (0.5 / D_MODEL)
        loss_ref[...] += jnp.broadcast_to(part, loss_ref.shape)

    t = x3.shape[0]
    return pl.pallas_call(
        body, name="loss_head", grid=(t // ROW_BLOCK,),
        in_specs=[_row_spec(), _vec_spec(), _row_spec()], out_specs=[_row_spec(), _vec_spec(), _vec_spec(LANE)],
        out_shape=[jax.ShapeDtypeStruct((t, D_MODEL), F32), jax.ShapeDtypeStruct((1, D_MODEL), F32),
                   jax.ShapeDtypeStruct((1, LANE), F32)],
        compiler_params=_cparams(("arbitrary",)),
    )(x3, g, target)


def _ple_bwd(dx3, gpre, pp):
    def body(dx_ref, gp_ref, pp_ref, dgp_ref, dpp_ref):
        dx, sg = dx_ref[...], _sigmoid(gp_ref[...])
        dpp_ref[...] = (dx * sg).astype(BF16)
        dgp_ref[...] = (dx * pp_ref[...] * sg * (1.0 - sg)).astype(BF16)

    t = dx3.shape[0]
    return pl.pallas_call(
        body, name="ple_bwd", grid=(t // ROW_BLOCK,), in_specs=[_row_spec()] * 3, out_specs=[_row_spec()] * 2,
        out_shape=[jax.ShapeDtypeStruct((t, D_MODEL), BF16)] * 2, compiler_params=_cparams(("parallel",)),
    )(dx3, gpre, pp)


CONV_COLS = 256


def _shift_down(x, s):
    if s == 0:
        return x
    return jnp.where(_iota2(x.shape, 0) >= s, pltpu.roll(x, s, 0), 0.0)


def _shift_up(x, s):
    if s == 0:
        return x
    rows = x.shape[0]
    return jnp.where(_iota2(x.shape, 0) < rows - s, pltpu.roll(x, rows - s, 0), 0.0)


def _conv_taps(xf):
    return [_shift_down(xf, DN_CONV - 1 - j) for j in range(DN_CONV)]


def _conv_pre(taps, w):
    return sum(tap * w[j:j + 1, :] for j, tap in enumerate(taps))


def _conv_fwd(zb, conv_w, bsz, seq):
    def body(x_ref, w_ref, y_ref):
        y_ref[...] = _silu(_conv_pre(_conv_taps(x_ref[...]), w_ref[...]))

    nblk = DN_QKV // CONV_COLS
    return pl.pallas_call(
        body, name="conv_fwd", grid=(bsz, nblk),
        in_specs=[pl.BlockSpec((seq, CONV_COLS), lambda b, j: (b, ZB_DQKV // CONV_COLS + j)),
                  pl.BlockSpec((DN_CONV, CONV_COLS), lambda b, j: (0, j))],
        out_specs=pl.BlockSpec((seq, CONV_COLS), lambda b, j: (b, j)),
        out_shape=jax.ShapeDtypeStruct((bsz * seq, DN_QKV), F32),
        compiler_params=_cparams(("parallel", "parallel")),
    )(zb, conv_w)


def _conv_bwd(zb, conv_w, dact, bsz, seq):
    def body(x_ref, w_ref, dy_ref, dx_ref, dw_ref):
        taps, w = _conv_taps(x_ref[...]), w_ref[...]
        c = _conv_pre(taps, w)
        sg = _sigmoid(c)
        dc = dy_ref[...].astype(F32) * sg * (1.0 + c * (1.0 - sg))
        dx = sum(_shift_up(dc, DN_CONV - 1 - j) * w[j:j + 1, :] for j in range(DN_CONV))
        dx_ref[...] = dx.astype(BF16)
        dw = jnp.concatenate([jnp.sum(dc * tap, axis=0, keepdims=True) for tap in taps], axis=0)

        @pl.when(pl.program_id(1) == 0)
        def _():
            dw_ref[...] = jnp.zeros_like(dw_ref)

        dw_ref[...] += dw

    nblk = DN_QKV // CONV_COLS
    return pl.pallas_call(
        body, name="conv_bwd", grid=(nblk, bsz),
        in_specs=[pl.BlockSpec((seq, CONV_COLS), lambda j, b: (b, ZB_DQKV // CONV_COLS + j)),
                  pl.BlockSpec((DN_CONV, CONV_COLS), lambda j, b: (0, j)),
                  pl.BlockSpec((seq, CONV_COLS), lambda j, b: (b, j))],
        out_specs=[pl.BlockSpec((seq, CONV_COLS), lambda j, b: (b, j)),
                   pl.BlockSpec((DN_CONV, CONV_COLS), lambda j, b: (0, j))],
        out_shape=[jax.ShapeDtypeStruct((bsz * seq, DN_QKV), BF16), jax.ShapeDtypeStruct((DN_CONV, DN_QKV), F32)],
        compiler_params=_cparams(("parallel", "arbitrary")),
    )(zb, conv_w, dact)


MESH_IDS = pl.DeviceIdType.MESH
ANY_SPEC = pl.BlockSpec(memory_space=pl.ANY)


COMM_SCRATCH = (pltpu.SemaphoreType.DMA((7,)), pltpu.SemaphoreType.DMA((7,)), pltpu.SemaphoreType.DMA)


def _gather_phases(x_ref, out_ref, send_sems, recv_sems, local_sem):
    mx, my, mc = lax.axis_index("x"), lax.axis_index("y"), lax.axis_index("c")
    me, sibling = (mx, my, mc), (mx, my, 1 - mc)
    chips = [(1 - mx, my), (mx, 1 - my), (1 - mx, 1 - my)]

    def slot(px, py, pc):
        return out_ref.at[4 * px + 2 * py + pc]

    def copy(k, block, to, src=None):
        return pltpu.make_async_remote_copy(
            src_ref=slot(*block) if src is None else src, dst_ref=slot(*block),
            send_sem=send_sems.at[k], recv_sem=recv_sems.at[k], device_id=to, device_id_type=MESH_IDS)

    def mine():
        return pltpu.make_async_copy(x_ref, slot(*me), local_sem)

    def first():
        return [copy(0, me, sibling, src=x_ref)] + [copy(1 + j, me, (*chip, mc), src=x_ref)
                                                    for j, chip in enumerate(chips)]

    def passed():
        return [copy(4 + j, (*chip, mc), sibling) for j, chip in enumerate(chips)]

    def start():
        mine().start()
        for cp in first():
            cp.start()

    def forward():
        for j, (chip, cp) in enumerate(zip(chips, passed())):
            copy(1 + j, (*chip, mc), me).wait_recv()
            cp.start()

    def finish():
        copy(0, sibling, me).wait_recv()
        for j, chip in enumerate(chips):
            copy(4 + j, (*chip, 1 - mc), me).wait_recv()
        for cp in first() + passed():
            cp.wait_send()
        mine().wait()

    return start, forward, finish


def _scatter_phases(x_ref, out_ref, send_sems, recv_sems, local_sem, among_chips=False):
    mx, my, mc = lax.axis_index("x"), lax.axis_index("y"), lax.axis_index("c")
    n_peers = 4 if among_chips else N_DEV
    me = 2 * mx + my if among_chips else 4 * mx + 2 * my + mc

    def peer(k):
        if among_chips:
            return (mx ^ ((k >> 1) & 1), my ^ (k & 1), mc)
        return (mx ^ ((k >> 2) & 1), my ^ ((k >> 1) & 1), mc ^ (k & 1))

    def slot_of(k):
        px, py, pc = peer(k)
        return 2 * px + py if among_chips else 4 * px + 2 * py + pc

    def copy(k, src_slot, dst_slot):
        return pltpu.make_async_remote_copy(
            src_ref=x_ref.at[src_slot], dst_ref=out_ref.at[dst_slot],
            send_sem=send_sems.at[k - 1], recv_sem=recv_sems.at[k - 1],
            device_id=peer(k), device_id_type=MESH_IDS)

    def sends():
        return [copy(k, slot_of(k), me) for k in range(1, n_peers)]

    def mine():
        return pltpu.make_async_copy(x_ref.at[me], out_ref.at[me], local_sem)

    def start():
        mine().start()
        for cp in sends():
            cp.start()

    def forward():
        pass

    def finish():
        for k in range(1, n_peers):
            copy(k, me, slot_of(k)).wait_recv()
        for cp in sends():
            cp.wait_send()
        mine().wait()

    return start, forward, finish


def _pair_phases(x_ref, out_ref, send_sems, recv_sems, local_sem):
    mx, my, mc = lax.axis_index("x"), lax.axis_index("y"), lax.axis_index("c")

    def copy(side):
        return pltpu.make_async_remote_copy(
            src_ref=x_ref.at[:, side], dst_ref=out_ref, send_sem=send_sems.at[0], recv_sem=recv_sems.at[0],
            device_id=(mx, my, 1 - mc), device_id_type=MESH_IDS)

    def start():
        copy(1 - mc).start()

    def forward():
        pass

    def finish():
        copy(mc).wait_recv()
        copy(1 - mc).wait_send()

    return start, forward, finish


class _Rider:
    def __init__(self, phases, x, out_shape):
        self.phases, self.x, self.out_shape = phases, x, out_shape


def _gather_rider(x):
    return _Rider(_gather_phases, x, jax.ShapeDtypeStruct((N_DEV,) + x.shape, x.dtype))


def _scatter_rider(x):
    return _Rider(_scatter_phases, x, jax.ShapeDtypeStruct(x.shape, x.dtype))


def _chip_scatter_rider(x):
    return _Rider(functools.partial(_scatter_phases, among_chips=True), x, jax.ShapeDtypeStruct(x.shape, x.dtype))


def _pair_rider(x):
    return _Rider(_pair_phases, x, jax.ShapeDtypeStruct((x.shape[0],) + x.shape[2:], x.dtype))


def _exchange(rider, name):
    def body(x_ref, out_ref, send_sems, recv_sems, local_sem):
        for phase in rider.phases(x_ref, out_ref, send_sems, recv_sems, local_sem):
            phase()

    return pl.pallas_call(body, name=name, out_shape=rider.out_shape, in_specs=[ANY_SPEC], out_specs=ANY_SPEC,
                          scratch_shapes=list(COMM_SCRATCH))(rider.x)


def _all_gather(x, name):
    return _exchange(_gather_rider(x), name)


def _all_to_all(x, name):
    return _exchange(_scatter_rider(x), name)


def _hosted_call(body, rider, *, name, grid, in_specs, out_specs, out_shape, scratch_shapes, args):
    if rider is None:
        return pl.pallas_call(body, name=name, grid=grid, in_specs=in_specs, out_specs=out_specs, out_shape=out_shape,
                              scratch_shapes=scratch_shapes, compiler_params=_cparams(("arbitrary",) * len(grid)))(*args)
    n_in, n_out, n_scr = len(in_specs), len(out_specs), len(scratch_shapes)
    total = math.prod(grid)

    def riding(*refs):
        host_in, x_ref = refs[:n_in], refs[n_in]
        host_out, out_ref = refs[n_in + 1:n_in + 1 + n_out], refs[n_in + 1 + n_out]
        host_scr = refs[n_in + 2 + n_out:n_in + 2 + n_out + n_scr]
        start, forward, finish = rider.phases(x_ref, out_ref, *refs[n_in + 2 + n_out + n_scr:])
        step = 0
        for axis, size in enumerate(grid):
            step = step * size + pl.program_id(axis)
        pl.when(step == 0)(start)
        pl.when(step == (3 * total) // 4)(forward)
        body(*host_in, *host_out, *host_scr)
        pl.when(step == total - 1)(finish)

    return pl.pallas_call(
        riding, name=name, grid=grid, in_specs=list(in_specs) + [ANY_SPEC], out_specs=list(out_specs) + [ANY_SPEC],
        out_shape=list(out_shape) + [rider.out_shape], scratch_shapes=list(scratch_shapes) + list(COMM_SCRATCH),
        compiler_params=_cparams(("arbitrary",) * len(grid)))(*args, rider.x)


def _adamw_math(w, g, m, v):
    m = ADAM_B1 * m + (1.0 - ADAM_B1) * g
    v = ADAM_B2 * v + (1.0 - ADAM_B2) * jnp.square(g)
    m_hat = m / (1.0 - ADAM_B1 ** ADAM_STEP)
    v_hat = v / (1.0 - ADAM_B2 ** ADAM_STEP)
    delta = -ADAM_LR * (m_hat / (jnp.sqrt(v_hat) + ADAM_EPS) + ADAM_WD * w)
    return delta, m, v


ADAM_ROWS = 128


def _elementwise_tile(rows, cols):
    if rows % ADAM_ROWS == 0:
        return ADAM_ROWS, cols
    return rows, (2 * LANE if cols % (2 * LANE) == 0 else cols)


def _add_blocks(a, b):
    g, rows, cols = a.shape
    tr, tc = _elementwise_tile(rows, cols)

    def body(a_ref, b_ref, o_ref):
        o_ref[...] = (a_ref[...].astype(F32) + b_ref[...].astype(F32)).astype(o_ref.dtype)

    blk = pl.BlockSpec((None, tr, tc), lambda k, i, j: (k, i, j))
    return pl.pallas_call(body, name="add_pair_blocks", grid=(g, rows // tr, cols // tc), in_specs=[blk, blk],
                          out_specs=blk, out_shape=jax.ShapeDtypeStruct(a.shape, a.dtype),
                          compiler_params=_cparams(("parallel", "parallel", "parallel")))(a, b)


def _adamw_reduce(w, m, v, parts, row0, name):
    rows, cols = w.shape
    n_parts = parts.shape[0]
    tr, tc = _elementwise_tile(rows, cols)
    r0 = row0 // tr

    def body(w_ref, m_ref, v_ref, *refs):
        part_refs, (g_ref, d_ref, nm_ref, nv_ref) = refs[:n_parts], refs[n_parts:]
        g = part_refs[0][...].astype(F32)
        for r in part_refs[1:]:
            g = g + r[...].astype(F32)
        delta, nm, nv = _adamw_math(w_ref[...], g, m_ref[...], v_ref[...])
        g_ref[...] = g
        d_ref[...] = delta
        nm_ref[...] = nm
        nv_ref[...] = nv

    blk = pl.BlockSpec((tr, tc), lambda i, j: (i, j))
    part_specs = [pl.BlockSpec((None, tr, tc), functools.partial(lambda i, j, k: (k, r0 + i, j), k=k))
                  for k in range(n_parts)]
    return pl.pallas_call(
        body, name=name, grid=(rows // tr, cols // tc), in_specs=[blk] * 3 + part_specs, out_specs=[blk] * 4,
        out_shape=[jax.ShapeDtypeStruct(w.shape, F32)] * 4, compiler_params=_cparams(("parallel", "parallel")),
    )(w, m, v, *([parts] * n_parts))


def _adamw_reduce_linear_rows(w, m, v, parts, name):
    rows = w.shape[0]
    n_parts = parts.shape[0]
    tr = max(d for d in range(1, 257) if rows % d == 0)

    def body(w_ref, m_ref, v_ref, *refs):
        part_refs, (g_ref, d_ref, nm_ref, nv_ref) = refs[:n_parts], refs[n_parts:]
        g = part_refs[0][...].astype(F32)
        for r in part_refs[1:]:
            g = g + r[...].astype(F32)
        delta, nm, nv = _adamw_math(w_ref[...], g, m_ref[...], v_ref[...])
        g_ref[...] = g
        d_ref[...] = delta
        nm_ref[...] = nm
        nv_ref[...] = nv

    blk = pl.BlockSpec((tr,) + w.shape[1:], lambda i: (i, 0, 0))
    part_specs = [pl.BlockSpec((None, tr) + w.shape[1:], functools.partial(lambda i, k: (k, i, 0, 0), k=k))
                  for k in range(n_parts)]
    return pl.pallas_call(
        body, name=name, grid=(rows // tr,), in_specs=[blk] * 3 + part_specs, out_specs=[blk] * 4,
        out_shape=[jax.ShapeDtypeStruct(w.shape, F32)] * 4, compiler_params=_cparams(("parallel",)),
    )(w, m, v, *([parts] * n_parts))


def _small_reduce(gathered):
    r = gathered.shape[1]

    def body(g_ref, o_ref):
        g = g_ref[0]
        for k in range(1, N_DEV):
            g = g + g_ref[k]
        o_ref[...] = g

    return pl.pallas_call(body, name="small_grad_reduce", out_shape=jax.ShapeDtypeStruct((r, LANE), F32))(gathered)


def _adamw_small(w, m, v, g):
    def body(w_ref, m_ref, v_ref, g_ref, d_ref, nm_ref, nv_ref):
        d_ref[...], nm_ref[...], nv_ref[...] = _adamw_math(w_ref[...], g_ref[...], m_ref[...], v_ref[...])

    return pl.pallas_call(body, name="adamw_small", out_shape=[jax.ShapeDtypeStruct(w.shape, F32)] * 3)(w, m, v, g)


def _pack_rows(arrays):
    rows = [jnp.pad(a.reshape(-1), (0, -a.size % LANE)).reshape(-1, LANE) for a in arrays]
    out = jnp.concatenate(rows, axis=0)
    return jnp.pad(out, ((0, -out.shape[0] % 8), (0, 0)))


def _unpack_rows(packed, shapes):
    out, r = [], 0
    for shp in shapes:
        size = math.prod(shp)
        nrows = -(-size // LANE)
        out.append(packed[r:r + nrows].reshape(-1)[:size].reshape(shp))
        r += nrows
    return out


def _add_residual(acc, res):
    return (res + acc,)


def _by_cols(g):
    return g.reshape(g.shape[0], N_DEV, -1).transpose(1, 0, 2)


def _from_cols(blocks):
    return blocks.transpose(1, 0, 2).reshape(blocks.shape[1], -1)


ROWS_OUT = D_MODEL // N_DEV

W_IN_SEGMENTS = ((0, WI_LR, "big", 0), (WI_LR, WI_DQKV, "gates", ZS_LR), (WI_DQKV, WI_DA, "big", ZB_DQKV),
                 (WI_DA, WI_DB, "gates", ZS_DA), (WI_DB, WI_GA, "gates", ZS_DB), (WI_GA, D_IN, "big", ZB_GA))


def _w_in_gate_rows(wt):
    parts = []
    for lo, hi, which, _ in W_IN_SEGMENTS:
        if which == "gates":
            parts += [wt[lo:hi], jnp.zeros((LANE - (hi - lo), wt.shape[1]), wt.dtype)]
    return jnp.concatenate(parts, axis=0)


def _fill_gate_rows(dw_in_t, dws_t):
    for lo, hi, which, first in W_IN_SEGMENTS:
        if which == "gates":
            dw_in_t = lax.dynamic_update_slice(dw_in_t, dws_t[first:first + hi - lo], (lo, 0))
    return dw_in_t


def _local_step(x, p, target, w, up_shard, rows_shard):
    bsz, seq, _ = x.shape
    t, nc = bsz * seq, seq // CHUNK
    x0, p2, tgt = x.reshape(t, D_MODEL), p.reshape(t, PLE_DIM), target.reshape(t, D_MODEL)

    h = _rms_fwd(x0, w["g_mix"], "rms_mix")
    zb, up_blocks = _mm(h, w["w_in_t"], tb=True, b_is_w_in_t=True, name="in_proj", rider=_gather_rider(up_shard))
    w_up = _from_cols(up_blocks)
    zs = _mm(h, w["ws_t"], tb=True, name="in_proj_gates")
    act = _conv_fwd(zb, w["conv"], bsz, seq)
    mix_gla, gla_states = _gla_fwd(zb, zs, w["w2p"], w["gla_b"], w["gla_norm"], bsz, nc)
    mixed, dn_states, dn_t_invs, row_blocks = _dn_fwd(act, zb, zs, w["alog_b"], w["dtb_b"], w["dn_norm"], mix_gla,
                                                      bsz, nc, rider=_gather_rider(rows_shard))
    w_out = row_blocks[:, :ROWS_OUT].reshape(D_MODEL, D_MODEL)
    w_pg = row_blocks[:, ROWS_OUT:2 * ROWS_OUT].reshape(D_MODEL, D_MODEL)
    w_down = row_blocks[:, 2 * ROWS_OUT:].reshape(D_FF, D_MODEL)
    x1 = _mm(mixed, w_out, epilogue=_add_residual, extras=(x0,), name="out_proj")
    h2 = _rms_fwd(x1, w["g_mlp"], "rms_mlp")
    u, a = _mm(h2, w_up, out_dtypes=(BF16, BF16), name="mlp_up",
               epilogue=lambda acc: (acc, jnp.square(jnp.maximum(acc, 0.0))))
    x2 = _mm(a, w_down, epilogue=_add_residual, extras=(x1,), name="mlp_down")
    h3 = _rms_fwd(x2, w["g_ple"], "rms_ple")
    pp = _mm(p2, w["w_pp"], name="ple_proj")
    gpre, x3 = _mm(h3, w_pg, out_dtypes=(F32, F32), extras=(x2, pp), name="ple_gate",
                   epilogue=lambda acc, res, proj: (acc, res + _sigmoid(acc) * proj))
    dx3, dg_final, loss = _loss_head(x3, w["g_final"], tgt)

    dgpre, dpp = _ple_bwd(dx3, gpre, pp)
    dw_pp = _mm(p2, dpp, ta=True, out_dtypes=(BF16,), name="d_w_ple_proj")
    dw_pg = _mm(h3, dgpre, ta=True, out_dtypes=(BF16,), name="d_w_ple_gate")
    dh3 = _mm(dgpre, w_pg, tb=True, name="d_h_ple")
    dx2, dg_ple = _rms_bwd(x2, w["g_ple"], dh3, dx3, "rms_ple_bwd")
    du = _mm(dx2, w_down, tb=True, out_dtypes=(BF16,), extras=(u,), name="d_mlp_hidden",
             epilogue=lambda acc, uu: (acc * (2.0 * jnp.maximum(uu.astype(F32), 0.0)),))
    dw_down = _mm(a, dx2, ta=True, out_dtypes=(BF16,), name="d_w_down")
    dw_up = _mm(h2, du, ta=True, out_dtypes=(BF16,), name="d_w_up")
    dh2 = _mm(du, w_up, tb=True, name="d_h_mlp")
    dx1, dg_mlp = _rms_bwd(x1, w["g_mlp"], dh2, dx2, "rms_mlp_bwd")
    dmixed = _mm(dx1, w_out, tb=True, out_dtypes=(BF16,), name="d_mixed")
    dw_out = _mm(mixed, dx1, ta=True, out_dtypes=(BF16,), name="d_w_out")

    d_rows = jnp.concatenate([dw_out.reshape(N_DEV, ROWS_OUT, D_MODEL), dw_pg.reshape(N_DEV, ROWS_OUT, D_MODEL),
                              dw_down.reshape(N_DEV, D_FF // N_DEV, D_MODEL)], axis=1)
    (dact, dza, dzb_, dgz, dgb, dal, ddt, ddn, recv_rows) = _dn_bwd(
        act, zb, zs, w["alog_b"], w["dtb_b"], w["dn_norm"], dn_states, dn_t_invs, dmixed, bsz, nc,
        rider=_scatter_rider(d_rows))
    (gdq, gdk, gdv, dgg, dga, dlr, dw2, dgla_b, dgla_norm, recv_up) = _gla_bwd(
        zb, zs, w["w2p"], w["gla_b"], w["gla_norm"], gla_states, dmixed, bsz, nc,
        rider=_scatter_rider(_by_cols(dw_up)))
    dqkv, dconv = _conv_bwd(zb, w["conv"], dact, bsz, seq)
    dzb = jnp.concatenate([gdq, gdk, gdv, dgg, dqkv, dgz, dga, dgb], axis=1)
    dzs = jnp.concatenate([dlr, dza, dzb_], axis=1)
    dw_in_t = _mm(dzb, h, ta=True, out_dtypes=(BF16,), out_is_w_in_t=True, name="d_w_in")
    dws_t = _mm(dzs, h, ta=True, out_dtypes=(BF16,), name="d_w_in_gates")
    by_chip = _fill_gate_rows(dw_in_t, dws_t).reshape(N_DEV // 2, 2, D_IN_SHARD, D_MODEL)
    from_sibling = _exchange(_pair_rider(by_chip), "pair_d_w_in")
    mine = lax.dynamic_index_in_dim(by_chip, lax.axis_index("c"), axis=1, keepdims=False)
    chip_sums = _add_blocks(mine, from_sibling)
    dh_gates = _mm(dzs, w["ws_t"], name="d_h_mix_gates")
    dh, recv_in = _mm(dzb, w["w_in_t"], b_is_w_in_t=True, epilogue=_add_residual, extras=(dh_gates,), name="d_h_mix",
                      rider=_chip_scatter_rider(chip_sums))
    gx, dg_mix = _rms_bwd(x0, w["g_mix"], dh, dx1, "rms_mix_bwd")

    dgla_w2 = dw2[:, :GLA_LOWRANK, :].transpose(1, 0, 2).reshape(GLA_LOWRANK, GLA_QK)
    return dict(
        loss=loss[0, 0], grad_x=gx.reshape(x.shape), recv_in=recv_in, recv_up=recv_up, recv_rows=recv_rows,
        w_ple_proj=dw_pp,
        g_mix=dg_mix, gla_b=dgla_b.reshape(1, GLA_QK), gla_norm=dgla_norm, dn_norm=ddn, g_mlp=dg_mlp, g_ple=dg_ple,
        g_final=dg_final, gla_w2=dgla_w2, dn_conv=dconv,
        dn_a_log=dal[:, :DN_HEADS], dn_dt_bias=ddt[:, :DN_HEADS])


def _first_weights(g_mix, w_in, gla_w2, gla_b, gla_norm, dn_conv, dn_a_log, dn_dt_bias, dn_norm, g_mlp, g_ple,
                   w_ple_proj, g_final):
    w_in_t = _all_gather(jnp.swapaxes(w_in[0], 0, 1).astype(BF16), "gather_w_in").reshape(D_IN, D_MODEL)
    w_pp = _all_gather(w_ple_proj[0].astype(BF16), "gather_w_ple_proj")
    small = _all_gather(_pack_rows([gla_w2[0], dn_conv[0]]), "gather_w_small")
    n_w2 = GLA_LOWRANK * GLA_QK // N_DEV // LANE
    n_cv = DN_CONV * DN_QKV // N_DEV // LANE
    w2 = small[:, :n_w2].reshape(N_DEV, GLA_LOWRANK, GLA_QK // N_DEV).transpose(1, 0, 2).reshape(GLA_LOWRANK, GLA_QK)
    conv = small[:, n_w2:n_w2 + n_cv].reshape(N_DEV, DN_CONV, DN_QKV // N_DEV).transpose(1, 0, 2).reshape(DN_CONV, DN_QKV)

    return dict(
        w_in_t=w_in_t, ws_t=_w_in_gate_rows(w_in_t), w_pp=_from_cols(w_pp),
        w2p=jnp.pad(w2, ((0, LANE - GLA_LOWRANK), (0, 0))), conv=conv,
        alog_b=jnp.pad(dn_a_log, ((0, 0), (0, LANE - DN_HEADS))),
        dtb_b=jnp.pad(dn_dt_bias, ((0, 0), (0, LANE - DN_HEADS))),
        g_mix=g_mix, gla_b=gla_b, gla_norm=gla_norm, dn_norm=dn_norm, g_mlp=g_mlp, g_ple=g_ple,
        g_final=g_final.reshape(1, D_MODEL))


def kernel(x, p, g_mix, w_in, gla_w2, gla_b, gla_norm, dn_conv, dn_a_log, dn_dt_bias, dn_norm, w_out, g_mlp, w_up, w_down, g_ple, w_ple_gate, w_ple_proj, g_final, loss_target, m_g_mix, m_w_in, m_gla_w2, m_gla_b, m_gla_norm, m_dn_conv, m_dn_a_log, m_dn_dt_bias, m_dn_norm, m_w_out, m_g_mlp, m_w_up, m_w_down, m_g_ple, m_w_ple_gate, m_w_ple_proj, m_g_final, v_g_mix, v_w_in, v_gla_w2, v_gla_b, v_gla_norm, v_dn_conv, v_dn_a_log, v_dn_dt_bias, v_dn_norm, v_w_out, v_g_mlp, v_w_up, v_w_down, v_g_ple, v_w_ple_gate, v_w_ple_proj, v_g_final):
    names = ["g_mix", "w_in", "gla_w2", "gla_b", "gla_norm", "dn_conv", "dn_a_log", "dn_dt_bias", "dn_norm", "w_out",
             "g_mlp", "w_up", "w_down", "g_ple", "w_ple_gate", "w_ple_proj", "g_final"]
    ws = dict(zip(names, (g_mix, w_in, gla_w2, gla_b, gla_norm, dn_conv, dn_a_log, dn_dt_bias, dn_norm, w_out, g_mlp,
                          w_up, w_down, g_ple, w_ple_gate, w_ple_proj, g_final)))
    ms = dict(zip(names, (m_g_mix, m_w_in, m_gla_w2, m_gla_b, m_gla_norm, m_dn_conv, m_dn_a_log, m_dn_dt_bias,
                          m_dn_norm, m_w_out, m_g_mlp, m_w_up, m_w_down, m_g_ple, m_w_ple_gate, m_w_ple_proj,
                          m_g_final)))
    vs = dict(zip(names, (v_g_mix, v_w_in, v_gla_w2, v_gla_b, v_gla_norm, v_dn_conv, v_dn_a_log, v_dn_dt_bias,
                          v_dn_norm, v_w_out, v_g_mlp, v_w_up, v_w_down, v_g_ple, v_w_ple_gate, v_w_ple_proj,
                          v_g_final)))
    me = 4 * lax.axis_index("x") + 2 * lax.axis_index("y") + lax.axis_index("c")

    first = _first_weights(g_mix, w_in, gla_w2, gla_b, gla_norm, dn_conv, dn_a_log, dn_dt_bias, dn_norm, g_mlp,
                           g_ple, w_ple_proj, g_final)
    rows_shard = jnp.concatenate([w_out[0], w_ple_gate[0], w_down[0]], axis=0).astype(BF16)
    r = _local_step(x, p[0], loss_target, first, w_up[0].astype(BF16), rows_shard)
    loss = lax.psum(r["loss"], ("x", "y", "c"))

    grads, deltas, new_m, new_v = {}, {}, {}, {}

    def big(name, parts, row0=0):
        g, d, nm, nv = _adamw_reduce(ws[name][0], ms[name][0], vs[name][0], parts, row0, "adamw_" + name)
        grads[name], deltas[name], new_m[name], new_v[name] = g[None], d[None], nm[None], nv[None]

    def linear_rows(a):
        return jnp.swapaxes(a, -1, -2).reshape(a.shape[:-2] + (D_IN_SHARD, D_MODEL // LANE, LANE))

    t_outs = _adamw_reduce_linear_rows(*[linear_rows(d["w_in"][0]) for d in (ws, ms, vs)],
                                       r["recv_in"].reshape(r["recv_in"].shape[:2] + (D_MODEL // LANE, LANE)),
                                       "adamw_w_in")
    grads["w_in"], deltas["w_in"], new_m["w_in"], new_v["w_in"] = [
        jnp.swapaxes(o.reshape(D_IN_SHARD, D_MODEL), 0, 1)[None] for o in t_outs]
    big("w_up", r["recv_up"])
    big("w_ple_proj", _all_to_all(_by_cols(r["w_ple_proj"]), "scatter_d_w_ple_proj"))
    big("w_out", r["recv_rows"], 0)
    big("w_ple_gate", r["recv_rows"], ROWS_OUT)
    big("w_down", r["recv_rows"], 2 * ROWS_OUT)

    vec_names = ["g_mix", "gla_b", "gla_norm", "dn_norm", "g_mlp", "g_ple", "g_final", "dn_a_log", "dn_dt_bias"]
    gathered_names = vec_names + ["gla_w2", "dn_conv"]
    total = _small_reduce(_all_gather(_pack_rows([r[n] for n in gathered_names]), "gather_small_grads"))
    sg = dict(zip(gathered_names, _unpack_rows(total, [r[n].shape for n in gathered_names])))
    sg["g_final"] = sg["g_final"].reshape(D_MODEL)
    sg["gla_w2"] = lax.dynamic_slice_in_dim(sg["gla_w2"], me * (GLA_QK // N_DEV), GLA_QK // N_DEV, axis=1)
    sg["dn_conv"] = lax.dynamic_slice_in_dim(sg["dn_conv"], me * (DN_QKV // N_DEV), DN_QKV // N_DEV, axis=1)
    small_names = gathered_names
    shapes = [ws[n].shape for n in small_names]
    d_s, m_s, v_s = _adamw_small(_pack_rows([ws[n] for n in small_names]), _pack_rows([ms[n] for n in small_names]),
                                 _pack_rows([vs[n] for n in small_names]), _pack_rows([sg[n] for n in small_names]))
    for n, d, nm, nv in zip(small_names, _unpack_rows(d_s, shapes), _unpack_rows(m_s, shapes), _unpack_rows(v_s, shapes)):
        grads[n], deltas[n], new_m[n], new_v[n] = sg[n].reshape(ws[n].shape), d, nm, nv

    return (loss, r["grad_x"], *[grads[n] for n in names], *[deltas[n] for n in names],
            *[new_m[n] for n in names], *[new_v[n] for n in names])
```

```python
import functools
import math

import jax
import jax.numpy as jnp
from jax import lax
from jax.experimental import pallas as pl
from jax.experimental.pallas import tpu as pltpu

F32 = jnp.float32
BF16 = jnp.bfloat16

N_DEV = 8
D_MODEL = 2048
CHUNK = 64
PLE_DIM = 256
EPS = 1e-6
GLA_HEADS = 4
GLA_DK = 256
GLA_DV = 512
GLA_LOWRANK = 16
GLA_TAU = 16.0
DN_HEADS = 16
DN_D = 128
DN_CONV = 4
D_FF = 4 * D_MODEL
GLA_QK = GLA_HEADS * GLA_DK
GLA_V = GLA_HEADS * GLA_DV
DN_QKV = 3 * DN_HEADS * DN_D
D_IN = 2 * GLA_QK + 2 * GLA_V + GLA_LOWRANK + DN_QKV + D_MODEL + 2 * DN_HEADS + 2 * D_MODEL
D_IN_SHARD = D_IN // N_DEV

ADAM_LR = 0.001
ADAM_B1 = 0.9
ADAM_B2 = 0.999
ADAM_EPS = 1e-08
ADAM_WD = 0.01
ADAM_STEP = 10

LANE = 128
ZB_GQ, ZB_GK, ZB_GV, ZB_GG = 0, 1024, 2048, 4096
ZB_DQKV, ZB_DZ, ZB_GA, ZB_GB = 6144, 12288, 14336, 16384
ZB_W = 18432
ZS_LR, ZS_DA, ZS_DB = 0, 128, 256
ZS_W = 384
WI_LR = 2 * GLA_QK + 2 * GLA_V
WI_DQKV = WI_LR + GLA_LOWRANK
WI_DA = WI_DQKV + DN_QKV + D_MODEL
WI_DB = WI_DA + DN_HEADS
WI_GA = WI_DB + DN_HEADS

VMEM_LIMIT = 56 * 1024 * 1024

def _bdot(a, b, dims):
    return lax.dot_general(a.astype(BF16), b.astype(BF16), dims, preferred_element_type=F32)


def _split3(x):
    hi = x.astype(BF16)
    rest = x - hi.astype(F32)
    mid = rest.astype(BF16)
    return hi, mid, (rest - mid.astype(F32)).astype(BF16)


def _dot01(m, x, dims):
    m = m.astype(BF16)
    out = None
    for piece in _split3(x):
        d = lax.dot_general(m, piece, dims, preferred_element_type=F32)
        out = d if out is None else out + d
    return out


@functools.partial(jax.custom_vjp, nondiff_argnums=(2, 3))
def _left_dot(m, x, dims, dims_t):
    return _dot01(m, x, dims)


def _left_dot_fwd(m, x, dims, dims_t):
    return _dot01(m, x, dims), m


def _left_dot_bwd(dims, dims_t, m, ct):
    return jnp.zeros_like(m), _dot01(m, ct, dims_t)


_left_dot.defvjp(_left_dot_fwd, _left_dot_bwd)


def _dot3(a, b, dims):
    ah, bh = a.astype(BF16), b.astype(BF16)
    al, bl = (a - ah.astype(F32)).astype(BF16), (b - bh.astype(F32)).astype(BF16)
    dot = functools.partial(lax.dot_general, dimension_numbers=dims, preferred_element_type=F32)
    return dot(ah, bh) + (dot(ah, bl) + dot(al, bh))


def _sigmoid(x):
    return 1.0 / (1.0 + jnp.exp(-x))


def _silu(x):
    return x * _sigmoid(x)


def _softplus(x):
    return jnp.maximum(x, 0.0) + jnp.log(1.0 + jnp.exp(-jnp.abs(x)))


def _iota2(shape, dim):
    return lax.broadcasted_iota(jnp.int32, shape, dim)


def _cparams(sem=None):
    return pltpu.CompilerParams(dimension_semantics=sem, vmem_limit_bytes=VMEM_LIMIT)


BNN = (((2,), (1,)), ((0,), (0,)))
BNT = (((2,), (2,)), ((0,), (0,)))
BTN = (((1,), (1,)), ((0,), (0,)))


def _gla_chunk(st, q, k, v, lr, w2, b, gg, ga, gn):
    hb, c, _ = q.shape
    incl = (_iota2((c, c), 0) >= _iota2((c, c), 1))[None]
    tri = jnp.broadcast_to(incl.astype(F32), (hb, c, c))
    q = q.astype(F32) * (GLA_DK ** -0.5)
    k = k.astype(F32)
    v = v.astype(F32)
    lr_b = jnp.broadcast_to(lr[None], (hb,) + lr.shape)
    lf = -_softplus(-(_bdot(lr_b, w2, BNN) + b)) / GLA_TAU
    bcum = _left_dot(tri, lf, BNN, BTN)
    blast = jnp.sum(lf, axis=1, keepdims=True)
    q_in = q * jnp.exp(bcum)
    k_in = k * jnp.exp(-bcum)
    a = jnp.where(incl, _bdot(q_in, k_in, BNT), 0.0)
    o = _bdot(a, v, BNN) + _bdot(q_in, st, BNT)
    k_dec = k * jnp.exp(blast - bcum)
    st_new = st * jnp.exp(blast) + _bdot(v, k_dec, BTN)
    on = o * lax.rsqrt(jnp.mean(o * o, axis=-1, keepdims=True) + EPS) * gn
    res = _sigmoid(ga.astype(F32)) * on * _silu(gg.astype(F32))
    return res, st_new


def _tri_inv_raw(a):
    _, c, _ = a.shape
    eye = (_iota2((c, c), 0) == _iota2((c, c), 1)).astype(F32)[None]
    x = a
    p = eye - a
    for _ in range(5):
        x = _dot3(x, x, BNN)
        p = _dot3(p, eye + x, BNN)
    return p


def _tri_inv_bwd(t, dt):
    return (-_dot3(_dot3(t, dt, BTN), t, BNT),)


@jax.custom_vjp
def _tri_inv_given(a, t):
    return t


def _tri_inv_given_fwd(a, t):
    return t, t


def _tri_inv_given_bwd(t, dt):
    return _tri_inv_bwd(t, dt) + (jnp.zeros_like(t),)


_tri_inv_given.defvjp(_tri_inv_given_fwd, _tri_inv_given_bwd)


@functools.partial(jax.custom_vjp, nondiff_argnums=(1, 2))
def _column_on_lanes(z, j, width):
    picked = jnp.where(_iota2(z.shape, 1) == j, z, 0.0)
    return jnp.broadcast_to(jnp.sum(picked, axis=1, keepdims=True), (z.shape[0], width))


def _column_on_lanes_fwd(z, j, width):
    return _column_on_lanes(z, j, width), None


def _column_on_lanes_bwd(j, width, _, ct):
    shape = (ct.shape[0], LANE)
    total = jnp.broadcast_to(jnp.sum(ct, axis=1, keepdims=True), shape)
    return (jnp.where(_iota2(shape, 1) == j, total, 0.0),)


_column_on_lanes.defvjp(_column_on_lanes_fwd, _column_on_lanes_bwd)


def _dn_chunk(s, qr, kr, vr, za, zb, alog, dtb, gz, gb, dn, t_saved=None):
    hb, c, _ = qr.shape
    row, col = _iota2((c, c), 0), _iota2((c, c), 1)
    incl = (row >= col)[None]
    strict = (row > col)[None]
    tri = jnp.broadcast_to(incl.astype(F32), (hb, c, c))

    def l2n(t):
        return t * lax.rsqrt(jnp.sum(t * t, axis=-1, keepdims=True) + EPS)

    q = l2n(qr.astype(F32)) * (DN_D ** -0.5)
    k = l2n(kr.astype(F32))
    v = vr.astype(F32)
    g_heads = -jnp.exp(alog) * _softplus(za + dtb)
    beta_heads = _sigmoid(zb)
    def per_head(z, width):
        return jnp.concatenate([_column_on_lanes(z, j, width)[None] for j in range(hb)], axis=0)

    g = per_head(g_heads, LANE)
    beta = per_head(beta_heads, LANE)
    gcum = _left_dot(tri, g, BNN, BTN)
    glast = jnp.sum(g, axis=1, keepdims=True)
    diff = _left_dot(tri, per_head(g_heads, c) * strict.astype(F32), BNN, BTN)
    dec = jnp.exp(jnp.where(incl, diff, -1e30))
    kb = k * beta
    a = jnp.where(strict, _bdot(kb, k, BNT) * dec, 0.0)
    t = _tri_inv_raw(a) if t_saved is None else _tri_inv_given(a, t_saved)
    egc = jnp.exp(gcum)
    u = _bdot(t, v * beta, BNN)
    w = _bdot(t, kb * egc, BNN)
    attn = jnp.where(incl, _bdot(q, k, BNT) * dec, 0.0)
    q_dec = q * egc
    k_dec = k * jnp.exp(glast - gcum)
    v_new = u - _bdot(w, s, BNN)
    o = _bdot(q_dec, s, BNN) + _bdot(attn, v_new, BNN)
    s_new = s * jnp.exp(glast) + _bdot(k_dec, v_new, BTN)
    on = o * lax.rsqrt(jnp.mean(o * o, axis=-1, keepdims=True) + EPS) * dn
    res = _sigmoid(gb.astype(F32)) * on * _silu(gz.astype(F32))
    return (res, s_new, t) if t_saved is None else (res, s_new)


def _heads(ref, n_heads, width):
    return jnp.stack([ref[:, j * width:(j + 1) * width] for j in range(n_heads)], axis=0)


def _gla_specs(nc, reverse):
    def rows(b, n):
        return b * nc + ((nc - 1 - n) if reverse else n)

    qk = lambda base: pl.BlockSpec((CHUNK, GLA_QK), lambda b, n: (rows(b, n), base // GLA_QK))
    vv = lambda base: pl.BlockSpec((CHUNK, GLA_V), lambda b, n: (rows(b, n), base // GLA_V))
    lr = lambda c: pl.BlockSpec((CHUNK, LANE), lambda b, n: (rows(b, n), c))
    full = lambda shape: pl.BlockSpec(shape, lambda b, n: (0,) * len(shape))
    return rows, qk, vv, lr, full


def _gla_inputs(q_ref, k_ref, v_ref, gg_ref, ga_ref, lr_ref, w2_ref, b_ref, gn_ref):
    return (_heads(q_ref, GLA_HEADS, GLA_DK), _heads(k_ref, GLA_HEADS, GLA_DK), _heads(v_ref, GLA_HEADS, GLA_DV),
            lr_ref[...], _heads(w2_ref, GLA_HEADS, GLA_DK), _heads(b_ref, GLA_HEADS, GLA_DK),
            _heads(gg_ref, GLA_HEADS, GLA_DV), _heads(ga_ref, GLA_HEADS, GLA_DV), gn_ref[...])


def _gla_fwd(zb, zs, w2p, gla_b, gla_norm, bsz, nc):
    t = zb.shape[0]
    rows, qk, vv, lr, full = _gla_specs(nc, False)

    def body(q_ref, k_ref, v_ref, gg_ref, ga_ref, lr_ref, w2_ref, b_ref, gn_ref, o_ref, st_ref, state):
        st = jnp.where(pl.program_id(1) > 0, state[...], 0.0)
        st_ref[0, 0] = st
        res, st_new = _gla_chunk(st, *_gla_inputs(q_ref, k_ref, v_ref, gg_ref, ga_ref, lr_ref, w2_ref, b_ref, gn_ref))
        for j in range(GLA_HEADS):
            o_ref[:, j * GLA_DV:(j + 1) * GLA_DV] = res[j]
        state[...] = st_new

    return pl.pallas_call(
        body, name="gla_fwd", grid=(bsz, nc),
        in_specs=[qk(ZB_GQ), qk(ZB_GK), vv(ZB_GV), vv(ZB_GG), vv(ZB_GA), lr(ZS_LR // LANE),
                  full((LANE, GLA_QK)), full((1, GLA_QK)), full((1, GLA_DV))],
        out_specs=[vv(0), pl.BlockSpec((1, 1, GLA_HEADS, GLA_DV, GLA_DK), lambda b, n: (b, n, 0, 0, 0))],
        out_shape=[jax.ShapeDtypeStruct((t, GLA_V), F32),
                   jax.ShapeDtypeStruct((bsz, nc, GLA_HEADS, GLA_DV, GLA_DK), F32)],
        scratch_shapes=[pltpu.VMEM((GLA_HEADS, GLA_DV, GLA_DK), F32)],
        compiler_params=_cparams(("arbitrary", "arbitrary")),
    )(zb, zb, zb, zb, zb, zs, w2p, gla_b, gla_norm)


def _gla_bwd(zb, zs, w2p, gla_b, gla_norm, states, dmixed, bsz, nc, rider=None):
    t = zb.shape[0]
    rows, qk, vv, lr, full = _gla_specs(nc, True)

    def body(q_ref, k_ref, v_ref, gg_ref, ga_ref, lr_ref, w2_ref, b_ref, gn_ref, st_ref, dm_ref,
             dq_ref, dk_ref, dv_ref, dgg_ref, dga_ref, dlr_ref, dw2_ref, db_ref, dgn_ref, dstate):
        b, n = pl.program_id(0), pl.program_id(1)

        @pl.when((b == 0) & (n == 0))
        def _():
            dw2_ref[...] = jnp.zeros_like(dw2_ref)
            db_ref[...] = jnp.zeros_like(db_ref)
            dgn_ref[...] = jnp.zeros_like(dgn_ref)

        _, vjp = jax.vjp(_gla_chunk, st_ref[0, 0],
                         *_gla_inputs(q_ref, k_ref, v_ref, gg_ref, ga_ref, lr_ref, w2_ref, b_ref, gn_ref))
        dst_in = jnp.where(n > 0, dstate[...], 0.0)
        dst, dq, dk, dv, dlr, dw2, db, dgg, dga, dgn = vjp((_heads(dm_ref, GLA_HEADS, GLA_DV).astype(F32), dst_in))
        dstate[...] = dst
        for j in range(GLA_HEADS):
            dq_ref[:, j * GLA_DK:(j + 1) * GLA_DK] = dq[j].astype(dq_ref.dtype)
            dk_ref[:, j * GLA_DK:(j + 1) * GLA_DK] = dk[j].astype(dk_ref.dtype)
            dv_ref[:, j * GLA_DV:(j + 1) * GLA_DV] = dv[j].astype(dv_ref.dtype)
            dgg_ref[:, j * GLA_DV:(j + 1) * GLA_DV] = dgg[j].astype(dgg_ref.dtype)
            dga_ref[:, j * GLA_DV:(j + 1) * GLA_DV] = dga[j].astype(dga_ref.dtype)
        dlr_ref[...] = dlr
        dw2_ref[...] += dw2
        db_ref[...] += db
        dgn_ref[...] += dgn

    return _hosted_call(
        body, rider, name="gla_bwd", grid=(bsz, nc),
        in_specs=[qk(ZB_GQ), qk(ZB_GK), vv(ZB_GV), vv(ZB_GG), vv(ZB_GA), lr(ZS_LR // LANE),
                  full((LANE, GLA_QK)), full((1, GLA_QK)), full((1, GLA_DV)),
                  pl.BlockSpec((1, 1, GLA_HEADS, GLA_DV, GLA_DK), lambda b, n: (b, nc - 1 - n, 0, 0, 0)),
                  vv(0)],
        out_specs=[qk(0), qk(0), vv(0), vv(0), vv(0), lr(0),
                   full((GLA_HEADS, LANE, GLA_DK)), full((GLA_HEADS, 1, GLA_DK)), full((1, GLA_DV))],
        out_shape=[jax.ShapeDtypeStruct((t, GLA_QK), BF16), jax.ShapeDtypeStruct((t, GLA_QK), BF16),
                   jax.ShapeDtypeStruct((t, GLA_V), BF16), jax.ShapeDtypeStruct((t, GLA_V), BF16),
                   jax.ShapeDtypeStruct((t, GLA_V), BF16), jax.ShapeDtypeStruct((t, LANE), F32),
                   jax.ShapeDtypeStruct((GLA_HEADS, LANE, GLA_DK), F32),
                   jax.ShapeDtypeStruct((GLA_HEADS, 1, GLA_DK), F32),
                   jax.ShapeDtypeStruct((1, GLA_DV), F32)],
        scratch_shapes=[pltpu.VMEM((GLA_HEADS, GLA_DV, GLA_DK), F32)],
        args=(zb, zb, zb, zb, zb, zs, w2p, gla_b, gla_norm, states, dmixed))


DN_HB = DN_HEADS


def _dn_specs(nc, reverse):
    wide = DN_HB * DN_D

    def rows(b, n, h):
        return b * nc + ((nc - 1 - n) if reverse else n)

    def col(base):
        return pl.BlockSpec((CHUNK, wide), lambda b, n, h: (rows(b, n, h), base // wide + h))

    def fixed(c):
        return pl.BlockSpec((CHUNK, LANE), lambda b, n, h: (rows(b, n, h), c))

    head = pl.BlockSpec((1, LANE), lambda b, n, h: (0, 0))
    return rows, col, fixed, head


def _lanes(j):
    return slice(j * DN_D, (j + 1) * DN_D)


def _by_head(ref):
    return jnp.stack([ref[:, _lanes(j)] for j in range(DN_HB)], axis=0)


def _dn_fwd(act, zb, zs, alog_b, dtb_b, dn_norm, mix_gla, bsz, nc, rider=None):
    t = zb.shape[0]
    rows, col, fixed, head = _dn_specs(nc, False)

    def body(q_ref, k_ref, v_ref, za_ref, zb_ref, al_ref, dt_ref, gz_ref, gb_ref, dn_ref, mg_ref,
             o_ref, st_ref, ti_ref, state):
        s = jnp.where(pl.program_id(1) > 0, state[...], 0.0)
        st_ref[0, 0] = s
        res, s_new, t_inv = _dn_chunk(s, _by_head(q_ref), _by_head(k_ref), _by_head(v_ref), za_ref[...], zb_ref[...],
                                      al_ref[...], dt_ref[...], _by_head(gz_ref), _by_head(gb_ref), dn_ref[...])
        ti_ref[0, 0] = t_inv
        for j in range(DN_HB):
            o_ref[:, _lanes(j)] = (res[j] + mg_ref[:, _lanes(j)]).astype(o_ref.dtype)
        state[...] = s_new

    return _hosted_call(
        body, rider, name="dn_fwd", grid=(bsz, nc, DN_HEADS // DN_HB),
        in_specs=[col(0), col(DN_HEADS * DN_D), col(2 * DN_HEADS * DN_D),
                  fixed(ZS_DA // LANE), fixed(ZS_DB // LANE), head, head,
                  col(ZB_DZ), col(ZB_GB), pl.BlockSpec((1, DN_D), lambda b, n, h: (0, 0)), col(0)],
        out_specs=[col(0), pl.BlockSpec((1, 1, DN_HB, DN_D, DN_D), lambda b, n, h: (b, n, h, 0, 0)),
                   pl.BlockSpec((1, 1, DN_HB, CHUNK, CHUNK), lambda b, n, h: (b, n, h, 0, 0))],
        out_shape=[jax.ShapeDtypeStruct((t, D_MODEL), BF16),
                   jax.ShapeDtypeStruct((bsz, nc, DN_HEADS, DN_D, DN_D), F32),
                   jax.ShapeDtypeStruct((bsz, nc, DN_HEADS, CHUNK, CHUNK), F32)],
        scratch_shapes=[pltpu.VMEM((DN_HEADS, DN_D, DN_D), F32)],
        args=(act, act, act, zs, zs, alog_b, dtb_b, zb, zb, dn_norm, mix_gla))


def _dn_bwd(act, zb, zs, alog_b, dtb_b, dn_norm, states, t_invs, dmixed, bsz, nc, rider=None):
    t = zb.shape[0]
    rows, col, fixed, head = _dn_specs(nc, True)

    def body(q_ref, k_ref, v_ref, za_ref, zb_ref, al_ref, dt_ref, gz_ref, gb_ref, dn_ref, st_ref, ti_ref, dm_ref,
             dact_ref, dza_ref, dzb_ref, dgz_ref, dgb_ref, dal_ref, ddt_ref, ddn_ref, dstate):
        b, n = pl.program_id(0), pl.program_id(1)

        @pl.when((b == 0) & (n == 0))
        def _():
            dal_ref[...] = jnp.zeros_like(dal_ref)
            ddt_ref[...] = jnp.zeros_like(ddt_ref)
            ddn_ref[...] = jnp.zeros_like(ddn_ref)

        fn = functools.partial(_dn_chunk, t_saved=ti_ref[0, 0])
        _, vjp = jax.vjp(fn, st_ref[0, 0], _by_head(q_ref), _by_head(k_ref), _by_head(v_ref), za_ref[...],
                         zb_ref[...], al_ref[...], dt_ref[...], _by_head(gz_ref), _by_head(gb_ref), dn_ref[...])
        ds_in = jnp.where(n > 0, dstate[...], 0.0)
        ds, dq, dk, dv, dza, dzb, dal, ddt, dgz, dgb, ddn = vjp((_by_head(dm_ref).astype(F32), ds_in))
        dstate[...] = ds
        for j in range(DN_HB):
            for part, d in enumerate((dq, dk, dv)):
                dact_ref[:, pl.ds(part * DN_HEADS * DN_D + j * DN_D, DN_D)] = d[j]
            dgz_ref[:, _lanes(j)] = dgz[j].astype(dgz_ref.dtype)
            dgb_ref[:, _lanes(j)] = dgb[j].astype(dgb_ref.dtype)
        dal_ref[...] += dal
        ddt_ref[...] += ddt
        dza_ref[...] = dza
        dzb_ref[...] = dzb
        ddn_ref[...] += ddn

    full = lambda shape: pl.BlockSpec(shape, lambda b, n, h: (0,) * len(shape))
    return _hosted_call(
        body, rider, name="dn_bwd", grid=(bsz, nc, DN_HEADS // DN_HB),
        in_specs=[col(0), col(DN_HEADS * DN_D), col(2 * DN_HEADS * DN_D),
                  fixed(ZS_DA // LANE), fixed(ZS_DB // LANE), head, head,
                  col(ZB_DZ), col(ZB_GB), pl.BlockSpec((1, DN_D), lambda b, n, h: (0, 0)),
                  pl.BlockSpec((1, 1, DN_HB, DN_D, DN_D), lambda b, n, h: (b, nc - 1 - n, h, 0, 0)),
                  pl.BlockSpec((1, 1, DN_HB, CHUNK, CHUNK), lambda b, n, h: (b, nc - 1 - n, h, 0, 0)), col(0)],
        out_specs=[pl.BlockSpec((CHUNK, DN_QKV), lambda b, n, h: (rows(b, n, h), 0)), fixed(0), fixed(0), col(0), col(0),
                   full((1, LANE)), full((1, LANE)), full((1, DN_D))],
        out_shape=[jax.ShapeDtypeStruct((t, DN_QKV), F32),
                   jax.ShapeDtypeStruct((t, LANE), F32), jax.ShapeDtypeStruct((t, LANE), F32),
                   jax.ShapeDtypeStruct((t, D_MODEL), BF16), jax.ShapeDtypeStruct((t, D_MODEL), BF16),
                   jax.ShapeDtypeStruct((1, LANE), F32), jax.ShapeDtypeStruct((1, LANE), F32),
                   jax.ShapeDtypeStruct((1, DN_D), F32)],
        scratch_shapes=[pltpu.VMEM((DN_HEADS, DN_D, DN_D), F32)],
        args=(act, act, act, zs, zs, alog_b, dtb_b, zb, zb, dn_norm, states, t_invs, dmixed))


MM_VMEM_BUDGET = 40 * 1024 * 1024
MM_TILE_PREF = (1024, 1024, 2048)


def _divisor_tile(n, cap):
    if n <= cap:
        return n
    for c in range(cap - cap % LANE, 0, -LANE):
        if n % c == 0:
            return c
    return n


def _mm_tiles(m, n, kd, a_bytes, b_bytes, mn_bytes):
    tm, tn, tk = (_divisor_tile(d, c) for d, c in zip((m, n, kd), MM_TILE_PREF))

    def need(tm, tn, tk):
        acc = 0 if tk == kd else 4 * tm * tn
        return 2 * (tm * tk * a_bytes + tk * tn * b_bytes + tm * tn * mn_bytes) + acc + 4 * tm * tn

    while need(tm, tn, tk) > MM_VMEM_BUDGET:
        if tk > 512 and tk * max(tm * a_bytes, tn * b_bytes) >= tm * tn * mn_bytes:
            tk = _divisor_tile(kd, tk // 2)
        elif tn >= tm and tn > LANE:
            tn = _divisor_tile(n, tn // 2)
        else:
            tm = _divisor_tile(m, tm // 2)
    return tm, tn, tk


def _w_in_row_of(tile, tile_rows):
    skipped = jnp.where(tile >= ZB_GA // tile_rows, WI_GA - ZB_GA, jnp.where(tile >= ZB_DQKV // tile_rows,
                                                                             WI_DQKV - ZB_DQKV, 0))
    return pl.multiple_of(tile * tile_rows + skipped, 16)


def _mm(a, b, *, ta=False, tb=False, out_dtypes=(F32,), epilogue=None, extras=(), name, rider=None,
        b_is_w_in_t=False, out_is_w_in_t=False):
    m, kd = (a.shape[1], a.shape[0]) if ta else a.shape
    n = b.shape[0] if tb else b.shape[1]
    if b_is_w_in_t:
        n, kd = (ZB_W, kd) if tb else (n, ZB_W)
    mn_bytes = sum(e.dtype.itemsize for e in extras) + sum(jnp.dtype(dt).itemsize for dt in out_dtypes)
    tm, tn, tk = _mm_tiles(m, n, kd, a.dtype.itemsize, b.dtype.itemsize, mn_bytes)
    nk = kd // tk
    n_ex = len(extras)
    dims = (((0,) if ta else (1,), (1,) if tb else (0,)), ((), ()))

    def finish(acc, ex_refs, out_refs):
        outs = (acc,) if epilogue is None else epilogue(acc, *[r[...] for r in ex_refs])
        for r, o in zip(out_refs, outs):
            r[...] = o.astype(r.dtype)

    def partial_product(a_ref, b_ref):
        return lax.dot_general(a_ref[...].astype(BF16), b_ref[...].astype(BF16), dims, preferred_element_type=F32)

    def body_single(*refs):
        finish(partial_product(refs[0], refs[1]), refs[2:2 + n_ex], refs[2 + n_ex:])

    def body_acc(*refs):
        acc = refs[-1]
        k = pl.program_id(2)

        @pl.when(k == 0)
        def _():
            acc[...] = partial_product(refs[0], refs[1])

        @pl.when(k > 0)
        def _():
            acc[...] += partial_product(refs[0], refs[1])

        @pl.when(k == nk - 1)
        def _():
            finish(acc[...], refs[2:2 + n_ex], refs[2 + n_ex:-1])

    a_spec = pl.BlockSpec((tk, tm), lambda i, j, k: (k, i)) if ta else pl.BlockSpec((tm, tk), lambda i, j, k: (i, k))
    b_spec = pl.BlockSpec((tn, tk), lambda i, j, k: (j, k)) if tb else pl.BlockSpec((tk, tn), lambda i, j, k: (k, j))
    mn_spec = pl.BlockSpec((tm, tn), lambda i, j, k: (i, j))
    out_spec, out_rows = mn_spec, m
    if b_is_w_in_t and tb:
        b_spec = pl.BlockSpec((pl.Element(tn), pl.Element(tk)),
                              lambda i, j, k: (_w_in_row_of(j, tn), pl.multiple_of(k * tk, LANE)))
    elif b_is_w_in_t:
        b_spec = pl.BlockSpec((pl.Element(tk), pl.Element(tn)),
                              lambda i, j, k: (_w_in_row_of(k, tk), pl.multiple_of(j * tn, LANE)))
    if out_is_w_in_t:
        out_spec, out_rows = pl.BlockSpec((pl.Element(tm), pl.Element(tn)),
                                          lambda i, j, k: (_w_in_row_of(i, tm), pl.multiple_of(j * tn, LANE))), D_IN
    outs = _hosted_call(
        body_single if nk == 1 else body_acc, rider, name=name, grid=(m // tm, n // tn, nk),
        in_specs=[a_spec, b_spec] + [mn_spec] * n_ex,
        out_specs=[out_spec] * len(out_dtypes),
        out_shape=[jax.ShapeDtypeStruct((out_rows, n), dt) for dt in out_dtypes],
        scratch_shapes=[] if nk == 1 else [pltpu.VMEM((tm, tn), F32)],
        args=(a, b, *extras))
    return outs[0] if len(outs) == 1 else outs


ROW_BLOCK = 256


def _row_spec(width=D_MODEL):
    return pl.BlockSpec((ROW_BLOCK, width), lambda i: (i, 0))


def _vec_spec(width=D_MODEL):
    return pl.BlockSpec((1, width), lambda i: (0, 0))


def _rms_fwd(x, g, name):
    def body(x_ref, g_ref, h_ref):
        xf = x_ref[...]
        h_ref[...] = (xf * lax.rsqrt(jnp.mean(xf * xf, axis=-1, keepdims=True) + EPS) * g_ref[...]).astype(BF16)

    t = x.shape[0]
    return pl.pallas_call(
        body, name=name, grid=(t // ROW_BLOCK,), in_specs=[_row_spec(), _vec_spec()], out_specs=_row_spec(),
        out_shape=jax.ShapeDtypeStruct((t, D_MODEL), BF16), compiler_params=_cparams(("parallel",)),
    )(x, g)


def _rms_bwd_math(xf, g, dh):
    rstd = lax.rsqrt(jnp.mean(xf * xf, axis=-1, keepdims=True) + EPS)
    xhat = xf * rstd
    dxhat = dh * g
    dx = rstd * (dxhat - xhat * jnp.mean(dxhat * xhat, axis=-1, keepdims=True))
    dg = jnp.sum(dh * xhat, axis=0, keepdims=True)
    return dx, dg


def _rms_bwd(x, g, dh, dres, name):
    def body(x_ref, g_ref, dh_ref, dres_ref, dx_ref, dg_ref):
        dx, dg = _rms_bwd_math(x_ref[...], g_ref[...], dh_ref[...].astype(F32))
        dx_ref[...] = dres_ref[...] + dx

        @pl.when(pl.program_id(0) == 0)
        def _():
            dg_ref[...] = jnp.zeros_like(dg_ref)

        dg_ref[...] += dg

    t = x.shape[0]
    return pl.pallas_call(
        body, name=name, grid=(t // ROW_BLOCK,),
        in_specs=[_row_spec(), _vec_spec(), _row_spec(), _row_spec()], out_specs=[_row_spec(), _vec_spec()],
        out_shape=[jax.ShapeDtypeStruct((t, D_MODEL), F32), jax.ShapeDtypeStruct((1, D_MODEL), F32)],
        compiler_params=_cparams(("arbitrary",)),
    )(x, g, dh, dres)


def _loss_head(x3, g, target):
    def body(x_ref, g_ref, t_ref, dx_ref, dg_ref, loss_ref):
        xf, gg = x_ref[...], g_ref[...]
        rstd = lax.rsqrt(jnp.mean(xf * xf, axis=-1, keepdims=True) + EPS)
        err = xf * rstd * gg - t_ref[...]
        dx, dg = _rms_bwd_math(xf, gg, err * (1.0 / D_MODEL))
        dx_ref[...] = dx

        @pl.when(pl.program_id(0) == 0)
        def _():
            dg_ref[...] = jnp.zeros_like(dg_ref)
            loss_ref[...] = jnp.zeros_like(loss_ref)

        dg_ref[...] += dg
        part = jnp.sum(jnp.sum(err * err, axis=-1, keepdims=True), axis=0, keepdims=True) * (0.5 / D_MODEL)
        loss_ref[...] += jnp.broadcast_to(part, loss_ref.shape)

    t = x3.shape[0]
    return pl.pallas_call(
        body, name="loss_head", grid=(t // ROW_BLOCK,),
        in_specs=[_row_spec(), _vec_spec(), _row_spec()], out_specs=[_row_spec(), _vec_spec(), _vec_spec(LANE)],
        out_shape=[jax.ShapeDtypeStruct((t, D_MODEL), F32), jax.ShapeDtypeStruct((1, D_MODEL), F32),
                   jax.ShapeDtypeStruct((1, LANE), F32)],
        compiler_params=_cparams(("arbitrary",)),
    )(x3, g, target)


def _ple_bwd(dx3, gpre, pp):
    def body(dx_ref, gp_ref, pp_ref, dgp_ref, dpp_ref):
        dx, sg = dx_ref[...], _sigmoid(gp_ref[...])
        dpp_ref[...] = (dx * sg).astype(BF16)
        dgp_ref[...] = (dx * pp_ref[...] * sg * (1.0 - sg)).astype(BF16)

    t = dx3.shape[0]
    return pl.pallas_call(
        body, name="ple_bwd", grid=(t // ROW_BLOCK,), in_specs=[_row_spec()] * 3, out_specs=[_row_spec()] * 2,
        out_shape=[jax.ShapeDtypeStruct((t, D_MODEL), BF16)] * 2, compiler_params=_cparams(("parallel",)),
    )(dx3, gpre, pp)


CONV_COLS = 256


def _shift_down(x, s):
    if s == 0:
        return x
    return jnp.where(_iota2(x.shape, 0) >= s, pltpu.roll(x, s, 0), 0.0)


def _shift_up(x, s):
    if s == 0:
        return x
    rows = x.shape[0]
    return jnp.where(_iota2(x.shape, 0) < rows - s, pltpu.roll(x, rows - s, 0), 0.0)


def _conv_taps(xf):
    return [_shift_down(xf, DN_CONV - 1 - j) for j in range(DN_CONV)]


def _conv_pre(taps, w):
    return sum(tap * w[j:j + 1, :] for j, tap in enumerate(taps))


def _conv_fwd(zb, conv_w, bsz, seq):
    def body(x_ref, w_ref, y_ref):
        y_ref[...] = _silu(_conv_pre(_conv_taps(x_ref[...]), w_ref[...]))

    nblk = DN_QKV // CONV_COLS
    return pl.pallas_call(
        body, name="conv_fwd", grid=(bsz, nblk),
        in_specs=[pl.BlockSpec((seq, CONV_COLS), lambda b, j: (b, ZB_DQKV // CONV_COLS + j)),
                  pl.BlockSpec((DN_CONV, CONV_COLS), lambda b, j: (0, j))],
        out_specs=pl.BlockSpec((seq, CONV_COLS), lambda b, j: (b, j)),
        out_shape=jax.ShapeDtypeStruct((bsz * seq, DN_QKV), F32),
        compiler_params=_cparams(("parallel", "parallel")),
    )(zb, conv_w)


def _conv_bwd(zb, conv_w, dact, bsz, seq):
    def body(x_ref, w_ref, dy_ref, dx_ref, dw_ref):
        taps, w = _conv_taps(x_ref[...]), w_ref[...]
        c = _conv_pre(taps, w)
        sg = _sigmoid(c)
        dc = dy_ref[...].astype(F32) * sg * (1.0 + c * (1.0 - sg))
        dx = sum(_shift_up(dc, DN_CONV - 1 - j) * w[j:j + 1, :] for j in range(DN_CONV))
        dx_ref[...] = dx.astype(BF16)
        dw = jnp.concatenate([jnp.sum(dc * tap, axis=0, keepdims=True) for tap in taps], axis=0)

        @pl.when(pl.program_id(1) == 0)
        def _():
            dw_ref[...] = jnp.zeros_like(dw_ref)

        dw_ref[...] += dw

    nblk = DN_QKV // CONV_COLS
    return pl.pallas_call(
        body, name="conv_bwd", grid=(nblk, bsz),
        in_specs=[pl.BlockSpec((seq, CONV_COLS), lambda j, b: (b, ZB_DQKV // CONV_COLS + j)),
                  pl.BlockSpec((DN_CONV, CONV_COLS), lambda j, b: (0, j)),
                  pl.BlockSpec((seq, CONV_COLS), lambda j, b: (b, j))],
        out_specs=[pl.BlockSpec((seq, CONV_COLS), lambda j, b: (b, j)),
                   pl.BlockSpec((DN_CONV, CONV_COLS), lambda j, b: (0, j))],
        out_shape=[jax.ShapeDtypeStruct((bsz * seq, DN_QKV), BF16), jax.ShapeDtypeStruct((DN_CONV, DN_QKV), F32)],
        compiler_params=_cparams(("parallel", "arbitrary")),
    )(zb, conv_w, dact)


MESH_IDS = pl.DeviceIdType.MESH
ANY_SPEC = pl.BlockSpec(memory_space=pl.ANY)


COMM_SCRATCH = (pltpu.SemaphoreType.DMA((7,)), pltpu.SemaphoreType.DMA((7,)), pltpu.SemaphoreType.DMA)


def _gather_phases(x_ref, out_ref, send_sems, recv_sems, local_sem):
    mx, my, mc = lax.axis_index("x"), lax.axis_index("y"), lax.axis_index("c")
    me, sibling = (mx, my, mc), (mx, my, 1 - mc)
    chips = [(1 - mx, my), (mx, 1 - my), (1 - mx, 1 - my)]

    def slot(px, py, pc):
        return out_ref.at[4 * px + 2 * py + pc]

    def copy(k, block, to, src=None):
        return pltpu.make_async_remote_copy(
            src_ref=slot(*block) if src is None else src, dst_ref=slot(*block),
            send_sem=send_sems.at[k], recv_sem=recv_sems.at[k], device_id=to, device_id_type=MESH_IDS)

    def mine():
        return pltpu.make_async_copy(x_ref, slot(*me), local_sem)

    def first():
        return [copy(0, me, sibling, src=x_ref)] + [copy(1 + j, me, (*chip, mc), src=x_ref)
                                                    for j, chip in enumerate(chips)]

    def passed():
        return [copy(4 + j, (*chip, mc), sibling) for j, chip in enumerate(chips)]

    def start():
        mine().start()
        for cp in first():
            cp.start()

    def forward():
        for j, (chip, cp) in enumerate(zip(chips, passed())):
            copy(1 + j, (*chip, mc), me).wait_recv()
            cp.start()

    def finish():
        copy(0, sibling, me).wait_recv()
        for j, chip in enumerate(chips):
            copy(4 + j, (*chip, 1 - mc), me).wait_recv()
        for cp in first() + passed():
            cp.wait_send()
        mine().wait()

    return start, forward, finish


def _scatter_phases(x_ref, out_ref, send_sems, recv_sems, local_sem, among_chips=False):
    mx, my, mc = lax.axis_index("x"), lax.axis_index("y"), lax.axis_index("c")
    n_peers = 4 if among_chips else N_DEV
    me = 2 * mx + my if among_chips else 4 * mx + 2 * my + mc

    def peer(k):
        if among_chips:
            return (mx ^ ((k >> 1) & 1), my ^ (k & 1), mc)
        return (mx ^ ((k >> 2) & 1), my ^ ((k >> 1) & 1), mc ^ (k & 1))

    def slot_of(k):
        px, py, pc = peer(k)
        return 2 * px + py if among_chips else 4 * px + 2 * py + pc

    def copy(k, src_slot, dst_slot):
        return pltpu.make_async_remote_copy(
            src_ref=x_ref.at[src_slot], dst_ref=out_ref.at[dst_slot],
            send_sem=send_sems.at[k - 1], recv_sem=recv_sems.at[k - 1],
            device_id=peer(k), device_id_type=MESH_IDS)

    def sends():
        return [copy(k, slot_of(k), me) for k in range(1, n_peers)]

    def mine():
        return pltpu.make_async_copy(x_ref.at[me], out_ref.at[me], local_sem)

    def start():
        mine().start()
        for cp in sends():
            cp.start()

    def forward():
        pass

    def finish():
        for k in range(1, n_peers):
            copy(k, me, slot_of(k)).wait_recv()
        for cp in sends():
            cp.wait_send()
        mine().wait()

    return start, forward, finish


def _pair_phases(x_ref, out_ref, send_sems, recv_sems, local_sem):
    mx, my, mc = lax.axis_index("x"), lax.axis_index("y"), lax.axis_index("c")

    def copy(side):
        return pltpu.make_async_remote_copy(
            src_ref=x_ref.at[:, side], dst_ref=out_ref, send_sem=send_sems.at[0], recv_sem=recv_sems.at[0],
            device_id=(mx, my, 1 - mc), device_id_type=MESH_IDS)

    def start():
        copy(1 - mc).start()

    def forward():
        pass

    def finish():
        copy(mc).wait_recv()
        copy(1 - mc).wait_send()

    return start, forward, finish


class _Rider:
    def __init__(self, phases, x, out_shape):
        self.phases, self.x, self.out_shape = phases, x, out_shape


def _gather_rider(x):
    return _Rider(_gather_phases, x, jax.ShapeDtypeStruct((N_DEV,) + x.shape, x.dtype))


def _scatter_rider(x):
    return _Rider(_scatter_phases, x, jax.ShapeDtypeStruct(x.shape, x.dtype))


def _chip_scatter_rider(x):
    return _Rider(functools.partial(_scatter_phases, among_chips=True), x, jax.ShapeDtypeStruct(x.shape, x.dtype))


def _pair_rider(x):
    return _Rider(_pair_phases, x, jax.ShapeDtypeStruct((x.shape[0],) + x.shape[2:], x.dtype))


def _exchange(rider, name):
    def body(x_ref, out_ref, send_sems, recv_sems, local_sem):
        for phase in rider.phases(x_ref, out_ref, send_sems, recv_sems, local_sem):
            phase()

    return pl.pallas_call(body, name=name, out_shape=rider.out_shape, in_specs=[ANY_SPEC], out_specs=ANY_SPEC,
                          scratch_shapes=list(COMM_SCRATCH))(rider.x)


def _all_gather(x, name):
    return _exchange(_gather_rider(x), name)


def _all_to_all(x, name):
    return _exchange(_scatter_rider(x), name)


def _hosted_call(body, rider, *, name, grid, in_specs, out_specs, out_shape, scratch_shapes, args):
    if rider is None:
        return pl.pallas_call(body, name=name, grid=grid, in_specs=in_specs, out_specs=out_specs, out_shape=out_shape,
                              scratch_shapes=scratch_shapes, compiler_params=_cparams(("arbitrary",) * len(grid)))(*args)
    n_in, n_out, n_scr = len(in_specs), len(out_specs), len(scratch_shapes)
    total = math.prod(grid)

    def riding(*refs):
        host_in, x_ref = refs[:n_in], refs[n_in]
        host_out, out_ref = refs[n_in + 1:n_in + 1 + n_out], refs[n_in + 1 + n_out]
        host_scr = refs[n_in + 2 + n_out:n_in + 2 + n_out + n_scr]
        start, forward, finish = rider.phases(x_ref, out_ref, *refs[n_in + 2 + n_out + n_scr:])
        step = 0
        for axis, size in enumerate(grid):
            step = step * size + pl.program_id(axis)
        pl.when(step == 0)(start)
        pl.when(step == (3 * total) // 4)(forward)
        body(*host_in, *host_out, *host_scr)
        pl.when(step == total - 1)(finish)

    return pl.pallas_call(
        riding, name=name, grid=grid, in_specs=list(in_specs) + [ANY_SPEC], out_specs=list(out_specs) + [ANY_SPEC],
        out_shape=list(out_shape) + [rider.out_shape], scratch_shapes=list(scratch_shapes) + list(COMM_SCRATCH),
        compiler_params=_cparams(("arbitrary",) * len(grid)))(*args, rider.x)


def _adamw_math(w, g, m, v):
    m = ADAM_B1 * m + (1.0 - ADAM_B1) * g
    v = ADAM_B2 * v + (1.0 - ADAM_B2) * jnp.square(g)
    m_hat = m / (1.0 - ADAM_B1 ** ADAM_STEP)
    v_hat = v / (1.0 - ADAM_B2 ** ADAM_STEP)
    delta = -ADAM_LR * (m_hat / (jnp.sqrt(v_hat) + ADAM_EPS) + ADAM_WD * w)
    return delta, m, v


ADAM_ROWS = 128


def _elementwise_tile(rows, cols):
    if rows % ADAM_ROWS == 0:
        return ADAM_ROWS, cols
    return rows, (2 * LANE if cols % (2 * LANE) == 0 else cols)


def _add_blocks(a, b):
    g, rows, cols = a.shape
    tr, tc = _elementwise_tile(rows, cols)

    def body(a_ref, b_ref, o_ref):
        o_ref[...] = (a_ref[...].astype(F32) + b_ref[...].astype(F32)).astype(o_ref.dtype)

    blk = pl.BlockSpec((None, tr, tc), lambda k, i, j: (k, i, j))
    return pl.pallas_call(body, name="add_pair_blocks", grid=(g, rows // tr, cols // tc), in_specs=[blk, blk],
                          out_specs=blk, out_shape=jax.ShapeDtypeStruct(a.shape, a.dtype),
                          compiler_params=_cparams(("parallel", "parallel", "parallel")))(a, b)


def _adamw_reduce(w, m, v, parts, row0, name):
    rows, cols = w.shape
    n_parts = parts.shape[0]
    tr, tc = _elementwise_tile(rows, cols)
    r0 = row0 // tr

    def body(w_ref, m_ref, v_ref, *refs):
        part_refs, (g_ref, d_ref, nm_ref, nv_ref) = refs[:n_parts], refs[n_parts:]
        g = part_refs[0][...].astype(F32)
        for r in part_refs[1:]:
            g = g + r[...].astype(F32)
        delta, nm, nv = _adamw_math(w_ref[...], g, m_ref[...], v_ref[...])
        g_ref[...] = g
        d_ref[...] = delta
        nm_ref[...] = nm
        nv_ref[...] = nv

    blk = pl.BlockSpec((tr, tc), lambda i, j: (i, j))
    part_specs = [pl.BlockSpec((None, tr, tc), functools.partial(lambda i, j, k: (k, r0 + i, j), k=k))
                  for k in range(n_parts)]
    return pl.pallas_call(
        body, name=name, grid=(rows // tr, cols // tc), in_specs=[blk] * 3 + part_specs, out_specs=[blk] * 4,
        out_shape=[jax.ShapeDtypeStruct(w.shape, F32)] * 4, compiler_params=_cparams(("parallel", "parallel")),
    )(w, m, v, *([parts] * n_parts))


def _small_reduce(gathered):
    r = gathered.shape[1]

    def body(g_ref, o_ref):
        g = g_ref[0]
        for k in range(1, N_DEV):
            g = g + g_ref[k]
        o_ref[...] = g

    return pl.pallas_call(body, name="small_grad_reduce", out_shape=jax.ShapeDtypeStruct((r, LANE), F32))(gathered)


def _adamw_small(w, m, v, g):
    def body(w_ref, m_ref, v_ref, g_ref, d_ref, nm_ref, nv_ref):
        d_ref[...], nm_ref[...], nv_ref[...] = _adamw_math(w_ref[...], g_ref[...], m_ref[...], v_ref[...])

    return pl.pallas_call(body, name="adamw_small", out_shape=[jax.ShapeDtypeStruct(w.shape, F32)] * 3)(w, m, v, g)


def _pack_rows(arrays):
    rows = [jnp.pad(a.reshape(-1), (0, -a.size % LANE)).reshape(-1, LANE) for a in arrays]
    out = jnp.concatenate(rows, axis=0)
    return jnp.pad(out, ((0, -out.shape[0] % 8), (0, 0)))


def _unpack_rows(packed, shapes):
    out, r = [], 0
    for shp in shapes:
        size = math.prod(shp)
        nrows = -(-size // LANE)
        out.append(packed[r:r + nrows].reshape(-1)[:size].reshape(shp))
        r += nrows
    return out


def _add_residual(acc, res):
    return (res + acc,)


def _by_cols(g):
    return g.reshape(g.shape[0], N_DEV, -1).transpose(1, 0, 2)


def _from_cols(blocks):
    return blocks.transpose(1, 0, 2).reshape(blocks.shape[1], -1)


ROWS_OUT = D_MODEL // N_DEV

W_IN_SEGMENTS = ((0, WI_LR, "big", 0), (WI_LR, WI_DQKV, "gates", ZS_LR), (WI_DQKV, WI_DA, "big", ZB_DQKV),
                 (WI_DA, WI_DB, "gates", ZS_DA), (WI_DB, WI_GA, "gates", ZS_DB), (WI_GA, D_IN, "big", ZB_GA))


def _w_in_gate_rows(wt):
    parts = []
    for lo, hi, which, _ in W_IN_SEGMENTS:
        if which == "gates":
            parts += [wt[lo:hi], jnp.zeros((LANE - (hi - lo), wt.shape[1]), wt.dtype)]
    return jnp.concatenate(parts, axis=0)


def _fill_gate_rows(dw_in_t, dws_t):
    for lo, hi, which, first in W_IN_SEGMENTS:
        if which == "gates":
            dw_in_t = lax.dynamic_update_slice(dw_in_t, dws_t[first:first + hi - lo], (lo, 0))
    return dw_in_t


def _local_step(x, p, target, w, up_shard, rows_shard):
    bsz, seq, _ = x.shape
    t, nc = bsz * seq, seq // CHUNK
    x0, p2, tgt = x.reshape(t, D_MODEL), p.reshape(t, PLE_DIM), target.reshape(t, D_MODEL)

    h = _rms_fwd(x0, w["g_mix"], "rms_mix")
    zb, up_blocks = _mm(h, w["w_in_t"], tb=True, b_is_w_in_t=True, name="in_proj", rider=_gather_rider(up_shard))
    w_up = _from_cols(up_blocks)
    zs = _mm(h, w["ws_t"], tb=True, name="in_proj_gates")
    act = _conv_fwd(zb, w["conv"], bsz, seq)
    mix_gla, gla_states = _gla_fwd(zb, zs, w["w2p"], w["gla_b"], w["gla_norm"], bsz, nc)
    mixed, dn_states, dn_t_invs, row_blocks = _dn_fwd(act, zb, zs, w["alog_b"], w["dtb_b"], w["dn_norm"], mix_gla,
                                                      bsz, nc, rider=_gather_rider(rows_shard))
    w_out = row_blocks[:, :ROWS_OUT].reshape(D_MODEL, D_MODEL)
    w_pg = row_blocks[:, ROWS_OUT:2 * ROWS_OUT].reshape(D_MODEL, D_MODEL)
    w_down = row_blocks[:, 2 * ROWS_OUT:].reshape(D_FF, D_MODEL)
    x1 = _mm(mixed, w_out, epilogue=_add_residual, extras=(x0,), name="out_proj")
    h2 = _rms_fwd(x1, w["g_mlp"], "rms_mlp")
    u, a = _mm(h2, w_up, out_dtypes=(BF16, BF16), name="mlp_up",
               epilogue=lambda acc: (acc, jnp.square(jnp.maximum(acc, 0.0))))
    x2 = _mm(a, w_down, epilogue=_add_residual, extras=(x1,), name="mlp_down")
    h3 = _rms_fwd(x2, w["g_ple"], "rms_ple")
    pp = _mm(p2, w["w_pp"], name="ple_proj")
    gpre, x3 = _mm(h3, w_pg, out_dtypes=(F32, F32), extras=(x2, pp), name="ple_gate",
                   epilogue=lambda acc, res, proj: (acc, res + _sigmoid(acc) * proj))
    dx3, dg_final, loss = _loss_head(x3, w["g_final"], tgt)

    dgpre, dpp = _ple_bwd(dx3, gpre, pp)
    dw_pp = _mm(p2, dpp, ta=True, out_dtypes=(BF16,), name="d_w_ple_proj")
    dw_pg = _mm(h3, dgpre, ta=True, out_dtypes=(BF16,), name="d_w_ple_gate")
    dh3 = _mm(dgpre, w_pg, tb=True, name="d_h_ple")
    dx2, dg_ple = _rms_bwd(x2, w["g_ple"], dh3, dx3, "rms_ple_bwd")
    du = _mm(dx2, w_down, tb=True, out_dtypes=(BF16,), extras=(u,), name="d_mlp_hidden",
             epilogue=lambda acc, uu: (acc * (2.0 * jnp.maximum(uu.astype(F32), 0.0)),))
    dw_down = _mm(a, dx2, ta=True, out_dtypes=(BF16,), name="d_w_down")
    dw_up = _mm(h2, du, ta=True, out_dtypes=(BF16,), name="d_w_up")
    dh2 = _mm(du, w_up, tb=True, name="d_h_mlp")
    dx1, dg_mlp = _rms_bwd(x1, w["g_mlp"], dh2, dx2, "rms_mlp_bwd")
    dmixed = _mm(dx1, w_out, tb=True, out_dtypes=(BF16,), name="d_mixed")
    dw_out = _mm(mixed, dx1, ta=True, out_dtypes=(BF16,), name="d_w_out")

    d_rows = jnp.concatenate([dw_out.reshape(N_DEV, ROWS_OUT, D_MODEL), dw_pg.reshape(N_DEV, ROWS_OUT, D_MODEL),
                              dw_down.reshape(N_DEV, D_FF // N_DEV, D_MODEL)], axis=1)
    (dact, dza, dzb_, dgz, dgb, dal, ddt, ddn, recv_rows) = _dn_bwd(
        act, zb, zs, w["alog_b"], w["dtb_b"], w["dn_norm"], dn_states, dn_t_invs, dmixed, bsz, nc,
        rider=_scatter_rider(d_rows))
    (gdq, gdk, gdv, dgg, dga, dlr, dw2, dgla_b, dgla_norm, recv_up) = _gla_bwd(
        zb, zs, w["w2p"], w["gla_b"], w["gla_norm"], gla_states, dmixed, bsz, nc,
        rider=_scatter_rider(_by_cols(dw_up)))
    dqkv, dconv = _conv_bwd(zb, w["conv"], dact, bsz, seq)
    dzb = jnp.concatenate([gdq, gdk, gdv, dgg, dqkv, dgz, dga, dgb], axis=1)
    dzs = jnp.concatenate([dlr, dza, dzb_], axis=1)
    dw_in_t = _mm(dzb, h, ta=True, out_dtypes=(BF16,), out_is_w_in_t=True, name="d_w_in")
    dws_t = _mm(dzs, h, ta=True, out_dtypes=(BF16,), name="d_w_in_gates")
    by_chip = _fill_gate_rows(dw_in_t, dws_t).reshape(N_DEV // 2, 2, D_IN_SHARD, D_MODEL)
    from_sibling = _exchange(_pair_rider(by_chip), "pair_d_w_in")
    mine = lax.dynamic_index_in_dim(by_chip, lax.axis_index("c"), axis=1, keepdims=False)
    chip_sums = _add_blocks(mine, from_sibling)
    dh_gates = _mm(dzs, w["ws_t"], name="d_h_mix_gates")
    dh, recv_in = _mm(dzb, w["w_in_t"], b_is_w_in_t=True, epilogue=_add_residual, extras=(dh_gates,), name="d_h_mix",
                      rider=_chip_scatter_rider(chip_sums))
    gx, dg_mix = _rms_bwd(x0, w["g_mix"], dh, dx1, "rms_mix_bwd")

    dgla_w2 = dw2[:, :GLA_LOWRANK, :].transpose(1, 0, 2).reshape(GLA_LOWRANK, GLA_QK)
    return dict(
        loss=loss[0, 0], grad_x=gx.reshape(x.shape), recv_in=recv_in, recv_up=recv_up, recv_rows=recv_rows,
        w_ple_proj=dw_pp,
        g_mix=dg_mix, gla_b=dgla_b.reshape(1, GLA_QK), gla_norm=dgla_norm, dn_norm=ddn, g_mlp=dg_mlp, g_ple=dg_ple,
        g_final=dg_final, gla_w2=dgla_w2, dn_conv=dconv,
        dn_a_log=dal[:, :DN_HEADS], dn_dt_bias=ddt[:, :DN_HEADS])


def _first_weights(g_mix, w_in, gla_w2, gla_b, gla_norm, dn_conv, dn_a_log, dn_dt_bias, dn_norm, g_mlp, g_ple,
                   w_ple_proj, g_final):
    w_in_t = _all_gather(jnp.swapaxes(w_in[0], 0, 1).astype(BF16), "gather_w_in").reshape(D_IN, D_MODEL)
    w_pp = _all_gather(w_ple_proj[0].astype(BF16), "gather_w_ple_proj")
    small = _all_gather(_pack_rows([gla_w2[0], dn_conv[0]]), "gather_w_small")
    n_w2 = GLA_LOWRANK * GLA_QK // N_DEV // LANE
    n_cv = DN_CONV * DN_QKV // N_DEV // LANE
    w2 = small[:, :n_w2].reshape(N_DEV, GLA_LOWRANK, GLA_QK // N_DEV).transpose(1, 0, 2).reshape(GLA_LOWRANK, GLA_QK)
    conv = small[:, n_w2:n_w2 + n_cv].reshape(N_DEV, DN_CONV, DN_QKV // N_DEV).transpose(1, 0, 2).reshape(DN_CONV, DN_QKV)

    return dict(
        w_in_t=w_in_t, ws_t=_w_in_gate_rows(w_in_t), w_pp=_from_cols(w_pp),
        w2p=jnp.pad(w2, ((0, LANE - GLA_LOWRANK), (0, 0))), conv=conv,
        alog_b=jnp.pad(dn_a_log, ((0, 0), (0, LANE - DN_HEADS))),
        dtb_b=jnp.pad(dn_dt_bias, ((0, 0), (0, LANE - DN_HEADS))),
        g_mix=g_mix, gla_b=gla_b, gla_norm=gla_norm, dn_norm=dn_norm, g_mlp=g_mlp, g_ple=g_ple,
        g_final=g_final.reshape(1, D_MODEL))


def kernel(x, p, g_mix, w_in, gla_w2, gla_b, gla_norm, dn_conv, dn_a_log, dn_dt_bias, dn_norm, w_out, g_mlp, w_up, w_down, g_ple, w_ple_gate, w_ple_proj, g_final, loss_target, m_g_mix, m_w_in, m_gla_w2, m_gla_b, m_gla_norm, m_dn_conv, m_dn_a_log, m_dn_dt_bias, m_dn_norm, m_w_out, m_g_mlp, m_w_up, m_w_down, m_g_ple, m_w_ple_gate, m_w_ple_proj, m_g_final, v_g_mix, v_w_in, v_gla_w2, v_gla_b, v_gla_norm, v_dn_conv, v_dn_a_log, v_dn_dt_bias, v_dn_norm, v_w_out, v_g_mlp, v_w_up, v_w_down, v_g_ple, v_w_ple_gate, v_w_ple_proj, v_g_final):
    names = ["g_mix", "w_in", "gla_w2", "gla_b", "gla_norm", "dn_conv", "dn_a_log", "dn_dt_bias", "dn_norm", "w_out",
             "g_mlp", "w_up", "w_down", "g_ple", "w_ple_gate", "w_ple_proj", "g_final"]
    ws = dict(zip(names, (g_mix, w_in, gla_w2, gla_b, gla_norm, dn_conv, dn_a_log, dn_dt_bias, dn_norm, w_out, g_mlp,
                          w_up, w_down, g_ple, w_ple_gate, w_ple_proj, g_final)))
    ms = dict(zip(names, (m_g_mix, m_w_in, m_gla_w2, m_gla_b, m_gla_norm, m_dn_conv, m_dn_a_log, m_dn_dt_bias,
                          m_dn_norm, m_w_out, m_g_mlp, m_w_up, m_w_down, m_g_ple, m_w_ple_gate, m_w_ple_proj,
                          m_g_final)))
    vs = dict(zip(names, (v_g_mix, v_w_in, v_gla_w2, v_gla_b, v_gla_norm, v_dn_conv, v_dn_a_log, v_dn_dt_bias,
                          v_dn_norm, v_w_out, v_g_mlp, v_w_up, v_w_down, v_g_ple, v_w_ple_gate, v_w_ple_proj,
                          v_g_final)))
    me = 4 * lax.axis_index("x") + 2 * lax.axis_index("y") + lax.axis_index("c")

    first = _first_weights(g_mix, w_in, gla_w2, gla_b, gla_norm, dn_conv, dn_a_log, dn_dt_bias, dn_norm, g_mlp,
                           g_ple, w_ple_proj, g_final)
    rows_shard = jnp.concatenate([w_out[0], w_ple_gate[0], w_down[0]], axis=0).astype(BF16)
    r = _local_step(x, p[0], loss_target, first, w_up[0].astype(BF16), rows_shard)
    loss = lax.psum(r["loss"], ("x", "y", "c"))

    grads, deltas, new_m, new_v = {}, {}, {}, {}

    def big(name, parts, row0=0):
        g, d, nm, nv = _adamw_reduce(ws[name][0], ms[name][0], vs[name][0], parts, row0, "adamw_" + name)
        grads[name], deltas[name], new_m[name], new_v[name] = g[None], d[None], nm[None], nv[None]

    t_outs = _adamw_reduce(*[jnp.swapaxes(d["w_in"][0], 0, 1) for d in (ws, ms, vs)], r["recv_in"], 0, "adamw_w_in")
    grads["w_in"], deltas["w_in"], new_m["w_in"], new_v["w_in"] = [jnp.swapaxes(o, 0, 1)[None] for o in t_outs]
    big("w_up", r["recv_up"])
    big("w_ple_proj", _all_to_all(_by_cols(r["w_ple_proj"]), "scatter_d_w_ple_proj"))
    big("w_out", r["recv_rows"], 0)
    big("w_ple_gate", r["recv_rows"], ROWS_OUT)
    big("w_down", r["recv_rows"], 2 * ROWS_OUT)

    vec_names = ["g_mix", "gla_b", "gla_norm", "dn_norm", "g_mlp", "g_ple", "g_final", "dn_a_log", "dn_dt_bias"]
    gathered_names = vec_names + ["gla_w2", "dn_conv"]
    total = _small_reduce(_all_gather(_pack_rows([r[n] for n in gathered_names]), "gather_small_grads"))
    sg = dict(zip(gathered_names, _unpack_rows(total, [r[n].shape for n in gathered_names])))
    sg["g_final"] = sg["g_final"].reshape(D_MODEL)
    sg["gla_w2"] = lax.dynamic_slice_in_dim(sg["gla_w2"], me * (GLA_QK // N_DEV), GLA_QK // N_DEV, axis=1)
    sg["dn_conv"] = lax.dynamic_slice_in_dim(sg["dn_conv"], me * (DN_QKV // N_DEV), DN_QKV // N_DEV, axis=1)
    small_names = gathered_names
    shapes = [ws[n].shape for n in small_names]
    d_s, m_s, v_s = _adamw_small(_pack_rows([ws[n] for n in small_names]), _pack_rows([ms[n] for n in small_names]),
                                 _pack_rows([vs[n] for n in small_names]), _pack_rows([sg[n] for n in small_names]))
    for n, d, nm, nv in zip(small_names, _unpack_rows(d_s, shapes), _unpack_rows(m_s, shapes), _unpack_rows(v_s, shapes)):
        grads[n], deltas[n], new_m[n], new_v[n] = sg[n].reshape(ws[n].shape), d, nm, nv

    return (loss, r["grad_x"], *[grads[n] for n in names], *[deltas[n] for n in names],
            *[new_m[n] for n in names], *[new_v[n] for n in names])
```

```python
import functools
import math

import jax
import jax.numpy as jnp
from jax import lax
from jax.experimental import pallas as pl
from jax.experimental.pallas import tpu as pltpu

F32 = jnp.float32
BF16 = jnp.bfloat16

N_DEV = 8
D_MODEL = 2048
CHUNK = 64
PLE_DIM = 256
EPS = 1e-6
GLA_HEADS = 4
GLA_DK = 256
GLA_DV = 512
GLA_LOWRANK = 16
GLA_TAU = 16.0
DN_HEADS = 16
DN_D = 128
DN_CONV = 4
D_FF = 4 * D_MODEL
GLA_QK = GLA_HEADS * GLA_DK
GLA_V = GLA_HEADS * GLA_DV
DN_QKV = 3 * DN_HEADS * DN_D
D_IN = 2 * GLA_QK + 2 * GLA_V + GLA_LOWRANK + DN_QKV + D_MODEL + 2 * DN_HEADS + 2 * D_MODEL
D_IN_SHARD = D_IN // N_DEV

ADAM_LR = 0.001
ADAM_B1 = 0.9
ADAM_B2 = 0.999
ADAM_EPS = 1e-08
ADAM_WD = 0.01
ADAM_STEP = 10

LANE = 128
ZB_GQ, ZB_GK, ZB_GV, ZB_GG = 0, 1024, 2048, 4096
ZB_DQKV, ZB_DZ, ZB_GA, ZB_GB = 6144, 12288, 14336, 16384
ZB_W = 18432
ZS_LR, ZS_DA, ZS_DB = 0, 128, 256
ZS_W = 384
WI_LR = 2 * GLA_QK + 2 * GLA_V
WI_DQKV = WI_LR + GLA_LOWRANK
WI_DA = WI_DQKV + DN_QKV + D_MODEL
WI_DB = WI_DA + DN_HEADS
WI_GA = WI_DB + DN_HEADS

VMEM_LIMIT = 56 * 1024 * 1024

def _bdot(a, b, dims):
    return lax.dot_general(a.astype(BF16), b.astype(BF16), dims, preferred_element_type=F32)


def _split3(x):
    hi = x.astype(BF16)
    rest = x - hi.astype(F32)
    mid = rest.astype(BF16)
    return hi, mid, (rest - mid.astype(F32)).astype(BF16)


def _dot01(m, x, dims):
    m = m.astype(BF16)
    out = None
    for piece in _split3(x):
        d = lax.dot_general(m, piece, dims, preferred_element_type=F32)
        out = d if out is None else out + d
    return out


@functools.partial(jax.custom_vjp, nondiff_argnums=(2, 3))
def _left_dot(m, x, dims, dims_t):
    return _dot01(m, x, dims)


def _left_dot_fwd(m, x, dims, dims_t):
    return _dot01(m, x, dims), m


def _left_dot_bwd(dims, dims_t, m, ct):
    return jnp.zeros_like(m), _dot01(m, ct, dims_t)


_left_dot.defvjp(_left_dot_fwd, _left_dot_bwd)


def _dot3(a, b, dims):
    ah, bh = a.astype(BF16), b.astype(BF16)
    al, bl = (a - ah.astype(F32)).astype(BF16), (b - bh.astype(F32)).astype(BF16)
    dot = functools.partial(lax.dot_general, dimension_numbers=dims, preferred_element_type=F32)
    return dot(ah, bh) + (dot(ah, bl) + dot(al, bh))


def _sigmoid(x):
    return 1.0 / (1.0 + jnp.exp(-x))


def _silu(x):
    return x * _sigmoid(x)


def _softplus(x):
    return jnp.maximum(x, 0.0) + jnp.log(1.0 + jnp.exp(-jnp.abs(x)))


def _iota2(shape, dim):
    return lax.broadcasted_iota(jnp.int32, shape, dim)


def _cparams(sem=None):
    return pltpu.CompilerParams(dimension_semantics=sem, vmem_limit_bytes=VMEM_LIMIT)


BNN = (((2,), (1,)), ((0,), (0,)))
BNT = (((2,), (2,)), ((0,), (0,)))
BTN = (((1,), (1,)), ((0,), (0,)))


def _gla_chunk(st, q, k, v, lr, w2, b, gg, ga, gn):
    hb, c, _ = q.shape
    incl = (_iota2((c, c), 0) >= _iota2((c, c), 1))[None]
    tri = jnp.broadcast_to(incl.astype(F32), (hb, c, c))
    q = q.astype(F32) * (GLA_DK ** -0.5)
    k = k.astype(F32)
    v = v.astype(F32)
    lr_b = jnp.broadcast_to(lr[None], (hb,) + lr.shape)
    lf = -_softplus(-(_bdot(lr_b, w2, BNN) + b)) / GLA_TAU
    bcum = _left_dot(tri, lf, BNN, BTN)
    blast = jnp.sum(lf, axis=1, keepdims=True)
    q_in = q * jnp.exp(bcum)
    k_in = k * jnp.exp(-bcum)
    a = jnp.where(incl, _bdot(q_in, k_in, BNT), 0.0)
    o = _bdot(a, v, BNN) + _bdot(q_in, st, BNT)
    k_dec = k * jnp.exp(blast - bcum)
    st_new = st * jnp.exp(blast) + _bdot(v, k_dec, BTN)
    on = o * lax.rsqrt(jnp.mean(o * o, axis=-1, keepdims=True) + EPS) * gn
    res = _sigmoid(ga.astype(F32)) * on * _silu(gg.astype(F32))
    return res, st_new


def _tri_inv_raw(a):
    _, c, _ = a.shape
    eye = (_iota2((c, c), 0) == _iota2((c, c), 1)).astype(F32)[None]
    x = a
    p = eye - a
    for _ in range(5):
        x = _dot3(x, x, BNN)
        p = _dot3(p, eye + x, BNN)
    return p


def _tri_inv_bwd(t, dt):
    return (-_dot3(_dot3(t, dt, BTN), t, BNT),)


@jax.custom_vjp
def _tri_inv_given(a, t):
    return t


def _tri_inv_given_fwd(a, t):
    return t, t


def _tri_inv_given_bwd(t, dt):
    return _tri_inv_bwd(t, dt) + (jnp.zeros_like(t),)


_tri_inv_given.defvjp(_tri_inv_given_fwd, _tri_inv_given_bwd)


@functools.partial(jax.custom_vjp, nondiff_argnums=(1, 2))
def _column_on_lanes(z, j, width):
    picked = jnp.where(_iota2(z.shape, 1) == j, z, 0.0)
    return jnp.broadcast_to(jnp.sum(picked, axis=1, keepdims=True), (z.shape[0], width))


def _column_on_lanes_fwd(z, j, width):
    return _column_on_lanes(z, j, width), None


def _column_on_lanes_bwd(j, width, _, ct):
    shape = (ct.shape[0], LANE)
    total = jnp.broadcast_to(jnp.sum(ct, axis=1, keepdims=True), shape)
    return (jnp.where(_iota2(shape, 1) == j, total, 0.0),)


_column_on_lanes.defvjp(_column_on_lanes_fwd, _column_on_lanes_bwd)


def _dn_chunk(s, qr, kr, vr, za, zb, alog, dtb, gz, gb, dn, t_saved=None):
    hb, c, _ = qr.shape
    row, col = _iota2((c, c), 0), _iota2((c, c), 1)
    incl = (row >= col)[None]
    strict = (row > col)[None]
    tri = jnp.broadcast_to(incl.astype(F32), (hb, c, c))

    def l2n(t):
        return t * lax.rsqrt(jnp.sum(t * t, axis=-1, keepdims=True) + EPS)

    q = l2n(qr.astype(F32)) * (DN_D ** -0.5)
    k = l2n(kr.astype(F32))
    v = vr.astype(F32)
    g_heads = -jnp.exp(alog) * _softplus(za + dtb)
    beta_heads = _sigmoid(zb)
    def per_head(z, width):
        return jnp.concatenate([_column_on_lanes(z, j, width)[None] for j in range(hb)], axis=0)

    g = per_head(g_heads, LANE)
    beta = per_head(beta_heads, LANE)
    gcum = _left_dot(tri, g, BNN, BTN)
    glast = jnp.sum(g, axis=1, keepdims=True)
    diff = _left_dot(tri, per_head(g_heads, c) * strict.astype(F32), BNN, BTN)
    dec = jnp.exp(jnp.where(incl, diff, -1e30))
    kb = k * beta
    a = jnp.where(strict, _bdot(kb, k, BNT) * dec, 0.0)
    t = _tri_inv_raw(a) if t_saved is None else _tri_inv_given(a, t_saved)
    egc = jnp.exp(gcum)
    u = _bdot(t, v * beta, BNN)
    w = _bdot(t, kb * egc, BNN)
    attn = jnp.where(incl, _bdot(q, k, BNT) * dec, 0.0)
    q_dec = q * egc
    k_dec = k * jnp.exp(glast - gcum)
    v_new = u - _bdot(w, s, BNN)
    o = _bdot(q_dec, s, BNN) + _bdot(attn, v_new, BNN)
    s_new = s * jnp.exp(glast) + _bdot(k_dec, v_new, BTN)
    on = o * lax.rsqrt(jnp.mean(o * o, axis=-1, keepdims=True) + EPS) * dn
    res = _sigmoid(gb.astype(F32)) * on * _silu(gz.astype(F32))
    return (res, s_new, t) if t_saved is None else (res, s_new)


def _heads(ref, n_heads, width):
    return jnp.stack([ref[:, j * width:(j + 1) * width] for j in range(n_heads)], axis=0)


def _gla_specs(nc, reverse):
    def rows(b, n):
        return b * nc + ((nc - 1 - n) if reverse else n)

    qk = lambda base: pl.BlockSpec((CHUNK, GLA_QK), lambda b, n: (rows(b, n), base // GLA_QK))
    vv = lambda base: pl.BlockSpec((CHUNK, GLA_V), lambda b, n: (rows(b, n), base // GLA_V))
    lr = lambda c: pl.BlockSpec((CHUNK, LANE), lambda b, n: (rows(b, n), c))
    full = lambda shape: pl.BlockSpec(shape, lambda b, n: (0,) * len(shape))
    return rows, qk, vv, lr, full


def _gla_inputs(q_ref, k_ref, v_ref, gg_ref, ga_ref, lr_ref, w2_ref, b_ref, gn_ref):
    return (_heads(q_ref, GLA_HEADS, GLA_DK), _heads(k_ref, GLA_HEADS, GLA_DK), _heads(v_ref, GLA_HEADS, GLA_DV),
            lr_ref[...], _heads(w2_ref, GLA_HEADS, GLA_DK), _heads(b_ref, GLA_HEADS, GLA_DK),
            _heads(gg_ref, GLA_HEADS, GLA_DV), _heads(ga_ref, GLA_HEADS, GLA_DV), gn_ref[...])


def _gla_fwd(zb, zs, w2p, gla_b, gla_norm, bsz, nc):
    t = zb.shape[0]
    rows, qk, vv, lr, full = _gla_specs(nc, False)

    def body(q_ref, k_ref, v_ref, gg_ref, ga_ref, lr_ref, w2_ref, b_ref, gn_ref, o_ref, st_ref, state):
        st = jnp.where(pl.program_id(1) > 0, state[...], 0.0)
        st_ref[0, 0] = st
        res, st_new = _gla_chunk(st, *_gla_inputs(q_ref, k_ref, v_ref, gg_ref, ga_ref, lr_ref, w2_ref, b_ref, gn_ref))
        for j in range(GLA_HEADS):
            o_ref[:, j * GLA_DV:(j + 1) * GLA_DV] = res[j]
        state[...] = st_new

    return pl.pallas_call(
        body, name="gla_fwd", grid=(bsz, nc),
        in_specs=[qk(ZB_GQ), qk(ZB_GK), vv(ZB_GV), vv(ZB_GG), vv(ZB_GA), lr(ZS_LR // LANE),
                  full((LANE, GLA_QK)), full((1, GLA_QK)), full((1, GLA_DV))],
        out_specs=[vv(0), pl.BlockSpec((1, 1, GLA_HEADS, GLA_DV, GLA_DK), lambda b, n: (b, n, 0, 0, 0))],
        out_shape=[jax.ShapeDtypeStruct((t, GLA_V), F32),
                   jax.ShapeDtypeStruct((bsz, nc, GLA_HEADS, GLA_DV, GLA_DK), F32)],
        scratch_shapes=[pltpu.VMEM((GLA_HEADS, GLA_DV, GLA_DK), F32)],
        compiler_params=_cparams(("arbitrary", "arbitrary")),
    )(zb, zb, zb, zb, zb, zs, w2p, gla_b, gla_norm)


def _gla_bwd(zb, zs, w2p, gla_b, gla_norm, states, dmixed, bsz, nc, rider=None):
    t = zb.shape[0]
    rows, qk, vv, lr, full = _gla_specs(nc, True)

    def body(q_ref, k_ref, v_ref, gg_ref, ga_ref, lr_ref, w2_ref, b_ref, gn_ref, st_ref, dm_ref,
             dq_ref, dk_ref, dv_ref, dgg_ref, dga_ref, dlr_ref, dw2_ref, db_ref, dgn_ref, dstate):
        b, n = pl.program_id(0), pl.program_id(1)

        @pl.when((b == 0) & (n == 0))
        def _():
            dw2_ref[...] = jnp.zeros_like(dw2_ref)
            db_ref[...] = jnp.zeros_like(db_ref)
            dgn_ref[...] = jnp.zeros_like(dgn_ref)

        _, vjp = jax.vjp(_gla_chunk, st_ref[0, 0],
                         *_gla_inputs(q_ref, k_ref, v_ref, gg_ref, ga_ref, lr_ref, w2_ref, b_ref, gn_ref))
        dst_in = jnp.where(n > 0, dstate[...], 0.0)
        dst, dq, dk, dv, dlr, dw2, db, dgg, dga, dgn = vjp((_heads(dm_ref, GLA_HEADS, GLA_DV).astype(F32), dst_in))
        dstate[...] = dst
        for j in range(GLA_HEADS):
            dq_ref[:, j * GLA_DK:(j + 1) * GLA_DK] = dq[j].astype(dq_ref.dtype)
            dk_ref[:, j * GLA_DK:(j + 1) * GLA_DK] = dk[j].astype(dk_ref.dtype)
            dv_ref[:, j * GLA_DV:(j + 1) * GLA_DV] = dv[j].astype(dv_ref.dtype)
            dgg_ref[:, j * GLA_DV:(j + 1) * GLA_DV] = dgg[j].astype(dgg_ref.dtype)
            dga_ref[:, j * GLA_DV:(j + 1) * GLA_DV] = dga[j].astype(dga_ref.dtype)
        dlr_ref[...] = dlr
        dw2_ref[...] += dw2
        db_ref[...] += db
        dgn_ref[...] += dgn

    return _hosted_call(
        body, rider, name="gla_bwd", grid=(bsz, nc),
        in_specs=[qk(ZB_GQ), qk(ZB_GK), vv(ZB_GV), vv(ZB_GG), vv(ZB_GA), lr(ZS_LR // LANE),
                  full((LANE, GLA_QK)), full((1, GLA_QK)), full((1, GLA_DV)),
                  pl.BlockSpec((1, 1, GLA_HEADS, GLA_DV, GLA_DK), lambda b, n: (b, nc - 1 - n, 0, 0, 0)),
                  vv(0)],
        out_specs=[qk(0), qk(0), vv(0), vv(0), vv(0), lr(0),
                   full((GLA_HEADS, LANE, GLA_DK)), full((GLA_HEADS, 1, GLA_DK)), full((1, GLA_DV))],
        out_shape=[jax.ShapeDtypeStruct((t, GLA_QK), BF16), jax.ShapeDtypeStruct((t, GLA_QK), BF16),
                   jax.ShapeDtypeStruct((t, GLA_V), BF16), jax.ShapeDtypeStruct((t, GLA_V), BF16),
                   jax.ShapeDtypeStruct((t, GLA_V), BF16), jax.ShapeDtypeStruct((t, LANE), F32),
                   jax.ShapeDtypeStruct((GLA_HEADS, LANE, GLA_DK), F32),
                   jax.ShapeDtypeStruct((GLA_HEADS, 1, GLA_DK), F32),
                   jax.ShapeDtypeStruct((1, GLA_DV), F32)],
        scratch_shapes=[pltpu.VMEM((GLA_HEADS, GLA_DV, GLA_DK), F32)],
        args=(zb, zb, zb, zb, zb, zs, w2p, gla_b, gla_norm, states, dmixed))


DN_HB = DN_HEADS


def _dn_specs(nc, reverse):
    wide = DN_HB * DN_D

    def rows(b, n, h):
        return b * nc + ((nc - 1 - n) if reverse else n)

    def col(base):
        return pl.BlockSpec((CHUNK, wide), lambda b, n, h: (rows(b, n, h), base // wide + h))

    def fixed(c):
        return pl.BlockSpec((CHUNK, LANE), lambda b, n, h: (rows(b, n, h), c))

    head = pl.BlockSpec((1, LANE), lambda b, n, h: (0, 0))
    return rows, col, fixed, head


def _lanes(j):
    return slice(j * DN_D, (j + 1) * DN_D)


def _by_head(ref):
    return jnp.stack([ref[:, _lanes(j)] for j in range(DN_HB)], axis=0)


def _dn_fwd(act, zb, zs, alog_b, dtb_b, dn_norm, mix_gla, bsz, nc, rider=None):
    t = zb.shape[0]
    rows, col, fixed, head = _dn_specs(nc, False)

    def body(q_ref, k_ref, v_ref, za_ref, zb_ref, al_ref, dt_ref, gz_ref, gb_ref, dn_ref, mg_ref,
             o_ref, st_ref, ti_ref, state):
        s = jnp.where(pl.program_id(1) > 0, state[...], 0.0)
        st_ref[0, 0] = s
        res, s_new, t_inv = _dn_chunk(s, _by_head(q_ref), _by_head(k_ref), _by_head(v_ref), za_ref[...], zb_ref[...],
                                      al_ref[...], dt_ref[...], _by_head(gz_ref), _by_head(gb_ref), dn_ref[...])
        ti_ref[0, 0] = t_inv
        for j in range(DN_HB):
            o_ref[:, _lanes(j)] = (res[j] + mg_ref[:, _lanes(j)]).astype(o_ref.dtype)
        state[...] = s_new

    return _hosted_call(
        body, rider, name="dn_fwd", grid=(bsz, nc, DN_HEADS // DN_HB),
        in_specs=[col(0), col(DN_HEADS * DN_D), col(2 * DN_HEADS * DN_D),
                  fixed(ZS_DA // LANE), fixed(ZS_DB // LANE), head, head,
                  col(ZB_DZ), col(ZB_GB), pl.BlockSpec((1, DN_D), lambda b, n, h: (0, 0)), col(0)],
        out_specs=[col(0), pl.BlockSpec((1, 1, DN_HB, DN_D, DN_D), lambda b, n, h: (b, n, h, 0, 0)),
                   pl.BlockSpec((1, 1, DN_HB, CHUNK, CHUNK), lambda b, n, h: (b, n, h, 0, 0))],
        out_shape=[jax.ShapeDtypeStruct((t, D_MODEL), BF16),
                   jax.ShapeDtypeStruct((bsz, nc, DN_HEADS, DN_D, DN_D), F32),
                   jax.ShapeDtypeStruct((bsz, nc, DN_HEADS, CHUNK, CHUNK), F32)],
        scratch_shapes=[pltpu.VMEM((DN_HEADS, DN_D, DN_D), F32)],
        args=(act, act, act, zs, zs, alog_b, dtb_b, zb, zb, dn_norm, mix_gla))


def _dn_bwd(act, zb, zs, alog_b, dtb_b, dn_norm, states, t_invs, dmixed, bsz, nc, rider=None):
    t = zb.shape[0]
    rows, col, fixed, head = _dn_specs(nc, True)

    def body(q_ref, k_ref, v_ref, za_ref, zb_ref, al_ref, dt_ref, gz_ref, gb_ref, dn_ref, st_ref, ti_ref, dm_ref,
             dact_ref, dza_ref, dzb_ref, dgz_ref, dgb_ref, dal_ref, ddt_ref, ddn_ref, dstate):
        b, n = pl.program_id(0), pl.program_id(1)

        @pl.when((b == 0) & (n == 0))
        def _():
            dal_ref[...] = jnp.zeros_like(dal_ref)
            ddt_ref[...] = jnp.zeros_like(ddt_ref)
            ddn_ref[...] = jnp.zeros_like(ddn_ref)

        fn = functools.partial(_dn_chunk, t_saved=ti_ref[0, 0])
        _, vjp = jax.vjp(fn, st_ref[0, 0], _by_head(q_ref), _by_head(k_ref), _by_head(v_ref), za_ref[...],
                         zb_ref[...], al_ref[...], dt_ref[...], _by_head(gz_ref), _by_head(gb_ref), dn_ref[...])
        ds_in = jnp.where(n > 0, dstate[...], 0.0)
        ds, dq, dk, dv, dza, dzb, dal, ddt, dgz, dgb, ddn = vjp((_by_head(dm_ref).astype(F32), ds_in))
        dstate[...] = ds
        for j in range(DN_HB):
            for part, d in enumerate((dq, dk, dv)):
                dact_ref[:, pl.ds(part * DN_HEADS * DN_D + j * DN_D, DN_D)] = d[j]
            dgz_ref[:, _lanes(j)] = dgz[j].astype(dgz_ref.dtype)
            dgb_ref[:, _lanes(j)] = dgb[j].astype(dgb_ref.dtype)
        dal_ref[...] += dal
        ddt_ref[...] += ddt
        dza_ref[...] = dza
        dzb_ref[...] = dzb
        ddn_ref[...] += ddn

    full = lambda shape: pl.BlockSpec(shape, lambda b, n, h: (0,) * len(shape))
    return _hosted_call(
        body, rider, name="dn_bwd", grid=(bsz, nc, DN_HEADS // DN_HB),
        in_specs=[col(0), col(DN_HEADS * DN_D), col(2 * DN_HEADS * DN_D),
                  fixed(ZS_DA // LANE), fixed(ZS_DB // LANE), head, head,
                  col(ZB_DZ), col(ZB_GB), pl.BlockSpec((1, DN_D), lambda b, n, h: (0, 0)),
                  pl.BlockSpec((1, 1, DN_HB, DN_D, DN_D), lambda b, n, h: (b, nc - 1 - n, h, 0, 0)),
                  pl.BlockSpec((1, 1, DN_HB, CHUNK, CHUNK), lambda b, n, h: (b, nc - 1 - n, h, 0, 0)), col(0)],
        out_specs=[pl.BlockSpec((CHUNK, DN_QKV), lambda b, n, h: (rows(b, n, h), 0)), fixed(0), fixed(0), col(0), col(0),
                   full((1, LANE)), full((1, LANE)), full((1, DN_D))],
        out_shape=[jax.ShapeDtypeStruct((t, DN_QKV), F32),
                   jax.ShapeDtypeStruct((t, LANE), F32), jax.ShapeDtypeStruct((t, LANE), F32),
                   jax.ShapeDtypeStruct((t, D_MODEL), BF16), jax.ShapeDtypeStruct((t, D_MODEL), BF16),
                   jax.ShapeDtypeStruct((1, LANE), F32), jax.ShapeDtypeStruct((1, LANE), F32),
                   jax.ShapeDtypeStruct((1, DN_D), F32)],
        scratch_shapes=[pltpu.VMEM((DN_HEADS, DN_D, DN_D), F32)],
        args=(act, act, act, zs, zs, alog_b, dtb_b, zb, zb, dn_norm, states, t_invs, dmixed))


MM_VMEM_BUDGET = 40 * 1024 * 1024
MM_TILE_PREF = (1024, 1024, 2048)


def _divisor_tile(n, cap):
    if n <= cap:
        return n
    for c in range(cap - cap % LANE, 0, -LANE):
        if n % c == 0:
            return c
    return n


def _mm_tiles(m, n, kd, a_bytes, b_bytes, mn_bytes):
    tm, tn, tk = (_divisor_tile(d, c) for d, c in zip((m, n, kd), MM_TILE_PREF))

    def need(tm, tn, tk):
        acc = 0 if tk == kd else 4 * tm * tn
        return 2 * (tm * tk * a_bytes + tk * tn * b_bytes + tm * tn * mn_bytes) + acc + 4 * tm * tn

    while need(tm, tn, tk) > MM_VMEM_BUDGET:
        if tk > 512 and tk * max(tm * a_bytes, tn * b_bytes) >= tm * tn * mn_bytes:
            tk = _divisor_tile(kd, tk // 2)
        elif tn >= tm and tn > LANE:
            tn = _divisor_tile(n, tn // 2)
        else:
            tm = _divisor_tile(m, tm // 2)
    return tm, tn, tk


def _w_in_row_of(tile, tile_rows):
    skipped = jnp.where(tile >= ZB_GA // tile_rows, WI_GA - ZB_GA, jnp.where(tile >= ZB_DQKV // tile_rows,
                                                                             WI_DQKV - ZB_DQKV, 0))
    return pl.multiple_of(tile * tile_rows + skipped, 16)


def _mm(a, b, *, ta=False, tb=False, out_dtypes=(F32,), epilogue=None, extras=(), name, rider=None,
        b_is_w_in_t=False, out_is_w_in_t=False):
    m, kd = (a.shape[1], a.shape[0]) if ta else a.shape
    n = b.shape[0] if tb else b.shape[1]
    if b_is_w_in_t:
        n, kd = (ZB_W, kd) if tb else (n, ZB_W)
    mn_bytes = sum(e.dtype.itemsize for e in extras) + sum(jnp.dtype(dt).itemsize for dt in out_dtypes)
    tm, tn, tk = _mm_tiles(m, n, kd, a.dtype.itemsize, b.dtype.itemsize, mn_bytes)
    nk = kd // tk
    n_ex = len(extras)
    dims = (((0,) if ta else (1,), (1,) if tb else (0,)), ((), ()))

    def finish(acc, ex_refs, out_refs):
        outs = (acc,) if epilogue is None else epilogue(acc, *[r[...] for r in ex_refs])
        for r, o in zip(out_refs, outs):
            r[...] = o.astype(r.dtype)

    def partial_product(a_ref, b_ref):
        return lax.dot_general(a_ref[...].astype(BF16), b_ref[...].astype(BF16), dims, preferred_element_type=F32)

    def body_single(*refs):
        finish(partial_product(refs[0], refs[1]), refs[2:2 + n_ex], refs[2 + n_ex:])

    def body_acc(*refs):
        acc = refs[-1]
        k = pl.program_id(2)

        @pl.when(k == 0)
        def _():
            acc[...] = partial_product(refs[0], refs[1])

        @pl.when(k > 0)
        def _():
            acc[...] += partial_product(refs[0], refs[1])

        @pl.when(k == nk - 1)
        def _():
            finish(acc[...], refs[2:2 + n_ex], refs[2 + n_ex:-1])

    a_spec = pl.BlockSpec((tk, tm), lambda i, j, k: (k, i)) if ta else pl.BlockSpec((tm, tk), lambda i, j, k: (i, k))
    b_spec = pl.BlockSpec((tn, tk), lambda i, j, k: (j, k)) if tb else pl.BlockSpec((tk, tn), lambda i, j, k: (k, j))
    mn_spec = pl.BlockSpec((tm, tn), lambda i, j, k: (i, j))
    out_spec, out_rows = mn_spec, m
    if b_is_w_in_t and tb:
        b_spec = pl.BlockSpec((pl.Element(tn), pl.Element(tk)),
                              lambda i, j, k: (_w_in_row_of(j, tn), pl.multiple_of(k * tk, LANE)))
    elif b_is_w_in_t:
        b_spec = pl.BlockSpec((pl.Element(tk), pl.Element(tn)),
                              lambda i, j, k: (_w_in_row_of(k, tk), pl.multiple_of(j * tn, LANE)))
    if out_is_w_in_t:
        out_spec, out_rows = pl.BlockSpec((pl.Element(tm), pl.Element(tn)),
                                          lambda i, j, k: (_w_in_row_of(i, tm), pl.multiple_of(j * tn, LANE))), D_IN
    outs = _hosted_call(
        body_single if nk == 1 else body_acc, rider, name=name, grid=(m // tm, n // tn, nk),
        in_specs=[a_spec, b_spec] + [mn_spec] * n_ex,
        out_specs=[out_spec] * len(out_dtypes),
        out_shape=[jax.ShapeDtypeStruct((out_rows, n), dt) for dt in out_dtypes],
        scratch_shapes=[] if nk == 1 else [pltpu.VMEM((tm, tn), F32)],
        args=(a, b, *extras))
    return outs[0] if len(outs) == 1 else outs


ROW_BLOCK = 256


def _row_spec(width=D_MODEL):
    return pl.BlockSpec((ROW_BLOCK, width), lambda i: (i, 0))


def _vec_spec(width=D_MODEL):
    return pl.BlockSpec((1, width), lambda i: (0, 0))


def _rms_fwd(x, g, name):
    def body(x_ref, g_ref, h_ref):
        xf = x_ref[...]
        h_ref[...] = (xf * lax.rsqrt(jnp.mean(xf * xf, axis=-1, keepdims=True) + EPS) * g_ref[...]).astype(BF16)

    t = x.shape[0]
    return pl.pallas_call(
        body, name=name, grid=(t // ROW_BLOCK,), in_specs=[_row_spec(), _vec_spec()], out_specs=_row_spec(),
        out_shape=jax.ShapeDtypeStruct((t, D_MODEL), BF16), compiler_params=_cparams(("parallel",)),
    )(x, g)


def _rms_bwd_math(xf, g, dh):
    rstd = lax.rsqrt(jnp.mean(xf * xf, axis=-1, keepdims=True) + EPS)
    xhat = xf * rstd
    dxhat = dh * g
    dx = rstd * (dxhat - xhat * jnp.mean(dxhat * xhat, axis=-1, keepdims=True))
    dg = jnp.sum(dh * xhat, axis=0, keepdims=True)
    return dx, dg


def _rms_bwd(x, g, dh, dres, name):
    def body(x_ref, g_ref, dh_ref, dres_ref, dx_ref, dg_ref):
        dx, dg = _rms_bwd_math(x_ref[...], g_ref[...], dh_ref[...].astype(F32))
        dx_ref[...] = dres_ref[...] + dx

        @pl.when(pl.program_id(0) == 0)
        def _():
            dg_ref[...] = jnp.zeros_like(dg_ref)

        dg_ref[...] += dg

    t = x.shape[0]
    return pl.pallas_call(
        body, name=name, grid=(t // ROW_BLOCK,),
        in_specs=[_row_spec(), _vec_spec(), _row_spec(), _row_spec()], out_specs=[_row_spec(), _vec_spec()],
        out_shape=[jax.ShapeDtypeStruct((t, D_MODEL), F32), jax.ShapeDtypeStruct((1, D_MODEL), F32)],
        compiler_params=_cparams(("arbitrary",)),
    )(x, g, dh, dres)


def _loss_head(x3, g, target):
    def body(x_ref, g_ref, t_ref, dx_ref, dg_ref, loss_ref):
        xf, gg = x_ref[...], g_ref[...]
        rstd = lax.rsqrt(jnp.mean(xf * xf, axis=-1, keepdims=True) + EPS)
        err = xf * rstd * gg - t_ref[...]
        dx, dg = _rms_bwd_math(xf, gg, err * (1.0 / D_MODEL))
        dx_ref[...] = dx

        @pl.when(pl.program_id(0) == 0)
        def _():
            dg_ref[...] = jnp.zeros_like(dg_ref)
            loss_ref[...] = jnp.zeros_like(loss_ref)

        dg_ref[...] += dg
        part = jnp.sum(jnp.sum(err * err, axis=-1, keepdims=True), axis=0, keepdims=True) * (0.5 / D_MODEL)
        loss_ref[...] += jnp.broadcast_to(part, loss_ref.shape)

    t = x3.shape[0]
    return pl.pallas_call(
        body, name="loss_head", grid=(t // ROW_BLOCK,),
        in_specs=[_row_spec(), _vec_spec(), _row_spec()], out_specs=[_row_spec(), _vec_spec(), _vec_spec(LANE)],
        out_shape=[jax.ShapeDtypeStruct((t, D_MODEL), F32), jax.ShapeDtypeStruct((1, D_MODEL), F32),
                   jax.ShapeDtypeStruct((1, LANE), F32)],
        compiler_params=_cparams(("arbitrary",)),
    )(x3, g, target)


def _ple_bwd(dx3, gpre, pp):
    def body(dx_ref, gp_ref, pp_ref, dgp_ref, dpp_ref):
        dx, sg = dx_ref[...], _sigmoid(gp_ref[...])
        dpp_ref[...] = (dx * sg).astype(BF16)
        dgp_ref[...] = (dx * pp_ref[...] * sg * (1.0 - sg)).astype(BF16)

    t = dx3.shape[0]
    return pl.pallas_call(
        body, name="ple_bwd", grid=(t // ROW_BLOCK,), in_specs=[_row_spec()] * 3, out_specs=[_row_spec()] * 2,
        out_shape=[jax.ShapeDtypeStruct((t, D_MODEL), BF16)] * 2, compiler_params=_cparams(("parallel",)),
    )(dx3, gpre, pp)


CONV_COLS = 256


def _shift_down(x, s):
    if s == 0:
        return x
    return jnp.where(_iota2(x.shape, 0) >= s, pltpu.roll(x, s, 0), 0.0)


def _shift_up(x, s):
    if s == 0:
        return x
    rows = x.shape[0]
    return jnp.where(_iota2(x.shape, 0) < rows - s, pltpu.roll(x, rows - s, 0), 0.0)


def _conv_taps(xf):
    return [_shift_down(xf, DN_CONV - 1 - j) for j in range(DN_CONV)]


def _conv_pre(taps, w):
    return sum(tap * w[j:j + 1, :] for j, tap in enumerate(taps))


def _conv_fwd(zb, conv_w, bsz, seq):
    def body(x_ref, w_ref, y_ref):
        y_ref[...] = _silu(_conv_pre(_conv_taps(x_ref[...]), w_ref[...]))

    nblk = DN_QKV // CONV_COLS
    return pl.pallas_call(
        body, name="conv_fwd", grid=(bsz, nblk),
        in_specs=[pl.BlockSpec((seq, CONV_COLS), lambda b, j: (b, ZB_DQKV // CONV_COLS + j)),
                  pl.BlockSpec((DN_CONV, CONV_COLS), lambda b, j: (0, j))],
        out_specs=pl.BlockSpec((seq, CONV_COLS), lambda b, j: (b, j)),
        out_shape=jax.ShapeDtypeStruct((bsz * seq, DN_QKV), F32),
        compiler_params=_cparams(("parallel", "parallel")),
    )(zb, conv_w)


def _conv_bwd(zb, conv_w, dact, bsz, seq):
    def body(x_ref, w_ref, dy_ref, dx_ref, dw_ref):
        taps, w = _conv_taps(x_ref[...]), w_ref[...]
        c = _conv_pre(taps, w)
        sg = _sigmoid(c)
        dc = dy_ref[...].astype(F32) * sg * (1.0 + c * (1.0 - sg))
        dx = sum(_shift_up(dc, DN_CONV - 1 - j) * w[j:j + 1, :] for j in range(DN_CONV))
        dx_ref[...] = dx.astype(BF16)
        dw = jnp.concatenate([jnp.sum(dc * tap, axis=0, keepdims=True) for tap in taps], axis=0)

        @pl.when(pl.program_id(1) == 0)
        def _():
            dw_ref[...] = jnp.zeros_like(dw_ref)

        dw_ref[...] += dw

    nblk = DN_QKV // CONV_COLS
    return pl.pallas_call(
        body, name="conv_bwd", grid=(nblk, bsz),
        in_specs=[pl.BlockSpec((seq, CONV_COLS), lambda j, b: (b, ZB_DQKV // CONV_COLS + j)),
                  pl.BlockSpec((DN_CONV, CONV_COLS), lambda j, b: (0, j)),
                  pl.BlockSpec((seq, CONV_COLS), lambda j, b: (b, j))],
        out_specs=[pl.BlockSpec((seq, CONV_COLS), lambda j, b: (b, j)),
                   pl.BlockSpec((DN_CONV, CONV_COLS), lambda j, b: (0, j))],
        out_shape=[jax.ShapeDtypeStruct((bsz * seq, DN_QKV), BF16), jax.ShapeDtypeStruct((DN_CONV, DN_QKV), F32)],
        compiler_params=_cparams(("parallel", "arbitrary")),
    )(zb, conv_w, dact)


MESH_IDS = pl.DeviceIdType.MESH
ANY_SPEC = pl.BlockSpec(memory_space=pl.ANY)


COMM_SCRATCH = (pltpu.SemaphoreType.DMA((7,)), pltpu.SemaphoreType.DMA((7,)), pltpu.SemaphoreType.DMA)


def _gather_phases(x_ref, out_ref, send_sems, recv_sems, local_sem):
    mx, my, mc = lax.axis_index("x"), lax.axis_index("y"), lax.axis_index("c")
    me, sibling = (mx, my, mc), (mx, my, 1 - mc)
    chips = [(1 - mx, my), (mx, 1 - my), (1 - mx, 1 - my)]

    def slot(px, py, pc):
        return out_ref.at[4 * px + 2 * py + pc]

    def copy(k, block, to, src=None):
        return pltpu.make_async_remote_copy(
            src_ref=slot(*block) if src is None else src, dst_ref=slot(*block),
            send_sem=send_sems.at[k], recv_sem=recv_sems.at[k], device_id=to, device_id_type=MESH_IDS)

    def mine():
        return pltpu.make_async_copy(x_ref, slot(*me), local_sem)

    def first():
        return [copy(0, me, sibling, src=x_ref)] + [copy(1 + j, me, (*chip, mc), src=x_ref)
                                                    for j, chip in enumerate(chips)]

    def passed():
        return [copy(4 + j, (*chip, mc), sibling) for j, chip in enumerate(chips)]

    def start():
        mine().start()
        for cp in first():
            cp.start()

    def forward():
        for j, (chip, cp) in enumerate(zip(chips, passed())):
            copy(1 + j, (*chip, mc), me).wait_recv()
            cp.start()

    def finish():
        copy(0, sibling, me).wait_recv()
        for j, chip in enumerate(chips):
            copy(4 + j, (*chip, 1 - mc), me).wait_recv()
        for cp in first() + passed():
            cp.wait_send()
        mine().wait()

    return start, forward, finish


def _scatter_phases(x_ref, out_ref, send_sems, recv_sems, local_sem, among_chips=False):
    mx, my, mc = lax.axis_index("x"), lax.axis_index("y"), lax.axis_index("c")
    n_peers = 4 if among_chips else N_DEV
    me = 2 * mx + my if among_chips else 4 * mx + 2 * my + mc

    def peer(k):
        if among_chips:
            return (mx ^ ((k >> 1) & 1), my ^ (k & 1), mc)
        return (mx ^ ((k >> 2) & 1), my ^ ((k >> 1) & 1), mc ^ (k & 1))

    def slot_of(k):
        px, py, pc = peer(k)
        return 2 * px + py if among_chips else 4 * px + 2 * py + pc

    def copy(k, src_slot, dst_slot):
        return pltpu.make_async_remote_copy(
            src_ref=x_ref.at[src_slot], dst_ref=out_ref.at[dst_slot],
            send_sem=send_sems.at[k - 1], recv_sem=recv_sems.at[k - 1],
            device_id=peer(k), device_id_type=MESH_IDS)

    def sends():
        return [copy(k, slot_of(k), me) for k in range(1, n_peers)]

    def mine():
        return pltpu.make_async_copy(x_ref.at[me], out_ref.at[me], local_sem)

    def start():
        mine().start()
        for cp in sends():
            cp.start()

    def forward():
        pass

    def finish():
        for k in range(1, n_peers):
            copy(k, me, slot_of(k)).wait_recv()
        for cp in sends():
            cp.wait_send()
        mine().wait()

    return start, forward, finish


def _pair_phases(x_ref, out_ref, send_sems, recv_sems, local_sem):
    mx, my, mc = lax.axis_index("x"), lax.axis_index("y"), lax.axis_index("c")

    def copy(side):
        return pltpu.make_async_remote_copy(
            src_ref=x_ref.at[:, side], dst_ref=out_ref, send_sem=send_sems.at[0], recv_sem=recv_sems.at[0],
            device_id=(mx, my, 1 - mc), device_id_type=MESH_IDS)

    def start():
        copy(1 - mc).start()

    def forward():
        pass

    def finish():
        copy(mc).wait_recv()
        copy(1 - mc).wait_send()

    return start, forward, finish


class _Rider:
    def __init__(self, phases, x, out_shape):
        self.phases, self.x, self.out_shape = phases, x, out_shape


def _gather_rider(x):
    return _Rider(_gather_phases, x, jax.ShapeDtypeStruct((N_DEV,) + x.shape, x.dtype))


def _scatter_rider(x):
    return _Rider(_scatter_phases, x, jax.ShapeDtypeStruct(x.shape, x.dtype))


def _chip_scatter_rider(x):
    return _Rider(functools.partial(_scatter_phases, among_chips=True), x, jax.ShapeDtypeStruct(x.shape, x.dtype))


def _pair_rider(x):
    return _Rider(_pair_phases, x, jax.ShapeDtypeStruct((x.shape[0],) + x.shape[2:], x.dtype))


def _exchange(rider, name):
    def body(x_ref, out_ref, send_sems, recv_sems, local_sem):
        for phase in rider.phases(x_ref, out_ref, send_sems, recv_sems, local_sem):
            phase()

    return pl.pallas_call(body, name=name, out_shape=rider.out_shape, in_specs=[ANY_SPEC], out_specs=ANY_SPEC,
                          scratch_shapes=list(COMM_SCRATCH))(rider.x)


def _all_gather(x, name):
    return _exchange(_gather_rider(x), name)


def _all_to_all(x, name):
    return _exchange(_scatter_rider(x), name)


def _hosted_call(body, rider, *, name, grid, in_specs, out_specs, out_shape, scratch_shapes, args):
    if rider is None:
        return pl.pallas_call(body, name=name, grid=grid, in_specs=in_specs, out_specs=out_specs, out_shape=out_shape,
                              scratch_shapes=scratch_shapes, compiler_params=_cparams(("arbitrary",) * len(grid)))(*args)
    n_in, n_out, n_scr = len(in_specs), len(out_specs), len(scratch_shapes)
    total = math.prod(grid)

    def riding(*refs):
        host_in, x_ref = refs[:n_in], refs[n_in]
        host_out, out_ref = refs[n_in + 1:n_in + 1 + n_out], refs[n_in + 1 + n_out]
        host_scr = refs[n_in + 2 + n_out:n_in + 2 + n_out + n_scr]
        start, forward, finish = rider.phases(x_ref, out_ref, *refs[n_in + 2 + n_out + n_scr:])
        step = 0
        for axis, size in enumerate(grid):
            step = step * size + pl.program_id(axis)
        pl.when(step == 0)(start)
        pl.when(step == (3 * total) // 4)(forward)
        body(*host_in, *host_out, *host_scr)
        pl.when(step == total - 1)(finish)

    return pl.pallas_call(
        riding, name=name, grid=grid, in_specs=list(in_specs) + [ANY_SPEC], out_specs=list(out_specs) + [ANY_SPEC],
        out_shape=list(out_shape) + [rider.out_shape], scratch_shapes=list(scratch_shapes) + list(COMM_SCRATCH),
        compiler_params=_cparams(("arbitrary",) * len(grid)))(*args, rider.x)


def _adamw_math(w, g, m, v):
    m = ADAM_B1 * m + (1.0 - ADAM_B1) * g
    v = ADAM_B2 * v + (1.0 - ADAM_B2) * jnp.square(g)
    m_hat = m / (1.0 - ADAM_B1 ** ADAM_STEP)
    v_hat = v / (1.0 - ADAM_B2 ** ADAM_STEP)
    delta = -ADAM_LR * (m_hat / (jnp.sqrt(v_hat) + ADAM_EPS) + ADAM_WD * w)
    return delta, m, v


ADAM_ROWS = 128


def _elementwise_tile(rows, cols):
    if rows % ADAM_ROWS == 0:
        return ADAM_ROWS, cols
    return rows, (2 * LANE if cols % (2 * LANE) == 0 else cols)


def _add_blocks(a, b):
    g, rows, cols = a.shape
    tr, tc = _elementwise_tile(rows, cols)

    def body(a_ref, b_ref, o_ref):
        o_ref[...] = (a_ref[...].astype(F32) + b_ref[...].astype(F32)).astype(o_ref.dtype)

    blk = pl.BlockSpec((None, tr, tc), lambda k, i, j: (k, i, j))
    return pl.pallas_call(body, name="add_pair_blocks", grid=(g, rows // tr, cols // tc), in_specs=[blk, blk],
                          out_specs=blk, out_shape=jax.ShapeDtypeStruct(a.shape, a.dtype),
                          compiler_params=_cparams(("parallel", "parallel", "parallel")))(a, b)


def _adamw_reduce(w, m, v, parts, row0, name):
    rows, cols = w.shape
    n_parts = parts.shape[0]
    tr, tc = _elementwise_tile(rows, cols)
    r0 = row0 // tr

    def body(w_ref, m_ref, v_ref, *refs):
        part_refs, (g_ref, d_ref, nm_ref, nv_ref) = refs[:n_parts], refs[n_parts:]
        g = part_refs[0][...].astype(F32)
        for r in part_refs[1:]:
            g = g + r[...].astype(F32)
        delta, nm, nv = _adamw_math(w_ref[...], g, m_ref[...], v_ref[...])
        g_ref[...] = g
        d_ref[...] = delta
        nm_ref[...] = nm
        nv_ref[...] = nv

    blk = pl.BlockSpec((tr, tc), lambda i, j: (i, j))
    part_specs = [pl.BlockSpec((None, tr, tc), functools.partial(lambda i, j, k: (k, r0 + i, j), k=k))
                  for k in range(n_parts)]
    return pl.pallas_call(
        body, name=name, grid=(rows // tr, cols // tc), in_specs=[blk] * 3 + part_specs, out_specs=[blk] * 4,
        out_shape=[jax.ShapeDtypeStruct(w.shape, F32)] * 4, compiler_params=_cparams(("parallel", "parallel")),
    )(w, m, v, *([parts] * n_parts))


def _small_reduce(gathered):
    r = gathered.shape[1]

    def body(g_ref, o_ref):
        g = g_ref[0]
        for k in range(1, N_DEV):
            g = g + g_ref[k]
        o_ref[...] = g

    return pl.pallas_call(body, name="small_grad_reduce", out_shape=jax.ShapeDtypeStruct((r, LANE), F32))(gathered)


def _adamw_small(w, m, v, g):
    def body(w_ref, m_ref, v_ref, g_ref, d_ref, nm_ref, nv_ref):
        d_ref[...], nm_ref[...], nv_ref[...] = _adamw_math(w_ref[...], g_ref[...], m_ref[...], v_ref[...])

    return pl.pallas_call(body, name="adamw_small", out_shape=[jax.ShapeDtypeStruct(w.shape, F32)] * 3)(w, m, v, g)


def _pack_rows(arrays):
    rows = [jnp.pad(a.reshape(-1), (0, -a.size % LANE)).reshape(-1, LANE) for a in arrays]
    out = jnp.concatenate(rows, axis=0)
    return jnp.pad(out, ((0, -out.shape[0] % 8), (0, 0)))


def _unpack_rows(packed, shapes):
    out, r = [], 0
    for shp in shapes:
        size = math.prod(shp)
        nrows = -(-size // LANE)
        out.append(packed[r:r + nrows].reshape(-1)[:size].reshape(shp))
        r += nrows
    return out


def _add_residual(acc, res):
    return (res + acc,)


def _by_cols(g):
    return g.reshape(g.shape[0], N_DEV, -1).transpose(1, 0, 2)


def _from_cols(blocks):
    return blocks.transpose(1, 0, 2).reshape(blocks.shape[1], -1)


ROWS_OUT = D_MODEL // N_DEV

W_IN_SEGMENTS = ((0, WI_LR, "big", 0), (WI_LR, WI_DQKV, "gates", ZS_LR), (WI_DQKV, WI_DA, "big", ZB_DQKV),
                 (WI_DA, WI_DB, "gates", ZS_DA), (WI_DB, WI_GA, "gates", ZS_DB), (WI_GA, D_IN, "big", ZB_GA))


def _w_in_gate_rows(wt):
    parts = []
    for lo, hi, which, _ in W_IN_SEGMENTS:
        if which == "gates":
            parts += [wt[lo:hi], jnp.zeros((LANE - (hi - lo), wt.shape[1]), wt.dtype)]
    return jnp.concatenate(parts, axis=0)


def _fill_gate_rows(dw_in_t, dws_t):
    for lo, hi, which, first in W_IN_SEGMENTS:
        if which == "gates":
            dw_in_t = lax.dynamic_update_slice(dw_in_t, dws_t[first:first + hi - lo], (lo, 0))
    return dw_in_t


def _local_step(x, p, target, w, up_shard, rows_shard):
    bsz, seq, _ = x.shape
    t, nc = bsz * seq, seq // CHUNK
    x0, p2, tgt = x.reshape(t, D_MODEL), p.reshape(t, PLE_DIM), target.reshape(t, D_MODEL)

    h = _rms_fwd(x0, w["g_mix"], "rms_mix")
    zb, row_blocks = _mm(h, w["w_in_t"], tb=True, b_is_w_in_t=True, name="in_proj", rider=_gather_rider(rows_shard))
    zs = _mm(h, w["ws_t"], tb=True, name="in_proj_gates")
    act = _conv_fwd(zb, w["conv"], bsz, seq)
    mix_gla, gla_states = _gla_fwd(zb, zs, w["w2p"], w["gla_b"], w["gla_norm"], bsz, nc)
    mixed, dn_states, dn_t_invs, up_blocks = _dn_fwd(act, zb, zs, w["alog_b"], w["dtb_b"], w["dn_norm"], mix_gla,
                                                     bsz, nc, rider=_gather_rider(up_shard))
    w_up = _from_cols(up_blocks)
    w_out = row_blocks[:, :ROWS_OUT].reshape(D_MODEL, D_MODEL)
    w_pg = row_blocks[:, ROWS_OUT:2 * ROWS_OUT].reshape(D_MODEL, D_MODEL)
    w_down = row_blocks[:, 2 * ROWS_OUT:].reshape(D_FF, D_MODEL)
    x1 = _mm(mixed, w_out, epilogue=_add_residual, extras=(x0,), name="out_proj")
    h2 = _rms_fwd(x1, w["g_mlp"], "rms_mlp")
    u, a = _mm(h2, w_up, out_dtypes=(BF16, BF16), name="mlp_up",
               epilogue=lambda acc: (acc, jnp.square(jnp.maximum(acc, 0.0))))
    x2 = _mm(a, w_down, epilogue=_add_residual, extras=(x1,), name="mlp_down")
    h3 = _rms_fwd(x2, w["g_ple"], "rms_ple")
    pp = _mm(p2, w["w_pp"], name="ple_proj")
    gpre, x3 = _mm(h3, w_pg, out_dtypes=(F32, F32), extras=(x2, pp), name="ple_gate",
                   epilogue=lambda acc, res, proj: (acc, res + _sigmoid(acc) * proj))
    dx3, dg_final, loss = _loss_head(x3, w["g_final"], tgt)

    dgpre, dpp = _ple_bwd(dx3, gpre, pp)
    dw_pp = _mm(p2, dpp, ta=True, out_dtypes=(BF16,), name="d_w_ple_proj")
    dw_pg = _mm(h3, dgpre, ta=True, out_dtypes=(BF16,), name="d_w_ple_gate")
    dh3 = _mm(dgpre, w_pg, tb=True, name="d_h_ple")
    dx2, dg_ple = _rms_bwd(x2, w["g_ple"], dh3, dx3, "rms_ple_bwd")
    du = _mm(dx2, w_down, tb=True, out_dtypes=(BF16,), extras=(u,), name="d_mlp_hidden",
             epilogue=lambda acc, uu: (acc * (2.0 * jnp.maximum(uu.astype(F32), 0.0)),))
    dw_down = _mm(a, dx2, ta=True, out_dtypes=(BF16,), name="d_w_down")
    dw_up = _mm(h2, du, ta=True, out_dtypes=(BF16,), name="d_w_up")
    dh2 = _mm(du, w_up, tb=True, name="d_h_mlp")
    dx1, dg_mlp = _rms_bwd(x1, w["g_mlp"], dh2, dx2, "rms_mlp_bwd")
    dmixed = _mm(dx1, w_out, tb=True, out_dtypes=(BF16,), name="d_mixed")
    dw_out = _mm(mixed, dx1, ta=True, out_dtypes=(BF16,), name="d_w_out")

    d_rows = jnp.concatenate([dw_out.reshape(N_DEV, ROWS_OUT, D_MODEL), dw_pg.reshape(N_DEV, ROWS_OUT, D_MODEL),
                              dw_down.reshape(N_DEV, D_FF // N_DEV, D_MODEL)], axis=1)
    (dact, dza, dzb_, dgz, dgb, dal, ddt, ddn, recv_rows) = _dn_bwd(
        act, zb, zs, w["alog_b"], w["dtb_b"], w["dn_norm"], dn_states, dn_t_invs, dmixed, bsz, nc,
        rider=_scatter_rider(d_rows))
    (gdq, gdk, gdv, dgg, dga, dlr, dw2, dgla_b, dgla_norm, recv_up) = _gla_bwd(
        zb, zs, w["w2p"], w["gla_b"], w["gla_norm"], gla_states, dmixed, bsz, nc,
        rider=_scatter_rider(_by_cols(dw_up)))
    dqkv, dconv = _conv_bwd(zb, w["conv"], dact, bsz, seq)
    dzb = jnp.concatenate([gdq, gdk, gdv, dgg, dqkv, dgz, dga, dgb], axis=1)
    dzs = jnp.concatenate([dlr, dza, dzb_], axis=1)
    dw_in_t = _mm(dzb, h, ta=True, out_dtypes=(BF16,), out_is_w_in_t=True, name="d_w_in")
    dws_t = _mm(dzs, h, ta=True, out_dtypes=(BF16,), name="d_w_in_gates")
    by_chip = _fill_gate_rows(dw_in_t, dws_t).reshape(N_DEV // 2, 2, D_IN_SHARD, D_MODEL)
    from_sibling = _exchange(_pair_rider(by_chip), "pair_d_w_in")
    mine = lax.dynamic_index_in_dim(by_chip, lax.axis_index("c"), axis=1, keepdims=False)
    chip_sums = _add_blocks(mine, from_sibling)
    dh_gates = _mm(dzs, w["ws_t"], name="d_h_mix_gates")
    dh, recv_in = _mm(dzb, w["w_in_t"], b_is_w_in_t=True, epilogue=_add_residual, extras=(dh_gates,), name="d_h_mix",
                      rider=_chip_scatter_rider(chip_sums))
    gx, dg_mix = _rms_bwd(x0, w["g_mix"], dh, dx1, "rms_mix_bwd")

    dgla_w2 = dw2[:, :GLA_LOWRANK, :].transpose(1, 0, 2).reshape(GLA_LOWRANK, GLA_QK)
    return dict(
        loss=loss[0, 0], grad_x=gx.reshape(x.shape), recv_in=recv_in, recv_up=recv_up, recv_rows=recv_rows,
        w_ple_proj=dw_pp,
        g_mix=dg_mix, gla_b=dgla_b.reshape(1, GLA_QK), gla_norm=dgla_norm, dn_norm=ddn, g_mlp=dg_mlp, g_ple=dg_ple,
        g_final=dg_final, gla_w2=dgla_w2, dn_conv=dconv,
        dn_a_log=dal[:, :DN_HEADS], dn_dt_bias=ddt[:, :DN_HEADS])


def _first_weights(g_mix, w_in, gla_w2, gla_b, gla_norm, dn_conv, dn_a_log, dn_dt_bias, dn_norm, g_mlp, g_ple,
                   w_ple_proj, g_final):
    w_in_t = _all_gather(jnp.swapaxes(w_in[0], 0, 1).astype(BF16), "gather_w_in").reshape(D_IN, D_MODEL)
    w_pp = _all_gather(w_ple_proj[0].astype(BF16), "gather_w_ple_proj")
    small = _all_gather(_pack_rows([gla_w2[0], dn_conv[0]]), "gather_w_small")
    n_w2 = GLA_LOWRANK * GLA_QK // N_DEV // LANE
    n_cv = DN_CONV * DN_QKV // N_DEV // LANE
    w2 = small[:, :n_w2].reshape(N_DEV, GLA_LOWRANK, GLA_QK // N_DEV).transpose(1, 0, 2).reshape(GLA_LOWRANK, GLA_QK)
    conv = small[:, n_w2:n_w2 + n_cv].reshape(N_DEV, DN_CONV, DN_QKV // N_DEV).transpose(1, 0, 2).reshape(DN_CONV, DN_QKV)

    return dict(
        w_in_t=w_in_t, ws_t=_w_in_gate_rows(w_in_t), w_pp=_from_cols(w_pp),
        w2p=jnp.pad(w2, ((0, LANE - GLA_LOWRANK), (0, 0))), conv=conv,
        alog_b=jnp.pad(dn_a_log, ((0, 0), (0, LANE - DN_HEADS))),
        dtb_b=jnp.pad(dn_dt_bias, ((0, 0), (0, LANE - DN_HEADS))),
        g_mix=g_mix, gla_b=gla_b, gla_norm=gla_norm, dn_norm=dn_norm, g_mlp=g_mlp, g_ple=g_ple,
        g_final=g_final.reshape(1, D_MODEL))


def kernel(x, p, g_mix, w_in, gla_w2, gla_b, gla_norm, dn_conv, dn_a_log, dn_dt_bias, dn_norm, w_out, g_mlp, w_up, w_down, g_ple, w_ple_gate, w_ple_proj, g_final, loss_target, m_g_mix, m_w_in, m_gla_w2, m_gla_b, m_gla_norm, m_dn_conv, m_dn_a_log, m_dn_dt_bias, m_dn_norm, m_w_out, m_g_mlp, m_w_up, m_w_down, m_g_ple, m_w_ple_gate, m_w_ple_proj, m_g_final, v_g_mix, v_w_in, v_gla_w2, v_gla_b, v_gla_norm, v_dn_conv, v_dn_a_log, v_dn_dt_bias, v_dn_norm, v_w_out, v_g_mlp, v_w_up, v_w_down, v_g_ple, v_w_ple_gate, v_w_ple_proj, v_g_final):
    names = ["g_mix", "w_in", "gla_w2", "gla_b", "gla_norm", "dn_conv", "dn_a_log", "dn_dt_bias", "dn_norm", "w_out",
             "g_mlp", "w_up", "w_down", "g_ple", "w_ple_gate", "w_ple_proj", "g_final"]
    ws = dict(zip(names, (g_mix, w_in, gla_w2, gla_b, gla_norm, dn_conv, dn_a_log, dn_dt_bias, dn_norm, w_out, g_mlp,
                          w_up, w_down, g_ple, w_ple_gate, w_ple_proj, g_final)))
    ms = dict(zip(names, (m_g_mix, m_w_in, m_gla_w2, m_gla_b, m_gla_norm, m_dn_conv, m_dn_a_log, m_dn_dt_bias,
                          m_dn_norm, m_w_out, m_g_mlp, m_w_up, m_w_down, m_g_ple, m_w_ple_gate, m_w_ple_proj,
                          m_g_final)))
    vs = dict(zip(names, (v_g_mix, v_w_in, v_gla_w2, v_gla_b, v_gla_norm, v_dn_conv, v_dn_a_log, v_dn_dt_bias,
                          v_dn_norm, v_w_out, v_g_mlp, v_w_up, v_w_down, v_g_ple, v_w_ple_gate, v_w_ple_proj,
                          v_g_final)))
    me = 4 * lax.axis_index("x") + 2 * lax.axis_index("y") + lax.axis_index("c")

    first = _first_weights(g_mix, w_in, gla_w2, gla_b, gla_norm, dn_conv, dn_a_log, dn_dt_bias, dn_norm, g_mlp,
                           g_ple, w_ple_proj, g_final)
    rows_shard = jnp.concatenate([w_out[0], w_ple_gate[0], w_down[0]], axis=0).astype(BF16)
    r = _local_step(x, p[0], loss_target, first, w_up[0].astype(BF16), rows_shard)
    loss = lax.psum(r["loss"], ("x", "y", "c"))

    grads, deltas, new_m, new_v = {}, {}, {}, {}

    def big(name, parts, row0=0):
        g, d, nm, nv = _adamw_reduce(ws[name][0], ms[name][0], vs[name][0], parts, row0, "adamw_" + name)
        grads[name], deltas[name], new_m[name], new_v[name] = g[None], d[None], nm[None], nv[None]

    t_outs = _adamw_reduce(*[jnp.swapaxes(d["w_in"][0], 0, 1) for d in (ws, ms, vs)], r["recv_in"], 0, "adamw_w_in")
    grads["w_in"], deltas["w_in"], new_m["w_in"], new_v["w_in"] = [jnp.swapaxes(o, 0, 1)[None] for o in t_outs]
    big("w_up", r["recv_up"])
    big("w_ple_proj", _all_to_all(_by_cols(r["w_ple_proj"]), "scatter_d_w_ple_proj"))
    big("w_out", r["recv_rows"], 0)
    big("w_ple_gate", r["recv_rows"], ROWS_OUT)
    big("w_down", r["recv_rows"], 2 * ROWS_OUT)

    vec_names = ["g_mix", "gla_b", "gla_norm", "dn_norm", "g_mlp", "g_ple", "g_final", "dn_a_log", "dn_dt_bias"]
    gathered_names = vec_names + ["gla_w2", "dn_conv"]
    total = _small_reduce(_all_gather(_pack_rows([r[n] for n in gathered_names]), "gather_small_grads"))
    sg = dict(zip(gathered_names, _unpack_rows(total, [r[n].shape for n in gathered_names])))
    sg["g_final"] = sg["g_final"].reshape(D_MODEL)
    sg["gla_w2"] = lax.dynamic_slice_in_dim(sg["gla_w2"], me * (GLA_QK // N_DEV), GLA_QK // N_DEV, axis=1)
    sg["dn_conv"] = lax.dynamic_slice_in_dim(sg["dn_conv"], me * (DN_QKV // N_DEV), DN_QKV // N_DEV, axis=1)
    small_names = gathered_names
    shapes = [ws[n].shape for n in small_names]
    d_s, m_s, v_s = _adamw_small(_pack_rows([ws[n] for n in small_names]), _pack_rows([ms[n] for n in small_names]),
                                 _pack_rows([vs[n] for n in small_names]), _pack_rows([sg[n] for n in small_names]))
    for n, d, nm, nv in zip(small_names, _unpack_rows(d_s, shapes), _unpack_rows(m_s, shapes), _unpack_rows(v_s, shapes)):
        grads[n], deltas[n], new_m[n], new_v[n] = sg[n].reshape(ws[n].shape), d, nm, nv

    return (loss, r["grad_x"], *[grads[n] for n in names], *[deltas[n] for n in names],
            *[new_m[n] for n in names], *[new_v[n] for n in names])
```

```python
import functools
import math

import jax
import jax.numpy as jnp
from jax import lax
from jax.experimental import pallas as pl
from jax.experimental.pallas import tpu as pltpu

F32 = jnp.float32
BF16 = jnp.bfloat16

N_DEV = 8
D_MODEL = 2048
CHUNK = 64
PLE_DIM = 256
EPS = 1e-6
GLA_HEADS = 4
GLA_DK = 256
GLA_DV = 512
GLA_LOWRANK = 16
GLA_TAU = 16.0
DN_HEADS = 16
DN_D = 128
DN_CONV = 4
D_FF = 4 * D_MODEL
GLA_QK = GLA_HEADS * GLA_DK
GLA_V = GLA_HEADS * GLA_DV
DN_QKV = 3 * DN_HEADS * DN_D
D_IN = 2 * GLA_QK + 2 * GLA_V + GLA_LOWRANK + DN_QKV + D_MODEL + 2 * DN_HEADS + 2 * D_MODEL
D_IN_SHARD = D_IN // N_DEV

ADAM_LR = 0.001
ADAM_B1 = 0.9
ADAM_B2 = 0.999
ADAM_EPS = 1e-08
ADAM_WD = 0.01
ADAM_STEP = 10

LANE = 128
ZB_GQ, ZB_GK, ZB_GV, ZB_GG = 0, 1024, 2048, 4096
ZB_DQKV, ZB_DZ, ZB_GA, ZB_GB = 6144, 12288, 14336, 16384
ZB_W = 18432
ZS_LR, ZS_DA, ZS_DB = 0, 128, 256
ZS_W = 384
WI_LR = 2 * GLA_QK + 2 * GLA_V
WI_DQKV = WI_LR + GLA_LOWRANK
WI_DA = WI_DQKV + DN_QKV + D_MODEL
WI_DB = WI_DA + DN_HEADS
WI_GA = WI_DB + DN_HEADS

VMEM_LIMIT = 56 * 1024 * 1024

def _bdot(a, b, dims):
    return lax.dot_general(a.astype(BF16), b.astype(BF16), dims, preferred_element_type=F32)


def _split3(x):
    hi = x.astype(BF16)
    rest = x - hi.astype(F32)
    mid = rest.astype(BF16)
    return hi, mid, (rest - mid.astype(F32)).astype(BF16)


def _dot01(m, x, dims):
    m = m.astype(BF16)
    out = None
    for piece in _split3(x):
        d = lax.dot_general(m, piece, dims, preferred_element_type=F32)
        out = d if out is None else out + d
    return out


@functools.partial(jax.custom_vjp, nondiff_argnums=(2, 3))
def _left_dot(m, x, dims, dims_t):
    return _dot01(m, x, dims)


def _left_dot_fwd(m, x, dims, dims_t):
    return _dot01(m, x, dims), m


def _left_dot_bwd(dims, dims_t, m, ct):
    return jnp.zeros_like(m), _dot01(m, ct, dims_t)


_left_dot.defvjp(_left_dot_fwd, _left_dot_bwd)


def _dot3(a, b, dims):
    ah, bh = a.astype(BF16), b.astype(BF16)
    al, bl = (a - ah.astype(F32)).astype(BF16), (b - bh.astype(F32)).astype(BF16)
    dot = functools.partial(lax.dot_general, dimension_numbers=dims, preferred_element_type=F32)
    return dot(ah, bh) + (dot(ah, bl) + dot(al, bh))


def _sigmoid(x):
    return 1.0 / (1.0 + jnp.exp(-x))


def _silu(x):
    return x * _sigmoid(x)


def _softplus(x):
    return jnp.maximum(x, 0.0) + jnp.log(1.0 + jnp.exp(-jnp.abs(x)))


def _iota2(shape, dim):
    return lax.broadcasted_iota(jnp.int32, shape, dim)


def _cparams(sem=None):
    return pltpu.CompilerParams(dimension_semantics=sem, vmem_limit_bytes=VMEM_LIMIT)


BNN = (((2,), (1,)), ((0,), (0,)))
BNT = (((2,), (2,)), ((0,), (0,)))
BTN = (((1,), (1,)), ((0,), (0,)))


def _gla_chunk(st, q, k, v, lr, w2, b, gg, ga, gn):
    hb, c, _ = q.shape
    incl = (_iota2((c, c), 0) >= _iota2((c, c), 1))[None]
    tri = jnp.broadcast_to(incl.astype(F32), (hb, c, c))
    q = q.astype(F32) * (GLA_DK ** -0.5)
    k = k.astype(F32)
    v = v.astype(F32)
    lr_b = jnp.broadcast_to(lr[None], (hb,) + lr.shape)
    lf = -_softplus(-(_bdot(lr_b, w2, BNN) + b)) / GLA_TAU
    bcum = _left_dot(tri, lf, BNN, BTN)
    blast = jnp.sum(lf, axis=1, keepdims=True)
    q_in = q * jnp.exp(bcum)
    k_in = k * jnp.exp(-bcum)
    a = jnp.where(incl, _bdot(q_in, k_in, BNT), 0.0)
    o = _bdot(a, v, BNN) + _bdot(q_in, st, BNT)
    k_dec = k * jnp.exp(blast - bcum)
    st_new = st * jnp.exp(blast) + _bdot(v, k_dec, BTN)
    on = o * lax.rsqrt(jnp.mean(o * o, axis=-1, keepdims=True) + EPS) * gn
    res = _sigmoid(ga.astype(F32)) * on * _silu(gg.astype(F32))
    return res, st_new


def _tri_inv_raw(a):
    _, c, _ = a.shape
    eye = (_iota2((c, c), 0) == _iota2((c, c), 1)).astype(F32)[None]
    x = a
    p = eye - a
    for _ in range(5):
        x = _dot3(x, x, BNN)
        p = _dot3(p, eye + x, BNN)
    return p


def _tri_inv_bwd(t, dt):
    return (-_dot3(_dot3(t, dt, BTN), t, BNT),)


@jax.custom_vjp
def _tri_inv_given(a, t):
    return t


def _tri_inv_given_fwd(a, t):
    return t, t


def _tri_inv_given_bwd(t, dt):
    return _tri_inv_bwd(t, dt) + (jnp.zeros_like(t),)


_tri_inv_given.defvjp(_tri_inv_given_fwd, _tri_inv_given_bwd)


@functools.partial(jax.custom_vjp, nondiff_argnums=(1, 2))
def _column_on_lanes(z, j, width):
    picked = jnp.where(_iota2(z.shape, 1) == j, z, 0.0)
    return jnp.broadcast_to(jnp.sum(picked, axis=1, keepdims=True), (z.shape[0], width))


def _column_on_lanes_fwd(z, j, width):
    return _column_on_lanes(z, j, width), None


def _column_on_lanes_bwd(j, width, _, ct):
    shape = (ct.shape[0], LANE)
    total = jnp.broadcast_to(jnp.sum(ct, axis=1, keepdims=True), shape)
    return (jnp.where(_iota2(shape, 1) == j, total, 0.0),)


_column_on_lanes.defvjp(_column_on_lanes_fwd, _column_on_lanes_bwd)


def _dn_chunk(s, qr, kr, vr, za, zb, alog, dtb, gz, gb, dn, t_saved=None):
    hb, c, _ = qr.shape
    row, col = _iota2((c, c), 0), _iota2((c, c), 1)
    incl = (row >= col)[None]
    strict = (row > col)[None]
    tri = jnp.broadcast_to(incl.astype(F32), (hb, c, c))

    def l2n(t):
        return t * lax.rsqrt(jnp.sum(t * t, axis=-1, keepdims=True) + EPS)

    q = l2n(qr.astype(F32)) * (DN_D ** -0.5)
    k = l2n(kr.astype(F32))
    v = vr.astype(F32)
    g_heads = -jnp.exp(alog) * _softplus(za + dtb)
    beta_heads = _sigmoid(zb)
    def per_head(z, width):
        return jnp.concatenate([_column_on_lanes(z, j, width)[None] for j in range(hb)], axis=0)

    g = per_head(g_heads, LANE)
    beta = per_head(beta_heads, LANE)
    gcum = _left_dot(tri, g, BNN, BTN)
    glast = jnp.sum(g, axis=1, keepdims=True)
    diff = _left_dot(tri, per_head(g_heads, c) * strict.astype(F32), BNN, BTN)
    dec = jnp.exp(jnp.where(incl, diff, -1e30))
    kb = k * beta
    a = jnp.where(strict, _bdot(kb, k, BNT) * dec, 0.0)
    t = _tri_inv_raw(a) if t_saved is None else _tri_inv_given(a, t_saved)
    egc = jnp.exp(gcum)
    u = _bdot(t, v * beta, BNN)
    w = _bdot(t, kb * egc, BNN)
    attn = jnp.where(incl, _bdot(q, k, BNT) * dec, 0.0)
    q_dec = q * egc
    k_dec = k * jnp.exp(glast - gcum)
    v_new = u - _bdot(w, s, BNN)
    o = _bdot(q_dec, s, BNN) + _bdot(attn, v_new, BNN)
    s_new = s * jnp.exp(glast) + _bdot(k_dec, v_new, BTN)
    on = o * lax.rsqrt(jnp.mean(o * o, axis=-1, keepdims=True) + EPS) * dn
    res = _sigmoid(gb.astype(F32)) * on * _silu(gz.astype(F32))
    return (res, s_new, t) if t_saved is None else (res, s_new)


def _heads(ref, n_heads, width):
    return jnp.stack([ref[:, j * width:(j + 1) * width] for j in range(n_heads)], axis=0)


def _gla_specs(nc, reverse):
    def rows(b, n):
        return b * nc + ((nc - 1 - n) if reverse else n)

    qk = lambda base: pl.BlockSpec((CHUNK, GLA_QK), lambda b, n: (rows(b, n), base // GLA_QK))
    vv = lambda base: pl.BlockSpec((CHUNK, GLA_V), lambda b, n: (rows(b, n), base // GLA_V))
    lr = lambda c: pl.BlockSpec((CHUNK, LANE), lambda b, n: (rows(b, n), c))
    full = lambda shape: pl.BlockSpec(shape, lambda b, n: (0,) * len(shape))
    return rows, qk, vv, lr, full


def _gla_inputs(q_ref, k_ref, v_ref, gg_ref, ga_ref, lr_ref, w2_ref, b_ref, gn_ref):
    return (_heads(q_ref, GLA_HEADS, GLA_DK), _heads(k_ref, GLA_HEADS, GLA_DK), _heads(v_ref, GLA_HEADS, GLA_DV),
            lr_ref[...], _heads(w2_ref, GLA_HEADS, GLA_DK), _heads(b_ref, GLA_HEADS, GLA_DK),
            _heads(gg_ref, GLA_HEADS, GLA_DV), _heads(ga_ref, GLA_HEADS, GLA_DV), gn_ref[...])


def _gla_fwd(zb, zs, w2p, gla_b, gla_norm, bsz, nc):
    t = zb.shape[0]
    rows, qk, vv, lr, full = _gla_specs(nc, False)

    def body(q_ref, k_ref, v_ref, gg_ref, ga_ref, lr_ref, w2_ref, b_ref, gn_ref, o_ref, st_ref, state):
        st = jnp.where(pl.program_id(1) > 0, state[...], 0.0)
        st_ref[0, 0] = st
        res, st_new = _gla_chunk(st, *_gla_inputs(q_ref, k_ref, v_ref, gg_ref, ga_ref, lr_ref, w2_ref, b_ref, gn_ref))
        for j in range(GLA_HEADS):
            o_ref[:, j * GLA_DV:(j + 1) * GLA_DV] = res[j]
        state[...] = st_new

    return pl.pallas_call(
        body, name="gla_fwd", grid=(bsz, nc),
        in_specs=[qk(ZB_GQ), qk(ZB_GK), vv(ZB_GV), vv(ZB_GG), vv(ZB_GA), lr(ZS_LR // LANE),
                  full((LANE, GLA_QK)), full((1, GLA_QK)), full((1, GLA_DV))],
        out_specs=[vv(0), pl.BlockSpec((1, 1, GLA_HEADS, GLA_DV, GLA_DK), lambda b, n: (b, n, 0, 0, 0))],
        out_shape=[jax.ShapeDtypeStruct((t, GLA_V), F32),
                   jax.ShapeDtypeStruct((bsz, nc, GLA_HEADS, GLA_DV, GLA_DK), F32)],
        scratch_shapes=[pltpu.VMEM((GLA_HEADS, GLA_DV, GLA_DK), F32)],
        compiler_params=_cparams(("arbitrary", "arbitrary")),
    )(zb, zb, zb, zb, zb, zs, w2p, gla_b, gla_norm)


def _gla_bwd(zb, zs, w2p, gla_b, gla_norm, states, dmixed, bsz, nc, rider=None):
    t = zb.shape[0]
    rows, qk, vv, lr, full = _gla_specs(nc, True)

    def body(q_ref, k_ref, v_ref, gg_ref, ga_ref, lr_ref, w2_ref, b_ref, gn_ref, st_ref, dm_ref,
             dq_ref, dk_ref, dv_ref, dgg_ref, dga_ref, dlr_ref, dw2_ref, db_ref, dgn_ref, dstate):
        b, n = pl.program_id(0), pl.program_id(1)

        @pl.when((b == 0) & (n == 0))
        def _():
            dw2_ref[...] = jnp.zeros_like(dw2_ref)
            db_ref[...] = jnp.zeros_like(db_ref)
            dgn_ref[...] = jnp.zeros_like(dgn_ref)

        _, vjp = jax.vjp(_gla_chunk, st_ref[0, 0],
                         *_gla_inputs(q_ref, k_ref, v_ref, gg_ref, ga_ref, lr_ref, w2_ref, b_ref, gn_ref))
        dst_in = jnp.where(n > 0, dstate[...], 0.0)
        dst, dq, dk, dv, dlr, dw2, db, dgg, dga, dgn = vjp((_heads(dm_ref, GLA_HEADS, GLA_DV).astype(F32), dst_in))
        dstate[...] = dst
        for j in range(GLA_HEADS):
            dq_ref[:, j * GLA_DK:(j + 1) * GLA_DK] = dq[j].astype(dq_ref.dtype)
            dk_ref[:, j * GLA_DK:(j + 1) * GLA_DK] = dk[j].astype(dk_ref.dtype)
            dv_ref[:, j * GLA_DV:(j + 1) * GLA_DV] = dv[j].astype(dv_ref.dtype)
            dgg_ref[:, j * GLA_DV:(j + 1) * GLA_DV] = dgg[j].astype(dgg_ref.dtype)
            dga_ref[:, j * GLA_DV:(j + 1) * GLA_DV] = dga[j].astype(dga_ref.dtype)
        dlr_ref[...] = dlr
        dw2_ref[...] += dw2
        db_ref[...] += db
        dgn_ref[...] += dgn

    return _hosted_call(
        body, rider, name="gla_bwd", grid=(bsz, nc),
        in_specs=[qk(ZB_GQ), qk(ZB_GK), vv(ZB_GV), vv(ZB_GG), vv(ZB_GA), lr(ZS_LR // LANE),
                  full((LANE, GLA_QK)), full((1, GLA_QK)), full((1, GLA_DV)),
                  pl.BlockSpec((1, 1, GLA_HEADS, GLA_DV, GLA_DK), lambda b, n: (b, nc - 1 - n, 0, 0, 0)),
                  vv(0)],
        out_specs=[qk(0), qk(0), vv(0), vv(0), vv(0), lr(0),
                   full((GLA_HEADS, LANE, GLA_DK)), full((GLA_HEADS, 1, GLA_DK)), full((1, GLA_DV))],
        out_shape=[jax.ShapeDtypeStruct((t, GLA_QK), BF16), jax.ShapeDtypeStruct((t, GLA_QK), BF16),
                   jax.ShapeDtypeStruct((t, GLA_V), BF16), jax.ShapeDtypeStruct((t, GLA_V), BF16),
                   jax.ShapeDtypeStruct((t, GLA_V), BF16), jax.ShapeDtypeStruct((t, LANE), F32),
                   jax.ShapeDtypeStruct((GLA_HEADS, LANE, GLA_DK), F32),
                   jax.ShapeDtypeStruct((GLA_HEADS, 1, GLA_DK), F32),
                   jax.ShapeDtypeStruct((1, GLA_DV), F32)],
        scratch_shapes=[pltpu.VMEM((GLA_HEADS, GLA_DV, GLA_DK), F32)],
        args=(zb, zb, zb, zb, zb, zs, w2p, gla_b, gla_norm, states, dmixed))


DN_HB = DN_HEADS


def _dn_specs(nc, reverse):
    wide = DN_HB * DN_D

    def rows(b, n, h):
        return b * nc + ((nc - 1 - n) if reverse else n)

    def col(base):
        return pl.BlockSpec((CHUNK, wide), lambda b, n, h: (rows(b, n, h), base // wide + h))

    def fixed(c):
        return pl.BlockSpec((CHUNK, LANE), lambda b, n, h: (rows(b, n, h), c))

    head = pl.BlockSpec((1, LANE), lambda b, n, h: (0, 0))
    return rows, col, fixed, head


def _lanes(j):
    return slice(j * DN_D, (j + 1) * DN_D)


def _by_head(ref):
    return jnp.stack([ref[:, _lanes(j)] for j in range(DN_HB)], axis=0)


def _dn_fwd(act, zb, zs, alog_b, dtb_b, dn_norm, mix_gla, bsz, nc, rider=None):
    t = zb.shape[0]
    rows, col, fixed, head = _dn_specs(nc, False)

    def body(q_ref, k_ref, v_ref, za_ref, zb_ref, al_ref, dt_ref, gz_ref, gb_ref, dn_ref, mg_ref,
             o_ref, st_ref, ti_ref, state):
        s = jnp.where(pl.program_id(1) > 0, state[...], 0.0)
        st_ref[0, 0] = s
        res, s_new, t_inv = _dn_chunk(s, _by_head(q_ref), _by_head(k_ref), _by_head(v_ref), za_ref[...], zb_ref[...],
                                      al_ref[...], dt_ref[...], _by_head(gz_ref), _by_head(gb_ref), dn_ref[...])
        ti_ref[0, 0] = t_inv
        for j in range(DN_HB):
            o_ref[:, _lanes(j)] = (res[j] + mg_ref[:, _lanes(j)]).astype(o_ref.dtype)
        state[...] = s_new

    return _hosted_call(
        body, rider, name="dn_fwd", grid=(bsz, nc, DN_HEADS // DN_HB),
        in_specs=[col(0), col(DN_HEADS * DN_D), col(2 * DN_HEADS * DN_D),
                  fixed(ZS_DA // LANE), fixed(ZS_DB // LANE), head, head,
                  col(ZB_DZ), col(ZB_GB), pl.BlockSpec((1, DN_D), lambda b, n, h: (0, 0)), col(0)],
        out_specs=[col(0), pl.BlockSpec((1, 1, DN_HB, DN_D, DN_D), lambda b, n, h: (b, n, h, 0, 0)),
                   pl.BlockSpec((1, 1, DN_HB, CHUNK, CHUNK), lambda b, n, h: (b, n, h, 0, 0))],
        out_shape=[jax.ShapeDtypeStruct((t, D_MODEL), BF16),
                   jax.ShapeDtypeStruct((bsz, nc, DN_HEADS, DN_D, DN_D), F32),
                   jax.ShapeDtypeStruct((bsz, nc, DN_HEADS, CHUNK, CHUNK), F32)],
        scratch_shapes=[pltpu.VMEM((DN_HEADS, DN_D, DN_D), F32)],
        args=(act, act, act, zs, zs, alog_b, dtb_b, zb, zb, dn_norm, mix_gla))


def _dn_bwd(act, zb, zs, alog_b, dtb_b, dn_norm, states, t_invs, dmixed, bsz, nc, rider=None):
    t = zb.shape[0]
    rows, col, fixed, head = _dn_specs(nc, True)

    def body(q_ref, k_ref, v_ref, za_ref, zb_ref, al_ref, dt_ref, gz_ref, gb_ref, dn_ref, st_ref, ti_ref, dm_ref,
             dact_ref, dza_ref, dzb_ref, dgz_ref, dgb_ref, dal_ref, ddt_ref, ddn_ref, dstate):
        b, n = pl.program_id(0), pl.program_id(1)

        @pl.when((b == 0) & (n == 0))
        def _():
            dal_ref[...] = jnp.zeros_like(dal_ref)
            ddt_ref[...] = jnp.zeros_like(ddt_ref)
            ddn_ref[...] = jnp.zeros_like(ddn_ref)

        fn = functools.partial(_dn_chunk, t_saved=ti_ref[0, 0])
        _, vjp = jax.vjp(fn, st_ref[0, 0], _by_head(q_ref), _by_head(k_ref), _by_head(v_ref), za_ref[...],
                         zb_ref[...], al_ref[...], dt_ref[...], _by_head(gz_ref), _by_head(gb_ref), dn_ref[...])
        ds_in = jnp.where(n > 0, dstate[...], 0.0)
        ds, dq, dk, dv, dza, dzb, dal, ddt, dgz, dgb, ddn = vjp((_by_head(dm_ref).astype(F32), ds_in))
        dstate[...] = ds
        for j in range(DN_HB):
            for part, d in enumerate((dq, dk, dv)):
                dact_ref[:, pl.ds(part * DN_HEADS * DN_D + j * DN_D, DN_D)] = d[j]
            dgz_ref[:, _lanes(j)] = dgz[j].astype(dgz_ref.dtype)
            dgb_ref[:, _lanes(j)] = dgb[j].astype(dgb_ref.dtype)
        dal_ref[...] += dal
        ddt_ref[...] += ddt
        dza_ref[...] = dza
        dzb_ref[...] = dzb
        ddn_ref[...] += ddn

    full = lambda shape: pl.BlockSpec(shape, lambda b, n, h: (0,) * len(shape))
    return _hosted_call(
        body, rider, name="dn_bwd", grid=(bsz, nc, DN_HEADS // DN_HB),
        in_specs=[col(0), col(DN_HEADS * DN_D), col(2 * DN_HEADS * DN_D),
                  fixed(ZS_DA // LANE), fixed(ZS_DB // LANE), head, head,
                  col(ZB_DZ), col(ZB_GB), pl.BlockSpec((1, DN_D), lambda b, n, h: (0, 0)),
                  pl.BlockSpec((1, 1, DN_HB, DN_D, DN_D), lambda b, n, h: (b, nc - 1 - n, h, 0, 0)),
                  pl.BlockSpec((1, 1, DN_HB, CHUNK, CHUNK), lambda b, n, h: (b, nc - 1 - n, h, 0, 0)), col(0)],
        out_specs=[pl.BlockSpec((CHUNK, DN_QKV), lambda b, n, h: (rows(b, n, h), 0)), fixed(0), fixed(0), col(0), col(0),
                   full((1, LANE)), full((1, LANE)), full((1, DN_D))],
        out_shape=[jax.ShapeDtypeStruct((t, DN_QKV), F32),
                   jax.ShapeDtypeStruct((t, LANE), F32), jax.ShapeDtypeStruct((t, LANE), F32),
                   jax.ShapeDtypeStruct((t, D_MODEL), BF16), jax.ShapeDtypeStruct((t, D_MODEL), BF16),
                   jax.ShapeDtypeStruct((1, LANE), F32), jax.ShapeDtypeStruct((1, LANE), F32),
                   jax.ShapeDtypeStruct((1, DN_D), F32)],
        scratch_shapes=[pltpu.VMEM((DN_HEADS, DN_D, DN_D), F32)],
        args=(act, act, act, zs, zs, alog_b, dtb_b, zb, zb, dn_norm, states, t_invs, dmixed))


MM_VMEM_BUDGET = 40 * 1024 * 1024
MM_TILE_PREF = (1024, 1024, 2048)


def _divisor_tile(n, cap):
    if n <= cap:
        return n
    for c in range(cap - cap % LANE, 0, -LANE):
        if n % c == 0:
            return c
    return n


def _mm_tiles(m, n, kd, a_bytes, b_bytes, mn_bytes):
    tm, tn, tk = (_divisor_tile(d, c) for d, c in zip((m, n, kd), MM_TILE_PREF))

    def need(tm, tn, tk):
        acc = 0 if tk == kd else 4 * tm * tn
        return 2 * (tm * tk * a_bytes + tk * tn * b_bytes + tm * tn * mn_bytes) + acc + 4 * tm * tn

    while need(tm, tn, tk) > MM_VMEM_BUDGET:
        if tk > 512 and tk * max(tm * a_bytes, tn * b_bytes) >= tm * tn * mn_bytes:
            tk = _divisor_tile(kd, tk // 2)
        elif tn >= tm and tn > LANE:
            tn = _divisor_tile(n, tn // 2)
        else:
            tm = _divisor_tile(m, tm // 2)
    return tm, tn, tk


def _w_in_row_of(tile, tile_rows):
    skipped = jnp.where(tile >= ZB_GA // tile_rows, WI_GA - ZB_GA, jnp.where(tile >= ZB_DQKV // tile_rows,
                                                                             WI_DQKV - ZB_DQKV, 0))
    return pl.multiple_of(tile * tile_rows + skipped, 16)


def _mm(a, b, *, ta=False, tb=False, out_dtypes=(F32,), epilogue=None, extras=(), name, rider=None,
        b_is_w_in_t=False, out_is_w_in_t=False):
    m, kd = (a.shape[1], a.shape[0]) if ta else a.shape
    n = b.shape[0] if tb else b.shape[1]
    if b_is_w_in_t:
        n, kd = (ZB_W, kd) if tb else (n, ZB_W)
    mn_bytes = sum(e.dtype.itemsize for e in extras) + sum(jnp.dtype(dt).itemsize for dt in out_dtypes)
    tm, tn, tk = _mm_tiles(m, n, kd, a.dtype.itemsize, b.dtype.itemsize, mn_bytes)
    nk = kd // tk
    n_ex = len(extras)
    dims = (((0,) if ta else (1,), (1,) if tb else (0,)), ((), ()))

    def finish(acc, ex_refs, out_refs):
        outs = (acc,) if epilogue is None else epilogue(acc, *[r[...] for r in ex_refs])
        for r, o in zip(out_refs, outs):
            r[...] = o.astype(r.dtype)

    def partial_product(a_ref, b_ref):
        return lax.dot_general(a_ref[...].astype(BF16), b_ref[...].astype(BF16), dims, preferred_element_type=F32)

    def body_single(*refs):
        finish(partial_product(refs[0], refs[1]), refs[2:2 + n_ex], refs[2 + n_ex:])

    def body_acc(*refs):
        acc = refs[-1]
        k = pl.program_id(2)

        @pl.when(k == 0)
        def _():
            acc[...] = partial_product(refs[0], refs[1])

        @pl.when(k > 0)
        def _():
            acc[...] += partial_product(refs[0], refs[1])

        @pl.when(k == nk - 1)
        def _():
            finish(acc[...], refs[2:2 + n_ex], refs[2 + n_ex:-1])

    a_spec = pl.BlockSpec((tk, tm), lambda i, j, k: (k, i)) if ta else pl.BlockSpec((tm, tk), lambda i, j, k: (i, k))
    b_spec = pl.BlockSpec((tn, tk), lambda i, j, k: (j, k)) if tb else pl.BlockSpec((tk, tn), lambda i, j, k: (k, j))
    mn_spec = pl.BlockSpec((tm, tn), lambda i, j, k: (i, j))
    out_spec, out_rows = mn_spec, m
    if b_is_w_in_t and tb:
        b_spec = pl.BlockSpec((pl.Element(tn), pl.Element(tk)),
                              lambda i, j, k: (_w_in_row_of(j, tn), pl.multiple_of(k * tk, LANE)))
    elif b_is_w_in_t:
        b_spec = pl.BlockSpec((pl.Element(tk), pl.Element(tn)),
                              lambda i, j, k: (_w_in_row_of(k, tk), pl.multiple_of(j * tn, LANE)))
    if out_is_w_in_t:
        out_spec, out_rows = pl.BlockSpec((pl.Element(tm), pl.Element(tn)),
                                          lambda i, j, k: (_w_in_row_of(i, tm), pl.multiple_of(j * tn, LANE))), D_IN
    outs = _hosted_call(
        body_single if nk == 1 else body_acc, rider, name=name, grid=(m // tm, n // tn, nk),
        in_specs=[a_spec, b_spec] + [mn_spec] * n_ex,
        out_specs=[out_spec] * len(out_dtypes),
        out_shape=[jax.ShapeDtypeStruct((out_rows, n), dt) for dt in out_dtypes],
        scratch_shapes=[] if nk == 1 else [pltpu.VMEM((tm, tn), F32)],
        args=(a, b, *extras))
    return outs[0] if len(outs) == 1 else outs


ROW_BLOCK = 256


def _row_spec(width=D_MODEL):
    return pl.BlockSpec((ROW_BLOCK, width), lambda i: (i, 0))


def _vec_spec(width=D_MODEL):
    return pl.BlockSpec((1, width), lambda i: (0, 0))


def _rms_fwd(x, g, name):
    def body(x_ref, g_ref, h_ref):
        xf = x_ref[...]
        h_ref[...] = (xf * lax.rsqrt(jnp.mean(xf * xf, axis=-1, keepdims=True) + EPS) * g_ref[...]).astype(BF16)

    t = x.shape[0]
    return pl.pallas_call(
        body, name=name, grid=(t // ROW_BLOCK,), in_specs=[_row_spec(), _vec_spec()], out_specs=_row_spec(),
        out_shape=jax.ShapeDtypeStruct((t, D_MODEL), BF16), compiler_params=_cparams(("parallel",)),
    )(x, g)


def _rms_bwd_math(xf, g, dh):
    rstd = lax.rsqrt(jnp.mean(xf * xf, axis=-1, keepdims=True) + EPS)
    xhat = xf * rstd
    dxhat = dh * g
    dx = rstd * (dxhat - xhat * jnp.mean(dxhat * xhat, axis=-1, keepdims=True))
    dg = jnp.sum(dh * xhat, axis=0, keepdims=True)
    return dx, dg


def _rms_bwd(x, g, dh, dres, name):
    def body(x_ref, g_ref, dh_ref, dres_ref, dx_ref, dg_ref):
        dx, dg = _rms_bwd_math(x_ref[...], g_ref[...], dh_ref[...].astype(F32))
        dx_ref[...] = dres_ref[...] + dx

        @pl.when(pl.program_id(0) == 0)
        def _():
            dg_ref[...] = jnp.zeros_like(dg_ref)

        dg_ref[...] += dg

    t = x.shape[0]
    return pl.pallas_call(
        body, name=name, grid=(t // ROW_BLOCK,),
        in_specs=[_row_spec(), _vec_spec(), _row_spec(), _row_spec()], out_specs=[_row_spec(), _vec_spec()],
        out_shape=[jax.ShapeDtypeStruct((t, D_MODEL), F32), jax.ShapeDtypeStruct((1, D_MODEL), F32)],
        compiler_params=_cparams(("arbitrary",)),
    )(x, g, dh, dres)


def _loss_head(x3, g, target):
    def body(x_ref, g_ref, t_ref, dx_ref, dg_ref, loss_ref):
        xf, gg = x_ref[...], g_ref[...]
        rstd = lax.rsqrt(jnp.mean(xf * xf, axis=-1, keepdims=True) + EPS)
        err = xf * rstd * gg - t_ref[...]
        dx, dg = _rms_bwd_math(xf, gg, err * (1.0 / D_MODEL))
        dx_ref[...] = dx

        @pl.when(pl.program_id(0) == 0)
        def _():
            dg_ref[...] = jnp.zeros_like(dg_ref)
            loss_ref[...] = jnp.zeros_like(loss_ref)

        dg_ref[...] += dg
        part = jnp.sum(jnp.sum(err * err, axis=-1, keepdims=True), axis=0, keepdims=True) * (0.5 / D_MODEL)
        loss_ref[...] += jnp.broadcast_to(part, loss_ref.shape)

    t = x3.shape[0]
    return pl.pallas_call(
        body, name="loss_head", grid=(t // ROW_BLOCK,),
        in_specs=[_row_spec(), _vec_spec(), _row_spec()], out_specs=[_row_spec(), _vec_spec(), _vec_spec(LANE)],
        out_shape=[jax.ShapeDtypeStruct((t, D_MODEL), F32), jax.ShapeDtypeStruct((1, D_MODEL), F32),
                   jax.ShapeDtypeStruct((1, LANE), F32)],
        compiler_params=_cparams(("arbitrary",)),
    )(x3, g, target)


def _ple_bwd(dx3, gpre, pp):
    def body(dx_ref, gp_ref, pp_ref, dgp_ref, dpp_ref):
        dx, sg = dx_ref[...], _sigmoid(gp_ref[...])
        dpp_ref[...] = (dx * sg).astype(BF16)
        dgp_ref[...] = (dx * pp_ref[...] * sg * (1.0 - sg)).astype(BF16)

    t = dx3.shape[0]
    return pl.pallas_call(
        body, name="ple_bwd", grid=(t // ROW_BLOCK,), in_specs=[_row_spec()] * 3, out_specs=[_row_spec()] * 2,
        out_shape=[jax.ShapeDtypeStruct((t, D_MODEL), BF16)] * 2, compiler_params=_cparams(("parallel",)),
    )(dx3, gpre, pp)


CONV_COLS = 256


def _shift_down(x, s):
    if s == 0:
        return x
    return jnp.where(_iota2(x.shape, 0) >= s, pltpu.roll(x, s, 0), 0.0)


def _shift_up(x, s):
    if s == 0:
        return x
    rows = x.shape[0]
    return jnp.where(_iota2(x.shape, 0) < rows - s, pltpu.roll(x, rows - s, 0), 0.0)


def _conv_taps(xf):
    return [_shift_down(xf, DN_CONV - 1 - j) for j in range(DN_CONV)]


def _conv_pre(taps, w):
    return sum(tap * w[j:j + 1, :] for j, tap in enumerate(taps))


def _conv_fwd(zb, conv_w, bsz, seq):
    def body(x_ref, w_ref, y_ref):
        y_ref[...] = _silu(_conv_pre(_conv_taps(x_ref[...]), w_ref[...]))

    nblk = DN_QKV // CONV_COLS
    return pl.pallas_call(
        body, name="conv_fwd", grid=(bsz, nblk),
        in_specs=[pl.BlockSpec((seq, CONV_COLS), lambda b, j: (b, ZB_DQKV // CONV_COLS + j)),
                  pl.BlockSpec((DN_CONV, CONV_COLS), lambda b, j: (0, j))],
        out_specs=pl.BlockSpec((seq, CONV_COLS), lambda b, j: (b, j)),
        out_shape=jax.ShapeDtypeStruct((bsz * seq, DN_QKV), F32),
        compiler_params=_cparams(("parallel", "parallel")),
    )(zb, conv_w)


def _conv_bwd(zb, conv_w, dact, bsz, seq):
    def body(x_ref, w_ref, dy_ref, dx_ref, dw_ref):
        taps, w = _conv_taps(x_ref[...]), w_ref[...]
        c = _conv_pre(taps, w)
        sg = _sigmoid(c)
        dc = dy_ref[...].astype(F32) * sg * (1.0 + c * (1.0 - sg))
        dx = sum(_shift_up(dc, DN_CONV - 1 - j) * w[j:j + 1, :] for j in range(DN_CONV))
        dx_ref[...] = dx.astype(BF16)
        dw = jnp.concatenate([jnp.sum(dc * tap, axis=0, keepdims=True) for tap in taps], axis=0)

        @pl.when(pl.program_id(1) == 0)
        def _():
            dw_ref[...] = jnp.zeros_like(dw_ref)

        dw_ref[...] += dw

    nblk = DN_QKV // CONV_COLS
    return pl.pallas_call(
        body, name="conv_bwd", grid=(nblk, bsz),
        in_specs=[pl.BlockSpec((seq, CONV_COLS), lambda j, b: (b, ZB_DQKV // CONV_COLS + j)),
                  pl.BlockSpec((DN_CONV, CONV_COLS), lambda j, b: (0, j)),
                  pl.BlockSpec((seq, CONV_COLS), lambda j, b: (b, j))],
        out_specs=[pl.BlockSpec((seq, CONV_COLS), lambda j, b: (b, j)),
                   pl.BlockSpec((DN_CONV, CONV_COLS), lambda j, b: (0, j))],
        out_shape=[jax.ShapeDtypeStruct((bsz * seq, DN_QKV), BF16), jax.ShapeDtypeStruct((DN_CONV, DN_QKV), F32)],
        compiler_params=_cparams(("parallel", "arbitrary")),
    )(zb, conv_w, dact)


MESH_IDS = pl.DeviceIdType.MESH
ANY_SPEC = pl.BlockSpec(memory_space=pl.ANY)


COMM_SCRATCH = (pltpu.SemaphoreType.DMA((7,)), pltpu.SemaphoreType.DMA((7,)), pltpu.SemaphoreType.DMA)


def _gather_phases(x_ref, out_ref, send_sems, recv_sems, local_sem):
    mx, my, mc = lax.axis_index("x"), lax.axis_index("y"), lax.axis_index("c")
    me, sibling = (mx, my, mc), (mx, my, 1 - mc)
    x_nbr, y_nbr, diag = (1 - mx, my), (mx, 1 - my), (1 - mx, 1 - my)
    north = mc == 1
    relayed = (jnp.where(north, mx, 1 - mx), jnp.where(north, 1 - my, my))
    relay_to = (jnp.where(north, 1 - mx, mx), jnp.where(north, my, 1 - my))

    def slot(px, py, pc):
        return out_ref.at[4 * px + 2 * py + pc]

    def copy(k, block, to, src=None):
        return pltpu.make_async_remote_copy(
            src_ref=slot(*block) if src is None else src, dst_ref=slot(*block),
            send_sem=send_sems.at[k], recv_sem=recv_sems.at[k], device_id=to, device_id_type=MESH_IDS)

    def mine():
        return pltpu.make_async_copy(x_ref, slot(*me), local_sem)

    def own_sends():
        return [copy(0, me, sibling, src=x_ref), copy(1, me, (*x_nbr, mc), src=x_ref),
                copy(2, me, (*y_nbr, mc), src=x_ref)]

    def later_sends():
        return [copy(3, (*relayed, mc), (*relay_to, mc)), copy(4, (*x_nbr, mc), sibling),
                copy(5, (*y_nbr, mc), sibling), copy(6, (*diag, mc), sibling)]

    def start():
        mine().start()
        for cp in own_sends():
            cp.start()

    def forward():
        copy(1, (*x_nbr, mc), me).wait_recv()
        copy(2, (*y_nbr, mc), me).wait_recv()
        for cp in later_sends()[:3]:
            cp.start()

    def forward_late():
        copy(3, (*diag, mc), me).wait_recv()
        later_sends()[3].start()

    def finish():
        copy(0, sibling, me).wait_recv()
        for k, chip in ((4, x_nbr), (5, y_nbr), (6, diag)):
            copy(k, (*chip, 1 - mc), me).wait_recv()
        for cp in own_sends() + later_sends():
            cp.wait_send()
        mine().wait()

    return start, forward, forward_late, finish


def _scatter_phases(x_ref, out_ref, send_sems, recv_sems, local_sem, among_chips=False):
    mx, my, mc = lax.axis_index("x"), lax.axis_index("y"), lax.axis_index("c")
    n_peers = 4 if among_chips else N_DEV
    me = 2 * mx + my if among_chips else 4 * mx + 2 * my + mc

    def peer(k):
        if among_chips:
            return (mx ^ ((k >> 1) & 1), my ^ (k & 1), mc)
        return (mx ^ ((k >> 2) & 1), my ^ ((k >> 1) & 1), mc ^ (k & 1))

    def slot_of(k):
        px, py, pc = peer(k)
        return 2 * px + py if among_chips else 4 * px + 2 * py + pc

    def copy(k, src_slot, dst_slot):
        return pltpu.make_async_remote_copy(
            src_ref=x_ref.at[src_slot], dst_ref=out_ref.at[dst_slot],
            send_sem=send_sems.at[k - 1], recv_sem=recv_sems.at[k - 1],
            device_id=peer(k), device_id_type=MESH_IDS)

    def sends():
        return [copy(k, slot_of(k), me) for k in range(1, n_peers)]

    def mine():
        return pltpu.make_async_copy(x_ref.at[me], out_ref.at[me], local_sem)

    def start():
        mine().start()
        for cp in sends():
            cp.start()

    def finish():
        for k in range(1, n_peers):
            copy(k, me, slot_of(k)).wait_recv()
        for cp in sends():
            cp.wait_send()
        mine().wait()

    return start, _nothing, _nothing, finish


def _pair_phases(x_ref, out_ref, send_sems, recv_sems, local_sem):
    mx, my, mc = lax.axis_index("x"), lax.axis_index("y"), lax.axis_index("c")

    def copy(side):
        return pltpu.make_async_remote_copy(
            src_ref=x_ref.at[:, side], dst_ref=out_ref, send_sem=send_sems.at[0], recv_sem=recv_sems.at[0],
            device_id=(mx, my, 1 - mc), device_id_type=MESH_IDS)

    def start():
        copy(1 - mc).start()

    def finish():
        copy(mc).wait_recv()
        copy(1 - mc).wait_send()

    return start, _nothing, _nothing, finish


def _nothing():
    pass


class _Rider:
    def __init__(self, phases, x, out_shape):
        self.phases, self.x, self.out_shape = phases, x, out_shape


def _gather_rider(x):
    return _Rider(_gather_phases, x, jax.ShapeDtypeStruct((N_DEV,) + x.shape, x.dtype))


def _scatter_rider(x):
    return _Rider(_scatter_phases, x, jax.ShapeDtypeStruct(x.shape, x.dtype))


def _chip_scatter_rider(x):
    return _Rider(functools.partial(_scatter_phases, among_chips=True), x, jax.ShapeDtypeStruct(x.shape, x.dtype))


def _pair_rider(x):
    return _Rider(_pair_phases, x, jax.ShapeDtypeStruct((x.shape[0],) + x.shape[2:], x.dtype))


def _exchange(rider, name):
    def body(x_ref, out_ref, send_sems, recv_sems, local_sem):
        for phase in rider.phases(x_ref, out_ref, send_sems, recv_sems, local_sem):
            phase()

    return pl.pallas_call(body, name=name, out_shape=rider.out_shape, in_specs=[ANY_SPEC], out_specs=ANY_SPEC,
                          scratch_shapes=list(COMM_SCRATCH))(rider.x)


def _all_gather(x, name):
    return _exchange(_gather_rider(x), name)


def _all_to_all(x, name):
    return _exchange(_scatter_rider(x), name)


def _hosted_call(body, rider, *, name, grid, in_specs, out_specs, out_shape, scratch_shapes, args):
    if rider is None:
        return pl.pallas_call(body, name=name, grid=grid, in_specs=in_specs, out_specs=out_specs, out_shape=out_shape,
                              scratch_shapes=scratch_shapes, compiler_params=_cparams(("arbitrary",) * len(grid)))(*args)
    n_in, n_out, n_scr = len(in_specs), len(out_specs), len(scratch_shapes)
    total = math.prod(grid)

    def riding(*refs):
        host_in, x_ref = refs[:n_in], refs[n_in]
        host_out, out_ref = refs[n_in + 1:n_in + 1 + n_out], refs[n_in + 1 + n_out]
        host_scr = refs[n_in + 2 + n_out:n_in + 2 + n_out + n_scr]
        start, forward, forward_late, finish = rider.phases(x_ref, out_ref, *refs[n_in + 2 + n_out + n_scr:])
        step = 0
        for axis, size in enumerate(grid):
            step = step * size + pl.program_id(axis)
        pl.when(step == 0)(start)
        pl.when(step == total // 2)(forward)
        pl.when(step == (3 * total) // 4)(forward_late)
        body(*host_in, *host_out, *host_scr)
        pl.when(step == total - 1)(finish)

    return pl.pallas_call(
        riding, name=name, grid=grid, in_specs=list(in_specs) + [ANY_SPEC], out_specs=list(out_specs) + [ANY_SPEC],
        out_shape=list(out_shape) + [rider.out_shape], scratch_shapes=list(scratch_shapes) + list(COMM_SCRATCH),
        compiler_params=_cparams(("arbitrary",) * len(grid)))(*args, rider.x)


def _adamw_math(w, g, m, v):
    m = ADAM_B1 * m + (1.0 - ADAM_B1) * g
    v = ADAM_B2 * v + (1.0 - ADAM_B2) * jnp.square(g)
    m_hat = m / (1.0 - ADAM_B1 ** ADAM_STEP)
    v_hat = v / (1.0 - ADAM_B2 ** ADAM_STEP)
    delta = -ADAM_LR * (m_hat / (jnp.sqrt(v_hat) + ADAM_EPS) + ADAM_WD * w)
    return delta, m, v


ADAM_ROWS = 128


def _elementwise_tile(rows, cols):
    if rows % ADAM_ROWS == 0:
        return ADAM_ROWS, cols
    return rows, (2 * LANE if cols % (2 * LANE) == 0 else cols)


def _add_blocks(a, b):
    g, rows, cols = a.shape
    tr, tc = _elementwise_tile(rows, cols)

    def body(a_ref, b_ref, o_ref):
        o_ref[...] = (a_ref[...].astype(F32) + b_ref[...].astype(F32)).astype(o_ref.dtype)

    blk = pl.BlockSpec((None, tr, tc), lambda k, i, j: (k, i, j))
    return pl.pallas_call(body, name="add_pair_blocks", grid=(g, rows // tr, cols // tc), in_specs=[blk, blk],
                          out_specs=blk, out_shape=jax.ShapeDtypeStruct(a.shape, a.dtype),
                          compiler_params=_cparams(("parallel", "parallel", "parallel")))(a, b)


def _adamw_reduce(w, m, v, parts, row0, name):
    rows, cols = w.shape
    n_parts = parts.shape[0]
    tr, tc = _elementwise_tile(rows, cols)
    r0 = row0 // tr

    def body(w_ref, m_ref, v_ref, *refs):
        part_refs, (g_ref, d_ref, nm_ref, nv_ref) = refs[:n_parts], refs[n_parts:]
        g = part_refs[0][...].astype(F32)
        for r in part_refs[1:]:
            g = g + r[...].astype(F32)
        delta, nm, nv = _adamw_math(w_ref[...], g, m_ref[...], v_ref[...])
        g_ref[...] = g
        d_ref[...] = delta
        nm_ref[...] = nm
        nv_ref[...] = nv

    blk = pl.BlockSpec((tr, tc), lambda i, j: (i, j))
    part_specs = [pl.BlockSpec((None, tr, tc), functools.partial(lambda i, j, k: (k, r0 + i, j), k=k))
                  for k in range(n_parts)]
    return pl.pallas_call(
        body, name=name, grid=(rows // tr, cols // tc), in_specs=[blk] * 3 + part_specs, out_specs=[blk] * 4,
        out_shape=[jax.ShapeDtypeStruct(w.shape, F32)] * 4, compiler_params=_cparams(("parallel", "parallel")),
    )(w, m, v, *([parts] * n_parts))


def _small_reduce(gathered):
    r = gathered.shape[1]

    def body(g_ref, o_ref):
        g = g_ref[0]
        for k in range(1, N_DEV):
            g = g + g_ref[k]
        o_ref[...] = g

    return pl.pallas_call(body, name="small_grad_reduce", out_shape=jax.ShapeDtypeStruct((r, LANE), F32))(gathered)


def _adamw_small(w, m, v, g):
    def body(w_ref, m_ref, v_ref, g_ref, d_ref, nm_ref, nv_ref):
        d_ref[...], nm_ref[...], nv_ref[...] = _adamw_math(w_ref[...], g_ref[...], m_ref[...], v_ref[...])

    return pl.pallas_call(body, name="adamw_small", out_shape=[jax.ShapeDtypeStruct(w.shape, F32)] * 3)(w, m, v, g)


def _pack_rows(arrays):
    rows = [jnp.pad(a.reshape(-1), (0, -a.size % LANE)).reshape(-1, LANE) for a in arrays]
    out = jnp.concatenate(rows, axis=0)
    return jnp.pad(out, ((0, -out.shape[0] % 8), (0, 0)))


def _unpack_rows(packed, shapes):
    out, r = [], 0
    for shp in shapes:
        size = math.prod(shp)
        nrows = -(-size // LANE)
        out.append(packed[r:r + nrows].reshape(-1)[:size].reshape(shp))
        r += nrows
    return out


def _add_residual(acc, res):
    return (res + acc,)


def _by_cols(g):
    return g.reshape(g.shape[0], N_DEV, -1).transpose(1, 0, 2)


def _from_cols(blocks):
    return blocks.transpose(1, 0, 2).reshape(blocks.shape[1], -1)


ROWS_OUT = D_MODEL // N_DEV

W_IN_SEGMENTS = ((0, WI_LR, "big", 0), (WI_LR, WI_DQKV, "gates", ZS_LR), (WI_DQKV, WI_DA, "big", ZB_DQKV),
                 (WI_DA, WI_DB, "gates", ZS_DA), (WI_DB, WI_GA, "gates", ZS_DB), (WI_GA, D_IN, "big", ZB_GA))


def _w_in_gate_rows(wt):
    parts = []
    for lo, hi, which, _ in W_IN_SEGMENTS:
        if which == "gates":
            parts += [wt[lo:hi], jnp.zeros((LANE - (hi - lo), wt.shape[1]), wt.dtype)]
    return jnp.concatenate(parts, axis=0)


def _fill_gate_rows(dw_in_t, dws_t):
    for lo, hi, which, first in W_IN_SEGMENTS:
        if which == "gates":
            dw_in_t = lax.dynamic_update_slice(dw_in_t, dws_t[first:first + hi - lo], (lo, 0))
    return dw_in_t


def _local_step(x, p, target, w, up_shard, rows_shard):
    bsz, seq, _ = x.shape
    t, nc = bsz * seq, seq // CHUNK
    x0, p2, tgt = x.reshape(t, D_MODEL), p.reshape(t, PLE_DIM), target.reshape(t, D_MODEL)

    h = _rms_fwd(x0, w["g_mix"], "rms_mix")
    zb, row_blocks = _mm(h, w["w_in_t"], tb=True, b_is_w_in_t=True, name="in_proj", rider=_gather_rider(rows_shard))
    zs = _mm(h, w["ws_t"], tb=True, name="in_proj_gates")
    act = _conv_fwd(zb, w["conv"], bsz, seq)
    mix_gla, gla_states = _gla_fwd(zb, zs, w["w2p"], w["gla_b"], w["gla_norm"], bsz, nc)
    mixed, dn_states, dn_t_invs, up_blocks = _dn_fwd(act, zb, zs, w["alog_b"], w["dtb_b"], w["dn_norm"], mix_gla,
                                                     bsz, nc, rider=_gather_rider(up_shard))
    w_up = _from_cols(up_blocks)
    w_out = row_blocks[:, :ROWS_OUT].reshape(D_MODEL, D_MODEL)
    w_pg = row_blocks[:, ROWS_OUT:2 * ROWS_OUT].reshape(D_MODEL, D_MODEL)
    w_down = row_blocks[:, 2 * ROWS_OUT:].reshape(D_FF, D_MODEL)
    x1 = _mm(mixed, w_out, epilogue=_add_residual, extras=(x0,), name="out_proj")
    h2 = _rms_fwd(x1, w["g_mlp"], "rms_mlp")
    u, a = _mm(h2, w_up, out_dtypes=(BF16, BF16), name="mlp_up",
               epilogue=lambda acc: (acc, jnp.square(jnp.maximum(acc, 0.0))))
    x2 = _mm(a, w_down, epilogue=_add_residual, extras=(x1,), name="mlp_down")
    h3 = _rms_fwd(x2, w["g_ple"], "rms_ple")
    pp = _mm(p2, w["w_pp"], name="ple_proj")
    gpre, x3 = _mm(h3, w_pg, out_dtypes=(F32, F32), extras=(x2, pp), name="ple_gate",
                   epilogue=lambda acc, res, proj: (acc, res + _sigmoid(acc) * proj))
    dx3, dg_final, loss = _loss_head(x3, w["g_final"], tgt)

    dgpre, dpp = _ple_bwd(dx3, gpre, pp)
    dw_pp = _mm(p2, dpp, ta=True, out_dtypes=(BF16,), name="d_w_ple_proj")
    dw_pg = _mm(h3, dgpre, ta=True, out_dtypes=(BF16,), name="d_w_ple_gate")
    dh3 = _mm(dgpre, w_pg, tb=True, name="d_h_ple")
    dx2, dg_ple = _rms_bwd(x2, w["g_ple"], dh3, dx3, "rms_ple_bwd")
    du = _mm(dx2, w_down, tb=True, out_dtypes=(BF16,), extras=(u,), name="d_mlp_hidden",
             epilogue=lambda acc, uu: (acc * (2.0 * jnp.maximum(uu.astype(F32), 0.0)),))
    dw_down = _mm(a, dx2, ta=True, out_dtypes=(BF16,), name="d_w_down")
    dw_up = _mm(h2, du, ta=True, out_dtypes=(BF16,), name="d_w_up")
    dh2 = _mm(du, w_up, tb=True, name="d_h_mlp")
    dx1, dg_mlp = _rms_bwd(x1, w["g_mlp"], dh2, dx2, "rms_mlp_bwd")
    dmixed = _mm(dx1, w_out, tb=True, out_dtypes=(BF16,), name="d_mixed")
    dw_out = _mm(mixed, dx1, ta=True, out_dtypes=(BF16,), name="d_w_out")

    d_rows = jnp.concatenate([dw_out.reshape(N_DEV, ROWS_OUT, D_MODEL), dw_pg.reshape(N_DEV, ROWS_OUT, D_MODEL),
                              dw_down.reshape(N_DEV, D_FF // N_DEV, D_MODEL)], axis=1)
    (dact, dza, dzb_, dgz, dgb, dal, ddt, ddn, recv_rows) = _dn_bwd(
        act, zb, zs, w["alog_b"], w["dtb_b"], w["dn_norm"], dn_states, dn_t_invs, dmixed, bsz, nc,
        rider=_scatter_rider(d_rows))
    (gdq, gdk, gdv, dgg, dga, dlr, dw2, dgla_b, dgla_norm, recv_up) = _gla_bwd(
        zb, zs, w["w2p"], w["gla_b"], w["gla_norm"], gla_states, dmixed, bsz, nc,
        rider=_scatter_rider(_by_cols(dw_up)))
    dqkv, dconv = _conv_bwd(zb, w["conv"], dact, bsz, seq)
    dzb = jnp.concatenate([gdq, gdk, gdv, dgg, dqkv, dgz, dga, dgb], axis=1)
    dzs = jnp.concatenate([dlr, dza, dzb_], axis=1)
    dw_in_t = _mm(dzb, h, ta=True, out_dtypes=(BF16,), out_is_w_in_t=True, name="d_w_in")
    dws_t = _mm(dzs, h, ta=True, out_dtypes=(BF16,), name="d_w_in_gates")
    by_chip = _fill_gate_rows(dw_in_t, dws_t).reshape(N_DEV // 2, 2, D_IN_SHARD, D_MODEL)
    from_sibling = _exchange(_pair_rider(by_chip), "pair_d_w_in")
    mine = lax.dynamic_index_in_dim(by_chip, lax.axis_index("c"), axis=1, keepdims=False)
    chip_sums = _add_blocks(mine, from_sibling)
    dh_gates = _mm(dzs, w["ws_t"], name="d_h_mix_gates")
    dh, recv_in = _mm(dzb, w["w_in_t"], b_is_w_in_t=True, epilogue=_add_residual, extras=(dh_gates,), name="d_h_mix",
                      rider=_chip_scatter_rider(chip_sums))
    gx, dg_mix = _rms_bwd(x0, w["g_mix"], dh, dx1, "rms_mix_bwd")

    dgla_w2 = dw2[:, :GLA_LOWRANK, :].transpose(1, 0, 2).reshape(GLA_LOWRANK, GLA_QK)
    return dict(
        loss=loss[0, 0], grad_x=gx.reshape(x.shape), recv_in=recv_in, recv_up=recv_up, recv_rows=recv_rows,
        w_ple_proj=dw_pp,
        g_mix=dg_mix, gla_b=dgla_b.reshape(1, GLA_QK), gla_norm=dgla_norm, dn_norm=ddn, g_mlp=dg_mlp, g_ple=dg_ple,
        g_final=dg_final, gla_w2=dgla_w2, dn_conv=dconv,
        dn_a_log=dal[:, :DN_HEADS], dn_dt_bias=ddt[:, :DN_HEADS])


def _first_weights(g_mix, w_in, gla_w2, gla_b, gla_norm, dn_conv, dn_a_log, dn_dt_bias, dn_norm, g_mlp, g_ple,
                   w_ple_proj, g_final):
    w_in_t = _all_gather(jnp.swapaxes(w_in[0], 0, 1).astype(BF16), "gather_w_in").reshape(D_IN, D_MODEL)
    w_pp = _all_gather(w_ple_proj[0].astype(BF16), "gather_w_ple_proj")
    small = _all_gather(_pack_rows([gla_w2[0], dn_conv[0]]), "gather_w_small")
    n_w2 = GLA_LOWRANK * GLA_QK // N_DEV // LANE
    n_cv = DN_CONV * DN_QKV // N_DEV // LANE
    w2 = small[:, :n_w2].reshape(N_DEV, GLA_LOWRANK, GLA_QK // N_DEV).transpose(1, 0, 2).reshape(GLA_LOWRANK, GLA_QK)
    conv = small[:, n_w2:n_w2 + n_cv].reshape(N_DEV, DN_CONV, DN_QKV // N_DEV).transpose(1, 0, 2).reshape(DN_CONV, DN_QKV)

    return dict(
        w_in_t=w_in_t, ws_t=_w_in_gate_rows(w_in_t), w_pp=_from_cols(w_pp),
        w2p=jnp.pad(w2, ((0, LANE - GLA_LOWRANK), (0, 0))), conv=conv,
        alog_b=jnp.pad(dn_a_log, ((0, 0), (0, LANE - DN_HEADS))),
        dtb_b=jnp.pad(dn_dt_bias, ((0, 0), (0, LANE - DN_HEADS))),
        g_mix=g_mix, gla_b=gla_b, gla_norm=gla_norm, dn_norm=dn_norm, g_mlp=g_mlp, g_ple=g_ple,
        g_final=g_final.reshape(1, D_MODEL))


def kernel(x, p, g_mix, w_in, gla_w2, gla_b, gla_norm, dn_conv, dn_a_log, dn_dt_bias, dn_norm, w_out, g_mlp, w_up, w_down, g_ple, w_ple_gate, w_ple_proj, g_final, loss_target, m_g_mix, m_w_in, m_gla_w2, m_gla_b, m_gla_norm, m_dn_conv, m_dn_a_log, m_dn_dt_bias, m_dn_norm, m_w_out, m_g_mlp, m_w_up, m_w_down, m_g_ple, m_w_ple_gate, m_w_ple_proj, m_g_final, v_g_mix, v_w_in, v_gla_w2, v_gla_b, v_gla_norm, v_dn_conv, v_dn_a_log, v_dn_dt_bias, v_dn_norm, v_w_out, v_g_mlp, v_w_up, v_w_down, v_g_ple, v_w_ple_gate, v_w_ple_proj, v_g_final):
    names = ["g_mix", "w_in", "gla_w2", "gla_b", "gla_norm", "dn_conv", "dn_a_log", "dn_dt_bias", "dn_norm", "w_out",
             "g_mlp", "w_up", "w_down", "g_ple", "w_ple_gate", "w_ple_proj", "g_final"]
    ws = dict(zip(names, (g_mix, w_in, gla_w2, gla_b, gla_norm, dn_conv, dn_a_log, dn_dt_bias, dn_norm, w_out, g_mlp,
                          w_up, w_down, g_ple, w_ple_gate, w_ple_proj, g_final)))
    ms = dict(zip(names, (m_g_mix, m_w_in, m_gla_w2, m_gla_b, m_gla_norm, m_dn_conv, m_dn_a_log, m_dn_dt_bias,
                          m_dn_norm, m_w_out, m_g_mlp, m_w_up, m_w_down, m_g_ple, m_w_ple_gate, m_w_ple_proj,
                          m_g_final)))
    vs = dict(zip(names, (v_g_mix, v_w_in, v_gla_w2, v_gla_b, v_gla_norm, v_dn_conv, v_dn_a_log, v_dn_dt_bias,
                          v_dn_norm, v_w_out, v_g_mlp, v_w_up, v_w_down, v_g_ple, v_w_ple_gate, v_w_ple_proj,
                          v_g_final)))
    me = 4 * lax.axis_index("x") + 2 * lax.axis_index("y") + lax.axis_index("c")

    first = _first_weights(g_mix, w_in, gla_w2, gla_b, gla_norm, dn_conv, dn_a_log, dn_dt_bias, dn_norm, g_mlp,
                           g_ple, w_ple_proj, g_final)
    rows_shard = jnp.concatenate([w_out[0], w_ple_gate[0], w_down[0]], axis=0).astype(BF16)
    r = _local_step(x, p[0], loss_target, first, w_up[0].astype(BF16), rows_shard)
    loss = lax.psum(r["loss"], ("x", "y", "c"))

    grads, deltas, new_m, new_v = {}, {}, {}, {}

    def big(name, parts, row0=0):
        g, d, nm, nv = _adamw_reduce(ws[name][0], ms[name][0], vs[name][0], parts, row0, "adamw_" + name)
        grads[name], deltas[name], new_m[name], new_v[name] = g[None], d[None], nm[None], nv[None]

    t_outs = _adamw_reduce(*[jnp.swapaxes(d["w_in"][0], 0, 1) for d in (ws, ms, vs)], r["recv_in"], 0, "adamw_w_in")
    grads["w_in"], deltas["w_in"], new_m["w_in"], new_v["w_in"] = [jnp.swapaxes(o, 0, 1)[None] for o in t_outs]
    big("w_up", r["recv_up"])
    big("w_ple_proj", _all_to_all(_by_cols(r["w_ple_proj"]), "scatter_d_w_ple_proj"))
    big("w_out", r["recv_rows"], 0)
    big("w_ple_gate", r["recv_rows"], ROWS_OUT)
    big("w_down", r["recv_rows"], 2 * ROWS_OUT)

    vec_names = ["g_mix", "gla_b", "gla_norm", "dn_norm", "g_mlp", "g_ple", "g_final", "dn_a_log", "dn_dt_bias"]
    gathered_names = vec_names + ["gla_w2", "dn_conv"]
    total = _small_reduce(_all_gather(_pack_rows([r[n] for n in gathered_names]), "gather_small_grads"))
    sg = dict(zip(gathered_names, _unpack_rows(total, [r[n].shape for n in gathered_names])))
    sg["g_final"] = sg["g_final"].reshape(D_MODEL)
    sg["gla_w2"] = lax.dynamic_slice_in_dim(sg["gla_w2"], me * (GLA_QK // N_DEV), GLA_QK // N_DEV, axis=1)
    sg["dn_conv"] = lax.dynamic_slice_in_dim(sg["dn_conv"], me * (DN_QKV // N_DEV), DN_QKV // N_DEV, axis=1)
    small_names = gathered_names
    shapes = [ws[n].shape for n in small_names]
    d_s, m_s, v_s = _adamw_small(_pack_rows([ws[n] for n in small_names]), _pack_rows([ms[n] for n in small_names]),
                                 _pack_rows([vs[n] for n in small_names]), _pack_rows([sg[n] for n in small_names]))
    for n, d, nm, nv in zip(small_names, _unpack_rows(d_s, shapes), _unpack_rows(m_s, shapes), _unpack_rows(v_s, shapes)):
        grads[n], deltas[n], new_m[n], new_v[n] = sg[n].reshape(ws[n].shape), d, nm, nv

    return (loss, r["grad_x"], *[grads[n] for n in names], *[deltas[n] for n in names],
            *[new_m[n] for n in names], *[new_v[n] for n in names])
```

```python
import functools
import math

import jax
import jax.numpy as jnp
from jax import lax
from jax.experimental import pallas as pl
from jax.experimental.pallas import tpu as pltpu

F32 = jnp.float32
BF16 = jnp.bfloat16

N_DEV = 8
D_MODEL = 2048
CHUNK = 64
PLE_DIM = 256
EPS = 1e-6
GLA_HEADS = 4
GLA_DK = 256
GLA_DV = 512
GLA_LOWRANK = 16
GLA_TAU = 16.0
DN_HEADS = 16
DN_D = 128
DN_CONV = 4
D_FF = 4 * D_MODEL
GLA_QK = GLA_HEADS * GLA_DK
GLA_V = GLA_HEADS * GLA_DV
DN_QKV = 3 * DN_HEADS * DN_D
D_IN = 2 * GLA_QK + 2 * GLA_V + GLA_LOWRANK + DN_QKV + D_MODEL + 2 * DN_HEADS + 2 * D_MODEL
D_IN_SHARD = D_IN // N_DEV

ADAM_LR = 0.001
ADAM_B1 = 0.9
ADAM_B2 = 0.999
ADAM_EPS = 1e-08
ADAM_WD = 0.01
ADAM_STEP = 10

LANE = 128
ZB_GQ, ZB_GK, ZB_GV, ZB_GG = 0, 1024, 2048, 4096
ZB_DQKV, ZB_DZ, ZB_GA, ZB_GB = 6144, 12288, 14336, 16384
ZB_W = 18432
ZS_LR, ZS_DA, ZS_DB = 0, 128, 256
ZS_W = 384
WI_LR = 2 * GLA_QK + 2 * GLA_V
WI_DQKV = WI_LR + GLA_LOWRANK
WI_DA = WI_DQKV + DN_QKV + D_MODEL
WI_DB = WI_DA + DN_HEADS
WI_GA = WI_DB + DN_HEADS

VMEM_LIMIT = 56 * 1024 * 1024

def _bdot(a, b, dims):
    return lax.dot_general(a.astype(BF16), b.astype(BF16), dims, preferred_element_type=F32)


def _split3(x):
    hi = x.astype(BF16)
    rest = x - hi.astype(F32)
    mid = rest.astype(BF16)
    return hi, mid, (rest - mid.astype(F32)).astype(BF16)


def _dot01(m, x, dims):
    m = m.astype(BF16)
    out = None
    for piece in _split3(x):
        d = lax.dot_general(m, piece, dims, preferred_element_type=F32)
        out = d if out is None else out + d
    return out


@functools.partial(jax.custom_vjp, nondiff_argnums=(2, 3))
def _left_dot(m, x, dims, dims_t):
    return _dot01(m, x, dims)


def _left_dot_fwd(m, x, dims, dims_t):
    return _dot01(m, x, dims), m


def _left_dot_bwd(dims, dims_t, m, ct):
    return jnp.zeros_like(m), _dot01(m, ct, dims_t)


_left_dot.defvjp(_left_dot_fwd, _left_dot_bwd)


def _dot3(a, b, dims):
    ah, bh = a.astype(BF16), b.astype(BF16)
    al, bl = (a - ah.astype(F32)).astype(BF16), (b - bh.astype(F32)).astype(BF16)
    dot = functools.partial(lax.dot_general, dimension_numbers=dims, preferred_element_type=F32)
    return dot(ah, bh) + (dot(ah, bl) + dot(al, bh))


def _sigmoid(x):
    return 1.0 / (1.0 + jnp.exp(-x))


def _silu(x):
    return x * _sigmoid(x)


def _softplus(x):
    return jnp.maximum(x, 0.0) + jnp.log(1.0 + jnp.exp(-jnp.abs(x)))


def _iota2(shape, dim):
    return lax.broadcasted_iota(jnp.int32, shape, dim)


def _cparams(sem=None):
    return pltpu.CompilerParams(dimension_semantics=sem, vmem_limit_bytes=VMEM_LIMIT)


BNN = (((2,), (1,)), ((0,), (0,)))
BNT = (((2,), (2,)), ((0,), (0,)))
BTN = (((1,), (1,)), ((0,), (0,)))


def _gla_chunk(st, q, k, v, lr, w2, b, gg, ga, gn):
    hb, c, _ = q.shape
    incl = (_iota2((c, c), 0) >= _iota2((c, c), 1))[None]
    tri = jnp.broadcast_to(incl.astype(F32), (hb, c, c))
    q = q.astype(F32) * (GLA_DK ** -0.5)
    k = k.astype(F32)
    v = v.astype(F32)
    lr_b = jnp.broadcast_to(lr[None], (hb,) + lr.shape)
    lf = -_softplus(-(_bdot(lr_b, w2, BNN) + b)) / GLA_TAU
    bcum = _left_dot(tri, lf, BNN, BTN)
    blast = jnp.sum(lf, axis=1, keepdims=True)
    q_in = q * jnp.exp(bcum)
    k_in = k * jnp.exp(-bcum)
    a = jnp.where(incl, _bdot(q_in, k_in, BNT), 0.0)
    o = _bdot(a, v, BNN) + _bdot(q_in, st, BNT)
    k_dec = k * jnp.exp(blast - bcum)
    st_new = st * jnp.exp(blast) + _bdot(v, k_dec, BTN)
    on = o * lax.rsqrt(jnp.mean(o * o, axis=-1, keepdims=True) + EPS) * gn
    res = _sigmoid(ga.astype(F32)) * on * _silu(gg.astype(F32))
    return res, st_new


def _tri_inv_raw(a):
    _, c, _ = a.shape
    eye = (_iota2((c, c), 0) == _iota2((c, c), 1)).astype(F32)[None]
    x = a
    p = eye - a
    for _ in range(5):
        x = _dot3(x, x, BNN)
        p = _dot3(p, eye + x, BNN)
    return p


def _tri_inv_bwd(t, dt):
    return (-_dot3(_dot3(t, dt, BTN), t, BNT),)


@jax.custom_vjp
def _tri_inv_given(a, t):
    return t


def _tri_inv_given_fwd(a, t):
    return t, t


def _tri_inv_given_bwd(t, dt):
    return _tri_inv_bwd(t, dt) + (jnp.zeros_like(t),)


_tri_inv_given.defvjp(_tri_inv_given_fwd, _tri_inv_given_bwd)


@functools.partial(jax.custom_vjp, nondiff_argnums=(1, 2))
def _column_on_lanes(z, j, width):
    picked = jnp.where(_iota2(z.shape, 1) == j, z, 0.0)
    return jnp.broadcast_to(jnp.sum(picked, axis=1, keepdims=True), (z.shape[0], width))


def _column_on_lanes_fwd(z, j, width):
    return _column_on_lanes(z, j, width), None


def _column_on_lanes_bwd(j, width, _, ct):
    shape = (ct.shape[0], LANE)
    total = jnp.broadcast_to(jnp.sum(ct, axis=1, keepdims=True), shape)
    return (jnp.where(_iota2(shape, 1) == j, total, 0.0),)


_column_on_lanes.defvjp(_column_on_lanes_fwd, _column_on_lanes_bwd)


def _dn_chunk(s, qr, kr, vr, za, zb, alog, dtb, gz, gb, dn, t_saved=None):
    hb, c, _ = qr.shape
    row, col = _iota2((c, c), 0), _iota2((c, c), 1)
    incl = (row >= col)[None]
    strict = (row > col)[None]
    tri = jnp.broadcast_to(incl.astype(F32), (hb, c, c))

    def l2n(t):
        return t * lax.rsqrt(jnp.sum(t * t, axis=-1, keepdims=True) + EPS)

    q = l2n(qr.astype(F32)) * (DN_D ** -0.5)
    k = l2n(kr.astype(F32))
    v = vr.astype(F32)
    g_heads = -jnp.exp(alog) * _softplus(za + dtb)
    beta_heads = _sigmoid(zb)
    def per_head(z, width):
        return jnp.concatenate([_column_on_lanes(z, j, width)[None] for j in range(hb)], axis=0)

    g = per_head(g_heads, LANE)
    beta = per_head(beta_heads, LANE)
    gcum = _left_dot(tri, g, BNN, BTN)
    glast = jnp.sum(g, axis=1, keepdims=True)
    diff = _left_dot(tri, per_head(g_heads, c) * strict.astype(F32), BNN, BTN)
    dec = jnp.exp(jnp.where(incl, diff, -1e30))
    kb = k * beta
    a = jnp.where(strict, _bdot(kb, k, BNT) * dec, 0.0)
    t = _tri_inv_raw(a) if t_saved is None else _tri_inv_given(a, t_saved)
    egc = jnp.exp(gcum)
    u = _bdot(t, v * beta, BNN)
    w = _bdot(t, kb * egc, BNN)
    attn = jnp.where(incl, _bdot(q, k, BNT) * dec, 0.0)
    q_dec = q * egc
    k_dec = k * jnp.exp(glast - gcum)
    v_new = u - _bdot(w, s, BNN)
    o = _bdot(q_dec, s, BNN) + _bdot(attn, v_new, BNN)
    s_new = s * jnp.exp(glast) + _bdot(k_dec, v_new, BTN)
    on = o * lax.rsqrt(jnp.mean(o * o, axis=-1, keepdims=True) + EPS) * dn
    res = _sigmoid(gb.astype(F32)) * on * _silu(gz.astype(F32))
    return (res, s_new, t) if t_saved is None else (res, s_new)


def _heads(ref, n_heads, width):
    return jnp.stack([ref[:, j * width:(j + 1) * width] for j in range(n_heads)], axis=0)


def _gla_specs(nc, reverse):
    def rows(b, n):
        return b * nc + ((nc - 1 - n) if reverse else n)

    qk = lambda base: pl.BlockSpec((CHUNK, GLA_QK), lambda b, n: (rows(b, n), base // GLA_QK))
    vv = lambda base: pl.BlockSpec((CHUNK, GLA_V), lambda b, n: (rows(b, n), base // GLA_V))
    lr = lambda c: pl.BlockSpec((CHUNK, LANE), lambda b, n: (rows(b, n), c))
    full = lambda shape: pl.BlockSpec(shape, lambda b, n: (0,) * len(shape))
    return rows, qk, vv, lr, full


def _gla_inputs(q_ref, k_ref, v_ref, gg_ref, ga_ref, lr_ref, w2_ref, b_ref, gn_ref):
    return (_heads(q_ref, GLA_HEADS, GLA_DK), _heads(k_ref, GLA_HEADS, GLA_DK), _heads(v_ref, GLA_HEADS, GLA_DV),
            lr_ref[...], _heads(w2_ref, GLA_HEADS, GLA_DK), _heads(b_ref, GLA_HEADS, GLA_DK),
            _heads(gg_ref, GLA_HEADS, GLA_DV), _heads(ga_ref, GLA_HEADS, GLA_DV), gn_ref[...])


def _gla_fwd(zb, zs, w2p, gla_b, gla_norm, bsz, nc):
    t = zb.shape[0]
    rows, qk, vv, lr, full = _gla_specs(nc, False)

    def body(q_ref, k_ref, v_ref, gg_ref, ga_ref, lr_ref, w2_ref, b_ref, gn_ref, o_ref, st_ref, state):
        st = jnp.where(pl.program_id(1) > 0, state[...], 0.0)
        st_ref[0, 0] = st
        res, st_new = _gla_chunk(st, *_gla_inputs(q_ref, k_ref, v_ref, gg_ref, ga_ref, lr_ref, w2_ref, b_ref, gn_ref))
        for j in range(GLA_HEADS):
            o_ref[:, j * GLA_DV:(j + 1) * GLA_DV] = res[j]
        state[...] = st_new

    return pl.pallas_call(
        body, name="gla_fwd", grid=(bsz, nc),
        in_specs=[qk(ZB_GQ), qk(ZB_GK), vv(ZB_GV), vv(ZB_GG), vv(ZB_GA), lr(ZS_LR // LANE),
                  full((LANE, GLA_QK)), full((1, GLA_QK)), full((1, GLA_DV))],
        out_specs=[vv(0), pl.BlockSpec((1, 1, GLA_HEADS, GLA_DV, GLA_DK), lambda b, n: (b, n, 0, 0, 0))],
        out_shape=[jax.ShapeDtypeStruct((t, GLA_V), F32),
                   jax.ShapeDtypeStruct((bsz, nc, GLA_HEADS, GLA_DV, GLA_DK), F32)],
        scratch_shapes=[pltpu.VMEM((GLA_HEADS, GLA_DV, GLA_DK), F32)],
        compiler_params=_cparams(("arbitrary", "arbitrary")),
    )(zb, zb, zb, zb, zb, zs, w2p, gla_b, gla_norm)


def _gla_bwd(zb, zs, w2p, gla_b, gla_norm, states, dmixed, bsz, nc, rider=None):
    t = zb.shape[0]
    rows, qk, vv, lr, full = _gla_specs(nc, True)

    def body(q_ref, k_ref, v_ref, gg_ref, ga_ref, lr_ref, w2_ref, b_ref, gn_ref, st_ref, dm_ref,
             dq_ref, dk_ref, dv_ref, dgg_ref, dga_ref, dlr_ref, dw2_ref, db_ref, dgn_ref, dstate):
        b, n = pl.program_id(0), pl.program_id(1)

        @pl.when((b == 0) & (n == 0))
        def _():
            dw2_ref[...] = jnp.zeros_like(dw2_ref)
            db_ref[...] = jnp.zeros_like(db_ref)
            dgn_ref[...] = jnp.zeros_like(dgn_ref)

        _, vjp = jax.vjp(_gla_chunk, st_ref[0, 0],
                         *_gla_inputs(q_ref, k_ref, v_ref, gg_ref, ga_ref, lr_ref, w2_ref, b_ref, gn_ref))
        dst_in = jnp.where(n > 0, dstate[...], 0.0)
        dst, dq, dk, dv, dlr, dw2, db, dgg, dga, dgn = vjp((_heads(dm_ref, GLA_HEADS, GLA_DV).astype(F32), dst_in))
        dstate[...] = dst
        for j in range(GLA_HEADS):
            dq_ref[:, j * GLA_DK:(j + 1) * GLA_DK] = dq[j].astype(dq_ref.dtype)
            dk_ref[:, j * GLA_DK:(j + 1) * GLA_DK] = dk[j].astype(dk_ref.dtype)
            dv_ref[:, j * GLA_DV:(j + 1) * GLA_DV] = dv[j].astype(dv_ref.dtype)
            dgg_ref[:, j * GLA_DV:(j + 1) * GLA_DV] = dgg[j].astype(dgg_ref.dtype)
            dga_ref[:, j * GLA_DV:(j + 1) * GLA_DV] = dga[j].astype(dga_ref.dtype)
        dlr_ref[...] = dlr
        dw2_ref[...] += dw2
        db_ref[...] += db
        dgn_ref[...] += dgn

    return _hosted_call(
        body, rider, name="gla_bwd", grid=(bsz, nc),
        in_specs=[qk(ZB_GQ), qk(ZB_GK), vv(ZB_GV), vv(ZB_GG), vv(ZB_GA), lr(ZS_LR // LANE),
                  full((LANE, GLA_QK)), full((1, GLA_QK)), full((1, GLA_DV)),
                  pl.BlockSpec((1, 1, GLA_HEADS, GLA_DV, GLA_DK), lambda b, n: (b, nc - 1 - n, 0, 0, 0)),
                  vv(0)],
        out_specs=[qk(0), qk(0), vv(0), vv(0), vv(0), lr(0),
                   full((GLA_HEADS, LANE, GLA_DK)), full((GLA_HEADS, 1, GLA_DK)), full((1, GLA_DV))],
        out_shape=[jax.ShapeDtypeStruct((t, GLA_QK), BF16), jax.ShapeDtypeStruct((t, GLA_QK), BF16),
                   jax.ShapeDtypeStruct((t, GLA_V), BF16), jax.ShapeDtypeStruct((t, GLA_V), BF16),
                   jax.ShapeDtypeStruct((t, GLA_V), BF16), jax.ShapeDtypeStruct((t, LANE), F32),
                   jax.ShapeDtypeStruct((GLA_HEADS, LANE, GLA_DK), F32),
                   jax.ShapeDtypeStruct((GLA_HEADS, 1, GLA_DK), F32),
                   jax.ShapeDtypeStruct((1, GLA_DV), F32)],
        scratch_shapes=[pltpu.VMEM((GLA_HEADS, GLA_DV, GLA_DK), F32)],
        args=(zb, zb, zb, zb, zb, zs, w2p, gla_b, gla_norm, states, dmixed))


DN_HB = DN_HEADS


def _dn_specs(nc, reverse):
    wide = DN_HB * DN_D

    def rows(b, n, h):
        return b * nc + ((nc - 1 - n) if reverse else n)

    def col(base):
        return pl.BlockSpec((CHUNK, wide), lambda b, n, h: (rows(b, n, h), base // wide + h))

    def fixed(c):
        return pl.BlockSpec((CHUNK, LANE), lambda b, n, h: (rows(b, n, h), c))

    head = pl.BlockSpec((1, LANE), lambda b, n, h: (0, 0))
    return rows, col, fixed, head


def _lanes(j):
    return slice(j * DN_D, (j + 1) * DN_D)


def _by_head(ref):
    return jnp.stack([ref[:, _lanes(j)] for j in range(DN_HB)], axis=0)


def _dn_fwd(act, zb, zs, alog_b, dtb_b, dn_norm, mix_gla, bsz, nc, rider=None):
    t = zb.shape[0]
    rows, col, fixed, head = _dn_specs(nc, False)

    def body(q_ref, k_ref, v_ref, za_ref, zb_ref, al_ref, dt_ref, gz_ref, gb_ref, dn_ref, mg_ref,
             o_ref, st_ref, ti_ref, state):
        s = jnp.where(pl.program_id(1) > 0, state[...], 0.0)
        st_ref[0, 0] = s
        res, s_new, t_inv = _dn_chunk(s, _by_head(q_ref), _by_head(k_ref), _by_head(v_ref), za_ref[...], zb_ref[...],
                                      al_ref[...], dt_ref[...], _by_head(gz_ref), _by_head(gb_ref), dn_ref[...])
        ti_ref[0, 0] = t_inv
        for j in range(DN_HB):
            o_ref[:, _lanes(j)] = (res[j] + mg_ref[:, _lanes(j)]).astype(o_ref.dtype)
        state[...] = s_new

    return _hosted_call(
        body, rider, name="dn_fwd", grid=(bsz, nc, DN_HEADS // DN_HB),
        in_specs=[col(0), col(DN_HEADS * DN_D), col(2 * DN_HEADS * DN_D),
                  fixed(ZS_DA // LANE), fixed(ZS_DB // LANE), head, head,
                  col(ZB_DZ), col(ZB_GB), pl.BlockSpec((1, DN_D), lambda b, n, h: (0, 0)), col(0)],
        out_specs=[col(0), pl.BlockSpec((1, 1, DN_HB, DN_D, DN_D), lambda b, n, h: (b, n, h, 0, 0)),
                   pl.BlockSpec((1, 1, DN_HB, CHUNK, CHUNK), lambda b, n, h: (b, n, h, 0, 0))],
        out_shape=[jax.ShapeDtypeStruct((t, D_MODEL), BF16),
                   jax.ShapeDtypeStruct((bsz, nc, DN_HEADS, DN_D, DN_D), F32),
                   jax.ShapeDtypeStruct((bsz, nc, DN_HEADS, CHUNK, CHUNK), F32)],
        scratch_shapes=[pltpu.VMEM((DN_HEADS, DN_D, DN_D), F32)],
        args=(act, act, act, zs, zs, alog_b, dtb_b, zb, zb, dn_norm, mix_gla))


def _dn_bwd(act, zb, zs, alog_b, dtb_b, dn_norm, states, t_invs, dmixed, bsz, nc, rider=None):
    t = zb.shape[0]
    rows, col, fixed, head = _dn_specs(nc, True)

    def body(q_ref, k_ref, v_ref, za_ref, zb_ref, al_ref, dt_ref, gz_ref, gb_ref, dn_ref, st_ref, ti_ref, dm_ref,
             dact_ref, dza_ref, dzb_ref, dgz_ref, dgb_ref, dal_ref, ddt_ref, ddn_ref, dstate):
        b, n = pl.program_id(0), pl.program_id(1)

        @pl.when((b == 0) & (n == 0))
        def _():
            dal_ref[...] = jnp.zeros_like(dal_ref)
            ddt_ref[...] = jnp.zeros_like(ddt_ref)
            ddn_ref[...] = jnp.zeros_like(ddn_ref)

        fn = functools.partial(_dn_chunk, t_saved=ti_ref[0, 0])
        _, vjp = jax.vjp(fn, st_ref[0, 0], _by_head(q_ref), _by_head(k_ref), _by_head(v_ref), za_ref[...],
                         zb_ref[...], al_ref[...], dt_ref[...], _by_head(gz_ref), _by_head(gb_ref), dn_ref[...])
        ds_in = jnp.where(n > 0, dstate[...], 0.0)
        ds, dq, dk, dv, dza, dzb, dal, ddt, dgz, dgb, ddn = vjp((_by_head(dm_ref).astype(F32), ds_in))
        dstate[...] = ds
        for j in range(DN_HB):
            for part, d in enumerate((dq, dk, dv)):
                dact_ref[:, pl.ds(part * DN_HEADS * DN_D + j * DN_D, DN_D)] = d[j]
            dgz_ref[:, _lanes(j)] = dgz[j].astype(dgz_ref.dtype)
            dgb_ref[:, _lanes(j)] = dgb[j].astype(dgb_ref.dtype)
        dal_ref[...] += dal
        ddt_ref[...] += ddt
        dza_ref[...] = dza
        dzb_ref[...] = dzb
        ddn_ref[...] += ddn

    full = lambda shape: pl.BlockSpec(shape, lambda b, n, h: (0,) * len(shape))
    return _hosted_call(
        body, rider, name="dn_bwd", grid=(bsz, nc, DN_HEADS // DN_HB),
        in_specs=[col(0), col(DN_HEADS * DN_D), col(2 * DN_HEADS * DN_D),
                  fixed(ZS_DA // LANE), fixed(ZS_DB // LANE), head, head,
                  col(ZB_DZ), col(ZB_GB), pl.BlockSpec((1, DN_D), lambda b, n, h: (0, 0)),
                  pl.BlockSpec((1, 1, DN_HB, DN_D, DN_D), lambda b, n, h: (b, nc - 1 - n, h, 0, 0)),
                  pl.BlockSpec((1, 1, DN_HB, CHUNK, CHUNK), lambda b, n, h: (b, nc - 1 - n, h, 0, 0)), col(0)],
        out_specs=[pl.BlockSpec((CHUNK, DN_QKV), lambda b, n, h: (rows(b, n, h), 0)), fixed(0), fixed(0), col(0), col(0),
                   full((1, LANE)), full((1, LANE)), full((1, DN_D))],
        out_shape=[jax.ShapeDtypeStruct((t, DN_QKV), F32),
                   jax.ShapeDtypeStruct((t, LANE), F32), jax.ShapeDtypeStruct((t, LANE), F32),
                   jax.ShapeDtypeStruct((t, D_MODEL), BF16), jax.ShapeDtypeStruct((t, D_MODEL), BF16),
                   jax.ShapeDtypeStruct((1, LANE), F32), jax.ShapeDtypeStruct((1, LANE), F32),
                   jax.ShapeDtypeStruct((1, DN_D), F32)],
        scratch_shapes=[pltpu.VMEM((DN_HEADS, DN_D, DN_D), F32)],
        args=(act, act, act, zs, zs, alog_b, dtb_b, zb, zb, dn_norm, states, t_invs, dmixed))


MM_VMEM_BUDGET = 40 * 1024 * 1024
MM_TILE_PREF = (1024, 1024, 2048)


def _divisor_tile(n, cap):
    if n <= cap:
        return n
    for c in range(cap - cap % LANE, 0, -LANE):
        if n % c == 0:
            return c
    return n


def _mm_tiles(m, n, kd, a_bytes, b_bytes, mn_bytes):
    tm, tn, tk = (_divisor_tile(d, c) for d, c in zip((m, n, kd), MM_TILE_PREF))

    def need(tm, tn, tk):
        acc = 0 if tk == kd else 4 * tm * tn
        return 2 * (tm * tk * a_bytes + tk * tn * b_bytes + tm * tn * mn_bytes) + acc + 4 * tm * tn

    while need(tm, tn, tk) > MM_VMEM_BUDGET:
        if tk > 512 and tk * max(tm * a_bytes, tn * b_bytes) >= tm * tn * mn_bytes:
            tk = _divisor_tile(kd, tk // 2)
        elif tn >= tm and tn > LANE:
            tn = _divisor_tile(n, tn // 2)
        else:
            tm = _divisor_tile(m, tm // 2)
    return tm, tn, tk


def _w_in_row_of(tile, tile_rows):
    skipped = jnp.where(tile >= ZB_GA // tile_rows, WI_GA - ZB_GA, jnp.where(tile >= ZB_DQKV // tile_rows,
                                                                             WI_DQKV - ZB_DQKV, 0))
    return pl.multiple_of(tile * tile_rows + skipped, 16)


def _mm(a, b, *, ta=False, tb=False, out_dtypes=(F32,), epilogue=None, extras=(), name, rider=None,
        b_is_w_in_t=False, out_is_w_in_t=False):
    m, kd = (a.shape[1], a.shape[0]) if ta else a.shape
    n = b.shape[0] if tb else b.shape[1]
    if b_is_w_in_t:
        n, kd = (ZB_W, kd) if tb else (n, ZB_W)
    mn_bytes = sum(e.dtype.itemsize for e in extras) + sum(jnp.dtype(dt).itemsize for dt in out_dtypes)
    tm, tn, tk = _mm_tiles(m, n, kd, a.dtype.itemsize, b.dtype.itemsize, mn_bytes)
    nk = kd // tk
    n_ex = len(extras)
    dims = (((0,) if ta else (1,), (1,) if tb else (0,)), ((), ()))

    def finish(acc, ex_refs, out_refs):
        outs = (acc,) if epilogue is None else epilogue(acc, *[r[...] for r in ex_refs])
        for r, o in zip(out_refs, outs):
            r[...] = o.astype(r.dtype)

    def partial_product(a_ref, b_ref):
        return lax.dot_general(a_ref[...].astype(BF16), b_ref[...].astype(BF16), dims, preferred_element_type=F32)

    def body_single(*refs):
        finish(partial_product(refs[0], refs[1]), refs[2:2 + n_ex], refs[2 + n_ex:])

    def body_acc(*refs):
        acc = refs[-1]
        k = pl.program_id(2)

        @pl.when(k == 0)
        def _():
            acc[...] = partial_product(refs[0], refs[1])

        @pl.when(k > 0)
        def _():
            acc[...] += partial_product(refs[0], refs[1])

        @pl.when(k == nk - 1)
        def _():
            finish(acc[...], refs[2:2 + n_ex], refs[2 + n_ex:-1])

    a_spec = pl.BlockSpec((tk, tm), lambda i, j, k: (k, i)) if ta else pl.BlockSpec((tm, tk), lambda i, j, k: (i, k))
    b_spec = pl.BlockSpec((tn, tk), lambda i, j, k: (j, k)) if tb else pl.BlockSpec((tk, tn), lambda i, j, k: (k, j))
    mn_spec = pl.BlockSpec((tm, tn), lambda i, j, k: (i, j))
    out_spec, out_rows = mn_spec, m
    if b_is_w_in_t and tb:
        b_spec = pl.BlockSpec((pl.Element(tn), pl.Element(tk)),
                              lambda i, j, k: (_w_in_row_of(j, tn), pl.multiple_of(k * tk, LANE)))
    elif b_is_w_in_t:
        b_spec = pl.BlockSpec((pl.Element(tk), pl.Element(tn)),
                              lambda i, j, k: (_w_in_row_of(k, tk), pl.multiple_of(j * tn, LANE)))
    if out_is_w_in_t:
        out_spec, out_rows = pl.BlockSpec((pl.Element(tm), pl.Element(tn)),
                                          lambda i, j, k: (_w_in_row_of(i, tm), pl.multiple_of(j * tn, LANE))), D_IN
    outs = _hosted_call(
        body_single if nk == 1 else body_acc, rider, name=name, grid=(m // tm, n // tn, nk),
        in_specs=[a_spec, b_spec] + [mn_spec] * n_ex,
        out_specs=[out_spec] * len(out_dtypes),
        out_shape=[jax.ShapeDtypeStruct((out_rows, n), dt) for dt in out_dtypes],
        scratch_shapes=[] if nk == 1 else [pltpu.VMEM((tm, tn), F32)],
        args=(a, b, *extras))
    return outs[0] if len(outs) == 1 else outs


ROW_BLOCK = 256


def _row_spec(width=D_MODEL):
    return pl.BlockSpec((ROW_BLOCK, width), lambda i: (i, 0))


def _vec_spec(width=D_MODEL):
    return pl.BlockSpec((1, width), lambda i: (0, 0))


def _rms_fwd(x, g, name):
    def body(x_ref, g_ref, h_ref):
        xf = x_ref[...]
        h_ref[...] = (xf * lax.rsqrt(jnp.mean(xf * xf, axis=-1, keepdims=True) + EPS) * g_ref[...]).astype(BF16)

    t = x.shape[0]
    return pl.pallas_call(
        body, name=name, grid=(t // ROW_BLOCK,), in_specs=[_row_spec(), _vec_spec()], out_specs=_row_spec(),
        out_shape=jax.ShapeDtypeStruct((t, D_MODEL), BF16), compiler_params=_cparams(("parallel",)),
    )(x, g)


def _rms_bwd_math(xf, g, dh):
    rstd = lax.rsqrt(jnp.mean(xf * xf, axis=-1, keepdims=True) + EPS)
    xhat = xf * rstd
    dxhat = dh * g
    dx = rstd * (dxhat - xhat * jnp.mean(dxhat * xhat, axis=-1, keepdims=True))
    dg = jnp.sum(dh * xhat, axis=0, keepdims=True)
    return dx, dg


def _rms_bwd(x, g, dh, dres, name):
    def body(x_ref, g_ref, dh_ref, dres_ref, dx_ref, dg_ref):
        dx, dg = _rms_bwd_math(x_ref[...], g_ref[...], dh_ref[...].astype(F32))
        dx_ref[...] = dres_ref[...] + dx

        @pl.when(pl.program_id(0) == 0)
        def _():
            dg_ref[...] = jnp.zeros_like(dg_ref)

        dg_ref[...] += dg

    t = x.shape[0]
    return pl.pallas_call(
        body, name=name, grid=(t // ROW_BLOCK,),
        in_specs=[_row_spec(), _vec_spec(), _row_spec(), _row_spec()], out_specs=[_row_spec(), _vec_spec()],
        out_shape=[jax.ShapeDtypeStruct((t, D_MODEL), F32), jax.ShapeDtypeStruct((1, D_MODEL), F32)],
        compiler_params=_cparams(("arbitrary",)),
    )(x, g, dh, dres)


def _loss_head(x3, g, target):
    def body(x_ref, g_ref, t_ref, dx_ref, dg_ref, loss_ref):
        xf, gg = x_ref[...], g_ref[...]
        rstd = lax.rsqrt(jnp.mean(xf * xf, axis=-1, keepdims=True) + EPS)
        err = xf * rstd * gg - t_ref[...]
        dx, dg = _rms_bwd_math(xf, gg, err * (1.0 / D_MODEL))
        dx_ref[...] = dx

        @pl.when(pl.program_id(0) == 0)
        def _():
            dg_ref[...] = jnp.zeros_like(dg_ref)
            loss_ref[...] = jnp.zeros_like(loss_ref)

        dg_ref[...] += dg
        part = jnp.sum(jnp.sum(err * err, axis=-1, keepdims=True), axis=0, keepdims=True) * (0.5 / D_MODEL)
        loss_ref[...] += jnp.broadcast_to(part, loss_ref.shape)

    t = x3.shape[0]
    return pl.pallas_call(
        body, name="loss_head", grid=(t // ROW_BLOCK,),
        in_specs=[_row_spec(), _vec_spec(), _row_spec()], out_specs=[_row_spec(), _vec_spec(), _vec_spec(LANE)],
        out_shape=[jax.ShapeDtypeStruct((t, D_MODEL), F32), jax.ShapeDtypeStruct((1, D_MODEL), F32),
                   jax.ShapeDtypeStruct((1, LANE), F32)],
        compiler_params=_cparams(("arbitrary",)),
    )(x3, g, target)


def _ple_bwd(dx3, gpre, pp):
    def body(dx_ref, gp_ref, pp_ref, dgp_ref, dpp_ref):
        dx, sg = dx_ref[...], _sigmoid(gp_ref[...])
        dpp_ref[...] = (dx * sg).astype(BF16)
        dgp_ref[...] = (dx * pp_ref[...] * sg * (1.0 - sg)).astype(BF16)

    t = dx3.shape[0]
    return pl.pallas_call(
        body, name="ple_bwd", grid=(t // ROW_BLOCK,), in_specs=[_row_spec()] * 3, out_specs=[_row_spec()] * 2,
        out_shape=[jax.ShapeDtypeStruct((t, D_MODEL), BF16)] * 2, compiler_params=_cparams(("parallel",)),
    )(dx3, gpre, pp)


CONV_COLS = 256


def _shift_down(x, s):
    if s == 0:
        return x
    return jnp.where(_iota2(x.shape, 0) >= s, pltpu.roll(x, s, 0), 0.0)


def _shift_up(x, s):
    if s == 0:
        return x
    rows = x.shape[0]
    return jnp.where(_iota2(x.shape, 0) < rows - s, pltpu.roll(x, rows - s, 0), 0.0)


def _conv_taps(xf):
    return [_shift_down(xf, DN_CONV - 1 - j) for j in range(DN_CONV)]


def _conv_pre(taps, w):
    return sum(tap * w[j:j + 1, :] for j, tap in enumerate(taps))


def _conv_fwd(zb, conv_w, bsz, seq):
    def body(x_ref, w_ref, y_ref):
        y_ref[...] = _silu(_conv_pre(_conv_taps(x_ref[...]), w_ref[...]))

    nblk = DN_QKV // CONV_COLS
    return pl.pallas_call(
        body, name="conv_fwd", grid=(bsz, nblk),
        in_specs=[pl.BlockSpec((seq, CONV_COLS), lambda b, j: (b, ZB_DQKV // CONV_COLS + j)),
                  pl.BlockSpec((DN_CONV, CONV_COLS), lambda b, j: (0, j))],
        out_specs=pl.BlockSpec((seq, CONV_COLS), lambda b, j: (b, j)),
        out_shape=jax.ShapeDtypeStruct((bsz * seq, DN_QKV), F32),
        compiler_params=_cparams(("parallel", "parallel")),
    )(zb, conv_w)


def _conv_bwd(zb, conv_w, dact, bsz, seq):
    def body(x_ref, w_ref, dy_ref, dx_ref, dw_ref):
        taps, w = _conv_taps(x_ref[...]), w_ref[...]
        c = _conv_pre(taps, w)
        sg = _sigmoid(c)
        dc = dy_ref[...].astype(F32) * sg * (1.0 + c * (1.0 - sg))
        dx = sum(_shift_up(dc, DN_CONV - 1 - j) * w[j:j + 1, :] for j in range(DN_CONV))
        dx_ref[...] = dx.astype(BF16)
        dw = jnp.concatenate([jnp.sum(dc * tap, axis=0, keepdims=True) for tap in taps], axis=0)

        @pl.when(pl.program_id(1) == 0)
        def _():
            dw_ref[...] = jnp.zeros_like(dw_ref)

        dw_ref[...] += dw

    nblk = DN_QKV // CONV_COLS
    return pl.pallas_call(
        body, name="conv_bwd", grid=(nblk, bsz),
        in_specs=[pl.BlockSpec((seq, CONV_COLS), lambda j, b: (b, ZB_DQKV // CONV_COLS + j)),
                  pl.BlockSpec((DN_CONV, CONV_COLS), lambda j, b: (0, j)),
                  pl.BlockSpec((seq, CONV_COLS), lambda j, b: (b, j))],
        out_specs=[pl.BlockSpec((seq, CONV_COLS), lambda j, b: (b, j)),
                   pl.BlockSpec((DN_CONV, CONV_COLS), lambda j, b: (0, j))],
        out_shape=[jax.ShapeDtypeStruct((bsz * seq, DN_QKV), BF16), jax.ShapeDtypeStruct((DN_CONV, DN_QKV), F32)],
        compiler_params=_cparams(("parallel", "arbitrary")),
    )(zb, conv_w, dact)


MESH_IDS = pl.DeviceIdType.MESH
ANY_SPEC = pl.BlockSpec(memory_space=pl.ANY)


COMM_SCRATCH = (pltpu.SemaphoreType.DMA((7,)), pltpu.SemaphoreType.DMA((7,)), pltpu.SemaphoreType.DMA)


def _gather_phases(x_ref, out_ref, send_sems, recv_sems, local_sem):
    mx, my, mc = lax.axis_index("x"), lax.axis_index("y"), lax.axis_index("c")
    me, sibling = (mx, my, mc), (mx, my, 1 - mc)
    x_nbr, y_nbr, diag = (1 - mx, my), (mx, 1 - my), (1 - mx, 1 - my)
    north = mc == 1
    relayed = (jnp.where(north, mx, 1 - mx), jnp.where(north, 1 - my, my))
    relay_to = (jnp.where(north, 1 - mx, mx), jnp.where(north, my, 1 - my))

    def slot(px, py, pc):
        return out_ref.at[4 * px + 2 * py + pc]

    def copy(k, block, to, src=None):
        return pltpu.make_async_remote_copy(
            src_ref=slot(*block) if src is None else src, dst_ref=slot(*block),
            send_sem=send_sems.at[k], recv_sem=recv_sems.at[k], device_id=to, device_id_type=MESH_IDS)

    def mine():
        return pltpu.make_async_copy(x_ref, slot(*me), local_sem)

    def own_sends():
        return [copy(0, me, sibling, src=x_ref), copy(1, me, (*x_nbr, mc), src=x_ref),
                copy(2, me, (*y_nbr, mc), src=x_ref)]

    def later_sends():
        return [copy(3, (*relayed, mc), (*relay_to, mc)), copy(4, (*x_nbr, mc), sibling),
                copy(5, (*y_nbr, mc), sibling), copy(6, (*diag, mc), sibling)]

    def start():
        mine().start()
        for cp in own_sends():
            cp.start()

    def forward():
        copy(1, (*x_nbr, mc), me).wait_recv()
        copy(2, (*y_nbr, mc), me).wait_recv()
        for cp in later_sends()[:3]:
            cp.start()

    def forward_late():
        copy(3, (*diag, mc), me).wait_recv()
        later_sends()[3].start()

    def finish():
        copy(0, sibling, me).wait_recv()
        for k, chip in ((4, x_nbr), (5, y_nbr), (6, diag)):
            copy(k, (*chip, 1 - mc), me).wait_recv()
        for cp in own_sends() + later_sends():
            cp.wait_send()
        mine().wait()

    return start, forward, forward_late, finish


def _scatter_phases(x_ref, out_ref, send_sems, recv_sems, local_sem, among_chips=False):
    mx, my, mc = lax.axis_index("x"), lax.axis_index("y"), lax.axis_index("c")
    n_peers = 4 if among_chips else N_DEV
    me = 2 * mx + my if among_chips else 4 * mx + 2 * my + mc

    def peer(k):
        if among_chips:
            return (mx ^ ((k >> 1) & 1), my ^ (k & 1), mc)
        return (mx ^ ((k >> 2) & 1), my ^ ((k >> 1) & 1), mc ^ (k & 1))

    def slot_of(k):
        px, py, pc = peer(k)
        return 2 * px + py if among_chips else 4 * px + 2 * py + pc

    def copy(k, src_slot, dst_slot):
        return pltpu.make_async_remote_copy(
            src_ref=x_ref.at[src_slot], dst_ref=out_ref.at[dst_slot],
            send_sem=send_sems.at[k - 1], recv_sem=recv_sems.at[k - 1],
            device_id=peer(k), device_id_type=MESH_IDS)

    def sends():
        return [copy(k, slot_of(k), me) for k in range(1, n_peers)]

    def mine():
        return pltpu.make_async_copy(x_ref.at[me], out_ref.at[me], local_sem)

    def start():
        mine().start()
        for cp in sends():
            cp.start()

    def finish():
        for k in range(1, n_peers):
            copy(k, me, slot_of(k)).wait_recv()
        for cp in sends():
            cp.wait_send()
        mine().wait()

    return start, _nothing, _nothing, finish


def _pair_phases(x_ref, out_ref, send_sems, recv_sems, local_sem):
    mx, my, mc = lax.axis_index("x"), lax.axis_index("y"), lax.axis_index("c")

    def copy(side):
        return pltpu.make_async_remote_copy(
            src_ref=x_ref.at[:, side], dst_ref=out_ref, send_sem=send_sems.at[0], recv_sem=recv_sems.at[0],
            device_id=(mx, my, 1 - mc), device_id_type=MESH_IDS)

    def start():
        copy(1 - mc).start()

    def finish():
        copy(mc).wait_recv()
        copy(1 - mc).wait_send()

    return start, _nothing, _nothing, finish


def _nothing():
    pass


class _Rider:
    def __init__(self, phases, x, out_shape):
        self.phases, self.x, self.out_shape = phases, x, out_shape


def _gather_rider(x):
    return _Rider(_gather_phases, x, jax.ShapeDtypeStruct((N_DEV,) + x.shape, x.dtype))


def _scatter_rider(x):
    return _Rider(_scatter_phases, x, jax.ShapeDtypeStruct(x.shape, x.dtype))


def _chip_scatter_rider(x):
    return _Rider(functools.partial(_scatter_phases, among_chips=True), x, jax.ShapeDtypeStruct(x.shape, x.dtype))


def _pair_rider(x):
    return _Rider(_pair_phases, x, jax.ShapeDtypeStruct((x.shape[0],) + x.shape[2:], x.dtype))


def _exchange(rider, name):
    def body(x_ref, out_ref, send_sems, recv_sems, local_sem):
        for phase in rider.phases(x_ref, out_ref, send_sems, recv_sems, local_sem):
            phase()

    return pl.pallas_call(body, name=name, out_shape=rider.out_shape, in_specs=[ANY_SPEC], out_specs=ANY_SPEC,
                          scratch_shapes=list(COMM_SCRATCH))(rider.x)


def _all_gather(x, name):
    return _exchange(_gather_rider(x), name)


def _hosted_call(body, rider, *, name, grid, in_specs, out_specs, out_shape, scratch_shapes, args):
    if rider is None:
        return pl.pallas_call(body, name=name, grid=grid, in_specs=in_specs, out_specs=out_specs, out_shape=out_shape,
                              scratch_shapes=scratch_shapes, compiler_params=_cparams(("arbitrary",) * len(grid)))(*args)
    n_in, n_out, n_scr = len(in_specs), len(out_specs), len(scratch_shapes)
    total = math.prod(grid)

    def riding(*refs):
        host_in, x_ref = refs[:n_in], refs[n_in]
        host_out, out_ref = refs[n_in + 1:n_in + 1 + n_out], refs[n_in + 1 + n_out]
        host_scr = refs[n_in + 2 + n_out:n_in + 2 + n_out + n_scr]
        start, forward, forward_late, finish = rider.phases(x_ref, out_ref, *refs[n_in + 2 + n_out + n_scr:])
        step = 0
        for axis, size in enumerate(grid):
            step = step * size + pl.program_id(axis)
        pl.when(step == 0)(start)
        pl.when(step == total // 2)(forward)
        pl.when(step == (3 * total) // 4)(forward_late)
        body(*host_in, *host_out, *host_scr)
        pl.when(step == total - 1)(finish)

    return pl.pallas_call(
        riding, name=name, grid=grid, in_specs=list(in_specs) + [ANY_SPEC], out_specs=list(out_specs) + [ANY_SPEC],
        out_shape=list(out_shape) + [rider.out_shape], scratch_shapes=list(scratch_shapes) + list(COMM_SCRATCH),
        compiler_params=_cparams(("arbitrary",) * len(grid)))(*args, rider.x)


def _adamw_math(w, g, m, v):
    m = ADAM_B1 * m + (1.0 - ADAM_B1) * g
    v = ADAM_B2 * v + (1.0 - ADAM_B2) * jnp.square(g)
    m_hat = m / (1.0 - ADAM_B1 ** ADAM_STEP)
    v_hat = v / (1.0 - ADAM_B2 ** ADAM_STEP)
    delta = -ADAM_LR * (m_hat / (jnp.sqrt(v_hat) + ADAM_EPS) + ADAM_WD * w)
    return delta, m, v


ADAM_ROWS = 128


def _elementwise_tile(rows, cols):
    if rows % ADAM_ROWS == 0:
        return ADAM_ROWS, cols
    return rows, (2 * LANE if cols % (2 * LANE) == 0 else cols)


def _add_blocks(a, b):
    g, rows, cols = a.shape
    tr, tc = _elementwise_tile(rows, cols)

    def body(a_ref, b_ref, o_ref):
        o_ref[...] = (a_ref[...].astype(F32) + b_ref[...].astype(F32)).astype(o_ref.dtype)

    blk = pl.BlockSpec((None, tr, tc), lambda k, i, j: (k, i, j))
    return pl.pallas_call(body, name="add_pair_blocks", grid=(g, rows // tr, cols // tc), in_specs=[blk, blk],
                          out_specs=blk, out_shape=jax.ShapeDtypeStruct(a.shape, a.dtype),
                          compiler_params=_cparams(("parallel", "parallel", "parallel")))(a, b)


def _adamw_reduce(w, m, v, parts, row0, name):
    rows, cols = w.shape
    n_parts = parts.shape[0]
    tr, tc = _elementwise_tile(rows, cols)
    r0 = row0 // tr

    def body(w_ref, m_ref, v_ref, *refs):
        part_refs, (g_ref, d_ref, nm_ref, nv_ref) = refs[:n_parts], refs[n_parts:]
        g = part_refs[0][...].astype(F32)
        for r in part_refs[1:]:
            g = g + r[...].astype(F32)
        delta, nm, nv = _adamw_math(w_ref[...], g, m_ref[...], v_ref[...])
        g_ref[...] = g
        d_ref[...] = delta
        nm_ref[...] = nm
        nv_ref[...] = nv

    blk = pl.BlockSpec((tr, tc), lambda i, j: (i, j))
    part_specs = [pl.BlockSpec((None, tr, tc), functools.partial(lambda i, j, k: (k, r0 + i, j), k=k))
                  for k in range(n_parts)]
    return pl.pallas_call(
        body, name=name, grid=(rows // tr, cols // tc), in_specs=[blk] * 3 + part_specs, out_specs=[blk] * 4,
        out_shape=[jax.ShapeDtypeStruct(w.shape, F32)] * 4, compiler_params=_cparams(("parallel", "parallel")),
    )(w, m, v, *([parts] * n_parts))


def _small_reduce(gathered):
    r = gathered.shape[1]

    def body(g_ref, o_ref):
        g = g_ref[0]
        for k in range(1, N_DEV):
            g = g + g_ref[k]
        o_ref[...] = g

    return pl.pallas_call(body, name="small_grad_reduce", out_shape=jax.ShapeDtypeStruct((r, LANE), F32))(gathered)


def _adamw_small(w, m, v, g):
    def body(w_ref, m_ref, v_ref, g_ref, d_ref, nm_ref, nv_ref):
        d_ref[...], nm_ref[...], nv_ref[...] = _adamw_math(w_ref[...], g_ref[...], m_ref[...], v_ref[...])

    return pl.pallas_call(body, name="adamw_small", out_shape=[jax.ShapeDtypeStruct(w.shape, F32)] * 3)(w, m, v, g)


def _pack_rows(arrays):
    rows = [jnp.pad(a.reshape(-1), (0, -a.size % LANE)).reshape(-1, LANE) for a in arrays]
    out = jnp.concatenate(rows, axis=0)
    return jnp.pad(out, ((0, -out.shape[0] % 8), (0, 0)))


def _unpack_rows(packed, shapes):
    out, r = [], 0
    for shp in shapes:
        size = math.prod(shp)
        nrows = -(-size // LANE)
        out.append(packed[r:r + nrows].reshape(-1)[:size].reshape(shp))
        r += nrows
    return out


def _add_residual(acc, res):
    return (res + acc,)


def _by_cols(g):
    return g.reshape(g.shape[0], N_DEV, -1).transpose(1, 0, 2)


def _from_cols(blocks):
    return blocks.transpose(1, 0, 2).reshape(blocks.shape[1], -1)


ROWS_OUT = D_MODEL // N_DEV

W_IN_SEGMENTS = ((0, WI_LR, "big", 0), (WI_LR, WI_DQKV, "gates", ZS_LR), (WI_DQKV, WI_DA, "big", ZB_DQKV),
                 (WI_DA, WI_DB, "gates", ZS_DA), (WI_DB, WI_GA, "gates", ZS_DB), (WI_GA, D_IN, "big", ZB_GA))


def _w_in_gate_rows(wt):
    parts = []
    for lo, hi, which, _ in W_IN_SEGMENTS:
        if which == "gates":
            parts += [wt[lo:hi], jnp.zeros((LANE - (hi - lo), wt.shape[1]), wt.dtype)]
    return jnp.concatenate(parts, axis=0)


def _fill_gate_rows(dw_in_t, dws_t):
    for lo, hi, which, first in W_IN_SEGMENTS:
        if which == "gates":
            dw_in_t = lax.dynamic_update_slice(dw_in_t, dws_t[first:first + hi - lo], (lo, 0))
    return dw_in_t


def _local_step(x, p, target, w, up_shard, rows_shard):
    bsz, seq, _ = x.shape
    t, nc = bsz * seq, seq // CHUNK
    x0, p2, tgt = x.reshape(t, D_MODEL), p.reshape(t, PLE_DIM), target.reshape(t, D_MODEL)

    h = _rms_fwd(x0, w["g_mix"], "rms_mix")
    zb, row_blocks = _mm(h, w["w_in_t"], tb=True, b_is_w_in_t=True, name="in_proj", rider=_gather_rider(rows_shard))
    zs = _mm(h, w["ws_t"], tb=True, name="in_proj_gates")
    act = _conv_fwd(zb, w["conv"], bsz, seq)
    mix_gla, gla_states = _gla_fwd(zb, zs, w["w2p"], w["gla_b"], w["gla_norm"], bsz, nc)
    mixed, dn_states, dn_t_invs, up_blocks = _dn_fwd(act, zb, zs, w["alog_b"], w["dtb_b"], w["dn_norm"], mix_gla,
                                                     bsz, nc, rider=_gather_rider(up_shard))
    w_up = _from_cols(up_blocks)
    w_out = row_blocks[:, :ROWS_OUT].reshape(D_MODEL, D_MODEL)
    w_pg = row_blocks[:, ROWS_OUT:2 * ROWS_OUT].reshape(D_MODEL, D_MODEL)
    w_down = row_blocks[:, 2 * ROWS_OUT:].reshape(D_FF, D_MODEL)
    x1 = _mm(mixed, w_out, epilogue=_add_residual, extras=(x0,), name="out_proj")
    h2 = _rms_fwd(x1, w["g_mlp"], "rms_mlp")
    u, a = _mm(h2, w_up, out_dtypes=(BF16, BF16), name="mlp_up",
               epilogue=lambda acc: (acc, jnp.square(jnp.maximum(acc, 0.0))))
    x2 = _mm(a, w_down, epilogue=_add_residual, extras=(x1,), name="mlp_down")
    h3 = _rms_fwd(x2, w["g_ple"], "rms_ple")
    pp = _mm(p2, w["w_pp"], name="ple_proj")
    gpre, x3 = _mm(h3, w_pg, out_dtypes=(F32, F32), extras=(x2, pp), name="ple_gate",
                   epilogue=lambda acc, res, proj: (acc, res + _sigmoid(acc) * proj))
    dx3, dg_final, loss = _loss_head(x3, w["g_final"], tgt)

    dgpre, dpp = _ple_bwd(dx3, gpre, pp)
    dw_pp = _mm(p2, dpp, ta=True, out_dtypes=(BF16,), name="d_w_ple_proj")
    dw_pg, recv_pp = _mm(h3, dgpre, ta=True, out_dtypes=(BF16,), name="d_w_ple_gate",
                         rider=_scatter_rider(_by_cols(dw_pp)))
    dh3 = _mm(dgpre, w_pg, tb=True, name="d_h_ple")
    dx2, dg_ple = _rms_bwd(x2, w["g_ple"], dh3, dx3, "rms_ple_bwd")
    du = _mm(dx2, w_down, tb=True, out_dtypes=(BF16,), extras=(u,), name="d_mlp_hidden",
             epilogue=lambda acc, uu: (acc * (2.0 * jnp.maximum(uu.astype(F32), 0.0)),))
    dw_down = _mm(a, dx2, ta=True, out_dtypes=(BF16,), name="d_w_down")
    dw_up = _mm(h2, du, ta=True, out_dtypes=(BF16,), name="d_w_up")
    dh2 = _mm(du, w_up, tb=True, name="d_h_mlp")
    dx1, dg_mlp = _rms_bwd(x1, w["g_mlp"], dh2, dx2, "rms_mlp_bwd")
    dmixed = _mm(dx1, w_out, tb=True, out_dtypes=(BF16,), name="d_mixed")
    dw_out = _mm(mixed, dx1, ta=True, out_dtypes=(BF16,), name="d_w_out")

    d_rows = jnp.concatenate([dw_out.reshape(N_DEV, ROWS_OUT, D_MODEL), dw_pg.reshape(N_DEV, ROWS_OUT, D_MODEL),
                              dw_down.reshape(N_DEV, D_FF // N_DEV, D_MODEL)], axis=1)
    (dact, dza, dzb_, dgz, dgb, dal, ddt, ddn, recv_rows) = _dn_bwd(
        act, zb, zs, w["alog_b"], w["dtb_b"], w["dn_norm"], dn_states, dn_t_invs, dmixed, bsz, nc,
        rider=_scatter_rider(d_rows))
    (gdq, gdk, gdv, dgg, dga, dlr, dw2, dgla_b, dgla_norm, recv_up) = _gla_bwd(
        zb, zs, w["w2p"], w["gla_b"], w["gla_norm"], gla_states, dmixed, bsz, nc,
        rider=_scatter_rider(_by_cols(dw_up)))
    dqkv, dconv = _conv_bwd(zb, w["conv"], dact, bsz, seq)
    dzb = jnp.concatenate([gdq, gdk, gdv, dgg, dqkv, dgz, dga, dgb], axis=1)
    dzs = jnp.concatenate([dlr, dza, dzb_], axis=1)
    dw_in_t = _mm(dzb, h, ta=True, out_dtypes=(BF16,), out_is_w_in_t=True, name="d_w_in")
    dws_t = _mm(dzs, h, ta=True, out_dtypes=(BF16,), name="d_w_in_gates")
    by_chip = _fill_gate_rows(dw_in_t, dws_t).reshape(N_DEV // 2, 2, D_IN_SHARD, D_MODEL)
    dh_gates, from_sibling = _mm(dzs, w["ws_t"], name="d_h_mix_gates", rider=_pair_rider(by_chip))
    mine = lax.dynamic_index_in_dim(by_chip, lax.axis_index("c"), axis=1, keepdims=False)
    chip_sums = _add_blocks(mine, from_sibling)
    dh, recv_in = _mm(dzb, w["w_in_t"], b_is_w_in_t=True, epilogue=_add_residual, extras=(dh_gates,), name="d_h_mix",
                      rider=_chip_scatter_rider(chip_sums))
    gx, dg_mix = _rms_bwd(x0, w["g_mix"], dh, dx1, "rms_mix_bwd")

    dgla_w2 = dw2[:, :GLA_LOWRANK, :].transpose(1, 0, 2).reshape(GLA_LOWRANK, GLA_QK)
    return dict(
        loss=loss[0, 0], grad_x=gx.reshape(x.shape), recv_in=recv_in, recv_up=recv_up, recv_rows=recv_rows,
        recv_pp=recv_pp,
        g_mix=dg_mix, gla_b=dgla_b.reshape(1, GLA_QK), gla_norm=dgla_norm, dn_norm=ddn, g_mlp=dg_mlp, g_ple=dg_ple,
        g_final=dg_final, gla_w2=dgla_w2, dn_conv=dconv,
        dn_a_log=dal[:, :DN_HEADS], dn_dt_bias=ddt[:, :DN_HEADS])


def _first_weights(g_mix, w_in, gla_w2, gla_b, gla_norm, dn_conv, dn_a_log, dn_dt_bias, dn_norm, g_mlp, g_ple,
                   w_ple_proj, g_final):
    w_in_t = _all_gather(jnp.swapaxes(w_in[0], 0, 1).astype(BF16), "gather_w_in").reshape(D_IN, D_MODEL)
    w_pp = _all_gather(w_ple_proj[0].astype(BF16), "gather_w_ple_proj")
    small = _all_gather(_pack_rows([gla_w2[0], dn_conv[0]]), "gather_w_small")
    n_w2 = GLA_LOWRANK * GLA_QK // N_DEV // LANE
    n_cv = DN_CONV * DN_QKV // N_DEV // LANE
    w2 = small[:, :n_w2].reshape(N_DEV, GLA_LOWRANK, GLA_QK // N_DEV).transpose(1, 0, 2).reshape(GLA_LOWRANK, GLA_QK)
    conv = small[:, n_w2:n_w2 + n_cv].reshape(N_DEV, DN_CONV, DN_QKV // N_DEV).transpose(1, 0, 2).reshape(DN_CONV, DN_QKV)

    return dict(
        w_in_t=w_in_t, ws_t=_w_in_gate_rows(w_in_t), w_pp=_from_cols(w_pp),
        w2p=jnp.pad(w2, ((0, LANE - GLA_LOWRANK), (0, 0))), conv=conv,
        alog_b=jnp.pad(dn_a_log, ((0, 0), (0, LANE - DN_HEADS))),
        dtb_b=jnp.pad(dn_dt_bias, ((0, 0), (0, LANE - DN_HEADS))),
        g_mix=g_mix, gla_b=gla_b, gla_norm=gla_norm, dn_norm=dn_norm, g_mlp=g_mlp, g_ple=g_ple,
        g_final=g_final.reshape(1, D_MODEL))


def kernel(x, p, g_mix, w_in, gla_w2, gla_b, gla_norm, dn_conv, dn_a_log, dn_dt_bias, dn_norm, w_out, g_mlp, w_up, w_down, g_ple, w_ple_gate, w_ple_proj, g_final, loss_target, m_g_mix, m_w_in, m_gla_w2, m_gla_b, m_gla_norm, m_dn_conv, m_dn_a_log, m_dn_dt_bias, m_dn_norm, m_w_out, m_g_mlp, m_w_up, m_w_down, m_g_ple, m_w_ple_gate, m_w_ple_proj, m_g_final, v_g_mix, v_w_in, v_gla_w2, v_gla_b, v_gla_norm, v_dn_conv, v_dn_a_log, v_dn_dt_bias, v_dn_norm, v_w_out, v_g_mlp, v_w_up, v_w_down, v_g_ple, v_w_ple_gate, v_w_ple_proj, v_g_final):
    names = ["g_mix", "w_in", "gla_w2", "gla_b", "gla_norm", "dn_conv", "dn_a_log", "dn_dt_bias", "dn_norm", "w_out",
             "g_mlp", "w_up", "w_down", "g_ple", "w_ple_gate", "w_ple_proj", "g_final"]
    ws = dict(zip(names, (g_mix, w_in, gla_w2, gla_b, gla_norm, dn_conv, dn_a_log, dn_dt_bias, dn_norm, w_out, g_mlp,
                          w_up, w_down, g_ple, w_ple_gate, w_ple_proj, g_final)))
    ms = dict(zip(names, (m_g_mix, m_w_in, m_gla_w2, m_gla_b, m_gla_norm, m_dn_conv, m_dn_a_log, m_dn_dt_bias,
                          m_dn_norm, m_w_out, m_g_mlp, m_w_up, m_w_down, m_g_ple, m_w_ple_gate, m_w_ple_proj,
                          m_g_final)))
    vs = dict(zip(names, (v_g_mix, v_w_in, v_gla_w2, v_gla_b, v_gla_norm, v_dn_conv, v_dn_a_log, v_dn_dt_bias,
                          v_dn_norm, v_w_out, v_g_mlp, v_w_up, v_w_down, v_g_ple, v_w_ple_gate, v_w_ple_proj,
                          v_g_final)))
    me = 4 * lax.axis_index("x") + 2 * lax.axis_index("y") + lax.axis_index("c")

    first = _first_weights(g_mix, w_in, gla_w2, gla_b, gla_norm, dn_conv, dn_a_log, dn_dt_bias, dn_norm, g_mlp,
                           g_ple, w_ple_proj, g_final)
    rows_shard = jnp.concatenate([w_out[0], w_ple_gate[0], w_down[0]], axis=0).astype(BF16)
    r = _local_step(x, p[0], loss_target, first, w_up[0].astype(BF16), rows_shard)
    loss = lax.psum(r["loss"], ("x", "y", "c"))

    grads, deltas, new_m, new_v = {}, {}, {}, {}

    def big(name, parts, row0=0):
        g, d, nm, nv = _adamw_reduce(ws[name][0], ms[name][0], vs[name][0], parts, row0, "adamw_" + name)
        grads[name], deltas[name], new_m[name], new_v[name] = g[None], d[None], nm[None], nv[None]

    t_outs = _adamw_reduce(*[jnp.swapaxes(d["w_in"][0], 0, 1) for d in (ws, ms, vs)], r["recv_in"], 0, "adamw_w_in")
    grads["w_in"], deltas["w_in"], new_m["w_in"], new_v["w_in"] = [jnp.swapaxes(o, 0, 1)[None] for o in t_outs]
    big("w_up", r["recv_up"])
    big("w_ple_proj", r["recv_pp"])
    big("w_out", r["recv_rows"], 0)
    big("w_ple_gate", r["recv_rows"], ROWS_OUT)
    big("w_down", r["recv_rows"], 2 * ROWS_OUT)

    vec_names = ["g_mix", "gla_b", "gla_norm", "dn_norm", "g_mlp", "g_ple", "g_final", "dn_a_log", "dn_dt_bias"]
    gathered_names = vec_names + ["gla_w2", "dn_conv"]
    total = _small_reduce(_all_gather(_pack_rows([r[n] for n in gathered_names]), "gather_small_grads"))
    sg = dict(zip(gathered_names, _unpack_rows(total, [r[n].shape for n in gathered_names])))
    sg["g_final"] = sg["g_final"].reshape(D_MODEL)
    sg["gla_w2"] = lax.dynamic_slice_in_dim(sg["gla_w2"], me * (GLA_QK // N_DEV), GLA_QK // N_DEV, axis=1)
    sg["dn_conv"] = lax.dynamic_slice_in_dim(sg["dn_conv"], me * (DN_QKV // N_DEV), DN_QKV // N_DEV, axis=1)
    small_names = gathered_names
    shapes = [ws[n].shape for n in small_names]
    d_s, m_s, v_s = _adamw_small(_pack_rows([ws[n] for n in small_names]), _pack_rows([ms[n] for n in small_names]),
                                 _pack_rows([vs[n] for n in small_names]), _pack_rows([sg[n] for n in small_names]))
    for n, d, nm, nv in zip(small_names, _unpack_rows(d_s, shapes), _unpack_rows(m_s, shapes), _unpack_rows(v_s, shapes)):
        grads[n], deltas[n], new_m[n], new_v[n] = sg[n].reshape(ws[n].shape), d, nm, nv

    return (loss, r["grad_x"], *[grads[n] for n in names], *[deltas[n] for n in names],
            *[new_m[n] for n in names], *[new_v[n] for n in names])
```

```python
import functools
import math

import jax
import jax.numpy as jnp
from jax import lax
from jax.experimental import pallas as pl
from jax.experimental.pallas import tpu as pltpu

F32 = jnp.float32
BF16 = jnp.bfloat16

N_DEV = 8
D_MODEL = 2048
CHUNK = 64
PLE_DIM = 256
EPS = 1e-6
GLA_HEADS = 4
GLA_DK = 256
GLA_DV = 512
GLA_LOWRANK = 16
GLA_TAU = 16.0
DN_HEADS = 16
DN_D = 128
DN_CONV = 4
D_FF = 4 * D_MODEL
GLA_QK = GLA_HEADS * GLA_DK
GLA_V = GLA_HEADS * GLA_DV
DN_QKV = 3 * DN_HEADS * DN_D
D_IN = 2 * GLA_QK + 2 * GLA_V + GLA_LOWRANK + DN_QKV + D_MODEL + 2 * DN_HEADS + 2 * D_MODEL
D_IN_SHARD = D_IN // N_DEV

ADAM_LR = 0.001
ADAM_B1 = 0.9
ADAM_B2 = 0.999
ADAM_EPS = 1e-08
ADAM_WD = 0.01
ADAM_STEP = 10

LANE = 128
ZB_GQ, ZB_GK, ZB_GV, ZB_GG = 0, 1024, 2048, 4096
ZB_DQKV, ZB_DZ, ZB_GA, ZB_GB = 6144, 12288, 14336, 16384
ZB_W = 18432
ZS_LR, ZS_DA, ZS_DB = 0, 128, 256
ZS_W = 384
WI_LR = 2 * GLA_QK + 2 * GLA_V
WI_DQKV = WI_LR + GLA_LOWRANK
WI_DA = WI_DQKV + DN_QKV + D_MODEL
WI_DB = WI_DA + DN_HEADS
WI_GA = WI_DB + DN_HEADS

VMEM_LIMIT = 56 * 1024 * 1024

def _bdot(a, b, dims):
    return lax.dot_general(a.astype(BF16), b.astype(BF16), dims, preferred_element_type=F32)


def _split3(x):
    hi = x.astype(BF16)
    rest = x - hi.astype(F32)
    mid = rest.astype(BF16)
    return hi, mid, (rest - mid.astype(F32)).astype(BF16)


def _dot01(m, x, dims):
    m = m.astype(BF16)
    out = None
    for piece in _split3(x):
        d = lax.dot_general(m, piece, dims, preferred_element_type=F32)
        out = d if out is None else out + d
    return out


@functools.partial(jax.custom_vjp, nondiff_argnums=(2, 3))
def _left_dot(m, x, dims, dims_t):
    return _dot01(m, x, dims)


def _left_dot_fwd(m, x, dims, dims_t):
    return _dot01(m, x, dims), m


def _left_dot_bwd(dims, dims_t, m, ct):
    return jnp.zeros_like(m), _dot01(m, ct, dims_t)


_left_dot.defvjp(_left_dot_fwd, _left_dot_bwd)


def _dot3(a, b, dims):
    ah, bh = a.astype(BF16), b.astype(BF16)
    al, bl = (a - ah.astype(F32)).astype(BF16), (b - bh.astype(F32)).astype(BF16)
    dot = functools.partial(lax.dot_general, dimension_numbers=dims, preferred_element_type=F32)
    return dot(ah, bh) + (dot(ah, bl) + dot(al, bh))


def _sigmoid(x):
    return 1.0 / (1.0 + jnp.exp(-x))


def _silu(x):
    return x * _sigmoid(x)


def _softplus(x):
    return jnp.maximum(x, 0.0) + jnp.log(1.0 + jnp.exp(-jnp.abs(x)))


def _iota2(shape, dim):
    return lax.broadcasted_iota(jnp.int32, shape, dim)


def _cparams(sem=None):
    return pltpu.CompilerParams(dimension_semantics=sem, vmem_limit_bytes=VMEM_LIMIT)


BNN = (((2,), (1,)), ((0,), (0,)))
BNT = (((2,), (2,)), ((0,), (0,)))
BTN = (((1,), (1,)), ((0,), (0,)))


def _gla_chunk(st, q, k, v, lr, w2, b, gg, ga, gn):
    hb, c, _ = q.shape
    incl = (_iota2((c, c), 0) >= _iota2((c, c), 1))[None]
    tri = jnp.broadcast_to(incl.astype(F32), (hb, c, c))
    q = q.astype(F32) * (GLA_DK ** -0.5)
    k = k.astype(F32)
    v = v.astype(F32)
    lr_b = jnp.broadcast_to(lr[None], (hb,) + lr.shape)
    lf = -_softplus(-(_bdot(lr_b, w2, BNN) + b)) / GLA_TAU
    bcum = _left_dot(tri, lf, BNN, BTN)
    blast = jnp.sum(lf, axis=1, keepdims=True)
    q_in = q * jnp.exp(bcum)
    k_in = k * jnp.exp(-bcum)
    a = jnp.where(incl, _bdot(q_in, k_in, BNT), 0.0)
    o = _bdot(a, v, BNN) + _bdot(q_in, st, BNT)
    k_dec = k * jnp.exp(blast - bcum)
    st_new = st * jnp.exp(blast) + _bdot(v, k_dec, BTN)
    on = o * lax.rsqrt(jnp.mean(o * o, axis=-1, keepdims=True) + EPS) * gn
    res = _sigmoid(ga.astype(F32)) * on * _silu(gg.astype(F32))
    return res, st_new


def _tri_inv_raw(a):
    _, c, _ = a.shape
    eye = (_iota2((c, c), 0) == _iota2((c, c), 1)).astype(F32)[None]
    x = a
    p = eye - a
    for _ in range(5):
        x = _dot3(x, x, BNN)
        p = _dot3(p, eye + x, BNN)
    return p


def _tri_inv_bwd(t, dt):
    return (-_dot3(_dot3(t, dt, BTN), t, BNT),)


@jax.custom_vjp
def _tri_inv_given(a, t):
    return t


def _tri_inv_given_fwd(a, t):
    return t, t


def _tri_inv_given_bwd(t, dt):
    return _tri_inv_bwd(t, dt) + (jnp.zeros_like(t),)


_tri_inv_given.defvjp(_tri_inv_given_fwd, _tri_inv_given_bwd)


@functools.partial(jax.custom_vjp, nondiff_argnums=(1, 2))
def _column_on_lanes(z, j, width):
    picked = jnp.where(_iota2(z.shape, 1) == j, z, 0.0)
    return jnp.broadcast_to(jnp.sum(picked, axis=1, keepdims=True), (z.shape[0], width))


def _column_on_lanes_fwd(z, j, width):
    return _column_on_lanes(z, j, width), None


def _column_on_lanes_bwd(j, width, _, ct):
    shape = (ct.shape[0], LANE)
    total = jnp.broadcast_to(jnp.sum(ct, axis=1, keepdims=True), shape)
    return (jnp.where(_iota2(shape, 1) == j, total, 0.0),)


_column_on_lanes.defvjp(_column_on_lanes_fwd, _column_on_lanes_bwd)


def _dn_chunk(s, qr, kr, vr, za, zb, alog, dtb, gz, gb, dn, t_saved=None):
    hb, c, _ = qr.shape
    row, col = _iota2((c, c), 0), _iota2((c, c), 1)
    incl = (row >= col)[None]
    strict = (row > col)[None]
    tri = jnp.broadcast_to(incl.astype(F32), (hb, c, c))

    def l2n(t):
        return t * lax.rsqrt(jnp.sum(t * t, axis=-1, keepdims=True) + EPS)

    q = l2n(qr.astype(F32)) * (DN_D ** -0.5)
    k = l2n(kr.astype(F32))
    v = vr.astype(F32)
    g_heads = -jnp.exp(alog) * _softplus(za + dtb)
    beta_heads = _sigmoid(zb)
    def per_head(z, width):
        return jnp.concatenate([_column_on_lanes(z, j, width)[None] for j in range(hb)], axis=0)

    g = per_head(g_heads, LANE)
    beta = per_head(beta_heads, LANE)
    gcum = _left_dot(tri, g, BNN, BTN)
    glast = jnp.sum(g, axis=1, keepdims=True)
    diff = _left_dot(tri, per_head(g_heads, c) * strict.astype(F32), BNN, BTN)
    dec = jnp.exp(jnp.where(incl, diff, -1e30))
    kb = k * beta
    a = jnp.where(strict, _bdot(kb, k, BNT) * dec, 0.0)
    t = _tri_inv_raw(a) if t_saved is None else _tri_inv_given(a, t_saved)
    egc = jnp.exp(gcum)
    u = _bdot(t, v * beta, BNN)
    w = _bdot(t, kb * egc, BNN)
    attn = jnp.where(incl, _bdot(q, k, BNT) * dec, 0.0)
    q_dec = q * egc
    k_dec = k * jnp.exp(glast - gcum)
    v_new = u - _bdot(w, s, BNN)
    o = _bdot(q_dec, s, BNN) + _bdot(attn, v_new, BNN)
    s_new = s * jnp.exp(glast) + _bdot(k_dec, v_new, BTN)
    on = o * lax.rsqrt(jnp.mean(o * o, axis=-1, keepdims=True) + EPS) * dn
    res = _sigmoid(gb.astype(F32)) * on * _silu(gz.astype(F32))
    return (res, s_new, t) if t_saved is None else (res, s_new)


def _heads(ref, n_heads, width):
    return jnp.stack([ref[:, j * width:(j + 1) * width] for j in range(n_heads)], axis=0)


def _gla_specs(nc, reverse):
    def rows(b, n):
        return b * nc + ((nc - 1 - n) if reverse else n)

    qk = lambda base: pl.BlockSpec((CHUNK, GLA_QK), lambda b, n: (rows(b, n), base // GLA_QK))
    vv = lambda base: pl.BlockSpec((CHUNK, GLA_V), lambda b, n: (rows(b, n), base // GLA_V))
    lr = lambda c: pl.BlockSpec((CHUNK, LANE), lambda b, n: (rows(b, n), c))
    full = lambda shape: pl.BlockSpec(shape, lambda b, n: (0,) * len(shape))
    return rows, qk, vv, lr, full


def _gla_inputs(q_ref, k_ref, v_ref, gg_ref, ga_ref, lr_ref, w2_ref, b_ref, gn_ref):
    return (_heads(q_ref, GLA_HEADS, GLA_DK), _heads(k_ref, GLA_HEADS, GLA_DK), _heads(v_ref, GLA_HEADS, GLA_DV),
            lr_ref[...], _heads(w2_ref, GLA_HEADS, GLA_DK), _heads(b_ref, GLA_HEADS, GLA_DK),
            _heads(gg_ref, GLA_HEADS, GLA_DV), _heads(ga_ref, GLA_HEADS, GLA_DV), gn_ref[...])


def _gla_fwd(zb, zs, w2p, gla_b, gla_norm, bsz, nc):
    t = zb.shape[0]
    rows, qk, vv, lr, full = _gla_specs(nc, False)

    def body(q_ref, k_ref, v_ref, gg_ref, ga_ref, lr_ref, w2_ref, b_ref, gn_ref, o_ref, st_ref, state):
        st = jnp.where(pl.program_id(1) > 0, state[...], 0.0)
        st_ref[0, 0] = st
        res, st_new = _gla_chunk(st, *_gla_inputs(q_ref, k_ref, v_ref, gg_ref, ga_ref, lr_ref, w2_ref, b_ref, gn_ref))
        for j in range(GLA_HEADS):
            o_ref[:, j * GLA_DV:(j + 1) * GLA_DV] = res[j]
        state[...] = st_new

    return pl.pallas_call(
        body, name="gla_fwd", grid=(bsz, nc),
        in_specs=[qk(ZB_GQ), qk(ZB_GK), vv(ZB_GV), vv(ZB_GG), vv(ZB_GA), lr(ZS_LR // LANE),
                  full((LANE, GLA_QK)), full((1, GLA_QK)), full((1, GLA_DV))],
        out_specs=[vv(0), pl.BlockSpec((1, 1, GLA_HEADS, GLA_DV, GLA_DK), lambda b, n: (b, n, 0, 0, 0))],
        out_shape=[jax.ShapeDtypeStruct((t, GLA_V), F32),
                   jax.ShapeDtypeStruct((bsz, nc, GLA_HEADS, GLA_DV, GLA_DK), F32)],
        scratch_shapes=[pltpu.VMEM((GLA_HEADS, GLA_DV, GLA_DK), F32)],
        compiler_params=_cparams(("arbitrary", "arbitrary")),
    )(zb, zb, zb, zb, zb, zs, w2p, gla_b, gla_norm)


def _gla_bwd(zb, zs, w2p, gla_b, gla_norm, states, dmixed, bsz, nc, rider=None):
    t = zb.shape[0]
    rows, qk, vv, lr, full = _gla_specs(nc, True)

    def body(q_ref, k_ref, v_ref, gg_ref, ga_ref, lr_ref, w2_ref, b_ref, gn_ref, st_ref, dm_ref,
             dq_ref, dk_ref, dv_ref, dgg_ref, dga_ref, dlr_ref, dw2_ref, db_ref, dgn_ref, dstate):
        b, n = pl.program_id(0), pl.program_id(1)

        @pl.when((b == 0) & (n == 0))
        def _():
            dw2_ref[...] = jnp.zeros_like(dw2_ref)
            db_ref[...] = jnp.zeros_like(db_ref)
            dgn_ref[...] = jnp.zeros_like(dgn_ref)

        _, vjp = jax.vjp(_gla_chunk, st_ref[0, 0],
                         *_gla_inputs(q_ref, k_ref, v_ref, gg_ref, ga_ref, lr_ref, w2_ref, b_ref, gn_ref))
        dst_in = jnp.where(n > 0, dstate[...], 0.0)
        dst, dq, dk, dv, dlr, dw2, db, dgg, dga, dgn = vjp((_heads(dm_ref, GLA_HEADS, GLA_DV).astype(F32), dst_in))
        dstate[...] = dst
        for j in range(GLA_HEADS):
            dq_ref[:, j * GLA_DK:(j + 1) * GLA_DK] = dq[j].astype(dq_ref.dtype)
            dk_ref[:, j * GLA_DK:(j + 1) * GLA_DK] = dk[j].astype(dk_ref.dtype)
            dv_ref[:, j * GLA_DV:(j + 1) * GLA_DV] = dv[j].astype(dv_ref.dtype)
            dgg_ref[:, j * GLA_DV:(j + 1) * GLA_DV] = dgg[j].astype(dgg_ref.dtype)
            dga_ref[:, j * GLA_DV:(j + 1) * GLA_DV] = dga[j].astype(dga_ref.dtype)
        dlr_ref[...] = dlr
        dw2_ref[...] += dw2
        db_ref[...] += db
        dgn_ref[...] += dgn

    return _hosted_call(
        body, rider, name="gla_bwd", grid=(bsz, nc),
        in_specs=[qk(ZB_GQ), qk(ZB_GK), vv(ZB_GV), vv(ZB_GG), vv(ZB_GA), lr(ZS_LR // LANE),
                  full((LANE, GLA_QK)), full((1, GLA_QK)), full((1, GLA_DV)),
                  pl.BlockSpec((1, 1, GLA_HEADS, GLA_DV, GLA_DK), lambda b, n: (b, nc - 1 - n, 0, 0, 0)),
                  vv(0)],
        out_specs=[qk(0), qk(0), vv(0), vv(0), vv(0), lr(0),
                   full((GLA_HEADS, LANE, GLA_DK)), full((GLA_HEADS, 1, GLA_DK)), full((1, GLA_DV))],
        out_shape=[jax.ShapeDtypeStruct((t, GLA_QK), BF16), jax.ShapeDtypeStruct((t, GLA_QK), BF16),
                   jax.ShapeDtypeStruct((t, GLA_V), BF16), jax.ShapeDtypeStruct((t, GLA_V), BF16),
                   jax.ShapeDtypeStruct((t, GLA_V), BF16), jax.ShapeDtypeStruct((t, LANE), F32),
                   jax.ShapeDtypeStruct((GLA_HEADS, LANE, GLA_DK), F32),
                   jax.ShapeDtypeStruct((GLA_HEADS, 1, GLA_DK), F32),
                   jax.ShapeDtypeStruct((1, GLA_DV), F32)],
        scratch_shapes=[pltpu.VMEM((GLA_HEADS, GLA_DV, GLA_DK), F32)],
        args=(zb, zb, zb, zb, zb, zs, w2p, gla_b, gla_norm, states, dmixed))


DN_HB = DN_HEADS


def _dn_specs(nc, reverse):
    wide = DN_HB * DN_D

    def rows(b, n, h):
        return b * nc + ((nc - 1 - n) if reverse else n)

    def col(base):
        return pl.BlockSpec((CHUNK, wide), lambda b, n, h: (rows(b, n, h), base // wide + h))

    def fixed(c):
        return pl.BlockSpec((CHUNK, LANE), lambda b, n, h: (rows(b, n, h), c))

    head = pl.BlockSpec((1, LANE), lambda b, n, h: (0, 0))
    return rows, col, fixed, head


def _lanes(j):
    return slice(j * DN_D, (j + 1) * DN_D)


def _by_head(ref):
    return jnp.stack([ref[:, _lanes(j)] for j in range(DN_HB)], axis=0)


def _dn_fwd(act, zb, zs, alog_b, dtb_b, dn_norm, mix_gla, bsz, nc, rider=None):
    t = zb.shape[0]
    rows, col, fixed, head = _dn_specs(nc, False)

    def body(q_ref, k_ref, v_ref, za_ref, zb_ref, al_ref, dt_ref, gz_ref, gb_ref, dn_ref, mg_ref,
             o_ref, st_ref, ti_ref, state):
        s = jnp.where(pl.program_id(1) > 0, state[...], 0.0)
        st_ref[0, 0] = s
        res, s_new, t_inv = _dn_chunk(s, _by_head(q_ref), _by_head(k_ref), _by_head(v_ref), za_ref[...], zb_ref[...],
                                      al_ref[...], dt_ref[...], _by_head(gz_ref), _by_head(gb_ref), dn_ref[...])
        ti_ref[0, 0] = t_inv
        for j in range(DN_HB):
            o_ref[:, _lanes(j)] = (res[j] + mg_ref[:, _lanes(j)]).astype(o_ref.dtype)
        state[...] = s_new

    return _hosted_call(
        body, rider, name="dn_fwd", grid=(bsz, nc, DN_HEADS // DN_HB),
        in_specs=[col(0), col(DN_HEADS * DN_D), col(2 * DN_HEADS * DN_D),
                  fixed(ZS_DA // LANE), fixed(ZS_DB // LANE), head, head,
                  col(ZB_DZ), col(ZB_GB), pl.BlockSpec((1, DN_D), lambda b, n, h: (0, 0)), col(0)],
        out_specs=[col(0), pl.BlockSpec((1, 1, DN_HB, DN_D, DN_D), lambda b, n, h: (b, n, h, 0, 0)),
                   pl.BlockSpec((1, 1, DN_HB, CHUNK, CHUNK), lambda b, n, h: (b, n, h, 0, 0))],
        out_shape=[jax.ShapeDtypeStruct((t, D_MODEL), BF16),
                   jax.ShapeDtypeStruct((bsz, nc, DN_HEADS, DN_D, DN_D), F32),
                   jax.ShapeDtypeStruct((bsz, nc, DN_HEADS, CHUNK, CHUNK), F32)],
        scratch_shapes=[pltpu.VMEM((DN_HEADS, DN_D, DN_D), F32)],
        args=(act, act, act, zs, zs, alog_b, dtb_b, zb, zb, dn_norm, mix_gla))


def _dn_bwd(act, zb, zs, alog_b, dtb_b, dn_norm, states, t_invs, dmixed, bsz, nc, rider=None):
    t = zb.shape[0]
    rows, col, fixed, head = _dn_specs(nc, True)

    def body(q_ref, k_ref, v_ref, za_ref, zb_ref, al_ref, dt_ref, gz_ref, gb_ref, dn_ref, st_ref, ti_ref, dm_ref,
             dact_ref, dza_ref, dzb_ref, dgz_ref, dgb_ref, dal_ref, ddt_ref, ddn_ref, dstate):
        b, n = pl.program_id(0), pl.program_id(1)

        @pl.when((b == 0) & (n == 0))
        def _():
            dal_ref[...] = jnp.zeros_like(dal_ref)
            ddt_ref[...] = jnp.zeros_like(ddt_ref)
            ddn_ref[...] = jnp.zeros_like(ddn_ref)

        fn = functools.partial(_dn_chunk, t_saved=ti_ref[0, 0])
        _, vjp = jax.vjp(fn, st_ref[0, 0], _by_head(q_ref), _by_head(k_ref), _by_head(v_ref), za_ref[...],
                         zb_ref[...], al_ref[...], dt_ref[...], _by_head(gz_ref), _by_head(gb_ref), dn_ref[...])
        ds_in = jnp.where(n > 0, dstate[...], 0.0)
        ds, dq, dk, dv, dza, dzb, dal, ddt, dgz, dgb, ddn = vjp((_by_head(dm_ref).astype(F32), ds_in))
        dstate[...] = ds
        for j in range(DN_HB):
            for part, d in enumerate((dq, dk, dv)):
                dact_ref[:, pl.ds(part * DN_HEADS * DN_D + j * DN_D, DN_D)] = d[j]
            dgz_ref[:, _lanes(j)] = dgz[j].astype(dgz_ref.dtype)
            dgb_ref[:, _lanes(j)] = dgb[j].astype(dgb_ref.dtype)
        dal_ref[...] += dal
        ddt_ref[...] += ddt
        dza_ref[...] = dza
        dzb_ref[...] = dzb
        ddn_ref[...] += ddn

    full = lambda shape: pl.BlockSpec(shape, lambda b, n, h: (0,) * len(shape))
    return _hosted_call(
        body, rider, name="dn_bwd", grid=(bsz, nc, DN_HEADS // DN_HB),
        in_specs=[col(0), col(DN_HEADS * DN_D), col(2 * DN_HEADS * DN_D),
                  fixed(ZS_DA // LANE), fixed(ZS_DB // LANE), head, head,
                  col(ZB_DZ), col(ZB_GB), pl.BlockSpec((1, DN_D), lambda b, n, h: (0, 0)),
                  pl.BlockSpec((1, 1, DN_HB, DN_D, DN_D), lambda b, n, h: (b, nc - 1 - n, h, 0, 0)),
                  pl.BlockSpec((1, 1, DN_HB, CHUNK, CHUNK), lambda b, n, h: (b, nc - 1 - n, h, 0, 0)), col(0)],
        out_specs=[pl.BlockSpec((CHUNK, DN_QKV), lambda b, n, h: (rows(b, n, h), 0)), fixed(0), fixed(0), col(0), col(0),
                   full((1, LANE)), full((1, LANE)), full((1, DN_D))],
        out_shape=[jax.ShapeDtypeStruct((t, DN_QKV), F32),
                   jax.ShapeDtypeStruct((t, LANE), F32), jax.ShapeDtypeStruct((t, LANE), F32),
                   jax.ShapeDtypeStruct((t, D_MODEL), BF16), jax.ShapeDtypeStruct((t, D_MODEL), BF16),
                   jax.ShapeDtypeStruct((1, LANE), F32), jax.ShapeDtypeStruct((1, LANE), F32),
                   jax.ShapeDtypeStruct((1, DN_D), F32)],
        scratch_shapes=[pltpu.VMEM((DN_HEADS, DN_D, DN_D), F32)],
        args=(act, act, act, zs, zs, alog_b, dtb_b, zb, zb, dn_norm, states, t_invs, dmixed))


MM_VMEM_BUDGET = 40 * 1024 * 1024
MM_TILE_PREF = (1024, 1024, 2048)


def _divisor_tile(n, cap):
    if n <= cap:
        return n
    for c in range(cap - cap % LANE, 0, -LANE):
        if n % c == 0:
            return c
    return n


def _mm_tiles(m, n, kd, a_bytes, b_bytes, mn_bytes):
    tm, tn, tk = (_divisor_tile(d, c) for d, c in zip((m, n, kd), MM_TILE_PREF))

    def need(tm, tn, tk):
        acc = 0 if tk == kd else 4 * tm * tn
        return 2 * (tm * tk * a_bytes + tk * tn * b_bytes + tm * tn * mn_bytes) + acc + 4 * tm * tn

    while need(tm, tn, tk) > MM_VMEM_BUDGET:
        if tk > 512 and tk * max(tm * a_bytes, tn * b_bytes) >= tm * tn * mn_bytes:
            tk = _divisor_tile(kd, tk // 2)
        elif tn >= tm and tn > LANE:
            tn = _divisor_tile(n, tn // 2)
        else:
            tm = _divisor_tile(m, tm // 2)
    return tm, tn, tk


def _w_in_row_of(tile, tile_rows):
    skipped = jnp.where(tile >= ZB_GA // tile_rows, WI_GA - ZB_GA, jnp.where(tile >= ZB_DQKV // tile_rows,
                                                                             WI_DQKV - ZB_DQKV, 0))
    return pl.multiple_of(tile * tile_rows + skipped, 16)


def _mm(a, b, *, ta=False, tb=False, out_dtypes=(F32,), epilogue=None, extras=(), name, rider=None,
        b_is_w_in_t=False, out_is_w_in_t=False):
    m, kd = (a.shape[1], a.shape[0]) if ta else a.shape
    n = b.shape[0] if tb else b.shape[1]
    if b_is_w_in_t:
        n, kd = (ZB_W, kd) if tb else (n, ZB_W)
    mn_bytes = sum(e.dtype.itemsize for e in extras) + sum(jnp.dtype(dt).itemsize for dt in out_dtypes)
    tm, tn, tk = _mm_tiles(m, n, kd, a.dtype.itemsize, b.dtype.itemsize, mn_bytes)
    nk = kd // tk
    n_ex = len(extras)
    dims = (((0,) if ta else (1,), (1,) if tb else (0,)), ((), ()))

    def finish(acc, ex_refs, out_refs):
        outs = (acc,) if epilogue is None else epilogue(acc, *[r[...] for r in ex_refs])
        for r, o in zip(out_refs, outs):
            r[...] = o.astype(r.dtype)

    def partial_product(a_ref, b_ref):
        return lax.dot_general(a_ref[...].astype(BF16), b_ref[...].astype(BF16), dims, preferred_element_type=F32)

    def body_single(*refs):
        finish(partial_product(refs[0], refs[1]), refs[2:2 + n_ex], refs[2 + n_ex:])

    def body_acc(*refs):
        acc = refs[-1]
        k = pl.program_id(2)

        @pl.when(k == 0)
        def _():
            acc[...] = partial_product(refs[0], refs[1])

        @pl.when(k > 0)
        def _():
            acc[...] += partial_product(refs[0], refs[1])

        @pl.when(k == nk - 1)
        def _():
            finish(acc[...], refs[2:2 + n_ex], refs[2 + n_ex:-1])

    a_spec = pl.BlockSpec((tk, tm), lambda i, j, k: (k, i)) if ta else pl.BlockSpec((tm, tk), lambda i, j, k: (i, k))
    b_spec = pl.BlockSpec((tn, tk), lambda i, j, k: (j, k)) if tb else pl.BlockSpec((tk, tn), lambda i, j, k: (k, j))
    mn_spec = pl.BlockSpec((tm, tn), lambda i, j, k: (i, j))
    out_spec, out_rows = mn_spec, m
    if b_is_w_in_t and tb:
        b_spec = pl.BlockSpec((pl.Element(tn), pl.Element(tk)),
                              lambda i, j, k: (_w_in_row_of(j, tn), pl.multiple_of(k * tk, LANE)))
    elif b_is_w_in_t:
        b_spec = pl.BlockSpec((pl.Element(tk), pl.Element(tn)),
                              lambda i, j, k: (_w_in_row_of(k, tk), pl.multiple_of(j * tn, LANE)))
    if out_is_w_in_t:
        out_spec, out_rows = pl.BlockSpec((pl.Element(tm), pl.Element(tn)),
                                          lambda i, j, k: (_w_in_row_of(i, tm), pl.multiple_of(j * tn, LANE))), D_IN
    outs = _hosted_call(
        body_single if nk == 1 else body_acc, rider, name=name, grid=(m // tm, n // tn, nk),
        in_specs=[a_spec, b_spec] + [mn_spec] * n_ex,
        out_specs=[out_spec] * len(out_dtypes),
        out_shape=[jax.ShapeDtypeStruct((out_rows, n), dt) for dt in out_dtypes],
        scratch_shapes=[] if nk == 1 else [pltpu.VMEM((tm, tn), F32)],
        args=(a, b, *extras))
    return outs[0] if len(outs) == 1 else outs


ROW_BLOCK = 256


def _row_spec(width=D_MODEL):
    return pl.BlockSpec((ROW_BLOCK, width), lambda i: (i, 0))


def _vec_spec(width=D_MODEL):
    return pl.BlockSpec((1, width), lambda i: (0, 0))


def _rms_fwd(x, g, name):
    def body(x_ref, g_ref, h_ref):
        xf = x_ref[...]
        h_ref[...] = (xf * lax.rsqrt(jnp.mean(xf * xf, axis=-1, keepdims=True) + EPS) * g_ref[...]).astype(BF16)

    t = x.shape[0]
    return pl.pallas_call(
        body, name=name, grid=(t // ROW_BLOCK,), in_specs=[_row_spec(), _vec_spec()], out_specs=_row_spec(),
        out_shape=jax.ShapeDtypeStruct((t, D_MODEL), BF16), compiler_params=_cparams(("parallel",)),
    )(x, g)


def _rms_bwd_math(xf, g, dh):
    rstd = lax.rsqrt(jnp.mean(xf * xf, axis=-1, keepdims=True) + EPS)
    xhat = xf * rstd
    dxhat = dh * g
    dx = rstd * (dxhat - xhat * jnp.mean(dxhat * xhat, axis=-1, keepdims=True))
    dg = jnp.sum(dh * xhat, axis=0, keepdims=True)
    return dx, dg


def _rms_bwd(x, g, dh, dres, name):
    def body(x_ref, g_ref, dh_ref, dres_ref, dx_ref, dg_ref):
        dx, dg = _rms_bwd_math(x_ref[...], g_ref[...], dh_ref[...].astype(F32))
        dx_ref[...] = dres_ref[...] + dx

        @pl.when(pl.program_id(0) == 0)
        def _():
            dg_ref[...] = jnp.zeros_like(dg_ref)

        dg_ref[...] += dg

    t = x.shape[0]
    return pl.pallas_call(
        body, name=name, grid=(t // ROW_BLOCK,),
        in_specs=[_row_spec(), _vec_spec(), _row_spec(), _row_spec()], out_specs=[_row_spec(), _vec_spec()],
        out_shape=[jax.ShapeDtypeStruct((t, D_MODEL), F32), jax.ShapeDtypeStruct((1, D_MODEL), F32)],
        compiler_params=_cparams(("arbitrary",)),
    )(x, g, dh, dres)


def _loss_head(x3, g, target):
    def body(x_ref, g_ref, t_ref, dx_ref, dg_ref, loss_ref):
        xf, gg = x_ref[...], g_ref[...]
        rstd = lax.rsqrt(jnp.mean(xf * xf, axis=-1, keepdims=True) + EPS)
        err = xf * rstd * gg - t_ref[...]
        dx, dg = _rms_bwd_math(xf, gg, err * (1.0 / D_MODEL))
        dx_ref[...] = dx

        @pl.when(pl.program_id(0) == 0)
        def _():
            dg_ref[...] = jnp.zeros_like(dg_ref)
            loss_ref[...] = jnp.zeros_like(loss_ref)

        dg_ref[...] += dg
        part = jnp.sum(jnp.sum(err * err, axis=-1, keepdims=True), axis=0, keepdims=True) * (0.5 / D_MODEL)
        loss_ref[...] += jnp.broadcast_to(part, loss_ref.shape)

    t = x3.shape[0]
    return pl.pallas_call(
        body, name="loss_head", grid=(t // ROW_BLOCK,),
        in_specs=[_row_spec(), _vec_spec(), _row_spec()], out_specs=[_row_spec(), _vec_spec(), _vec_spec(LANE)],
        out_shape=[jax.ShapeDtypeStruct((t, D_MODEL), F32), jax.ShapeDtypeStruct((1, D_MODEL), F32),
                   jax.ShapeDtypeStruct((1, LANE), F32)],
        compiler_params=_cparams(("arbitrary",)),
    )(x3, g, target)


def _ple_bwd(dx3, gpre, pp):
    def body(dx_ref, gp_ref, pp_ref, dgp_ref, dpp_ref):
        dx, sg = dx_ref[...], _sigmoid(gp_ref[...])
        dpp_ref[...] = (dx * sg).astype(BF16)
        dgp_ref[...] = (dx * pp_ref[...] * sg * (1.0 - sg)).astype(BF16)

    t = dx3.shape[0]
    return pl.pallas_call(
        body, name="ple_bwd", grid=(t // ROW_BLOCK,), in_specs=[_row_spec()] * 3, out_specs=[_row_spec()] * 2,
        out_shape=[jax.ShapeDtypeStruct((t, D_MODEL), BF16)] * 2, compiler_params=_cparams(("parallel",)),
    )(dx3, gpre, pp)


CONV_COLS = 256


def _shift_down(x, s):
    if s == 0:
        return x
    return jnp.where(_iota2(x.shape, 0) >= s, pltpu.roll(x, s, 0), 0.0)


def _shift_up(x, s):
    if s == 0:
        return x
    rows = x.shape[0]
    return jnp.where(_iota2(x.shape, 0) < rows - s, pltpu.roll(x, rows - s, 0), 0.0)


def _conv_taps(xf):
    return [_shift_down(xf, DN_CONV - 1 - j) for j in range(DN_CONV)]


def _conv_pre(taps, w):
    return sum(tap * w[j:j + 1, :] for j, tap in enumerate(taps))


def _conv_fwd(zb, conv_w, bsz, seq):
    def body(x_ref, w_ref, y_ref):
        y_ref[...] = _silu(_conv_pre(_conv_taps(x_ref[...]), w_ref[...]))

    nblk = DN_QKV // CONV_COLS
    return pl.pallas_call(
        body, name="conv_fwd", grid=(bsz, nblk),
        in_specs=[pl.BlockSpec((seq, CONV_COLS), lambda b, j: (b, ZB_DQKV // CONV_COLS + j)),
                  pl.BlockSpec((DN_CONV, CONV_COLS), lambda b, j: (0, j))],
        out_specs=pl.BlockSpec((seq, CONV_COLS), lambda b, j: (b, j)),
        out_shape=jax.ShapeDtypeStruct((bsz * seq, DN_QKV), F32),
        compiler_params=_cparams(("parallel", "parallel")),
    )(zb, conv_w)


def _conv_bwd(zb, conv_w, dact, bsz, seq):
    def body(x_ref, w_ref, dy_ref, dx_ref, dw_ref):
        taps, w = _conv_taps(x_ref[...]), w_ref[...]
        c = _conv_pre(taps, w)
        sg = _sigmoid(c)
        dc = dy_ref[...].astype(F32) * sg * (1.0 + c * (1.0 - sg))
        dx = sum(_shift_up(dc, DN_CONV - 1 - j) * w[j:j + 1, :] for j in range(DN_CONV))
        dx_ref[...] = dx.astype(BF16)
        dw = jnp.concatenate([jnp.sum(dc * tap, axis=0, keepdims=True) for tap in taps], axis=0)

        @pl.when(pl.program_id(1) == 0)
        def _():
            dw_ref[...] = jnp.zeros_like(dw_ref)

        dw_ref[...] += dw

    nblk = DN_QKV // CONV_COLS
    return pl.pallas_call(
        body, name="conv_bwd", grid=(nblk, bsz),
        in_specs=[pl.BlockSpec((seq, CONV_COLS), lambda j, b: (b, ZB_DQKV // CONV_COLS + j)),
                  pl.BlockSpec((DN_CONV, CONV_COLS), lambda j, b: (0, j)),
                  pl.BlockSpec((seq, CONV_COLS), lambda j, b: (b, j))],
        out_specs=[pl.BlockSpec((seq, CONV_COLS), lambda j, b: (b, j)),
                   pl.BlockSpec((DN_CONV, CONV_COLS), lambda j, b: (0, j))],
        out_shape=[jax.ShapeDtypeStruct((bsz * seq, DN_QKV), BF16), jax.ShapeDtypeStruct((DN_CONV, DN_QKV), F32)],
        compiler_params=_cparams(("parallel", "arbitrary")),
    )(zb, conv_w, dact)


MESH_IDS = pl.DeviceIdType.MESH
ANY_SPEC = pl.BlockSpec(memory_space=pl.ANY)


COMM_SCRATCH = (pltpu.SemaphoreType.DMA((7,)), pltpu.SemaphoreType.DMA((7,)), pltpu.SemaphoreType.DMA)


def _gather_phases(x_ref, out_ref, send_sems, recv_sems, local_sem):
    mx, my, mc = lax.axis_index("x"), lax.axis_index("y"), lax.axis_index("c")
    me, sibling = (mx, my, mc), (mx, my, 1 - mc)
    x_nbr, y_nbr, diag = (1 - mx, my), (mx, 1 - my), (1 - mx, 1 - my)
    north = mc == 1
    relayed = (jnp.where(north, mx, 1 - mx), jnp.where(north, 1 - my, my))
    relay_to = (jnp.where(north, 1 - mx, mx), jnp.where(north, my, 1 - my))

    def slot(px, py, pc):
        return out_ref.at[4 * px + 2 * py + pc]

    def copy(k, block, to, src=None):
        return pltpu.make_async_remote_copy(
            src_ref=slot(*block) if src is None else src, dst_ref=slot(*block),
            send_sem=send_sems.at[k], recv_sem=recv_sems.at[k], device_id=to, device_id_type=MESH_IDS)

    def mine():
        return pltpu.make_async_copy(x_ref, slot(*me), local_sem)

    def own_sends():
        return [copy(0, me, sibling, src=x_ref), copy(1, me, (*x_nbr, mc), src=x_ref),
                copy(2, me, (*y_nbr, mc), src=x_ref)]

    def later_sends():
        return [copy(3, (*relayed, mc), (*relay_to, mc)), copy(4, (*x_nbr, mc), sibling),
                copy(5, (*y_nbr, mc), sibling), copy(6, (*diag, mc), sibling)]

    def start():
        mine().start()
        for cp in own_sends():
            cp.start()

    def forward():
        copy(1, (*x_nbr, mc), me).wait_recv()
        copy(2, (*y_nbr, mc), me).wait_recv()
        for cp in later_sends()[:3]:
            cp.start()

    def forward_late():
        copy(3, (*diag, mc), me).wait_recv()
        later_sends()[3].start()

    def finish():
        copy(0, sibling, me).wait_recv()
        for k, chip in ((4, x_nbr), (5, y_nbr), (6, diag)):
            copy(k, (*chip, 1 - mc), me).wait_recv()
        for cp in own_sends() + later_sends():
            cp.wait_send()
        mine().wait()

    return start, forward, forward_late, finish


def _scatter_phases(x_ref, out_ref, send_sems, recv_sems, local_sem, among_chips=False):
    mx, my, mc = lax.axis_index("x"), lax.axis_index("y"), lax.axis_index("c")
    n_peers = 4 if among_chips else N_DEV
    me = 2 * mx + my if among_chips else 4 * mx + 2 * my + mc

    def peer(k):
        if among_chips:
            return (mx ^ ((k >> 1) & 1), my ^ (k & 1), mc)
        return (mx ^ ((k >> 2) & 1), my ^ ((k >> 1) & 1), mc ^ (k & 1))

    def slot_of(k):
        px, py, pc = peer(k)
        return 2 * px + py if among_chips else 4 * px + 2 * py + pc

    def copy(k, src_slot, dst_slot):
        return pltpu.make_async_remote_copy(
            src_ref=x_ref.at[src_slot], dst_ref=out_ref.at[dst_slot],
            send_sem=send_sems.at[k - 1], recv_sem=recv_sems.at[k - 1],
            device_id=peer(k), device_id_type=MESH_IDS)

    def sends():
        return [copy(k, slot_of(k), me) for k in range(1, n_peers)]

    def mine():
        return pltpu.make_async_copy(x_ref.at[me], out_ref.at[me], local_sem)

    def start():
        mine().start()
        for cp in sends():
            cp.start()

    def finish():
        for k in range(1, n_peers):
            copy(k, me, slot_of(k)).wait_recv()
        for cp in sends():
            cp.wait_send()
        mine().wait()

    return start, _nothing, _nothing, finish


def _pair_phases(x_ref, out_ref, send_sems, recv_sems, local_sem):
    mx, my, mc = lax.axis_index("x"), lax.axis_index("y"), lax.axis_index("c")

    def copy(side):
        return pltpu.make_async_remote_copy(
            src_ref=x_ref.at[:, side], dst_ref=out_ref, send_sem=send_sems.at[0], recv_sem=recv_sems.at[0],
            device_id=(mx, my, 1 - mc), device_id_type=MESH_IDS)

    def start():
        copy(1 - mc).start()

    def finish():
        copy(mc).wait_recv()
        copy(1 - mc).wait_send()

    return start, _nothing, _nothing, finish


def _nothing():
    pass


class _Rider:
    def __init__(self, phases, x, out_shape):
        self.phases, self.x, self.out_shape = phases, x, out_shape


def _gather_rider(x):
    return _Rider(_gather_phases, x, jax.ShapeDtypeStruct((N_DEV,) + x.shape, x.dtype))


def _scatter_rider(x):
    return _Rider(_scatter_phases, x, jax.ShapeDtypeStruct(x.shape, x.dtype))


def _chip_scatter_rider(x):
    return _Rider(functools.partial(_scatter_phases, among_chips=True), x, jax.ShapeDtypeStruct(x.shape, x.dtype))


def _pair_rider(x):
    return _Rider(_pair_phases, x, jax.ShapeDtypeStruct((x.shape[0],) + x.shape[2:], x.dtype))


def _exchange(rider, name):
    def body(x_ref, out_ref, send_sems, recv_sems, local_sem):
        for phase in rider.phases(x_ref, out_ref, send_sems, recv_sems, local_sem):
            phase()

    return pl.pallas_call(body, name=name, out_shape=rider.out_shape, in_specs=[ANY_SPEC], out_specs=ANY_SPEC,
                          scratch_shapes=list(COMM_SCRATCH))(rider.x)


def _all_gather(x, name):
    return _exchange(_gather_rider(x), name)


def _hosted_call(body, rider, *, name, grid, in_specs, out_specs, out_shape, scratch_shapes, args):
    if rider is None:
        return pl.pallas_call(body, name=name, grid=grid, in_specs=in_specs, out_specs=out_specs, out_shape=out_shape,
                              scratch_shapes=scratch_shapes, compiler_params=_cparams(("arbitrary",) * len(grid)))(*args)
    n_in, n_out, n_scr = len(in_specs), len(out_specs), len(scratch_shapes)
    total = math.prod(grid)

    def riding(*refs):
        host_in, x_ref = refs[:n_in], refs[n_in]
        host_out, out_ref = refs[n_in + 1:n_in + 1 + n_out], refs[n_in + 1 + n_out]
        host_scr = refs[n_in + 2 + n_out:n_in + 2 + n_out + n_scr]
        start, forward, forward_late, finish = rider.phases(x_ref, out_ref, *refs[n_in + 2 + n_out + n_scr:])
        step = 0
        for axis, size in enumerate(grid):
            step = step * size + pl.program_id(axis)
        pl.when(step == 0)(start)
        pl.when(step == total // 2)(forward)
        pl.when(step == (3 * total) // 4)(forward_late)
        body(*host_in, *host_out, *host_scr)
        pl.when(step == total - 1)(finish)

    return pl.pallas_call(
        riding, name=name, grid=grid, in_specs=list(in_specs) + [ANY_SPEC], out_specs=list(out_specs) + [ANY_SPEC],
        out_shape=list(out_shape) + [rider.out_shape], scratch_shapes=list(scratch_shapes) + list(COMM_SCRATCH),
        compiler_params=_cparams(("arbitrary",) * len(grid)))(*args, rider.x)


def _adamw_math(w, g, m, v):
    m = ADAM_B1 * m + (1.0 - ADAM_B1) * g
    v = ADAM_B2 * v + (1.0 - ADAM_B2) * jnp.square(g)
    m_hat = m / (1.0 - ADAM_B1 ** ADAM_STEP)
    v_hat = v / (1.0 - ADAM_B2 ** ADAM_STEP)
    delta = -ADAM_LR * (m_hat / (jnp.sqrt(v_hat) + ADAM_EPS) + ADAM_WD * w)
    return delta, m, v


ADAM_ROWS = 128


def _elementwise_tile(rows, cols):
    if rows % ADAM_ROWS == 0:
        return ADAM_ROWS, cols
    return rows, (2 * LANE if cols % (2 * LANE) == 0 else cols)


def _add_blocks(a, b):
    g, rows, cols = a.shape
    tr, tc = _elementwise_tile(rows, cols)

    def body(a_ref, b_ref, o_ref):
        o_ref[...] = (a_ref[...].astype(F32) + b_ref[...].astype(F32)).astype(o_ref.dtype)

    blk = pl.BlockSpec((None, tr, tc), lambda k, i, j: (k, i, j))
    return pl.pallas_call(body, name="add_pair_blocks", grid=(g, rows // tr, cols // tc), in_specs=[blk, blk],
                          out_specs=blk, out_shape=jax.ShapeDtypeStruct(a.shape, a.dtype),
                          compiler_params=_cparams(("parallel", "parallel", "parallel")))(a, b)


def _adamw_reduce(w, m, v, parts, row0, name):
    rows, cols = w.shape
    n_parts = parts.shape[0]
    tr, tc = _elementwise_tile(rows, cols)
    r0 = row0 // tr

    def body(w_ref, m_ref, v_ref, *refs):
        part_refs, (g_ref, d_ref, nm_ref, nv_ref) = refs[:n_parts], refs[n_parts:]
        g = part_refs[0][...].astype(F32)
        for r in part_refs[1:]:
            g = g + r[...].astype(F32)
        delta, nm, nv = _adamw_math(w_ref[...], g, m_ref[...], v_ref[...])
        g_ref[...] = g
        d_ref[...] = delta
        nm_ref[...] = nm
        nv_ref[...] = nv

    blk = pl.BlockSpec((tr, tc), lambda i, j: (i, j))
    part_specs = [pl.BlockSpec((None, tr, tc), functools.partial(lambda i, j, k: (k, r0 + i, j), k=k))
                  for k in range(n_parts)]
    return pl.pallas_call(
        body, name=name, grid=(rows // tr, cols // tc), in_specs=[blk] * 3 + part_specs, out_specs=[blk] * 4,
        out_shape=[jax.ShapeDtypeStruct(w.shape, F32)] * 4, compiler_params=_cparams(("parallel", "parallel")),
    )(w, m, v, *([parts] * n_parts))


def _small_reduce(gathered):
    r = gathered.shape[1]

    def body(g_ref, o_ref):
        g = g_ref[0]
        for k in range(1, N_DEV):
            g = g + g_ref[k]
        o_ref[...] = g

    return pl.pallas_call(body, name="small_grad_reduce", out_shape=jax.ShapeDtypeStruct((r, LANE), F32))(gathered)


def _adamw_small(w, m, v, g):
    def body(w_ref, m_ref, v_ref, g_ref, d_ref, nm_ref, nv_ref):
        d_ref[...], nm_ref[...], nv_ref[...] = _adamw_math(w_ref[...], g_ref[...], m_ref[...], v_ref[...])

    return pl.pallas_call(body, name="adamw_small", out_shape=[jax.ShapeDtypeStruct(w.shape, F32)] * 3)(w, m, v, g)


def _pack_rows(arrays):
    rows = [jnp.pad(a.reshape(-1), (0, -a.size % LANE)).reshape(-1, LANE) for a in arrays]
    out = jnp.concatenate(rows, axis=0)
    return jnp.pad(out, ((0, -out.shape[0] % 8), (0, 0)))


def _unpack_rows(packed, shapes):
    out, r = [], 0
    for shp in shapes:
        size = math.prod(shp)
        nrows = -(-size // LANE)
        out.append(packed[r:r + nrows].reshape(-1)[:size].reshape(shp))
        r += nrows
    return out


def _add_residual(acc, res):
    return (res + acc,)


def _by_cols(g):
    return g.reshape(g.shape[0], N_DEV, -1).transpose(1, 0, 2)


def _from_cols(blocks):
    return blocks.transpose(1, 0, 2).reshape(blocks.shape[1], -1)


ROWS_OUT = D_MODEL // N_DEV

W_IN_SEGMENTS = ((0, WI_LR, "big", 0), (WI_LR, WI_DQKV, "gates", ZS_LR), (WI_DQKV, WI_DA, "big", ZB_DQKV),
                 (WI_DA, WI_DB, "gates", ZS_DA), (WI_DB, WI_GA, "gates", ZS_DB), (WI_GA, D_IN, "big", ZB_GA))


def _w_in_gate_rows(wt):
    parts = []
    for lo, hi, which, _ in W_IN_SEGMENTS:
        if which == "gates":
            parts += [wt[lo:hi], jnp.zeros((LANE - (hi - lo), wt.shape[1]), wt.dtype)]
    return jnp.concatenate(parts, axis=0)


def _fill_gate_rows(dw_in_t, dws_t):
    for lo, hi, which, first in W_IN_SEGMENTS:
        if which == "gates":
            dw_in_t = lax.dynamic_update_slice(dw_in_t, dws_t[first:first + hi - lo], (lo, 0))
    return dw_in_t


def _local_step(x, p, target, w, up_shard, rows_shard):
    bsz, seq, _ = x.shape
    t, nc = bsz * seq, seq // CHUNK
    x0, p2, tgt = x.reshape(t, D_MODEL), p.reshape(t, PLE_DIM), target.reshape(t, D_MODEL)

    h = _rms_fwd(x0, w["g_mix"], "rms_mix")
    zb, row_blocks = _mm(h, w["w_in_t"], tb=True, b_is_w_in_t=True, name="in_proj", rider=_gather_rider(rows_shard))
    zs, small = _mm(h, w["ws_t"], tb=True, name="in_proj_gates", rider=_gather_rider(w["small_shard"]))
    n_w2, n_cv = GLA_LOWRANK * GLA_QK // N_DEV // LANE, DN_CONV * DN_QKV // N_DEV // LANE
    n_pp = PLE_DIM * D_MODEL // N_DEV // LANE
    w = dict(w,
             w2p=jnp.pad(_from_cols(small[:, :n_w2].reshape(N_DEV, GLA_LOWRANK, -1)), ((0, LANE - GLA_LOWRANK), (0, 0))),
             conv=_from_cols(small[:, n_w2:n_w2 + n_cv].reshape(N_DEV, DN_CONV, -1)),
             w_pp=_from_cols(small[:, n_w2 + n_cv:n_w2 + n_cv + n_pp].reshape(N_DEV, PLE_DIM, -1)).astype(BF16))
    act = _conv_fwd(zb, w["conv"], bsz, seq)
    mix_gla, gla_states = _gla_fwd(zb, zs, w["w2p"], w["gla_b"], w["gla_norm"], bsz, nc)
    mixed, dn_states, dn_t_invs, up_blocks = _dn_fwd(act, zb, zs, w["alog_b"], w["dtb_b"], w["dn_norm"], mix_gla,
                                                     bsz, nc, rider=_gather_rider(up_shard))
    w_up = _from_cols(up_blocks)
    w_out = row_blocks[:, :ROWS_OUT].reshape(D_MODEL, D_MODEL)
    w_pg = row_blocks[:, ROWS_OUT:2 * ROWS_OUT].reshape(D_MODEL, D_MODEL)
    w_down = row_blocks[:, 2 * ROWS_OUT:].reshape(D_FF, D_MODEL)
    x1 = _mm(mixed, w_out, epilogue=_add_residual, extras=(x0,), name="out_proj")
    h2 = _rms_fwd(x1, w["g_mlp"], "rms_mlp")
    u, a = _mm(h2, w_up, out_dtypes=(BF16, BF16), name="mlp_up",
               epilogue=lambda acc: (acc, jnp.square(jnp.maximum(acc, 0.0))))
    x2 = _mm(a, w_down, epilogue=_add_residual, extras=(x1,), name="mlp_down")
    h3 = _rms_fwd(x2, w["g_ple"], "rms_ple")
    pp = _mm(p2, w["w_pp"], name="ple_proj")
    gpre, x3 = _mm(h3, w_pg, out_dtypes=(F32, F32), extras=(x2, pp), name="ple_gate",
                   epilogue=lambda acc, res, proj: (acc, res + _sigmoid(acc) * proj))
    dx3, dg_final, loss = _loss_head(x3, w["g_final"], tgt)

    dgpre, dpp = _ple_bwd(dx3, gpre, pp)
    dw_pp = _mm(p2, dpp, ta=True, out_dtypes=(BF16,), name="d_w_ple_proj")
    dw_pg, recv_pp = _mm(h3, dgpre, ta=True, out_dtypes=(BF16,), name="d_w_ple_gate",
                         rider=_scatter_rider(_by_cols(dw_pp)))
    dh3 = _mm(dgpre, w_pg, tb=True, name="d_h_ple")
    dx2, dg_ple = _rms_bwd(x2, w["g_ple"], dh3, dx3, "rms_ple_bwd")
    du = _mm(dx2, w_down, tb=True, out_dtypes=(BF16,), extras=(u,), name="d_mlp_hidden",
             epilogue=lambda acc, uu: (acc * (2.0 * jnp.maximum(uu.astype(F32), 0.0)),))
    dw_down = _mm(a, dx2, ta=True, out_dtypes=(BF16,), name="d_w_down")
    dw_up = _mm(h2, du, ta=True, out_dtypes=(BF16,), name="d_w_up")
    dh2 = _mm(du, w_up, tb=True, name="d_h_mlp")
    dx1, dg_mlp = _rms_bwd(x1, w["g_mlp"], dh2, dx2, "rms_mlp_bwd")
    dmixed = _mm(dx1, w_out, tb=True, out_dtypes=(BF16,), name="d_mixed")
    dw_out = _mm(mixed, dx1, ta=True, out_dtypes=(BF16,), name="d_w_out")

    d_rows = jnp.concatenate([dw_out.reshape(N_DEV, ROWS_OUT, D_MODEL), dw_pg.reshape(N_DEV, ROWS_OUT, D_MODEL),
                              dw_down.reshape(N_DEV, D_FF // N_DEV, D_MODEL)], axis=1)
    (dact, dza, dzb_, dgz, dgb, dal, ddt, ddn, recv_rows) = _dn_bwd(
        act, zb, zs, w["alog_b"], w["dtb_b"], w["dn_norm"], dn_states, dn_t_invs, dmixed, bsz, nc,
        rider=_scatter_rider(d_rows))
    (gdq, gdk, gdv, dgg, dga, dlr, dw2, dgla_b, dgla_norm, recv_up) = _gla_bwd(
        zb, zs, w["w2p"], w["gla_b"], w["gla_norm"], gla_states, dmixed, bsz, nc,
        rider=_scatter_rider(_by_cols(dw_up)))
    dqkv, dconv = _conv_bwd(zb, w["conv"], dact, bsz, seq)
    dzb = jnp.concatenate([gdq, gdk, gdv, dgg, dqkv, dgz, dga, dgb], axis=1)
    dzs = jnp.concatenate([dlr, dza, dzb_], axis=1)
    dw_in_t = _mm(dzb, h, ta=True, out_dtypes=(BF16,), out_is_w_in_t=True, name="d_w_in")
    dws_t = _mm(dzs, h, ta=True, out_dtypes=(BF16,), name="d_w_in_gates")
    by_chip = _fill_gate_rows(dw_in_t, dws_t).reshape(N_DEV // 2, 2, D_IN_SHARD, D_MODEL)
    dh_gates, from_sibling = _mm(dzs, w["ws_t"], name="d_h_mix_gates", rider=_pair_rider(by_chip))
    mine = lax.dynamic_index_in_dim(by_chip, lax.axis_index("c"), axis=1, keepdims=False)
    chip_sums = _add_blocks(mine, from_sibling)
    dh, recv_in = _mm(dzb, w["w_in_t"], b_is_w_in_t=True, epilogue=_add_residual, extras=(dh_gates,), name="d_h_mix",
                      rider=_chip_scatter_rider(chip_sums))
    gx, dg_mix = _rms_bwd(x0, w["g_mix"], dh, dx1, "rms_mix_bwd")

    dgla_w2 = dw2[:, :GLA_LOWRANK, :].transpose(1, 0, 2).reshape(GLA_LOWRANK, GLA_QK)
    return dict(
        loss=loss[0, 0], grad_x=gx.reshape(x.shape), recv_in=recv_in, recv_up=recv_up, recv_rows=recv_rows,
        recv_pp=recv_pp,
        g_mix=dg_mix, gla_b=dgla_b.reshape(1, GLA_QK), gla_norm=dgla_norm, dn_norm=ddn, g_mlp=dg_mlp, g_ple=dg_ple,
        g_final=dg_final, gla_w2=dgla_w2, dn_conv=dconv,
        dn_a_log=dal[:, :DN_HEADS], dn_dt_bias=ddt[:, :DN_HEADS])


def _first_weights(g_mix, w_in, gla_w2, gla_b, gla_norm, dn_conv, dn_a_log, dn_dt_bias, dn_norm, g_mlp, g_ple,
                   w_ple_proj, g_final):
    w_in_t = _all_gather(jnp.swapaxes(w_in[0], 0, 1).astype(BF16), "gather_w_in").reshape(D_IN, D_MODEL)
    return dict(
        w_in_t=w_in_t, ws_t=_w_in_gate_rows(w_in_t),
        small_shard=_pack_rows([gla_w2[0], dn_conv[0], w_ple_proj[0]]),
        alog_b=jnp.pad(dn_a_log, ((0, 0), (0, LANE - DN_HEADS))),
        dtb_b=jnp.pad(dn_dt_bias, ((0, 0), (0, LANE - DN_HEADS))),
        g_mix=g_mix, gla_b=gla_b, gla_norm=gla_norm, dn_norm=dn_norm, g_mlp=g_mlp, g_ple=g_ple,
        g_final=g_final.reshape(1, D_MODEL))


def kernel(x, p, g_mix, w_in, gla_w2, gla_b, gla_norm, dn_conv, dn_a_log, dn_dt_bias, dn_norm, w_out, g_mlp, w_up, w_down, g_ple, w_ple_gate, w_ple_proj, g_final, loss_target, m_g_mix, m_w_in, m_gla_w2, m_gla_b, m_gla_norm, m_dn_conv, m_dn_a_log, m_dn_dt_bias, m_dn_norm, m_w_out, m_g_mlp, m_w_up, m_w_down, m_g_ple, m_w_ple_gate, m_w_ple_proj, m_g_final, v_g_mix, v_w_in, v_gla_w2, v_gla_b, v_gla_norm, v_dn_conv, v_dn_a_log, v_dn_dt_bias, v_dn_norm, v_w_out, v_g_mlp, v_w_up, v_w_down, v_g_ple, v_w_ple_gate, v_w_ple_proj, v_g_final):
    names = ["g_mix", "w_in", "gla_w2", "gla_b", "gla_norm", "dn_conv", "dn_a_log", "dn_dt_bias", "dn_norm", "w_out",
             "g_mlp", "w_up", "w_down", "g_ple", "w_ple_gate", "w_ple_proj", "g_final"]
    ws = dict(zip(names, (g_mix, w_in, gla_w2, gla_b, gla_norm, dn_conv, dn_a_log, dn_dt_bias, dn_norm, w_out, g_mlp,
                          w_up, w_down, g_ple, w_ple_gate, w_ple_proj, g_final)))
    ms = dict(zip(names, (m_g_mix, m_w_in, m_gla_w2, m_gla_b, m_gla_norm, m_dn_conv, m_dn_a_log, m_dn_dt_bias,
                          m_dn_norm, m_w_out, m_g_mlp, m_w_up, m_w_down, m_g_ple, m_w_ple_gate, m_w_ple_proj,
                          m_g_final)))
    vs = dict(zip(names, (v_g_mix, v_w_in, v_gla_w2, v_gla_b, v_gla_norm, v_dn_conv, v_dn_a_log, v_dn_dt_bias,
                          v_dn_norm, v_w_out, v_g_mlp, v_w_up, v_w_down, v_g_ple, v_w_ple_gate, v_w_ple_proj,
                          v_g_final)))
    me = 4 * lax.axis_index("x") + 2 * lax.axis_index("y") + lax.axis_index("c")

    first = _first_weights(g_mix, w_in, gla_w2, gla_b, gla_norm, dn_conv, dn_a_log, dn_dt_bias, dn_norm, g_mlp,
                           g_ple, w_ple_proj, g_final)
    rows_shard = jnp.concatenate([w_out[0], w_ple_gate[0], w_down[0]], axis=0).astype(BF16)
    r = _local_step(x, p[0], loss_target, first, w_up[0].astype(BF16), rows_shard)
    loss = lax.psum(r["loss"], ("x", "y", "c"))

    grads, deltas, new_m, new_v = {}, {}, {}, {}

    def big(name, parts, row0=0):
        g, d, nm, nv = _adamw_reduce(ws[name][0], ms[name][0], vs[name][0], parts, row0, "adamw_" + name)
        grads[name], deltas[name], new_m[name], new_v[name] = g[None], d[None], nm[None], nv[None]

    t_outs = _adamw_reduce(*[jnp.swapaxes(d["w_in"][0], 0, 1) for d in (ws, ms, vs)], r["recv_in"], 0, "adamw_w_in")
    grads["w_in"], deltas["w_in"], new_m["w_in"], new_v["w_in"] = [jnp.swapaxes(o, 0, 1)[None] for o in t_outs]
    big("w_up", r["recv_up"])
    big("w_ple_proj", r["recv_pp"])
    big("w_out", r["recv_rows"], 0)
    big("w_ple_gate", r["recv_rows"], ROWS_OUT)
    big("w_down", r["recv_rows"], 2 * ROWS_OUT)

    vec_names = ["g_mix", "gla_b", "gla_norm", "dn_norm", "g_mlp", "g_ple", "g_final", "dn_a_log", "dn_dt_bias"]
    gathered_names = vec_names + ["gla_w2", "dn_conv"]
    total = _small_reduce(_all_gather(_pack_rows([r[n] for n in gathered_names]), "gather_small_grads"))
    sg = dict(zip(gathered_names, _unpack_rows(total, [r[n].shape for n in gathered_names])))
    sg["g_final"] = sg["g_final"].reshape(D_MODEL)
    sg["gla_w2"] = lax.dynamic_slice_in_dim(sg["gla_w2"], me * (GLA_QK // N_DEV), GLA_QK // N_DEV, axis=1)
    sg["dn_conv"] = lax.dynamic_slice_in_dim(sg["dn_conv"], me * (DN_QKV // N_DEV), DN_QKV // N_DEV, axis=1)
    small_names = gathered_names
    shapes = [ws[n].shape for n in small_names]
    d_s, m_s, v_s = _adamw_small(_pack_rows([ws[n] for n in small_names]), _pack_rows([ms[n] for n in small_names]),
                                 _pack_rows([vs[n] for n in small_names]), _pack_rows([sg[n] for n in small_names]))
    for n, d, nm, nv in zip(small_names, _unpack_rows(d_s, shapes), _unpack_rows(m_s, shapes), _unpack_rows(v_s, shapes)):
        grads[n], deltas[n], new_m[n], new_v[n] = sg[n].reshape(ws[n].shape), d, nm, nv

    return (loss, r["grad_x"], *[grads[n] for n in names], *[deltas[n] for n in names],
            *[new_m[n] for n in names], *[new_v[n] for n in names])
```

```python
import functools
import math

import jax
import jax.numpy as jnp
from jax import lax
from jax.experimental import pallas as pl
from jax.experimental.pallas import tpu as pltpu

F32 = jnp.float32
BF16 = jnp.bfloat16

N_DEV = 8
D_MODEL = 2048
CHUNK = 64
PLE_DIM = 256
EPS = 1e-6
GLA_HEADS = 4
GLA_DK = 256
GLA_DV = 512
GLA_LOWRANK = 16
GLA_TAU = 16.0
DN_HEADS = 16
DN_D = 128
DN_CONV = 4
D_FF = 4 * D_MODEL
GLA_QK = GLA_HEADS * GLA_DK
GLA_V = GLA_HEADS * GLA_DV
DN_QKV = 3 * DN_HEADS * DN_D
D_IN = 2 * GLA_QK + 2 * GLA_V + GLA_LOWRANK + DN_QKV + D_MODEL + 2 * DN_HEADS + 2 * D_MODEL
D_IN_SHARD = D_IN // N_DEV

ADAM_LR = 0.001
ADAM_B1 = 0.9
ADAM_B2 = 0.999
ADAM_EPS = 1e-08
ADAM_WD = 0.01
ADAM_STEP = 10

LANE = 128
ZB_GQ, ZB_GK, ZB_GV, ZB_GG = 0, 1024, 2048, 4096
ZB_DQKV, ZB_DZ, ZB_GA, ZB_GB = 6144, 12288, 14336, 16384
ZB_W = 18432
ZS_LR, ZS_DA, ZS_DB = 0, 128, 256
ZS_W = 384
WI_LR = 2 * GLA_QK + 2 * GLA_V
WI_DQKV = WI_LR + GLA_LOWRANK
WI_DA = WI_DQKV + DN_QKV + D_MODEL
WI_DB = WI_DA + DN_HEADS
WI_GA = WI_DB + DN_HEADS

VMEM_LIMIT = 56 * 1024 * 1024

def _bdot(a, b, dims):
    return lax.dot_general(a.astype(BF16), b.astype(BF16), dims, preferred_element_type=F32)


def _split3(x):
    hi = x.astype(BF16)
    rest = x - hi.astype(F32)
    mid = rest.astype(BF16)
    return hi, mid, (rest - mid.astype(F32)).astype(BF16)


def _dot01(m, x, dims):
    m = m.astype(BF16)
    out = None
    for piece in _split3(x):
        d = lax.dot_general(m, piece, dims, preferred_element_type=F32)
        out = d if out is None else out + d
    return out


@functools.partial(jax.custom_vjp, nondiff_argnums=(2, 3))
def _left_dot(m, x, dims, dims_t):
    return _dot01(m, x, dims)


def _left_dot_fwd(m, x, dims, dims_t):
    return _dot01(m, x, dims), m


def _left_dot_bwd(dims, dims_t, m, ct):
    return jnp.zeros_like(m), _dot01(m, ct, dims_t)


_left_dot.defvjp(_left_dot_fwd, _left_dot_bwd)


def _dot3(a, b, dims):
    ah, bh = a.astype(BF16), b.astype(BF16)
    al, bl = (a - ah.astype(F32)).astype(BF16), (b - bh.astype(F32)).astype(BF16)
    dot = functools.partial(lax.dot_general, dimension_numbers=dims, preferred_element_type=F32)
    return dot(ah, bh) + (dot(ah, bl) + dot(al, bh))


def _sigmoid(x):
    return 1.0 / (1.0 + jnp.exp(-x))


def _silu(x):
    return x * _sigmoid(x)


def _softplus(x):
    return jnp.maximum(x, 0.0) + jnp.log(1.0 + jnp.exp(-jnp.abs(x)))


def _iota2(shape, dim):
    return lax.broadcasted_iota(jnp.int32, shape, dim)


def _cparams(sem=None):
    return pltpu.CompilerParams(dimension_semantics=sem, vmem_limit_bytes=VMEM_LIMIT)


BNN = (((2,), (1,)), ((0,), (0,)))
BNT = (((2,), (2,)), ((0,), (0,)))
BTN = (((1,), (1,)), ((0,), (0,)))


def _gla_chunk(st, q, k, v, lr, w2, b, gg, ga, gn):
    hb, c, _ = q.shape
    incl = (_iota2((c, c), 0) >= _iota2((c, c), 1))[None]
    tri = jnp.broadcast_to(incl.astype(F32), (hb, c, c))
    q = q.astype(F32) * (GLA_DK ** -0.5)
    k = k.astype(F32)
    v = v.astype(F32)
    lr_b = jnp.broadcast_to(lr[None], (hb,) + lr.shape)
    lf = -_softplus(-(_bdot(lr_b, w2, BNN) + b)) / GLA_TAU
    bcum = _left_dot(tri, lf, BNN, BTN)
    blast = jnp.sum(lf, axis=1, keepdims=True)
    q_in = q * jnp.exp(bcum)
    k_in = k * jnp.exp(-bcum)
    a = jnp.where(incl, _bdot(q_in, k_in, BNT), 0.0)
    o = _bdot(a, v, BNN) + _bdot(q_in, st, BNT)
    k_dec = k * jnp.exp(blast - bcum)
    st_new = st * jnp.exp(blast) + _bdot(v, k_dec, BTN)
    on = o * lax.rsqrt(jnp.mean(o * o, axis=-1, keepdims=True) + EPS) * gn
    res = _sigmoid(ga.astype(F32)) * on * _silu(gg.astype(F32))
    return res, st_new


def _tri_inv_raw(a):
    _, c, _ = a.shape
    eye = (_iota2((c, c), 0) == _iota2((c, c), 1)).astype(F32)[None]
    x = a
    p = eye - a
    for _ in range(5):
        x = _dot3(x, x, BNN)
        p = _dot3(p, eye + x, BNN)
    return p


def _tri_inv_bwd(t, dt):
    return (-_dot3(_dot3(t, dt, BTN), t, BNT),)


@jax.custom_vjp
def _tri_inv_given(a, t):
    return t


def _tri_inv_given_fwd(a, t):
    return t, t


def _tri_inv_given_bwd(t, dt):
    return _tri_inv_bwd(t, dt) + (jnp.zeros_like(t),)


_tri_inv_given.defvjp(_tri_inv_given_fwd, _tri_inv_given_bwd)


@functools.partial(jax.custom_vjp, nondiff_argnums=(1, 2))
def _column_on_lanes(z, j, width):
    picked = jnp.where(_iota2(z.shape, 1) == j, z, 0.0)
    return jnp.broadcast_to(jnp.sum(picked, axis=1, keepdims=True), (z.shape[0], width))


def _column_on_lanes_fwd(z, j, width):
    return _column_on_lanes(z, j, width), None


def _column_on_lanes_bwd(j, width, _, ct):
    shape = (ct.shape[0], LANE)
    total = jnp.broadcast_to(jnp.sum(ct, axis=1, keepdims=True), shape)
    return (jnp.where(_iota2(shape, 1) == j, total, 0.0),)


_column_on_lanes.defvjp(_column_on_lanes_fwd, _column_on_lanes_bwd)


def _dn_chunk(s, qr, kr, vr, za, zb, alog, dtb, gz, gb, dn, t_saved=None):
    hb, c, _ = qr.shape
    row, col = _iota2((c, c), 0), _iota2((c, c), 1)
    incl = (row >= col)[None]
    strict = (row > col)[None]
    tri = jnp.broadcast_to(incl.astype(F32), (hb, c, c))

    def l2n(t):
        return t * lax.rsqrt(jnp.sum(t * t, axis=-1, keepdims=True) + EPS)

    q = l2n(qr.astype(F32)) * (DN_D ** -0.5)
    k = l2n(kr.astype(F32))
    v = vr.astype(F32)
    g_heads = -jnp.exp(alog) * _softplus(za + dtb)
    beta_heads = _sigmoid(zb)
    def per_head(z, width):
        return jnp.concatenate([_column_on_lanes(z, j, width)[None] for j in range(hb)], axis=0)

    g = per_head(g_heads, LANE)
    beta = per_head(beta_heads, LANE)
    gcum = _left_dot(tri, g, BNN, BTN)
    glast = jnp.sum(g, axis=1, keepdims=True)
    diff = _left_dot(tri, per_head(g_heads, c) * strict.astype(F32), BNN, BTN)
    dec = jnp.exp(jnp.where(incl, diff, -1e30))
    kb = k * beta
    a = jnp.where(strict, _bdot(kb, k, BNT) * dec, 0.0)
    t = _tri_inv_raw(a) if t_saved is None else _tri_inv_given(a, t_saved)
    egc = jnp.exp(gcum)
    u = _bdot(t, v * beta, BNN)
    w = _bdot(t, kb * egc, BNN)
    attn = jnp.where(incl, _bdot(q, k, BNT) * dec, 0.0)
    q_dec = q * egc
    k_dec = k * jnp.exp(glast - gcum)
    v_new = u - _bdot(w, s, BNN)
    o = _bdot(q_dec, s, BNN) + _bdot(attn, v_new, BNN)
    s_new = s * jnp.exp(glast) + _bdot(k_dec, v_new, BTN)
    on = o * lax.rsqrt(jnp.mean(o * o, axis=-1, keepdims=True) + EPS) * dn
    res = _sigmoid(gb.astype(F32)) * on * _silu(gz.astype(F32))
    return (res, s_new, t) if t_saved is None else (res, s_new)


def _heads(ref, n_heads, width):
    return jnp.stack([ref[:, j * width:(j + 1) * width] for j in range(n_heads)], axis=0)


def _gla_specs(nc, reverse):
    def rows(b, n):
        return b * nc + ((nc - 1 - n) if reverse else n)

    qk = lambda base: pl.BlockSpec((CHUNK, GLA_QK), lambda b, n: (rows(b, n), base // GLA_QK))
    vv = lambda base: pl.BlockSpec((CHUNK, GLA_V), lambda b, n: (rows(b, n), base // GLA_V))
    lr = lambda c: pl.BlockSpec((CHUNK, LANE), lambda b, n: (rows(b, n), c))
    full = lambda shape: pl.BlockSpec(shape, lambda b, n: (0,) * len(shape))
    return rows, qk, vv, lr, full


def _gla_inputs(q_ref, k_ref, v_ref, gg_ref, ga_ref, lr_ref, w2_ref, b_ref, gn_ref):
    return (_heads(q_ref, GLA_HEADS, GLA_DK), _heads(k_ref, GLA_HEADS, GLA_DK), _heads(v_ref, GLA_HEADS, GLA_DV),
            lr_ref[...], _heads(w2_ref, GLA_HEADS, GLA_DK), _heads(b_ref, GLA_HEADS, GLA_DK),
            _heads(gg_ref, GLA_HEADS, GLA_DV), _heads(ga_ref, GLA_HEADS, GLA_DV), gn_ref[...])


def _gla_fwd(zb, zs, w2p, gla_b, gla_norm, bsz, nc):
    t = zb.shape[0]
    rows, qk, vv, lr, full = _gla_specs(nc, False)

    def body(q_ref, k_ref, v_ref, gg_ref, ga_ref, lr_ref, w2_ref, b_ref, gn_ref, o_ref, st_ref, state):
        st = jnp.where(pl.program_id(1) > 0, state[...], 0.0)
        st_ref[0, 0] = st
        res, st_new = _gla_chunk(st, *_gla_inputs(q_ref, k_ref, v_ref, gg_ref, ga_ref, lr_ref, w2_ref, b_ref, gn_ref))
        for j in range(GLA_HEADS):
            o_ref[:, j * GLA_DV:(j + 1) * GLA_DV] = res[j]
        state[...] = st_new

    return pl.pallas_call(
        body, name="gla_fwd", grid=(bsz, nc),
        in_specs=[qk(ZB_GQ), qk(ZB_GK), vv(ZB_GV), vv(ZB_GG), vv(ZB_GA), lr(ZS_LR // LANE),
                  full((LANE, GLA_QK)), full((1, GLA_QK)), full((1, GLA_DV))],
        out_specs=[vv(0), pl.BlockSpec((1, 1, GLA_HEADS, GLA_DV, GLA_DK), lambda b, n: (b, n, 0, 0, 0))],
        out_shape=[jax.ShapeDtypeStruct((t, GLA_V), F32),
                   jax.ShapeDtypeStruct((bsz, nc, GLA_HEADS, GLA_DV, GLA_DK), F32)],
        scratch_shapes=[pltpu.VMEM((GLA_HEADS, GLA_DV, GLA_DK), F32)],
        compiler_params=_cparams(("arbitrary", "arbitrary")),
    )(zb, zb, zb, zb, zb, zs, w2p, gla_b, gla_norm)


def _gla_bwd(zb, zs, w2p, gla_b, gla_norm, states, dmixed, bsz, nc, rider=None):
    t = zb.shape[0]
    rows, qk, vv, lr, full = _gla_specs(nc, True)

    def body(q_ref, k_ref, v_ref, gg_ref, ga_ref, lr_ref, w2_ref, b_ref, gn_ref, st_ref, dm_ref,
             dq_ref, dk_ref, dv_ref, dgg_ref, dga_ref, dlr_ref, dw2_ref, db_ref, dgn_ref, dstate):
        b, n = pl.program_id(0), pl.program_id(1)

        @pl.when((b == 0) & (n == 0))
        def _():
            dw2_ref[...] = jnp.zeros_like(dw2_ref)
            db_ref[...] = jnp.zeros_like(db_ref)
            dgn_ref[...] = jnp.zeros_like(dgn_ref)

        _, vjp = jax.vjp(_gla_chunk, st_ref[0, 0],
                         *_gla_inputs(q_ref, k_ref, v_ref, gg_ref, ga_ref, lr_ref, w2_ref, b_ref, gn_ref))
        dst_in = jnp.where(n > 0, dstate[...], 0.0)
        dst, dq, dk, dv, dlr, dw2, db, dgg, dga, dgn = vjp((_heads(dm_ref, GLA_HEADS, GLA_DV).astype(F32), dst_in))
        dstate[...] = dst
        for j in range(GLA_HEADS):
            dq_ref[:, j * GLA_DK:(j + 1) * GLA_DK] = dq[j].astype(dq_ref.dtype)
            dk_ref[:, j * GLA_DK:(j + 1) * GLA_DK] = dk[j].astype(dk_ref.dtype)
            dv_ref[:, j * GLA_DV:(j + 1) * GLA_DV] = dv[j].astype(dv_ref.dtype)
            dgg_ref[:, j * GLA_DV:(j + 1) * GLA_DV] = dgg[j].astype(dgg_ref.dtype)
            dga_ref[:, j * GLA_DV:(j + 1) * GLA_DV] = dga[j].astype(dga_ref.dtype)
        dlr_ref[...] = dlr
        dw2_ref[...] += dw2
        db_ref[...] += db
        dgn_ref[...] += dgn

    return _hosted_call(
        body, rider, name="gla_bwd", grid=(bsz, nc),
        in_specs=[qk(ZB_GQ), qk(ZB_GK), vv(ZB_GV), vv(ZB_GG), vv(ZB_GA), lr(ZS_LR // LANE),
                  full((LANE, GLA_QK)), full((1, GLA_QK)), full((1, GLA_DV)),
                  pl.BlockSpec((1, 1, GLA_HEADS, GLA_DV, GLA_DK), lambda b, n: (b, nc - 1 - n, 0, 0, 0)),
                  vv(0)],
        out_specs=[qk(0), qk(0), vv(0), vv(0), vv(0), lr(0),
                   full((GLA_HEADS, LANE, GLA_DK)), full((GLA_HEADS, 1, GLA_DK)), full((1, GLA_DV))],
        out_shape=[jax.ShapeDtypeStruct((t, GLA_QK), BF16), jax.ShapeDtypeStruct((t, GLA_QK), BF16),
                   jax.ShapeDtypeStruct((t, GLA_V), BF16), jax.ShapeDtypeStruct((t, GLA_V), BF16),
                   jax.ShapeDtypeStruct((t, GLA_V), BF16), jax.ShapeDtypeStruct((t, LANE), F32),
                   jax.ShapeDtypeStruct((GLA_HEADS, LANE, GLA_DK), F32),
                   jax.ShapeDtypeStruct((GLA_HEADS, 1, GLA_DK), F32),
                   jax.ShapeDtypeStruct((1, GLA_DV), F32)],
        scratch_shapes=[pltpu.VMEM((GLA_HEADS, GLA_DV, GLA_DK), F32)],
        args=(zb, zb, zb, zb, zb, zs, w2p, gla_b, gla_norm, states, dmixed))


DN_HB = DN_HEADS


def _dn_specs(nc, reverse):
    wide = DN_HB * DN_D

    def rows(b, n, h):
        return b * nc + ((nc - 1 - n) if reverse else n)

    def col(base):
        return pl.BlockSpec((CHUNK, wide), lambda b, n, h: (rows(b, n, h), base // wide + h))

    def fixed(c):
        return pl.BlockSpec((CHUNK, LANE), lambda b, n, h: (rows(b, n, h), c))

    head = pl.BlockSpec((1, LANE), lambda b, n, h: (0, 0))
    return rows, col, fixed, head


def _lanes(j):
    return slice(j * DN_D, (j + 1) * DN_D)


def _by_head(ref):
    return jnp.stack([ref[:, _lanes(j)] for j in range(DN_HB)], axis=0)


def _dn_fwd(act, zb, zs, alog_b, dtb_b, dn_norm, mix_gla, bsz, nc, rider=None):
    t = zb.shape[0]
    rows, col, fixed, head = _dn_specs(nc, False)

    def body(q_ref, k_ref, v_ref, za_ref, zb_ref, al_ref, dt_ref, gz_ref, gb_ref, dn_ref, mg_ref,
             o_ref, st_ref, ti_ref, state):
        s = jnp.where(pl.program_id(1) > 0, state[...], 0.0)
        st_ref[0, 0] = s
        res, s_new, t_inv = _dn_chunk(s, _by_head(q_ref), _by_head(k_ref), _by_head(v_ref), za_ref[...], zb_ref[...],
                                      al_ref[...], dt_ref[...], _by_head(gz_ref), _by_head(gb_ref), dn_ref[...])
        ti_ref[0, 0] = t_inv
        for j in range(DN_HB):
            o_ref[:, _lanes(j)] = (res[j] + mg_ref[:, _lanes(j)]).astype(o_ref.dtype)
        state[...] = s_new

    return _hosted_call(
        body, rider, name="dn_fwd", grid=(bsz, nc, DN_HEADS // DN_HB),
        in_specs=[col(0), col(DN_HEADS * DN_D), col(2 * DN_HEADS * DN_D),
                  fixed(ZS_DA // LANE), fixed(ZS_DB // LANE), head, head,
                  col(ZB_DZ), col(ZB_GB), pl.BlockSpec((1, DN_D), lambda b, n, h: (0, 0)), col(0)],
        out_specs=[col(0), pl.BlockSpec((1, 1, DN_HB, DN_D, DN_D), lambda b, n, h: (b, n, h, 0, 0)),
                   pl.BlockSpec((1, 1, DN_HB, CHUNK, CHUNK), lambda b, n, h: (b, n, h, 0, 0))],
        out_shape=[jax.ShapeDtypeStruct((t, D_MODEL), BF16),
                   jax.ShapeDtypeStruct((bsz, nc, DN_HEADS, DN_D, DN_D), F32),
                   jax.ShapeDtypeStruct((bsz, nc, DN_HEADS, CHUNK, CHUNK), F32)],
        scratch_shapes=[pltpu.VMEM((DN_HEADS, DN_D, DN_D), F32)],
        args=(act, act, act, zs, zs, alog_b, dtb_b, zb, zb, dn_norm, mix_gla))


def _dn_bwd(act, zb, zs, alog_b, dtb_b, dn_norm, states, t_invs, dmixed, bsz, nc, rider=None):
    t = zb.shape[0]
    rows, col, fixed, head = _dn_specs(nc, True)

    def body(q_ref, k_ref, v_ref, za_ref, zb_ref, al_ref, dt_ref, gz_ref, gb_ref, dn_ref, st_ref, ti_ref, dm_ref,
             dact_ref, dza_ref, dzb_ref, dgz_ref, dgb_ref, dal_ref, ddt_ref, ddn_ref, dstate):
        b, n = pl.program_id(0), pl.program_id(1)

        @pl.when((b == 0) & (n == 0))
        def _():
            dal_ref[...] = jnp.zeros_like(dal_ref)
            ddt_ref[...] = jnp.zeros_like(ddt_ref)
            ddn_ref[...] = jnp.zeros_like(ddn_ref)

        fn = functools.partial(_dn_chunk, t_saved=ti_ref[0, 0])
        _, vjp = jax.vjp(fn, st_ref[0, 0], _by_head(q_ref), _by_head(k_ref), _by_head(v_ref), za_ref[...],
                         zb_ref[...], al_ref[...], dt_ref[...], _by_head(gz_ref), _by_head(gb_ref), dn_ref[...])
        ds_in = jnp.where(n > 0, dstate[...], 0.0)
        ds, dq, dk, dv, dza, dzb, dal, ddt, dgz, dgb, ddn = vjp((_by_head(dm_ref).astype(F32), ds_in))
        dstate[...] = ds
        for j in range(DN_HB):
            for part, d in enumerate((dq, dk, dv)):
                dact_ref[:, pl.ds(part * DN_HEADS * DN_D + j * DN_D, DN_D)] = d[j]
            dgz_ref[:, _lanes(j)] = dgz[j].astype(dgz_ref.dtype)
            dgb_ref[:, _lanes(j)] = dgb[j].astype(dgb_ref.dtype)
        dal_ref[...] += dal
        ddt_ref[...] += ddt
        dza_ref[...] = dza
        dzb_ref[...] = dzb
        ddn_ref[...] += ddn

    full = lambda shape: pl.BlockSpec(shape, lambda b, n, h: (0,) * len(shape))
    return _hosted_call(
        body, rider, name="dn_bwd", grid=(bsz, nc, DN_HEADS // DN_HB),
        in_specs=[col(0), col(DN_HEADS * DN_D), col(2 * DN_HEADS * DN_D),
                  fixed(ZS_DA // LANE), fixed(ZS_DB // LANE), head, head,
                  col(ZB_DZ), col(ZB_GB), pl.BlockSpec((1, DN_D), lambda b, n, h: (0, 0)),
                  pl.BlockSpec((1, 1, DN_HB, DN_D, DN_D), lambda b, n, h: (b, nc - 1 - n, h, 0, 0)),
                  pl.BlockSpec((1, 1, DN_HB, CHUNK, CHUNK), lambda b, n, h: (b, nc - 1 - n, h, 0, 0)), col(0)],
        out_specs=[pl.BlockSpec((CHUNK, DN_QKV), lambda b, n, h: (rows(b, n, h), 0)), fixed(0), fixed(0), col(0), col(0),
                   full((1, LANE)), full((1, LANE)), full((1, DN_D))],
        out_shape=[jax.ShapeDtypeStruct((t, DN_QKV), F32),
                   jax.ShapeDtypeStruct((t, LANE), F32), jax.ShapeDtypeStruct((t, LANE), F32),
                   jax.ShapeDtypeStruct((t, D_MODEL), BF16), jax.ShapeDtypeStruct((t, D_MODEL), BF16),
                   jax.ShapeDtypeStruct((1, LANE), F32), jax.ShapeDtypeStruct((1, LANE), F32),
                   jax.ShapeDtypeStruct((1, DN_D), F32)],
        scratch_shapes=[pltpu.VMEM((DN_HEADS, DN_D, DN_D), F32)],
        args=(act, act, act, zs, zs, alog_b, dtb_b, zb, zb, dn_norm, states, t_invs, dmixed))


MM_VMEM_BUDGET = 40 * 1024 * 1024
MM_TILE_PREF = (1024, 1024, 2048)


def _divisor_tile(n, cap):
    if n <= cap:
        return n
    for c in range(cap - cap % LANE, 0, -LANE):
        if n % c == 0:
            return c
    return n


def _mm_tiles(m, n, kd, a_bytes, b_bytes, mn_bytes):
    tm, tn, tk = (_divisor_tile(d, c) for d, c in zip((m, n, kd), MM_TILE_PREF))

    def need(tm, tn, tk):
        acc = 0 if tk == kd else 4 * tm * tn
        return 2 * (tm * tk * a_bytes + tk * tn * b_bytes + tm * tn * mn_bytes) + acc + 4 * tm * tn

    while need(tm, tn, tk) > MM_VMEM_BUDGET:
        if tk > 512 and tk * max(tm * a_bytes, tn * b_bytes) >= tm * tn * mn_bytes:
            tk = _divisor_tile(kd, tk // 2)
        elif tn >= tm and tn > LANE:
            tn = _divisor_tile(n, tn // 2)
        else:
            tm = _divisor_tile(m, tm // 2)
    return tm, tn, tk


def _w_in_row_of(tile, tile_rows):
    skipped = jnp.where(tile >= ZB_GA // tile_rows, WI_GA - ZB_GA, jnp.where(tile >= ZB_DQKV // tile_rows,
                                                                             WI_DQKV - ZB_DQKV, 0))
    return pl.multiple_of(tile * tile_rows + skipped, 16)


def _mm(a, b, *, ta=False, tb=False, out_dtypes=(F32,), epilogue=None, extras=(), name, rider=None,
        b_is_w_in_t=False, out_is_w_in_t=False):
    m, kd = (a.shape[1], a.shape[0]) if ta else a.shape
    n = b.shape[0] if tb else b.shape[1]
    if b_is_w_in_t:
        n, kd = (ZB_W, kd) if tb else (n, ZB_W)
    mn_bytes = sum(e.dtype.itemsize for e in extras) + sum(jnp.dtype(dt).itemsize for dt in out_dtypes)
    tm, tn, tk = _mm_tiles(m, n, kd, a.dtype.itemsize, b.dtype.itemsize, mn_bytes)
    nk = kd // tk
    n_ex = len(extras)
    dims = (((0,) if ta else (1,), (1,) if tb else (0,)), ((), ()))

    def finish(acc, ex_refs, out_refs):
        outs = (acc,) if epilogue is None else epilogue(acc, *[r[...] for r in ex_refs])
        for r, o in zip(out_refs, outs):
            r[...] = o.astype(r.dtype)

    def partial_product(a_ref, b_ref):
        return lax.dot_general(a_ref[...].astype(BF16), b_ref[...].astype(BF16), dims, preferred_element_type=F32)

    def body_single(*refs):
        finish(partial_product(refs[0], refs[1]), refs[2:2 + n_ex], refs[2 + n_ex:])

    def body_acc(*refs):
        acc = refs[-1]
        k = pl.program_id(2)

        @pl.when(k == 0)
        def _():
            acc[...] = partial_product(refs[0], refs[1])

        @pl.when(k > 0)
        def _():
            acc[...] += partial_product(refs[0], refs[1])

        @pl.when(k == nk - 1)
        def _():
            finish(acc[...], refs[2:2 + n_ex], refs[2 + n_ex:-1])

    a_spec = pl.BlockSpec((tk, tm), lambda i, j, k: (k, i)) if ta else pl.BlockSpec((tm, tk), lambda i, j, k: (i, k))
    b_spec = pl.BlockSpec((tn, tk), lambda i, j, k: (j, k)) if tb else pl.BlockSpec((tk, tn), lambda i, j, k: (k, j))
    mn_spec = pl.BlockSpec((tm, tn), lambda i, j, k: (i, j))
    out_spec, out_rows = mn_spec, m
    if b_is_w_in_t and tb:
        b_spec = pl.BlockSpec((pl.Element(tn), pl.Element(tk)),
                              lambda i, j, k: (_w_in_row_of(j, tn), pl.multiple_of(k * tk, LANE)))
    elif b_is_w_in_t:
        b_spec = pl.BlockSpec((pl.Element(tk), pl.Element(tn)),
                              lambda i, j, k: (_w_in_row_of(k, tk), pl.multiple_of(j * tn, LANE)))
    if out_is_w_in_t:
        out_spec, out_rows = pl.BlockSpec((pl.Element(tm), pl.Element(tn)),
                                          lambda i, j, k: (_w_in_row_of(i, tm), pl.multiple_of(j * tn, LANE))), D_IN
    outs = _hosted_call(
        body_single if nk == 1 else body_acc, rider, name=name, grid=(m // tm, n // tn, nk),
        in_specs=[a_spec, b_spec] + [mn_spec] * n_ex,
        out_specs=[out_spec] * len(out_dtypes),
        out_shape=[jax.ShapeDtypeStruct((out_rows, n), dt) for dt in out_dtypes],
        scratch_shapes=[] if nk == 1 else [pltpu.VMEM((tm, tn), F32)],
        args=(a, b, *extras))
    return outs[0] if len(outs) == 1 else outs


ROW_BLOCK = 256


def _row_spec(width=D_MODEL):
    return pl.BlockSpec((ROW_BLOCK, width), lambda i: (i, 0))


def _vec_spec(width=D_MODEL):
    return pl.BlockSpec((1, width), lambda i: (0, 0))


def _rms_fwd(x, g, name):
    def body(x_ref, g_ref, h_ref):
        xf = x_ref[...]
        h_ref[...] = (xf * lax.rsqrt(jnp.mean(xf * xf, axis=-1, keepdims=True) + EPS) * g_ref[...]).astype(BF16)

    t = x.shape[0]
    return pl.pallas_call(
        body, name=name, grid=(t // ROW_BLOCK,), in_specs=[_row_spec(), _vec_spec()], out_specs=_row_spec(),
        out_shape=jax.ShapeDtypeStruct((t, D_MODEL), BF16), compiler_params=_cparams(("parallel",)),
    )(x, g)


def _rms_bwd_math(xf, g, dh):
    rstd = lax.rsqrt(jnp.mean(xf * xf, axis=-1, keepdims=True) + EPS)
    xhat = xf * rstd
    dxhat = dh * g
    dx = rstd * (dxhat - xhat * jnp.mean(dxhat * xhat, axis=-1, keepdims=True))
    dg = jnp.sum(dh * xhat, axis=0, keepdims=True)
    return dx, dg


def _rms_bwd(x, g, dh, dres, name):
    def body(x_ref, g_ref, dh_ref, dres_ref, dx_ref, dx16_ref, dg_ref):
        dx, dg = _rms_bwd_math(x_ref[...], g_ref[...], dh_ref[...].astype(F32))
        total = dres_ref[...] + dx
        dx_ref[...] = total
        dx16_ref[...] = total.astype(BF16)

        @pl.when(pl.program_id(0) == 0)
        def _():
            dg_ref[...] = jnp.zeros_like(dg_ref)

        dg_ref[...] += dg

    t = x.shape[0]
    return pl.pallas_call(
        body, name=name, grid=(t // ROW_BLOCK,),
        in_specs=[_row_spec(), _vec_spec(), _row_spec(), _row_spec()],
        out_specs=[_row_spec(), _row_spec(), _vec_spec()],
        out_shape=[jax.ShapeDtypeStruct((t, D_MODEL), F32), jax.ShapeDtypeStruct((t, D_MODEL), BF16),
                   jax.ShapeDtypeStruct((1, D_MODEL), F32)],
        compiler_params=_cparams(("arbitrary",)),
    )(x, g, dh, dres)


def _loss_head(x3, g, target):
    def body(x_ref, g_ref, t_ref, dx_ref, dg_ref, loss_ref):
        xf, gg = x_ref[...], g_ref[...]
        rstd = lax.rsqrt(jnp.mean(xf * xf, axis=-1, keepdims=True) + EPS)
        err = xf * rstd * gg - t_ref[...]
        dx, dg = _rms_bwd_math(xf, gg, err * (1.0 / D_MODEL))
        dx_ref[...] = dx

        @pl.when(pl.program_id(0) == 0)
        def _():
            dg_ref[...] = jnp.zeros_like(dg_ref)
            loss_ref[...] = jnp.zeros_like(loss_ref)

        dg_ref[...] += dg
        part = jnp.sum(jnp.sum(err * err, axis=-1, keepdims=True), axis=0, keepdims=True) * (0.5 / D_MODEL)
        loss_ref[...] += jnp.broadcast_to(part, loss_ref.shape)

    t = x3.shape[0]
    return pl.pallas_call(
        body, name="loss_head", grid=(t // ROW_BLOCK,),
        in_specs=[_row_spec(), _vec_spec(), _row_spec()], out_specs=[_row_spec(), _vec_spec(), _vec_spec(LANE)],
        out_shape=[jax.ShapeDtypeStruct((t, D_MODEL), F32), jax.ShapeDtypeStruct((1, D_MODEL), F32),
                   jax.ShapeDtypeStruct((1, LANE), F32)],
        compiler_params=_cparams(("arbitrary",)),
    )(x3, g, target)


def _ple_bwd(dx3, gpre, pp):
    def body(dx_ref, gp_ref, pp_ref, dgp_ref, dpp_ref):
        dx, sg = dx_ref[...], _sigmoid(gp_ref[...])
        dpp_ref[...] = (dx * sg).astype(BF16)
        dgp_ref[...] = (dx * pp_ref[...] * sg * (1.0 - sg)).astype(BF16)

    t = dx3.shape[0]
    return pl.pallas_call(
        body, name="ple_bwd", grid=(t // ROW_BLOCK,), in_specs=[_row_spec()] * 3, out_specs=[_row_spec()] * 2,
        out_shape=[jax.ShapeDtypeStruct((t, D_MODEL), BF16)] * 2, compiler_params=_cparams(("parallel",)),
    )(dx3, gpre, pp)


CONV_COLS = 256


def _shift_down(x, s):
    if s == 0:
        return x
    return jnp.where(_iota2(x.shape, 0) >= s, pltpu.roll(x, s, 0), 0.0)


def _shift_up(x, s):
    if s == 0:
        return x
    rows = x.shape[0]
    return jnp.where(_iota2(x.shape, 0) < rows - s, pltpu.roll(x, rows - s, 0), 0.0)


def _conv_taps(xf):
    return [_shift_down(xf, DN_CONV - 1 - j) for j in range(DN_CONV)]


def _conv_pre(taps, w):
    return sum(tap * w[j:j + 1, :] for j, tap in enumerate(taps))


def _conv_fwd(zb, conv_w, bsz, seq):
    def body(x_ref, w_ref, y_ref):
        y_ref[...] = _silu(_conv_pre(_conv_taps(x_ref[...]), w_ref[...]))

    nblk = DN_QKV // CONV_COLS
    return pl.pallas_call(
        body, name="conv_fwd", grid=(bsz, nblk),
        in_specs=[pl.BlockSpec((seq, CONV_COLS), lambda b, j: (b, ZB_DQKV // CONV_COLS + j)),
                  pl.BlockSpec((DN_CONV, CONV_COLS), lambda b, j: (0, j))],
        out_specs=pl.BlockSpec((seq, CONV_COLS), lambda b, j: (b, j)),
        out_shape=jax.ShapeDtypeStruct((bsz * seq, DN_QKV), F32),
        compiler_params=_cparams(("parallel", "parallel")),
    )(zb, conv_w)


def _conv_bwd(zb, conv_w, dact, bsz, seq):
    def body(x_ref, w_ref, dy_ref, dx_ref, dw_ref):
        taps, w = _conv_taps(x_ref[...]), w_ref[...]
        c = _conv_pre(taps, w)
        sg = _sigmoid(c)
        dc = dy_ref[...].astype(F32) * sg * (1.0 + c * (1.0 - sg))
        dx = sum(_shift_up(dc, DN_CONV - 1 - j) * w[j:j + 1, :] for j in range(DN_CONV))
        dx_ref[...] = dx.astype(BF16)
        dw = jnp.concatenate([jnp.sum(dc * tap, axis=0, keepdims=True) for tap in taps], axis=0)

        @pl.when(pl.program_id(1) == 0)
        def _():
            dw_ref[...] = jnp.zeros_like(dw_ref)

        dw_ref[...] += dw

    nblk = DN_QKV // CONV_COLS
    return pl.pallas_call(
        body, name="conv_bwd", grid=(nblk, bsz),
        in_specs=[pl.BlockSpec((seq, CONV_COLS), lambda j, b: (b, ZB_DQKV // CONV_COLS + j)),
                  pl.BlockSpec((DN_CONV, CONV_COLS), lambda j, b: (0, j)),
                  pl.BlockSpec((seq, CONV_COLS), lambda j, b: (b, j))],
        out_specs=[pl.BlockSpec((seq, CONV_COLS), lambda j, b: (b, j)),
                   pl.BlockSpec((DN_CONV, CONV_COLS), lambda j, b: (0, j))],
        out_shape=[jax.ShapeDtypeStruct((bsz * seq, DN_QKV), BF16), jax.ShapeDtypeStruct((DN_CONV, DN_QKV), F32)],
        compiler_params=_cparams(("parallel", "arbitrary")),
    )(zb, conv_w, dact)


MESH_IDS = pl.DeviceIdType.MESH
ANY_SPEC = pl.BlockSpec(memory_space=pl.ANY)


COMM_SCRATCH = (pltpu.SemaphoreType.DMA((7,)), pltpu.SemaphoreType.DMA((7,)), pltpu.SemaphoreType.DMA)


def _gather_phases(x_ref, out_ref, send_sems, recv_sems, local_sem):
    mx, my, mc = lax.axis_index("x"), lax.axis_index("y"), lax.axis_index("c")
    me, sibling = (mx, my, mc), (mx, my, 1 - mc)
    x_nbr, y_nbr, diag = (1 - mx, my), (mx, 1 - my), (1 - mx, 1 - my)
    north = mc == 1
    relayed = (jnp.where(north, mx, 1 - mx), jnp.where(north, 1 - my, my))
    relay_to = (jnp.where(north, 1 - mx, mx), jnp.where(north, my, 1 - my))

    def slot(px, py, pc):
        return out_ref.at[4 * px + 2 * py + pc]

    def copy(k, block, to, src=None):
        return pltpu.make_async_remote_copy(
            src_ref=slot(*block) if src is None else src, dst_ref=slot(*block),
            send_sem=send_sems.at[k], recv_sem=recv_sems.at[k], device_id=to, device_id_type=MESH_IDS)

    def mine():
        return pltpu.make_async_copy(x_ref, slot(*me), local_sem)

    def own_sends():
        return [copy(0, me, sibling, src=x_ref), copy(1, me, (*x_nbr, mc), src=x_ref),
                copy(2, me, (*y_nbr, mc), src=x_ref)]

    def later_sends():
        return [copy(3, (*relayed, mc), (*relay_to, mc)), copy(4, (*x_nbr, mc), sibling),
                copy(5, (*y_nbr, mc), sibling), copy(6, (*diag, mc), sibling)]

    def start():
        mine().start()
        for cp in own_sends():
            cp.start()

    def forward():
        copy(1, (*x_nbr, mc), me).wait_recv()
        copy(2, (*y_nbr, mc), me).wait_recv()
        for cp in later_sends()[:3]:
            cp.start()

    def forward_late():
        copy(3, (*diag, mc), me).wait_recv()
        later_sends()[3].start()

    def finish():
        copy(0, sibling, me).wait_recv()
        for k, chip in ((4, x_nbr), (5, y_nbr), (6, diag)):
            copy(k, (*chip, 1 - mc), me).wait_recv()
        for cp in own_sends() + later_sends():
            cp.wait_send()
        mine().wait()

    return start, forward, forward_late, finish


def _scatter_phases(x_ref, out_ref, send_sems, recv_sems, local_sem, among_chips=False):
    mx, my, mc = lax.axis_index("x"), lax.axis_index("y"), lax.axis_index("c")
    n_peers = 4 if among_chips else N_DEV
    me = 2 * mx + my if among_chips else 4 * mx + 2 * my + mc

    def peer(k):
        if among_chips:
            return (mx ^ ((k >> 1) & 1), my ^ (k & 1), mc)
        return (mx ^ ((k >> 2) & 1), my ^ ((k >> 1) & 1), mc ^ (k & 1))

    def slot_of(k):
        px, py, pc = peer(k)
        return 2 * px + py if among_chips else 4 * px + 2 * py + pc

    def copy(k, src_slot, dst_slot):
        return pltpu.make_async_remote_copy(
            src_ref=x_ref.at[src_slot], dst_ref=out_ref.at[dst_slot],
            send_sem=send_sems.at[k - 1], recv_sem=recv_sems.at[k - 1],
            device_id=peer(k), device_id_type=MESH_IDS)

    def sends():
        return [copy(k, slot_of(k), me) for k in range(1, n_peers)]

    def mine():
        return pltpu.make_async_copy(x_ref.at[me], out_ref.at[me], local_sem)

    def start():
        mine().start()
        for cp in sends():
            cp.start()

    def finish():
        for k in range(1, n_peers):
            copy(k, me, slot_of(k)).wait_recv()
        for cp in sends():
            cp.wait_send()
        mine().wait()

    return start, _nothing, _nothing, finish


def _pair_phases(x_ref, out_ref, send_sems, recv_sems, local_sem):
    mx, my, mc = lax.axis_index("x"), lax.axis_index("y"), lax.axis_index("c")

    def copy(side):
        return pltpu.make_async_remote_copy(
            src_ref=x_ref.at[:, side], dst_ref=out_ref, send_sem=send_sems.at[0], recv_sem=recv_sems.at[0],
            device_id=(mx, my, 1 - mc), device_id_type=MESH_IDS)

    def start():
        copy(1 - mc).start()

    def finish():
        copy(mc).wait_recv()
        copy(1 - mc).wait_send()

    return start, _nothing, _nothing, finish


def _nothing():
    pass


class _Rider:
    def __init__(self, phases, x, out_shape):
        self.phases, self.x, self.out_shape = phases, x, out_shape


def _gather_rider(x):
    return _Rider(_gather_phases, x, jax.ShapeDtypeStruct((N_DEV,) + x.shape, x.dtype))


def _scatter_rider(x):
    return _Rider(_scatter_phases, x, jax.ShapeDtypeStruct(x.shape, x.dtype))


def _chip_scatter_rider(x):
    return _Rider(functools.partial(_scatter_phases, among_chips=True), x, jax.ShapeDtypeStruct(x.shape, x.dtype))


def _pair_rider(x):
    return _Rider(_pair_phases, x, jax.ShapeDtypeStruct((x.shape[0],) + x.shape[2:], x.dtype))


def _exchange(rider, name):
    def body(x_ref, out_ref, send_sems, recv_sems, local_sem):
        for phase in rider.phases(x_ref, out_ref, send_sems, recv_sems, local_sem):
            phase()

    return pl.pallas_call(body, name=name, out_shape=rider.out_shape, in_specs=[ANY_SPEC], out_specs=ANY_SPEC,
                          scratch_shapes=list(COMM_SCRATCH))(rider.x)


def _all_gather(x, name):
    return _exchange(_gather_rider(x), name)


def _hosted_call(body, rider, *, name, grid, in_specs, out_specs, out_shape, scratch_shapes, args):
    if rider is None:
        return pl.pallas_call(body, name=name, grid=grid, in_specs=in_specs, out_specs=out_specs, out_shape=out_shape,
                              scratch_shapes=scratch_shapes, compiler_params=_cparams(("arbitrary",) * len(grid)))(*args)
    n_in, n_out, n_scr = len(in_specs), len(out_specs), len(scratch_shapes)
    total = math.prod(grid)

    def riding(*refs):
        host_in, x_ref = refs[:n_in], refs[n_in]
        host_out, out_ref = refs[n_in + 1:n_in + 1 + n_out], refs[n_in + 1 + n_out]
        host_scr = refs[n_in + 2 + n_out:n_in + 2 + n_out + n_scr]
        start, forward, forward_late, finish = rider.phases(x_ref, out_ref, *refs[n_in + 2 + n_out + n_scr:])
        step = 0
        for axis, size in enumerate(grid):
            step = step * size + pl.program_id(axis)
        pl.when(step == 0)(start)
        pl.when(step == total // 2)(forward)
        pl.when(step == (3 * total) // 4)(forward_late)
        body(*host_in, *host_out, *host_scr)
        pl.when(step == total - 1)(finish)

    return pl.pallas_call(
        riding, name=name, grid=grid, in_specs=list(in_specs) + [ANY_SPEC], out_specs=list(out_specs) + [ANY_SPEC],
        out_shape=list(out_shape) + [rider.out_shape], scratch_shapes=list(scratch_shapes) + list(COMM_SCRATCH),
        compiler_params=_cparams(("arbitrary",) * len(grid)))(*args, rider.x)


def _adamw_math(w, g, m, v):
    m = ADAM_B1 * m + (1.0 - ADAM_B1) * g
    v = ADAM_B2 * v + (1.0 - ADAM_B2) * jnp.square(g)
    m_hat = m / (1.0 - ADAM_B1 ** ADAM_STEP)
    v_hat = v / (1.0 - ADAM_B2 ** ADAM_STEP)
    delta = -ADAM_LR * (m_hat / (jnp.sqrt(v_hat) + ADAM_EPS) + ADAM_WD * w)
    return delta, m, v


ADAM_ROWS = 128


def _elementwise_tile(rows, cols):
    if rows % ADAM_ROWS == 0:
        return ADAM_ROWS, cols
    return rows, (2 * LANE if cols % (2 * LANE) == 0 else cols)


def _add_blocks(a, b):
    g, rows, cols = a.shape
    tr, tc = _elementwise_tile(rows, cols)

    def body(a_ref, b_ref, o_ref):
        o_ref[...] = (a_ref[...].astype(F32) + b_ref[...].astype(F32)).astype(o_ref.dtype)

    blk = pl.BlockSpec((None, tr, tc), lambda k, i, j: (k, i, j))
    return pl.pallas_call(body, name="add_pair_blocks", grid=(g, rows // tr, cols // tc), in_specs=[blk, blk],
                          out_specs=blk, out_shape=jax.ShapeDtypeStruct(a.shape, a.dtype),
                          compiler_params=_cparams(("parallel", "parallel", "parallel")))(a, b)


def _adamw_reduce(w, m, v, parts, row0, name):
    rows, cols = w.shape
    n_parts = parts.shape[0]
    tr, tc = _elementwise_tile(rows, cols)
    r0 = row0 // tr

    def body(w_ref, m_ref, v_ref, *refs):
        part_refs, (g_ref, d_ref, nm_ref, nv_ref) = refs[:n_parts], refs[n_parts:]
        g = part_refs[0][...].astype(F32)
        for r in part_refs[1:]:
            g = g + r[...].astype(F32)
        delta, nm, nv = _adamw_math(w_ref[...], g, m_ref[...], v_ref[...])
        g_ref[...] = g
        d_ref[...] = delta
        nm_ref[...] = nm
        nv_ref[...] = nv

    blk = pl.BlockSpec((tr, tc), lambda i, j: (i, j))
    part_specs = [pl.BlockSpec((None, tr, tc), functools.partial(lambda i, j, k: (k, r0 + i, j), k=k))
                  for k in range(n_parts)]
    return pl.pallas_call(
        body, name=name, grid=(rows // tr, cols // tc), in_specs=[blk] * 3 + part_specs, out_specs=[blk] * 4,
        out_shape=[jax.ShapeDtypeStruct(w.shape, F32)] * 4, compiler_params=_cparams(("parallel", "parallel")),
    )(w, m, v, *([parts] * n_parts))


def _small_reduce(gathered):
    r = gathered.shape[1]

    def body(g_ref, o_ref):
        g = g_ref[0]
        for k in range(1, N_DEV):
            g = g + g_ref[k]
        o_ref[...] = g

    return pl.pallas_call(body, name="small_grad_reduce", out_shape=jax.ShapeDtypeStruct((r, LANE), F32))(gathered)


def _adamw_small(w, m, v, g):
    def body(w_ref, m_ref, v_ref, g_ref, d_ref, nm_ref, nv_ref):
        d_ref[...], nm_ref[...], nv_ref[...] = _adamw_math(w_ref[...], g_ref[...], m_ref[...], v_ref[...])

    return pl.pallas_call(body, name="adamw_small", out_shape=[jax.ShapeDtypeStruct(w.shape, F32)] * 3)(w, m, v, g)


def _pack_rows(arrays):
    rows = [jnp.pad(a.reshape(-1), (0, -a.size % LANE)).reshape(-1, LANE) for a in arrays]
    out = jnp.concatenate(rows, axis=0)
    return jnp.pad(out, ((0, -out.shape[0] % 8), (0, 0)))


def _unpack_rows(packed, shapes):
    out, r = [], 0
    for shp in shapes:
        size = math.prod(shp)
        nrows = -(-size // LANE)
        out.append(packed[r:r + nrows].reshape(-1)[:size].reshape(shp))
        r += nrows
    return out


def _add_residual(acc, res):
    return (res + acc,)


def _by_cols(g):
    return g.reshape(g.shape[0], N_DEV, -1).transpose(1, 0, 2)


def _from_cols(blocks):
    return blocks.transpose(1, 0, 2).reshape(blocks.shape[1], -1)


ROWS_OUT = D_MODEL // N_DEV

W_IN_SEGMENTS = ((0, WI_LR, "big", 0), (WI_LR, WI_DQKV, "gates", ZS_LR), (WI_DQKV, WI_DA, "big", ZB_DQKV),
                 (WI_DA, WI_DB, "gates", ZS_DA), (WI_DB, WI_GA, "gates", ZS_DB), (WI_GA, D_IN, "big", ZB_GA))


def _w_in_gate_rows(wt):
    parts = []
    for lo, hi, which, _ in W_IN_SEGMENTS:
        if which == "gates":
            parts += [wt[lo:hi], jnp.zeros((LANE - (hi - lo), wt.shape[1]), wt.dtype)]
    return jnp.concatenate(parts, axis=0)


def _fill_gate_rows(dw_in_t, dws_t):
    for lo, hi, which, first in W_IN_SEGMENTS:
        if which == "gates":
            dw_in_t = lax.dynamic_update_slice(dw_in_t, dws_t[first:first + hi - lo], (lo, 0))
    return dw_in_t


def _local_step(x, p, target, w, up_shard, rows_shard):
    bsz, seq, _ = x.shape
    t, nc = bsz * seq, seq // CHUNK
    x0, p2, tgt = x.reshape(t, D_MODEL), p.reshape(t, PLE_DIM), target.reshape(t, D_MODEL)

    h = _rms_fwd(x0, w["g_mix"], "rms_mix")
    zb, row_blocks = _mm(h, w["w_in_t"], tb=True, b_is_w_in_t=True, name="in_proj", rider=_gather_rider(rows_shard))
    zs, small = _mm(h, w["ws_t"], tb=True, name="in_proj_gates", rider=_gather_rider(w["small_shard"]))
    n_w2, n_cv = GLA_LOWRANK * GLA_QK // N_DEV // LANE, DN_CONV * DN_QKV // N_DEV // LANE
    n_pp = PLE_DIM * D_MODEL // N_DEV // LANE
    w = dict(w,
             w2p=jnp.pad(_from_cols(small[:, :n_w2].reshape(N_DEV, GLA_LOWRANK, -1)), ((0, LANE - GLA_LOWRANK), (0, 0))),
             conv=_from_cols(small[:, n_w2:n_w2 + n_cv].reshape(N_DEV, DN_CONV, -1)),
             w_pp=_from_cols(small[:, n_w2 + n_cv:n_w2 + n_cv + n_pp].reshape(N_DEV, PLE_DIM, -1)).astype(BF16))
    act = _conv_fwd(zb, w["conv"], bsz, seq)
    mix_gla, gla_states = _gla_fwd(zb, zs, w["w2p"], w["gla_b"], w["gla_norm"], bsz, nc)
    mixed, dn_states, dn_t_invs, up_blocks = _dn_fwd(act, zb, zs, w["alog_b"], w["dtb_b"], w["dn_norm"], mix_gla,
                                                     bsz, nc, rider=_gather_rider(up_shard))
    w_up = _from_cols(up_blocks)
    w_out = row_blocks[:, :ROWS_OUT].reshape(D_MODEL, D_MODEL)
    w_pg = row_blocks[:, ROWS_OUT:2 * ROWS_OUT].reshape(D_MODEL, D_MODEL)
    w_down = row_blocks[:, 2 * ROWS_OUT:].reshape(D_FF, D_MODEL)
    x1 = _mm(mixed, w_out, epilogue=_add_residual, extras=(x0,), name="out_proj")
    h2 = _rms_fwd(x1, w["g_mlp"], "rms_mlp")
    u, a = _mm(h2, w_up, out_dtypes=(BF16, BF16), name="mlp_up",
               epilogue=lambda acc: (acc, jnp.square(jnp.maximum(acc, 0.0))))
    x2 = _mm(a, w_down, epilogue=_add_residual, extras=(x1,), name="mlp_down")
    h3 = _rms_fwd(x2, w["g_ple"], "rms_ple")
    pp = _mm(p2, w["w_pp"], name="ple_proj")
    gpre, x3 = _mm(h3, w_pg, out_dtypes=(F32, F32), extras=(x2, pp), name="ple_gate",
                   epilogue=lambda acc, res, proj: (acc, res + _sigmoid(acc) * proj))
    dx3, dg_final, loss = _loss_head(x3, w["g_final"], tgt)

    dgpre, dpp = _ple_bwd(dx3, gpre, pp)
    dw_pp = _mm(p2, dpp, ta=True, out_dtypes=(BF16,), name="d_w_ple_proj")
    dw_pg, recv_pp = _mm(h3, dgpre, ta=True, out_dtypes=(BF16,), name="d_w_ple_gate",
                         rider=_scatter_rider(_by_cols(dw_pp)))
    dh3 = _mm(dgpre, w_pg, tb=True, name="d_h_ple")
    dx2, dx2_16, dg_ple = _rms_bwd(x2, w["g_ple"], dh3, dx3, "rms_ple_bwd")
    du = _mm(dx2_16, w_down, tb=True, out_dtypes=(BF16,), extras=(u,), name="d_mlp_hidden",
             epilogue=lambda acc, uu: (acc * (2.0 * jnp.maximum(uu.astype(F32), 0.0)),))
    dw_down = _mm(a, dx2_16, ta=True, out_dtypes=(BF16,), name="d_w_down")
    dw_up = _mm(h2, du, ta=True, out_dtypes=(BF16,), name="d_w_up")
    dh2 = _mm(du, w_up, tb=True, name="d_h_mlp")
    dx1, dx1_16, dg_mlp = _rms_bwd(x1, w["g_mlp"], dh2, dx2, "rms_mlp_bwd")
    dmixed = _mm(dx1_16, w_out, tb=True, out_dtypes=(BF16,), name="d_mixed")
    dw_out = _mm(mixed, dx1_16, ta=True, out_dtypes=(BF16,), name="d_w_out")

    d_rows = jnp.concatenate([dw_out.reshape(N_DEV, ROWS_OUT, D_MODEL), dw_pg.reshape(N_DEV, ROWS_OUT, D_MODEL),
                              dw_down.reshape(N_DEV, D_FF // N_DEV, D_MODEL)], axis=1)
    (dact, dza, dzb_, dgz, dgb, dal, ddt, ddn, recv_rows) = _dn_bwd(
        act, zb, zs, w["alog_b"], w["dtb_b"], w["dn_norm"], dn_states, dn_t_invs, dmixed, bsz, nc,
        rider=_scatter_rider(d_rows))
    (gdq, gdk, gdv, dgg, dga, dlr, dw2, dgla_b, dgla_norm, recv_up) = _gla_bwd(
        zb, zs, w["w2p"], w["gla_b"], w["gla_norm"], gla_states, dmixed, bsz, nc,
        rider=_scatter_rider(_by_cols(dw_up)))
    dqkv, dconv = _conv_bwd(zb, w["conv"], dact, bsz, seq)
    dzb = jnp.concatenate([gdq, gdk, gdv, dgg, dqkv, dgz, dga, dgb], axis=1)
    dzs = jnp.concatenate([dlr, dza, dzb_], axis=1)
    dw_in_t = _mm(dzb, h, ta=True, out_dtypes=(BF16,), out_is_w_in_t=True, name="d_w_in")
    dws_t = _mm(dzs, h, ta=True, out_dtypes=(BF16,), name="d_w_in_gates")
    by_chip = _fill_gate_rows(dw_in_t, dws_t).reshape(N_DEV // 2, 2, D_IN_SHARD, D_MODEL)
    dh_gates, from_sibling = _mm(dzs, w["ws_t"], name="d_h_mix_gates", rider=_pair_rider(by_chip))
    mine = lax.dynamic_index_in_dim(by_chip, lax.axis_index("c"), axis=1, keepdims=False)
    chip_sums = _add_blocks(mine, from_sibling)
    dh, recv_in = _mm(dzb, w["w_in_t"], b_is_w_in_t=True, epilogue=_add_residual, extras=(dh_gates,), name="d_h_mix",
                      rider=_chip_scatter_rider(chip_sums))
    gx, _, dg_mix = _rms_bwd(x0, w["g_mix"], dh, dx1, "rms_mix_bwd")

    dgla_w2 = dw2[:, :GLA_LOWRANK, :].transpose(1, 0, 2).reshape(GLA_LOWRANK, GLA_QK)
    return dict(
        loss=loss[0, 0], grad_x=gx.reshape(x.shape), recv_in=recv_in, recv_up=recv_up, recv_rows=recv_rows,
        recv_pp=recv_pp,
        g_mix=dg_mix, gla_b=dgla_b.reshape(1, GLA_QK), gla_norm=dgla_norm, dn_norm=ddn, g_mlp=dg_mlp, g_ple=dg_ple,
        g_final=dg_final, gla_w2=dgla_w2, dn_conv=dconv,
        dn_a_log=dal[:, :DN_HEADS], dn_dt_bias=ddt[:, :DN_HEADS])


def _first_weights(g_mix, w_in, gla_w2, gla_b, gla_norm, dn_conv, dn_a_log, dn_dt_bias, dn_norm, g_mlp, g_ple,
                   w_ple_proj, g_final):
    w_in_t = _all_gather(jnp.swapaxes(w_in[0], 0, 1).astype(BF16), "gather_w_in").reshape(D_IN, D_MODEL)
    return dict(
        w_in_t=w_in_t, ws_t=_w_in_gate_rows(w_in_t),
        small_shard=_pack_rows([gla_w2[0], dn_conv[0], w_ple_proj[0]]),
        alog_b=jnp.pad(dn_a_log, ((0, 0), (0, LANE - DN_HEADS))),
        dtb_b=jnp.pad(dn_dt_bias, ((0, 0), (0, LANE - DN_HEADS))),
        g_mix=g_mix, gla_b=gla_b, gla_norm=gla_norm, dn_norm=dn_norm, g_mlp=g_mlp, g_ple=g_ple,
        g_final=g_final.reshape(1, D_MODEL))


def kernel(x, p, g_mix, w_in, gla_w2, gla_b, gla_norm, dn_conv, dn_a_log, dn_dt_bias, dn_norm, w_out, g_mlp, w_up, w_down, g_ple, w_ple_gate, w_ple_proj, g_final, loss_target, m_g_mix, m_w_in, m_gla_w2, m_gla_b, m_gla_norm, m_dn_conv, m_dn_a_log, m_dn_dt_bias, m_dn_norm, m_w_out, m_g_mlp, m_w_up, m_w_down, m_g_ple, m_w_ple_gate, m_w_ple_proj, m_g_final, v_g_mix, v_w_in, v_gla_w2, v_gla_b, v_gla_norm, v_dn_conv, v_dn_a_log, v_dn_dt_bias, v_dn_norm, v_w_out, v_g_mlp, v_w_up, v_w_down, v_g_ple, v_w_ple_gate, v_w_ple_proj, v_g_final):
    names = ["g_mix", "w_in", "gla_w2", "gla_b", "gla_norm", "dn_conv", "dn_a_log", "dn_dt_bias", "dn_norm", "w_out",
             "g_mlp", "w_up", "w_down", "g_ple", "w_ple_gate", "w_ple_proj", "g_final"]
    ws = dict(zip(names, (g_mix, w_in, gla_w2, gla_b, gla_norm, dn_conv, dn_a_log, dn_dt_bias, dn_norm, w_out, g_mlp,
                          w_up, w_down, g_ple, w_ple_gate, w_ple_proj, g_final)))
    ms = dict(zip(names, (m_g_mix, m_w_in, m_gla_w2, m_gla_b, m_gla_norm, m_dn_conv, m_dn_a_log, m_dn_dt_bias,
                          m_dn_norm, m_w_out, m_g_mlp, m_w_up, m_w_down, m_g_ple, m_w_ple_gate, m_w_ple_proj,
                          m_g_final)))
    vs = dict(zip(names, (v_g_mix, v_w_in, v_gla_w2, v_gla_b, v_gla_norm, v_dn_conv, v_dn_a_log, v_dn_dt_bias,
                          v_dn_norm, v_w_out, v_g_mlp, v_w_up, v_w_down, v_g_ple, v_w_ple_gate, v_w_ple_proj,
                          v_g_final)))
    me = 4 * lax.axis_index("x") + 2 * lax.axis_index("y") + lax.axis_index("c")

    first = _first_weights(g_mix, w_in, gla_w2, gla_b, gla_norm, dn_conv, dn_a_log, dn_dt_bias, dn_norm, g_mlp,
                           g_ple, w_ple_proj, g_final)
    rows_shard = jnp.concatenate([w_out[0], w_ple_gate[0], w_down[0]], axis=0).astype(BF16)
    r = _local_step(x, p[0], loss_target, first, w_up[0].astype(BF16), rows_shard)
    loss = lax.psum(r["loss"], ("x", "y", "c"))

    grads, deltas, new_m, new_v = {}, {}, {}, {}

    def big(name, parts, row0=0):
        g, d, nm, nv = _adamw_reduce(ws[name][0], ms[name][0], vs[name][0], parts, row0, "adamw_" + name)
        grads[name], deltas[name], new_m[name], new_v[name] = g[None], d[None], nm[None], nv[None]

    t_outs = _adamw_reduce(*[jnp.swapaxes(d["w_in"][0], 0, 1) for d in (ws, ms, vs)], r["recv_in"], 0, "adamw_w_in")
    grads["w_in"], deltas["w_in"], new_m["w_in"], new_v["w_in"] = [jnp.swapaxes(o, 0, 1)[None] for o in t_outs]
    big("w_up", r["recv_up"])
    big("w_ple_proj", r["recv_pp"])
    big("w_out", r["recv_rows"], 0)
    big("w_ple_gate", r["recv_rows"], ROWS_OUT)
    big("w_down", r["recv_rows"], 2 * ROWS_OUT)

    vec_names = ["g_mix", "gla_b", "gla_norm", "dn_norm", "g_mlp", "g_ple", "g_final", "dn_a_log", "dn_dt_bias"]
    gathered_names = vec_names + ["gla_w2", "dn_conv"]
    total = _small_reduce(_all_gather(_pack_rows([r[n] for n in gathered_names]), "gather_small_grads"))
    sg = dict(zip(gathered_names, _unpack_rows(total, [r[n].shape for n in gathered_names])))
    sg["g_final"] = sg["g_final"].reshape(D_MODEL)
    sg["gla_w2"] = lax.dynamic_slice_in_dim(sg["gla_w2"], me * (GLA_QK // N_DEV), GLA_QK // N_DEV, axis=1)
    sg["dn_conv"] = lax.dynamic_slice_in_dim(sg["dn_conv"], me * (DN_QKV // N_DEV), DN_QKV // N_DEV, axis=1)
    small_names = gathered_names
    shapes = [ws[n].shape for n in small_names]
    d_s, m_s, v_s = _adamw_small(_pack_rows([ws[n] for n in small_names]), _pack_rows([ms[n] for n in small_names]),
                                 _pack_rows([vs[n] for n in small_names]), _pack_rows([sg[n] for n in small_names]))
    for n, d, nm, nv in zip(small_names, _unpack_rows(d_s, shapes), _unpack_rows(m_s, shapes), _unpack_rows(v_s, shapes)):
        grads[n], deltas[n], new_m[n], new_v[n] = sg[n].reshape(ws[n].shape), d, nm, nv

    return (loss, r["grad_x"], *[grads[n] for n in names], *[deltas[n] for n in names],
            *[new_m[n] for n in names], *[new_v[n] for n in names])
```

```python
import functools
import math

import jax
import jax.numpy as jnp
from jax import lax
from jax.experimental import pallas as pl
from jax.experimental.pallas import tpu as pltpu

F32 = jnp.float32
BF16 = jnp.bfloat16

N_DEV = 8
D_MODEL = 2048
CHUNK = 64
PLE_DIM = 256
EPS = 1e-6
GLA_HEADS = 4
GLA_DK = 256
GLA_DV = 512
GLA_LOWRANK = 16
GLA_TAU = 16.0
DN_HEADS = 16
DN_D = 128
DN_CONV = 4
D_FF = 4 * D_MODEL
GLA_QK = GLA_HEADS * GLA_DK
GLA_V = GLA_HEADS * GLA_DV
DN_QKV = 3 * DN_HEADS * DN_D
D_IN = 2 * GLA_QK + 2 * GLA_V + GLA_LOWRANK + DN_QKV + D_MODEL + 2 * DN_HEADS + 2 * D_MODEL
D_IN_SHARD = D_IN // N_DEV

ADAM_LR = 0.001
ADAM_B1 = 0.9
ADAM_B2 = 0.999
ADAM_EPS = 1e-08
ADAM_WD = 0.01
ADAM_STEP = 10

LANE = 128
ZB_GQ, ZB_GK, ZB_GV, ZB_GG = 0, 1024, 2048, 4096
ZB_DQKV, ZB_DZ, ZB_GA, ZB_GB = 6144, 12288, 14336, 16384
ZB_W = 18432
ZS_LR, ZS_DA, ZS_DB = 0, 128, 256
ZS_W = 384
WI_LR = 2 * GLA_QK + 2 * GLA_V
WI_DQKV = WI_LR + GLA_LOWRANK
WI_DA = WI_DQKV + DN_QKV + D_MODEL
WI_DB = WI_DA + DN_HEADS
WI_GA = WI_DB + DN_HEADS

VMEM_LIMIT = 56 * 1024 * 1024

def _bdot(a, b, dims):
    return lax.dot_general(a.astype(BF16), b.astype(BF16), dims, preferred_element_type=F32)


def _split3(x):
    hi = x.astype(BF16)
    rest = x - hi.astype(F32)
    mid = rest.astype(BF16)
    return hi, mid, (rest - mid.astype(F32)).astype(BF16)


def _dot01(m, x, dims):
    m = m.astype(BF16)
    out = None
    for piece in _split3(x):
        d = lax.dot_general(m, piece, dims, preferred_element_type=F32)
        out = d if out is None else out + d
    return out


@functools.partial(jax.custom_vjp, nondiff_argnums=(2, 3))
def _left_dot(m, x, dims, dims_t):
    return _dot01(m, x, dims)


def _left_dot_fwd(m, x, dims, dims_t):
    return _dot01(m, x, dims), m


def _left_dot_bwd(dims, dims_t, m, ct):
    return jnp.zeros_like(m), _dot01(m, ct, dims_t)


_left_dot.defvjp(_left_dot_fwd, _left_dot_bwd)


def _dot3(a, b, dims):
    ah, bh = a.astype(BF16), b.astype(BF16)
    al, bl = (a - ah.astype(F32)).astype(BF16), (b - bh.astype(F32)).astype(BF16)
    dot = functools.partial(lax.dot_general, dimension_numbers=dims, preferred_element_type=F32)
    return dot(ah, bh) + (dot(ah, bl) + dot(al, bh))


def _sigmoid(x):
    return 1.0 / (1.0 + jnp.exp(-x))


def _silu(x):
    return x * _sigmoid(x)


def _softplus(x):
    return jnp.maximum(x, 0.0) + jnp.log(1.0 + jnp.exp(-jnp.abs(x)))


def _iota2(shape, dim):
    return lax.broadcasted_iota(jnp.int32, shape, dim)


def _cparams(sem=None):
    return pltpu.CompilerParams(dimension_semantics=sem, vmem_limit_bytes=VMEM_LIMIT)


BNN = (((2,), (1,)), ((0,), (0,)))
BNT = (((2,), (2,)), ((0,), (0,)))
BTN = (((1,), (1,)), ((0,), (0,)))


def _gla_chunk(st, q, k, v, lr, w2, b, gg, ga, gn):
    hb, c, _ = q.shape
    incl = (_iota2((c, c), 0) >= _iota2((c, c), 1))[None]
    tri = jnp.broadcast_to(incl.astype(F32), (hb, c, c))
    q = q.astype(F32) * (GLA_DK ** -0.5)
    k = k.astype(F32)
    v = v.astype(F32)
    lr_b = jnp.broadcast_to(lr[None], (hb,) + lr.shape)
    lf = -_softplus(-(_bdot(lr_b, w2, BNN) + b)) / GLA_TAU
    bcum = _left_dot(tri, lf, BNN, BTN)
    blast = jnp.sum(lf, axis=1, keepdims=True)
    q_in = q * jnp.exp(bcum)
    k_in = k * jnp.exp(-bcum)
    a = jnp.where(incl, _bdot(q_in, k_in, BNT), 0.0)
    o = _bdot(a, v, BNN) + _bdot(q_in, st, BNT)
    k_dec = k * jnp.exp(blast - bcum)
    st_new = st * jnp.exp(blast) + _bdot(v, k_dec, BTN)
    on = o * lax.rsqrt(jnp.mean(o * o, axis=-1, keepdims=True) + EPS) * gn
    res = _sigmoid(ga.astype(F32)) * on * _silu(gg.astype(F32))
    return res, st_new


def _tri_inv_raw(a):
    _, c, _ = a.shape
    eye = (_iota2((c, c), 0) == _iota2((c, c), 1)).astype(F32)[None]
    x = a
    p = eye - a
    for _ in range(5):
        x = _dot3(x, x, BNN)
        p = _dot3(p, eye + x, BNN)
    return p


def _tri_inv_bwd(t, dt):
    return (-_dot3(_dot3(t, dt, BTN), t, BNT),)


@jax.custom_vjp
def _tri_inv_given(a, t):
    return t


def _tri_inv_given_fwd(a, t):
    return t, t


def _tri_inv_given_bwd(t, dt):
    return _tri_inv_bwd(t, dt) + (jnp.zeros_like(t),)


_tri_inv_given.defvjp(_tri_inv_given_fwd, _tri_inv_given_bwd)


@functools.partial(jax.custom_vjp, nondiff_argnums=(1, 2))
def _column_on_lanes(z, j, width):
    picked = jnp.where(_iota2(z.shape, 1) == j, z, 0.0)
    return jnp.broadcast_to(jnp.sum(picked, axis=1, keepdims=True), (z.shape[0], width))


def _column_on_lanes_fwd(z, j, width):
    return _column_on_lanes(z, j, width), None


def _column_on_lanes_bwd(j, width, _, ct):
    shape = (ct.shape[0], LANE)
    total = jnp.broadcast_to(jnp.sum(ct, axis=1, keepdims=True), shape)
    return (jnp.where(_iota2(shape, 1) == j, total, 0.0),)


_column_on_lanes.defvjp(_column_on_lanes_fwd, _column_on_lanes_bwd)


def _dn_chunk(s, qr, kr, vr, za, zb, alog, dtb, gz, gb, dn, t_saved=None):
    hb, c, _ = qr.shape
    row, col = _iota2((c, c), 0), _iota2((c, c), 1)
    incl = (row >= col)[None]
    strict = (row > col)[None]
    tri = jnp.broadcast_to(incl.astype(F32), (hb, c, c))

    def l2n(t):
        return t * lax.rsqrt(jnp.sum(t * t, axis=-1, keepdims=True) + EPS)

    q = l2n(qr.astype(F32)) * (DN_D ** -0.5)
    k = l2n(kr.astype(F32))
    v = vr.astype(F32)
    g_heads = -jnp.exp(alog) * _softplus(za + dtb)
    beta_heads = _sigmoid(zb)
    def per_head(z, width):
        return jnp.concatenate([_column_on_lanes(z, j, width)[None] for j in range(hb)], axis=0)

    g = per_head(g_heads, LANE)
    beta = per_head(beta_heads, LANE)
    gcum = _left_dot(tri, g, BNN, BTN)
    glast = jnp.sum(g, axis=1, keepdims=True)
    diff = _left_dot(tri, per_head(g_heads, c) * strict.astype(F32), BNN, BTN)
    dec = jnp.exp(jnp.where(incl, diff, -1e30))
    kb = k * beta
    a = jnp.where(strict, _bdot(kb, k, BNT) * dec, 0.0)
    t = _tri_inv_raw(a) if t_saved is None else _tri_inv_given(a, t_saved)
    egc = jnp.exp(gcum)
    u = _bdot(t, v * beta, BNN)
    w = _bdot(t, kb * egc, BNN)
    attn = jnp.where(incl, _bdot(q, k, BNT) * dec, 0.0)
    q_dec = q * egc
    k_dec = k * jnp.exp(glast - gcum)
    v_new = u - _bdot(w, s, BNN)
    o = _bdot(q_dec, s, BNN) + _bdot(attn, v_new, BNN)
    s_new = s * jnp.exp(glast) + _bdot(k_dec, v_new, BTN)
    on = o * lax.rsqrt(jnp.mean(o * o, axis=-1, keepdims=True) + EPS) * dn
    res = _sigmoid(gb.astype(F32)) * on * _silu(gz.astype(F32))
    return (res, s_new, t) if t_saved is None else (res, s_new)


def _heads(ref, n_heads, width):
    return jnp.stack([ref[:, j * width:(j + 1) * width] for j in range(n_heads)], axis=0)


def _gla_specs(nc, reverse):
    def rows(b, n):
        return b * nc + ((nc - 1 - n) if reverse else n)

    qk = lambda base: pl.BlockSpec((CHUNK, GLA_QK), lambda b, n: (rows(b, n), base // GLA_QK))
    vv = lambda base: pl.BlockSpec((CHUNK, GLA_V), lambda b, n: (rows(b, n), base // GLA_V))
    lr = lambda c: pl.BlockSpec((CHUNK, LANE), lambda b, n: (rows(b, n), c))
    full = lambda shape: pl.BlockSpec(shape, lambda b, n: (0,) * len(shape))
    return rows, qk, vv, lr, full


def _gla_inputs(q_ref, k_ref, v_ref, gg_ref, ga_ref, lr_ref, w2_ref, b_ref, gn_ref):
    return (_heads(q_ref, GLA_HEADS, GLA_DK), _heads(k_ref, GLA_HEADS, GLA_DK), _heads(v_ref, GLA_HEADS, GLA_DV),
            lr_ref[...], _heads(w2_ref, GLA_HEADS, GLA_DK), _heads(b_ref, GLA_HEADS, GLA_DK),
            _heads(gg_ref, GLA_HEADS, GLA_DV), _heads(ga_ref, GLA_HEADS, GLA_DV), gn_ref[...])


def _gla_fwd(zb, zs, w2p, gla_b, gla_norm, bsz, nc):
    t = zb.shape[0]
    rows, qk, vv, lr, full = _gla_specs(nc, False)

    def body(q_ref, k_ref, v_ref, gg_ref, ga_ref, lr_ref, w2_ref, b_ref, gn_ref, o_ref, st_ref, state):
        st = jnp.where(pl.program_id(1) > 0, state[...], 0.0)
        st_ref[0, 0] = st
        res, st_new = _gla_chunk(st, *_gla_inputs(q_ref, k_ref, v_ref, gg_ref, ga_ref, lr_ref, w2_ref, b_ref, gn_ref))
        for j in range(GLA_HEADS):
            o_ref[:, j * GLA_DV:(j + 1) * GLA_DV] = res[j]
        state[...] = st_new

    return pl.pallas_call(
        body, name="gla_fwd", grid=(bsz, nc),
        in_specs=[qk(ZB_GQ), qk(ZB_GK), vv(ZB_GV), vv(ZB_GG), vv(ZB_GA), lr(ZS_LR // LANE),
                  full((LANE, GLA_QK)), full((1, GLA_QK)), full((1, GLA_DV))],
        out_specs=[vv(0), pl.BlockSpec((1, 1, GLA_HEADS, GLA_DV, GLA_DK), lambda b, n: (b, n, 0, 0, 0))],
        out_shape=[jax.ShapeDtypeStruct((t, GLA_V), F32),
                   jax.ShapeDtypeStruct((bsz, nc, GLA_HEADS, GLA_DV, GLA_DK), F32)],
        scratch_shapes=[pltpu.VMEM((GLA_HEADS, GLA_DV, GLA_DK), F32)],
        compiler_params=_cparams(("arbitrary", "arbitrary")),
    )(zb, zb, zb, zb, zb, zs, w2p, gla_b, gla_norm)


def _gla_bwd(zb, zs, w2p, gla_b, gla_norm, states, dmixed, bsz, nc, rider=None):
    t = zb.shape[0]
    rows, qk, vv, lr, full = _gla_specs(nc, True)

    def body(q_ref, k_ref, v_ref, gg_ref, ga_ref, lr_ref, w2_ref, b_ref, gn_ref, st_ref, dm_ref,
             dq_ref, dk_ref, dv_ref, dgg_ref, dga_ref, dlr_ref, dw2_ref, db_ref, dgn_ref, dstate):
        b, n = pl.program_id(0), pl.program_id(1)

        @pl.when((b == 0) & (n == 0))
        def _():
            dw2_ref[...] = jnp.zeros_like(dw2_ref)
            db_ref[...] = jnp.zeros_like(db_ref)
            dgn_ref[...] = jnp.zeros_like(dgn_ref)

        _, vjp = jax.vjp(_gla_chunk, st_ref[0, 0],
                         *_gla_inputs(q_ref, k_ref, v_ref, gg_ref, ga_ref, lr_ref, w2_ref, b_ref, gn_ref))
        dst_in = jnp.where(n > 0, dstate[...], 0.0)
        dst, dq, dk, dv, dlr, dw2, db, dgg, dga, dgn = vjp((_heads(dm_ref, GLA_HEADS, GLA_DV).astype(F32), dst_in))
        dstate[...] = dst
        for j in range(GLA_HEADS):
            dq_ref[:, j * GLA_DK:(j + 1) * GLA_DK] = dq[j].astype(dq_ref.dtype)
            dk_ref[:, j * GLA_DK:(j + 1) * GLA_DK] = dk[j].astype(dk_ref.dtype)
            dv_ref[:, j * GLA_DV:(j + 1) * GLA_DV] = dv[j].astype(dv_ref.dtype)
            dgg_ref[:, j * GLA_DV:(j + 1) * GLA_DV] = dgg[j].astype(dgg_ref.dtype)
            dga_ref[:, j * GLA_DV:(j + 1) * GLA_DV] = dga[j].astype(dga_ref.dtype)
        dlr_ref[...] = dlr
        dw2_ref[...] += dw2
        db_ref[...] += db
        dgn_ref[...] += dgn

    return _hosted_call(
        body, rider, name="gla_bwd", grid=(bsz, nc),
        in_specs=[qk(ZB_GQ), qk(ZB_GK), vv(ZB_GV), vv(ZB_GG), vv(ZB_GA), lr(ZS_LR // LANE),
                  full((LANE, GLA_QK)), full((1, GLA_QK)), full((1, GLA_DV)),
                  pl.BlockSpec((1, 1, GLA_HEADS, GLA_DV, GLA_DK), lambda b, n: (b, nc - 1 - n, 0, 0, 0)),
                  vv(0)],
        out_specs=[qk(0), qk(0), vv(0), vv(0), vv(0), lr(0),
                   full((GLA_HEADS, LANE, GLA_DK)), full((GLA_HEADS, 1, GLA_DK)), full((1, GLA_DV))],
        out_shape=[jax.ShapeDtypeStruct((t, GLA_QK), BF16), jax.ShapeDtypeStruct((t, GLA_QK), BF16),
                   jax.ShapeDtypeStruct((t, GLA_V), BF16), jax.ShapeDtypeStruct((t, GLA_V), BF16),
                   jax.ShapeDtypeStruct((t, GLA_V), BF16), jax.ShapeDtypeStruct((t, LANE), F32),
                   jax.ShapeDtypeStruct((GLA_HEADS, LANE, GLA_DK), F32),
                   jax.ShapeDtypeStruct((GLA_HEADS, 1, GLA_DK), F32),
                   jax.ShapeDtypeStruct((1, GLA_DV), F32)],
        scratch_shapes=[pltpu.VMEM((GLA_HEADS, GLA_DV, GLA_DK), F32)],
        args=(zb, zb, zb, zb, zb, zs, w2p, gla_b, gla_norm, states, dmixed))


DN_HB = DN_HEADS


def _dn_specs(nc, reverse):
    wide = DN_HB * DN_D

    def rows(b, n, h):
        return b * nc + ((nc - 1 - n) if reverse else n)

    def col(base):
        return pl.BlockSpec((CHUNK, wide), lambda b, n, h: (rows(b, n, h), base // wide + h))

    def fixed(c):
        return pl.BlockSpec((CHUNK, LANE), lambda b, n, h: (rows(b, n, h), c))

    head = pl.BlockSpec((1, LANE), lambda b, n, h: (0, 0))
    return rows, col, fixed, head


def _lanes(j):
    return slice(j * DN_D, (j + 1) * DN_D)


def _by_head(ref):
    return jnp.stack([ref[:, _lanes(j)] for j in range(DN_HB)], axis=0)


def _dn_fwd(act, zb, zs, alog_b, dtb_b, dn_norm, mix_gla, bsz, nc, rider=None):
    t = zb.shape[0]
    rows, col, fixed, head = _dn_specs(nc, False)

    def body(q_ref, k_ref, v_ref, za_ref, zb_ref, al_ref, dt_ref, gz_ref, gb_ref, dn_ref, mg_ref,
             o_ref, st_ref, ti_ref, state):
        s = jnp.where(pl.program_id(1) > 0, state[...], 0.0)
        st_ref[0, 0] = s
        res, s_new, t_inv = _dn_chunk(s, _by_head(q_ref), _by_head(k_ref), _by_head(v_ref), za_ref[...], zb_ref[...],
                                      al_ref[...], dt_ref[...], _by_head(gz_ref), _by_head(gb_ref), dn_ref[...])
        ti_ref[0, 0] = t_inv
        for j in range(DN_HB):
            o_ref[:, _lanes(j)] = (res[j] + mg_ref[:, _lanes(j)]).astype(o_ref.dtype)
        state[...] = s_new

    return _hosted_call(
        body, rider, name="dn_fwd", grid=(bsz, nc, DN_HEADS // DN_HB),
        in_specs=[col(0), col(DN_HEADS * DN_D), col(2 * DN_HEADS * DN_D),
                  fixed(ZS_DA // LANE), fixed(ZS_DB // LANE), head, head,
                  col(ZB_DZ), col(ZB_GB), pl.BlockSpec((1, DN_D), lambda b, n, h: (0, 0)), col(0)],
        out_specs=[col(0), pl.BlockSpec((1, 1, DN_HB, DN_D, DN_D), lambda b, n, h: (b, n, h, 0, 0)),
                   pl.BlockSpec((1, 1, DN_HB, CHUNK, CHUNK), lambda b, n, h: (b, n, h, 0, 0))],
        out_shape=[jax.ShapeDtypeStruct((t, D_MODEL), BF16),
                   jax.ShapeDtypeStruct((bsz, nc, DN_HEADS, DN_D, DN_D), F32),
                   jax.ShapeDtypeStruct((bsz, nc, DN_HEADS, CHUNK, CHUNK), F32)],
        scratch_shapes=[pltpu.VMEM((DN_HEADS, DN_D, DN_D), F32)],
        args=(act, act, act, zs, zs, alog_b, dtb_b, zb, zb, dn_norm, mix_gla))


def _dn_bwd(act, zb, zs, alog_b, dtb_b, dn_norm, states, t_invs, dmixed, bsz, nc, rider=None):
    t = zb.shape[0]
    rows, col, fixed, head = _dn_specs(nc, True)

    def body(q_ref, k_ref, v_ref, za_ref, zb_ref, al_ref, dt_ref, gz_ref, gb_ref, dn_ref, st_ref, ti_ref, dm_ref,
             dact_ref, dza_ref, dzb_ref, dgz_ref, dgb_ref, dal_ref, ddt_ref, ddn_ref, dstate):
        b, n = pl.program_id(0), pl.program_id(1)

        @pl.when((b == 0) & (n == 0))
        def _():
            dal_ref[...] = jnp.zeros_like(dal_ref)
            ddt_ref[...] = jnp.zeros_like(ddt_ref)
            ddn_ref[...] = jnp.zeros_like(ddn_ref)

        fn = functools.partial(_dn_chunk, t_saved=ti_ref[0, 0])
        _, vjp = jax.vjp(fn, st_ref[0, 0], _by_head(q_ref), _by_head(k_ref), _by_head(v_ref), za_ref[...],
                         zb_ref[...], al_ref[...], dt_ref[...], _by_head(gz_ref), _by_head(gb_ref), dn_ref[...])
        ds_in = jnp.where(n > 0, dstate[...], 0.0)
        ds, dq, dk, dv, dza, dzb, dal, ddt, dgz, dgb, ddn = vjp((_by_head(dm_ref).astype(F32), ds_in))
        dstate[...] = ds
        for j in range(DN_HB):
            for part, d in enumerate((dq, dk, dv)):
                dact_ref[:, pl.ds(part * DN_HEADS * DN_D + j * DN_D, DN_D)] = d[j]
            dgz_ref[:, _lanes(j)] = dgz[j].astype(dgz_ref.dtype)
            dgb_ref[:, _lanes(j)] = dgb[j].astype(dgb_ref.dtype)
        dal_ref[...] += dal
        ddt_ref[...] += ddt
        dza_ref[...] = dza
        dzb_ref[...] = dzb
        ddn_ref[...] += ddn

    full = lambda shape: pl.BlockSpec(shape, lambda b, n, h: (0,) * len(shape))
    return _hosted_call(
        body, rider, name="dn_bwd", grid=(bsz, nc, DN_HEADS // DN_HB),
        in_specs=[col(0), col(DN_HEADS * DN_D), col(2 * DN_HEADS * DN_D),
                  fixed(ZS_DA // LANE), fixed(ZS_DB // LANE), head, head,
                  col(ZB_DZ), col(ZB_GB), pl.BlockSpec((1, DN_D), lambda b, n, h: (0, 0)),
                  pl.BlockSpec((1, 1, DN_HB, DN_D, DN_D), lambda b, n, h: (b, nc - 1 - n, h, 0, 0)),
                  pl.BlockSpec((1, 1, DN_HB, CHUNK, CHUNK), lambda b, n, h: (b, nc - 1 - n, h, 0, 0)), col(0)],
        out_specs=[pl.BlockSpec((CHUNK, DN_QKV), lambda b, n, h: (rows(b, n, h), 0)), fixed(0), fixed(0), col(0), col(0),
                   full((1, LANE)), full((1, LANE)), full((1, DN_D))],
        out_shape=[jax.ShapeDtypeStruct((t, DN_QKV), F32),
                   jax.ShapeDtypeStruct((t, LANE), F32), jax.ShapeDtypeStruct((t, LANE), F32),
                   jax.ShapeDtypeStruct((t, D_MODEL), BF16), jax.ShapeDtypeStruct((t, D_MODEL), BF16),
                   jax.ShapeDtypeStruct((1, LANE), F32), jax.ShapeDtypeStruct((1, LANE), F32),
                   jax.ShapeDtypeStruct((1, DN_D), F32)],
        scratch_shapes=[pltpu.VMEM((DN_HEADS, DN_D, DN_D), F32)],
        args=(act, act, act, zs, zs, alog_b, dtb_b, zb, zb, dn_norm, states, t_invs, dmixed))


MM_VMEM_BUDGET = 40 * 1024 * 1024
MM_TILE_PREF = (1024, 1024, 2048)


def _divisor_tile(n, cap):
    if n <= cap:
        return n
    for c in range(cap - cap % LANE, 0, -LANE):
        if n % c == 0:
            return c
    return n


def _mm_tiles(m, n, kd, a_bytes, b_bytes, mn_bytes):
    tm, tn, tk = (_divisor_tile(d, c) for d, c in zip((m, n, kd), MM_TILE_PREF))

    def need(tm, tn, tk):
        acc = 0 if tk == kd else 4 * tm * tn
        return 2 * (tm * tk * a_bytes + tk * tn * b_bytes + tm * tn * mn_bytes) + acc + 4 * tm * tn

    while need(tm, tn, tk) > MM_VMEM_BUDGET:
        if tk > 512 and tk * max(tm * a_bytes, tn * b_bytes) >= tm * tn * mn_bytes:
            tk = _divisor_tile(kd, tk // 2)
        elif tn >= tm and tn > LANE:
            tn = _divisor_tile(n, tn // 2)
        else:
            tm = _divisor_tile(m, tm // 2)
    return tm, tn, tk


def _w_in_row_of(tile, tile_rows):
    skipped = jnp.where(tile >= ZB_GA // tile_rows, WI_GA - ZB_GA, jnp.where(tile >= ZB_DQKV // tile_rows,
                                                                             WI_DQKV - ZB_DQKV, 0))
    return pl.multiple_of(tile * tile_rows + skipped, 16)


def _mm(a, b, *, ta=False, tb=False, out_dtypes=(F32,), epilogue=None, extras=(), name, rider=None,
        b_is_w_in_t=False, out_is_w_in_t=False):
    m, kd = (a.shape[1], a.shape[0]) if ta else a.shape
    n = b.shape[0] if tb else b.shape[1]
    if b_is_w_in_t:
        n, kd = (ZB_W, kd) if tb else (n, ZB_W)
    mn_bytes = sum(e.dtype.itemsize for e in extras) + sum(jnp.dtype(dt).itemsize for dt in out_dtypes)
    tm, tn, tk = _mm_tiles(m, n, kd, a.dtype.itemsize, b.dtype.itemsize, mn_bytes)
    nk = kd // tk
    n_ex = len(extras)
    dims = (((0,) if ta else (1,), (1,) if tb else (0,)), ((), ()))

    def finish(acc, ex_refs, out_refs):
        outs = (acc,) if epilogue is None else epilogue(acc, *[r[...] for r in ex_refs])
        for r, o in zip(out_refs, outs):
            r[...] = o.astype(r.dtype)

    def partial_product(a_ref, b_ref):
        return lax.dot_general(a_ref[...].astype(BF16), b_ref[...].astype(BF16), dims, preferred_element_type=F32)

    def body_single(*refs):
        finish(partial_product(refs[0], refs[1]), refs[2:2 + n_ex], refs[2 + n_ex:])

    def body_acc(*refs):
        acc = refs[-1]
        k = pl.program_id(2)

        @pl.when(k == 0)
        def _():
            acc[...] = partial_product(refs[0], refs[1])

        @pl.when(k > 0)
        def _():
            acc[...] += partial_product(refs[0], refs[1])

        @pl.when(k == nk - 1)
        def _():
            finish(acc[...], refs[2:2 + n_ex], refs[2 + n_ex:-1])

    a_spec = pl.BlockSpec((tk, tm), lambda i, j, k: (k, i)) if ta else pl.BlockSpec((tm, tk), lambda i, j, k: (i, k))
    b_spec = pl.BlockSpec((tn, tk), lambda i, j, k: (j, k)) if tb else pl.BlockSpec((tk, tn), lambda i, j, k: (k, j))
    mn_spec = pl.BlockSpec((tm, tn), lambda i, j, k: (i, j))
    out_spec, out_rows = mn_spec, m
    if b_is_w_in_t and tb:
        b_spec = pl.BlockSpec((pl.Element(tn), pl.Element(tk)),
                              lambda i, j, k: (_w_in_row_of(j, tn), pl.multiple_of(k * tk, LANE)))
    elif b_is_w_in_t:
        b_spec = pl.BlockSpec((pl.Element(tk), pl.Element(tn)),
                              lambda i, j, k: (_w_in_row_of(k, tk), pl.multiple_of(j * tn, LANE)))
    if out_is_w_in_t:
        out_spec, out_rows = pl.BlockSpec((pl.Element(tm), pl.Element(tn)),
                                          lambda i, j, k: (_w_in_row_of(i, tm), pl.multiple_of(j * tn, LANE))), D_IN
    outs = _hosted_call(
        body_single if nk == 1 else body_acc, rider, name=name, grid=(m // tm, n // tn, nk),
        in_specs=[a_spec, b_spec] + [mn_spec] * n_ex,
        out_specs=[out_spec] * len(out_dtypes),
        out_shape=[jax.ShapeDtypeStruct((out_rows, n), dt) for dt in out_dtypes],
        scratch_shapes=[] if nk == 1 else [pltpu.VMEM((tm, tn), F32)],
        args=(a, b, *extras))
    return outs[0] if len(outs) == 1 else outs


ROW_BLOCK = 256


def _row_spec(width=D_MODEL):
    return pl.BlockSpec((ROW_BLOCK, width), lambda i: (i, 0))


def _vec_spec(width=D_MODEL):
    return pl.BlockSpec((1, width), lambda i: (0, 0))


def _rms_fwd(x, g, name):
    def body(x_ref, g_ref, h_ref):
        xf = x_ref[...]
        h_ref[...] = (xf * lax.rsqrt(jnp.mean(xf * xf, axis=-1, keepdims=True) + EPS) * g_ref[...]).astype(BF16)

    t = x.shape[0]
    return pl.pallas_call(
        body, name=name, grid=(t // ROW_BLOCK,), in_specs=[_row_spec(), _vec_spec()], out_specs=_row_spec(),
        out_shape=jax.ShapeDtypeStruct((t, D_MODEL), BF16), compiler_params=_cparams(("parallel",)),
    )(x, g)


def _rms_bwd_math(xf, g, dh):
    rstd = lax.rsqrt(jnp.mean(xf * xf, axis=-1, keepdims=True) + EPS)
    xhat = xf * rstd
    dxhat = dh * g
    dx = rstd * (dxhat - xhat * jnp.mean(dxhat * xhat, axis=-1, keepdims=True))
    dg = jnp.sum(dh * xhat, axis=0, keepdims=True)
    return dx, dg


def _rms_bwd(x, g, dh, dres, name, bf16_copy):
    def body(x_ref, g_ref, dh_ref, dres_ref, *out_refs):
        dx, dg = _rms_bwd_math(x_ref[...], g_ref[...], dh_ref[...].astype(F32))
        total = dres_ref[...] + dx
        for r in out_refs[:-1]:
            r[...] = total.astype(r.dtype)
        dg_ref = out_refs[-1]

        @pl.when(pl.program_id(0) == 0)
        def _():
            dg_ref[...] = jnp.zeros_like(dg_ref)

        dg_ref[...] += dg

    t = x.shape[0]
    dtypes = (F32, BF16) if bf16_copy else (F32,)
    return pl.pallas_call(
        body, name=name, grid=(t // ROW_BLOCK,),
        in_specs=[_row_spec(), _vec_spec(), _row_spec(), _row_spec()],
        out_specs=[_row_spec()] * len(dtypes) + [_vec_spec()],
        out_shape=[jax.ShapeDtypeStruct((t, D_MODEL), dt) for dt in dtypes] + [jax.ShapeDtypeStruct((1, D_MODEL), F32)],
        compiler_params=_cparams(("arbitrary",)),
    )(x, g, dh, dres)


def _loss_head(x3, g, target):
    def body(x_ref, g_ref, t_ref, dx_ref, dg_ref, loss_ref):
        xf, gg = x_ref[...], g_ref[...]
        rstd = lax.rsqrt(jnp.mean(xf * xf, axis=-1, keepdims=True) + EPS)
        err = xf * rstd * gg - t_ref[...]
        dx, dg = _rms_bwd_math(xf, gg, err * (1.0 / D_MODEL))
        dx_ref[...] = dx

        @pl.when(pl.program_id(0) == 0)
        def _():
            dg_ref[...] = jnp.zeros_like(dg_ref)
            loss_ref[...] = jnp.zeros_like(loss_ref)

        dg_ref[...] += dg
        part = jnp.sum(jnp.sum(err * err, axis=-1, keepdims=True), axis=0, keepdims=True) * (0.5 / D_MODEL)
        loss_ref[...] += jnp.broadcast_to(part, loss_ref.shape)

    t = x3.shape[0]
    return pl.pallas_call(
        body, name="loss_head", grid=(t // ROW_BLOCK,),
        in_specs=[_row_spec(), _vec_spec(), _row_spec()], out_specs=[_row_spec(), _vec_spec(), _vec_spec(LANE)],
        out_shape=[jax.ShapeDtypeStruct((t, D_MODEL), F32), jax.ShapeDtypeStruct((1, D_MODEL), F32),
                   jax.ShapeDtypeStruct((1, LANE), F32)],
        compiler_params=_cparams(("arbitrary",)),
    )(x3, g, target)


def _ple_bwd(dx3, gpre, pp):
    def body(dx_ref, gp_ref, pp_ref, dgp_ref, dpp_ref):
        dx, sg = dx_ref[...], _sigmoid(gp_ref[...])
        dpp_ref[...] = (dx * sg).astype(BF16)
        dgp_ref[...] = (dx * pp_ref[...] * sg * (1.0 - sg)).astype(BF16)

    t = dx3.shape[0]
    return pl.pallas_call(
        body, name="ple_bwd", grid=(t // ROW_BLOCK,), in_specs=[_row_spec()] * 3, out_specs=[_row_spec()] * 2,
        out_shape=[jax.ShapeDtypeStruct((t, D_MODEL), BF16)] * 2, compiler_params=_cparams(("parallel",)),
    )(dx3, gpre, pp)


CONV_COLS = 256


def _shift_down(x, s):
    if s == 0:
        return x
    return jnp.where(_iota2(x.shape, 0) >= s, pltpu.roll(x, s, 0), 0.0)


def _shift_up(x, s):
    if s == 0:
        return x
    rows = x.shape[0]
    return jnp.where(_iota2(x.shape, 0) < rows - s, pltpu.roll(x, rows - s, 0), 0.0)


def _conv_taps(xf):
    return [_shift_down(xf, DN_CONV - 1 - j) for j in range(DN_CONV)]


def _conv_pre(taps, w):
    return sum(tap * w[j:j + 1, :] for j, tap in enumerate(taps))


def _conv_fwd(zb, conv_w, bsz, seq):
    def body(x_ref, w_ref, y_ref):
        y_ref[...] = _silu(_conv_pre(_conv_taps(x_ref[...]), w_ref[...]))

    nblk = DN_QKV // CONV_COLS
    return pl.pallas_call(
        body, name="conv_fwd", grid=(bsz, nblk),
        in_specs=[pl.BlockSpec((seq, CONV_COLS), lambda b, j: (b, ZB_DQKV // CONV_COLS + j)),
                  pl.BlockSpec((DN_CONV, CONV_COLS), lambda b, j: (0, j))],
        out_specs=pl.BlockSpec((seq, CONV_COLS), lambda b, j: (b, j)),
        out_shape=jax.ShapeDtypeStruct((bsz * seq, DN_QKV), F32),
        compiler_params=_cparams(("parallel", "parallel")),
    )(zb, conv_w)


def _conv_bwd(zb, conv_w, dact, bsz, seq):
    def body(x_ref, w_ref, dy_ref, dx_ref, dw_ref):
        taps, w = _conv_taps(x_ref[...]), w_ref[...]
        c = _conv_pre(taps, w)
        sg = _sigmoid(c)
        dc = dy_ref[...].astype(F32) * sg * (1.0 + c * (1.0 - sg))
        dx = sum(_shift_up(dc, DN_CONV - 1 - j) * w[j:j + 1, :] for j in range(DN_CONV))
        dx_ref[...] = dx.astype(BF16)
        dw = jnp.concatenate([jnp.sum(dc * tap, axis=0, keepdims=True) for tap in taps], axis=0)

        @pl.when(pl.program_id(1) == 0)
        def _():
            dw_ref[...] = jnp.zeros_like(dw_ref)

        dw_ref[...] += dw

    nblk = DN_QKV // CONV_COLS
    return pl.pallas_call(
        body, name="conv_bwd", grid=(nblk, bsz),
        in_specs=[pl.BlockSpec((seq, CONV_COLS), lambda j, b: (b, ZB_DQKV // CONV_COLS + j)),
                  pl.BlockSpec((DN_CONV, CONV_COLS), lambda j, b: (0, j)),
                  pl.BlockSpec((seq, CONV_COLS), lambda j, b: (b, j))],
        out_specs=[pl.BlockSpec((seq, CONV_COLS), lambda j, b: (b, j)),
                   pl.BlockSpec((DN_CONV, CONV_COLS), lambda j, b: (0, j))],
        out_shape=[jax.ShapeDtypeStruct((bsz * seq, DN_QKV), BF16), jax.ShapeDtypeStruct((DN_CONV, DN_QKV), F32)],
        compiler_params=_cparams(("parallel", "arbitrary")),
    )(zb, conv_w, dact)


MESH_IDS = pl.DeviceIdType.MESH
ANY_SPEC = pl.BlockSpec(memory_space=pl.ANY)


COMM_SCRATCH = (pltpu.SemaphoreType.DMA((7,)), pltpu.SemaphoreType.DMA((7,)), pltpu.SemaphoreType.DMA)


def _gather_phases(x_ref, out_ref, send_sems, recv_sems, local_sem):
    mx, my, mc = lax.axis_index("x"), lax.axis_index("y"), lax.axis_index("c")
    me, sibling = (mx, my, mc), (mx, my, 1 - mc)
    x_nbr, y_nbr, diag = (1 - mx, my), (mx, 1 - my), (1 - mx, 1 - my)
    north = mc == 1
    relayed = (jnp.where(north, mx, 1 - mx), jnp.where(north, 1 - my, my))
    relay_to = (jnp.where(north, 1 - mx, mx), jnp.where(north, my, 1 - my))

    def slot(px, py, pc):
        return out_ref.at[4 * px + 2 * py + pc]

    def copy(k, block, to, src=None):
        return pltpu.make_async_remote_copy(
            src_ref=slot(*block) if src is None else src, dst_ref=slot(*block),
            send_sem=send_sems.at[k], recv_sem=recv_sems.at[k], device_id=to, device_id_type=MESH_IDS)

    def mine():
        return pltpu.make_async_copy(x_ref, slot(*me), local_sem)

    def own_sends():
        return [copy(0, me, sibling, src=x_ref), copy(1, me, (*x_nbr, mc), src=x_ref),
                copy(2, me, (*y_nbr, mc), src=x_ref)]

    def later_sends():
        return [copy(3, (*relayed, mc), (*relay_to, mc)), copy(4, (*x_nbr, mc), sibling),
                copy(5, (*y_nbr, mc), sibling), copy(6, (*diag, mc), sibling)]

    def start():
        mine().start()
        for cp in own_sends():
            cp.start()

    def forward():
        copy(1, (*x_nbr, mc), me).wait_recv()
        copy(2, (*y_nbr, mc), me).wait_recv()
        for cp in later_sends()[:3]:
            cp.start()

    def forward_late():
        copy(3, (*diag, mc), me).wait_recv()
        later_sends()[3].start()

    def finish():
        copy(0, sibling, me).wait_recv()
        for k, chip in ((4, x_nbr), (5, y_nbr), (6, diag)):
            copy(k, (*chip, 1 - mc), me).wait_recv()
        for cp in own_sends() + later_sends():
            cp.wait_send()
        mine().wait()

    return start, forward, forward_late, finish


def _scatter_phases(x_ref, out_ref, send_sems, recv_sems, local_sem, among_chips=False):
    mx, my, mc = lax.axis_index("x"), lax.axis_index("y"), lax.axis_index("c")
    n_peers = 4 if among_chips else N_DEV
    me = 2 * mx + my if among_chips else 4 * mx + 2 * my + mc

    def peer(k):
        if among_chips:
            return (mx ^ ((k >> 1) & 1), my ^ (k & 1), mc)
        return (mx ^ ((k >> 2) & 1), my ^ ((k >> 1) & 1), mc ^ (k & 1))

    def slot_of(k):
        px, py, pc = peer(k)
        return 2 * px + py if among_chips else 4 * px + 2 * py + pc

    def copy(k, src_slot, dst_slot):
        return pltpu.make_async_remote_copy(
            src_ref=x_ref.at[src_slot], dst_ref=out_ref.at[dst_slot],
            send_sem=send_sems.at[k - 1], recv_sem=recv_sems.at[k - 1],
            device_id=peer(k), device_id_type=MESH_IDS)

    def sends():
        return [copy(k, slot_of(k), me) for k in range(1, n_peers)]

    def mine():
        return pltpu.make_async_copy(x_ref.at[me], out_ref.at[me], local_sem)

    def start():
        mine().start()
        for cp in sends():
            cp.start()

    def finish():
        for k in range(1, n_peers):
            copy(k, me, slot_of(k)).wait_recv()
        for cp in sends():
            cp.wait_send()
        mine().wait()

    return start, _nothing, _nothing, finish


def _pair_phases(x_ref, out_ref, send_sems, recv_sems, local_sem):
    mx, my, mc = lax.axis_index("x"), lax.axis_index("y"), lax.axis_index("c")

    def copy(side):
        return pltpu.make_async_remote_copy(
            src_ref=x_ref.at[:, side], dst_ref=out_ref, send_sem=send_sems.at[0], recv_sem=recv_sems.at[0],
            device_id=(mx, my, 1 - mc), device_id_type=MESH_IDS)

    def start():
        copy(1 - mc).start()

    def finish():
        copy(mc).wait_recv()
        copy(1 - mc).wait_send()

    return start, _nothing, _nothing, finish


def _nothing():
    pass


class _Rider:
    def __init__(self, phases, x, out_shape):
        self.phases, self.x, self.out_shape = phases, x, out_shape


def _gather_rider(x):
    return _Rider(_gather_phases, x, jax.ShapeDtypeStruct((N_DEV,) + x.shape, x.dtype))


def _scatter_rider(x):
    return _Rider(_scatter_phases, x, jax.ShapeDtypeStruct(x.shape, x.dtype))


def _chip_scatter_rider(x):
    return _Rider(functools.partial(_scatter_phases, among_chips=True), x, jax.ShapeDtypeStruct(x.shape, x.dtype))


def _pair_rider(x):
    return _Rider(_pair_phases, x, jax.ShapeDtypeStruct((x.shape[0],) + x.shape[2:], x.dtype))


def _exchange(rider, name):
    def body(x_ref, out_ref, send_sems, recv_sems, local_sem):
        for phase in rider.phases(x_ref, out_ref, send_sems, recv_sems, local_sem):
            phase()

    return pl.pallas_call(body, name=name, out_shape=rider.out_shape, in_specs=[ANY_SPEC], out_specs=ANY_SPEC,
                          scratch_shapes=list(COMM_SCRATCH))(rider.x)


def _all_gather(x, name):
    return _exchange(_gather_rider(x), name)


def _hosted_call(body, rider, *, name, grid, in_specs, out_specs, out_shape, scratch_shapes, args):
    if rider is None:
        return pl.pallas_call(body, name=name, grid=grid, in_specs=in_specs, out_specs=out_specs, out_shape=out_shape,
                              scratch_shapes=scratch_shapes, compiler_params=_cparams(("arbitrary",) * len(grid)))(*args)
    n_in, n_out, n_scr = len(in_specs), len(out_specs), len(scratch_shapes)
    total = math.prod(grid)

    def riding(*refs):
        host_in, x_ref = refs[:n_in], refs[n_in]
        host_out, out_ref = refs[n_in + 1:n_in + 1 + n_out], refs[n_in + 1 + n_out]
        host_scr = refs[n_in + 2 + n_out:n_in + 2 + n_out + n_scr]
        start, forward, forward_late, finish = rider.phases(x_ref, out_ref, *refs[n_in + 2 + n_out + n_scr:])
        step = 0
        for axis, size in enumerate(grid):
            step = step * size + pl.program_id(axis)
        pl.when(step == 0)(start)
        pl.when(step == total // 2)(forward)
        pl.when(step == (3 * total) // 4)(forward_late)
        body(*host_in, *host_out, *host_scr)
        pl.when(step == total - 1)(finish)

    return pl.pallas_call(
        riding, name=name, grid=grid, in_specs=list(in_specs) + [ANY_SPEC], out_specs=list(out_specs) + [ANY_SPEC],
        out_shape=list(out_shape) + [rider.out_shape], scratch_shapes=list(scratch_shapes) + list(COMM_SCRATCH),
        compiler_params=_cparams(("arbitrary",) * len(grid)))(*args, rider.x)


def _adamw_math(w, g, m, v):
    m = ADAM_B1 * m + (1.0 - ADAM_B1) * g
    v = ADAM_B2 * v + (1.0 - ADAM_B2) * jnp.square(g)
    m_hat = m / (1.0 - ADAM_B1 ** ADAM_STEP)
    v_hat = v / (1.0 - ADAM_B2 ** ADAM_STEP)
    delta = -ADAM_LR * (m_hat / (jnp.sqrt(v_hat) + ADAM_EPS) + ADAM_WD * w)
    return delta, m, v


ADAM_ROWS = 128


def _elementwise_tile(rows, cols):
    if rows % ADAM_ROWS == 0:
        return ADAM_ROWS, cols
    return rows, (2 * LANE if cols % (2 * LANE) == 0 else cols)


def _add_blocks(a, b):
    g, rows, cols = a.shape
    tr, tc = _elementwise_tile(rows, cols)

    def body(a_ref, b_ref, o_ref):
        o_ref[...] = (a_ref[...].astype(F32) + b_ref[...].astype(F32)).astype(o_ref.dtype)

    blk = pl.BlockSpec((None, tr, tc), lambda k, i, j: (k, i, j))
    return pl.pallas_call(body, name="add_pair_blocks", grid=(g, rows // tr, cols // tc), in_specs=[blk, blk],
                          out_specs=blk, out_shape=jax.ShapeDtypeStruct(a.shape, a.dtype),
                          compiler_params=_cparams(("parallel", "parallel", "parallel")))(a, b)


def _adamw_reduce(w, m, v, parts, row0, name):
    rows, cols = w.shape
    n_parts = parts.shape[0]
    tr, tc = _elementwise_tile(rows, cols)
    r0 = row0 // tr

    def body(w_ref, m_ref, v_ref, *refs):
        part_refs, (g_ref, d_ref, nm_ref, nv_ref) = refs[:n_parts], refs[n_parts:]
        g = part_refs[0][...].astype(F32)
        for r in part_refs[1:]:
            g = g + r[...].astype(F32)
        delta, nm, nv = _adamw_math(w_ref[...], g, m_ref[...], v_ref[...])
        g_ref[...] = g
        d_ref[...] = delta
        nm_ref[...] = nm
        nv_ref[...] = nv

    blk = pl.BlockSpec((tr, tc), lambda i, j: (i, j))
    part_specs = [pl.BlockSpec((None, tr, tc), functools.partial(lambda i, j, k: (k, r0 + i, j), k=k))
                  for k in range(n_parts)]
    return pl.pallas_call(
        body, name=name, grid=(rows // tr, cols // tc), in_specs=[blk] * 3 + part_specs, out_specs=[blk] * 4,
        out_shape=[jax.ShapeDtypeStruct(w.shape, F32)] * 4, compiler_params=_cparams(("parallel", "parallel")),
    )(w, m, v, *([parts] * n_parts))


def _small_reduce(gathered):
    r = gathered.shape[1]

    def body(g_ref, o_ref):
        g = g_ref[0]
        for k in range(1, N_DEV):
            g = g + g_ref[k]
        o_ref[...] = g

    return pl.pallas_call(body, name="small_grad_reduce", out_shape=jax.ShapeDtypeStruct((r, LANE), F32))(gathered)


def _adamw_small(w, m, v, g):
    def body(w_ref, m_ref, v_ref, g_ref, d_ref, nm_ref, nv_ref):
        d_ref[...], nm_ref[...], nv_ref[...] = _adamw_math(w_ref[...], g_ref[...], m_ref[...], v_ref[...])

    return pl.pallas_call(body, name="adamw_small", out_shape=[jax.ShapeDtypeStruct(w.shape, F32)] * 3)(w, m, v, g)


def _pack_rows(arrays):
    rows = [jnp.pad(a.reshape(-1), (0, -a.size % LANE)).reshape(-1, LANE) for a in arrays]
    out = jnp.concatenate(rows, axis=0)
    return jnp.pad(out, ((0, -out.shape[0] % 8), (0, 0)))


def _unpack_rows(packed, shapes):
    out, r = [], 0
    for shp in shapes:
        size = math.prod(shp)
        nrows = -(-size // LANE)
        out.append(packed[r:r + nrows].reshape(-1)[:size].reshape(shp))
        r += nrows
    return out


def _add_residual(acc, res):
    return (res + acc,)


def _by_cols(g):
    return g.reshape(g.shape[0], N_DEV, -1).transpose(1, 0, 2)


def _from_cols(blocks):
    return blocks.transpose(1, 0, 2).reshape(blocks.shape[1], -1)


ROWS_OUT = D_MODEL // N_DEV

W_IN_SEGMENTS = ((0, WI_LR, "big", 0), (WI_LR, WI_DQKV, "gates", ZS_LR), (WI_DQKV, WI_DA, "big", ZB_DQKV),
                 (WI_DA, WI_DB, "gates", ZS_DA), (WI_DB, WI_GA, "gates", ZS_DB), (WI_GA, D_IN, "big", ZB_GA))


def _w_in_gate_rows(wt):
    parts = []
    for lo, hi, which, _ in W_IN_SEGMENTS:
        if which == "gates":
            parts += [wt[lo:hi], jnp.zeros((LANE - (hi - lo), wt.shape[1]), wt.dtype)]
    return jnp.concatenate(parts, axis=0)


def _fill_gate_rows(dw_in_t, dws_t):
    for lo, hi, which, first in W_IN_SEGMENTS:
        if which == "gates":
            dw_in_t = lax.dynamic_update_slice(dw_in_t, dws_t[first:first + hi - lo], (lo, 0))
    return dw_in_t


def _local_step(x, p, target, w, up_shard, rows_shard):
    bsz, seq, _ = x.shape
    t, nc = bsz * seq, seq // CHUNK
    x0, p2, tgt = x.reshape(t, D_MODEL), p.reshape(t, PLE_DIM), target.reshape(t, D_MODEL)

    h = _rms_fwd(x0, w["g_mix"], "rms_mix")
    zb, row_blocks = _mm(h, w["w_in_t"], tb=True, b_is_w_in_t=True, name="in_proj", rider=_gather_rider(rows_shard))
    zs, small = _mm(h, w["ws_t"], tb=True, name="in_proj_gates", rider=_gather_rider(w["small_shard"]))
    n_w2, n_cv = GLA_LOWRANK * GLA_QK // N_DEV // LANE, DN_CONV * DN_QKV // N_DEV // LANE
    n_pp = PLE_DIM * D_MODEL // N_DEV // LANE
    w = dict(w,
             w2p=jnp.pad(_from_cols(small[:, :n_w2].reshape(N_DEV, GLA_LOWRANK, -1)), ((0, LANE - GLA_LOWRANK), (0, 0))),
             conv=_from_cols(small[:, n_w2:n_w2 + n_cv].reshape(N_DEV, DN_CONV, -1)),
             w_pp=_from_cols(small[:, n_w2 + n_cv:n_w2 + n_cv + n_pp].reshape(N_DEV, PLE_DIM, -1)).astype(BF16))
    act = _conv_fwd(zb, w["conv"], bsz, seq)
    mix_gla, gla_states = _gla_fwd(zb, zs, w["w2p"], w["gla_b"], w["gla_norm"], bsz, nc)
    mixed, dn_states, dn_t_invs, up_blocks = _dn_fwd(act, zb, zs, w["alog_b"], w["dtb_b"], w["dn_norm"], mix_gla,
                                                     bsz, nc, rider=_gather_rider(up_shard))
    w_up = _from_cols(up_blocks)
    w_out = row_blocks[:, :ROWS_OUT].reshape(D_MODEL, D_MODEL)
    w_pg = row_blocks[:, ROWS_OUT:2 * ROWS_OUT].reshape(D_MODEL, D_MODEL)
    w_down = row_blocks[:, 2 * ROWS_OUT:].reshape(D_FF, D_MODEL)
    x1 = _mm(mixed, w_out, epilogue=_add_residual, extras=(x0,), name="out_proj")
    h2 = _rms_fwd(x1, w["g_mlp"], "rms_mlp")
    u, a = _mm(h2, w_up, out_dtypes=(BF16, BF16), name="mlp_up",
               epilogue=lambda acc: (acc, jnp.square(jnp.maximum(acc, 0.0))))
    x2 = _mm(a, w_down, epilogue=_add_residual, extras=(x1,), name="mlp_down")
    h3 = _rms_fwd(x2, w["g_ple"], "rms_ple")
    pp = _mm(p2, w["w_pp"], name="ple_proj")
    gpre, x3 = _mm(h3, w_pg, out_dtypes=(F32, F32), extras=(x2, pp), name="ple_gate",
                   epilogue=lambda acc, res, proj: (acc, res + _sigmoid(acc) * proj))
    dx3, dg_final, loss = _loss_head(x3, w["g_final"], tgt)

    dgpre, dpp = _ple_bwd(dx3, gpre, pp)
    dw_pp = _mm(p2, dpp, ta=True, out_dtypes=(BF16,), name="d_w_ple_proj")
    dw_pg, recv_pp = _mm(h3, dgpre, ta=True, out_dtypes=(BF16,), name="d_w_ple_gate",
                         rider=_scatter_rider(_by_cols(dw_pp)))
    dh3 = _mm(dgpre, w_pg, tb=True, name="d_h_ple")
    dx2, dx2_16, dg_ple = _rms_bwd(x2, w["g_ple"], dh3, dx3, "rms_ple_bwd", True)
    du = _mm(dx2_16, w_down, tb=True, out_dtypes=(BF16,), extras=(u,), name="d_mlp_hidden",
             epilogue=lambda acc, uu: (acc * (2.0 * jnp.maximum(uu.astype(F32), 0.0)),))
    dw_down = _mm(a, dx2_16, ta=True, out_dtypes=(BF16,), name="d_w_down")
    dw_up = _mm(h2, du, ta=True, out_dtypes=(BF16,), name="d_w_up")
    dh2 = _mm(du, w_up, tb=True, name="d_h_mlp")
    dx1, dx1_16, dg_mlp = _rms_bwd(x1, w["g_mlp"], dh2, dx2, "rms_mlp_bwd", True)
    dmixed = _mm(dx1_16, w_out, tb=True, out_dtypes=(BF16,), name="d_mixed")
    dw_out = _mm(mixed, dx1_16, ta=True, out_dtypes=(BF16,), name="d_w_out")

    d_rows = jnp.concatenate([dw_out.reshape(N_DEV, ROWS_OUT, D_MODEL), dw_pg.reshape(N_DEV, ROWS_OUT, D_MODEL),
                              dw_down.reshape(N_DEV, D_FF // N_DEV, D_MODEL)], axis=1)
    (dact, dza, dzb_, dgz, dgb, dal, ddt, ddn, recv_rows) = _dn_bwd(
        act, zb, zs, w["alog_b"], w["dtb_b"], w["dn_norm"], dn_states, dn_t_invs, dmixed, bsz, nc,
        rider=_scatter_rider(d_rows))
    (gdq, gdk, gdv, dgg, dga, dlr, dw2, dgla_b, dgla_norm, recv_up) = _gla_bwd(
        zb, zs, w["w2p"], w["gla_b"], w["gla_norm"], gla_states, dmixed, bsz, nc,
        rider=_scatter_rider(_by_cols(dw_up)))
    dqkv, dconv = _conv_bwd(zb, w["conv"], dact, bsz, seq)
    dzb = jnp.concatenate([gdq, gdk, gdv, dgg, dqkv, dgz, dga, dgb], axis=1)
    dzs = jnp.concatenate([dlr, dza, dzb_], axis=1)
    dw_in_t = _mm(dzb, h, ta=True, out_dtypes=(BF16,), out_is_w_in_t=True, name="d_w_in")
    dws_t = _mm(dzs, h, ta=True, out_dtypes=(BF16,), name="d_w_in_gates")
    by_chip = _fill_gate_rows(dw_in_t, dws_t).reshape(N_DEV // 2, 2, D_IN_SHARD, D_MODEL)
    dh_gates, from_sibling = _mm(dzs, w["ws_t"], name="d_h_mix_gates", rider=_pair_rider(by_chip))
    mine = lax.dynamic_index_in_dim(by_chip, lax.axis_index("c"), axis=1, keepdims=False)
    chip_sums = _add_blocks(mine, from_sibling)
    dh, recv_in = _mm(dzb, w["w_in_t"], b_is_w_in_t=True, epilogue=_add_residual, extras=(dh_gates,), name="d_h_mix",
                      rider=_chip_scatter_rider(chip_sums))
    gx, dg_mix = _rms_bwd(x0, w["g_mix"], dh, dx1, "rms_mix_bwd", False)

    dgla_w2 = dw2[:, :GLA_LOWRANK, :].transpose(1, 0, 2).reshape(GLA_LOWRANK, GLA_QK)
    return dict(
        loss=loss[0, 0], grad_x=gx.reshape(x.shape), recv_in=recv_in, recv_up=recv_up, recv_rows=recv_rows,
        recv_pp=recv_pp,
        g_mix=dg_mix, gla_b=dgla_b.reshape(1, GLA_QK), gla_norm=dgla_norm, dn_norm=ddn, g_mlp=dg_mlp, g_ple=dg_ple,
        g_final=dg_final, gla_w2=dgla_w2, dn_conv=dconv,
        dn_a_log=dal[:, :DN_HEADS], dn_dt_bias=ddt[:, :DN_HEADS])


def _first_weights(g_mix, w_in, gla_w2, gla_b, gla_norm, dn_conv, dn_a_log, dn_dt_bias, dn_norm, g_mlp, g_ple,
                   w_ple_proj, g_final):
    w_in_t = _all_gather(jnp.swapaxes(w_in[0], 0, 1).astype(BF16), "gather_w_in").reshape(D_IN, D_MODEL)
    return dict(
        w_in_t=w_in_t, ws_t=_w_in_gate_rows(w_in_t),
        small_shard=_pack_rows([gla_w2[0], dn_conv[0], w_ple_proj[0]]),
        alog_b=jnp.pad(dn_a_log, ((0, 0), (0, LANE - DN_HEADS))),
        dtb_b=jnp.pad(dn_dt_bias, ((0, 0), (0, LANE - DN_HEADS))),
        g_mix=g_mix, gla_b=gla_b, gla_norm=gla_norm, dn_norm=dn_norm, g_mlp=g_mlp, g_ple=g_ple,
        g_final=g_final.reshape(1, D_MODEL))


def kernel(x, p, g_mix, w_in, gla_w2, gla_b, gla_norm, dn_conv, dn_a_log, dn_dt_bias, dn_norm, w_out, g_mlp, w_up, w_down, g_ple, w_ple_gate, w_ple_proj, g_final, loss_target, m_g_mix, m_w_in, m_gla_w2, m_gla_b, m_gla_norm, m_dn_conv, m_dn_a_log, m_dn_dt_bias, m_dn_norm, m_w_out, m_g_mlp, m_w_up, m_w_down, m_g_ple, m_w_ple_gate, m_w_ple_proj, m_g_final, v_g_mix, v_w_in, v_gla_w2, v_gla_b, v_gla_norm, v_dn_conv, v_dn_a_log, v_dn_dt_bias, v_dn_norm, v_w_out, v_g_mlp, v_w_up, v_w_down, v_g_ple, v_w_ple_gate, v_w_ple_proj, v_g_final):
    names = ["g_mix", "w_in", "gla_w2", "gla_b", "gla_norm", "dn_conv", "dn_a_log", "dn_dt_bias", "dn_norm", "w_out",
             "g_mlp", "w_up", "w_down", "g_ple", "w_ple_gate", "w_ple_proj", "g_final"]
    ws = dict(zip(names, (g_mix, w_in, gla_w2, gla_b, gla_norm, dn_conv, dn_a_log, dn_dt_bias, dn_norm, w_out, g_mlp,
                          w_up, w_down, g_ple, w_ple_gate, w_ple_proj, g_final)))
    ms = dict(zip(names, (m_g_mix, m_w_in, m_gla_w2, m_gla_b, m_gla_norm, m_dn_conv, m_dn_a_log, m_dn_dt_bias,
                          m_dn_norm, m_w_out, m_g_mlp, m_w_up, m_w_down, m_g_ple, m_w_ple_gate, m_w_ple_proj,
                          m_g_final)))
    vs = dict(zip(names, (v_g_mix, v_w_in, v_gla_w2, v_gla_b, v_gla_norm, v_dn_conv, v_dn_a_log, v_dn_dt_bias,
                          v_dn_norm, v_w_out, v_g_mlp, v_w_up, v_w_down, v_g_ple, v_w_ple_gate, v_w_ple_proj,
                          v_g_final)))
    me = 4 * lax.axis_index("x") + 2 * lax.axis_index("y") + lax.axis_index("c")

    first = _first_weights(g_mix, w_in, gla_w2, gla_b, gla_norm, dn_conv, dn_a_log, dn_dt_bias, dn_norm, g_mlp,
                           g_ple, w_ple_proj, g_final)
    rows_shard = jnp.concatenate([w_out[0], w_ple_gate[0], w_down[0]], axis=0).astype(BF16)
    r = _local_step(x, p[0], loss_target, first, w_up[0].astype(BF16), rows_shard)
    loss = lax.psum(r["loss"], ("x", "y", "c"))

    grads, deltas, new_m, new_v = {}, {}, {}, {}

    def big(name, parts, row0=0):
        g, d, nm, nv = _adamw_reduce(ws[name][0], ms[name][0], vs[name][0], parts, row0, "adamw_" + name)
        grads[name], deltas[name], new_m[name], new_v[name] = g[None], d[None], nm[None], nv[None]

    t_outs = _adamw_reduce(*[jnp.swapaxes(d["w_in"][0], 0, 1) for d in (ws, ms, vs)], r["recv_in"], 0, "adamw_w_in")
    grads["w_in"], deltas["w_in"], new_m["w_in"], new_v["w_in"] = [jnp.swapaxes(o, 0, 1)[None] for o in t_outs]
    big("w_up", r["recv_up"])
    big("w_ple_proj", r["recv_pp"])
    big("w_out", r["recv_rows"], 0)
    big("w_ple_gate", r["recv_rows"], ROWS_OUT)
    big("w_down", r["recv_rows"], 2 * ROWS_OUT)

    vec_names = ["g_mix", "gla_b", "gla_norm", "dn_norm", "g_mlp", "g_ple", "g_final", "dn_a_log", "dn_dt_bias"]
    gathered_names = vec_names + ["gla_w2", "dn_conv"]
    total = _small_reduce(_all_gather(_pack_rows([r[n] for n in gathered_names]), "gather_small_grads"))
    sg = dict(zip(gathered_names, _unpack_rows(total, [r[n].shape for n in gathered_names])))
    sg["g_final"] = sg["g_final"].reshape(D_MODEL)
    sg["gla_w2"] = lax.dynamic_slice_in_dim(sg["gla_w2"], me * (GLA_QK // N_DEV), GLA_QK // N_DEV, axis=1)
    sg["dn_conv"] = lax.dynamic_slice_in_dim(sg["dn_conv"], me * (DN_QKV // N_DEV), DN_QKV // N_DEV, axis=1)
    small_names = gathered_names
    shapes = [ws[n].shape for n in small_names]
    d_s, m_s, v_s = _adamw_small(_pack_rows([ws[n] for n in small_names]), _pack_rows([ms[n] for n in small_names]),
                                 _pack_rows([vs[n] for n in small_names]), _pack_rows([sg[n] for n in small_names]))
    for n, d, nm, nv in zip(small_names, _unpack_rows(d_s, shapes), _unpack_rows(m_s, shapes), _unpack_rows(v_s, shapes)):
        grads[n], deltas[n], new_m[n], new_v[n] = sg[n].reshape(ws[n].shape), d, nm, nv

    return (loss, r["grad_x"], *[grads[n] for n in names], *[deltas[n] for n in names],
            *[new_m[n] for n in names], *[new_v[n] for n in names])
```
